```python
import jax
import jax.numpy as jnp
from jax import lax
import numpy as np

D_MODEL = 1024
BATCH = 2
SEQ = 16384
DEPTH = 1

GRID_W = 64
CTX_LEN = 256
EPS = 1e-6
M_HEADS = 4
M_DQK = 128
M_DV = 256
M_CHUNK = 64
ROPE_BASE = 10000.0
NA_HEADS = 8
NA_DH = 64
NA_KH = 8
NA_KW = 16
N_EXPERTS = 256
TOP_K = 8
N_GROUPS = 8
TOPK_GROUPS = 4
EXPERT_FF = 256
SHARED_FF = 256
ROUTE_SCALE = 2.5
MOE_BLOCK = 256
M_QK_W = M_HEADS * M_DQK
M_V_W = M_HEADS * M_DV
NA_W = NA_HEADS * NA_DH
IN_SIZES = (M_QK_W, M_QK_W, M_V_W, M_V_W, 4 * M_HEADS, NA_W, NA_W, NA_W, D_MODEL, D_MODEL)
IN_COLS = sum(IN_SIZES)
IN_SPLITS = tuple(sum(IN_SIZES[: i + 1]) for i in range(len(IN_SIZES) - 1))

kernel_name = "hybrid_mlstm_natten_moe_dit_block"


def rms(x):
    xf = x.astype(jnp.float32)
    return (xf * lax.rsqrt(jnp.mean(xf * xf, axis=-1, keepdims=True) + EPS)).astype(x.dtype)


def modulate(x, shift, scale):
    return rms(x) * (1.0 + scale) + shift


def heads(a, n_heads):
    b, t, _ = a.shape
    return a.reshape(b, t, n_heads, -1).transpose(0, 2, 1, 3)


def merge_heads(a):
    b, h, t, d = a.shape
    return a.transpose(0, 2, 1, 3).reshape(b, t, h * d)


def axial_rope(x, row_pos, col_pos):
    half = x.shape[-1] // 2
    nf = half // 2
    inv = jnp.power(ROPE_BASE, -jnp.arange(nf, dtype=jnp.float32) / nf)

    def rot(xa, pos):
        ang = pos[:, None] * inv[None, :]
        cos, sin = jnp.cos(ang), jnp.sin(ang)
        x1 = xa[..., :nf].astype(jnp.float32)
        x2 = xa[..., nf:].astype(jnp.float32)
        return jnp.concatenate([x1 * cos - x2 * sin, x1 * sin + x2 * cos], axis=-1)

    return jnp.concatenate([rot(x[..., :half], row_pos), rot(x[..., half:], col_pos)], axis=-1).astype(x.dtype)


def in_projection(h, w_in, b_mgate, na_qn_w, na_kn_w):
    mq, mk, mv, mo, mg, nq, nk, nv, gm, gn = jnp.split(h @ w_in, IN_SPLITS, axis=-1)
    b, t, _ = h.shape
    gates = (mg + b_mgate).astype(jnp.float32).reshape(b, t, 4, M_HEADS).transpose(2, 0, 3, 1)
    mq = heads(mq, M_HEADS) * (M_DQK ** -0.5)
    nq = rms(heads(nq, NA_HEADS)) * na_qn_w * (NA_DH ** -0.5)
    nk = rms(heads(nk, NA_HEADS)) * na_kn_w
    return mq, heads(mk, M_HEADS), heads(mv, M_HEADS), mo, gates, nq, nk, heads(nv, NA_HEADS), gm, gn


def mlstm_scan(q, k, v, ig, lf, state):
    b, h, t, _ = q.shape
    nc = t // M_CHUNK

    def chunks(a):
        a = a.reshape(a.shape[:2] + (nc, M_CHUNK) + a.shape[3:])
        return jnp.moveaxis(a, 2, 0)

    causal = jnp.tril(jnp.ones((M_CHUNK, M_CHUNK), dtype=bool))

    def step(carry, xs):
        c_prev, n_prev, m_prev = carry
        qc, kc, vc, ic, fc = xs
        bcum = jnp.cumsum(fc, axis=-1)
        g = bcum[..., -1]
        dlog = jnp.where(causal, bcum[..., :, None] - bcum[..., None, :] + ic[..., None, :], -jnp.inf)
        inter = bcum + m_prev[..., None]
        m_t = jnp.maximum(inter, jnp.max(dlog, axis=-1))
        s = jnp.einsum("bhtd,bhsd->bhts", qc, kc) * jnp.exp(dlog - m_t[..., None])
        w_inter = jnp.exp(inter - m_t)
        num = jnp.einsum("bhts,bhsv->bhtv", s, vc) + w_inter[..., None] * jnp.einsum("bhtd,bhdv->bhtv", qc, c_prev)
        den = jnp.sum(s, axis=-1) + w_inter * jnp.einsum("bhtd,bhd->bht", qc, n_prev)
        h_out = num / jnp.maximum(jnp.abs(den), jnp.exp(-m_t))[..., None]
        a = g[..., None] - bcum + ic
        m_new = jnp.maximum(g + m_prev, jnp.max(a, axis=-1))
        decay = jnp.exp(g + m_prev - m_new)
        wa = jnp.exp(a - m_new[..., None])
        c_new = decay[..., None, None] * c_prev + jnp.einsum("bhs,bhsd,bhsv->bhdv", wa, kc, vc)
        n_new = decay[..., None] * n_prev + jnp.einsum("bhs,bhsd->bhd", wa, kc)
        return (c_new, n_new, m_new), h_out

    state, hs = lax.scan(step, state, (chunks(q), chunks(k), chunks(v), chunks(ig), chunks(lf)))
    hs = jnp.moveaxis(hs, 0, 2).reshape(b, h, t, -1).astype(v.dtype)
    return hs, state


def mlstm_bidir(q, k, v, gates, init_fwd, init_bwd):
    ig_f, lf_f = gates[0], jax.nn.log_sigmoid(gates[1])
    ig_b, lf_b = gates[2], jax.nn.log_sigmoid(gates[3])
    h_f, st_f = mlstm_scan(q, k, v, ig_f, lf_f, init_fwd)
    rev = lambda a: jnp.flip(a, axis=2)
    h_b, st_b = mlstm_scan(rev(q), rev(k), rev(v), rev(ig_b), rev(lf_b), init_bwd)
    return h_f + rev(h_b), st_f, st_b


def mlstm_readout(h, o_pre, m_norm_w):
    return merge_heads(rms(h)) * m_norm_w * jax.nn.sigmoid(o_pre)


def na_latent(q, k, v, k_ctx, v_ctx, rpb):
    b, h, rows, w, dh = q.shape
    kh = min(NA_KH, rows)
    cols = jnp.arange(w)
    col_idx = jnp.clip(cols - NA_KW // 2, 0, w - NA_KW)[:, None] + jnp.arange(NA_KW)[None, :]
    rpb_cols = rpb[:, :, col_idx - cols[:, None] + (NA_KW - 1)]
    n_win = kh * NA_KW

    def one_row(r):
        rs = jnp.clip(r - kh // 2, 0, rows - kh)
        q_r = lax.dynamic_index_in_dim(q, r, axis=2, keepdims=False)
        k_r = lax.dynamic_slice_in_dim(k, rs, kh, axis=2)[:, :, :, col_idx]
        v_r = lax.dynamic_slice_in_dim(v, rs, kh, axis=2)[:, :, :, col_idx]
        bias = rpb_cols[:, rs + jnp.arange(kh) - r + (NA_KH - 1)].transpose(0, 2, 1, 3)
        s_win = jnp.einsum("bhcd,bhicjd->bhcij", q_r, k_r) + bias
        s_ctx = jnp.einsum("bhcd,bhnd->bhcn", q_r, k_ctx)
        logits = jnp.concatenate([s_win.reshape(b, h, w, n_win), s_ctx], axis=-1).astype(jnp.float32)
        p = jax.nn.softmax(logits, axis=-1).astype(v.dtype)
        p_win = p[..., :n_win].reshape(b, h, w, kh, NA_KW)
        return (jnp.einsum("bhcij,bhicjd->bhcd", p_win, v_r)
                + jnp.einsum("bhcn,bhnd->bhcd", p[..., n_win:], v_ctx))

    out = lax.map(one_row, jnp.arange(rows))
    return out.transpose(1, 2, 0, 3, 4).reshape(b, h, rows * w, dh)


def na_context(q, k, v):
    p = jax.nn.softmax(jnp.einsum("bhqd,bhkd->bhqk", q, k).astype(jnp.float32), axis=-1).astype(v.dtype)
    return jnp.einsum("bhqk,bhkd->bhqd", p, v)


def token_mixer(h, hc, row_pos, col_pos, w_in, b_mgate, m_norm_w, na_qn_w, na_kn_w, na_rpb,
                w_br_m, w_br_na, w_out, with_ctx_out):
    b, t, _ = h.shape
    rows = t // GRID_W
    mq, mk, mv, mo, gates, nq, nk, nv, gm, gn = in_projection(h, w_in, b_mgate, na_qn_w, na_kn_w)
    cmq, cmk, cmv, cmo, cgates, cnq, cnk, cnv, cgm, cgn = in_projection(hc, w_in, b_mgate, na_qn_w, na_kn_w)

    zero = (jnp.zeros((b, M_HEADS, M_DQK, M_DV), jnp.float32),
            jnp.zeros((b, M_HEADS, M_DQK), jnp.float32),
            jnp.zeros((b, M_HEADS), jnp.float32))
    hm_c, st_f, st_b = mlstm_bidir(cmq, cmk, cmv, cgates, zero, zero)
    mq = axial_rope(mq, row_pos, col_pos)
    mk = axial_rope(mk, row_pos, col_pos)
    hm, _, _ = mlstm_bidir(mq, mk, mv, gates, st_f, st_b)
    y_m = mlstm_readout(hm, mo, m_norm_w)

    grid = lambda a: a.reshape(b, NA_HEADS, rows, GRID_W, NA_DH)
    y_na = merge_heads(na_latent(grid(nq), grid(nk), grid(nv), cnk, cnv, na_rpb))

    y = (jax.nn.sigmoid(gm) * (y_m @ w_br_m) + jax.nn.sigmoid(gn) * (y_na @ w_br_na)) @ w_out
    if not with_ctx_out:
        return y, None
    yc_m = mlstm_readout(hm_c, cmo, m_norm_w)
    yc_na = merge_heads(na_context(cnq, cnk, cnv))
    yc = (jax.nn.sigmoid(cgm) * (yc_m @ w_br_m) + jax.nn.sigmoid(cgn) * (yc_na @ w_br_na)) @ w_out
    return y, yc


def swiglu(x, wg, wu, wd):
    return (jax.nn.silu(x @ wg) * (x @ wu)) @ wd


def routed_experts(tok, top_e, top_w, w_gate, w_up, w_down):
    n, d = tok.shape
    a = n * TOP_K
    e_flat = top_e.reshape(-1)
    tok_id = jnp.repeat(jnp.arange(n, dtype=jnp.int32), TOP_K)
    w_flat = top_w.reshape(-1)
    order = jnp.argsort(e_flat)
    e_s, tok_s, w_s = e_flat[order], tok_id[order], w_flat[order]
    counts = jnp.bincount(e_flat, length=N_EXPERTS)
    start = jnp.cumsum(counts) - counts
    padded = (counts + MOE_BLOCK - 1) // MOE_BLOCK * MOE_BLOCK
    pend = jnp.cumsum(padded)
    pstart = pend - padded
    pos = pstart[e_s] + jnp.arange(a) - start[e_s]
    nb = -(-a // MOE_BLOCK) + N_EXPERTS
    p_rows = nb * MOE_BLOCK
    slot_tok = jnp.full((p_rows,), n, jnp.int32).at[pos].set(tok_s)
    slot_w = jnp.zeros((p_rows,), tok.dtype).at[pos].set(w_s)
    block_e = jnp.clip(jnp.searchsorted(pend, jnp.arange(nb) * MOE_BLOCK, side="right"), 0, N_EXPERTS - 1)
    tok_pad = jnp.concatenate([tok, jnp.zeros((1, d), tok.dtype)], axis=0)

    def run_block(args):
        ids, e = args
        return swiglu(tok_pad[ids], w_gate[e], w_up[e], w_down[e])

    yb = lax.map(run_block, (slot_tok.reshape(nb, MOE_BLOCK), block_e))
    out = jnp.zeros((n + 1, d), tok.dtype).at[slot_tok].add(yb.reshape(p_rows, d) * slot_w[:, None])
    return out[:n]


def moe_ffn(h, w_router, router_bias, w_exp_gate, w_exp_up, w_exp_down, w_sh_gate, w_sh_up, w_sh_down):
    b, t, d = h.shape
    tok = h.reshape(b * t, d)
    n = tok.shape[0]
    scores = jax.nn.sigmoid((tok @ w_router).astype(jnp.float32))
    sel = scores + router_bias.astype(jnp.float32)
    grp_score = jnp.sum(lax.top_k(sel.reshape(n, N_GROUPS, -1), 2)[0], axis=-1)
    _, top_g = lax.top_k(grp_score, TOPK_GROUPS)
    gmask = jnp.any(top_g[..., None] == jnp.arange(N_GROUPS), axis=1)
    sel = jnp.where(jnp.repeat(gmask, N_EXPERTS // N_GROUPS, axis=1), sel, -jnp.inf)
    _, top_e = lax.top_k(sel, TOP_K)
    top_w = jnp.take_along_axis(scores, top_e, axis=-1)
    top_w = top_w / jnp.sum(top_w, axis=-1, keepdims=True) * ROUTE_SCALE
    routed = routed_experts(tok, top_e, top_w.astype(tok.dtype), w_exp_gate, w_exp_up, w_exp_down)
    shared = swiglu(tok, w_sh_gate, w_sh_up, w_sh_down)
    return (routed + shared).reshape(b, t, d)


def setup_inputs(seed: int = 0) -> dict:
    key = jax.random.key(seed)
    ks = jax.random.split(key, 24)
    L, D = DEPTH, D_MODEL

    def nrm(k, shape, s):
        return jax.random.normal(k, shape, jnp.float32) * s

    gate_base = jnp.repeat(jnp.array([0.0, 3.0, 0.0, 3.0], jnp.float32), M_HEADS)[None, :]
    return {
        "x": nrm(ks[0], (BATCH, SEQ, D), 1.0),
        "c": nrm(ks[1], (BATCH, D), 1.0),
        "ctx": nrm(ks[2], (BATCH, CTX_LEN, D), 1.0),
        "c_ctx": nrm(ks[3], (D,), 1.0),
        "w_ada": nrm(ks[4], (L, D, 6 * D), 0.5 * D ** -0.5),
        "b_ada": nrm(ks[5], (L, 6 * D), 0.02),
        "w_in": nrm(ks[6], (L, D, IN_COLS), D ** -0.5),
        "b_mgate": gate_base + nrm(ks[7], (L, 4 * M_HEADS), 0.5),
        "m_norm_w": 1.0 + nrm(ks[8], (L, M_V_W), 0.02),
        "na_qn_w": 1.0 + nrm(ks[9], (L, NA_DH), 0.02),
        "na_kn_w": 1.0 + nrm(ks[10], (L, NA_DH), 0.02),
        "na_rpb": nrm(ks[11], (L, NA_HEADS, 2 * NA_KH - 1, 2 * NA_KW - 1), 0.1),
        "w_br_m": nrm(ks[12], (L, M_V_W, D), M_V_W ** -0.5),
        "w_br_na": nrm(ks[13], (L, NA_W, D), NA_W ** -0.5),
        "w_out": nrm(ks[14], (L, D, D), D ** -0.5),
        "w_router": nrm(ks[15], (L, D, N_EXPERTS), D ** -0.5),
        "router_bias": nrm(ks[16], (L, N_EXPERTS), 0.01),
        "w_exp_gate": nrm(ks[17], (L, N_EXPERTS, D, EXPERT_FF), D ** -0.5),
        "w_exp_up": nrm(ks[18], (L, N_EXPERTS, D, EXPERT_FF), D ** -0.5),
        "w_exp_down": nrm(ks[19], (L, N_EXPERTS, EXPERT_FF, D), EXPERT_FF ** -0.5),
        "w_sh_gate": nrm(ks[20], (L, D, SHARED_FF), D ** -0.5),
        "w_sh_up": nrm(ks[21], (L, D, SHARED_FF), D ** -0.5),
        "w_sh_down": nrm(ks[22], (L, SHARED_FF, D), SHARED_FF ** -0.5),
    }


def reference(x, c, ctx, c_ctx, w_ada, b_ada, w_in, b_mgate, m_norm_w, na_qn_w, na_kn_w, na_rpb,
              w_br_m, w_br_na, w_out, w_router, router_bias, w_exp_gate, w_exp_up, w_exp_down,
              w_sh_gate, w_sh_up, w_sh_down):
    t = jnp.arange(x.shape[1])
    row_pos = (t // GRID_W).astype(jnp.float32)
    col_pos = (t % GRID_W).astype(jnp.float32)
    for l in range(DEPTH):
        last = l == DEPTH - 1
        mod = jax.nn.silu(c) @ w_ada[l] + b_ada[l]
        mod_ctx = jax.nn.silu(c_ctx) @ w_ada[l] + b_ada[l]
        sh1, sc1, g1, sh2, sc2, g2 = jnp.split(mod[:, None, :], 6, axis=-1)
        csh1, csc1, cg1, csh2, csc2, cg2 = jnp.split(mod_ctx, 6, axis=-1)
        y, yc = token_mixer(modulate(x, sh1, sc1), modulate(ctx, csh1, csc1), row_pos, col_pos,
                            w_in[l], b_mgate[l], m_norm_w[l], na_qn_w[l], na_kn_w[l], na_rpb[l],
                            w_br_m[l], w_br_na[l], w_out[l], not last)
        x = x + g1 * y
        x = x + g2 * moe_ffn(modulate(x, sh2, sc2), w_router[l], router_bias[l], w_exp_gate[l], w_exp_up[l],
                             w_exp_down[l], w_sh_gate[l], w_sh_up[l], w_sh_down[l])
        if not last:
            ctx = ctx + cg1 * yc
            ctx = ctx + cg2 * moe_ffn(modulate(ctx, csh2, csc2), w_router[l], router_bias[l], w_exp_gate[l],
                                      w_exp_up[l], w_exp_down[l], w_sh_gate[l], w_sh_up[l], w_sh_down[l])
    return x
```

```python
import functools

import numpy as np
import jax
import jax.numpy as jnp
from jax import lax
from jax.experimental import pallas as pl
from jax.experimental.pallas import tpu as pltpu

F32 = jnp.float32
BF16 = jnp.bfloat16

EPS = 1e-6
GRID_W = 64
M_HEADS, M_DQK, M_DV = 4, 128, 256
ROPE_BASE = 10000.0
NA_HEADS, NA_DH, NA_KH, NA_KW = 8, 64, 8, 16
N_EXPERTS, TOP_K, N_GROUPS, TOPK_GROUPS = 256, 8, 8, 4
ROUTE_SCALE = 2.5

LANES = 128
VMEM_LIMIT = 56 * 1024 * 1024
NEG = -1e30

MLSTM_CHUNK = 256
NA_ROWS = 4
NA_KEY_ROWS = NA_ROWS + NA_KH - 1
MOE_BLOCK = 256

_W_SEGS = (("mq", 512), ("mk", 512), ("mv", 1024), ("mo", 1024), ("gi", 128), ("gf", 128),
           ("nq", 512), ("nk", 512), ("nv", 512), ("gm", 1024), ("gn", 1024))
_W_OFF = {}
_o = 0
for _n, _w in _W_SEGS:
    _W_OFF[_n] = (_o, _w)
    _o += _w
W_COLS = _o


def _dot(a, b):
    return jnp.dot(a, b, preferred_element_type=F32)


def _dot_nt(a, b):
    return lax.dot_general(a, b, (((1,), (1,)), ((), ())), preferred_element_type=F32)


def _sigmoid(x):
    return 1.0 / (1.0 + jnp.exp(-x))


def _params(*sem):
    return pltpu.CompilerParams(dimension_semantics=sem, vmem_limit_bytes=VMEM_LIMIT)


def _resident(shape):
    nd = len(shape)
    return pl.BlockSpec(shape, lambda *_: (0,) * nd, pipeline_mode=pl.Buffered(1))


def _ada_kernel(c_ref, w_ref, b_ref, o_ref):
    c = c_ref[...]
    s = c * _sigmoid(c)
    o_ref[...] = _dot(s.astype(BF16), w_ref[...].astype(BF16)) + b_ref[...]


def _ada(cc, w_ada, b_ada):
    d = cc.shape[1]
    n = w_ada.shape[1]
    return pl.pallas_call(
        _ada_kernel,
        grid=(n // d,),
        in_specs=[pl.BlockSpec((8, d), lambda j: (0, 0)),
                  pl.BlockSpec((d, d), lambda j: (0, j)),
                  pl.BlockSpec((1, d), lambda j: (0, j))],
        out_specs=pl.BlockSpec((8, d), lambda j: (0, j)),
        out_shape=jax.ShapeDtypeStruct((8, n), F32),
        compiler_params=_params("arbitrary"),
        name="ada",
    )(cc, w_ada, b_ada.reshape(1, n))


def _rope_rotate(t, cos, sin):
    lane = lax.broadcasted_iota(jnp.int32, t.shape, 1)
    partner = jnp.where((lane & 32) == 0, pltpu.roll(t, 96, 1), pltpu.roll(t, 32, 1))
    return t * cos + partner * sin


def _inproj_kernel(*refs, rope):
    if rope:
        (x_ref, mod_ref, w_ref, bg_ref, qnw_ref, knw_ref, seg_ref, segt_ref, cos_ref, sin_ref,
         mq_ref, mk_ref, mv_ref, mo_ref, gi_ref, gf_ref, nq_ref, nk_ref, nv_ref, gm_ref, gn_ref) = refs
    else:
        (x_ref, mod_ref, w_ref, bg_ref, qnw_ref, knw_ref, seg_ref, segt_ref,
         mq_ref, mk_ref, mv_ref, mo_ref, gi_ref, gf_ref, nq_ref, nk_ref, nv_ref, gm_ref, gn_ref) = refs
    x = x_ref[0]
    xn = x * lax.rsqrt(jnp.mean(x * x, axis=-1, keepdims=True) + EPS)
    h = xn * (1.0 + mod_ref[0, 1:2, :]) + mod_ref[0, 0:1, :]
    hb = h.astype(BF16)

    def proj(name):
        off, width = _W_OFF[name]
        return _dot(hb, w_ref[:, off:off + width])

    def head_rms(t, w_row, scale):
        ss = _dot((t * t).astype(BF16), seg_ref[...])
        r = lax.rsqrt(ss * (1.0 / NA_DH) + EPS)
        r_hi = r.astype(BF16)
        r_lo = (r - r_hi.astype(F32)).astype(BF16)
        rb = _dot(r_hi, segt_ref[...]) + _dot(r_lo, segt_ref[...])
        return t * rb * w_row * scale

    mq = proj("mq") * (M_DQK ** -0.5)
    mk = proj("mk")
    if rope:
        cos, sin = cos_ref[...], sin_ref[...]
        mq = jnp.concatenate([_rope_rotate(mq[:, i * LANES:(i + 1) * LANES], cos, sin)
                              for i in range(M_HEADS)], axis=1)
        mk = jnp.concatenate([_rope_rotate(mk[:, i * LANES:(i + 1) * LANES], cos, sin)
                              for i in range(M_HEADS)], axis=1)
    mq_ref[0] = mq.astype(BF16)
    mk_ref[0] = mk.astype(BF16)
    mv_ref[0] = proj("mv").astype(BF16)
    mo_ref[0] = _sigmoid(proj("mo")).astype(BF16)
    gi_ref[0] = proj("gi") + bg_ref[0:1, :]
    gf_ref[0] = proj("gf") + bg_ref[1:2, :]
    nq_ref[0] = head_rms(proj("nq"), qnw_ref[...], NA_DH ** -0.5).astype(BF16)
    nk_ref[0] = head_rms(proj("nk"), knw_ref[...], 1.0).astype(BF16)
    nv_ref[0] = proj("nv").astype(BF16)
    gm_ref[0] = _sigmoid(proj("gm")).astype(BF16)
    gn_ref[0] = _sigmoid(proj("gn")).astype(BF16)


def _inproj(x, mod, w_all, bg, qnw, knw, seg, segt, rope_tabs, tm):
    b, t, d = x.shape
    rope = rope_tabs is not None
    tok = lambda w: pl.BlockSpec((1, tm, w), lambda bi, i: (bi, i, 0))
    in_specs = [tok(d),
                pl.BlockSpec((1, 8, d), lambda bi, i: (bi, 0, 0)),
                _resident(w_all.shape), _resident(bg.shape), _resident(qnw.shape),
                _resident(knw.shape), _resident(seg.shape), _resident(segt.shape)]
    args = [x, mod, w_all, bg, qnw, knw, seg, segt]
    if rope:
        in_specs += [pl.BlockSpec((tm, LANES), lambda bi, i: (i, 0))] * 2
        args += list(rope_tabs)
    widths = [("mq", BF16), ("mk", BF16), ("mv", BF16), ("mo", BF16), ("gi", F32), ("gf", F32),
              ("nq", BF16), ("nk", BF16), ("nv", BF16), ("gm", BF16), ("gn", BF16)]
    out_specs = [tok(_W_OFF[n][1]) for n, _ in widths]
    out_shape = [jax.ShapeDtypeStruct((b, t, _W_OFF[n][1]), dt) for n, dt in widths]
    return pl.pallas_call(
        functools.partial(_inproj_kernel, rope=rope),
        grid=(b, t // tm),
        in_specs=in_specs, out_specs=out_specs, out_shape=out_shape,
        compiler_params=_params("parallel", "parallel"),
        name="inproj_rope" if rope else "inproj_ctx",
    )(*args)


def _log_sigmoid(x):
    return jnp.minimum(x, 0.0) - jnp.log(1.0 + jnp.exp(-jnp.abs(x)))


def _mlstm_kernel(qf_ref, kf_ref, vf_ref, gif_ref, gff_ref,
                  qb_ref, kb_ref, vb_ref, gib_ref, gfb_ref,
                  c0_ref, n0_ref, m0_ref,
                  hf_ref, hb_ref, cn_ref, nn_ref, mn_ref,
                  c_scr, n_scr, m_scr):
    step = pl.program_id(1)
    L = qf_ref.shape[1]

    @pl.when(step == 0)
    def _():
        c_scr[...] = c0_ref[0]
        n_scr[...] = n0_ref[0]
        m_scr[...] = m0_ref[0]

    t_idx = lax.broadcasted_iota(jnp.int32, (L, L), 0)
    s_idx = lax.broadcasted_iota(jnp.int32, (L, L), 1)
    dirs = ((qf_ref, kf_ref, vf_ref, gif_ref, gff_ref, hf_ref, s_idx <= t_idx),
            (qb_ref, kb_ref, vb_ref, gib_ref, gfb_ref, hb_ref, s_idx >= t_idx))
    for d, (q_ref, k_ref, v_ref, gi_ref, gf_ref, h_ref, mask) in enumerate(dirs):
        gi = gi_ref[0]
        ls = _log_sigmoid(gf_ref[0])
        tri = mask.astype(F32)
        bcum = jnp.dot(tri, ls, precision=lax.Precision.HIGHEST, preferred_element_type=F32)
        u_t = jnp.transpose(gi - bcum)
        g_row = jnp.sum(ls, axis=0, keepdims=True)
        for hd in range(M_HEADS):
            j = d * M_HEADS + hd
            q = q_ref[0, :, hd * M_DQK:(hd + 1) * M_DQK]
            k = k_ref[0, :, hd * M_DQK:(hd + 1) * M_DQK]
            v = v_ref[0, :, hd * M_DV:(hd + 1) * M_DV]
            b_col = bcum[:, j:j + 1]
            i_col = gi[:, j:j + 1]
            u_row = u_t[j:j + 1, :]
            g = g_row[:, j:j + 1]
            m_prev = m_scr[j:j + 1, 0:1]
            n_prev = n_scr[j:j + 1, :]
            c_prev = c_scr[j]

            m_loc = jnp.max(jnp.where(mask, u_row, NEG), axis=1, keepdims=True)
            m_row = jnp.maximum(m_loc, m_prev)
            dmat = jnp.exp(jnp.where(mask, u_row - m_row, NEG))
            s = _dot_nt(q, k) * dmat
            w_inter = jnp.exp(m_prev - m_row)
            num = _dot(s.astype(BF16), v) + w_inter * _dot(q, c_prev.astype(BF16))
            den = (jnp.sum(s, axis=1, keepdims=True)
                   + w_inter * jnp.sum(q.astype(F32) * n_prev, axis=1, keepdims=True))
            bound = jnp.exp(-(b_col + m_row))
            h_ref[0, :, hd * M_DV:(hd + 1) * M_DV] = (
                num / jnp.maximum(jnp.abs(den), bound)).astype(h_ref.dtype)

            a_col = g - b_col + i_col
            m_new = jnp.maximum(g + m_prev, jnp.max(a_col, axis=0, keepdims=True))
            decay = jnp.exp(g + m_prev - m_new)
            kw = k.astype(F32) * jnp.exp(a_col - m_new)
            c_scr[j] = decay * c_prev + _dot(jnp.transpose(kw).astype(BF16), v)
            n_scr[j:j + 1, :] = decay * n_prev + jnp.sum(kw, axis=0, keepdims=True)
            m_scr[j:j + 1, :] = jnp.broadcast_to(m_new, (1, LANES))

    @pl.when(step == pl.num_programs(1) - 1)
    def _():
        cn_ref[0] = c_scr[...]
        nn_ref[0] = n_scr[...]
        mn_ref[0] = m_scr[...]


def _mlstm(q, k, v, gi, gf, c0, n0, m0):
    b, t, _ = q.shape
    L = min(MLSTM_CHUNK, t)
    nc = t // L
    fwd = lambda w: pl.BlockSpec((1, L, w), lambda bi, i: (bi, i, 0))
    bwd = lambda w: pl.BlockSpec((1, L, w), lambda bi, i: (bi, nc - 1 - i, 0))
    st_c = pl.BlockSpec((1, 8, M_DQK, M_DV), lambda bi, i: (bi, 0, 0, 0))
    st_v = pl.BlockSpec((1, 8, LANES), lambda bi, i: (bi, 0, 0))
    qk_w, v_w = M_HEADS * M_DQK, M_HEADS * M_DV
    return pl.pallas_call(
        _mlstm_kernel,
        grid=(b, nc),
        in_specs=[fwd(qk_w), fwd(qk_w), fwd(v_w), fwd(LANES), fwd(LANES),
                  bwd(qk_w), bwd(qk_w), bwd(v_w), bwd(LANES), bwd(LANES),
                  st_c, st_v, st_v],
        out_specs=[fwd(v_w), bwd(v_w), st_c, st_v, st_v],
        out_shape=[jax.ShapeDtypeStruct((b, t, v_w), BF16),
                   jax.ShapeDtypeStruct((b, t, v_w), BF16),
                   jax.ShapeDtypeStruct(c0.shape, F32),
                   jax.ShapeDtypeStruct(n0.shape, F32),
                   jax.ShapeDtypeStruct(m0.shape, F32)],
        scratch_shapes=[pltpu.VMEM((8, M_DQK, M_DV), F32),
                        pltpu.VMEM((8, LANES), F32),
                        pltpu.VMEM((8, LANES), F32)],
        compiler_params=_params("parallel", "arbitrary"),
        name="mlstm",
    )(q, k, v, gi, gf, q, k, v, gi, gf, c0, n0, m0)


def _na_kernel(q_ref, k_ref, v_ref, kc_ref, vc_ref, bias_ref, o_ref, *, rows):
    r0 = pl.program_id(2) * NA_ROWS
    ks = jnp.clip(r0 - NA_KH // 2, 0, rows - NA_KEY_ROWS)
    kstart = pl.multiple_of(ks * GRID_W, GRID_W)
    nkeys = NA_KEY_ROWS * GRID_W
    kblk = k_ref[0, pl.ds(kstart, nkeys), :]
    vblk = v_ref[0, pl.ds(kstart, nkeys), :]
    kc = kc_ref[0]
    vc = vc_ref[0]
    q = q_ref[0]
    lane = lax.broadcasted_iota(jnp.int32, q.shape, 1)
    outs = []
    for hh in range(2):
        in_head = (lane < NA_DH) if hh == 0 else (lane >= NA_DH)
        qm = jnp.where(in_head, q, jnp.zeros_like(q))
        sw = _dot_nt(qm, kblk) + bias_ref[hh, 0]
        sc = _dot_nt(qm, kc)
        m = jnp.maximum(jnp.max(sw, axis=1, keepdims=True), jnp.max(sc, axis=1, keepdims=True))
        ew = jnp.exp(sw - m)
        ec = jnp.exp(sc - m)
        l = jnp.sum(ew, axis=1, keepdims=True) + jnp.sum(ec, axis=1, keepdims=True)
        o = _dot(ew.astype(BF16), vblk) + _dot(ec.astype(BF16), vc)
        outs.append(o / l)
    o_ref[0] = jnp.where(lane < NA_DH, outs[0], outs[1]).astype(o_ref.dtype)


def _na(nq, nk, nv, cnk, cnv, bias):
    b, t, w = nq.shape
    rows = t // GRID_W
    tq = NA_ROWS * GRID_W
    nrb = rows // NA_ROWS
    nctx = cnk.shape[1]
    kind = lambda rb: jnp.where(rb == 0, 0, jnp.where(rb == nrb - 1, 2, 1))
    return pl.pallas_call(
        functools.partial(_na_kernel, rows=rows),
        grid=(b, w // LANES, nrb),
        in_specs=[pl.BlockSpec((1, tq, LANES), lambda bi, hp, rb: (bi, rb, hp)),
                  pl.BlockSpec((1, t, LANES), lambda bi, hp, rb: (bi, 0, hp)),
                  pl.BlockSpec((1, t, LANES), lambda bi, hp, rb: (bi, 0, hp)),
                  pl.BlockSpec((1, nctx, LANES), lambda bi, hp, rb: (bi, 0, hp)),
                  pl.BlockSpec((1, nctx, LANES), lambda bi, hp, rb: (bi, 0, hp)),
                  pl.BlockSpec((2, 1, tq, NA_KEY_ROWS * GRID_W),
                               lambda bi, hp, rb: (hp, kind(rb), 0, 0))],
        out_specs=pl.BlockSpec((1, tq, LANES), lambda bi, hp, rb: (bi, rb, hp)),
        out_shape=jax.ShapeDtypeStruct((b, t, w), BF16),
        compiler_params=_params("parallel", "parallel", "arbitrary"),
        name="na",
    )(nq, nk, nv, cnk, cnv, bias)


def _na_bias_table(na_rpb, rows):
    h = na_rpb.shape[0]
    w = GRID_W
    c = np.arange(w)[:, None]
    kj = np.arange(w)[None, :]
    cs = np.clip(c - NA_KW // 2, 0, w - NA_KW)
    col_valid = (kj >= cs) & (kj < cs + NA_KW)
    dc = np.clip(kj - c + (NA_KW - 1), 0, 2 * NA_KW - 2)
    onehot = np.zeros((2 * NA_KW - 1, w, w), np.float32)
    onehot[dc, np.arange(w)[:, None], np.arange(w)[None, :]] = 1.0
    t2 = jnp.einsum("hrd,dck->hrck", na_rpb, jnp.asarray(onehot), precision=lax.Precision.HIGHEST)
    t2 = jnp.where(jnp.asarray(col_valid)[None, None], t2, NEG)
    t2 = jnp.concatenate([t2, jnp.full((h, 1, w, w), NEG, F32)], axis=1)
    invalid = 2 * NA_KH - 1
    dr_idx = np.full((3, NA_ROWS, NA_KEY_ROWS), invalid, np.int32)
    for kind, r0 in enumerate((0, NA_ROWS, rows - NA_ROWS)):
        ks = int(np.clip(r0 - NA_KH // 2, 0, rows - NA_KEY_ROWS))
        for qa in range(NA_ROWS):
            r = r0 + qa
            rs = int(np.clip(r - NA_KH // 2, 0, rows - NA_KH))
            for kl in range(NA_KEY_ROWS):
                ki = ks + kl
                if rs <= ki < rs + NA_KH:
                    dr_idx[kind, qa, kl] = ki - r + NA_KH - 1
    tab = jnp.take(t2, jnp.asarray(dr_idx.reshape(-1)), axis=1)
    tab = tab.reshape(h, 3, NA_ROWS, NA_KEY_ROWS, w, w).transpose(0, 1, 2, 4, 3, 5)
    return tab.reshape(h, 3, NA_ROWS * w, NA_KEY_ROWS * w)


def _post_kernel(x_ref, mod_ref, hf_ref, hb_ref, mo_ref, na_ref, gm_ref, gn_ref,
                 mnw_ref, wbm_ref, wbn_ref, wout_ref, wr_ref, wsg_ref, wsu_ref, wsd_ref,
                 h2_ref, st_ref, base_ref):
    hm = hf_ref[0].astype(F32) + hb_ref[0].astype(F32)
    parts = []
    for hd in range(M_HEADS):
        t = hm[:, hd * M_DV:(hd + 1) * M_DV]
        parts.append(t * lax.rsqrt(jnp.mean(t * t, axis=-1, keepdims=True) + EPS))
    y_m = jnp.concatenate(parts, axis=1) * mnw_ref[...] * mo_ref[0].astype(F32)
    a = _dot(y_m.astype(BF16), wbm_ref[...])
    bn = _dot(na_ref[0], wbn_ref[...])
    z = gm_ref[0].astype(F32) * a + gn_ref[0].astype(F32) * bn
    y = _dot(z.astype(BF16), wout_ref[...])
    x1 = x_ref[0] + mod_ref[0, 2:3, :] * y
    xn = x1 * lax.rsqrt(jnp.mean(x1 * x1, axis=-1, keepdims=True) + EPS)
    h2 = (xn * (1.0 + mod_ref[0, 4:5, :]) + mod_ref[0, 3:4, :]).astype(BF16)
    h2_ref[0] = h2
    st_ref[...] = _sigmoid(_dot_nt(wr_ref[...], h2))
    sh = _dot(h2, wsg_ref[...])
    sh = sh * _sigmoid(sh) * _dot(h2, wsu_ref[...])
    base_ref[0] = x1 + mod_ref[0, 5:6, :] * _dot(sh.astype(BF16), wsd_ref[...])


def _post(x, mod, hf, hb, mo, yna, gm, gn, mnw, wbm, wbn, wout, wr_t, wsg, wsu, wsd, tm):
    b, t, d = x.shape
    nt = t // tm
    tok = lambda w: pl.BlockSpec((1, tm, w), lambda bi, i: (bi, i, 0))
    res = [mnw, wbm, wbn, wout, wr_t, wsg, wsu, wsd]
    return pl.pallas_call(
        _post_kernel,
        grid=(b, nt),
        in_specs=[tok(d), pl.BlockSpec((1, 8, d), lambda bi, i: (bi, 0, 0)),
                  tok(hf.shape[2]), tok(hb.shape[2]), tok(mo.shape[2]), tok(yna.shape[2]),
                  tok(gm.shape[2]), tok(gn.shape[2])] + [_resident(a.shape) for a in res],
        out_specs=[tok(d),
                   pl.BlockSpec((N_EXPERTS, tm), lambda bi, i: (0, bi * nt + i)),
                   tok(d)],
        out_shape=[jax.ShapeDtypeStruct((b, t, d), BF16),
                   jax.ShapeDtypeStruct((N_EXPERTS, b * t), F32),
                   jax.ShapeDtypeStruct((b, t, d), F32)],
        compiler_params=_params("parallel", "parallel"),
        name="post",
    )(x, mod, hf, hb, mo, yna, gm, gn, *res)


def _route_kernel(s_ref, b_ref, e_ref, w_ref, r_ref, cnt_ref, run_scr):
    @pl.when(pl.program_id(0) == 0)
    def _():
        run_scr[...] = jnp.zeros_like(run_scr)

    s = s_ref[...]
    tm = s.shape[1]
    sel = s + b_ref[...][:, 0:1]
    gsz = N_EXPERTS // N_GROUPS
    ninf = -jnp.inf

    x3 = sel.reshape(N_GROUPS, gsz, tm)
    r3 = lax.broadcasted_iota(jnp.int32, x3.shape, 1)
    m1 = jnp.max(x3, axis=1, keepdims=True)
    i1 = jnp.min(jnp.where(x3 == m1, r3, gsz), axis=1, keepdims=True)
    m2 = jnp.max(jnp.where(r3 == i1, ninf, x3), axis=1)
    gs = m1[:, 0, :] + m2

    gidx = lax.broadcasted_iota(jnp.int32, gs.shape, 0)
    gkeep = jnp.zeros(gs.shape, jnp.bool_)
    cur = gs
    for _ in range(TOPK_GROUPS):
        mm = jnp.max(cur, axis=0, keepdims=True)
        ii = jnp.min(jnp.where(cur == mm, gidx, N_GROUPS), axis=0, keepdims=True)
        hit = gidx == ii
        gkeep = jnp.logical_or(gkeep, hit)
        cur = jnp.where(hit, ninf, cur)
    keep = jnp.broadcast_to(gkeep[:, None, :], x3.shape).reshape(N_EXPERTS, tm)

    row = lax.broadcasted_iota(jnp.int32, s.shape, 0)
    cur = jnp.where(keep, sel, ninf)
    idxs, ws = [], []
    chosen = jnp.zeros(s.shape, jnp.bool_)
    for _ in range(TOP_K):
        mm = jnp.max(cur, axis=0, keepdims=True)
        ii = jnp.min(jnp.where(cur == mm, row, N_EXPERTS), axis=0, keepdims=True)
        hit = row == ii
        idxs.append(ii)
        ws.append(jnp.sum(jnp.where(hit, s, 0.0), axis=0, keepdims=True))
        chosen = jnp.logical_or(chosen, hit)
        cur = jnp.where(hit, ninf, cur)
    wsum = ws[0]
    for wk in ws[1:]:
        wsum = wsum + wk

    chosen_f = jnp.where(chosen, 1.0, 0.0)
    tp = lax.broadcasted_iota(jnp.int32, (tm, tm), 0)
    tc = lax.broadcasted_iota(jnp.int32, (tm, tm), 1)
    before = jnp.where(tp < tc, 1.0, 0.0).astype(BF16)
    rank = _dot(chosen_f.astype(BF16), before) + run_scr[...][:, 0:1]
    run_scr[...] = run_scr[...] + jnp.sum(chosen_f, axis=1, keepdims=True)
    cnt_ref[...] = run_scr[...]

    for kk in range(TOP_K):
        e_ref[kk:kk + 1, :] = idxs[kk]
        w_ref[kk:kk + 1, :] = ws[kk] / wsum * ROUTE_SCALE
        r_ref[kk:kk + 1, :] = jnp.sum(jnp.where(row == idxs[kk], rank, 0.0), axis=0,
                                      keepdims=True).astype(jnp.int32)


def _route(scores_t, bias, tm):
    e, n = scores_t.shape
    blk = lambda: pl.BlockSpec((TOP_K, tm), lambda i: (0, i))
    return pl.pallas_call(
        _route_kernel,
        grid=(n // tm,),
        in_specs=[pl.BlockSpec((e, tm), lambda i: (0, i)), _resident(bias.shape)],
        out_specs=[blk(), blk(), blk(), pl.BlockSpec((e, LANES), lambda i: (0, 0))],
        out_shape=[jax.ShapeDtypeStruct((TOP_K, n), jnp.int32),
                   jax.ShapeDtypeStruct((TOP_K, n), F32),
                   jax.ShapeDtypeStruct((TOP_K, n), jnp.int32),
                   jax.ShapeDtypeStruct((e, LANES), F32)],
        scratch_shapes=[pltpu.VMEM((e, LANES), F32)],
        compiler_params=_params("arbitrary"),
        name="route",
    )(scores_t, bias)


def _experts_kernel(be_ref, nu_ref, x_ref, sw_ref, wg_ref, wu_ref, wd_ref, o_ref):
    i = pl.program_id(0)

    @pl.when(i < nu_ref[0])
    def _():
        x = x_ref[...]
        g = _dot(x, wg_ref[0].astype(BF16))
        u = _dot(x, wu_ref[0].astype(BF16))
        a = (g * _sigmoid(g) * u).astype(BF16)
        o_ref[...] = (_dot(a, wd_ref[0].astype(BF16)) * sw_ref[...]).astype(o_ref.dtype)

    @pl.when(i >= nu_ref[0])
    def _():
        o_ref[...] = jnp.zeros_like(o_ref)


def _experts(block_e, n_used, xs, slot_w, wg, wu, wd):
    p, d = xs.shape
    nb = p // MOE_BLOCK
    ff = wg.shape[2]
    last = lambda i, nu: jnp.minimum(i, nu[0] - 1)
    grid_spec = pltpu.PrefetchScalarGridSpec(
        num_scalar_prefetch=2,
        grid=(nb,),
        in_specs=[pl.BlockSpec((MOE_BLOCK, d), lambda i, be, nu: (last(i, nu), 0)),
                  pl.BlockSpec((MOE_BLOCK, 1), lambda i, be, nu: (last(i, nu), 0)),
                  pl.BlockSpec((1, d, ff), lambda i, be, nu: (be[last(i, nu)], 0, 0)),
                  pl.BlockSpec((1, d, ff), lambda i, be, nu: (be[last(i, nu)], 0, 0)),
                  pl.BlockSpec((1, ff, d), lambda i, be, nu: (be[last(i, nu)], 0, 0))],
        out_specs=pl.BlockSpec((MOE_BLOCK, d), lambda i, be, nu: (i, 0)),
    )
    return pl.pallas_call(
        _experts_kernel,
        grid_spec=grid_spec,
        out_shape=jax.ShapeDtypeStruct((p, d), BF16),
        compiler_params=_params("arbitrary"),
        name="experts",
    )(block_e, n_used, xs, slot_w, wg, wu, wd)


def _rope_tables(t):
    pos = np.arange(t)
    half = M_DQK // 2
    nf = half // 2
    inv = np.power(ROPE_BASE, -np.arange(nf, dtype=np.float32) / nf).astype(np.float32)
    row_pos = jnp.asarray((pos // GRID_W).astype(np.float32))
    col_pos = jnp.asarray((pos % GRID_W).astype(np.float32))
    inv = jnp.asarray(inv)
    ar = row_pos[:, None] * inv[None, :]
    ac = col_pos[:, None] * inv[None, :]
    cos = jnp.concatenate([jnp.cos(ar), jnp.cos(ar), jnp.cos(ac), jnp.cos(ac)], axis=1)
    sin = jnp.concatenate([-jnp.sin(ar), jnp.sin(ar), -jnp.sin(ac), jnp.sin(ac)], axis=1)
    return cos, sin


def _arrange_w_in(w_in, b_mgate):
    d = w_in.shape[0]
    sizes = (512, 512, 1024, 1024, 16, 512, 512, 512, 1024, 1024)
    offs = np.concatenate([[0], np.cumsum(sizes)])
    mq, mk, mv, mo, mg, nq, nk, nv, gm, gn = [w_in[:, offs[i]:offs[i + 1]] for i in range(10)]
    pad = jnp.zeros((d, LANES - 2 * M_HEADS), w_in.dtype)
    gi = jnp.concatenate([mg[:, 0:4], mg[:, 8:12], pad], axis=1)
    gf = jnp.concatenate([mg[:, 4:8], mg[:, 12:16], pad], axis=1)
    w_all = jnp.concatenate([mq, mk, mv, mo, gi, gf, nq, nk, nv, gm, gn], axis=1).astype(BF16)
    bpad = jnp.zeros((LANES - 2 * M_HEADS,), F32)
    bg = jnp.stack([jnp.concatenate([b_mgate[0:4], b_mgate[8:12], bpad]),
                    jnp.concatenate([b_mgate[4:8], b_mgate[12:16], bpad])])
    bg = jnp.concatenate([bg, jnp.zeros((6, LANES), F32)], axis=0)
    return w_all, bg


def _segment_mats():
    na_w = NA_HEADS * NA_DH
    seg = np.zeros((na_w, LANES), np.float32)
    seg[np.arange(na_w), np.arange(na_w) // NA_DH] = 1.0
    return jnp.asarray(seg, BF16), jnp.asarray(seg.T.copy(), BF16)


def kernel(x, c, ctx, c_ctx, w_ada, b_ada, w_in, b_mgate, m_norm_w, na_qn_w, na_kn_w, na_rpb,
           w_br_m, w_br_na, w_out, w_router, router_bias, w_exp_gate, w_exp_up, w_exp_down,
           w_sh_gate, w_sh_up, w_sh_down):
    b, t, d = x.shape
    n = b * t
    rows = t // GRID_W
    l = 0

    cc = jnp.concatenate([c, c_ctx[None, :], jnp.zeros((8 - b - 1, d), F32)], axis=0)
    mod = _ada(cc, w_ada[l], b_ada[l])
    mod = mod.reshape(8, 6, d)
    mod = jnp.concatenate([mod, jnp.zeros((8, 2, d), F32)], axis=1)
    mod_x = mod[:b]
    mod_c = jnp.broadcast_to(mod[b:b + 1], (b, 8, d))

    w_all, bg = _arrange_w_in(w_in[l], b_mgate[l])
    seg, segt = _segment_mats()
    qnw = jnp.tile(na_qn_w[l], NA_HEADS)[None, :]
    knw = jnp.tile(na_kn_w[l], NA_HEADS)[None, :]
    tm = min(256, t)

    cp = _inproj(ctx, mod_c, w_all, bg, qnw, knw, seg, segt, None, min(tm, ctx.shape[1]))
    xp = _inproj(x, mod_x, w_all, bg, qnw, knw, seg, segt, _rope_tables(t), tm)
    cmq, cmk, cmv, _, cgi, cgf, _, cnk, cnv, _, _ = cp
    mq, mk, mv, mo, gi, gf, nq, nk, nv, gm, gn = xp

    c0 = jnp.zeros((b, 8, M_DQK, M_DV), F32)
    v0 = jnp.zeros((b, 8, LANES), F32)
    _, _, c1, n1, m1 = _mlstm(cmq, cmk, cmv, cgi, cgf, c0, v0, v0)
    hf, hb, _, _, _ = _mlstm(mq, mk, mv, gi, gf, c1, n1, m1)

    yna = _na(nq, nk, nv, cnk, cnv, _na_bias_table(na_rpb[l], rows))

    h2, scores_t, base = _post(
        x, mod_x, hf, hb, mo, yna, gm, gn, m_norm_w[l][None, :],
        w_br_m[l].astype(BF16), w_br_na[l].astype(BF16), w_out[l].astype(BF16),
        w_router[l].T.astype(BF16), w_sh_gate[l].astype(BF16), w_sh_up[l].astype(BF16),
        w_sh_down[l].astype(BF16), tm)

    bias_col = jnp.broadcast_to(router_bias[l][:, None], (N_EXPERTS, LANES))
    top_e, top_w, rank, cnt = _route(scores_t, bias_col, min(512, n))

    counts = cnt[:, 0].astype(jnp.int32)
    padded = (counts + MOE_BLOCK - 1) // MOE_BLOCK * MOE_BLOCK
    pend = jnp.cumsum(padded)
    pstart = pend - padded
    nb = -(-(n * TOP_K) // MOE_BLOCK) + N_EXPERTS
    p_rows = nb * MOE_BLOCK
    pos = pstart[top_e] + rank
    tok_id = jnp.broadcast_to(jnp.arange(n, dtype=jnp.int32)[None, :], pos.shape)
    slot_tok = jnp.full((p_rows,), n, jnp.int32).at[pos.reshape(-1)].set(tok_id.reshape(-1))
    slot_w = jnp.zeros((p_rows,), F32).at[pos.reshape(-1)].set(top_w.reshape(-1))
    block_e = jnp.clip(jnp.searchsorted(pend, jnp.arange(nb, dtype=jnp.int32) * MOE_BLOCK,
                                        side="right"), 0, N_EXPERTS - 1).astype(jnp.int32)
    n_used = (pend[-1:] // MOE_BLOCK).astype(jnp.int32)

    h2_pad = jnp.concatenate([h2.reshape(n, d), jnp.zeros((1, d), BF16)], axis=0)
    xs = jnp.take(h2_pad, slot_tok, axis=0)
    ys = _experts(block_e, n_used, xs, slot_w[:, None], w_exp_gate[l], w_exp_up[l], w_exp_down[l])
    routed = jnp.sum(jnp.take(ys, pos, axis=0).astype(F32), axis=0)
    g2 = mod_x[:, 5:6, :]
    return base + g2 * routed.reshape(b, t, d)
```

```python
import functools

import numpy as np
import jax
import jax.numpy as jnp
from jax import lax
from jax.experimental import pallas as pl
from jax.experimental.pallas import tpu as pltpu
from jax.experimental.pallas import tpu_sc as plsc

F32 = jnp.float32
BF16 = jnp.bfloat16

EPS = 1e-6
GRID_W = 64
M_HEADS, M_DQK, M_DV = 4, 128, 256
ROPE_BASE = 10000.0
NA_HEADS, NA_DH, NA_KH, NA_KW = 8, 64, 8, 16
N_EXPERTS, TOP_K, N_GROUPS, TOPK_GROUPS = 256, 8, 8, 4
ROUTE_SCALE = 2.5

LANES = 128
VMEM_LIMIT = 56 * 1024 * 1024
NEG = -1e30

MLSTM_CHUNK = 256
NA_ROWS = 4
NA_KEY_ROWS = NA_ROWS + NA_KH - 1
MOE_BLOCK = 256

_W_SEGS = (("mq", 512), ("mk", 512), ("mv", 1024), ("mo", 1024), ("gi", 128), ("gf", 128),
           ("nq", 512), ("nk", 512), ("nv", 512), ("gm", 1024), ("gn", 1024))
_W_OFF = {}
_o = 0
for _n, _w in _W_SEGS:
    _W_OFF[_n] = (_o, _w)
    _o += _w
W_COLS = _o


def _dot(a, b):
    return jnp.dot(a, b, preferred_element_type=F32)


def _dot_nt(a, b):
    return lax.dot_general(a, b, (((1,), (1,)), ((), ())), preferred_element_type=F32)


def _sigmoid(x):
    return 1.0 / (1.0 + jnp.exp(-x))


def _pack_bf16_pairs(v):
    w = v.shape[1] // 2
    bits = pltpu.bitcast(v.astype(BF16).astype(F32), jnp.int32)
    return lax.shift_right_logical(bits[:, :w], 16) | bits[:, w:]


def _unpack_bf16_pairs(p):
    lo = pltpu.bitcast(lax.shift_left(p, 16), F32)
    hi = pltpu.bitcast(p & jnp.int32(-65536), F32)
    return jnp.concatenate([lo, hi], axis=1)


def _params(*sem):
    return pltpu.CompilerParams(dimension_semantics=sem, vmem_limit_bytes=VMEM_LIMIT)


def _resident(shape):
    nd = len(shape)
    return pl.BlockSpec(shape, lambda *_: (0,) * nd, pipeline_mode=pl.Buffered(1))


def _ada_kernel(c_ref, w_ref, b_ref, o_ref):
    c = c_ref[...]
    s = c * _sigmoid(c)
    o_ref[...] = _dot(s.astype(BF16), w_ref[...].astype(BF16)) + b_ref[...]


def _ada(cc, w_ada, b_ada):
    d = cc.shape[1]
    n = w_ada.shape[1]
    return pl.pallas_call(
        _ada_kernel,
        grid=(n // d,),
        in_specs=[pl.BlockSpec((8, d), lambda j: (0, 0)),
                  pl.BlockSpec((d, d), lambda j: (0, j)),
                  pl.BlockSpec((1, d), lambda j: (0, j))],
        out_specs=pl.BlockSpec((8, d), lambda j: (0, j)),
        out_shape=jax.ShapeDtypeStruct((8, n), F32),
        compiler_params=_params("arbitrary"),
        name="ada",
    )(cc, w_ada, b_ada.reshape(1, n))


def _rope_rotate(t, cos, sin):
    lane = lax.broadcasted_iota(jnp.int32, t.shape, 1)
    partner = jnp.where((lane & 32) == 0, pltpu.roll(t, 96, 1), pltpu.roll(t, 32, 1))
    return t * cos + partner * sin


def _inproj_kernel(*refs, rope):
    if rope:
        (x_ref, mod_ref, w_ref, bg_ref, qnw_ref, knw_ref, seg_ref, segt_ref, cos_ref, sin_ref,
         mq_ref, mk_ref, mv_ref, mo_ref, gi_ref, gf_ref, nq_ref, nk_ref, nv_ref, gm_ref, gn_ref) = refs
    else:
        (x_ref, mod_ref, w_ref, bg_ref, qnw_ref, knw_ref, seg_ref, segt_ref,
         mq_ref, mk_ref, mv_ref, mo_ref, gi_ref, gf_ref, nq_ref, nk_ref, nv_ref, gm_ref, gn_ref) = refs
    x = x_ref[0]
    xn = x * lax.rsqrt(jnp.mean(x * x, axis=-1, keepdims=True) + EPS)
    h = xn * (1.0 + mod_ref[0, 1:2, :]) + mod_ref[0, 0:1, :]
    hb = h.astype(BF16)

    def proj(name):
        off, width = _W_OFF[name]
        return _dot(hb, w_ref[:, off:off + width])

    def head_rms(t, w_row, scale):
        ss = _dot((t * t).astype(BF16), seg_ref[...])
        r = lax.rsqrt(ss * (1.0 / NA_DH) + EPS)
        r_hi = r.astype(BF16)
        r_lo = (r - r_hi.astype(F32)).astype(BF16)
        rb = _dot(r_hi, segt_ref[...]) + _dot(r_lo, segt_ref[...])
        return t * rb * w_row * scale

    mq = proj("mq") * (M_DQK ** -0.5)
    mk = proj("mk")
    if rope:
        cos, sin = cos_ref[...], sin_ref[...]
        mq = jnp.concatenate([_rope_rotate(mq[:, i * LANES:(i + 1) * LANES], cos, sin)
                              for i in range(M_HEADS)], axis=1)
        mk = jnp.concatenate([_rope_rotate(mk[:, i * LANES:(i + 1) * LANES], cos, sin)
                              for i in range(M_HEADS)], axis=1)
    mq_ref[0] = mq.astype(BF16)
    mk_ref[0] = mk.astype(BF16)
    mv_ref[0] = proj("mv").astype(BF16)
    mo_ref[0] = _sigmoid(proj("mo")).astype(BF16)
    gi_ref[0] = proj("gi") + bg_ref[0:1, :]
    gf_ref[0] = proj("gf") + bg_ref[1:2, :]
    nq_ref[0] = head_rms(proj("nq"), qnw_ref[...], NA_DH ** -0.5).astype(BF16)
    nk_ref[0] = head_rms(proj("nk"), knw_ref[...], 1.0).astype(BF16)
    nv_ref[0] = proj("nv").astype(BF16)
    gm_ref[0] = _sigmoid(proj("gm")).astype(BF16)
    gn_ref[0] = _sigmoid(proj("gn")).astype(BF16)


def _inproj(x, mod, w_all, bg, qnw, knw, seg, segt, rope_tabs, tm):
    b, t, d = x.shape
    rope = rope_tabs is not None
    tok = lambda w: pl.BlockSpec((1, tm, w), lambda bi, i: (bi, i, 0))
    in_specs = [tok(d),
                pl.BlockSpec((1, 8, d), lambda bi, i: (bi, 0, 0)),
                _resident(w_all.shape), _resident(bg.shape), _resident(qnw.shape),
                _resident(knw.shape), _resident(seg.shape), _resident(segt.shape)]
    args = [x, mod, w_all, bg, qnw, knw, seg, segt]
    if rope:
        in_specs += [pl.BlockSpec((tm, LANES), lambda bi, i: (i, 0))] * 2
        args += list(rope_tabs)
    widths = [("mq", BF16), ("mk", BF16), ("mv", BF16), ("mo", BF16), ("gi", F32), ("gf", F32),
              ("nq", BF16), ("nk", BF16), ("nv", BF16), ("gm", BF16), ("gn", BF16)]
    out_specs = [tok(_W_OFF[n][1]) for n, _ in widths]
    out_shape = [jax.ShapeDtypeStruct((b, t, _W_OFF[n][1]), dt) for n, dt in widths]
    return pl.pallas_call(
        functools.partial(_inproj_kernel, rope=rope),
        grid=(b, t // tm),
        in_specs=in_specs, out_specs=out_specs, out_shape=out_shape,
        compiler_params=_params("parallel", "parallel"),
        name="inproj_rope" if rope else "inproj_ctx",
    )(*args)


def _log_sigmoid(x):
    return jnp.minimum(x, 0.0) - jnp.log(1.0 + jnp.exp(-jnp.abs(x)))


def _mlstm_kernel(qf_ref, kf_ref, vf_ref, gif_ref, gff_ref,
                  qb_ref, kb_ref, vb_ref, gib_ref, gfb_ref,
                  c0_ref, n0_ref, m0_ref,
                  hf_ref, hb_ref, cn_ref, nn_ref, mn_ref,
                  c_scr, n_scr, m_scr):
    step = pl.program_id(1)
    L = qf_ref.shape[1]

    @pl.when(step == 0)
    def _():
        c_scr[...] = c0_ref[0]
        n_scr[...] = n0_ref[0]
        m_scr[...] = m0_ref[0]

    t_idx = lax.broadcasted_iota(jnp.int32, (L, L), 0)
    s_idx = lax.broadcasted_iota(jnp.int32, (L, L), 1)
    dirs = ((qf_ref, kf_ref, vf_ref, gif_ref, gff_ref, hf_ref, s_idx <= t_idx),
            (qb_ref, kb_ref, vb_ref, gib_ref, gfb_ref, hb_ref, s_idx >= t_idx))
    for d, (q_ref, k_ref, v_ref, gi_ref, gf_ref, h_ref, mask) in enumerate(dirs):
        gi = gi_ref[0]
        ls = _log_sigmoid(gf_ref[0])
        tri = mask.astype(F32)
        bcum = jnp.dot(tri, ls, precision=lax.Precision.HIGHEST, preferred_element_type=F32)
        u_t = jnp.transpose(gi - bcum)
        g_row = jnp.sum(ls, axis=0, keepdims=True)
        for hd in range(M_HEADS):
            j = d * M_HEADS + hd
            q = q_ref[0, :, hd * M_DQK:(hd + 1) * M_DQK]
            k = k_ref[0, :, hd * M_DQK:(hd + 1) * M_DQK]
            v = v_ref[0, :, hd * M_DV:(hd + 1) * M_DV]
            b_col = bcum[:, j:j + 1]
            i_col = gi[:, j:j + 1]
            u_row = u_t[j:j + 1, :]
            g = g_row[:, j:j + 1]
            m_prev = m_scr[j:j + 1, 0:1]
            n_prev = n_scr[j:j + 1, :]
            c_prev = c_scr[j]

            m_loc = jnp.max(jnp.where(mask, u_row, NEG), axis=1, keepdims=True)
            m_row = jnp.maximum(m_loc, m_prev)
            dmat = jnp.exp(jnp.where(mask, u_row - m_row, NEG))
            s = _dot_nt(q, k) * dmat
            w_inter = jnp.exp(m_prev - m_row)
            num = _dot(s.astype(BF16), v) + w_inter * _dot(q, c_prev.astype(BF16))
            den = (jnp.sum(s, axis=1, keepdims=True)
                   + w_inter * jnp.sum(q.astype(F32) * n_prev, axis=1, keepdims=True))
            bound = jnp.exp(-(b_col + m_row))
            h_ref[0, :, hd * M_DV:(hd + 1) * M_DV] = (
                num / jnp.maximum(jnp.abs(den), bound)).astype(h_ref.dtype)

            a_col = g - b_col + i_col
            m_new = jnp.maximum(g + m_prev, jnp.max(a_col, axis=0, keepdims=True))
            decay = jnp.exp(g + m_prev - m_new)
            kw = k.astype(F32) * jnp.exp(a_col - m_new)
            c_scr[j] = decay * c_prev + _dot(jnp.transpose(kw).astype(BF16), v)
            n_scr[j:j + 1, :] = decay * n_prev + jnp.sum(kw, axis=0, keepdims=True)
            m_scr[j:j + 1, :] = jnp.broadcast_to(m_new, (1, LANES))

    @pl.when(step == pl.num_programs(1) - 1)
    def _():
        cn_ref[0] = c_scr[...]
        nn_ref[0] = n_scr[...]
        mn_ref[0] = m_scr[...]


def _mlstm(q, k, v, gi, gf, c0, n0, m0):
    b, t, _ = q.shape
    L = min(MLSTM_CHUNK, t)
    nc = t // L
    fwd = lambda w: pl.BlockSpec((1, L, w), lambda bi, i: (bi, i, 0))
    bwd = lambda w: pl.BlockSpec((1, L, w), lambda bi, i: (bi, nc - 1 - i, 0))
    st_c = pl.BlockSpec((1, 8, M_DQK, M_DV), lambda bi, i: (bi, 0, 0, 0))
    st_v = pl.BlockSpec((1, 8, LANES), lambda bi, i: (bi, 0, 0))
    qk_w, v_w = M_HEADS * M_DQK, M_HEADS * M_DV
    return pl.pallas_call(
        _mlstm_kernel,
        grid=(b, nc),
        in_specs=[fwd(qk_w), fwd(qk_w), fwd(v_w), fwd(LANES), fwd(LANES),
                  bwd(qk_w), bwd(qk_w), bwd(v_w), bwd(LANES), bwd(LANES),
                  st_c, st_v, st_v],
        out_specs=[fwd(v_w), bwd(v_w), st_c, st_v, st_v],
        out_shape=[jax.ShapeDtypeStruct((b, t, v_w), BF16),
                   jax.ShapeDtypeStruct((b, t, v_w), BF16),
                   jax.ShapeDtypeStruct(c0.shape, F32),
                   jax.ShapeDtypeStruct(n0.shape, F32),
                   jax.ShapeDtypeStruct(m0.shape, F32)],
        scratch_shapes=[pltpu.VMEM((8, M_DQK, M_DV), F32),
                        pltpu.VMEM((8, LANES), F32),
                        pltpu.VMEM((8, LANES), F32)],
        compiler_params=_params("parallel", "arbitrary"),
        name="mlstm",
    )(q, k, v, gi, gf, q, k, v, gi, gf, c0, n0, m0)


def _na_kernel(q_ref, k_ref, v_ref, kc_ref, vc_ref, bias_ref, o_ref, *, rows):
    r0 = pl.program_id(2) * NA_ROWS
    ks = jnp.clip(r0 - NA_KH // 2, 0, rows - NA_KEY_ROWS)
    kstart = pl.multiple_of(ks * GRID_W, GRID_W)
    nkeys = NA_KEY_ROWS * GRID_W
    kblk = k_ref[0, pl.ds(kstart, nkeys), :]
    vblk = v_ref[0, pl.ds(kstart, nkeys), :]
    kc = kc_ref[0]
    vc = vc_ref[0]
    q = q_ref[0]
    lane = lax.broadcasted_iota(jnp.int32, q.shape, 1)
    outs = []
    for hh in range(2):
        in_head = (lane < NA_DH) if hh == 0 else (lane >= NA_DH)
        qm = jnp.where(in_head, q, jnp.zeros_like(q))
        sw = _dot_nt(qm, kblk) + bias_ref[hh, 0]
        sc = _dot_nt(qm, kc)
        m = jnp.maximum(jnp.max(sw, axis=1, keepdims=True), jnp.max(sc, axis=1, keepdims=True))
        ew = jnp.exp(sw - m)
        ec = jnp.exp(sc - m)
        l = jnp.sum(ew, axis=1, keepdims=True) + jnp.sum(ec, axis=1, keepdims=True)
        o = _dot(ew.astype(BF16), vblk) + _dot(ec.astype(BF16), vc)
        outs.append(o / l)
    o_ref[0] = jnp.where(lane < NA_DH, outs[0], outs[1]).astype(o_ref.dtype)


def _na(nq, nk, nv, cnk, cnv, bias):
    b, t, w = nq.shape
    rows = t // GRID_W
    tq = NA_ROWS * GRID_W
    nrb = rows // NA_ROWS
    nctx = cnk.shape[1]
    kind = lambda rb: jnp.where(rb == 0, 0, jnp.where(rb == nrb - 1, 2, 1))
    return pl.pallas_call(
        functools.partial(_na_kernel, rows=rows),
        grid=(b, w // LANES, nrb),
        in_specs=[pl.BlockSpec((1, tq, LANES), lambda bi, hp, rb: (bi, rb, hp)),
                  pl.BlockSpec((1, t, LANES), lambda bi, hp, rb: (bi, 0, hp)),
                  pl.BlockSpec((1, t, LANES), lambda bi, hp, rb: (bi, 0, hp)),
                  pl.BlockSpec((1, nctx, LANES), lambda bi, hp, rb: (bi, 0, hp)),
                  pl.BlockSpec((1, nctx, LANES), lambda bi, hp, rb: (bi, 0, hp)),
                  pl.BlockSpec((2, 1, tq, NA_KEY_ROWS * GRID_W),
                               lambda bi, hp, rb: (hp, kind(rb), 0, 0))],
        out_specs=pl.BlockSpec((1, tq, LANES), lambda bi, hp, rb: (bi, rb, hp)),
        out_shape=jax.ShapeDtypeStruct((b, t, w), BF16),
        compiler_params=_params("parallel", "parallel", "arbitrary"),
        name="na",
    )(nq, nk, nv, cnk, cnv, bias)


def _na_bias_table(na_rpb, rows):
    h = na_rpb.shape[0]
    w = GRID_W
    c = np.arange(w)[:, None]
    kj = np.arange(w)[None, :]
    cs = np.clip(c - NA_KW // 2, 0, w - NA_KW)
    col_valid = (kj >= cs) & (kj < cs + NA_KW)
    dc = np.clip(kj - c + (NA_KW - 1), 0, 2 * NA_KW - 2)
    onehot = np.zeros((2 * NA_KW - 1, w, w), np.float32)
    onehot[dc, np.arange(w)[:, None], np.arange(w)[None, :]] = 1.0
    t2 = jnp.einsum("hrd,dck->hrck", na_rpb, jnp.asarray(onehot), precision=lax.Precision.HIGHEST)
    t2 = jnp.where(jnp.asarray(col_valid)[None, None], t2, NEG)
    t2 = jnp.concatenate([t2, jnp.full((h, 1, w, w), NEG, F32)], axis=1)
    invalid = 2 * NA_KH - 1
    dr_idx = np.full((3, NA_ROWS, NA_KEY_ROWS), invalid, np.int32)
    for kind, r0 in enumerate((0, NA_ROWS, rows - NA_ROWS)):
        ks = int(np.clip(r0 - NA_KH // 2, 0, rows - NA_KEY_ROWS))
        for qa in range(NA_ROWS):
            r = r0 + qa
            rs = int(np.clip(r - NA_KH // 2, 0, rows - NA_KH))
            for kl in range(NA_KEY_ROWS):
                ki = ks + kl
                if rs <= ki < rs + NA_KH:
                    dr_idx[kind, qa, kl] = ki - r + NA_KH - 1
    tab = jnp.take(t2, jnp.asarray(dr_idx.reshape(-1)), axis=1)
    tab = tab.reshape(h, 3, NA_ROWS, NA_KEY_ROWS, w, w).transpose(0, 1, 2, 4, 3, 5)
    return tab.reshape(h, 3, NA_ROWS * w, NA_KEY_ROWS * w)


def _post_kernel(x_ref, mod_ref, hf_ref, hb_ref, mo_ref, na_ref, gm_ref, gn_ref,
                 mnw_ref, wbm_ref, wbn_ref, wout_ref, wr_ref, wsg_ref, wsu_ref, wsd_ref,
                 h2_ref, st_ref, base_ref):
    hm = hf_ref[0].astype(F32) + hb_ref[0].astype(F32)
    parts = []
    for hd in range(M_HEADS):
        t = hm[:, hd * M_DV:(hd + 1) * M_DV]
        parts.append(t * lax.rsqrt(jnp.mean(t * t, axis=-1, keepdims=True) + EPS))
    y_m = jnp.concatenate(parts, axis=1) * mnw_ref[...] * mo_ref[0].astype(F32)
    a = _dot(y_m.astype(BF16), wbm_ref[...])
    bn = _dot(na_ref[0], wbn_ref[...])
    z = gm_ref[0].astype(F32) * a + gn_ref[0].astype(F32) * bn
    y = _dot(z.astype(BF16), wout_ref[...])
    x1 = x_ref[0] + mod_ref[0, 2:3, :] * y
    xn = x1 * lax.rsqrt(jnp.mean(x1 * x1, axis=-1, keepdims=True) + EPS)
    h2f = xn * (1.0 + mod_ref[0, 4:5, :]) + mod_ref[0, 3:4, :]
    h2_ref[...] = _pack_bf16_pairs(h2f)
    h2 = h2f.astype(BF16)
    st_ref[...] = _sigmoid(_dot_nt(wr_ref[...], h2))
    sh = _dot(h2, wsg_ref[...])
    sh = sh * _sigmoid(sh) * _dot(h2, wsu_ref[...])
    base_ref[0] = x1 + mod_ref[0, 5:6, :] * _dot(sh.astype(BF16), wsd_ref[...])


def _post(x, mod, hf, hb, mo, yna, gm, gn, mnw, wbm, wbn, wout, wr_t, wsg, wsu, wsd, tm):
    b, t, d = x.shape
    nt = t // tm
    tok = lambda w: pl.BlockSpec((1, tm, w), lambda bi, i: (bi, i, 0))
    res = [mnw, wbm, wbn, wout, wr_t, wsg, wsu, wsd]
    return pl.pallas_call(
        _post_kernel,
        grid=(b, nt),
        in_specs=[tok(d), pl.BlockSpec((1, 8, d), lambda bi, i: (bi, 0, 0)),
                  tok(hf.shape[2]), tok(hb.shape[2]), tok(mo.shape[2]), tok(yna.shape[2]),
                  tok(gm.shape[2]), tok(gn.shape[2])] + [_resident(a.shape) for a in res],
        out_specs=[pl.BlockSpec((tm, d // 2), lambda bi, i: (bi * nt + i, 0)),
                   pl.BlockSpec((N_EXPERTS, tm), lambda bi, i: (0, bi * nt + i)),
                   tok(d)],
        out_shape=[jax.ShapeDtypeStruct((b * t, d // 2), jnp.int32),
                   jax.ShapeDtypeStruct((N_EXPERTS, b * t), F32),
                   jax.ShapeDtypeStruct((b, t, d), F32)],
        compiler_params=_params("parallel", "parallel"),
        name="post",
    )(x, mod, hf, hb, mo, yna, gm, gn, *res)


def _route_kernel(s_ref, b_ref, e_ref, w_ref, r_ref, cnt_ref, run_scr):
    @pl.when(pl.program_id(0) == 0)
    def _():
        run_scr[...] = jnp.zeros_like(run_scr)

    s = s_ref[...]
    tm = s.shape[1]
    sel = s + b_ref[...][:, 0:1]
    gsz = N_EXPERTS // N_GROUPS
    ninf = -jnp.inf

    x3 = sel.reshape(N_GROUPS, gsz, tm)
    r3 = lax.broadcasted_iota(jnp.int32, x3.shape, 1)
    m1 = jnp.max(x3, axis=1, keepdims=True)
    i1 = jnp.min(jnp.where(x3 == m1, r3, gsz), axis=1, keepdims=True)
    m2 = jnp.max(jnp.where(r3 == i1, ninf, x3), axis=1)
    gs = m1[:, 0, :] + m2

    gidx = lax.broadcasted_iota(jnp.int32, gs.shape, 0)
    gkeep = jnp.zeros(gs.shape, jnp.bool_)
    cur = gs
    for _ in range(TOPK_GROUPS):
        mm = jnp.max(cur, axis=0, keepdims=True)
        ii = jnp.min(jnp.where(cur == mm, gidx, N_GROUPS), axis=0, keepdims=True)
        hit = gidx == ii
        gkeep = jnp.logical_or(gkeep, hit)
        cur = jnp.where(hit, ninf, cur)
    keep = jnp.broadcast_to(gkeep[:, None, :], x3.shape).reshape(N_EXPERTS, tm)

    row = lax.broadcasted_iota(jnp.int32, s.shape, 0)
    cur = jnp.where(keep, sel, ninf)
    idxs, ws = [], []
    chosen = jnp.zeros(s.shape, jnp.bool_)
    for _ in range(TOP_K):
        mm = jnp.max(cur, axis=0, keepdims=True)
        ii = jnp.min(jnp.where(cur == mm, row, N_EXPERTS), axis=0, keepdims=True)
        hit = row == ii
        idxs.append(ii)
        ws.append(jnp.sum(jnp.where(hit, s, 0.0), axis=0, keepdims=True))
        chosen = jnp.logical_or(chosen, hit)
        cur = jnp.where(hit, ninf, cur)
    wsum = ws[0]
    for wk in ws[1:]:
        wsum = wsum + wk

    chosen_f = jnp.where(chosen, 1.0, 0.0)
    tp = lax.broadcasted_iota(jnp.int32, (tm, tm), 0)
    tc = lax.broadcasted_iota(jnp.int32, (tm, tm), 1)
    before = jnp.where(tp < tc, 1.0, 0.0).astype(BF16)
    rank = _dot(chosen_f.astype(BF16), before) + run_scr[...][:, 0:1]
    run_scr[...] = run_scr[...] + jnp.sum(chosen_f, axis=1, keepdims=True)
    cnt_ref[...] = run_scr[...]

    for kk in range(TOP_K):
        e_ref[kk:kk + 1, :] = idxs[kk]
        w_ref[kk:kk + 1, :] = ws[kk] / wsum * ROUTE_SCALE
        r_ref[kk:kk + 1, :] = jnp.sum(jnp.where(row == idxs[kk], rank, 0.0), axis=0,
                                      keepdims=True).astype(jnp.int32)


def _route(scores_t, bias, tm):
    e, n = scores_t.shape
    blk = lambda: pl.BlockSpec((TOP_K, tm), lambda i: (0, i))
    return pl.pallas_call(
        _route_kernel,
        grid=(n // tm,),
        in_specs=[pl.BlockSpec((e, tm), lambda i: (0, i)), _resident(bias.shape)],
        out_specs=[blk(), blk(), blk(), pl.BlockSpec((e, LANES), lambda i: (0, 0))],
        out_shape=[jax.ShapeDtypeStruct((TOP_K, n), jnp.int32),
                   jax.ShapeDtypeStruct((TOP_K, n), F32),
                   jax.ShapeDtypeStruct((TOP_K, n), jnp.int32),
                   jax.ShapeDtypeStruct((e, LANES), F32)],
        scratch_shapes=[pltpu.VMEM((e, LANES), F32)],
        compiler_params=_params("arbitrary"),
        name="route",
    )(scores_t, bias)


def _slot_kernel(e_ref, r_ref, ps_ref, o_ref):
    row = lax.broadcasted_iota(jnp.int32, (N_EXPERTS, e_ref.shape[1]), 0)
    ps = ps_ref[...][:, 0:1]
    for kk in range(TOP_K):
        first = jnp.sum(jnp.where(row == e_ref[kk:kk + 1, :], ps, 0.0), axis=0, keepdims=True)
        o_ref[kk:kk + 1, :] = first.astype(jnp.int32) + r_ref[kk:kk + 1, :]


def _slots(top_e, rank, pstart, tm):
    k, n = top_e.shape
    blk = pl.BlockSpec((k, tm), lambda i: (0, i))
    return pl.pallas_call(
        _slot_kernel,
        grid=(n // tm,),
        in_specs=[blk, blk, _resident(pstart.shape)],
        out_specs=blk,
        out_shape=jax.ShapeDtypeStruct((k, n), jnp.int32),
        compiler_params=_params("parallel"),
        name="slots",
    )(top_e, rank, pstart)


SC_WINDOW = 128


def _sc_mesh():
    return plsc.VectorSubcoreMesh(core_axis_name="core", subcore_axis_name="subcore")


def _sc_workers():
    info = plsc.get_sparse_core_info()
    return info.num_cores, info.num_cores * info.num_subcores


def _dispatch_rows(x, pos, p_rows):
    n, w = x.shape
    kk = pos.shape[0]
    ncores, nw = _sc_workers()
    steps = n // nw // SC_WINDOW
    pos4 = pos.reshape(kk, nw, steps, SC_WINDOW).transpose(1, 2, 0, 3)

    @functools.partial(
        pl.kernel, mesh=_sc_mesh(),
        out_type=jax.ShapeDtypeStruct((p_rows, w), x.dtype),
        scratch_types=[pltpu.VMEM((kk, SC_WINDOW), jnp.int32),
                       pltpu.VMEM((SC_WINDOW, w), x.dtype),
                       pltpu.SemaphoreType.DMA],
    )
    def scatter(x_hbm, i_hbm, o_hbm, idx_v, rows_v, sem):
        wid = lax.axis_index("subcore") * ncores + lax.axis_index("core")

        @pl.loop(0, steps)
        def _(s):
            base = pl.multiple_of((wid * steps + s) * SC_WINDOW, SC_WINDOW)
            pltpu.sync_copy(i_hbm.at[wid, s], idx_v)
            pltpu.sync_copy(x_hbm.at[pl.ds(base, SC_WINDOW)], rows_v)
            copies = [pltpu.make_async_copy(rows_v, o_hbm.at[idx_v.at[j]], sem) for j in range(kk)]
            for cp in copies:
                cp.start()
            for cp in copies:
                cp.wait()

    return scatter(x, pos4)


def _gather_rows(x, idx):
    m = idx.shape[0]
    w = x.shape[1]
    ncores, nw = _sc_workers()
    steps = m // nw // SC_WINDOW
    idx3 = idx.reshape(nw, steps, SC_WINDOW)

    @functools.partial(
        pl.kernel, mesh=_sc_mesh(),
        out_type=jax.ShapeDtypeStruct((m, w), x.dtype),
        scratch_types=[pltpu.VMEM((steps, SC_WINDOW), jnp.int32),
                       pltpu.VMEM((SC_WINDOW, w), x.dtype),
                       pltpu.SemaphoreType.DMA],
    )
    def gather(x_hbm, i_hbm, o_hbm, idx_v, rows_v, sem):
        wid = lax.axis_index("subcore") * ncores + lax.axis_index("core")
        pltpu.sync_copy(i_hbm.at[wid], idx_v)

        @pl.loop(0, steps)
        def _(s):
            pltpu.async_copy(x_hbm.at[idx_v.at[s]], rows_v, sem).wait()
            base = pl.multiple_of((wid * steps + s) * SC_WINDOW, SC_WINDOW)
            pltpu.sync_copy(rows_v, o_hbm.at[pl.ds(base, SC_WINDOW)])

    return gather(x, idx3)


def _experts_kernel(be_ref, nv_ref, nu_ref, x_ref, wg_ref, wu_ref, wd_ref, o_ref):
    i = pl.program_id(0)

    @pl.when(i < nu_ref[0])
    def _():
        rid = lax.broadcasted_iota(jnp.int32, x_ref.shape, 0)
        xp = jnp.where(rid < nv_ref[i], x_ref[...], 0)
        x = _unpack_bf16_pairs(xp).astype(BF16)
        g = _dot(x, wg_ref[0].astype(BF16))
        u = _dot(x, wu_ref[0].astype(BF16))
        a = (g * _sigmoid(g) * u).astype(BF16)
        o_ref[...] = _pack_bf16_pairs(_dot(a, wd_ref[0].astype(BF16)))

    @pl.when(i >= nu_ref[0])
    def _():
        o_ref[...] = jnp.zeros_like(o_ref)


def _experts(block_e, n_valid, n_used, xs, wg, wu, wd):
    p, dp = xs.shape
    nb = p // MOE_BLOCK
    d, ff = wg.shape[1], wg.shape[2]
    last = lambda i, nu: jnp.minimum(i, nu[0] - 1)
    grid_spec = pltpu.PrefetchScalarGridSpec(
        num_scalar_prefetch=3,
        grid=(nb,),
        in_specs=[pl.BlockSpec((MOE_BLOCK, dp), lambda i, be, nv, nu: (last(i, nu), 0)),
                  pl.BlockSpec((1, d, ff), lambda i, be, nv, nu: (be[last(i, nu)], 0, 0)),
                  pl.BlockSpec((1, d, ff), lambda i, be, nv, nu: (be[last(i, nu)], 0, 0)),
                  pl.BlockSpec((1, ff, d), lambda i, be, nv, nu: (be[last(i, nu)], 0, 0))],
        out_specs=pl.BlockSpec((MOE_BLOCK, dp), lambda i, be, nv, nu: (i, 0)),
    )
    return pl.pallas_call(
        _experts_kernel,
        grid_spec=grid_spec,
        out_shape=jax.ShapeDtypeStruct((p, dp), jnp.int32),
        compiler_params=_params("arbitrary"),
        name="experts",
    )(block_e, n_valid, n_used, xs, wg, wu, wd)


def _combine_kernel(base_ref, mod_ref, w_ref, y_ref, o_ref):
    acc = None
    for kk in range(TOP_K):
        term = w_ref[:, kk:kk + 1] * _unpack_bf16_pairs(y_ref[kk])
        acc = term if acc is None else acc + term
    o_ref[0] = base_ref[0] + mod_ref[0, 5:6, :] * acc


def _combine(base, mod, w_tk, yg, tm):
    b, t, d = base.shape
    nt = t // tm
    tok = pl.BlockSpec((1, tm, d), lambda bi, i: (bi, i, 0))
    return pl.pallas_call(
        _combine_kernel,
        grid=(b, nt),
        in_specs=[tok, pl.BlockSpec((1, 8, d), lambda bi, i: (bi, 0, 0)),
                  pl.BlockSpec((tm, TOP_K), lambda bi, i: (bi * nt + i, 0)),
                  pl.BlockSpec((TOP_K, tm, d // 2), lambda bi, i: (0, bi * nt + i, 0))],
        out_specs=tok,
        out_shape=jax.ShapeDtypeStruct((b, t, d), F32),
        compiler_params=_params("parallel", "parallel"),
        name="combine",
    )(base, mod, w_tk, yg)


def _rope_tables(t):
    pos = np.arange(t)
    half = M_DQK // 2
    nf = half // 2
    inv = np.power(ROPE_BASE, -np.arange(nf, dtype=np.float32) / nf).astype(np.float32)
    row_pos = jnp.asarray((pos // GRID_W).astype(np.float32))
    col_pos = jnp.asarray((pos % GRID_W).astype(np.float32))
    inv = jnp.asarray(inv)
    ar = row_pos[:, None] * inv[None, :]
    ac = col_pos[:, None] * inv[None, :]
    cos = jnp.concatenate([jnp.cos(ar), jnp.cos(ar), jnp.cos(ac), jnp.cos(ac)], axis=1)
    sin = jnp.concatenate([-jnp.sin(ar), jnp.sin(ar), -jnp.sin(ac), jnp.sin(ac)], axis=1)
    return cos, sin


def _arrange_w_in(w_in, b_mgate):
    d = w_in.shape[0]
    sizes = (512, 512, 1024, 1024, 16, 512, 512, 512, 1024, 1024)
    offs = np.concatenate([[0], np.cumsum(sizes)])
    mq, mk, mv, mo, mg, nq, nk, nv, gm, gn = [w_in[:, offs[i]:offs[i + 1]] for i in range(10)]
    pad = jnp.zeros((d, LANES - 2 * M_HEADS), w_in.dtype)
    gi = jnp.concatenate([mg[:, 0:4], mg[:, 8:12], pad], axis=1)
    gf = jnp.concatenate([mg[:, 4:8], mg[:, 12:16], pad], axis=1)
    w_all = jnp.concatenate([mq, mk, mv, mo, gi, gf, nq, nk, nv, gm, gn], axis=1).astype(BF16)
    bpad = jnp.zeros((LANES - 2 * M_HEADS,), F32)
    bg = jnp.stack([jnp.concatenate([b_mgate[0:4], b_mgate[8:12], bpad]),
                    jnp.concatenate([b_mgate[4:8], b_mgate[12:16], bpad])])
    bg = jnp.concatenate([bg, jnp.zeros((6, LANES), F32)], axis=0)
    return w_all, bg


def _segment_mats():
    na_w = NA_HEADS * NA_DH
    seg = np.zeros((na_w, LANES), np.float32)
    seg[np.arange(na_w), np.arange(na_w) // NA_DH] = 1.0
    return jnp.asarray(seg, BF16), jnp.asarray(seg.T.copy(), BF16)


def kernel(x, c, ctx, c_ctx, w_ada, b_ada, w_in, b_mgate, m_norm_w, na_qn_w, na_kn_w, na_rpb,
           w_br_m, w_br_na, w_out, w_router, router_bias, w_exp_gate, w_exp_up, w_exp_down,
           w_sh_gate, w_sh_up, w_sh_down):
    b, t, d = x.shape
    n = b * t
    rows = t // GRID_W
    l = 0

    cc = jnp.concatenate([c, c_ctx[None, :], jnp.zeros((8 - b - 1, d), F32)], axis=0)
    mod = _ada(cc, w_ada[l], b_ada[l])
    mod = mod.reshape(8, 6, d)
    mod = jnp.concatenate([mod, jnp.zeros((8, 2, d), F32)], axis=1)
    mod_x = mod[:b]
    mod_c = jnp.broadcast_to(mod[b:b + 1], (b, 8, d))

    w_all, bg = _arrange_w_in(w_in[l], b_mgate[l])
    seg, segt = _segment_mats()
    qnw = jnp.tile(na_qn_w[l], NA_HEADS)[None, :]
    knw = jnp.tile(na_kn_w[l], NA_HEADS)[None, :]
    tm = min(256, t)

    cp = _inproj(ctx, mod_c, w_all, bg, qnw, knw, seg, segt, None, min(tm, ctx.shape[1]))
    xp = _inproj(x, mod_x, w_all, bg, qnw, knw, seg, segt, _rope_tables(t), tm)
    cmq, cmk, cmv, _, cgi, cgf, _, cnk, cnv, _, _ = cp
    mq, mk, mv, mo, gi, gf, nq, nk, nv, gm, gn = xp

    c0 = jnp.zeros((b, 8, M_DQK, M_DV), F32)
    v0 = jnp.zeros((b, 8, LANES), F32)
    _, _, c1, n1, m1 = _mlstm(cmq, cmk, cmv, cgi, cgf, c0, v0, v0)
    hf, hb, _, _, _ = _mlstm(mq, mk, mv, gi, gf, c1, n1, m1)

    yna = _na(nq, nk, nv, cnk, cnv, _na_bias_table(na_rpb[l], rows))

    h2p, scores_t, base = _post(
        x, mod_x, hf, hb, mo, yna, gm, gn, m_norm_w[l][None, :],
        w_br_m[l].astype(BF16), w_br_na[l].astype(BF16), w_out[l].astype(BF16),
        w_router[l].T.astype(BF16), w_sh_gate[l].astype(BF16), w_sh_up[l].astype(BF16),
        w_sh_down[l].astype(BF16), tm)

    bias_col = jnp.broadcast_to(router_bias[l][:, None], (N_EXPERTS, LANES))
    top_e, top_w, rank, cnt = _route(scores_t, bias_col, min(512, n))

    counts = cnt[:, 0].astype(jnp.int32)
    padded = (counts + MOE_BLOCK - 1) // MOE_BLOCK * MOE_BLOCK
    pend = jnp.cumsum(padded)
    pstart = pend - padded
    nb = -(-(n * TOP_K) // MOE_BLOCK) + N_EXPERTS
    p_rows = nb * MOE_BLOCK
    blk_row0 = jnp.arange(nb, dtype=jnp.int32) * MOE_BLOCK
    block_e = jnp.minimum(jnp.sum((pend[None, :] <= blk_row0[:, None]).astype(jnp.int32), axis=1),
                          N_EXPERTS - 1)
    n_used = pend[-1:] // MOE_BLOCK
    onehot_e = (block_e[:, None] == jnp.arange(N_EXPERTS, dtype=jnp.int32)[None, :]).astype(jnp.int32)
    e_end = jnp.sum(onehot_e * (pstart + counts)[None, :], axis=1)
    n_valid = jnp.clip(e_end - blk_row0, 0, MOE_BLOCK)

    pstart_col = jnp.broadcast_to(pstart.astype(F32)[:, None], (N_EXPERTS, LANES))
    pos = _slots(top_e, rank, pstart_col, min(512, n))
    xs = _dispatch_rows(h2p, pos, p_rows)
    ys = _experts(block_e, n_valid, n_used, xs, w_exp_gate[l], w_exp_up[l], w_exp_down[l])
    yg = _gather_rows(ys, pos.reshape(-1)).reshape(TOP_K, n, d // 2)
    return _combine(base, mod_x, top_w.T, yg, tm)
```

```python
import functools

import numpy as np
import jax
import jax.numpy as jnp
from jax import lax
from jax.experimental import pallas as pl
from jax.experimental.pallas import tpu as pltpu
from jax.experimental.pallas import tpu_sc as plsc

F32 = jnp.float32
BF16 = jnp.bfloat16

EPS = 1e-6
GRID_W = 64
M_HEADS, M_DQK, M_DV = 4, 128, 256
ROPE_BASE = 10000.0
NA_HEADS, NA_DH, NA_KH, NA_KW = 8, 64, 8, 16
N_EXPERTS, TOP_K, N_GROUPS, TOPK_GROUPS = 256, 8, 8, 4
ROUTE_SCALE = 2.5

LANES = 128
VMEM_LIMIT = 56 * 1024 * 1024
NEG = -1e30

MLSTM_CHUNK = 256
NA_ROWS = 4
NA_KEY_ROWS = NA_ROWS + NA_KH - 1
MOE_BLOCK = 512

_W_SEGS = (("mq", 512), ("mk", 512), ("mv", 1024), ("mo", 1024), ("gi", 128), ("gf", 128),
           ("nq", 512), ("nk", 512), ("nv", 512), ("gm", 1024), ("gn", 1024))
_W_OFF = {}
_o = 0
for _n, _w in _W_SEGS:
    _W_OFF[_n] = (_o, _w)
    _o += _w
W_COLS = _o


def _dot(a, b):
    return jnp.dot(a, b, preferred_element_type=F32)


def _dot_nt(a, b):
    return lax.dot_general(a, b, (((1,), (1,)), ((), ())), preferred_element_type=F32)


def _sigmoid(x):
    return 1.0 / (1.0 + jnp.exp(-x))


def _pack_bf16_pairs(v):
    w = v.shape[1] // 2
    bits = pltpu.bitcast(v.astype(BF16).astype(F32), jnp.int32)
    return lax.shift_right_logical(bits[:, :w], 16) | bits[:, w:]


def _unpack_bf16_pairs(p):
    lo = pltpu.bitcast(lax.shift_left(p, 16), F32)
    hi = pltpu.bitcast(p & jnp.int32(-65536), F32)
    return jnp.concatenate([lo, hi], axis=1)


def _params(*sem):
    return pltpu.CompilerParams(dimension_semantics=sem, vmem_limit_bytes=VMEM_LIMIT)


def _resident(shape):
    nd = len(shape)
    return pl.BlockSpec(shape, lambda *_: (0,) * nd, pipeline_mode=pl.Buffered(1))


def _ada_kernel(c_ref, w_ref, b_ref, o_ref):
    c = c_ref[...]
    s = c * _sigmoid(c)
    o_ref[...] = _dot(s.astype(BF16), w_ref[...].astype(BF16)) + b_ref[...]


def _ada(cc, w_ada, b_ada):
    d = cc.shape[1]
    n = w_ada.shape[1]
    return pl.pallas_call(
        _ada_kernel,
        grid=(n // d,),
        in_specs=[pl.BlockSpec((8, d), lambda j: (0, 0)),
                  pl.BlockSpec((d, d), lambda j: (0, j)),
                  pl.BlockSpec((1, d), lambda j: (0, j))],
        out_specs=pl.BlockSpec((8, d), lambda j: (0, j)),
        out_shape=jax.ShapeDtypeStruct((8, n), F32),
        compiler_params=_params("arbitrary"),
        name="ada",
    )(cc, w_ada, b_ada.reshape(1, n))


def _rope_rotate(t, cos, sin):
    lane = lax.broadcasted_iota(jnp.int32, t.shape, 1)
    partner = jnp.where((lane & 32) == 0, pltpu.roll(t, 96, 1), pltpu.roll(t, 32, 1))
    return t * cos + partner * sin


def _inproj_kernel(*refs, rope):
    if rope:
        (x_ref, mod_ref, w_ref, bg_ref, qnw_ref, knw_ref, seg_ref, segt_ref, cos_ref, sin_ref,
         mq_ref, mk_ref, mv_ref, mo_ref, gi_ref, gf_ref, nq_ref, nk_ref, nv_ref, gm_ref, gn_ref) = refs
    else:
        (x_ref, mod_ref, w_ref, bg_ref, qnw_ref, knw_ref, seg_ref, segt_ref,
         mq_ref, mk_ref, mv_ref, mo_ref, gi_ref, gf_ref, nq_ref, nk_ref, nv_ref, gm_ref, gn_ref) = refs
    x = x_ref[0]
    xn = x * lax.rsqrt(jnp.mean(x * x, axis=-1, keepdims=True) + EPS)
    h = xn * (1.0 + mod_ref[0, 1:2, :]) + mod_ref[0, 0:1, :]
    hb = h.astype(BF16)

    def proj(name):
        off, width = _W_OFF[name]
        return _dot(hb, w_ref[:, off:off + width])

    def head_rms(t, w_row, scale):
        ss = _dot((t * t).astype(BF16), seg_ref[...])
        r = lax.rsqrt(ss * (1.0 / NA_DH) + EPS)
        r_hi = r.astype(BF16)
        r_lo = (r - r_hi.astype(F32)).astype(BF16)
        rb = _dot(r_hi, segt_ref[...]) + _dot(r_lo, segt_ref[...])
        return t * rb * w_row * scale

    mq = proj("mq") * (M_DQK ** -0.5)
    mk = proj("mk")
    if rope:
        cos, sin = cos_ref[...], sin_ref[...]
        mq = jnp.concatenate([_rope_rotate(mq[:, i * LANES:(i + 1) * LANES], cos, sin)
                              for i in range(M_HEADS)], axis=1)
        mk = jnp.concatenate([_rope_rotate(mk[:, i * LANES:(i + 1) * LANES], cos, sin)
                              for i in range(M_HEADS)], axis=1)
    mq_ref[0] = mq.astype(BF16)
    mk_ref[0] = mk.astype(BF16)
    mv_ref[0] = proj("mv").astype(BF16)
    mo_ref[0] = _sigmoid(proj("mo")).astype(BF16)
    gi_ref[0] = proj("gi") + bg_ref[0:1, :]
    gf_ref[0] = proj("gf") + bg_ref[1:2, :]
    nq_ref[0] = head_rms(proj("nq"), qnw_ref[...], NA_DH ** -0.5).astype(BF16)
    nk_ref[0] = head_rms(proj("nk"), knw_ref[...], 1.0).astype(BF16)
    nv_ref[0] = proj("nv").astype(BF16)
    gm_ref[0] = _sigmoid(proj("gm")).astype(BF16)
    gn_ref[0] = _sigmoid(proj("gn")).astype(BF16)


def _inproj(x, mod, w_all, bg, qnw, knw, seg, segt, rope_tabs, tm):
    b, t, d = x.shape
    rope = rope_tabs is not None
    tok = lambda w: pl.BlockSpec((1, tm, w), lambda bi, i: (bi, i, 0))
    in_specs = [tok(d),
                pl.BlockSpec((1, 8, d), lambda bi, i: (bi, 0, 0)),
                _resident(w_all.shape), _resident(bg.shape), _resident(qnw.shape),
                _resident(knw.shape), _resident(seg.shape), _resident(segt.shape)]
    args = [x, mod, w_all, bg, qnw, knw, seg, segt]
    if rope:
        in_specs += [pl.BlockSpec((tm, LANES), lambda bi, i: (i, 0))] * 2
        args += list(rope_tabs)
    widths = [("mq", BF16), ("mk", BF16), ("mv", BF16), ("mo", BF16), ("gi", F32), ("gf", F32),
              ("nq", BF16), ("nk", BF16), ("nv", BF16), ("gm", BF16), ("gn", BF16)]
    out_specs = [tok(_W_OFF[n][1]) for n, _ in widths]
    out_shape = [jax.ShapeDtypeStruct((b, t, _W_OFF[n][1]), dt) for n, dt in widths]
    return pl.pallas_call(
        functools.partial(_inproj_kernel, rope=rope),
        grid=(b, t // tm),
        in_specs=in_specs, out_specs=out_specs, out_shape=out_shape,
        compiler_params=_params("parallel", "parallel"),
        name="inproj_rope" if rope else "inproj_ctx",
    )(*args)


def _log_sigmoid(x):
    return jnp.minimum(x, 0.0) - jnp.log(1.0 + jnp.exp(-jnp.abs(x)))


def _mlstm_kernel(qf_ref, kf_ref, vf_ref, gif_ref, gff_ref,
                  qb_ref, kb_ref, vb_ref, gib_ref, gfb_ref,
                  c0_ref, n0_ref, m0_ref,
                  hf_ref, hb_ref, cn_ref, nn_ref, mn_ref,
                  *scratch):
    c_scrs, n_scr, m_scr = scratch[:2 * M_HEADS], scratch[2 * M_HEADS], scratch[2 * M_HEADS + 1]
    step = pl.program_id(1)
    L = qf_ref.shape[1]

    @pl.when(step == 0)
    def _():
        for j, c_scr in enumerate(c_scrs):
            c_scr[...] = c0_ref[0, j]
        n_scr[...] = n0_ref[0]
        m_scr[...] = m0_ref[0]

    n_all = n_scr[...]
    m_all = m_scr[...]
    n_rows, m_rows = [], []
    t_idx = lax.broadcasted_iota(jnp.int32, (L, L), 0)
    s_idx = lax.broadcasted_iota(jnp.int32, (L, L), 1)
    dirs = ((qf_ref, kf_ref, vf_ref, gif_ref, gff_ref, hf_ref, s_idx <= t_idx),
            (qb_ref, kb_ref, vb_ref, gib_ref, gfb_ref, hb_ref, s_idx >= t_idx))
    for d, (q_ref, k_ref, v_ref, gi_ref, gf_ref, h_ref, mask) in enumerate(dirs):
        gi = gi_ref[0]
        ls = _log_sigmoid(gf_ref[0])
        tri = jnp.where(mask, 1.0, 0.0).astype(BF16)
        ls_hi = ls.astype(BF16)
        ls_lo = (ls - ls_hi.astype(F32)).astype(BF16)
        bcum = _dot(tri, ls_hi) + _dot(tri, ls_lo)
        u_t = jnp.transpose(gi - bcum)
        g_row = jnp.sum(ls, axis=0, keepdims=True)
        for hd in range(M_HEADS):
            j = d * M_HEADS + hd
            q = q_ref[0, :, hd * M_DQK:(hd + 1) * M_DQK]
            k = k_ref[0, :, hd * M_DQK:(hd + 1) * M_DQK]
            v = v_ref[0, :, hd * M_DV:(hd + 1) * M_DV]
            b_col = bcum[:, j:j + 1]
            i_col = gi[:, j:j + 1]
            u_row = u_t[j:j + 1, :]
            g = g_row[:, j:j + 1]
            c_scr = c_scrs[j]
            m_prev = m_all[j:j + 1, 0:1]
            n_prev = n_all[j:j + 1, :]
            c_prev = c_scr[...]

            m_loc = jnp.max(jnp.where(mask, u_row, NEG), axis=1, keepdims=True)
            m_row = jnp.maximum(m_loc, m_prev)
            dmat = jnp.exp(jnp.where(mask, u_row - m_row, NEG))
            s = _dot_nt(q, k) * dmat
            w_inter = jnp.exp(m_prev - m_row)
            num = _dot(s.astype(BF16), v) + w_inter * _dot(q, c_prev.astype(BF16))
            den = (jnp.sum(s, axis=1, keepdims=True)
                   + w_inter * jnp.sum(q.astype(F32) * n_prev, axis=1, keepdims=True))
            bound = jnp.exp(-(b_col + m_row))
            h_ref[0, :, hd * M_DV:(hd + 1) * M_DV] = (
                num / jnp.maximum(jnp.abs(den), bound)).astype(h_ref.dtype)

            a_col = g - b_col + i_col
            m_new = jnp.maximum(g + m_prev, jnp.max(a_col, axis=0, keepdims=True))
            decay = jnp.exp(g + m_prev - m_new)
            kw = k.astype(F32) * jnp.exp(a_col - m_new)
            kv = lax.dot_general(kw.astype(BF16), v, (((0,), (0,)), ((), ())),
                                 preferred_element_type=F32)
            c_scr[...] = decay * c_prev + kv
            n_rows.append(decay * n_prev + jnp.sum(kw, axis=0, keepdims=True))
            m_rows.append(jnp.broadcast_to(m_new, (1, LANES)))

    n_scr[...] = jnp.concatenate(n_rows, axis=0)
    m_scr[...] = jnp.concatenate(m_rows, axis=0)

    @pl.when(step == pl.num_programs(1) - 1)
    def _():
        for j, c_scr in enumerate(c_scrs):
            cn_ref[0, j] = c_scr[...]
        nn_ref[0] = n_scr[...]
        mn_ref[0] = m_scr[...]


def _mlstm(q, k, v, gi, gf, c0, n0, m0):
    b, t, _ = q.shape
    L = min(MLSTM_CHUNK, t)
    nc = t // L
    fwd = lambda w: pl.BlockSpec((1, L, w), lambda bi, i: (bi, i, 0))
    bwd = lambda w: pl.BlockSpec((1, L, w), lambda bi, i: (bi, nc - 1 - i, 0))
    st_c = pl.BlockSpec((1, 8, M_DQK, M_DV), lambda bi, i: (bi, 0, 0, 0))
    st_v = pl.BlockSpec((1, 8, LANES), lambda bi, i: (bi, 0, 0))
    qk_w, v_w = M_HEADS * M_DQK, M_HEADS * M_DV
    return pl.pallas_call(
        _mlstm_kernel,
        grid=(b, nc),
        in_specs=[fwd(qk_w), fwd(qk_w), fwd(v_w), fwd(LANES), fwd(LANES),
                  bwd(qk_w), bwd(qk_w), bwd(v_w), bwd(LANES), bwd(LANES),
                  st_c, st_v, st_v],
        out_specs=[fwd(v_w), bwd(v_w), st_c, st_v, st_v],
        out_shape=[jax.ShapeDtypeStruct((b, t, v_w), BF16),
                   jax.ShapeDtypeStruct((b, t, v_w), BF16),
                   jax.ShapeDtypeStruct(c0.shape, F32),
                   jax.ShapeDtypeStruct(n0.shape, F32),
                   jax.ShapeDtypeStruct(m0.shape, F32)],
        scratch_shapes=([pltpu.VMEM((M_DQK, M_DV), F32) for _ in range(2 * M_HEADS)]
                        + [pltpu.VMEM((8, LANES), F32), pltpu.VMEM((8, LANES), F32)]),
        compiler_params=_params("parallel", "arbitrary"),
        name="mlstm",
    )(q, k, v, gi, gf, q, k, v, gi, gf, c0, n0, m0)


def _na_kernel(q_ref, k_ref, v_ref, kc_ref, vc_ref, bias_ref, o_ref, *, rows):
    r0 = pl.program_id(2) * NA_ROWS
    ks = jnp.clip(r0 - NA_KH // 2, 0, rows - NA_KEY_ROWS)
    kstart = pl.multiple_of(ks * GRID_W, GRID_W)
    nkeys = NA_KEY_ROWS * GRID_W
    kblk = k_ref[0, pl.ds(kstart, nkeys), :]
    vblk = v_ref[0, pl.ds(kstart, nkeys), :]
    kc = kc_ref[0]
    vc = vc_ref[0]
    q = q_ref[0]
    lane = lax.broadcasted_iota(jnp.int32, q.shape, 1)
    outs = []
    for hh in range(2):
        in_head = (lane < NA_DH) if hh == 0 else (lane >= NA_DH)
        qm = jnp.where(in_head, q, jnp.zeros_like(q))
        sw = _dot_nt(qm, kblk) + bias_ref[hh, 0]
        sc = _dot_nt(qm, kc)
        m = jnp.maximum(jnp.max(sw, axis=1, keepdims=True), jnp.max(sc, axis=1, keepdims=True))
        ew = jnp.exp(sw - m)
        ec = jnp.exp(sc - m)
        l = jnp.sum(ew, axis=1, keepdims=True) + jnp.sum(ec, axis=1, keepdims=True)
        o = _dot(ew.astype(BF16), vblk) + _dot(ec.astype(BF16), vc)
        outs.append(o / l)
    o_ref[0] = jnp.where(lane < NA_DH, outs[0], outs[1]).astype(o_ref.dtype)


def _na(nq, nk, nv, cnk, cnv, bias):
    b, t, w = nq.shape
    rows = t // GRID_W
    tq = NA_ROWS * GRID_W
    nrb = rows // NA_ROWS
    nctx = cnk.shape[1]
    kind = lambda rb: jnp.where(rb == 0, 0, jnp.where(rb == nrb - 1, 2, 1))
    return pl.pallas_call(
        functools.partial(_na_kernel, rows=rows),
        grid=(b, w // LANES, nrb),
        in_specs=[pl.BlockSpec((1, tq, LANES), lambda bi, hp, rb: (bi, rb, hp)),
                  pl.BlockSpec((1, t, LANES), lambda bi, hp, rb: (bi, 0, hp)),
                  pl.BlockSpec((1, t, LANES), lambda bi, hp, rb: (bi, 0, hp)),
                  pl.BlockSpec((1, nctx, LANES), lambda bi, hp, rb: (bi, 0, hp)),
                  pl.BlockSpec((1, nctx, LANES), lambda bi, hp, rb: (bi, 0, hp)),
                  pl.BlockSpec((2, 1, tq, NA_KEY_ROWS * GRID_W),
                               lambda bi, hp, rb: (hp, kind(rb), 0, 0))],
        out_specs=pl.BlockSpec((1, tq, LANES), lambda bi, hp, rb: (bi, rb, hp)),
        out_shape=jax.ShapeDtypeStruct((b, t, w), BF16),
        compiler_params=_params("parallel", "parallel", "arbitrary"),
        name="na",
    )(nq, nk, nv, cnk, cnv, bias)


def _na_bias_table(na_rpb, rows):
    h = na_rpb.shape[0]
    w = GRID_W
    c = np.arange(w)[:, None]
    kj = np.arange(w)[None, :]
    cs = np.clip(c - NA_KW // 2, 0, w - NA_KW)
    col_valid = (kj >= cs) & (kj < cs + NA_KW)
    dc = np.clip(kj - c + (NA_KW - 1), 0, 2 * NA_KW - 2)
    onehot = np.zeros((2 * NA_KW - 1, w, w), np.float32)
    onehot[dc, np.arange(w)[:, None], np.arange(w)[None, :]] = 1.0
    t2 = jnp.einsum("hrd,dck->hrck", na_rpb, jnp.asarray(onehot), precision=lax.Precision.HIGHEST)
    t2 = jnp.where(jnp.asarray(col_valid)[None, None], t2, NEG)
    t2 = jnp.concatenate([t2, jnp.full((h, 1, w, w), NEG, F32)], axis=1)
    invalid = 2 * NA_KH - 1
    dr_idx = np.full((3, NA_ROWS, NA_KEY_ROWS), invalid, np.int32)
    for kind, r0 in enumerate((0, NA_ROWS, rows - NA_ROWS)):
        ks = int(np.clip(r0 - NA_KH // 2, 0, rows - NA_KEY_ROWS))
        for qa in range(NA_ROWS):
            r = r0 + qa
            rs = int(np.clip(r - NA_KH // 2, 0, rows - NA_KH))
            for kl in range(NA_KEY_ROWS):
                ki = ks + kl
                if rs <= ki < rs + NA_KH:
                    dr_idx[kind, qa, kl] = ki - r + NA_KH - 1
    tab = jnp.take(t2, jnp.asarray(dr_idx.reshape(-1)), axis=1)
    tab = tab.reshape(h, 3, NA_ROWS, NA_KEY_ROWS, w, w).transpose(0, 1, 2, 4, 3, 5)
    return tab.reshape(h, 3, NA_ROWS * w, NA_KEY_ROWS * w)


def _post_kernel(x_ref, mod_ref, hf_ref, hb_ref, mo_ref, na_ref, gm_ref, gn_ref,
                 mnw_ref, wbm_ref, wbn_ref, wout_ref, wr_ref, wsg_ref, wsu_ref, wsd_ref,
                 h2_ref, st_ref, base_ref):
    hm = hf_ref[0].astype(F32) + hb_ref[0].astype(F32)
    parts = []
    for hd in range(M_HEADS):
        t = hm[:, hd * M_DV:(hd + 1) * M_DV]
        parts.append(t * lax.rsqrt(jnp.mean(t * t, axis=-1, keepdims=True) + EPS))
    y_m = jnp.concatenate(parts, axis=1) * mnw_ref[...] * mo_ref[0].astype(F32)
    a = _dot(y_m.astype(BF16), wbm_ref[...])
    bn = _dot(na_ref[0], wbn_ref[...])
    z = gm_ref[0].astype(F32) * a + gn_ref[0].astype(F32) * bn
    y = _dot(z.astype(BF16), wout_ref[...])
    x1 = x_ref[0] + mod_ref[0, 2:3, :] * y
    xn = x1 * lax.rsqrt(jnp.mean(x1 * x1, axis=-1, keepdims=True) + EPS)
    h2f = xn * (1.0 + mod_ref[0, 4:5, :]) + mod_ref[0, 3:4, :]
    h2_ref[...] = _pack_bf16_pairs(h2f)
    h2 = h2f.astype(BF16)
    st_ref[...] = _sigmoid(_dot_nt(wr_ref[...], h2))
    sh = _dot(h2, wsg_ref[...])
    sh = sh * _sigmoid(sh) * _dot(h2, wsu_ref[...])
    base_ref[0] = x1 + mod_ref[0, 5:6, :] * _dot(sh.astype(BF16), wsd_ref[...])


def _post(x, mod, hf, hb, mo, yna, gm, gn, mnw, wbm, wbn, wout, wr_t, wsg, wsu, wsd, tm):
    b, t, d = x.shape
    nt = t // tm
    tok = lambda w: pl.BlockSpec((1, tm, w), lambda bi, i: (bi, i, 0))
    res = [mnw, wbm, wbn, wout, wr_t, wsg, wsu, wsd]
    return pl.pallas_call(
        _post_kernel,
        grid=(b, nt),
        in_specs=[tok(d), pl.BlockSpec((1, 8, d), lambda bi, i: (bi, 0, 0)),
                  tok(hf.shape[2]), tok(hb.shape[2]), tok(mo.shape[2]), tok(yna.shape[2]),
                  tok(gm.shape[2]), tok(gn.shape[2])] + [_resident(a.shape) for a in res],
        out_specs=[pl.BlockSpec((tm, d // 2), lambda bi, i: (bi * nt + i, 0)),
                   pl.BlockSpec((N_EXPERTS, tm), lambda bi, i: (0, bi * nt + i)),
                   tok(d)],
        out_shape=[jax.ShapeDtypeStruct((b * t, d // 2), jnp.int32),
                   jax.ShapeDtypeStruct((N_EXPERTS, b * t), F32),
                   jax.ShapeDtypeStruct((b, t, d), F32)],
        compiler_params=_params("parallel", "parallel"),
        name="post",
    )(x, mod, hf, hb, mo, yna, gm, gn, *res)


def _route_kernel(s_ref, b_ref, e_ref, w_ref, r_ref, cnt_ref, run_scr):
    @pl.when(pl.program_id(0) == 0)
    def _():
        run_scr[...] = jnp.zeros_like(run_scr)

    s = s_ref[...]
    tm = s.shape[1]
    sel = s + b_ref[...][:, 0:1]
    gsz = N_EXPERTS // N_GROUPS
    ninf = -jnp.inf

    x3 = sel.reshape(N_GROUPS, gsz, tm)
    r3 = lax.broadcasted_iota(jnp.int32, x3.shape, 1)
    m1 = jnp.max(x3, axis=1, keepdims=True)
    i1 = jnp.min(jnp.where(x3 == m1, r3, gsz), axis=1, keepdims=True)
    m2 = jnp.max(jnp.where(r3 == i1, ninf, x3), axis=1)
    gs = m1[:, 0, :] + m2

    gidx = lax.broadcasted_iota(jnp.int32, gs.shape, 0)
    gkeep = jnp.zeros(gs.shape, jnp.bool_)
    cur = gs
    for _ in range(TOPK_GROUPS):
        mm = jnp.max(cur, axis=0, keepdims=True)
        ii = jnp.min(jnp.where(cur == mm, gidx, N_GROUPS), axis=0, keepdims=True)
        hit = gidx == ii
        gkeep = jnp.logical_or(gkeep, hit)
        cur = jnp.where(hit, ninf, cur)
    keep = jnp.broadcast_to(gkeep[:, None, :], x3.shape).reshape(N_EXPERTS, tm)

    row = lax.broadcasted_iota(jnp.int32, s.shape, 0)
    cur = jnp.where(keep, sel, ninf)
    idxs, ws = [], []
    chosen = jnp.zeros(s.shape, jnp.bool_)
    for _ in range(TOP_K):
        mm = jnp.max(cur, axis=0, keepdims=True)
        ii = jnp.min(jnp.where(cur == mm, row, N_EXPERTS), axis=0, keepdims=True)
        hit = row == ii
        idxs.append(ii)
        ws.append(jnp.sum(jnp.where(hit, s, 0.0), axis=0, keepdims=True))
        chosen = jnp.logical_or(chosen, hit)
        cur = jnp.where(hit, ninf, cur)
    wsum = ws[0]
    for wk in ws[1:]:
        wsum = wsum + wk

    chosen_f = jnp.where(chosen, 1.0, 0.0)
    tp = lax.broadcasted_iota(jnp.int32, (tm, tm), 0)
    tc = lax.broadcasted_iota(jnp.int32, (tm, tm), 1)
    before = jnp.where(tp < tc, 1.0, 0.0).astype(BF16)
    rank = _dot(chosen_f.astype(BF16), before) + run_scr[...][:, 0:1]
    run_scr[...] = run_scr[...] + jnp.sum(chosen_f, axis=1, keepdims=True)
    cnt_ref[...] = run_scr[...]

    for kk in range(TOP_K):
        e_ref[kk:kk + 1, :] = idxs[kk]
        w_ref[kk:kk + 1, :] = ws[kk] / wsum * ROUTE_SCALE
        r_ref[kk:kk + 1, :] = jnp.sum(jnp.where(row == idxs[kk], rank, 0.0), axis=0,
                                      keepdims=True).astype(jnp.int32)


def _route(scores_t, bias, tm):
    e, n = scores_t.shape
    blk = lambda: pl.BlockSpec((TOP_K, tm), lambda i: (0, i))
    return pl.pallas_call(
        _route_kernel,
        grid=(n // tm,),
        in_specs=[pl.BlockSpec((e, tm), lambda i: (0, i)), _resident(bias.shape)],
        out_specs=[blk(), blk(), blk(), pl.BlockSpec((e, LANES), lambda i: (0, 0))],
        out_shape=[jax.ShapeDtypeStruct((TOP_K, n), jnp.int32),
                   jax.ShapeDtypeStruct((TOP_K, n), F32),
                   jax.ShapeDtypeStruct((TOP_K, n), jnp.int32),
                   jax.ShapeDtypeStruct((e, LANES), F32)],
        scratch_shapes=[pltpu.VMEM((e, LANES), F32)],
        compiler_params=_params("arbitrary"),
        name="route",
    )(scores_t, bias)


def _slot_kernel(e_ref, r_ref, ps_ref, o_ref):
    row = lax.broadcasted_iota(jnp.int32, (N_EXPERTS, e_ref.shape[1]), 0)
    ps = ps_ref[...][:, 0:1]
    for kk in range(TOP_K):
        first = jnp.sum(jnp.where(row == e_ref[kk:kk + 1, :], ps, 0.0), axis=0, keepdims=True)
        o_ref[kk:kk + 1, :] = first.astype(jnp.int32) + r_ref[kk:kk + 1, :]


def _slots(top_e, rank, pstart, tm):
    k, n = top_e.shape
    blk = pl.BlockSpec((k, tm), lambda i: (0, i))
    return pl.pallas_call(
        _slot_kernel,
        grid=(n // tm,),
        in_specs=[blk, blk, _resident(pstart.shape)],
        out_specs=blk,
        out_shape=jax.ShapeDtypeStruct((k, n), jnp.int32),
        compiler_params=_params("parallel"),
        name="slots",
    )(top_e, rank, pstart)


SC_WINDOW = 128


def _sc_mesh():
    return plsc.VectorSubcoreMesh(core_axis_name="core", subcore_axis_name="subcore")


def _sc_workers():
    info = plsc.get_sparse_core_info()
    return info.num_cores, info.num_cores * info.num_subcores


def _dispatch_rows(x, pos, p_rows):
    n, w = x.shape
    kk = pos.shape[0]
    ncores, nw = _sc_workers()
    steps = n // nw // SC_WINDOW
    pos4 = pos.reshape(kk, nw, steps, SC_WINDOW).transpose(1, 2, 0, 3)

    @functools.partial(
        pl.kernel, mesh=_sc_mesh(),
        out_type=jax.ShapeDtypeStruct((p_rows, w), x.dtype),
        scratch_types=[pltpu.VMEM((kk, SC_WINDOW), jnp.int32),
                       pltpu.VMEM((SC_WINDOW, w), x.dtype),
                       pltpu.SemaphoreType.DMA],
    )
    def scatter(x_hbm, i_hbm, o_hbm, idx_v, rows_v, sem):
        wid = lax.axis_index("subcore") * ncores + lax.axis_index("core")

        @pl.loop(0, steps)
        def _(s):
            base = pl.multiple_of((wid * steps + s) * SC_WINDOW, SC_WINDOW)
            pltpu.sync_copy(i_hbm.at[wid, s], idx_v)
            pltpu.sync_copy(x_hbm.at[pl.ds(base, SC_WINDOW)], rows_v)
            copies = [pltpu.make_async_copy(rows_v, o_hbm.at[idx_v.at[j]], sem) for j in range(kk)]
            for cp in copies:
                cp.start()
            for cp in copies:
                cp.wait()

    return scatter(x, pos4)


def _gather_rows(x, idx):
    m = idx.shape[0]
    w = x.shape[1]
    ncores, nw = _sc_workers()
    steps = m // nw // SC_WINDOW
    idx3 = idx.reshape(nw, steps, SC_WINDOW)

    @functools.partial(
        pl.kernel, mesh=_sc_mesh(),
        out_type=jax.ShapeDtypeStruct((m, w), x.dtype),
        scratch_types=[pltpu.VMEM((steps, SC_WINDOW), jnp.int32),
                       pltpu.VMEM((SC_WINDOW, w), x.dtype),
                       pltpu.SemaphoreType.DMA],
    )
    def gather(x_hbm, i_hbm, o_hbm, idx_v, rows_v, sem):
        wid = lax.axis_index("subcore") * ncores + lax.axis_index("core")
        pltpu.sync_copy(i_hbm.at[wid], idx_v)

        @pl.loop(0, steps)
        def _(s):
            pltpu.async_copy(x_hbm.at[idx_v.at[s]], rows_v, sem).wait()
            base = pl.multiple_of((wid * steps + s) * SC_WINDOW, SC_WINDOW)
            pltpu.sync_copy(rows_v, o_hbm.at[pl.ds(base, SC_WINDOW)])

    return gather(x, idx3)


def _experts_kernel(be_ref, nv_ref, nu_ref, x_ref, wg_ref, wu_ref, wd_ref, o_ref):
    i = pl.program_id(0)

    @pl.when(i < nu_ref[0])
    def _():
        rid = lax.broadcasted_iota(jnp.int32, x_ref.shape, 0)
        xp = jnp.where(rid < nv_ref[i], x_ref[...], 0)
        x = _unpack_bf16_pairs(xp).astype(BF16)
        g = _dot(x, wg_ref[0].astype(BF16))
        u = _dot(x, wu_ref[0].astype(BF16))
        a = (g * _sigmoid(g) * u).astype(BF16)
        o_ref[...] = _pack_bf16_pairs(_dot(a, wd_ref[0].astype(BF16)))

    @pl.when(i >= nu_ref[0])
    def _():
        o_ref[...] = jnp.zeros_like(o_ref)


def _experts(block_e, n_valid, n_used, xs, wg, wu, wd):
    p, dp = xs.shape
    nb = p // MOE_BLOCK
    d, ff = wg.shape[1], wg.shape[2]
    last = lambda i, nu: jnp.minimum(i, nu[0] - 1)
    grid_spec = pltpu.PrefetchScalarGridSpec(
        num_scalar_prefetch=3,
        grid=(nb,),
        in_specs=[pl.BlockSpec((MOE_BLOCK, dp), lambda i, be, nv, nu: (last(i, nu), 0)),
                  pl.BlockSpec((1, d, ff), lambda i, be, nv, nu: (be[last(i, nu)], 0, 0)),
                  pl.BlockSpec((1, d, ff), lambda i, be, nv, nu: (be[last(i, nu)], 0, 0)),
                  pl.BlockSpec((1, ff, d), lambda i, be, nv, nu: (be[last(i, nu)], 0, 0))],
        out_specs=pl.BlockSpec((MOE_BLOCK, dp), lambda i, be, nv, nu: (i, 0)),
    )
    return pl.pallas_call(
        _experts_kernel,
        grid_spec=grid_spec,
        out_shape=jax.ShapeDtypeStruct((p, dp), jnp.int32),
        compiler_params=_params("arbitrary"),
        name="experts",
    )(block_e, n_valid, n_used, xs, wg, wu, wd)


def _combine_kernel(base_ref, mod_ref, w_ref, y_ref, o_ref):
    acc = None
    for kk in range(TOP_K):
        term = w_ref[:, kk:kk + 1] * _unpack_bf16_pairs(y_ref[kk])
        acc = term if acc is None else acc + term
    o_ref[0] = base_ref[0] + mod_ref[0, 5:6, :] * acc


def _combine(base, mod, w_tk, yg, tm):
    b, t, d = base.shape
    nt = t // tm
    tok = pl.BlockSpec((1, tm, d), lambda bi, i: (bi, i, 0))
    return pl.pallas_call(
        _combine_kernel,
        grid=(b, nt),
        in_specs=[tok, pl.BlockSpec((1, 8, d), lambda bi, i: (bi, 0, 0)),
                  pl.BlockSpec((tm, TOP_K), lambda bi, i: (bi * nt + i, 0)),
                  pl.BlockSpec((TOP_K, tm, d // 2), lambda bi, i: (0, bi * nt + i, 0))],
        out_specs=tok,
        out_shape=jax.ShapeDtypeStruct((b, t, d), F32),
        compiler_params=_params("parallel", "parallel"),
        name="combine",
    )(base, mod, w_tk, yg)


def _rope_tables(t):
    pos = np.arange(t)
    half = M_DQK // 2
    nf = half // 2
    inv = np.power(ROPE_BASE, -np.arange(nf, dtype=np.float32) / nf).astype(np.float32)
    row_pos = jnp.asarray((pos // GRID_W).astype(np.float32))
    col_pos = jnp.asarray((pos % GRID_W).astype(np.float32))
    inv = jnp.asarray(inv)
    ar = row_pos[:, None] * inv[None, :]
    ac = col_pos[:, None] * inv[None, :]
    cos = jnp.concatenate([jnp.cos(ar), jnp.cos(ar), jnp.cos(ac), jnp.cos(ac)], axis=1)
    sin = jnp.concatenate([-jnp.sin(ar), jnp.sin(ar), -jnp.sin(ac), jnp.sin(ac)], axis=1)
    return cos, sin


def _arrange_w_in(w_in, b_mgate):
    d = w_in.shape[0]
    sizes = (512, 512, 1024, 1024, 16, 512, 512, 512, 1024, 1024)
    offs = np.concatenate([[0], np.cumsum(sizes)])
    mq, mk, mv, mo, mg, nq, nk, nv, gm, gn = [w_in[:, offs[i]:offs[i + 1]] for i in range(10)]
    pad = jnp.zeros((d, LANES - 2 * M_HEADS), w_in.dtype)
    gi = jnp.concatenate([mg[:, 0:4], mg[:, 8:12], pad], axis=1)
    gf = jnp.concatenate([mg[:, 4:8], mg[:, 12:16], pad], axis=1)
    w_all = jnp.concatenate([mq, mk, mv, mo, gi, gf, nq, nk, nv, gm, gn], axis=1).astype(BF16)
    bpad = jnp.zeros((LANES - 2 * M_HEADS,), F32)
    bg = jnp.stack([jnp.concatenate([b_mgate[0:4], b_mgate[8:12], bpad]),
                    jnp.concatenate([b_mgate[4:8], b_mgate[12:16], bpad])])
    bg = jnp.concatenate([bg, jnp.zeros((6, LANES), F32)], axis=0)
    return w_all, bg


def _segment_mats():
    na_w = NA_HEADS * NA_DH
    seg = np.zeros((na_w, LANES), np.float32)
    seg[np.arange(na_w), np.arange(na_w) // NA_DH] = 1.0
    return jnp.asarray(seg, BF16), jnp.asarray(seg.T.copy(), BF16)


def kernel(x, c, ctx, c_ctx, w_ada, b_ada, w_in, b_mgate, m_norm_w, na_qn_w, na_kn_w, na_rpb,
           w_br_m, w_br_na, w_out, w_router, router_bias, w_exp_gate, w_exp_up, w_exp_down,
           w_sh_gate, w_sh_up, w_sh_down):
    b, t, d = x.shape
    n = b * t
    rows = t // GRID_W
    l = 0

    cc = jnp.concatenate([c, c_ctx[None, :], jnp.zeros((8 - b - 1, d), F32)], axis=0)
    mod = _ada(cc, w_ada[l], b_ada[l])
    mod = mod.reshape(8, 6, d)
    mod = jnp.concatenate([mod, jnp.zeros((8, 2, d), F32)], axis=1)
    mod_x = mod[:b]
    mod_c = jnp.broadcast_to(mod[b:b + 1], (b, 8, d))

    w_all, bg = _arrange_w_in(w_in[l], b_mgate[l])
    seg, segt = _segment_mats()
    qnw = jnp.tile(na_qn_w[l], NA_HEADS)[None, :]
    knw = jnp.tile(na_kn_w[l], NA_HEADS)[None, :]
    tm = min(512, t)

    cp = _inproj(ctx, mod_c, w_all, bg, qnw, knw, seg, segt, None, min(tm, ctx.shape[1]))
    xp = _inproj(x, mod_x, w_all, bg, qnw, knw, seg, segt, _rope_tables(t), tm)
    cmq, cmk, cmv, _, cgi, cgf, _, cnk, cnv, _, _ = cp
    mq, mk, mv, mo, gi, gf, nq, nk, nv, gm, gn = xp

    c0 = jnp.zeros((b, 8, M_DQK, M_DV), F32)
    v0 = jnp.zeros((b, 8, LANES), F32)
    _, _, c1, n1, m1 = _mlstm(cmq, cmk, cmv, cgi, cgf, c0, v0, v0)
    hf, hb, _, _, _ = _mlstm(mq, mk, mv, gi, gf, c1, n1, m1)

    yna = _na(nq, nk, nv, cnk, cnv, _na_bias_table(na_rpb[l], rows))

    h2p, scores_t, base = _post(
        x, mod_x, hf, hb, mo, yna, gm, gn, m_norm_w[l][None, :],
        w_br_m[l].astype(BF16), w_br_na[l].astype(BF16), w_out[l].astype(BF16),
        w_router[l].T.astype(BF16), w_sh_gate[l].astype(BF16), w_sh_up[l].astype(BF16),
        w_sh_down[l].astype(BF16), tm)

    bias_col = jnp.broadcast_to(router_bias[l][:, None], (N_EXPERTS, LANES))
    top_e, top_w, rank, cnt = _route(scores_t, bias_col, min(512, n))

    counts = cnt[:, 0].astype(jnp.int32)
    padded = (counts + MOE_BLOCK - 1) // MOE_BLOCK * MOE_BLOCK
    pend = jnp.cumsum(padded)
    pstart = pend - padded
    nb = -(-(n * TOP_K) // MOE_BLOCK) + N_EXPERTS
    p_rows = nb * MOE_BLOCK
    blk_row0 = jnp.arange(nb, dtype=jnp.int32) * MOE_BLOCK
    block_e = jnp.minimum(jnp.sum((pend[None, :] <= blk_row0[:, None]).astype(jnp.int32), axis=1),
                          N_EXPERTS - 1)
    n_used = pend[-1:] // MOE_BLOCK
    onehot_e = (block_e[:, None] == jnp.arange(N_EXPERTS, dtype=jnp.int32)[None, :]).astype(jnp.int32)
    e_end = jnp.sum(onehot_e * (pstart + counts)[None, :], axis=1)
    n_valid = jnp.clip(e_end - blk_row0, 0, MOE_BLOCK)

    pstart_col = jnp.broadcast_to(pstart.astype(F32)[:, None], (N_EXPERTS, LANES))
    pos = _slots(top_e, rank, pstart_col, min(512, n))
    xs = _dispatch_rows(h2p, pos, p_rows)
    ys = _experts(block_e, n_valid, n_used, xs, w_exp_gate[l], w_exp_up[l], w_exp_down[l])
    yg = _gather_rows(ys, pos.reshape(-1)).reshape(TOP_K, n, d // 2)
    return _combine(base, mod_x, top_w.T, yg, tm)
```

```python
import functools

import numpy as np
import jax
import jax.numpy as jnp
from jax import lax
from jax.experimental import pallas as pl
from jax.experimental.pallas import tpu as pltpu
from jax.experimental.pallas import tpu_sc as plsc

F32 = jnp.float32
BF16 = jnp.bfloat16

EPS = 1e-6
GRID_W = 64
M_HEADS, M_DQK, M_DV = 4, 128, 256
ROPE_BASE = 10000.0
NA_HEADS, NA_DH, NA_KH, NA_KW = 8, 64, 8, 16
N_EXPERTS, TOP_K, N_GROUPS, TOPK_GROUPS = 256, 8, 8, 4
ROUTE_SCALE = 2.5

LANES = 128
VMEM_LIMIT = 56 * 1024 * 1024
NEG = -1e30

MLSTM_CHUNK = 256
NA_ROWS = 4
NA_KEY_ROWS = NA_ROWS + NA_KH - 1
MOE_BLOCK = 512

_W_SEGS = (("mq", 512), ("mk", 512), ("mv", 1024), ("mo", 1024), ("gf", 128),
           ("nq", 512), ("nk", 512), ("nv", 512), ("gm", 1024), ("gn", 1024))
_W_OFF = {}
_o = 0
for _n, _w in _W_SEGS:
    _W_OFF[_n] = (_o, _w)
    _o += _w
W_COLS = _o


def _dot(a, b):
    return jnp.dot(a, b, preferred_element_type=F32)


def _dot_nt(a, b):
    return lax.dot_general(a, b, (((1,), (1,)), ((), ())), preferred_element_type=F32)


def _sigmoid(x):
    return 1.0 / (1.0 + jnp.exp(-x))


def _pack_bf16_pairs(v):
    w = v.shape[1] // 2
    bits = pltpu.bitcast(v.astype(BF16).astype(F32), jnp.int32)
    return lax.shift_right_logical(bits[:, :w], 16) | bits[:, w:]


def _unpack_bf16_pairs(p):
    lo = pltpu.bitcast(lax.shift_left(p, 16), F32)
    hi = pltpu.bitcast(p & jnp.int32(-65536), F32)
    return jnp.concatenate([lo, hi], axis=1)


def _params(*sem):
    return pltpu.CompilerParams(dimension_semantics=sem, vmem_limit_bytes=VMEM_LIMIT)


def _resident(shape):
    nd = len(shape)
    return pl.BlockSpec(shape, lambda *_: (0,) * nd, pipeline_mode=pl.Buffered(1))


def _ada_kernel(c_ref, w_ref, b_ref, o_ref):
    c = c_ref[...]
    s = c * _sigmoid(c)
    o_ref[...] = _dot(s.astype(BF16), w_ref[...].astype(BF16)) + b_ref[...]


def _ada(cc, w_ada, b_ada):
    d = cc.shape[1]
    n = w_ada.shape[1]
    return pl.pallas_call(
        _ada_kernel,
        grid=(n // d,),
        in_specs=[pl.BlockSpec((8, d), lambda j: (0, 0)),
                  pl.BlockSpec((d, d), lambda j: (0, j)),
                  pl.BlockSpec((1, d), lambda j: (0, j))],
        out_specs=pl.BlockSpec((8, d), lambda j: (0, j)),
        out_shape=jax.ShapeDtypeStruct((8, n), F32),
        compiler_params=_params("arbitrary"),
        name="ada",
    )(cc, w_ada, b_ada.reshape(1, n))


def _rope_rotate(t, cos, sin):
    lane = lax.broadcasted_iota(jnp.int32, t.shape, 1)
    partner = jnp.where((lane & 32) == 0, pltpu.roll(t, 96, 1), pltpu.roll(t, 32, 1))
    return t * cos + partner * sin


def _rope_rotate_t(t, cos, sin):
    q = M_DQK // 4
    partner = jnp.concatenate([t[q:2 * q], t[0:q], t[3 * q:4 * q], t[2 * q:3 * q]], axis=0)
    return t * cos + partner * sin


def _inproj_kernel(*refs, rope):
    if rope:
        (x_ref, mod_ref, w_ref, wt_ref, bg_ref, bgt_ref, qnw_ref, knw_ref, seg_ref, segt_ref,
         cos_ref, sin_ref, cost_ref, sint_ref,
         mq_ref, mk_ref, mv_ref, mo_ref, gf_ref, nq_ref, nk_ref, nv_ref, gm_ref, gn_ref,
         mkt_ref, git_ref, gft_ref) = refs
    else:
        (x_ref, mod_ref, w_ref, wt_ref, bg_ref, bgt_ref, qnw_ref, knw_ref, seg_ref, segt_ref,
         mq_ref, mk_ref, mv_ref, mo_ref, gf_ref, nq_ref, nk_ref, nv_ref, gm_ref, gn_ref,
         mkt_ref, git_ref, gft_ref) = refs
    x = x_ref[0]
    xn = x * lax.rsqrt(jnp.mean(x * x, axis=-1, keepdims=True) + EPS)
    h = xn * (1.0 + mod_ref[0, 1:2, :]) + mod_ref[0, 0:1, :]
    hb = h.astype(BF16)

    def proj(name):
        off, width = _W_OFF[name]
        return _dot(hb, w_ref[:, off:off + width])

    def head_rms(t, w_row, scale):
        ss = _dot((t * t).astype(BF16), seg_ref[...])
        r = lax.rsqrt(ss * (1.0 / NA_DH) + EPS)
        r_hi = r.astype(BF16)
        r_lo = (r - r_hi.astype(F32)).astype(BF16)
        rb = _dot(r_hi, segt_ref[...]) + _dot(r_lo, segt_ref[...])
        return t * rb * w_row * scale

    mq = proj("mq") * (M_DQK ** -0.5)
    mk = proj("mk")
    if rope:
        cos, sin = cos_ref[...], sin_ref[...]
        mq = jnp.concatenate([_rope_rotate(mq[:, i * LANES:(i + 1) * LANES], cos, sin)
                              for i in range(M_HEADS)], axis=1)
        mk = jnp.concatenate([_rope_rotate(mk[:, i * LANES:(i + 1) * LANES], cos, sin)
                              for i in range(M_HEADS)], axis=1)
    mq_ref[0] = mq.astype(BF16)
    mk_ref[0] = mk.astype(BF16)
    mv_ref[0] = proj("mv").astype(BF16)
    mo_ref[0] = _sigmoid(proj("mo")).astype(BF16)
    gf_ref[0] = proj("gf") + bg_ref[0:1, :]
    nq_ref[0] = head_rms(proj("nq"), qnw_ref[...], NA_DH ** -0.5).astype(BF16)
    nk_ref[0] = head_rms(proj("nk"), knw_ref[...], 1.0).astype(BF16)
    nv_ref[0] = proj("nv").astype(BF16)
    gm_ref[0] = _sigmoid(proj("gm")).astype(BF16)
    gn_ref[0] = _sigmoid(proj("gn")).astype(BF16)

    qk_w = M_HEADS * M_DQK
    mkt = _dot_nt(wt_ref[0:qk_w, :], hb)
    if rope:
        cost, sint = cost_ref[...], sint_ref[...]
        mkt = jnp.concatenate([_rope_rotate_t(mkt[i * M_DQK:(i + 1) * M_DQK], cost, sint)
                               for i in range(M_HEADS)], axis=0)
    mkt_ref[0] = mkt.astype(BF16)
    git_ref[0] = _dot_nt(wt_ref[qk_w:qk_w + LANES, :], hb) + bgt_ref[0:LANES, :]
    gft_ref[0] = _dot_nt(wt_ref[qk_w + LANES:qk_w + 2 * LANES, :], hb) + bgt_ref[LANES:2 * LANES, :]


def _inproj(x, mod, w_all, wt_all, bg, bgt, qnw, knw, seg, segt, rope_tabs, tm):
    b, t, d = x.shape
    rope = rope_tabs is not None
    tok = lambda w: pl.BlockSpec((1, tm, w), lambda bi, i: (bi, i, 0))
    tok_t = lambda w: pl.BlockSpec((1, w, tm), lambda bi, i: (bi, 0, i))
    in_specs = [tok(d),
                pl.BlockSpec((1, 8, d), lambda bi, i: (bi, 0, 0)),
                _resident(w_all.shape), _resident(wt_all.shape), _resident(bg.shape),
                _resident(bgt.shape), _resident(qnw.shape),
                _resident(knw.shape), _resident(seg.shape), _resident(segt.shape)]
    args = [x, mod, w_all, wt_all, bg, bgt, qnw, knw, seg, segt]
    if rope:
        in_specs += [pl.BlockSpec((tm, LANES), lambda bi, i: (i, 0))] * 2
        in_specs += [pl.BlockSpec((LANES, tm), lambda bi, i: (0, i))] * 2
        args += list(rope_tabs)
    widths = [("mq", BF16), ("mk", BF16), ("mv", BF16), ("mo", BF16), ("gf", F32),
              ("nq", BF16), ("nk", BF16), ("nv", BF16), ("gm", BF16), ("gn", BF16)]
    out_specs = [tok(_W_OFF[n][1]) for n, _ in widths]
    out_shape = [jax.ShapeDtypeStruct((b, t, _W_OFF[n][1]), dt) for n, dt in widths]
    out_specs += [tok_t(M_HEADS * M_DQK), tok_t(LANES), tok_t(LANES)]
    out_shape += [jax.ShapeDtypeStruct((b, M_HEADS * M_DQK, t), BF16),
                  jax.ShapeDtypeStruct((b, LANES, t), F32),
                  jax.ShapeDtypeStruct((b, LANES, t), F32)]
    return pl.pallas_call(
        functools.partial(_inproj_kernel, rope=rope),
        grid=(b, t // tm),
        in_specs=in_specs, out_specs=out_specs, out_shape=out_shape,
        compiler_params=_params("parallel", "parallel"),
        name="inproj_rope" if rope else "inproj_ctx",
    )(*args)


def _log_sigmoid(x):
    return jnp.minimum(x, 0.0) - jnp.log(1.0 + jnp.exp(-jnp.abs(x)))


def _dot_split(a, b, split_a):
    x = a if split_a else b
    hi = x.astype(BF16)
    lo = (x - hi.astype(F32)).astype(BF16)
    return (_dot(hi, b) + _dot(lo, b)) if split_a else (_dot(a, hi) + _dot(a, lo))


MLSTM_EXT = M_DV + LANES


def _mlstm_kernel(qf_ref, kf_ref, ktf_ref, vf_ref, gff_ref, gitf_ref, gftf_ref,
                  qb_ref, kb_ref, ktb_ref, vb_ref, gfb_ref, gitb_ref, gftb_ref,
                  c0_ref, m0_ref,
                  hf_ref, hb_ref, cn_ref, mn_ref,
                  *scratch):
    c_scrs, m_scr = scratch[:2 * M_HEADS], scratch[2 * M_HEADS]
    step = pl.program_id(1)
    L = qf_ref.shape[1]
    nu = 2 * M_HEADS

    @pl.when(step == 0)
    def _():
        for j, c_scr in enumerate(c_scrs):
            c_scr[...] = c0_ref[0, j]
        m_scr[...] = m0_ref[0]

    row_i = lax.broadcasted_iota(jnp.int32, (L, L), 0)
    col_i = lax.broadcasted_iota(jnp.int32, (L, L), 1)
    lower = col_i <= row_i
    upper = col_i >= row_i
    tri_lo = jnp.where(lower, 1.0, 0.0).astype(BF16)
    tri_up = jnp.where(upper, 1.0, 0.0).astype(BF16)

    is_f = lax.broadcasted_iota(jnp.int32, (nu, L), 0) < M_HEADS
    gi_t = jnp.where(is_f, gitf_ref[0, 0:nu, :], gitb_ref[0, 0:nu, :])
    ls_tf = _log_sigmoid(gftf_ref[0, 0:nu, :])
    ls_tb = _log_sigmoid(gftb_ref[0, 0:nu, :])
    b_t = jnp.where(is_f, _dot_split(ls_tf, tri_up, True), _dot_split(ls_tb, tri_lo, True))
    u_t = gi_t - b_t
    g_c = jnp.sum(jnp.where(is_f, ls_tf, ls_tb), axis=1, keepdims=True)
    m_prev = m_scr[...]
    a_t = g_c + u_t
    m_new = jnp.maximum(g_c + m_prev, jnp.max(a_t, axis=1, keepdims=True))
    decay = jnp.exp(g_c + m_prev - m_new)
    wa_t = jnp.exp(a_t - jnp.concatenate([m_new] * (L // LANES), axis=1))
    m_scr[...] = m_new

    ones = jnp.ones((L, LANES), BF16)
    dirs = ((qf_ref, kf_ref, ktf_ref, vf_ref, gff_ref, hf_ref, lower, tri_lo),
            (qb_ref, kb_ref, ktb_ref, vb_ref, gfb_ref, hb_ref, upper, tri_up))
    for d, (q_ref, k_ref, kt_ref, v_ref, gf_ref, h_ref, mask, tri) in enumerate(dirs):
        bcum = _dot_split(tri, _log_sigmoid(gf_ref[0]), False)
        for hd in range(M_HEADS):
            j = d * M_HEADS + hd
            c_scr = c_scrs[j]
            q = q_ref[0, :, hd * M_DQK:(hd + 1) * M_DQK]
            k = k_ref[0, :, hd * M_DQK:(hd + 1) * M_DQK]
            k_t = kt_ref[0, hd * M_DQK:(hd + 1) * M_DQK, :]
            v_ext = jnp.concatenate([v_ref[0, :, hd * M_DV:(hd + 1) * M_DV], ones], axis=1)
            u_row = u_t[j:j + 1, :]
            mp_row = m_prev[j:j + 1, :]
            c_prev = c_scr[...]

            m_loc = jnp.max(jnp.where(mask, u_row, NEG), axis=1, keepdims=True)
            m_rep = jnp.maximum(jnp.broadcast_to(m_loc, (L, LANES)), mp_row)
            m_wide = jnp.concatenate([m_rep] * (L // LANES), axis=1)
            dmat = jnp.exp(jnp.where(mask, u_row - m_wide, NEG))
            s = (_dot_nt(q, k) * dmat).astype(BF16)
            qw = (q.astype(F32) * jnp.exp(mp_row - m_rep)).astype(BF16)
            r = _dot(s, v_ext) + _dot(qw, c_prev.astype(BF16))
            b_rep = jnp.broadcast_to(bcum[:, j:j + 1], (L, LANES))
            dn = jnp.maximum(jnp.abs(r[:, M_DV:]), jnp.exp(-(b_rep + m_rep)))
            h_ref[0, :, hd * M_DV:(hd + 1) * M_DV] = (
                r[:, :M_DV] / jnp.concatenate([dn] * (M_DV // LANES), axis=1)).astype(h_ref.dtype)

            kw = (k_t.astype(F32) * wa_t[j:j + 1, :]).astype(BF16)
            dec = jnp.concatenate([decay[j:j + 1, :]] * (MLSTM_EXT // LANES), axis=1)
            c_scr[...] = dec * c_prev + _dot(kw, v_ext)

    @pl.when(step == pl.num_programs(1) - 1)
    def _():
        for j, c_scr in enumerate(c_scrs):
            cn_ref[0, j] = c_scr[...]
        mn_ref[0] = m_scr[...]


def _mlstm(q, k, kt, v, gf, git, gft, c0, m0):
    b, t, _ = q.shape
    L = min(MLSTM_CHUNK, t)
    nc = t // L
    fwd = lambda w: pl.BlockSpec((1, L, w), lambda bi, i: (bi, i, 0))
    bwd = lambda w: pl.BlockSpec((1, L, w), lambda bi, i: (bi, nc - 1 - i, 0))
    fwd_t = lambda w: pl.BlockSpec((1, w, L), lambda bi, i: (bi, 0, i))
    bwd_t = lambda w: pl.BlockSpec((1, w, L), lambda bi, i: (bi, 0, nc - 1 - i))
    st_c = pl.BlockSpec((1, 8, M_DQK, MLSTM_EXT), lambda bi, i: (bi, 0, 0, 0))
    st_v = pl.BlockSpec((1, 8, LANES), lambda bi, i: (bi, 0, 0))
    qk_w, v_w = M_HEADS * M_DQK, M_HEADS * M_DV
    return pl.pallas_call(
        _mlstm_kernel,
        grid=(b, nc),
        in_specs=[fwd(qk_w), fwd(qk_w), fwd_t(qk_w), fwd(v_w), fwd(LANES),
                  fwd_t(LANES), fwd_t(LANES),
                  bwd(qk_w), bwd(qk_w), bwd_t(qk_w), bwd(v_w), bwd(LANES),
                  bwd_t(LANES), bwd_t(LANES),
                  st_c, st_v],
        out_specs=[fwd(v_w), bwd(v_w), st_c, st_v],
        out_shape=[jax.ShapeDtypeStruct((b, t, v_w), BF16),
                   jax.ShapeDtypeStruct((b, t, v_w), BF16),
                   jax.ShapeDtypeStruct(c0.shape, F32),
                   jax.ShapeDtypeStruct(m0.shape, F32)],
        scratch_shapes=([pltpu.VMEM((M_DQK, MLSTM_EXT), F32) for _ in range(2 * M_HEADS)]
                        + [pltpu.VMEM((8, LANES), F32)]),
        compiler_params=_params("parallel", "arbitrary"),
        name="mlstm",
    )(q, k, kt, v, gf, git, gft, q, k, kt, v, gf, git, gft, c0, m0)


def _na_kernel(q_ref, k_ref, v_ref, kc_ref, vc_ref, bias_ref, o_ref, *, rows):
    r0 = pl.program_id(2) * NA_ROWS
    ks = jnp.clip(r0 - NA_KH // 2, 0, rows - NA_KEY_ROWS)
    kstart = pl.multiple_of(ks * GRID_W, GRID_W)
    nkeys = NA_KEY_ROWS * GRID_W
    kblk = k_ref[0, pl.ds(kstart, nkeys), :]
    vblk = v_ref[0, pl.ds(kstart, nkeys), :]
    kc = kc_ref[0]
    vc = vc_ref[0]
    q = q_ref[0]
    lane = lax.broadcasted_iota(jnp.int32, q.shape, 1)
    outs = []
    for hh in range(2):
        in_head = (lane < NA_DH) if hh == 0 else (lane >= NA_DH)
        qm = jnp.where(in_head, q, jnp.zeros_like(q))
        sw = _dot_nt(qm, kblk) + bias_ref[hh, 0]
        sc = _dot_nt(qm, kc)
        m = jnp.maximum(jnp.max(sw, axis=1, keepdims=True), jnp.max(sc, axis=1, keepdims=True))
        ew = jnp.exp(sw - m)
        ec = jnp.exp(sc - m)
        l = jnp.sum(ew, axis=1, keepdims=True) + jnp.sum(ec, axis=1, keepdims=True)
        o = _dot(ew.astype(BF16), vblk) + _dot(ec.astype(BF16), vc)
        outs.append(o / l)
    o_ref[0] = jnp.where(lane < NA_DH, outs[0], outs[1]).astype(o_ref.dtype)


def _na(nq, nk, nv, cnk, cnv, bias):
    b, t, w = nq.shape
    rows = t // GRID_W
    tq = NA_ROWS * GRID_W
    nrb = rows // NA_ROWS
    nctx = cnk.shape[1]
    kind = lambda rb: jnp.where(rb == 0, 0, jnp.where(rb == nrb - 1, 2, 1))
    return pl.pallas_call(
        functools.partial(_na_kernel, rows=rows),
        grid=(b, w // LANES, nrb),
        in_specs=[pl.BlockSpec((1, tq, LANES), lambda bi, hp, rb: (bi, rb, hp)),
                  pl.BlockSpec((1, t, LANES), lambda bi, hp, rb: (bi, 0, hp)),
                  pl.BlockSpec((1, t, LANES), lambda bi, hp, rb: (bi, 0, hp)),
                  pl.BlockSpec((1, nctx, LANES), lambda bi, hp, rb: (bi, 0, hp)),
                  pl.BlockSpec((1, nctx, LANES), lambda bi, hp, rb: (bi, 0, hp)),
                  pl.BlockSpec((2, 1, tq, NA_KEY_ROWS * GRID_W),
                               lambda bi, hp, rb: (hp, kind(rb), 0, 0))],
        out_specs=pl.BlockSpec((1, tq, LANES), lambda bi, hp, rb: (bi, rb, hp)),
        out_shape=jax.ShapeDtypeStruct((b, t, w), BF16),
        compiler_params=_params("parallel", "parallel", "arbitrary"),
        name="na",
    )(nq, nk, nv, cnk, cnv, bias)


def _na_bias_table(na_rpb, rows):
    h = na_rpb.shape[0]
    w = GRID_W
    c = np.arange(w)[:, None]
    kj = np.arange(w)[None, :]
    cs = np.clip(c - NA_KW // 2, 0, w - NA_KW)
    col_valid = (kj >= cs) & (kj < cs + NA_KW)
    dc = np.clip(kj - c + (NA_KW - 1), 0, 2 * NA_KW - 2)
    onehot = np.zeros((2 * NA_KW - 1, w, w), np.float32)
    onehot[dc, np.arange(w)[:, None], np.arange(w)[None, :]] = 1.0
    t2 = jnp.einsum("hrd,dck->hrck", na_rpb, jnp.asarray(onehot), precision=lax.Precision.HIGHEST)
    t2 = jnp.where(jnp.asarray(col_valid)[None, None], t2, NEG)
    t2 = jnp.concatenate([t2, jnp.full((h, 1, w, w), NEG, F32)], axis=1)
    invalid = 2 * NA_KH - 1
    dr_idx = np.full((3, NA_ROWS, NA_KEY_ROWS), invalid, np.int32)
    for kind, r0 in enumerate((0, NA_ROWS, rows - NA_ROWS)):
        ks = int(np.clip(r0 - NA_KH // 2, 0, rows - NA_KEY_ROWS))
        for qa in range(NA_ROWS):
            r = r0 + qa
            rs = int(np.clip(r - NA_KH // 2, 0, rows - NA_KH))
            for kl in range(NA_KEY_ROWS):
                ki = ks + kl
                if rs <= ki < rs + NA_KH:
                    dr_idx[kind, qa, kl] = ki - r + NA_KH - 1
    tab = jnp.take(t2, jnp.asarray(dr_idx.reshape(-1)), axis=1)
    tab = tab.reshape(h, 3, NA_ROWS, NA_KEY_ROWS, w, w).transpose(0, 1, 2, 4, 3, 5)
    return tab.reshape(h, 3, NA_ROWS * w, NA_KEY_ROWS * w)


def _post_kernel(x_ref, mod_ref, hf_ref, hb_ref, mo_ref, na_ref, gm_ref, gn_ref,
                 mnw_ref, wbm_ref, wbn_ref, wout_ref, wr_ref, wsg_ref, wsu_ref, wsd_ref,
                 h2_ref, st_ref, base_ref):
    hm = hf_ref[0].astype(F32) + hb_ref[0].astype(F32)
    parts = []
    for hd in range(M_HEADS):
        t = hm[:, hd * M_DV:(hd + 1) * M_DV]
        parts.append(t * lax.rsqrt(jnp.mean(t * t, axis=-1, keepdims=True) + EPS))
    y_m = jnp.concatenate(parts, axis=1) * mnw_ref[...] * mo_ref[0].astype(F32)
    a = _dot(y_m.astype(BF16), wbm_ref[...])
    bn = _dot(na_ref[0], wbn_ref[...])
    z = gm_ref[0].astype(F32) * a + gn_ref[0].astype(F32) * bn
    y = _dot(z.astype(BF16), wout_ref[...])
    x1 = x_ref[0] + mod_ref[0, 2:3, :] * y
    xn = x1 * lax.rsqrt(jnp.mean(x1 * x1, axis=-1, keepdims=True) + EPS)
    h2f = xn * (1.0 + mod_ref[0, 4:5, :]) + mod_ref[0, 3:4, :]
    h2_ref[...] = _pack_bf16_pairs(h2f)
    h2 = h2f.astype(BF16)
    st_ref[...] = _sigmoid(_dot_nt(wr_ref[...], h2))
    sh = _dot(h2, wsg_ref[...])
    sh = sh * _sigmoid(sh) * _dot(h2, wsu_ref[...])
    base_ref[0] = x1 + mod_ref[0, 5:6, :] * _dot(sh.astype(BF16), wsd_ref[...])


def _post(x, mod, hf, hb, mo, yna, gm, gn, mnw, wbm, wbn, wout, wr_t, wsg, wsu, wsd, tm):
    b, t, d = x.shape
    nt = t // tm
    tok = lambda w: pl.BlockSpec((1, tm, w), lambda bi, i: (bi, i, 0))
    res = [mnw, wbm, wbn, wout, wr_t, wsg, wsu, wsd]
    return pl.pallas_call(
        _post_kernel,
        grid=(b, nt),
        in_specs=[tok(d), pl.BlockSpec((1, 8, d), lambda bi, i: (bi, 0, 0)),
                  tok(hf.shape[2]), tok(hb.shape[2]), tok(mo.shape[2]), tok(yna.shape[2]),
                  tok(gm.shape[2]), tok(gn.shape[2])] + [_resident(a.shape) for a in res],
        out_specs=[pl.BlockSpec((tm, d // 2), lambda bi, i: (bi * nt + i, 0)),
                   pl.BlockSpec((N_EXPERTS, tm), lambda bi, i: (0, bi * nt + i)),
                   tok(d)],
        out_shape=[jax.ShapeDtypeStruct((b * t, d // 2), jnp.int32),
                   jax.ShapeDtypeStruct((N_EXPERTS, b * t), F32),
                   jax.ShapeDtypeStruct((b, t, d), F32)],
        compiler_params=_params("parallel", "parallel"),
        name="post",
    )(x, mod, hf, hb, mo, yna, gm, gn, *res)


def _route_kernel(s_ref, b_ref, e_ref, w_ref, r_ref, cnt_ref, run_scr):
    @pl.when(pl.program_id(0) == 0)
    def _():
        run_scr[...] = jnp.zeros_like(run_scr)

    s = s_ref[...]
    tm = s.shape[1]
    sel = s + b_ref[...][:, 0:1]
    gsz = N_EXPERTS // N_GROUPS
    ninf = -jnp.inf

    x3 = sel.reshape(N_GROUPS, gsz, tm)
    r3 = lax.broadcasted_iota(jnp.int32, x3.shape, 1)
    m1 = jnp.max(x3, axis=1, keepdims=True)
    i1 = jnp.min(jnp.where(x3 == m1, r3, gsz), axis=1, keepdims=True)
    m2 = jnp.max(jnp.where(r3 == i1, ninf, x3), axis=1)
    gs = m1[:, 0, :] + m2

    gidx = lax.broadcasted_iota(jnp.int32, gs.shape, 0)
    gkeep = jnp.zeros(gs.shape, jnp.bool_)
    cur = gs
    for _ in range(TOPK_GROUPS):
        mm = jnp.max(cur, axis=0, keepdims=True)
        ii = jnp.min(jnp.where(cur == mm, gidx, N_GROUPS), axis=0, keepdims=True)
        hit = gidx == ii
        gkeep = jnp.logical_or(gkeep, hit)
        cur = jnp.where(hit, ninf, cur)
    keep = jnp.broadcast_to(gkeep[:, None, :], x3.shape).reshape(N_EXPERTS, tm)

    row = lax.broadcasted_iota(jnp.int32, s.shape, 0)
    cur = jnp.where(keep, sel, ninf)
    idxs, ws = [], []
    chosen = jnp.zeros(s.shape, jnp.bool_)
    for _ in range(TOP_K):
        mm = jnp.max(cur, axis=0, keepdims=True)
        ii = jnp.min(jnp.where(cur == mm, row, N_EXPERTS), axis=0, keepdims=True)
        hit = row == ii
        idxs.append(ii)
        ws.append(jnp.sum(jnp.where(hit, s, 0.0), axis=0, keepdims=True))
        chosen = jnp.logical_or(chosen, hit)
        cur = jnp.where(hit, ninf, cur)
    wsum = ws[0]
    for wk in ws[1:]:
        wsum = wsum + wk

    chosen_f = jnp.where(chosen, 1.0, 0.0)
    tp = lax.broadcasted_iota(jnp.int32, (tm, tm), 0)
    tc = lax.broadcasted_iota(jnp.int32, (tm, tm), 1)
    before = jnp.where(tp < tc, 1.0, 0.0).astype(BF16)
    rank = _dot(chosen_f.astype(BF16), before) + run_scr[...][:, 0:1]
    run_scr[...] = run_scr[...] + jnp.sum(chosen_f, axis=1, keepdims=True)
    cnt_ref[...] = run_scr[...]

    for kk in range(TOP_K):
        e_ref[kk:kk + 1, :] = idxs[kk]
        w_ref[kk:kk + 1, :] = ws[kk] / wsum * ROUTE_SCALE
        r_ref[kk:kk + 1, :] = jnp.sum(jnp.where(row == idxs[kk], rank, 0.0), axis=0,
                                      keepdims=True).astype(jnp.int32)


def _route(scores_t, bias, tm):
    e, n = scores_t.shape
    blk = lambda: pl.BlockSpec((TOP_K, tm), lambda i: (0, i))
    return pl.pallas_call(
        _route_kernel,
        grid=(n // tm,),
        in_specs=[pl.BlockSpec((e, tm), lambda i: (0, i)), _resident(bias.shape)],
        out_specs=[blk(), blk(), blk(), pl.BlockSpec((e, LANES), lambda i: (0, 0))],
        out_shape=[jax.ShapeDtypeStruct((TOP_K, n), jnp.int32),
                   jax.ShapeDtypeStruct((TOP_K, n), F32),
                   jax.ShapeDtypeStruct((TOP_K, n), jnp.int32),
                   jax.ShapeDtypeStruct((e, LANES), F32)],
        scratch_shapes=[pltpu.VMEM((e, LANES), F32)],
        compiler_params=_params("arbitrary"),
        name="route",
    )(scores_t, bias)


def _slot_kernel(e_ref, r_ref, ps_ref, o_ref):
    row = lax.broadcasted_iota(jnp.int32, (N_EXPERTS, e_ref.shape[1]), 0)
    ps = ps_ref[...][:, 0:1]
    for kk in range(TOP_K):
        first = jnp.sum(jnp.where(row == e_ref[kk:kk + 1, :], ps, 0.0), axis=0, keepdims=True)
        o_ref[kk:kk + 1, :] = first.astype(jnp.int32) + r_ref[kk:kk + 1, :]


def _slots(top_e, rank, pstart, tm):
    k, n = top_e.shape
    blk = pl.BlockSpec((k, tm), lambda i: (0, i))
    return pl.pallas_call(
        _slot_kernel,
        grid=(n // tm,),
        in_specs=[blk, blk, _resident(pstart.shape)],
        out_specs=blk,
        out_shape=jax.ShapeDtypeStruct((k, n), jnp.int32),
        compiler_params=_params("parallel"),
        name="slots",
    )(top_e, rank, pstart)


SC_WINDOW = 128


def _sc_mesh():
    return plsc.VectorSubcoreMesh(core_axis_name="core", subcore_axis_name="subcore")


def _sc_workers():
    info = plsc.get_sparse_core_info()
    return info.num_cores, info.num_cores * info.num_subcores


def _dispatch_rows(x, pos, p_rows):
    n, w = x.shape
    kk = pos.shape[0]
    ncores, nw = _sc_workers()
    steps = n // nw // SC_WINDOW
    pos4 = pos.reshape(kk, nw, steps, SC_WINDOW).transpose(1, 2, 0, 3)

    @functools.partial(
        pl.kernel, mesh=_sc_mesh(),
        out_type=jax.ShapeDtypeStruct((p_rows, w), x.dtype),
        scratch_types=[pltpu.VMEM((kk, SC_WINDOW), jnp.int32),
                       pltpu.VMEM((SC_WINDOW, w), x.dtype),
                       pltpu.SemaphoreType.DMA],
    )
    def scatter(x_hbm, i_hbm, o_hbm, idx_v, rows_v, sem):
        wid = lax.axis_index("subcore") * ncores + lax.axis_index("core")

        @pl.loop(0, steps)
        def _(s):
            base = pl.multiple_of((wid * steps + s) * SC_WINDOW, SC_WINDOW)
            pltpu.sync_copy(i_hbm.at[wid, s], idx_v)
            pltpu.sync_copy(x_hbm.at[pl.ds(base, SC_WINDOW)], rows_v)
            copies = [pltpu.make_async_copy(rows_v, o_hbm.at[idx_v.at[j]], sem) for j in range(kk)]
            for cp in copies:
                cp.start()
            for cp in copies:
                cp.wait()

    return scatter(x, pos4)


def _gather_rows(x, idx):
    m = idx.shape[0]
    w = x.shape[1]
    ncores, nw = _sc_workers()
    steps = m // nw // SC_WINDOW
    idx3 = idx.reshape(nw, steps, SC_WINDOW)

    @functools.partial(
        pl.kernel, mesh=_sc_mesh(),
        out_type=jax.ShapeDtypeStruct((m, w), x.dtype),
        scratch_types=[pltpu.VMEM((steps, SC_WINDOW), jnp.int32),
                       pltpu.VMEM((SC_WINDOW, w), x.dtype),
                       pltpu.SemaphoreType.DMA],
    )
    def gather(x_hbm, i_hbm, o_hbm, idx_v, rows_v, sem):
        wid = lax.axis_index("subcore") * ncores + lax.axis_index("core")
        pltpu.sync_copy(i_hbm.at[wid], idx_v)

        @pl.loop(0, steps)
        def _(s):
            pltpu.async_copy(x_hbm.at[idx_v.at[s]], rows_v, sem).wait()
            base = pl.multiple_of((wid * steps + s) * SC_WINDOW, SC_WINDOW)
            pltpu.sync_copy(rows_v, o_hbm.at[pl.ds(base, SC_WINDOW)])

    return gather(x, idx3)


def _experts_kernel(be_ref, nv_ref, nu_ref, x_ref, wg_ref, wu_ref, wd_ref, o_ref):
    i = pl.program_id(0)

    @pl.when(i < nu_ref[0])
    def _():
        rid = lax.broadcasted_iota(jnp.int32, x_ref.shape, 0)
        xp = jnp.where(rid < nv_ref[i], x_ref[...], 0)
        x = _unpack_bf16_pairs(xp).astype(BF16)
        g = _dot(x, wg_ref[0].astype(BF16))
        u = _dot(x, wu_ref[0].astype(BF16))
        a = (g * _sigmoid(g) * u).astype(BF16)
        o_ref[...] = _pack_bf16_pairs(_dot(a, wd_ref[0].astype(BF16)))

    @pl.when(i >= nu_ref[0])
    def _():
        o_ref[...] = jnp.zeros_like(o_ref)


def _experts(block_e, n_valid, n_used, xs, wg, wu, wd):
    p, dp = xs.shape
    nb = p // MOE_BLOCK
    d, ff = wg.shape[1], wg.shape[2]
    last = lambda i, nu: jnp.minimum(i, nu[0] - 1)
    grid_spec = pltpu.PrefetchScalarGridSpec(
        num_scalar_prefetch=3,
        grid=(nb,),
        in_specs=[pl.BlockSpec((MOE_BLOCK, dp), lambda i, be, nv, nu: (last(i, nu), 0)),
                  pl.BlockSpec((1, d, ff), lambda i, be, nv, nu: (be[last(i, nu)], 0, 0)),
                  pl.BlockSpec((1, d, ff), lambda i, be, nv, nu: (be[last(i, nu)], 0, 0)),
                  pl.BlockSpec((1, ff, d), lambda i, be, nv, nu: (be[last(i, nu)], 0, 0))],
        out_specs=pl.BlockSpec((MOE_BLOCK, dp), lambda i, be, nv, nu: (i, 0)),
    )
    return pl.pallas_call(
        _experts_kernel,
        grid_spec=grid_spec,
        out_shape=jax.ShapeDtypeStruct((p, dp), jnp.int32),
        compiler_params=_params("arbitrary"),
        name="experts",
    )(block_e, n_valid, n_used, xs, wg, wu, wd)


def _combine_kernel(base_ref, mod_ref, w_ref, y_ref, o_ref):
    acc = None
    for kk in range(TOP_K):
        term = w_ref[:, kk:kk + 1] * _unpack_bf16_pairs(y_ref[kk])
        acc = term if acc is None else acc + term
    o_ref[0] = base_ref[0] + mod_ref[0, 5:6, :] * acc


def _combine(base, mod, w_tk, yg, tm):
    b, t, d = base.shape
    nt = t // tm
    tok = pl.BlockSpec((1, tm, d), lambda bi, i: (bi, i, 0))
    return pl.pallas_call(
        _combine_kernel,
        grid=(b, nt),
        in_specs=[tok, pl.BlockSpec((1, 8, d), lambda bi, i: (bi, 0, 0)),
                  pl.BlockSpec((tm, TOP_K), lambda bi, i: (bi * nt + i, 0)),
                  pl.BlockSpec((TOP_K, tm, d // 2), lambda bi, i: (0, bi * nt + i, 0))],
        out_specs=tok,
        out_shape=jax.ShapeDtypeStruct((b, t, d), F32),
        compiler_params=_params("parallel", "parallel"),
        name="combine",
    )(base, mod, w_tk, yg)


def _rope_tables(t):
    pos = np.arange(t)
    half = M_DQK // 2
    nf = half // 2
    inv = np.power(ROPE_BASE, -np.arange(nf, dtype=np.float32) / nf).astype(np.float32)
    row_pos = jnp.asarray((pos // GRID_W).astype(np.float32))
    col_pos = jnp.asarray((pos % GRID_W).astype(np.float32))
    inv = jnp.asarray(inv)
    ar = row_pos[:, None] * inv[None, :]
    ac = col_pos[:, None] * inv[None, :]
    cos = jnp.concatenate([jnp.cos(ar), jnp.cos(ar), jnp.cos(ac), jnp.cos(ac)], axis=1)
    sin = jnp.concatenate([-jnp.sin(ar), jnp.sin(ar), -jnp.sin(ac), jnp.sin(ac)], axis=1)
    return cos, sin, cos.T, sin.T


def _arrange_w_in(w_in, b_mgate):
    d = w_in.shape[0]
    sizes = (512, 512, 1024, 1024, 16, 512, 512, 512, 1024, 1024)
    offs = np.concatenate([[0], np.cumsum(sizes)])
    mq, mk, mv, mo, mg, nq, nk, nv, gm, gn = [w_in[:, offs[i]:offs[i + 1]] for i in range(10)]
    pad = jnp.zeros((d, LANES - 2 * M_HEADS), w_in.dtype)
    gi = jnp.concatenate([mg[:, 0:4], mg[:, 8:12], pad], axis=1)
    gf = jnp.concatenate([mg[:, 4:8], mg[:, 12:16], pad], axis=1)
    w_all = jnp.concatenate([mq, mk, mv, mo, gf, nq, nk, nv, gm, gn], axis=1).astype(BF16)
    wt_all = jnp.concatenate([mk, gi, gf], axis=1).T.astype(BF16)
    bpad = jnp.zeros((LANES - 2 * M_HEADS,), F32)
    bi = jnp.concatenate([b_mgate[0:4], b_mgate[8:12], bpad])
    bf = jnp.concatenate([b_mgate[4:8], b_mgate[12:16], bpad])
    bg = jnp.concatenate([bf[None, :], jnp.zeros((7, LANES), F32)], axis=0)
    bgt = jnp.concatenate([bi, bf])[:, None]
    return w_all, wt_all, bg, bgt


def _segment_mats():
    na_w = NA_HEADS * NA_DH
    seg = np.zeros((na_w, LANES), np.float32)
    seg[np.arange(na_w), np.arange(na_w) // NA_DH] = 1.0
    return jnp.asarray(seg, BF16), jnp.asarray(seg.T.copy(), BF16)


def kernel(x, c, ctx, c_ctx, w_ada, b_ada, w_in, b_mgate, m_norm_w, na_qn_w, na_kn_w, na_rpb,
           w_br_m, w_br_na, w_out, w_router, router_bias, w_exp_gate, w_exp_up, w_exp_down,
           w_sh_gate, w_sh_up, w_sh_down):
    b, t, d = x.shape
    n = b * t
    rows = t // GRID_W
    l = 0

    cc = jnp.concatenate([c, c_ctx[None, :], jnp.zeros((8 - b - 1, d), F32)], axis=0)
    mod = _ada(cc, w_ada[l], b_ada[l])
    mod = mod.reshape(8, 6, d)
    mod = jnp.concatenate([mod, jnp.zeros((8, 2, d), F32)], axis=1)
    mod_x = mod[:b]
    mod_c = jnp.broadcast_to(mod[b:b + 1], (b, 8, d))

    w_all, wt_all, bg, bgt = _arrange_w_in(w_in[l], b_mgate[l])
    seg, segt = _segment_mats()
    qnw = jnp.tile(na_qn_w[l], NA_HEADS)[None, :]
    knw = jnp.tile(na_kn_w[l], NA_HEADS)[None, :]
    tm = min(512, t)

    cp = _inproj(ctx, mod_c, w_all, wt_all, bg, bgt, qnw, knw, seg, segt, None,
                 min(tm, ctx.shape[1]))
    xp = _inproj(x, mod_x, w_all, wt_all, bg, bgt, qnw, knw, seg, segt, _rope_tables(t), tm)
    cmq, cmk, cmv, _, cgf, _, cnk, cnv, _, _, cmkt, cgit, cgft = cp
    mq, mk, mv, mo, gf, nq, nk, nv, gm, gn, mkt, git, gft = xp

    c0 = jnp.zeros((b, 8, M_DQK, MLSTM_EXT), F32)
    m0 = jnp.zeros((b, 8, LANES), F32)
    _, _, c1, m1 = _mlstm(cmq, cmk, cmkt, cmv, cgf, cgit, cgft, c0, m0)
    hf, hb, _, _ = _mlstm(mq, mk, mkt, mv, gf, git, gft, c1, m1)

    yna = _na(nq, nk, nv, cnk, cnv, _na_bias_table(na_rpb[l], rows))

    h2p, scores_t, base = _post(
        x, mod_x, hf, hb, mo, yna, gm, gn, m_norm_w[l][None, :],
        w_br_m[l].astype(BF16), w_br_na[l].astype(BF16), w_out[l].astype(BF16),
        w_router[l].T.astype(BF16), w_sh_gate[l].astype(BF16), w_sh_up[l].astype(BF16),
        w_sh_down[l].astype(BF16), tm)

    bias_col = jnp.broadcast_to(router_bias[l][:, None], (N_EXPERTS, LANES))
    top_e, top_w, rank, cnt = _route(scores_t, bias_col, min(512, n))

    counts = cnt[:, 0].astype(jnp.int32)
    padded = (counts + MOE_BLOCK - 1) // MOE_BLOCK * MOE_BLOCK
    pend = jnp.cumsum(padded)
    pstart = pend - padded
    nb = -(-(n * TOP_K) // MOE_BLOCK) + N_EXPERTS
    p_rows = nb * MOE_BLOCK
    blk_row0 = jnp.arange(nb, dtype=jnp.int32) * MOE_BLOCK
    block_e = jnp.minimum(jnp.sum((pend[None, :] <= blk_row0[:, None]).astype(jnp.int32), axis=1),
                          N_EXPERTS - 1)
    n_used = pend[-1:] // MOE_BLOCK
    onehot_e = (block_e[:, None] == jnp.arange(N_EXPERTS, dtype=jnp.int32)[None, :]).astype(jnp.int32)
    e_end = jnp.sum(onehot_e * (pstart + counts)[None, :], axis=1)
    n_valid = jnp.clip(e_end - blk_row0, 0, MOE_BLOCK)

    pstart_col = jnp.broadcast_to(pstart.astype(F32)[:, None], (N_EXPERTS, LANES))
    pos = _slots(top_e, rank, pstart_col, min(512, n))
    xs = _dispatch_rows(h2p, pos, p_rows)
    ys = _experts(block_e, n_valid, n_used, xs, w_exp_gate[l], w_exp_up[l], w_exp_down[l])
    yg = _gather_rows(ys, pos.reshape(-1)).reshape(TOP_K, n, d // 2)
    return _combine(base, mod_x, top_w.T, yg, tm)
```

```python
import functools

import numpy as np
import jax
import jax.numpy as jnp
from jax import lax
from jax.experimental import pallas as pl
from jax.experimental.pallas import tpu as pltpu
from jax.experimental.pallas import tpu_sc as plsc

F32 = jnp.float32
BF16 = jnp.bfloat16

EPS = 1e-6
GRID_W = 64
M_HEADS, M_DQK, M_DV = 4, 128, 256
ROPE_BASE = 10000.0
NA_HEADS, NA_DH, NA_KH, NA_KW = 8, 64, 8, 16
N_EXPERTS, TOP_K, N_GROUPS, TOPK_GROUPS = 256, 8, 8, 4
ROUTE_SCALE = 2.5

LANES = 128
VMEM_LIMIT = 56 * 1024 * 1024
NEG = -1e30

MLSTM_CHUNK = 256
NA_ROWS = 4
NA_KEY_ROWS = NA_ROWS + NA_KH - 1
MOE_BLOCK = 512

_W_SEGS = (("mq", 512), ("mv", 1024), ("mo", 1024), ("gf", 128),
           ("nq", 512), ("nk", 512), ("nv", 512), ("gm", 1024), ("gn", 1024))
_W_OFF = {}
_o = 0
for _n, _w in _W_SEGS:
    _W_OFF[_n] = (_o, _w)
    _o += _w
W_COLS = _o


def _dot(a, b):
    return jnp.dot(a, b, preferred_element_type=F32)


def _dot_nt(a, b):
    return lax.dot_general(a, b, (((1,), (1,)), ((), ())), preferred_element_type=F32)


def _sigmoid(x):
    return 1.0 / (1.0 + jnp.exp(-x))


def _pack_bf16_pairs(v):
    w = v.shape[1] // 2
    bits = pltpu.bitcast(v.astype(BF16).astype(F32), jnp.int32)
    return lax.shift_right_logical(bits[:, :w], 16) | bits[:, w:]


def _unpack_bf16_pairs(p):
    lo = pltpu.bitcast(lax.shift_left(p, 16), F32)
    hi = pltpu.bitcast(p & jnp.int32(-65536), F32)
    return jnp.concatenate([lo, hi], axis=1)


def _params(*sem):
    return pltpu.CompilerParams(dimension_semantics=sem, vmem_limit_bytes=VMEM_LIMIT)


def _resident(shape):
    nd = len(shape)
    return pl.BlockSpec(shape, lambda *_: (0,) * nd, pipeline_mode=pl.Buffered(1))


def _ada_kernel(c_ref, w_ref, b_ref, o_ref):
    c = c_ref[...]
    s = c * _sigmoid(c)
    o_ref[...] = _dot(s.astype(BF16), w_ref[...].astype(BF16)) + b_ref[...]


def _ada(cc, w_ada, b_ada):
    d = cc.shape[1]
    n = w_ada.shape[1]
    return pl.pallas_call(
        _ada_kernel,
        grid=(n // d,),
        in_specs=[pl.BlockSpec((8, d), lambda j: (0, 0)),
                  pl.BlockSpec((d, d), lambda j: (0, j)),
                  pl.BlockSpec((1, d), lambda j: (0, j))],
        out_specs=pl.BlockSpec((8, d), lambda j: (0, j)),
        out_shape=jax.ShapeDtypeStruct((8, n), F32),
        compiler_params=_params("arbitrary"),
        name="ada",
    )(cc, w_ada, b_ada.reshape(1, n))


def _rope_rotate(t, cos, sin):
    lane = lax.broadcasted_iota(jnp.int32, t.shape, 1)
    partner = jnp.where((lane & 32) == 0, pltpu.roll(t, 96, 1), pltpu.roll(t, 32, 1))
    return t * cos + partner * sin


def _rope_rotate_t(t, cos, sin):
    q = M_DQK // 4
    partner = jnp.concatenate([t[q:2 * q], t[0:q], t[3 * q:4 * q], t[2 * q:3 * q]], axis=0)
    return t * cos + partner * sin


def _inproj_kernel(*refs, rope):
    if rope:
        (x_ref, mod_ref, w_ref, wt_ref, bg_ref, bgt_ref, qnw_ref, knw_ref, seg_ref, segt_ref,
         cos_ref, sin_ref, cost_ref, sint_ref,
         mq_ref, mv_ref, mo_ref, gf_ref, nq_ref, nk_ref, nv_ref, gm_ref, gn_ref,
         mkt_ref, git_ref, gft_ref) = refs
    else:
        (x_ref, mod_ref, w_ref, wt_ref, bg_ref, bgt_ref, qnw_ref, knw_ref, seg_ref, segt_ref,
         mq_ref, mv_ref, mo_ref, gf_ref, nq_ref, nk_ref, nv_ref, gm_ref, gn_ref,
         mkt_ref, git_ref, gft_ref) = refs
    x = x_ref[0]
    xn = x * lax.rsqrt(jnp.mean(x * x, axis=-1, keepdims=True) + EPS)
    h = xn * (1.0 + mod_ref[0, 1:2, :]) + mod_ref[0, 0:1, :]
    hb = h.astype(BF16)

    def proj(name):
        off, width = _W_OFF[name]
        return _dot(hb, w_ref[:, off:off + width])

    def head_rms(t, w_row, scale):
        ss = _dot((t * t).astype(BF16), seg_ref[...])
        r = lax.rsqrt(ss * (1.0 / NA_DH) + EPS)
        r_hi = r.astype(BF16)
        r_lo = (r - r_hi.astype(F32)).astype(BF16)
        rb = _dot(r_hi, segt_ref[...]) + _dot(r_lo, segt_ref[...])
        return t * rb * w_row * scale

    mq = proj("mq") * (M_DQK ** -0.5)
    if rope:
        cos, sin = cos_ref[...], sin_ref[...]
        mq = jnp.concatenate([_rope_rotate(mq[:, i * LANES:(i + 1) * LANES], cos, sin)
                              for i in range(M_HEADS)], axis=1)
    mq_ref[0] = mq.astype(BF16)
    mv_ref[0] = proj("mv").astype(BF16)
    mo_ref[0] = _sigmoid(proj("mo")).astype(BF16)
    gf_ref[0] = proj("gf") + bg_ref[0:1, :]
    nq_ref[0] = head_rms(proj("nq"), qnw_ref[...], NA_DH ** -0.5).astype(BF16)
    nk_ref[0] = head_rms(proj("nk"), knw_ref[...], 1.0).astype(BF16)
    nv_ref[0] = proj("nv").astype(BF16)
    gm_ref[0] = _sigmoid(proj("gm")).astype(BF16)
    gn_ref[0] = _sigmoid(proj("gn")).astype(BF16)

    qk_w = M_HEADS * M_DQK
    mkt = _dot_nt(wt_ref[0:qk_w, :], hb)
    if rope:
        cost, sint = cost_ref[...], sint_ref[...]
        mkt = jnp.concatenate([_rope_rotate_t(mkt[i * M_DQK:(i + 1) * M_DQK], cost, sint)
                               for i in range(M_HEADS)], axis=0)
    mkt_ref[0] = mkt.astype(BF16)
    git_ref[0] = _dot_nt(wt_ref[qk_w:qk_w + LANES, :], hb) + bgt_ref[0:LANES, :]
    gft_ref[0] = _dot_nt(wt_ref[qk_w + LANES:qk_w + 2 * LANES, :], hb) + bgt_ref[LANES:2 * LANES, :]


def _inproj(x, mod, w_all, wt_all, bg, bgt, qnw, knw, seg, segt, rope_tabs, tm):
    b, t, d = x.shape
    rope = rope_tabs is not None
    tok = lambda w: pl.BlockSpec((1, tm, w), lambda bi, i: (bi, i, 0))
    tok_t = lambda w: pl.BlockSpec((1, w, tm), lambda bi, i: (bi, 0, i))
    in_specs = [tok(d),
                pl.BlockSpec((1, 8, d), lambda bi, i: (bi, 0, 0)),
                _resident(w_all.shape), _resident(wt_all.shape), _resident(bg.shape),
                _resident(bgt.shape), _resident(qnw.shape),
                _resident(knw.shape), _resident(seg.shape), _resident(segt.shape)]
    args = [x, mod, w_all, wt_all, bg, bgt, qnw, knw, seg, segt]
    if rope:
        in_specs += [pl.BlockSpec((tm, LANES), lambda bi, i: (i, 0))] * 2
        in_specs += [pl.BlockSpec((LANES, tm), lambda bi, i: (0, i))] * 2
        args += list(rope_tabs)
    widths = [("mq", BF16), ("mv", BF16), ("mo", BF16), ("gf", F32),
              ("nq", BF16), ("nk", BF16), ("nv", BF16), ("gm", BF16), ("gn", BF16)]
    out_specs = [tok(_W_OFF[n][1]) for n, _ in widths]
    out_shape = [jax.ShapeDtypeStruct((b, t, _W_OFF[n][1]), dt) for n, dt in widths]
    out_specs += [tok_t(M_HEADS * M_DQK), tok_t(LANES), tok_t(LANES)]
    out_shape += [jax.ShapeDtypeStruct((b, M_HEADS * M_DQK, t), BF16),
                  jax.ShapeDtypeStruct((b, LANES, t), F32),
                  jax.ShapeDtypeStruct((b, LANES, t), F32)]
    return pl.pallas_call(
        functools.partial(_inproj_kernel, rope=rope),
        grid=(b, t // tm),
        in_specs=in_specs, out_specs=out_specs, out_shape=out_shape,
        compiler_params=_params("parallel", "parallel"),
        name="inproj_rope" if rope else "inproj_ctx",
    )(*args)


def _log_sigmoid(x):
    return jnp.minimum(x, 0.0) - jnp.log(1.0 + jnp.exp(-jnp.abs(x)))


def _dot_split(a, b, split_a):
    x = a if split_a else b
    hi = x.astype(BF16)
    lo = (x - hi.astype(F32)).astype(BF16)
    return (_dot(hi, b) + _dot(lo, b)) if split_a else (_dot(a, hi) + _dot(a, lo))


MLSTM_EXT = M_DV + LANES


def _mlstm_kernel(qf_ref, ktf_ref, vf_ref, gff_ref, gitf_ref, gftf_ref,
                  qb_ref, ktb_ref, vb_ref, gfb_ref, gitb_ref, gftb_ref,
                  c0_ref, m0_ref,
                  hf_ref, hb_ref, cn_ref, mn_ref,
                  *scratch):
    c_scrs, m_scr = scratch[:2 * M_HEADS], scratch[2 * M_HEADS]
    step = pl.program_id(1)
    L = qf_ref.shape[1]
    nu = 2 * M_HEADS

    @pl.when(step == 0)
    def _():
        for j, c_scr in enumerate(c_scrs):
            c_scr[...] = c0_ref[0, j]
        m_scr[...] = m0_ref[0]

    row_i = lax.broadcasted_iota(jnp.int32, (L, L), 0)
    col_i = lax.broadcasted_iota(jnp.int32, (L, L), 1)
    lower = col_i <= row_i
    upper = col_i >= row_i
    tri_lo = jnp.where(lower, 1.0, 0.0).astype(BF16)
    tri_up = jnp.where(upper, 1.0, 0.0).astype(BF16)

    is_f = lax.broadcasted_iota(jnp.int32, (nu, L), 0) < M_HEADS
    gi_t = jnp.where(is_f, gitf_ref[0, 0:nu, :], gitb_ref[0, 0:nu, :])
    ls_tf = _log_sigmoid(gftf_ref[0, 0:nu, :])
    ls_tb = _log_sigmoid(gftb_ref[0, 0:nu, :])
    b_t = jnp.where(is_f, _dot_split(ls_tf, tri_up, True), _dot_split(ls_tb, tri_lo, True))
    u_t = gi_t - b_t
    g_c = jnp.sum(jnp.where(is_f, ls_tf, ls_tb), axis=1, keepdims=True)
    m_prev = m_scr[...]
    a_t = g_c + u_t
    m_new = jnp.maximum(g_c + m_prev, jnp.max(a_t, axis=1, keepdims=True))
    decay = jnp.exp(g_c + m_prev - m_new)
    wa_t = jnp.exp(a_t - jnp.concatenate([m_new] * (L // LANES), axis=1))
    m_scr[...] = m_new

    ones = jnp.ones((L, LANES), BF16)
    dirs = ((qf_ref, ktf_ref, vf_ref, gff_ref, hf_ref, lower, tri_lo),
            (qb_ref, ktb_ref, vb_ref, gfb_ref, hb_ref, upper, tri_up))
    for d, (q_ref, kt_ref, v_ref, gf_ref, h_ref, mask, tri) in enumerate(dirs):
        bcum = _dot_split(tri, _log_sigmoid(gf_ref[0]), False)
        for hd in range(M_HEADS):
            j = d * M_HEADS + hd
            c_scr = c_scrs[j]
            q = q_ref[0, :, hd * M_DQK:(hd + 1) * M_DQK]
            k_t = kt_ref[0, hd * M_DQK:(hd + 1) * M_DQK, :]
            v_ext = jnp.concatenate([v_ref[0, :, hd * M_DV:(hd + 1) * M_DV], ones], axis=1)
            u_row = u_t[j:j + 1, :]
            mp_row = m_prev[j:j + 1, :]
            c_prev = c_scr[...]

            m_loc = jnp.max(jnp.where(mask, u_row, NEG), axis=1, keepdims=True)
            m_rep = jnp.maximum(jnp.broadcast_to(m_loc, (L, LANES)), mp_row)
            m_wide = jnp.concatenate([m_rep] * (L // LANES), axis=1)
            dmat = jnp.exp(jnp.where(mask, u_row - m_wide, NEG))
            s = (_dot(q, k_t) * dmat).astype(BF16)
            qw = (q.astype(F32) * jnp.exp(mp_row - m_rep)).astype(BF16)
            r = _dot(s, v_ext) + _dot(qw, c_prev.astype(BF16))
            b_rep = jnp.broadcast_to(bcum[:, j:j + 1], (L, LANES))
            dn = jnp.maximum(jnp.abs(r[:, M_DV:]), jnp.exp(-(b_rep + m_rep)))
            h_ref[0, :, hd * M_DV:(hd + 1) * M_DV] = (
                r[:, :M_DV] / jnp.concatenate([dn] * (M_DV // LANES), axis=1)).astype(h_ref.dtype)

            kw = (k_t.astype(F32) * wa_t[j:j + 1, :]).astype(BF16)
            dec = jnp.concatenate([decay[j:j + 1, :]] * (MLSTM_EXT // LANES), axis=1)
            c_scr[...] = dec * c_prev + _dot(kw, v_ext)

    @pl.when(step == pl.num_programs(1) - 1)
    def _():
        for j, c_scr in enumerate(c_scrs):
            cn_ref[0, j] = c_scr[...]
        mn_ref[0] = m_scr[...]


def _mlstm(q, kt, v, gf, git, gft, c0, m0):
    b, t, _ = q.shape
    L = min(MLSTM_CHUNK, t)
    nc = t // L
    fwd = lambda w: pl.BlockSpec((1, L, w), lambda bi, i: (bi, i, 0))
    bwd = lambda w: pl.BlockSpec((1, L, w), lambda bi, i: (bi, nc - 1 - i, 0))
    fwd_t = lambda w: pl.BlockSpec((1, w, L), lambda bi, i: (bi, 0, i))
    bwd_t = lambda w: pl.BlockSpec((1, w, L), lambda bi, i: (bi, 0, nc - 1 - i))
    st_c = pl.BlockSpec((1, 8, M_DQK, MLSTM_EXT), lambda bi, i: (bi, 0, 0, 0))
    st_v = pl.BlockSpec((1, 8, LANES), lambda bi, i: (bi, 0, 0))
    qk_w, v_w = M_HEADS * M_DQK, M_HEADS * M_DV
    return pl.pallas_call(
        _mlstm_kernel,
        grid=(b, nc),
        in_specs=[fwd(qk_w), fwd_t(qk_w), fwd(v_w), fwd(LANES), fwd_t(LANES), fwd_t(LANES),
                  bwd(qk_w), bwd_t(qk_w), bwd(v_w), bwd(LANES), bwd_t(LANES), bwd_t(LANES),
                  st_c, st_v],
        out_specs=[fwd(v_w), bwd(v_w), st_c, st_v],
        out_shape=[jax.ShapeDtypeStruct((b, t, v_w), BF16),
                   jax.ShapeDtypeStruct((b, t, v_w), BF16),
                   jax.ShapeDtypeStruct(c0.shape, F32),
                   jax.ShapeDtypeStruct(m0.shape, F32)],
        scratch_shapes=([pltpu.VMEM((M_DQK, MLSTM_EXT), F32) for _ in range(2 * M_HEADS)]
                        + [pltpu.VMEM((8, LANES), F32)]),
        compiler_params=_params("parallel", "arbitrary"),
        name="mlstm",
    )(q, kt, v, gf, git, gft, q, kt, v, gf, git, gft, c0, m0)


def _na_kernel(q_ref, k_ref, v_ref, kc_ref, vc_ref, bias_ref, o_ref, *, rows):
    r0 = pl.program_id(2) * NA_ROWS
    ks = jnp.clip(r0 - NA_KH // 2, 0, rows - NA_KEY_ROWS)
    kstart = pl.multiple_of(ks * GRID_W, GRID_W)
    nkeys = NA_KEY_ROWS * GRID_W
    kblk = k_ref[0, pl.ds(kstart, nkeys), :]
    vblk = v_ref[0, pl.ds(kstart, nkeys), :]
    kc = kc_ref[0]
    vc = vc_ref[0]
    q = q_ref[0]
    lane = lax.broadcasted_iota(jnp.int32, q.shape, 1)
    outs = []
    for hh in range(2):
        in_head = (lane < NA_DH) if hh == 0 else (lane >= NA_DH)
        qm = jnp.where(in_head, q, jnp.zeros_like(q))
        sw = _dot_nt(qm, kblk) + bias_ref[hh, 0]
        sc = _dot_nt(qm, kc)
        m = jnp.maximum(jnp.max(sw, axis=1, keepdims=True), jnp.max(sc, axis=1, keepdims=True))
        ew = jnp.exp(sw - m)
        ec = jnp.exp(sc - m)
        l = jnp.sum(ew, axis=1, keepdims=True) + jnp.sum(ec, axis=1, keepdims=True)
        o = _dot(ew.astype(BF16), vblk) + _dot(ec.astype(BF16), vc)
        outs.append(o / l)
    o_ref[0] = jnp.where(lane < NA_DH, outs[0], outs[1]).astype(o_ref.dtype)


def _na(nq, nk, nv, cnk, cnv, bias):
    b, t, w = nq.shape
    rows = t // GRID_W
    tq = NA_ROWS * GRID_W
    nrb = rows // NA_ROWS
    nctx = cnk.shape[1]
    kind = lambda rb: jnp.where(rb == 0, 0, jnp.where(rb == nrb - 1, 2, 1))
    return pl.pallas_call(
        functools.partial(_na_kernel, rows=rows),
        grid=(b, w // LANES, nrb),
        in_specs=[pl.BlockSpec((1, tq, LANES), lambda bi, hp, rb: (bi, rb, hp)),
                  pl.BlockSpec((1, t, LANES), lambda bi, hp, rb: (bi, 0, hp)),
                  pl.BlockSpec((1, t, LANES), lambda bi, hp, rb: (bi, 0, hp)),
                  pl.BlockSpec((1, nctx, LANES), lambda bi, hp, rb: (bi, 0, hp)),
                  pl.BlockSpec((1, nctx, LANES), lambda bi, hp, rb: (bi, 0, hp)),
                  pl.BlockSpec((2, 1, tq, NA_KEY_ROWS * GRID_W),
                               lambda bi, hp, rb: (hp, kind(rb), 0, 0))],
        out_specs=pl.BlockSpec((1, tq, LANES), lambda bi, hp, rb: (bi, rb, hp)),
        out_shape=jax.ShapeDtypeStruct((b, t, w), BF16),
        compiler_params=_params("parallel", "parallel", "arbitrary"),
        name="na",
    )(nq, nk, nv, cnk, cnv, bias)


def _na_bias_table(na_rpb, rows):
    h = na_rpb.shape[0]
    w = GRID_W
    c = np.arange(w)[:, None]
    kj = np.arange(w)[None, :]
    cs = np.clip(c - NA_KW // 2, 0, w - NA_KW)
    col_valid = (kj >= cs) & (kj < cs + NA_KW)
    dc = np.clip(kj - c + (NA_KW - 1), 0, 2 * NA_KW - 2)
    onehot = np.zeros((2 * NA_KW - 1, w, w), np.float32)
    onehot[dc, np.arange(w)[:, None], np.arange(w)[None, :]] = 1.0
    t2 = jnp.einsum("hrd,dck->hrck", na_rpb, jnp.asarray(onehot), precision=lax.Precision.HIGHEST)
    t2 = jnp.where(jnp.asarray(col_valid)[None, None], t2, NEG)
    t2 = jnp.concatenate([t2, jnp.full((h, 1, w, w), NEG, F32)], axis=1)
    invalid = 2 * NA_KH - 1
    dr_idx = np.full((3, NA_ROWS, NA_KEY_ROWS), invalid, np.int32)
    for kind, r0 in enumerate((0, NA_ROWS, rows - NA_ROWS)):
        ks = int(np.clip(r0 - NA_KH // 2, 0, rows - NA_KEY_ROWS))
        for qa in range(NA_ROWS):
            r = r0 + qa
            rs = int(np.clip(r - NA_KH // 2, 0, rows - NA_KH))
            for kl in range(NA_KEY_ROWS):
                ki = ks + kl
                if rs <= ki < rs + NA_KH:
                    dr_idx[kind, qa, kl] = ki - r + NA_KH - 1
    tab = jnp.take(t2, jnp.asarray(dr_idx.reshape(-1)), axis=1)
    tab = tab.reshape(h, 3, NA_ROWS, NA_KEY_ROWS, w, w).transpose(0, 1, 2, 4, 3, 5)
    return tab.reshape(h, 3, NA_ROWS * w, NA_KEY_ROWS * w)


def _post_kernel(x_ref, mod_ref, hf_ref, hb_ref, mo_ref, na_ref, gm_ref, gn_ref,
                 mnw_ref, wbm_ref, wbn_ref, wout_ref, wr_ref, wsg_ref, wsu_ref, wsd_ref,
                 h2_ref, st_ref, base_ref):
    hm = hf_ref[0].astype(F32) + hb_ref[0].astype(F32)
    parts = []
    for hd in range(M_HEADS):
        t = hm[:, hd * M_DV:(hd + 1) * M_DV]
        parts.append(t * lax.rsqrt(jnp.mean(t * t, axis=-1, keepdims=True) + EPS))
    y_m = jnp.concatenate(parts, axis=1) * mnw_ref[...] * mo_ref[0].astype(F32)
    a = _dot(y_m.astype(BF16), wbm_ref[...])
    bn = _dot(na_ref[0], wbn_ref[...])
    z = gm_ref[0].astype(F32) * a + gn_ref[0].astype(F32) * bn
    y = _dot(z.astype(BF16), wout_ref[...])
    x1 = x_ref[0] + mod_ref[0, 2:3, :] * y
    xn = x1 * lax.rsqrt(jnp.mean(x1 * x1, axis=-1, keepdims=True) + EPS)
    h2f = xn * (1.0 + mod_ref[0, 4:5, :]) + mod_ref[0, 3:4, :]
    h2_ref[...] = _pack_bf16_pairs(h2f)
    h2 = h2f.astype(BF16)
    st_ref[...] = _sigmoid(_dot_nt(wr_ref[...], h2))
    sh = _dot(h2, wsg_ref[...])
    sh = sh * _sigmoid(sh) * _dot(h2, wsu_ref[...])
    base_ref[0] = x1 + mod_ref[0, 5:6, :] * _dot(sh.astype(BF16), wsd_ref[...])


def _post(x, mod, hf, hb, mo, yna, gm, gn, mnw, wbm, wbn, wout, wr_t, wsg, wsu, wsd, tm):
    b, t, d = x.shape
    nt = t // tm
    tok = lambda w: pl.BlockSpec((1, tm, w), lambda bi, i: (bi, i, 0))
    res = [mnw, wbm, wbn, wout, wr_t, wsg, wsu, wsd]
    return pl.pallas_call(
        _post_kernel,
        grid=(b, nt),
        in_specs=[tok(d), pl.BlockSpec((1, 8, d), lambda bi, i: (bi, 0, 0)),
                  tok(hf.shape[2]), tok(hb.shape[2]), tok(mo.shape[2]), tok(yna.shape[2]),
                  tok(gm.shape[2]), tok(gn.shape[2])] + [_resident(a.shape) for a in res],
        out_specs=[pl.BlockSpec((tm, d // 2), lambda bi, i: (bi * nt + i, 0)),
                   pl.BlockSpec((N_EXPERTS, tm), lambda bi, i: (0, bi * nt + i)),
                   tok(d)],
        out_shape=[jax.ShapeDtypeStruct((b * t, d // 2), jnp.int32),
                   jax.ShapeDtypeStruct((N_EXPERTS, b * t), F32),
                   jax.ShapeDtypeStruct((b, t, d), F32)],
        compiler_params=_params("parallel", "parallel"),
        name="post",
    )(x, mod, hf, hb, mo, yna, gm, gn, *res)


def _route_kernel(s_ref, b_ref, e_ref, w_ref, r_ref, cnt_ref, run_scr):
    @pl.when(pl.program_id(0) == 0)
    def _():
        run_scr[...] = jnp.zeros_like(run_scr)

    s = s_ref[...]
    tm = s.shape[1]
    sel = s + b_ref[...][:, 0:1]
    gsz = N_EXPERTS // N_GROUPS
    ninf = -jnp.inf

    x3 = sel.reshape(N_GROUPS, gsz, tm)
    r3 = lax.broadcasted_iota(jnp.int32, x3.shape, 1)
    m1 = jnp.max(x3, axis=1, keepdims=True)
    i1 = jnp.min(jnp.where(x3 == m1, r3, gsz), axis=1, keepdims=True)
    m2 = jnp.max(jnp.where(r3 == i1, ninf, x3), axis=1)
    gs = m1[:, 0, :] + m2

    gidx = lax.broadcasted_iota(jnp.int32, gs.shape, 0)
    gkeep = jnp.zeros(gs.shape, jnp.bool_)
    cur = gs
    for _ in range(TOPK_GROUPS):
        mm = jnp.max(cur, axis=0, keepdims=True)
        ii = jnp.min(jnp.where(cur == mm, gidx, N_GROUPS), axis=0, keepdims=True)
        hit = gidx == ii
        gkeep = jnp.logical_or(gkeep, hit)
        cur = jnp.where(hit, ninf, cur)
    keep = jnp.broadcast_to(gkeep[:, None, :], x3.shape).reshape(N_EXPERTS, tm)

    row = lax.broadcasted_iota(jnp.int32, s.shape, 0)
    cur = jnp.where(keep, sel, ninf)
    idxs, ws = [], []
    chosen = jnp.zeros(s.shape, jnp.bool_)
    for _ in range(TOP_K):
        mm = jnp.max(cur, axis=0, keepdims=True)
        ii = jnp.min(jnp.where(cur == mm, row, N_EXPERTS), axis=0, keepdims=True)
        hit = row == ii
        idxs.append(ii)
        ws.append(jnp.sum(jnp.where(hit, s, 0.0), axis=0, keepdims=True))
        chosen = jnp.logical_or(chosen, hit)
        cur = jnp.where(hit, ninf, cur)
    wsum = ws[0]
    for wk in ws[1:]:
        wsum = wsum + wk

    chosen_f = jnp.where(chosen, 1.0, 0.0)
    tp = lax.broadcasted_iota(jnp.int32, (tm, tm), 0)
    tc = lax.broadcasted_iota(jnp.int32, (tm, tm), 1)
    before = jnp.where(tp < tc, 1.0, 0.0).astype(BF16)
    rank = _dot(chosen_f.astype(BF16), before) + run_scr[...][:, 0:1]
    run_scr[...] = run_scr[...] + jnp.sum(chosen_f, axis=1, keepdims=True)
    cnt_ref[...] = run_scr[...]

    for kk in range(TOP_K):
        e_ref[kk:kk + 1, :] = idxs[kk]
        w_ref[kk:kk + 1, :] = ws[kk] / wsum * ROUTE_SCALE
        r_ref[kk:kk + 1, :] = jnp.sum(jnp.where(row == idxs[kk], rank, 0.0), axis=0,
                                      keepdims=True).astype(jnp.int32)


def _route(scores_t, bias, tm):
    e, n = scores_t.shape
    blk = lambda: pl.BlockSpec((TOP_K, tm), lambda i: (0, i))
    return pl.pallas_call(
        _route_kernel,
        grid=(n // tm,),
        in_specs=[pl.BlockSpec((e, tm), lambda i: (0, i)), _resident(bias.shape)],
        out_specs=[blk(), blk(), blk(), pl.BlockSpec((e, LANES), lambda i: (0, 0))],
        out_shape=[jax.ShapeDtypeStruct((TOP_K, n), jnp.int32),
                   jax.ShapeDtypeStruct((TOP_K, n), F32),
                   jax.ShapeDtypeStruct((TOP_K, n), jnp.int32),
                   jax.ShapeDtypeStruct((e, LANES), F32)],
        scratch_shapes=[pltpu.VMEM((e, LANES), F32)],
        compiler_params=_params("arbitrary"),
        name="route",
    )(scores_t, bias)


def _slot_kernel(e_ref, r_ref, ps_ref, o_ref):
    row = lax.broadcasted_iota(jnp.int32, (N_EXPERTS, e_ref.shape[1]), 0)
    ps = ps_ref[...][:, 0:1]
    for kk in range(TOP_K):
        first = jnp.sum(jnp.where(row == e_ref[kk:kk + 1, :], ps, 0.0), axis=0, keepdims=True)
        o_ref[kk:kk + 1, :] = first.astype(jnp.int32) + r_ref[kk:kk + 1, :]


def _slots(top_e, rank, pstart, tm):
    k, n = top_e.shape
    blk = pl.BlockSpec((k, tm), lambda i: (0, i))
    return pl.pallas_call(
        _slot_kernel,
        grid=(n // tm,),
        in_specs=[blk, blk, _resident(pstart.shape)],
        out_specs=blk,
        out_shape=jax.ShapeDtypeStruct((k, n), jnp.int32),
        compiler_params=_params("parallel"),
        name="slots",
    )(top_e, rank, pstart)


SC_WINDOW = 128


def _sc_mesh():
    return plsc.VectorSubcoreMesh(core_axis_name="core", subcore_axis_name="subcore")


def _sc_workers():
    info = plsc.get_sparse_core_info()
    return info.num_cores, info.num_cores * info.num_subcores


def _dispatch_rows(x, pos, p_rows):
    n, w = x.shape
    kk = pos.shape[0]
    ncores, nw = _sc_workers()
    steps = n // nw // SC_WINDOW
    pos4 = pos.reshape(kk, nw, steps, SC_WINDOW).transpose(1, 2, 0, 3)

    @functools.partial(
        pl.kernel, mesh=_sc_mesh(),
        out_type=jax.ShapeDtypeStruct((p_rows, w), x.dtype),
        scratch_types=[pltpu.VMEM((kk, SC_WINDOW), jnp.int32),
                       pltpu.VMEM((SC_WINDOW, w), x.dtype),
                       pltpu.SemaphoreType.DMA],
    )
    def scatter(x_hbm, i_hbm, o_hbm, idx_v, rows_v, sem):
        wid = lax.axis_index("subcore") * ncores + lax.axis_index("core")

        @pl.loop(0, steps)
        def _(s):
            base = pl.multiple_of((wid * steps + s) * SC_WINDOW, SC_WINDOW)
            pltpu.sync_copy(i_hbm.at[wid, s], idx_v)
            pltpu.sync_copy(x_hbm.at[pl.ds(base, SC_WINDOW)], rows_v)
            copies = [pltpu.make_async_copy(rows_v, o_hbm.at[idx_v.at[j]], sem) for j in range(kk)]
            for cp in copies:
                cp.start()
            for cp in copies:
                cp.wait()

    return scatter(x, pos4)


def _gather_rows(x, idx):
    m = idx.shape[0]
    w = x.shape[1]
    ncores, nw = _sc_workers()
    steps = m // nw // SC_WINDOW
    idx3 = idx.reshape(nw, steps, SC_WINDOW)

    @functools.partial(
        pl.kernel, mesh=_sc_mesh(),
        out_type=jax.ShapeDtypeStruct((m, w), x.dtype),
        scratch_types=[pltpu.VMEM((steps, SC_WINDOW), jnp.int32),
                       pltpu.VMEM((SC_WINDOW, w), x.dtype),
                       pltpu.SemaphoreType.DMA],
    )
    def gather(x_hbm, i_hbm, o_hbm, idx_v, rows_v, sem):
        wid = lax.axis_index("subcore") * ncores + lax.axis_index("core")
        pltpu.sync_copy(i_hbm.at[wid], idx_v)

        @pl.loop(0, steps)
        def _(s):
            pltpu.async_copy(x_hbm.at[idx_v.at[s]], rows_v, sem).wait()
            base = pl.multiple_of((wid * steps + s) * SC_WINDOW, SC_WINDOW)
            pltpu.sync_copy(rows_v, o_hbm.at[pl.ds(base, SC_WINDOW)])

    return gather(x, idx3)


def _experts_kernel(blk0_ref, nblk_ref, cnt_ref, nu_ref,
                    x_hbm, wg_ref, wu_ref, wd_ref, y_hbm,
                    xbuf, ybuf, wg_scr, wu_scr, wd_scr, in_sem, out_sem):
    e = pl.program_id(0)
    n_used = nu_ref[0]
    blk0 = blk0_ref[e]

    def x_copy(g, slot):
        return pltpu.make_async_copy(x_hbm.at[pl.ds(pl.multiple_of(g * MOE_BLOCK, MOE_BLOCK), MOE_BLOCK)],
                                     xbuf.at[slot], in_sem.at[slot])

    def y_copy(g, slot):
        return pltpu.make_async_copy(ybuf.at[slot],
                                     y_hbm.at[pl.ds(pl.multiple_of(g * MOE_BLOCK, MOE_BLOCK), MOE_BLOCK)],
                                     out_sem.at[slot])

    @pl.when(e == 0)
    def _():
        x_copy(0, 0).start()

    wg_scr[...] = wg_ref[0].astype(BF16)
    wu_scr[...] = wu_ref[0].astype(BF16)
    wd_scr[...] = wd_ref[0].astype(BF16)

    def block(b, carry):
        g = blk0 + b
        slot = g % 2
        x_copy(g, slot).wait()

        @pl.when(g + 1 < n_used)
        def _():
            x_copy(g + 1, 1 - slot).start()

        rid = lax.broadcasted_iota(jnp.int32, (MOE_BLOCK, xbuf.shape[2]), 0)
        xp = jnp.where(rid < cnt_ref[e] - b * MOE_BLOCK, xbuf[slot], 0)
        x = _unpack_bf16_pairs(xp).astype(BF16)
        gt = _dot(x, wg_scr[...])
        up = _dot(x, wu_scr[...])
        a = (gt * _sigmoid(gt) * up).astype(BF16)
        y = _pack_bf16_pairs(_dot(a, wd_scr[...]))

        @pl.when(g >= 2)
        def _():
            y_copy(g - 2, slot).wait()

        ybuf[slot] = y
        y_copy(g, slot).start()
        return carry

    lax.fori_loop(0, nblk_ref[e], block, 0)

    @pl.when(e == pl.num_programs(0) - 1)
    def _():
        @pl.when(n_used >= 2)
        def _():
            y_copy(n_used - 2, n_used % 2).wait()

        y_copy(n_used - 1, (n_used - 1) % 2).wait()


def _experts(blk0, nblk, counts, n_used, xs, wg, wu, wd):
    p, dp = xs.shape
    ne, d, ff = wg.shape
    grid_spec = pltpu.PrefetchScalarGridSpec(
        num_scalar_prefetch=4,
        grid=(ne,),
        in_specs=[pl.BlockSpec(memory_space=pl.ANY),
                  pl.BlockSpec((1, d, ff), lambda e, *_: (e, 0, 0)),
                  pl.BlockSpec((1, d, ff), lambda e, *_: (e, 0, 0)),
                  pl.BlockSpec((1, ff, d), lambda e, *_: (e, 0, 0))],
        out_specs=pl.BlockSpec(memory_space=pl.ANY),
        scratch_shapes=[pltpu.VMEM((2, MOE_BLOCK, dp), jnp.int32),
                        pltpu.VMEM((2, MOE_BLOCK, dp), jnp.int32),
                        pltpu.VMEM((d, ff), BF16), pltpu.VMEM((d, ff), BF16), pltpu.VMEM((ff, d), BF16),
                        pltpu.SemaphoreType.DMA((2,)), pltpu.SemaphoreType.DMA((2,))],
    )
    return pl.pallas_call(
        _experts_kernel,
        grid_spec=grid_spec,
        out_shape=jax.ShapeDtypeStruct((p, dp), jnp.int32),
        compiler_params=_params("arbitrary"),
        name="experts",
    )(blk0, nblk, counts, n_used, xs, wg, wu, wd)


def _combine_kernel(base_ref, mod_ref, w_ref, y_ref, o_ref):
    acc = None
    for kk in range(TOP_K):
        term = w_ref[:, kk:kk + 1] * _unpack_bf16_pairs(y_ref[kk])
        acc = term if acc is None else acc + term
    o_ref[0] = base_ref[0] + mod_ref[0, 5:6, :] * acc


def _combine(base, mod, w_tk, yg, tm):
    b, t, d = base.shape
    nt = t // tm
    tok = pl.BlockSpec((1, tm, d), lambda bi, i: (bi, i, 0))
    return pl.pallas_call(
        _combine_kernel,
        grid=(b, nt),
        in_specs=[tok, pl.BlockSpec((1, 8, d), lambda bi, i: (bi, 0, 0)),
                  pl.BlockSpec((tm, TOP_K), lambda bi, i: (bi * nt + i, 0)),
                  pl.BlockSpec((TOP_K, tm, d // 2), lambda bi, i: (0, bi * nt + i, 0))],
        out_specs=tok,
        out_shape=jax.ShapeDtypeStruct((b, t, d), F32),
        compiler_params=_params("parallel", "parallel"),
        name="combine",
    )(base, mod, w_tk, yg)


def _rope_tables(t):
    pos = np.arange(t)
    half = M_DQK // 2
    nf = half // 2
    inv = np.power(ROPE_BASE, -np.arange(nf, dtype=np.float32) / nf).astype(np.float32)
    row_pos = jnp.asarray((pos // GRID_W).astype(np.float32))
    col_pos = jnp.asarray((pos % GRID_W).astype(np.float32))
    inv = jnp.asarray(inv)
    ar = row_pos[:, None] * inv[None, :]
    ac = col_pos[:, None] * inv[None, :]
    cos = jnp.concatenate([jnp.cos(ar), jnp.cos(ar), jnp.cos(ac), jnp.cos(ac)], axis=1)
    sin = jnp.concatenate([-jnp.sin(ar), jnp.sin(ar), -jnp.sin(ac), jnp.sin(ac)], axis=1)
    return cos, sin, cos.T, sin.T


def _arrange_w_in(w_in, b_mgate):
    d = w_in.shape[0]
    sizes = (512, 512, 1024, 1024, 16, 512, 512, 512, 1024, 1024)
    offs = np.concatenate([[0], np.cumsum(sizes)])
    mq, mk, mv, mo, mg, nq, nk, nv, gm, gn = [w_in[:, offs[i]:offs[i + 1]] for i in range(10)]
    pad = jnp.zeros((d, LANES - 2 * M_HEADS), w_in.dtype)
    gi = jnp.concatenate([mg[:, 0:4], mg[:, 8:12], pad], axis=1)
    gf = jnp.concatenate([mg[:, 4:8], mg[:, 12:16], pad], axis=1)
    w_all = jnp.concatenate([mq, mv, mo, gf, nq, nk, nv, gm, gn], axis=1).astype(BF16)
    wt_all = jnp.concatenate([mk, gi, gf], axis=1).T.astype(BF16)
    bpad = jnp.zeros((LANES - 2 * M_HEADS,), F32)
    bi = jnp.concatenate([b_mgate[0:4], b_mgate[8:12], bpad])
    bf = jnp.concatenate([b_mgate[4:8], b_mgate[12:16], bpad])
    bg = jnp.concatenate([bf[None, :], jnp.zeros((7, LANES), F32)], axis=0)
    bgt = jnp.concatenate([bi, bf])[:, None]
    return w_all, wt_all, bg, bgt


def _segment_mats():
    na_w = NA_HEADS * NA_DH
    seg = np.zeros((na_w, LANES), np.float32)
    seg[np.arange(na_w), np.arange(na_w) // NA_DH] = 1.0
    return jnp.asarray(seg, BF16), jnp.asarray(seg.T.copy(), BF16)


def kernel(x, c, ctx, c_ctx, w_ada, b_ada, w_in, b_mgate, m_norm_w, na_qn_w, na_kn_w, na_rpb,
           w_br_m, w_br_na, w_out, w_router, router_bias, w_exp_gate, w_exp_up, w_exp_down,
           w_sh_gate, w_sh_up, w_sh_down):
    b, t, d = x.shape
    n = b * t
    rows = t // GRID_W
    l = 0

    cc = jnp.concatenate([c, c_ctx[None, :], jnp.zeros((8 - b - 1, d), F32)], axis=0)
    mod = _ada(cc, w_ada[l], b_ada[l])
    mod = mod.reshape(8, 6, d)
    mod = jnp.concatenate([mod, jnp.zeros((8, 2, d), F32)], axis=1)
    mod_x = mod[:b]
    mod_c = jnp.broadcast_to(mod[b:b + 1], (b, 8, d))

    w_all, wt_all, bg, bgt = _arrange_w_in(w_in[l], b_mgate[l])
    seg, segt = _segment_mats()
    qnw = jnp.tile(na_qn_w[l], NA_HEADS)[None, :]
    knw = jnp.tile(na_kn_w[l], NA_HEADS)[None, :]
    tm = min(512, t)

    cp = _inproj(ctx, mod_c, w_all, wt_all, bg, bgt, qnw, knw, seg, segt, None,
                 min(tm, ctx.shape[1]))
    xp = _inproj(x, mod_x, w_all, wt_all, bg, bgt, qnw, knw, seg, segt, _rope_tables(t), tm)
    cmq, cmv, _, cgf, _, cnk, cnv, _, _, cmkt, cgit, cgft = cp
    mq, mv, mo, gf, nq, nk, nv, gm, gn, mkt, git, gft = xp

    c0 = jnp.zeros((b, 8, M_DQK, MLSTM_EXT), F32)
    m0 = jnp.zeros((b, 8, LANES), F32)
    _, _, c1, m1 = _mlstm(cmq, cmkt, cmv, cgf, cgit, cgft, c0, m0)
    hf, hb, _, _ = _mlstm(mq, mkt, mv, gf, git, gft, c1, m1)

    yna = _na(nq, nk, nv, cnk, cnv, _na_bias_table(na_rpb[l], rows))

    h2p, scores_t, base = _post(
        x, mod_x, hf, hb, mo, yna, gm, gn, m_norm_w[l][None, :],
        w_br_m[l].astype(BF16), w_br_na[l].astype(BF16), w_out[l].astype(BF16),
        w_router[l].T.astype(BF16), w_sh_gate[l].astype(BF16), w_sh_up[l].astype(BF16),
        w_sh_down[l].astype(BF16), tm)

    bias_col = jnp.broadcast_to(router_bias[l][:, None], (N_EXPERTS, LANES))
    top_e, top_w, rank, cnt = _route(scores_t, bias_col, min(512, n))

    counts = cnt[:, 0].astype(jnp.int32)
    padded = (counts + MOE_BLOCK - 1) // MOE_BLOCK * MOE_BLOCK
    pend = jnp.cumsum(padded)
    pstart = pend - padded
    nb = -(-(n * TOP_K) // MOE_BLOCK) + N_EXPERTS
    p_rows = nb * MOE_BLOCK
    n_used = pend[-1:] // MOE_BLOCK

    pstart_col = jnp.broadcast_to(pstart.astype(F32)[:, None], (N_EXPERTS, LANES))
    pos = _slots(top_e, rank, pstart_col, min(512, n))
    xs = _dispatch_rows(h2p, pos, p_rows)
    ys = _experts(pstart // MOE_BLOCK, padded // MOE_BLOCK, counts, n_used, xs,
                  w_exp_gate[l], w_exp_up[l], w_exp_down[l])
    yg = _gather_rows(ys, pos.reshape(-1)).reshape(TOP_K, n, d // 2)
    return _combine(base, mod_x, top_w.T, yg, tm)
```

```python
import functools

import numpy as np
import jax
import jax.numpy as jnp
from jax import lax
from jax.experimental import pallas as pl
from jax.experimental.pallas import tpu as pltpu
from jax.experimental.pallas import tpu_sc as plsc

F32 = jnp.float32
BF16 = jnp.bfloat16

EPS = 1e-6
GRID_W = 64
M_HEADS, M_DQK, M_DV = 4, 128, 256
ROPE_BASE = 10000.0
NA_HEADS, NA_DH, NA_KH, NA_KW = 8, 64, 8, 16
N_EXPERTS, TOP_K, N_GROUPS, TOPK_GROUPS = 256, 8, 8, 4
ROUTE_SCALE = 2.5

LANES = 128
VMEM_LIMIT = 56 * 1024 * 1024
NEG = -1e30

MLSTM_CHUNK = 256
NA_ROWS = 4
NA_KEY_ROWS = NA_ROWS + NA_KH - 1
MOE_BLOCK = 512
EXPERT_SLOTS = 4

_W_SEGS = (("mq", 512), ("mv", 1024), ("mo", 1024), ("gf", 128),
           ("nq", 512), ("nk", 512), ("nv", 512), ("gm", 1024), ("gn", 1024))
_W_OFF = {}
_o = 0
for _n, _w in _W_SEGS:
    _W_OFF[_n] = (_o, _w)
    _o += _w
W_COLS = _o


def _dot(a, b):
    return jnp.dot(a, b, preferred_element_type=F32)


def _dot_nt(a, b):
    return lax.dot_general(a, b, (((1,), (1,)), ((), ())), preferred_element_type=F32)


def _sigmoid(x):
    return 1.0 / (1.0 + jnp.exp(-x))


def _pack_bf16_pairs(v):
    w = v.shape[1] // 2
    bits = pltpu.bitcast(v.astype(BF16).astype(F32), jnp.int32)
    return lax.shift_right_logical(bits[:, :w], 16) | bits[:, w:]


def _unpack_bf16_pairs(p):
    lo = pltpu.bitcast(lax.shift_left(p, 16), F32)
    hi = pltpu.bitcast(p & jnp.int32(-65536), F32)
    return jnp.concatenate([lo, hi], axis=1)


def _params(*sem):
    return pltpu.CompilerParams(dimension_semantics=sem, vmem_limit_bytes=VMEM_LIMIT)


def _resident(shape):
    nd = len(shape)
    return pl.BlockSpec(shape, lambda *_: (0,) * nd, pipeline_mode=pl.Buffered(1))


def _ada_kernel(c_ref, w_ref, b_ref, o_ref):
    c = c_ref[...]
    s = c * _sigmoid(c)
    o_ref[...] = _dot(s.astype(BF16), w_ref[...].astype(BF16)) + b_ref[...]


def _ada(cc, w_ada, b_ada):
    d = cc.shape[1]
    n = w_ada.shape[1]
    return pl.pallas_call(
        _ada_kernel,
        grid=(n // d,),
        in_specs=[pl.BlockSpec((8, d), lambda j: (0, 0)),
                  pl.BlockSpec((d, d), lambda j: (0, j)),
                  pl.BlockSpec((1, d), lambda j: (0, j))],
        out_specs=pl.BlockSpec((8, d), lambda j: (0, j)),
        out_shape=jax.ShapeDtypeStruct((8, n), F32),
        compiler_params=_params("arbitrary"),
        name="ada",
    )(cc, w_ada, b_ada.reshape(1, n))


def _rope_rotate(t, cos, sin):
    lane = lax.broadcasted_iota(jnp.int32, t.shape, 1)
    partner = jnp.where((lane & 32) == 0, pltpu.roll(t, 96, 1), pltpu.roll(t, 32, 1))
    return t * cos + partner * sin


def _rope_rotate_t(t, cos, sin):
    q = M_DQK // 4
    partner = jnp.concatenate([t[q:2 * q], t[0:q], t[3 * q:4 * q], t[2 * q:3 * q]], axis=0)
    return t * cos + partner * sin


def _inproj_kernel(*refs, rope):
    if rope:
        (x_ref, mod_ref, w_ref, wt_ref, bg_ref, bgt_ref, qnw_ref, knw_ref, seg_ref, segt_ref,
         cos_ref, sin_ref, cost_ref, sint_ref,
         mq_ref, mv_ref, mo_ref, gf_ref, nq_ref, nk_ref, nv_ref, gm_ref, gn_ref,
         mkt_ref, git_ref, gft_ref) = refs
    else:
        (x_ref, mod_ref, w_ref, wt_ref, bg_ref, bgt_ref, qnw_ref, knw_ref, seg_ref, segt_ref,
         mq_ref, mv_ref, mo_ref, gf_ref, nq_ref, nk_ref, nv_ref, gm_ref, gn_ref,
         mkt_ref, git_ref, gft_ref) = refs
    x = x_ref[0]
    xn = x * lax.rsqrt(jnp.mean(x * x, axis=-1, keepdims=True) + EPS)
    h = xn * (1.0 + mod_ref[0, 1:2, :]) + mod_ref[0, 0:1, :]
    hb = h.astype(BF16)

    def proj(name):
        off, width = _W_OFF[name]
        return _dot(hb, w_ref[:, off:off + width])

    def head_rms(t, w_row, scale):
        ss = _dot((t * t).astype(BF16), seg_ref[...])
        r = lax.rsqrt(ss * (1.0 / NA_DH) + EPS)
        r_hi = r.astype(BF16)
        r_lo = (r - r_hi.astype(F32)).astype(BF16)
        rb = _dot(r_hi, segt_ref[...]) + _dot(r_lo, segt_ref[...])
        return t * rb * w_row * scale

    mq = proj("mq") * (M_DQK ** -0.5)
    if rope:
        cos, sin = cos_ref[...], sin_ref[...]
        mq = jnp.concatenate([_rope_rotate(mq[:, i * LANES:(i + 1) * LANES], cos, sin)
                              for i in range(M_HEADS)], axis=1)
    mq_ref[0] = mq.astype(BF16)
    mv_ref[0] = proj("mv").astype(BF16)
    mo_ref[0] = _sigmoid(proj("mo")).astype(BF16)
    gf_ref[0] = proj("gf") + bg_ref[0:1, :]
    nq_ref[0] = head_rms(proj("nq"), qnw_ref[...], NA_DH ** -0.5).astype(BF16)
    nk_ref[0] = head_rms(proj("nk"), knw_ref[...], 1.0).astype(BF16)
    nv_ref[0] = proj("nv").astype(BF16)
    gm_ref[0] = _sigmoid(proj("gm")).astype(BF16)
    gn_ref[0] = _sigmoid(proj("gn")).astype(BF16)

    qk_w = M_HEADS * M_DQK
    mkt = _dot_nt(wt_ref[0:qk_w, :], hb)
    if rope:
        cost, sint = cost_ref[...], sint_ref[...]
        mkt = jnp.concatenate([_rope_rotate_t(mkt[i * M_DQK:(i + 1) * M_DQK], cost, sint)
                               for i in range(M_HEADS)], axis=0)
    mkt_ref[0] = mkt.astype(BF16)
    git_ref[0] = _dot_nt(wt_ref[qk_w:qk_w + LANES, :], hb) + bgt_ref[0:LANES, :]
    gft_ref[0] = _dot_nt(wt_ref[qk_w + LANES:qk_w + 2 * LANES, :], hb) + bgt_ref[LANES:2 * LANES, :]


def _inproj(x, mod, w_all, wt_all, bg, bgt, qnw, knw, seg, segt, rope_tabs, tm):
    b, t, d = x.shape
    rope = rope_tabs is not None
    tok = lambda w: pl.BlockSpec((1, tm, w), lambda bi, i: (bi, i, 0))
    tok_t = lambda w: pl.BlockSpec((1, w, tm), lambda bi, i: (bi, 0, i))
    in_specs = [tok(d),
                pl.BlockSpec((1, 8, d), lambda bi, i: (bi, 0, 0)),
                _resident(w_all.shape), _resident(wt_all.shape), _resident(bg.shape),
                _resident(bgt.shape), _resident(qnw.shape),
                _resident(knw.shape), _resident(seg.shape), _resident(segt.shape)]
    args = [x, mod, w_all, wt_all, bg, bgt, qnw, knw, seg, segt]
    if rope:
        in_specs += [pl.BlockSpec((tm, LANES), lambda bi, i: (i, 0))] * 2
        in_specs += [pl.BlockSpec((LANES, tm), lambda bi, i: (0, i))] * 2
        args += list(rope_tabs)
    widths = [("mq", BF16), ("mv", BF16), ("mo", BF16), ("gf", F32),
              ("nq", BF16), ("nk", BF16), ("nv", BF16), ("gm", BF16), ("gn", BF16)]
    out_specs = [tok(_W_OFF[n][1]) for n, _ in widths]
    out_shape = [jax.ShapeDtypeStruct((b, t, _W_OFF[n][1]), dt) for n, dt in widths]
    out_specs += [tok_t(M_HEADS * M_DQK), tok_t(LANES), tok_t(LANES)]
    out_shape += [jax.ShapeDtypeStruct((b, M_HEADS * M_DQK, t), BF16),
                  jax.ShapeDtypeStruct((b, LANES, t), F32),
                  jax.ShapeDtypeStruct((b, LANES, t), F32)]
    return pl.pallas_call(
        functools.partial(_inproj_kernel, rope=rope),
        grid=(b, t // tm),
        in_specs=in_specs, out_specs=out_specs, out_shape=out_shape,
        compiler_params=_params("parallel", "parallel"),
        name="inproj_rope" if rope else "inproj_ctx",
    )(*args)


def _log_sigmoid(x):
    return jnp.minimum(x, 0.0) - jnp.log(1.0 + jnp.exp(-jnp.abs(x)))


def _dot_split(a, b, split_a):
    x = a if split_a else b
    hi = x.astype(BF16)
    lo = (x - hi.astype(F32)).astype(BF16)
    return (_dot(hi, b) + _dot(lo, b)) if split_a else (_dot(a, hi) + _dot(a, lo))


MLSTM_EXT = M_DV + LANES


def _mlstm_kernel(qf_ref, ktf_ref, vf_ref, gff_ref, gitf_ref, gftf_ref,
                  qb_ref, ktb_ref, vb_ref, gfb_ref, gitb_ref, gftb_ref,
                  c0_ref, m0_ref,
                  hf_ref, hb_ref, cn_ref, mn_ref,
                  *scratch):
    c_scrs, m_scr = scratch[:2 * M_HEADS], scratch[2 * M_HEADS]
    step = pl.program_id(1)
    L = qf_ref.shape[1]
    nu = 2 * M_HEADS

    @pl.when(step == 0)
    def _():
        for j, c_scr in enumerate(c_scrs):
            c_scr[...] = c0_ref[0, j]
        m_scr[...] = m0_ref[0]

    row_i = lax.broadcasted_iota(jnp.int32, (L, L), 0)
    col_i = lax.broadcasted_iota(jnp.int32, (L, L), 1)
    lower = col_i <= row_i
    upper = col_i >= row_i
    tri_lo = jnp.where(lower, 1.0, 0.0).astype(BF16)
    tri_up = jnp.where(upper, 1.0, 0.0).astype(BF16)

    is_f = lax.broadcasted_iota(jnp.int32, (nu, L), 0) < M_HEADS
    gi_t = jnp.where(is_f, gitf_ref[0, 0:nu, :], gitb_ref[0, 0:nu, :])
    ls_tf = _log_sigmoid(gftf_ref[0, 0:nu, :])
    ls_tb = _log_sigmoid(gftb_ref[0, 0:nu, :])
    b_t = jnp.where(is_f, _dot_split(ls_tf, tri_up, True), _dot_split(ls_tb, tri_lo, True))
    u_t = gi_t - b_t
    g_c = jnp.sum(jnp.where(is_f, ls_tf, ls_tb), axis=1, keepdims=True)
    m_prev = m_scr[...]
    a_t = g_c + u_t
    m_new = jnp.maximum(g_c + m_prev, jnp.max(a_t, axis=1, keepdims=True))
    decay = jnp.exp(g_c + m_prev - m_new)
    wa_t = jnp.exp(a_t - jnp.concatenate([m_new] * (L // LANES), axis=1))
    m_scr[...] = m_new

    ones = jnp.ones((L, LANES), BF16)
    dirs = ((qf_ref, ktf_ref, vf_ref, gff_ref, hf_ref, lower, tri_lo),
            (qb_ref, ktb_ref, vb_ref, gfb_ref, hb_ref, upper, tri_up))
    for d, (q_ref, kt_ref, v_ref, gf_ref, h_ref, mask, tri) in enumerate(dirs):
        bcum = _dot_split(tri, _log_sigmoid(gf_ref[0]), False)
        for hd in range(M_HEADS):
            j = d * M_HEADS + hd
            c_scr = c_scrs[j]
            q = q_ref[0, :, hd * M_DQK:(hd + 1) * M_DQK]
            k_t = kt_ref[0, hd * M_DQK:(hd + 1) * M_DQK, :]
            v_ext = jnp.concatenate([v_ref[0, :, hd * M_DV:(hd + 1) * M_DV], ones], axis=1)
            u_row = u_t[j:j + 1, :]
            mp_row = m_prev[j:j + 1, :]
            c_prev = c_scr[...]

            m_loc = jnp.max(jnp.where(mask, u_row, NEG), axis=1, keepdims=True)
            m_rep = jnp.maximum(jnp.broadcast_to(m_loc, (L, LANES)), mp_row)
            m_wide = jnp.concatenate([m_rep] * (L // LANES), axis=1)
            dmat = jnp.exp(jnp.where(mask, u_row - m_wide, NEG))
            s = (_dot(q, k_t) * dmat).astype(BF16)
            qw = (q.astype(F32) * jnp.exp(mp_row - m_rep)).astype(BF16)
            r = _dot(s, v_ext) + _dot(qw, c_prev.astype(BF16))
            b_rep = jnp.broadcast_to(bcum[:, j:j + 1], (L, LANES))
            dn = jnp.maximum(jnp.abs(r[:, M_DV:]), jnp.exp(-(b_rep + m_rep)))
            h_ref[0, :, hd * M_DV:(hd + 1) * M_DV] = (
                r[:, :M_DV] / jnp.concatenate([dn] * (M_DV // LANES), axis=1)).astype(h_ref.dtype)

            kw = (k_t.astype(F32) * wa_t[j:j + 1, :]).astype(BF16)
            dec = jnp.concatenate([decay[j:j + 1, :]] * (MLSTM_EXT // LANES), axis=1)
            c_scr[...] = dec * c_prev + _dot(kw, v_ext)

    @pl.when(step == pl.num_programs(1) - 1)
    def _():
        for j, c_scr in enumerate(c_scrs):
            cn_ref[0, j] = c_scr[...]
        mn_ref[0] = m_scr[...]


def _mlstm(q, kt, v, gf, git, gft, c0, m0):
    b, t, _ = q.shape
    L = min(MLSTM_CHUNK, t)
    nc = t // L
    fwd = lambda w: pl.BlockSpec((1, L, w), lambda bi, i: (bi, i, 0))
    bwd = lambda w: pl.BlockSpec((1, L, w), lambda bi, i: (bi, nc - 1 - i, 0))
    fwd_t = lambda w: pl.BlockSpec((1, w, L), lambda bi, i: (bi, 0, i))
    bwd_t = lambda w: pl.BlockSpec((1, w, L), lambda bi, i: (bi, 0, nc - 1 - i))
    st_c = pl.BlockSpec((1, 8, M_DQK, MLSTM_EXT), lambda bi, i: (bi, 0, 0, 0))
    st_v = pl.BlockSpec((1, 8, LANES), lambda bi, i: (bi, 0, 0))
    qk_w, v_w = M_HEADS * M_DQK, M_HEADS * M_DV
    return pl.pallas_call(
        _mlstm_kernel,
        grid=(b, nc),
        in_specs=[fwd(qk_w), fwd_t(qk_w), fwd(v_w), fwd(LANES), fwd_t(LANES), fwd_t(LANES),
                  bwd(qk_w), bwd_t(qk_w), bwd(v_w), bwd(LANES), bwd_t(LANES), bwd_t(LANES),
                  st_c, st_v],
        out_specs=[fwd(v_w), bwd(v_w), st_c, st_v],
        out_shape=[jax.ShapeDtypeStruct((b, t, v_w), BF16),
                   jax.ShapeDtypeStruct((b, t, v_w), BF16),
                   jax.ShapeDtypeStruct(c0.shape, F32),
                   jax.ShapeDtypeStruct(m0.shape, F32)],
        scratch_shapes=([pltpu.VMEM((M_DQK, MLSTM_EXT), F32) for _ in range(2 * M_HEADS)]
                        + [pltpu.VMEM((8, LANES), F32)]),
        compiler_params=_params("parallel", "arbitrary"),
        name="mlstm",
    )(q, kt, v, gf, git, gft, q, kt, v, gf, git, gft, c0, m0)


def _na_kernel(q_ref, k_ref, v_ref, kc_ref, vc_ref, bias_ref, o_ref, *, rows):
    r0 = pl.program_id(2) * NA_ROWS
    ks = jnp.clip(r0 - NA_KH // 2, 0, rows - NA_KEY_ROWS)
    kstart = pl.multiple_of(ks * GRID_W, GRID_W)
    nkeys = NA_KEY_ROWS * GRID_W
    kblk = k_ref[0, pl.ds(kstart, nkeys), :]
    vblk = v_ref[0, pl.ds(kstart, nkeys), :]
    kc = kc_ref[0]
    vc = vc_ref[0]
    q = q_ref[0]
    lane = lax.broadcasted_iota(jnp.int32, q.shape, 1)
    outs = []
    for hh in range(2):
        in_head = (lane < NA_DH) if hh == 0 else (lane >= NA_DH)
        qm = jnp.where(in_head, q, jnp.zeros_like(q))
        sw = _dot_nt(qm, kblk) + bias_ref[hh, 0]
        sc = _dot_nt(qm, kc)
        m = jnp.maximum(jnp.max(sw, axis=1, keepdims=True), jnp.max(sc, axis=1, keepdims=True))
        ew = jnp.exp(sw - m)
        ec = jnp.exp(sc - m)
        l = jnp.sum(ew, axis=1, keepdims=True) + jnp.sum(ec, axis=1, keepdims=True)
        o = _dot(ew.astype(BF16), vblk) + _dot(ec.astype(BF16), vc)
        outs.append(o / l)
    o_ref[0] = jnp.where(lane < NA_DH, outs[0], outs[1]).astype(o_ref.dtype)


def _na(nq, nk, nv, cnk, cnv, bias):
    b, t, w = nq.shape
    rows = t // GRID_W
    tq = NA_ROWS * GRID_W
    nrb = rows // NA_ROWS
    nctx = cnk.shape[1]
    kind = lambda rb: jnp.where(rb == 0, 0, jnp.where(rb == nrb - 1, 2, 1))
    return pl.pallas_call(
        functools.partial(_na_kernel, rows=rows),
        grid=(b, w // LANES, nrb),
        in_specs=[pl.BlockSpec((1, tq, LANES), lambda bi, hp, rb: (bi, rb, hp)),
                  pl.BlockSpec((1, t, LANES), lambda bi, hp, rb: (bi, 0, hp)),
                  pl.BlockSpec((1, t, LANES), lambda bi, hp, rb: (bi, 0, hp)),
                  pl.BlockSpec((1, nctx, LANES), lambda bi, hp, rb: (bi, 0, hp)),
                  pl.BlockSpec((1, nctx, LANES), lambda bi, hp, rb: (bi, 0, hp)),
                  pl.BlockSpec((2, 1, tq, NA_KEY_ROWS * GRID_W),
                               lambda bi, hp, rb: (hp, kind(rb), 0, 0))],
        out_specs=pl.BlockSpec((1, tq, LANES), lambda bi, hp, rb: (bi, rb, hp)),
        out_shape=jax.ShapeDtypeStruct((b, t, w), BF16),
        compiler_params=_params("parallel", "parallel", "arbitrary"),
        name="na",
    )(nq, nk, nv, cnk, cnv, bias)


def _na_bias_table(na_rpb, rows):
    h = na_rpb.shape[0]
    w = GRID_W
    c = np.arange(w)[:, None]
    kj = np.arange(w)[None, :]
    cs = np.clip(c - NA_KW // 2, 0, w - NA_KW)
    col_valid = (kj >= cs) & (kj < cs + NA_KW)
    dc = np.clip(kj - c + (NA_KW - 1), 0, 2 * NA_KW - 2)
    onehot = np.zeros((2 * NA_KW - 1, w, w), np.float32)
    onehot[dc, np.arange(w)[:, None], np.arange(w)[None, :]] = 1.0
    t2 = jnp.einsum("hrd,dck->hrck", na_rpb, jnp.asarray(onehot), precision=lax.Precision.HIGHEST)
    t2 = jnp.where(jnp.asarray(col_valid)[None, None], t2, NEG)
    t2 = jnp.concatenate([t2, jnp.full((h, 1, w, w), NEG, F32)], axis=1)
    invalid = 2 * NA_KH - 1
    dr_idx = np.full((3, NA_ROWS, NA_KEY_ROWS), invalid, np.int32)
    for kind, r0 in enumerate((0, NA_ROWS, rows - NA_ROWS)):
        ks = int(np.clip(r0 - NA_KH // 2, 0, rows - NA_KEY_ROWS))
        for qa in range(NA_ROWS):
            r = r0 + qa
            rs = int(np.clip(r - NA_KH // 2, 0, rows - NA_KH))
            for kl in range(NA_KEY_ROWS):
                ki = ks + kl
                if rs <= ki < rs + NA_KH:
                    dr_idx[kind, qa, kl] = ki - r + NA_KH - 1
    t2t = t2.transpose(0, 2, 1, 3)
    strips = [jnp.concatenate([t2t[:, :, int(dr), :] for dr in dr_idx[kind, qa]], axis=-1)
              for kind in range(3) for qa in range(NA_ROWS)]
    return jnp.stack(strips, axis=1).reshape(h, 3, NA_ROWS * w, NA_KEY_ROWS * w)


def _post_kernel(x_ref, mod_ref, hf_ref, hb_ref, mo_ref, na_ref, gm_ref, gn_ref,
                 mnw_ref, wbm_ref, wbn_ref, wout_ref, wr_ref, wsg_ref, wsu_ref, wsd_ref,
                 h2_ref, st_ref, base_ref):
    hm = hf_ref[0].astype(F32) + hb_ref[0].astype(F32)
    parts = []
    for hd in range(M_HEADS):
        t = hm[:, hd * M_DV:(hd + 1) * M_DV]
        parts.append(t * lax.rsqrt(jnp.mean(t * t, axis=-1, keepdims=True) + EPS))
    y_m = jnp.concatenate(parts, axis=1) * mnw_ref[...] * mo_ref[0].astype(F32)
    a = _dot(y_m.astype(BF16), wbm_ref[...])
    bn = _dot(na_ref[0], wbn_ref[...])
    z = gm_ref[0].astype(F32) * a + gn_ref[0].astype(F32) * bn
    y = _dot(z.astype(BF16), wout_ref[...])
    x1 = x_ref[0] + mod_ref[0, 2:3, :] * y
    xn = x1 * lax.rsqrt(jnp.mean(x1 * x1, axis=-1, keepdims=True) + EPS)
    h2f = xn * (1.0 + mod_ref[0, 4:5, :]) + mod_ref[0, 3:4, :]
    h2_ref[...] = _pack_bf16_pairs(h2f)
    h2 = h2f.astype(BF16)
    st_ref[...] = _sigmoid(_dot_nt(wr_ref[...], h2))
    sh = _dot(h2, wsg_ref[...])
    sh = sh * _sigmoid(sh) * _dot(h2, wsu_ref[...])
    base_ref[0] = x1 + mod_ref[0, 5:6, :] * _dot(sh.astype(BF16), wsd_ref[...])


def _post(x, mod, hf, hb, mo, yna, gm, gn, mnw, wbm, wbn, wout, wr_t, wsg, wsu, wsd, tm):
    b, t, d = x.shape
    nt = t // tm
    tok = lambda w: pl.BlockSpec((1, tm, w), lambda bi, i: (bi, i, 0))
    res = [mnw, wbm, wbn, wout, wr_t, wsg, wsu, wsd]
    return pl.pallas_call(
        _post_kernel,
        grid=(b, nt),
        in_specs=[tok(d), pl.BlockSpec((1, 8, d), lambda bi, i: (bi, 0, 0)),
                  tok(hf.shape[2]), tok(hb.shape[2]), tok(mo.shape[2]), tok(yna.shape[2]),
                  tok(gm.shape[2]), tok(gn.shape[2])] + [_resident(a.shape) for a in res],
        out_specs=[pl.BlockSpec((tm, d // 2), lambda bi, i: (bi * nt + i, 0)),
                   pl.BlockSpec((N_EXPERTS, tm), lambda bi, i: (0, bi * nt + i)),
                   tok(d)],
        out_shape=[jax.ShapeDtypeStruct((b * t, d // 2), jnp.int32),
                   jax.ShapeDtypeStruct((N_EXPERTS, b * t), F32),
                   jax.ShapeDtypeStruct((b, t, d), F32)],
        compiler_params=_params("parallel", "parallel"),
        name="post",
    )(x, mod, hf, hb, mo, yna, gm, gn, *res)


def _route_kernel(s_ref, b_ref, e_ref, w_ref, r_ref, cnt_ref, run_scr):
    @pl.when(pl.program_id(0) == 0)
    def _():
        run_scr[...] = jnp.zeros_like(run_scr)

    s = s_ref[...]
    tm = s.shape[1]
    sel = s + b_ref[...][:, 0:1]
    gsz = N_EXPERTS // N_GROUPS
    ninf = -jnp.inf

    x3 = sel.reshape(N_GROUPS, gsz, tm)
    r3 = lax.broadcasted_iota(jnp.int32, x3.shape, 1)
    m1 = jnp.max(x3, axis=1, keepdims=True)
    i1 = jnp.min(jnp.where(x3 == m1, r3, gsz), axis=1, keepdims=True)
    m2 = jnp.max(jnp.where(r3 == i1, ninf, x3), axis=1)
    gs = m1[:, 0, :] + m2

    gidx = lax.broadcasted_iota(jnp.int32, gs.shape, 0)
    gkeep = jnp.zeros(gs.shape, jnp.bool_)
    cur = gs
    for _ in range(TOPK_GROUPS):
        mm = jnp.max(cur, axis=0, keepdims=True)
        ii = jnp.min(jnp.where(cur == mm, gidx, N_GROUPS), axis=0, keepdims=True)
        hit = gidx == ii
        gkeep = jnp.logical_or(gkeep, hit)
        cur = jnp.where(hit, ninf, cur)
    keep = jnp.broadcast_to(gkeep[:, None, :], x3.shape).reshape(N_EXPERTS, tm)

    row = lax.broadcasted_iota(jnp.int32, s.shape, 0)
    cur = jnp.where(keep, sel, ninf)
    idxs, ws = [], []
    chosen = jnp.zeros(s.shape, jnp.bool_)
    for _ in range(TOP_K):
        mm = jnp.max(cur, axis=0, keepdims=True)
        ii = jnp.min(jnp.where(cur == mm, row, N_EXPERTS), axis=0, keepdims=True)
        hit = row == ii
        idxs.append(ii)
        ws.append(jnp.sum(jnp.where(hit, s, 0.0), axis=0, keepdims=True))
        chosen = jnp.logical_or(chosen, hit)
        cur = jnp.where(hit, ninf, cur)
    wsum = ws[0]
    for wk in ws[1:]:
        wsum = wsum + wk

    chosen_f = jnp.where(chosen, 1.0, 0.0)
    tp = lax.broadcasted_iota(jnp.int32, (tm, tm), 0)
    tc = lax.broadcasted_iota(jnp.int32, (tm, tm), 1)
    before = jnp.where(tp < tc, 1.0, 0.0).astype(BF16)
    rank = _dot(chosen_f.astype(BF16), before) + run_scr[...][:, 0:1]
    run_scr[...] = run_scr[...] + jnp.sum(chosen_f, axis=1, keepdims=True)
    cnt_ref[...] = run_scr[...]

    for kk in range(TOP_K):
        e_ref[kk:kk + 1, :] = idxs[kk]
        w_ref[kk:kk + 1, :] = ws[kk] / wsum * ROUTE_SCALE
        r_ref[kk:kk + 1, :] = jnp.sum(jnp.where(row == idxs[kk], rank, 0.0), axis=0,
                                      keepdims=True).astype(jnp.int32)


def _route(scores_t, bias, tm):
    e, n = scores_t.shape
    blk = lambda: pl.BlockSpec((TOP_K, tm), lambda i: (0, i))
    return pl.pallas_call(
        _route_kernel,
        grid=(n // tm,),
        in_specs=[pl.BlockSpec((e, tm), lambda i: (0, i)), _resident(bias.shape)],
        out_specs=[blk(), blk(), blk(), pl.BlockSpec((e, LANES), lambda i: (0, 0))],
        out_shape=[jax.ShapeDtypeStruct((TOP_K, n), jnp.int32),
                   jax.ShapeDtypeStruct((TOP_K, n), F32),
                   jax.ShapeDtypeStruct((TOP_K, n), jnp.int32),
                   jax.ShapeDtypeStruct((e, LANES), F32)],
        scratch_shapes=[pltpu.VMEM((e, LANES), F32)],
        compiler_params=_params("arbitrary"),
        name="route",
    )(scores_t, bias)


def _slot_kernel(e_ref, r_ref, ps_ref, o_ref):
    row = lax.broadcasted_iota(jnp.int32, (N_EXPERTS, e_ref.shape[1]), 0)
    ps = ps_ref[...][:, 0:1]
    for kk in range(TOP_K):
        first = jnp.sum(jnp.where(row == e_ref[kk:kk + 1, :], ps, 0.0), axis=0, keepdims=True)
        o_ref[kk:kk + 1, :] = first.astype(jnp.int32) + r_ref[kk:kk + 1, :]


def _slots(top_e, rank, pstart, tm):
    k, n = top_e.shape
    blk = pl.BlockSpec((k, tm), lambda i: (0, i))
    return pl.pallas_call(
        _slot_kernel,
        grid=(n // tm,),
        in_specs=[blk, blk, _resident(pstart.shape)],
        out_specs=blk,
        out_shape=jax.ShapeDtypeStruct((k, n), jnp.int32),
        compiler_params=_params("parallel"),
        name="slots",
    )(top_e, rank, pstart)


SC_WINDOW = 128


def _sc_mesh():
    return plsc.VectorSubcoreMesh(core_axis_name="core", subcore_axis_name="subcore")


def _sc_workers():
    info = plsc.get_sparse_core_info()
    return info.num_cores, info.num_cores * info.num_subcores


def _dispatch_rows(x, pos, p_rows):
    n, w = x.shape
    kk = pos.shape[0]
    ncores, nw = _sc_workers()
    steps = n // nw // SC_WINDOW
    pos4 = pos.reshape(kk, nw, steps, SC_WINDOW).transpose(1, 2, 0, 3)

    @functools.partial(
        pl.kernel, mesh=_sc_mesh(),
        out_type=jax.ShapeDtypeStruct((p_rows, w), x.dtype),
        scratch_types=[pltpu.VMEM((kk, SC_WINDOW), jnp.int32),
                       pltpu.VMEM((SC_WINDOW, w), x.dtype),
                       pltpu.SemaphoreType.DMA],
    )
    def scatter(x_hbm, i_hbm, o_hbm, idx_v, rows_v, sem):
        wid = lax.axis_index("subcore") * ncores + lax.axis_index("core")

        @pl.loop(0, steps)
        def _(s):
            base = pl.multiple_of((wid * steps + s) * SC_WINDOW, SC_WINDOW)
            pltpu.sync_copy(i_hbm.at[wid, s], idx_v)
            pltpu.sync_copy(x_hbm.at[pl.ds(base, SC_WINDOW)], rows_v)
            copies = [pltpu.make_async_copy(rows_v, o_hbm.at[idx_v.at[j]], sem) for j in range(kk)]
            for cp in copies:
                cp.start()
            for cp in copies:
                cp.wait()

    return scatter(x, pos4)


def _gather_rows(x, idx):
    m = idx.shape[0]
    w = x.shape[1]
    ncores, nw = _sc_workers()
    steps = m // nw // SC_WINDOW
    idx3 = idx.reshape(nw, steps, SC_WINDOW)

    @functools.partial(
        pl.kernel, mesh=_sc_mesh(),
        out_type=jax.ShapeDtypeStruct((m, w), x.dtype),
        scratch_types=[pltpu.VMEM((steps, SC_WINDOW), jnp.int32),
                       pltpu.VMEM((SC_WINDOW, w), x.dtype),
                       pltpu.SemaphoreType.DMA],
    )
    def gather(x_hbm, i_hbm, o_hbm, idx_v, rows_v, sem):
        wid = lax.axis_index("subcore") * ncores + lax.axis_index("core")
        pltpu.sync_copy(i_hbm.at[wid], idx_v)

        @pl.loop(0, steps)
        def _(s):
            pltpu.async_copy(x_hbm.at[idx_v.at[s]], rows_v, sem).wait()
            base = pl.multiple_of((wid * steps + s) * SC_WINDOW, SC_WINDOW)
            pltpu.sync_copy(rows_v, o_hbm.at[pl.ds(base, SC_WINDOW)])

    return gather(x, idx3)


def _experts_kernel(blk0_ref, nblk_ref, cnt_ref, nu_ref,
                    x_hbm, wg_ref, wu_ref, wd_ref, y_hbm,
                    xbuf, ybuf, wg_scr, wu_scr, wd_scr, in_sem, out_sem):
    e = pl.program_id(0)
    n_used = nu_ref[0]
    blk0 = blk0_ref[e]
    ns = EXPERT_SLOTS

    def x_copy(g):
        return pltpu.make_async_copy(x_hbm.at[pl.ds(pl.multiple_of(g * MOE_BLOCK, MOE_BLOCK), MOE_BLOCK)],
                                     xbuf.at[g % ns], in_sem.at[g % ns])

    def y_copy(g):
        return pltpu.make_async_copy(ybuf.at[g % ns],
                                     y_hbm.at[pl.ds(pl.multiple_of(g * MOE_BLOCK, MOE_BLOCK), MOE_BLOCK)],
                                     out_sem.at[g % ns])

    @pl.when(e == 0)
    def _():
        for g0 in range(ns - 1):
            @pl.when(g0 < n_used)
            def _():
                x_copy(g0).start()

    wg_scr[...] = wg_ref[0].astype(BF16)
    wu_scr[...] = wu_ref[0].astype(BF16)
    wd_scr[...] = wd_ref[0].astype(BF16)

    def block(b, carry):
        g = blk0 + b
        x_copy(g).wait()

        @pl.when(g + ns - 1 < n_used)
        def _():
            x_copy(g + ns - 1).start()

        rid = lax.broadcasted_iota(jnp.int32, (MOE_BLOCK, xbuf.shape[2]), 0)
        xp = jnp.where(rid < cnt_ref[e] - b * MOE_BLOCK, xbuf[g % ns], 0)
        x = _unpack_bf16_pairs(xp).astype(BF16)
        gt = _dot(x, wg_scr[...])
        up = _dot(x, wu_scr[...])
        a = (gt * _sigmoid(gt) * up).astype(BF16)
        y = _pack_bf16_pairs(_dot(a, wd_scr[...]))

        @pl.when(g >= ns)
        def _():
            y_copy(g - ns).wait()

        ybuf[g % ns] = y
        y_copy(g).start()
        return carry

    lax.fori_loop(0, nblk_ref[e], block, 0)

    @pl.when(e == pl.num_programs(0) - 1)
    def _():
        for back in range(ns, 0, -1):
            @pl.when(n_used >= back)
            def _():
                y_copy(n_used - back).wait()


def _experts(blk0, nblk, counts, n_used, xs, wg, wu, wd):
    p, dp = xs.shape
    ne, d, ff = wg.shape
    grid_spec = pltpu.PrefetchScalarGridSpec(
        num_scalar_prefetch=4,
        grid=(ne,),
        in_specs=[pl.BlockSpec(memory_space=pl.ANY),
                  pl.BlockSpec((1, d, ff), lambda e, *_: (e, 0, 0)),
                  pl.BlockSpec((1, d, ff), lambda e, *_: (e, 0, 0)),
                  pl.BlockSpec((1, ff, d), lambda e, *_: (e, 0, 0))],
        out_specs=pl.BlockSpec(memory_space=pl.ANY),
        scratch_shapes=[pltpu.VMEM((EXPERT_SLOTS, MOE_BLOCK, dp), jnp.int32),
                        pltpu.VMEM((EXPERT_SLOTS, MOE_BLOCK, dp), jnp.int32),
                        pltpu.VMEM((d, ff), BF16), pltpu.VMEM((d, ff), BF16), pltpu.VMEM((ff, d), BF16),
                        pltpu.SemaphoreType.DMA((EXPERT_SLOTS,)),
                        pltpu.SemaphoreType.DMA((EXPERT_SLOTS,))],
    )
    return pl.pallas_call(
        _experts_kernel,
        grid_spec=grid_spec,
        out_shape=jax.ShapeDtypeStruct((p, dp), jnp.int32),
        compiler_params=_params("arbitrary"),
        name="experts",
    )(blk0, nblk, counts, n_used, xs, wg, wu, wd)


def _combine_kernel(base_ref, mod_ref, w_ref, y_ref, o_ref):
    acc = None
    for kk in range(TOP_K):
        term = w_ref[:, kk:kk + 1] * _unpack_bf16_pairs(y_ref[kk])
        acc = term if acc is None else acc + term
    o_ref[0] = base_ref[0] + mod_ref[0, 5:6, :] * acc


def _combine(base, mod, w_tk, yg, tm):
    b, t, d = base.shape
    nt = t // tm
    tok = pl.BlockSpec((1, tm, d), lambda bi, i: (bi, i, 0))
    return pl.pallas_call(
        _combine_kernel,
        grid=(b, nt),
        in_specs=[tok, pl.BlockSpec((1, 8, d), lambda bi, i: (bi, 0, 0)),
                  pl.BlockSpec((tm, TOP_K), lambda bi, i: (bi * nt + i, 0)),
                  pl.BlockSpec((TOP_K, tm, d // 2), lambda bi, i: (0, bi * nt + i, 0))],
        out_specs=tok,
        out_shape=jax.ShapeDtypeStruct((b, t, d), F32),
        compiler_params=_params("parallel", "parallel"),
        name="combine",
    )(base, mod, w_tk, yg)


def _rope_tables(t):
    pos = np.arange(t)
    half = M_DQK // 2
    nf = half // 2
    inv = np.power(ROPE_BASE, -np.arange(nf, dtype=np.float32) / nf).astype(np.float32)
    row_pos = jnp.asarray((pos // GRID_W).astype(np.float32))
    col_pos = jnp.asarray((pos % GRID_W).astype(np.float32))
    inv = jnp.asarray(inv)
    ar = row_pos[:, None] * inv[None, :]
    ac = col_pos[:, None] * inv[None, :]
    cos = jnp.concatenate([jnp.cos(ar), jnp.cos(ar), jnp.cos(ac), jnp.cos(ac)], axis=1)
    sin = jnp.concatenate([-jnp.sin(ar), jnp.sin(ar), -jnp.sin(ac), jnp.sin(ac)], axis=1)
    return cos, sin, cos.T, sin.T


def _arrange_w_in(w_in, b_mgate):
    d = w_in.shape[0]
    sizes = (512, 512, 1024, 1024, 16, 512, 512, 512, 1024, 1024)
    offs = np.concatenate([[0], np.cumsum(sizes)])
    mq, mk, mv, mo, mg, nq, nk, nv, gm, gn = [w_in[:, offs[i]:offs[i + 1]] for i in range(10)]
    pad = jnp.zeros((d, LANES - 2 * M_HEADS), w_in.dtype)
    gi = jnp.concatenate([mg[:, 0:4], mg[:, 8:12], pad], axis=1)
    gf = jnp.concatenate([mg[:, 4:8], mg[:, 12:16], pad], axis=1)
    w_all = jnp.concatenate([mq, mv, mo, gf, nq, nk, nv, gm, gn], axis=1).astype(BF16)
    wt_all = jnp.concatenate([mk, gi, gf], axis=1).T.astype(BF16)
    bpad = jnp.zeros((LANES - 2 * M_HEADS,), F32)
    bi = jnp.concatenate([b_mgate[0:4], b_mgate[8:12], bpad])
    bf = jnp.concatenate([b_mgate[4:8], b_mgate[12:16], bpad])
    bg = jnp.concatenate([bf[None, :], jnp.zeros((7, LANES), F32)], axis=0)
    bgt = jnp.concatenate([bi, bf])[:, None]
    return w_all, wt_all, bg, bgt


def _segment_mats():
    na_w = NA_HEADS * NA_DH
    seg = np.zeros((na_w, LANES), np.float32)
    seg[np.arange(na_w), np.arange(na_w) // NA_DH] = 1.0
    return jnp.asarray(seg, BF16), jnp.asarray(seg.T.copy(), BF16)


def kernel(x, c, ctx, c_ctx, w_ada, b_ada, w_in, b_mgate, m_norm_w, na_qn_w, na_kn_w, na_rpb,
           w_br_m, w_br_na, w_out, w_router, router_bias, w_exp_gate, w_exp_up, w_exp_down,
           w_sh_gate, w_sh_up, w_sh_down):
    b, t, d = x.shape
    n = b * t
    rows = t // GRID_W
    l = 0

    cc = jnp.concatenate([c, c_ctx[None, :], jnp.zeros((8 - b - 1, d), F32)], axis=0)
    mod = _ada(cc, w_ada[l], b_ada[l])
    mod = mod.reshape(8, 6, d)
    mod = jnp.concatenate([mod, jnp.zeros((8, 2, d), F32)], axis=1)
    mod_x = mod[:b]
    mod_c = jnp.broadcast_to(mod[b:b + 1], (b, 8, d))

    w_all, wt_all, bg, bgt = _arrange_w_in(w_in[l], b_mgate[l])
    seg, segt = _segment_mats()
    qnw = jnp.tile(na_qn_w[l], NA_HEADS)[None, :]
    knw = jnp.tile(na_kn_w[l], NA_HEADS)[None, :]
    tm = min(512, t)

    cp = _inproj(ctx, mod_c, w_all, wt_all, bg, bgt, qnw, knw, seg, segt, None,
                 min(tm, ctx.shape[1]))
    xp = _inproj(x, mod_x, w_all, wt_all, bg, bgt, qnw, knw, seg, segt, _rope_tables(t), tm)
    cmq, cmv, _, cgf, _, cnk, cnv, _, _, cmkt, cgit, cgft = cp
    mq, mv, mo, gf, nq, nk, nv, gm, gn, mkt, git, gft = xp

    c0 = jnp.zeros((b, 8, M_DQK, MLSTM_EXT), F32)
    m0 = jnp.zeros((b, 8, LANES), F32)
    _, _, c1, m1 = _mlstm(cmq, cmkt, cmv, cgf, cgit, cgft, c0, m0)
    hf, hb, _, _ = _mlstm(mq, mkt, mv, gf, git, gft, c1, m1)

    yna = _na(nq, nk, nv, cnk, cnv, _na_bias_table(na_rpb[l], rows))

    h2p, scores_t, base = _post(
        x, mod_x, hf, hb, mo, yna, gm, gn, m_norm_w[l][None, :],
        w_br_m[l].astype(BF16), w_br_na[l].astype(BF16), w_out[l].astype(BF16),
        w_router[l].T.astype(BF16), w_sh_gate[l].astype(BF16), w_sh_up[l].astype(BF16),
        w_sh_down[l].astype(BF16), tm)

    bias_col = jnp.broadcast_to(router_bias[l][:, None], (N_EXPERTS, LANES))
    top_e, top_w, rank, cnt = _route(scores_t, bias_col, min(512, n))

    counts = cnt[:, 0].astype(jnp.int32)
    padded = (counts + MOE_BLOCK - 1) // MOE_BLOCK * MOE_BLOCK
    pend = jnp.cumsum(padded)
    pstart = pend - padded
    nb = -(-(n * TOP_K) // MOE_BLOCK) + N_EXPERTS
    p_rows = nb * MOE_BLOCK
    n_used = pend[-1:] // MOE_BLOCK

    pstart_col = jnp.broadcast_to(pstart.astype(F32)[:, None], (N_EXPERTS, LANES))
    pos = _slots(top_e, rank, pstart_col, min(512, n))
    xs = _dispatch_rows(h2p, pos, p_rows)
    ys = _experts(pstart // MOE_BLOCK, padded // MOE_BLOCK, counts, n_used, xs,
                  w_exp_gate[l], w_exp_up[l], w_exp_down[l])
    yg = _gather_rows(ys, pos.reshape(-1)).reshape(TOP_K, n, d // 2)
    return _combine(base, mod_x, top_w.T, yg, tm)
```

```python
import functools

import numpy as np
import jax
import jax.numpy as jnp
from jax import lax
from jax.experimental import pallas as pl
from jax.experimental.pallas import tpu as pltpu
from jax.experimental.pallas import tpu_sc as plsc

F32 = jnp.float32
BF16 = jnp.bfloat16

EPS = 1e-6
GRID_W = 64
M_HEADS, M_DQK, M_DV = 4, 128, 256
ROPE_BASE = 10000.0
NA_HEADS, NA_DH, NA_KH, NA_KW = 8, 64, 8, 16
N_EXPERTS, TOP_K, N_GROUPS, TOPK_GROUPS = 256, 8, 8, 4
ROUTE_SCALE = 2.5

LANES = 128
VMEM_LIMIT = 56 * 1024 * 1024
NEG = -1e30

MLSTM_CHUNK = 256
NA_ROWS = 4
NA_KEY_ROWS = NA_ROWS + NA_KH - 1
MOE_BLOCK = 512
EXPERT_SLOTS = 4

_W_SEGS = (("mq", 512), ("mv", 1024), ("mo", 1024), ("gf", 128),
           ("nq", 512), ("nk", 512), ("nv", 512), ("gm", 1024), ("gn", 1024))
_W_OFF = {}
_o = 0
for _n, _w in _W_SEGS:
    _W_OFF[_n] = (_o, _w)
    _o += _w
W_COLS = _o


def _dot(a, b):
    return jnp.dot(a, b, preferred_element_type=F32)


def _dot_nt(a, b):
    return lax.dot_general(a, b, (((1,), (1,)), ((), ())), preferred_element_type=F32)


def _sigmoid(x):
    return 1.0 / (1.0 + jnp.exp(-x))


def _pack_bf16_pairs(v):
    w = v.shape[1] // 2
    bits = pltpu.bitcast(v.astype(BF16).astype(F32), jnp.int32)
    return lax.shift_right_logical(bits[:, :w], 16) | bits[:, w:]


def _unpack_bf16_pairs(p):
    lo = pltpu.bitcast(lax.shift_left(p, 16), F32)
    hi = pltpu.bitcast(p & jnp.int32(-65536), F32)
    return jnp.concatenate([lo, hi], axis=1)


def _params(*sem):
    return pltpu.CompilerParams(dimension_semantics=sem, vmem_limit_bytes=VMEM_LIMIT)


def _resident(shape):
    nd = len(shape)
    return pl.BlockSpec(shape, lambda *_: (0,) * nd, pipeline_mode=pl.Buffered(1))


def _ada_kernel(c_ref, w_ref, b_ref, o_ref):
    c = c_ref[...]
    s = c * _sigmoid(c)
    o_ref[...] = _dot(s.astype(BF16), w_ref[...].astype(BF16)) + b_ref[...]


def _ada(cc, w_ada, b_ada):
    d = cc.shape[1]
    n = w_ada.shape[1]
    return pl.pallas_call(
        _ada_kernel,
        grid=(n // d,),
        in_specs=[pl.BlockSpec((8, d), lambda j: (0, 0)),
                  pl.BlockSpec((d, d), lambda j: (0, j)),
                  pl.BlockSpec((1, d), lambda j: (0, j))],
        out_specs=pl.BlockSpec((8, d), lambda j: (0, j)),
        out_shape=jax.ShapeDtypeStruct((8, n), F32),
        compiler_params=_params("arbitrary"),
        name="ada",
    )(cc, w_ada, b_ada.reshape(1, n))


def _rope_rotate(t, cos, sin):
    lane = lax.broadcasted_iota(jnp.int32, t.shape, 1)
    partner = jnp.where((lane & 32) == 0, pltpu.roll(t, 96, 1), pltpu.roll(t, 32, 1))
    return t * cos + partner * sin


def _rope_rotate_t(t, cos, sin):
    q = M_DQK // 4
    partner = jnp.concatenate([t[q:2 * q], t[0:q], t[3 * q:4 * q], t[2 * q:3 * q]], axis=0)
    return t * cos + partner * sin


def _inproj_kernel(*refs, rope):
    if rope:
        (x_ref, mod_ref, w_ref, wt_ref, bg_ref, bgt_ref, qnw_ref, knw_ref, seg_ref, segt_ref,
         cos_ref, sin_ref, cost_ref, sint_ref,
         mq_ref, mv_ref, mo_ref, gf_ref, nq_ref, nk_ref, nv_ref, gm_ref, gn_ref,
         mkt_ref, git_ref, gft_ref) = refs
    else:
        (x_ref, mod_ref, w_ref, wt_ref, bg_ref, bgt_ref, qnw_ref, knw_ref, seg_ref, segt_ref,
         mq_ref, mv_ref, mo_ref, gf_ref, nq_ref, nk_ref, nv_ref, gm_ref, gn_ref,
         mkt_ref, git_ref, gft_ref) = refs
    x = x_ref[0]
    xn = x * lax.rsqrt(jnp.mean(x * x, axis=-1, keepdims=True) + EPS)
    h = xn * (1.0 + mod_ref[0, 1:2, :]) + mod_ref[0, 0:1, :]
    hb = h.astype(BF16)

    def proj(name):
        off, width = _W_OFF[name]
        return _dot(hb, w_ref[:, off:off + width])

    def head_rms(t, w_row, scale):
        ss = _dot((t * t).astype(BF16), seg_ref[...])
        r = lax.rsqrt(ss * (1.0 / NA_DH) + EPS)
        r_hi = r.astype(BF16)
        r_lo = (r - r_hi.astype(F32)).astype(BF16)
        rb = _dot(r_hi, segt_ref[...]) + _dot(r_lo, segt_ref[...])
        return t * rb * w_row * scale

    mq = proj("mq") * (M_DQK ** -0.5)
    if rope:
        cos, sin = cos_ref[...], sin_ref[...]
        mq = jnp.concatenate([_rope_rotate(mq[:, i * LANES:(i + 1) * LANES], cos, sin)
                              for i in range(M_HEADS)], axis=1)
    mq_ref[0] = mq.astype(BF16)
    mv_ref[0] = proj("mv").astype(BF16)
    mo_ref[0] = _sigmoid(proj("mo")).astype(BF16)
    gf_ref[0] = proj("gf") + bg_ref[0:1, :]
    nq_ref[0] = head_rms(proj("nq"), qnw_ref[...], NA_DH ** -0.5).astype(BF16)
    nk_ref[0] = head_rms(proj("nk"), knw_ref[...], 1.0).astype(BF16)
    nv_ref[0] = proj("nv").astype(BF16)
    gm_ref[0] = _sigmoid(proj("gm")).astype(BF16)
    gn_ref[0] = _sigmoid(proj("gn")).astype(BF16)

    qk_w = M_HEADS * M_DQK
    mkt = _dot_nt(wt_ref[0:qk_w, :], hb)
    if rope:
        cost, sint = cost_ref[...], sint_ref[...]
        mkt = jnp.concatenate([_rope_rotate_t(mkt[i * M_DQK:(i + 1) * M_DQK], cost, sint)
                               for i in range(M_HEADS)], axis=0)
    mkt_ref[0] = mkt.astype(BF16)
    git_ref[0] = _dot_nt(wt_ref[qk_w:qk_w + LANES, :], hb) + bgt_ref[0:LANES, :]
    gft_ref[0] = _dot_nt(wt_ref[qk_w + LANES:qk_w + 2 * LANES, :], hb) + bgt_ref[LANES:2 * LANES, :]


def _inproj(x, mod, w_all, wt_all, bg, bgt, qnw, knw, seg, segt, rope_tabs, tm):
    b, t, d = x.shape
    rope = rope_tabs is not None
    tok = lambda w: pl.BlockSpec((1, tm, w), lambda bi, i: (bi, i, 0))
    tok_t = lambda w: pl.BlockSpec((1, w, tm), lambda bi, i: (bi, 0, i))
    in_specs = [tok(d),
                pl.BlockSpec((1, 8, d), lambda bi, i: (bi, 0, 0)),
                _resident(w_all.shape), _resident(wt_all.shape), _resident(bg.shape),
                _resident(bgt.shape), _resident(qnw.shape),
                _resident(knw.shape), _resident(seg.shape), _resident(segt.shape)]
    args = [x, mod, w_all, wt_all, bg, bgt, qnw, knw, seg, segt]
    if rope:
        in_specs += [pl.BlockSpec((tm, LANES), lambda bi, i: (i, 0))] * 2
        in_specs += [pl.BlockSpec((LANES, tm), lambda bi, i: (0, i))] * 2
        args += list(rope_tabs)
    widths = [("mq", BF16), ("mv", BF16), ("mo", BF16), ("gf", F32),
              ("nq", BF16), ("nk", BF16), ("nv", BF16), ("gm", BF16), ("gn", BF16)]
    out_specs = [tok(_W_OFF[n][1]) for n, _ in widths]
    out_shape = [jax.ShapeDtypeStruct((b, t, _W_OFF[n][1]), dt) for n, dt in widths]
    out_specs += [tok_t(M_HEADS * M_DQK), tok_t(LANES), tok_t(LANES)]
    out_shape += [jax.ShapeDtypeStruct((b, M_HEADS * M_DQK, t), BF16),
                  jax.ShapeDtypeStruct((b, LANES, t), F32),
                  jax.ShapeDtypeStruct((b, LANES, t), F32)]
    return pl.pallas_call(
        functools.partial(_inproj_kernel, rope=rope),
        grid=(b, t // tm),
        in_specs=in_specs, out_specs=out_specs, out_shape=out_shape,
        compiler_params=_params("parallel", "parallel"),
        name="inproj_rope" if rope else "inproj_ctx",
    )(*args)


def _log_sigmoid(x):
    return jnp.minimum(x, 0.0) - jnp.log(1.0 + jnp.exp(-jnp.abs(x)))


def _dot_split(a, b, split_a):
    x = a if split_a else b
    hi = x.astype(BF16)
    lo = (x - hi.astype(F32)).astype(BF16)
    return (_dot(hi, b) + _dot(lo, b)) if split_a else (_dot(a, hi) + _dot(a, lo))


MLSTM_EXT = M_DV + LANES


def _mlstm_kernel(qf_ref, ktf_ref, vf_ref, gff_ref, gitf_ref, gftf_ref,
                  qb_ref, ktb_ref, vb_ref, gfb_ref, gitb_ref, gftb_ref,
                  c0_ref, m0_ref,
                  hf_ref, hb_ref, cn_ref, mn_ref,
                  *scratch):
    c_scrs, m_scr = scratch[:2 * M_HEADS], scratch[2 * M_HEADS]
    step = pl.program_id(1)
    L = qf_ref.shape[1]
    nu = 2 * M_HEADS

    @pl.when(step == 0)
    def _():
        for j, c_scr in enumerate(c_scrs):
            c_scr[...] = c0_ref[0, j]
        m_scr[...] = m0_ref[0]

    row_i = lax.broadcasted_iota(jnp.int32, (L, L), 0)
    col_i = lax.broadcasted_iota(jnp.int32, (L, L), 1)
    lower = col_i <= row_i
    upper = col_i >= row_i
    tri_lo = jnp.where(lower, 1.0, 0.0).astype(BF16)
    tri_up = jnp.where(upper, 1.0, 0.0).astype(BF16)

    is_f = lax.broadcasted_iota(jnp.int32, (nu, L), 0) < M_HEADS
    gi_t = jnp.where(is_f, gitf_ref[0, 0:nu, :], gitb_ref[0, 0:nu, :])
    ls_tf = _log_sigmoid(gftf_ref[0, 0:nu, :])
    ls_tb = _log_sigmoid(gftb_ref[0, 0:nu, :])
    b_t = jnp.where(is_f, _dot_split(ls_tf, tri_up, True), _dot_split(ls_tb, tri_lo, True))
    u_t = gi_t - b_t
    g_c = jnp.sum(jnp.where(is_f, ls_tf, ls_tb), axis=1, keepdims=True)
    m_prev = m_scr[...]
    a_t = g_c + u_t
    m_new = jnp.maximum(g_c + m_prev, jnp.max(a_t, axis=1, keepdims=True))
    decay = jnp.exp(g_c + m_prev - m_new)
    wa_t = jnp.exp(a_t - jnp.concatenate([m_new] * (L // LANES), axis=1))
    m_scr[...] = m_new

    ones = jnp.ones((L, LANES), BF16)
    dirs = ((qf_ref, ktf_ref, vf_ref, gff_ref, hf_ref, lower, tri_lo),
            (qb_ref, ktb_ref, vb_ref, gfb_ref, hb_ref, upper, tri_up))
    for d, (q_ref, kt_ref, v_ref, gf_ref, h_ref, mask, tri) in enumerate(dirs):
        bcum = _dot_split(tri, _log_sigmoid(gf_ref[0]), False)
        for hd in range(M_HEADS):
            j = d * M_HEADS + hd
            c_scr = c_scrs[j]
            q = q_ref[0, :, hd * M_DQK:(hd + 1) * M_DQK]
            k_t = kt_ref[0, hd * M_DQK:(hd + 1) * M_DQK, :]
            v_ext = jnp.concatenate([v_ref[0, :, hd * M_DV:(hd + 1) * M_DV], ones], axis=1)
            u_row = u_t[j:j + 1, :]
            mp_row = m_prev[j:j + 1, :]
            c_prev = c_scr[...]

            m_loc = jnp.max(jnp.where(mask, u_row, NEG), axis=1, keepdims=True)
            m_rep = jnp.maximum(jnp.broadcast_to(m_loc, (L, LANES)), mp_row)
            m_wide = jnp.concatenate([m_rep] * (L // LANES), axis=1)
            dmat = jnp.exp(jnp.where(mask, u_row - m_wide, NEG))
            s = (_dot(q, k_t) * dmat).astype(BF16)
            qw = (q.astype(F32) * jnp.exp(mp_row - m_rep)).astype(BF16)
            r = _dot(s, v_ext) + _dot(qw, c_prev.astype(BF16))
            b_rep = jnp.broadcast_to(bcum[:, j:j + 1], (L, LANES))
            dn = jnp.maximum(jnp.abs(r[:, M_DV:]), jnp.exp(-(b_rep + m_rep)))
            h_ref[0, :, hd * M_DV:(hd + 1) * M_DV] = (
                r[:, :M_DV] / jnp.concatenate([dn] * (M_DV // LANES), axis=1)).astype(h_ref.dtype)

            kw = (k_t.astype(F32) * wa_t[j:j + 1, :]).astype(BF16)
            dec = jnp.concatenate([decay[j:j + 1, :]] * (MLSTM_EXT // LANES), axis=1)
            c_scr[...] = dec * c_prev + _dot(kw, v_ext)

    @pl.when(step == pl.num_programs(1) - 1)
    def _():
        for j, c_scr in enumerate(c_scrs):
            cn_ref[0, j] = c_scr[...]
        mn_ref[0] = m_scr[...]


def _mlstm(q, kt, v, gf, git, gft, c0, m0):
    b, t, _ = q.shape
    L = min(MLSTM_CHUNK, t)
    nc = t // L
    fwd = lambda w: pl.BlockSpec((1, L, w), lambda bi, i: (bi, i, 0))
    bwd = lambda w: pl.BlockSpec((1, L, w), lambda bi, i: (bi, nc - 1 - i, 0))
    fwd_t = lambda w: pl.BlockSpec((1, w, L), lambda bi, i: (bi, 0, i))
    bwd_t = lambda w: pl.BlockSpec((1, w, L), lambda bi, i: (bi, 0, nc - 1 - i))
    st_c = pl.BlockSpec((1, 8, M_DQK, MLSTM_EXT), lambda bi, i: (bi, 0, 0, 0))
    st_v = pl.BlockSpec((1, 8, LANES), lambda bi, i: (bi, 0, 0))
    qk_w, v_w = M_HEADS * M_DQK, M_HEADS * M_DV
    return pl.pallas_call(
        _mlstm_kernel,
        grid=(b, nc),
        in_specs=[fwd(qk_w), fwd_t(qk_w), fwd(v_w), fwd(LANES), fwd_t(LANES), fwd_t(LANES),
                  bwd(qk_w), bwd_t(qk_w), bwd(v_w), bwd(LANES), bwd_t(LANES), bwd_t(LANES),
                  st_c, st_v],
        out_specs=[fwd(v_w), bwd(v_w), st_c, st_v],
        out_shape=[jax.ShapeDtypeStruct((b, t, v_w), BF16),
                   jax.ShapeDtypeStruct((b, t, v_w), BF16),
                   jax.ShapeDtypeStruct(c0.shape, F32),
                   jax.ShapeDtypeStruct(m0.shape, F32)],
        scratch_shapes=([pltpu.VMEM((M_DQK, MLSTM_EXT), F32) for _ in range(2 * M_HEADS)]
                        + [pltpu.VMEM((8, LANES), F32)]),
        compiler_params=_params("parallel", "arbitrary"),
        name="mlstm",
    )(q, kt, v, gf, git, gft, q, kt, v, gf, git, gft, c0, m0)


def _na_kernel(q_ref, k_ref, v_ref, kc_ref, vc_ref, bias_ref, o_ref, *, rows):
    r0 = pl.program_id(2) * NA_ROWS
    ks = jnp.clip(r0 - NA_KH // 2, 0, rows - NA_KEY_ROWS)
    kstart = pl.multiple_of(ks * GRID_W, GRID_W)
    nkeys = NA_KEY_ROWS * GRID_W
    kblk = k_ref[0, pl.ds(kstart, nkeys), :]
    vblk = v_ref[0, pl.ds(kstart, nkeys), :]
    kc = kc_ref[0]
    vc = vc_ref[0]
    q = q_ref[0]
    lane = lax.broadcasted_iota(jnp.int32, q.shape, 1)
    outs = []
    for hh in range(2):
        in_head = (lane < NA_DH) if hh == 0 else (lane >= NA_DH)
        qm = jnp.where(in_head, q, jnp.zeros_like(q))
        sw = _dot_nt(qm, kblk) + bias_ref[hh, 0]
        sc = _dot_nt(qm, kc)
        m = jnp.maximum(jnp.max(sw, axis=1, keepdims=True), jnp.max(sc, axis=1, keepdims=True))
        ew = jnp.exp(sw - m)
        ec = jnp.exp(sc - m)
        l = jnp.sum(ew, axis=1, keepdims=True) + jnp.sum(ec, axis=1, keepdims=True)
        o = _dot(ew.astype(BF16), vblk) + _dot(ec.astype(BF16), vc)
        outs.append(o / l)
    o_ref[0] = jnp.where(lane < NA_DH, outs[0], outs[1]).astype(o_ref.dtype)


def _na(nq, nk, nv, cnk, cnv, bias):
    b, t, w = nq.shape
    rows = t // GRID_W
    tq = NA_ROWS * GRID_W
    nrb = rows // NA_ROWS
    nctx = cnk.shape[1]
    kind = lambda rb: jnp.where(rb == 0, 0, jnp.where(rb == nrb - 1, 2, 1))
    return pl.pallas_call(
        functools.partial(_na_kernel, rows=rows),
        grid=(b, w // LANES, nrb),
        in_specs=[pl.BlockSpec((1, tq, LANES), lambda bi, hp, rb: (bi, rb, hp)),
                  pl.BlockSpec((1, t, LANES), lambda bi, hp, rb: (bi, 0, hp)),
                  pl.BlockSpec((1, t, LANES), lambda bi, hp, rb: (bi, 0, hp)),
                  pl.BlockSpec((1, nctx, LANES), lambda bi, hp, rb: (bi, 0, hp)),
                  pl.BlockSpec((1, nctx, LANES), lambda bi, hp, rb: (bi, 0, hp)),
                  pl.BlockSpec((2, 1, tq, NA_KEY_ROWS * GRID_W),
                               lambda bi, hp, rb: (hp, kind(rb), 0, 0))],
        out_specs=pl.BlockSpec((1, tq, LANES), lambda bi, hp, rb: (bi, rb, hp)),
        out_shape=jax.ShapeDtypeStruct((b, t, w), BF16),
        compiler_params=_params("parallel", "parallel", "arbitrary"),
        name="na",
    )(nq, nk, nv, cnk, cnv, bias)


def _na_bias_table(na_rpb, rows):
    h = na_rpb.shape[0]
    w = GRID_W
    c = np.arange(w)[:, None]
    kj = np.arange(w)[None, :]
    cs = np.clip(c - NA_KW // 2, 0, w - NA_KW)
    col_valid = (kj >= cs) & (kj < cs + NA_KW)
    dc = np.clip(kj - c + (NA_KW - 1), 0, 2 * NA_KW - 2)
    onehot = np.zeros((2 * NA_KW - 1, w, w), np.float32)
    onehot[dc, np.arange(w)[:, None], np.arange(w)[None, :]] = 1.0
    t2 = jnp.einsum("hrd,dck->hrck", na_rpb, jnp.asarray(onehot), precision=lax.Precision.HIGHEST)
    t2 = jnp.where(jnp.asarray(col_valid)[None, None], t2, NEG)
    t2 = jnp.concatenate([t2, jnp.full((h, 1, w, w), NEG, F32)], axis=1)
    invalid = 2 * NA_KH - 1
    dr_idx = np.full((3, NA_ROWS, NA_KEY_ROWS), invalid, np.int32)
    for kind, r0 in enumerate((0, NA_ROWS, rows - NA_ROWS)):
        ks = int(np.clip(r0 - NA_KH // 2, 0, rows - NA_KEY_ROWS))
        for qa in range(NA_ROWS):
            r = r0 + qa
            rs = int(np.clip(r - NA_KH // 2, 0, rows - NA_KH))
            for kl in range(NA_KEY_ROWS):
                ki = ks + kl
                if rs <= ki < rs + NA_KH:
                    dr_idx[kind, qa, kl] = ki - r + NA_KH - 1
    t2t = t2.transpose(0, 2, 1, 3)
    strips = [jnp.concatenate([t2t[:, :, int(dr), :] for dr in dr_idx[kind, qa]], axis=-1)
              for kind in range(3) for qa in range(NA_ROWS)]
    return jnp.stack(strips, axis=1).reshape(h, 3, NA_ROWS * w, NA_KEY_ROWS * w)


def _post_kernel(x_ref, mod_ref, hf_ref, hb_ref, mo_ref, na_ref, gm_ref, gn_ref,
                 mnw_ref, wbm_ref, wbn_ref, wout_ref, wr_ref, wsg_ref, wsu_ref, wsd_ref, rb_ref,
                 h2_ref, base_ref, e_ref, w_ref, r_ref, cnt_ref, run_scr):
    @pl.when((pl.program_id(0) == 0) & (pl.program_id(1) == 0))
    def _():
        run_scr[...] = jnp.zeros_like(run_scr)

    hm = hf_ref[0].astype(F32) + hb_ref[0].astype(F32)
    parts = []
    for hd in range(M_HEADS):
        t = hm[:, hd * M_DV:(hd + 1) * M_DV]
        parts.append(t * lax.rsqrt(jnp.mean(t * t, axis=-1, keepdims=True) + EPS))
    y_m = jnp.concatenate(parts, axis=1) * mnw_ref[...] * mo_ref[0].astype(F32)
    a = _dot(y_m.astype(BF16), wbm_ref[...])
    bn = _dot(na_ref[0], wbn_ref[...])
    z = gm_ref[0].astype(F32) * a + gn_ref[0].astype(F32) * bn
    y = _dot(z.astype(BF16), wout_ref[...])
    x1 = x_ref[0] + mod_ref[0, 2:3, :] * y
    xn = x1 * lax.rsqrt(jnp.mean(x1 * x1, axis=-1, keepdims=True) + EPS)
    h2f = xn * (1.0 + mod_ref[0, 4:5, :]) + mod_ref[0, 3:4, :]
    h2_ref[...] = _pack_bf16_pairs(h2f)
    h2 = h2f.astype(BF16)
    sh = _dot(h2, wsg_ref[...])
    sh = sh * _sigmoid(sh) * _dot(h2, wsu_ref[...])
    base_ref[0] = x1 + mod_ref[0, 5:6, :] * _dot(sh.astype(BF16), wsd_ref[...])
    scores = _sigmoid(_dot_nt(wr_ref[...], h2))
    _route_block(scores, rb_ref, e_ref, w_ref, r_ref, cnt_ref, run_scr)


def _post(x, mod, hf, hb, mo, yna, gm, gn, mnw, wbm, wbn, wout, wr_t, wsg, wsu, wsd, rbias, tm):
    b, t, d = x.shape
    nt = t // tm
    n = b * t
    tok = lambda w: pl.BlockSpec((1, tm, w), lambda bi, i: (bi, i, 0))
    rt = lambda: pl.BlockSpec((TOP_K, tm), lambda bi, i: (0, bi * nt + i))
    res = [mnw, wbm, wbn, wout, wr_t, wsg, wsu, wsd, rbias]
    return pl.pallas_call(
        _post_kernel,
        grid=(b, nt),
        in_specs=[tok(d), pl.BlockSpec((1, 8, d), lambda bi, i: (bi, 0, 0)),
                  tok(hf.shape[2]), tok(hb.shape[2]), tok(mo.shape[2]), tok(yna.shape[2]),
                  tok(gm.shape[2]), tok(gn.shape[2])] + [_resident(a.shape) for a in res],
        out_specs=[pl.BlockSpec((tm, d // 2), lambda bi, i: (bi * nt + i, 0)),
                   tok(d), rt(), rt(), rt(),
                   pl.BlockSpec((N_EXPERTS, LANES), lambda bi, i: (0, 0))],
        out_shape=[jax.ShapeDtypeStruct((n, d // 2), jnp.int32),
                   jax.ShapeDtypeStruct((b, t, d), F32),
                   jax.ShapeDtypeStruct((TOP_K, n), jnp.int32),
                   jax.ShapeDtypeStruct((TOP_K, n), F32),
                   jax.ShapeDtypeStruct((TOP_K, n), jnp.int32),
                   jax.ShapeDtypeStruct((N_EXPERTS, LANES), F32)],
        scratch_shapes=[pltpu.VMEM((N_EXPERTS, LANES), F32)],
        compiler_params=_params("arbitrary", "arbitrary"),
        name="post",
    )(x, mod, hf, hb, mo, yna, gm, gn, *res)


def _route_block(s, b_ref, e_ref, w_ref, r_ref, cnt_ref, run_scr):
    tm = s.shape[1]
    sel = s + b_ref[...][:, 0:1]
    gsz = N_EXPERTS // N_GROUPS
    ninf = -jnp.inf

    x3 = sel.reshape(N_GROUPS, gsz, tm)
    r3 = lax.broadcasted_iota(jnp.int32, x3.shape, 1)
    m1 = jnp.max(x3, axis=1, keepdims=True)
    i1 = jnp.min(jnp.where(x3 == m1, r3, gsz), axis=1, keepdims=True)
    m2 = jnp.max(jnp.where(r3 == i1, ninf, x3), axis=1)
    gs = m1[:, 0, :] + m2

    gidx = lax.broadcasted_iota(jnp.int32, gs.shape, 0)
    gkeep = jnp.zeros(gs.shape, jnp.bool_)
    cur = gs
    for _ in range(TOPK_GROUPS):
        mm = jnp.max(cur, axis=0, keepdims=True)
        ii = jnp.min(jnp.where(cur == mm, gidx, N_GROUPS), axis=0, keepdims=True)
        hit = gidx == ii
        gkeep = jnp.logical_or(gkeep, hit)
        cur = jnp.where(hit, ninf, cur)
    keep = jnp.broadcast_to(gkeep[:, None, :], x3.shape).reshape(N_EXPERTS, tm)

    row = lax.broadcasted_iota(jnp.int32, s.shape, 0)
    cur = jnp.where(keep, sel, ninf)
    idxs, ws = [], []
    chosen = jnp.zeros(s.shape, jnp.bool_)
    for _ in range(TOP_K):
        mm = jnp.max(cur, axis=0, keepdims=True)
        ii = jnp.min(jnp.where(cur == mm, row, N_EXPERTS), axis=0, keepdims=True)
        hit = row == ii
        idxs.append(ii)
        ws.append(jnp.sum(jnp.where(hit, s, 0.0), axis=0, keepdims=True))
        chosen = jnp.logical_or(chosen, hit)
        cur = jnp.where(hit, ninf, cur)
    wsum = ws[0]
    for wk in ws[1:]:
        wsum = wsum + wk

    chosen_f = jnp.where(chosen, 1.0, 0.0)
    tp = lax.broadcasted_iota(jnp.int32, (tm, tm), 0)
    tc = lax.broadcasted_iota(jnp.int32, (tm, tm), 1)
    before = jnp.where(tp < tc, 1.0, 0.0).astype(BF16)
    rank = _dot(chosen_f.astype(BF16), before) + run_scr[...][:, 0:1]
    run_scr[...] = run_scr[...] + jnp.sum(chosen_f, axis=1, keepdims=True)
    cnt_ref[...] = run_scr[...]

    for kk in range(TOP_K):
        e_ref[kk:kk + 1, :] = idxs[kk]
        w_ref[kk:kk + 1, :] = ws[kk] / wsum * ROUTE_SCALE
        r_ref[kk:kk + 1, :] = jnp.sum(jnp.where(row == idxs[kk], rank, 0.0), axis=0,
                                      keepdims=True).astype(jnp.int32)


def _slot_kernel(e_ref, r_ref, ps_ref, o_ref):
    row = lax.broadcasted_iota(jnp.int32, (N_EXPERTS, e_ref.shape[1]), 0)
    ps = ps_ref[...][:, 0:1]
    for kk in range(TOP_K):
        first = jnp.sum(jnp.where(row == e_ref[kk:kk + 1, :], ps, 0.0), axis=0, keepdims=True)
        o_ref[kk:kk + 1, :] = first.astype(jnp.int32) + r_ref[kk:kk + 1, :]


def _slots(top_e, rank, pstart, tm):
    k, n = top_e.shape
    blk = pl.BlockSpec((k, tm), lambda i: (0, i))
    return pl.pallas_call(
        _slot_kernel,
        grid=(n // tm,),
        in_specs=[blk, blk, _resident(pstart.shape)],
        out_specs=blk,
        out_shape=jax.ShapeDtypeStruct((k, n), jnp.int32),
        compiler_params=_params("parallel"),
        name="slots",
    )(top_e, rank, pstart)


SC_WINDOW = 128


def _sc_mesh():
    return plsc.VectorSubcoreMesh(core_axis_name="core", subcore_axis_name="subcore")


def _sc_workers():
    info = plsc.get_sparse_core_info()
    return info.num_cores, info.num_cores * info.num_subcores


def _dispatch_rows(x, pos, p_rows):
    n, w = x.shape
    kk = pos.shape[0]
    ncores, nw = _sc_workers()
    steps = n // nw // SC_WINDOW
    pos4 = pos.reshape(kk, nw, steps, SC_WINDOW).transpose(1, 2, 0, 3)

    @functools.partial(
        pl.kernel, mesh=_sc_mesh(),
        out_type=jax.ShapeDtypeStruct((p_rows, w), x.dtype),
        scratch_types=[pltpu.VMEM((kk, SC_WINDOW), jnp.int32),
                       pltpu.VMEM((SC_WINDOW, w), x.dtype),
                       pltpu.SemaphoreType.DMA],
    )
    def scatter(x_hbm, i_hbm, o_hbm, idx_v, rows_v, sem):
        wid = lax.axis_index("subcore") * ncores + lax.axis_index("core")

        @pl.loop(0, steps)
        def _(s):
            base = pl.multiple_of((wid * steps + s) * SC_WINDOW, SC_WINDOW)
            pltpu.sync_copy(i_hbm.at[wid, s], idx_v)
            pltpu.sync_copy(x_hbm.at[pl.ds(base, SC_WINDOW)], rows_v)
            copies = [pltpu.make_async_copy(rows_v, o_hbm.at[idx_v.at[j]], sem) for j in range(kk)]
            for cp in copies:
                cp.start()
            for cp in copies:
                cp.wait()

    return scatter(x, pos4)


def _gather_rows(x, idx):
    m = idx.shape[0]
    w = x.shape[1]
    ncores, nw = _sc_workers()
    steps = m // nw // SC_WINDOW
    idx3 = idx.reshape(nw, steps, SC_WINDOW)

    @functools.partial(
        pl.kernel, mesh=_sc_mesh(),
        out_type=jax.ShapeDtypeStruct((m, w), x.dtype),
        scratch_types=[pltpu.VMEM((steps, SC_WINDOW), jnp.int32),
                       pltpu.VMEM((SC_WINDOW, w), x.dtype),
                       pltpu.SemaphoreType.DMA],
    )
    def gather(x_hbm, i_hbm, o_hbm, idx_v, rows_v, sem):
        wid = lax.axis_index("subcore") * ncores + lax.axis_index("core")
        pltpu.sync_copy(i_hbm.at[wid], idx_v)

        @pl.loop(0, steps)
        def _(s):
            pltpu.async_copy(x_hbm.at[idx_v.at[s]], rows_v, sem).wait()
            base = pl.multiple_of((wid * steps + s) * SC_WINDOW, SC_WINDOW)
            pltpu.sync_copy(rows_v, o_hbm.at[pl.ds(base, SC_WINDOW)])

    return gather(x, idx3)


def _experts_kernel(blk0_ref, nblk_ref, cnt_ref, nu_ref,
                    x_hbm, wg_ref, wu_ref, wd_ref, y_hbm,
                    xbuf, ybuf, wg_scr, wu_scr, wd_scr, in_sem, out_sem):
    e = pl.program_id(0)
    n_used = nu_ref[0]
    blk0 = blk0_ref[e]
    ns = EXPERT_SLOTS

    def x_copy(g):
        return pltpu.make_async_copy(x_hbm.at[pl.ds(pl.multiple_of(g * MOE_BLOCK, MOE_BLOCK), MOE_BLOCK)],
                                     xbuf.at[g % ns], in_sem.at[g % ns])

    def y_copy(g):
        return pltpu.make_async_copy(ybuf.at[g % ns],
                                     y_hbm.at[pl.ds(pl.multiple_of(g * MOE_BLOCK, MOE_BLOCK), MOE_BLOCK)],
                                     out_sem.at[g % ns])

    @pl.when(e == 0)
    def _():
        for g0 in range(ns - 1):
            @pl.when(g0 < n_used)
            def _():
                x_copy(g0).start()

    wg_scr[...] = wg_ref[0].astype(BF16)
    wu_scr[...] = wu_ref[0].astype(BF16)
    wd_scr[...] = wd_ref[0].astype(BF16)

    def block(b, carry):
        g = blk0 + b
        x_copy(g).wait()

        @pl.when(g + ns - 1 < n_used)
        def _():
            x_copy(g + ns - 1).start()

        rid = lax.broadcasted_iota(jnp.int32, (MOE_BLOCK, xbuf.shape[2]), 0)
        xp = jnp.where(rid < cnt_ref[e] - b * MOE_BLOCK, xbuf[g % ns], 0)
        x = _unpack_bf16_pairs(xp).astype(BF16)
        gt = _dot(x, wg_scr[...])
        up = _dot(x, wu_scr[...])
        a = (gt * _sigmoid(gt) * up).astype(BF16)
        y = _pack_bf16_pairs(_dot(a, wd_scr[...]))

        @pl.when(g >= ns)
        def _():
            y_copy(g - ns).wait()

        ybuf[g % ns] = y
        y_copy(g).start()
        return carry

    lax.fori_loop(0, nblk_ref[e], block, 0)

    @pl.when(e == pl.num_programs(0) - 1)
    def _():
        for back in range(ns, 0, -1):
            @pl.when(n_used >= back)
            def _():
                y_copy(n_used - back).wait()


def _experts(blk0, nblk, counts, n_used, xs, wg, wu, wd):
    p, dp = xs.shape
    ne, d, ff = wg.shape
    grid_spec = pltpu.PrefetchScalarGridSpec(
        num_scalar_prefetch=4,
        grid=(ne,),
        in_specs=[pl.BlockSpec(memory_space=pl.ANY),
                  pl.BlockSpec((1, d, ff), lambda e, *_: (e, 0, 0)),
                  pl.BlockSpec((1, d, ff), lambda e, *_: (e, 0, 0)),
                  pl.BlockSpec((1, ff, d), lambda e, *_: (e, 0, 0))],
        out_specs=pl.BlockSpec(memory_space=pl.ANY),
        scratch_shapes=[pltpu.VMEM((EXPERT_SLOTS, MOE_BLOCK, dp), jnp.int32),
                        pltpu.VMEM((EXPERT_SLOTS, MOE_BLOCK, dp), jnp.int32),
                        pltpu.VMEM((d, ff), BF16), pltpu.VMEM((d, ff), BF16), pltpu.VMEM((ff, d), BF16),
                        pltpu.SemaphoreType.DMA((EXPERT_SLOTS,)),
                        pltpu.SemaphoreType.DMA((EXPERT_SLOTS,))],
    )
    return pl.pallas_call(
        _experts_kernel,
        grid_spec=grid_spec,
        out_shape=jax.ShapeDtypeStruct((p, dp), jnp.int32),
        compiler_params=_params("arbitrary"),
        name="experts",
    )(blk0, nblk, counts, n_used, xs, wg, wu, wd)


def _combine_kernel(base_ref, mod_ref, w_ref, y_ref, o_ref):
    acc = None
    for kk in range(TOP_K):
        term = w_ref[:, kk:kk + 1] * _unpack_bf16_pairs(y_ref[kk])
        acc = term if acc is None else acc + term
    o_ref[0] = base_ref[0] + mod_ref[0, 5:6, :] * acc


def _combine(acc, bi, mod, w_tk, yg, tm):
    b, t, d = acc.shape
    nt = t // tm
    tok = pl.BlockSpec((1, tm, d), lambda i: (bi, i, 0))
    return pl.pallas_call(
        _combine_kernel,
        grid=(nt,),
        in_specs=[tok, pl.BlockSpec((1, 8, d), lambda i: (bi, 0, 0)),
                  pl.BlockSpec((tm, TOP_K), lambda i: (bi * nt + i, 0)),
                  pl.BlockSpec((TOP_K, tm, d // 2), lambda i: (0, i, 0))],
        out_specs=tok,
        out_shape=jax.ShapeDtypeStruct((b, t, d), F32),
        input_output_aliases={0: 0},
        compiler_params=_params("parallel"),
        name="combine",
    )(acc, mod, w_tk, yg)


def _rope_tables(t):
    pos = np.arange(t)
    half = M_DQK // 2
    nf = half // 2
    inv = np.power(ROPE_BASE, -np.arange(nf, dtype=np.float32) / nf).astype(np.float32)
    row_pos = jnp.asarray((pos // GRID_W).astype(np.float32))
    col_pos = jnp.asarray((pos % GRID_W).astype(np.float32))
    inv = jnp.asarray(inv)
    ar = row_pos[:, None] * inv[None, :]
    ac = col_pos[:, None] * inv[None, :]
    cos = jnp.concatenate([jnp.cos(ar), jnp.cos(ar), jnp.cos(ac), jnp.cos(ac)], axis=1)
    sin = jnp.concatenate([-jnp.sin(ar), jnp.sin(ar), -jnp.sin(ac), jnp.sin(ac)], axis=1)
    return cos, sin, cos.T, sin.T


def _arrange_w_in(w_in, b_mgate):
    d = w_in.shape[0]
    sizes = (512, 512, 1024, 1024, 16, 512, 512, 512, 1024, 1024)
    offs = np.concatenate([[0], np.cumsum(sizes)])
    mq, mk, mv, mo, mg, nq, nk, nv, gm, gn = [w_in[:, offs[i]:offs[i + 1]] for i in range(10)]
    pad = jnp.zeros((d, LANES - 2 * M_HEADS), w_in.dtype)
    gi = jnp.concatenate([mg[:, 0:4], mg[:, 8:12], pad], axis=1)
    gf = jnp.concatenate([mg[:, 4:8], mg[:, 12:16], pad], axis=1)
    w_all = jnp.concatenate([mq, mv, mo, gf, nq, nk, nv, gm, gn], axis=1).astype(BF16)
    wt_all = jnp.concatenate([mk, gi, gf], axis=1).T.astype(BF16)
    bpad = jnp.zeros((LANES - 2 * M_HEADS,), F32)
    bi = jnp.concatenate([b_mgate[0:4], b_mgate[8:12], bpad])
    bf = jnp.concatenate([b_mgate[4:8], b_mgate[12:16], bpad])
    bg = jnp.concatenate([bf[None, :], jnp.zeros((7, LANES), F32)], axis=0)
    bgt = jnp.concatenate([bi, bf])[:, None]
    return w_all, wt_all, bg, bgt


def _segment_mats():
    na_w = NA_HEADS * NA_DH
    seg = np.zeros((na_w, LANES), np.float32)
    seg[np.arange(na_w), np.arange(na_w) // NA_DH] = 1.0
    return jnp.asarray(seg, BF16), jnp.asarray(seg.T.copy(), BF16)


def kernel(x, c, ctx, c_ctx, w_ada, b_ada, w_in, b_mgate, m_norm_w, na_qn_w, na_kn_w, na_rpb,
           w_br_m, w_br_na, w_out, w_router, router_bias, w_exp_gate, w_exp_up, w_exp_down,
           w_sh_gate, w_sh_up, w_sh_down):
    b, t, d = x.shape
    n = b * t
    rows = t // GRID_W
    l = 0

    cc = jnp.concatenate([c, c_ctx[None, :], jnp.zeros((8 - b - 1, d), F32)], axis=0)
    mod = _ada(cc, w_ada[l], b_ada[l])
    mod = mod.reshape(8, 6, d)
    mod = jnp.concatenate([mod, jnp.zeros((8, 2, d), F32)], axis=1)
    mod_x = mod[:b]
    mod_c = jnp.broadcast_to(mod[b:b + 1], (b, 8, d))

    w_all, wt_all, bg, bgt = _arrange_w_in(w_in[l], b_mgate[l])
    seg, segt = _segment_mats()
    qnw = jnp.tile(na_qn_w[l], NA_HEADS)[None, :]
    knw = jnp.tile(na_kn_w[l], NA_HEADS)[None, :]
    tm = min(512, t)

    cp = _inproj(ctx, mod_c, w_all, wt_all, bg, bgt, qnw, knw, seg, segt, None,
                 min(tm, ctx.shape[1]))
    xp = _inproj(x, mod_x, w_all, wt_all, bg, bgt, qnw, knw, seg, segt, _rope_tables(t), tm)
    cmq, cmv, _, cgf, _, cnk, cnv, _, _, cmkt, cgit, cgft = cp
    mq, mv, mo, gf, nq, nk, nv, gm, gn, mkt, git, gft = xp

    c0 = jnp.zeros((b, 8, M_DQK, MLSTM_EXT), F32)
    m0 = jnp.zeros((b, 8, LANES), F32)
    _, _, c1, m1 = _mlstm(cmq, cmkt, cmv, cgf, cgit, cgft, c0, m0)
    hf, hb, _, _ = _mlstm(mq, mkt, mv, gf, git, gft, c1, m1)

    yna = _na(nq, nk, nv, cnk, cnv, _na_bias_table(na_rpb[l], rows))

    bias_col = jnp.broadcast_to(router_bias[l][:, None], (N_EXPERTS, LANES))
    h2p, base, top_e, top_w, rank, cnt = _post(
        x, mod_x, hf, hb, mo, yna, gm, gn, m_norm_w[l][None, :],
        w_br_m[l].astype(BF16), w_br_na[l].astype(BF16), w_out[l].astype(BF16),
        w_router[l].T.astype(BF16), w_sh_gate[l].astype(BF16), w_sh_up[l].astype(BF16),
        w_sh_down[l].astype(BF16), bias_col, tm)

    counts = cnt[:, 0].astype(jnp.int32)
    padded = (counts + MOE_BLOCK - 1) // MOE_BLOCK * MOE_BLOCK
    pend = jnp.cumsum(padded)
    pstart = pend - padded
    nb = -(-(n * TOP_K) // MOE_BLOCK) + N_EXPERTS
    p_rows = nb * MOE_BLOCK
    n_used = pend[-1:] // MOE_BLOCK

    pstart_col = jnp.broadcast_to(pstart.astype(F32)[:, None], (N_EXPERTS, LANES))
    pos = _slots(top_e, rank, pstart_col, min(512, n))
    xs = _dispatch_rows(h2p, pos, p_rows)
    ys = _experts(pstart // MOE_BLOCK, padded // MOE_BLOCK, counts, n_used, xs,
                  w_exp_gate[l], w_exp_up[l], w_exp_down[l])
    w_tk = top_w.T
    out = base
    for bi in range(b):
        idx = pos[:, bi * t:(bi + 1) * t].reshape(-1)
        yg = _gather_rows(ys, idx).reshape(TOP_K, t, d // 2)
        out = _combine(out, bi, mod_x, w_tk, yg, tm)
    return out
```

```python
import functools

import numpy as np
import jax
import jax.numpy as jnp
from jax import lax
from jax.experimental import pallas as pl
from jax.experimental.pallas import tpu as pltpu
from jax.experimental.pallas import tpu_sc as plsc

F32 = jnp.float32
BF16 = jnp.bfloat16

EPS = 1e-6
GRID_W = 64
M_HEADS, M_DQK, M_DV = 4, 128, 256
ROPE_BASE = 10000.0
NA_HEADS, NA_DH, NA_KH, NA_KW = 8, 64, 8, 16
N_EXPERTS, TOP_K, N_GROUPS, TOPK_GROUPS = 256, 8, 8, 4
ROUTE_SCALE = 2.5

LANES = 128
VMEM_LIMIT = 56 * 1024 * 1024
NEG = -1e30

MLSTM_CHUNK = 256
NA_ROWS = 4
NA_KEY_ROWS = NA_ROWS + NA_KH - 1
MOE_BLOCK = 512
EXPERT_SLOTS = 4

_W_SEGS = (("mq", 512), ("mv", 1024), ("mo", 1024), ("gf", 128),
           ("nq", 512), ("nk", 512), ("nv", 512), ("gm", 1024), ("gn", 1024))
_W_OFF = {}
_o = 0
for _n, _w in _W_SEGS:
    _W_OFF[_n] = (_o, _w)
    _o += _w
W_COLS = _o


def _dot(a, b):
    return jnp.dot(a, b, preferred_element_type=F32)


def _dot_nt(a, b):
    return lax.dot_general(a, b, (((1,), (1,)), ((), ())), preferred_element_type=F32)


def _sigmoid(x):
    return 1.0 / (1.0 + jnp.exp(-x))


def _pack_bf16_pairs(v):
    w = v.shape[1] // 2
    bits = pltpu.bitcast(v.astype(BF16).astype(F32), jnp.int32)
    return lax.shift_right_logical(bits[:, :w], 16) | bits[:, w:]


def _unpack_bf16_pairs(p):
    lo = pltpu.bitcast(lax.shift_left(p, 16), F32)
    hi = pltpu.bitcast(p & jnp.int32(-65536), F32)
    return jnp.concatenate([lo, hi], axis=1)


def _params(*sem):
    return pltpu.CompilerParams(dimension_semantics=sem, vmem_limit_bytes=VMEM_LIMIT)


def _resident(shape):
    nd = len(shape)
    return pl.BlockSpec(shape, lambda *_: (0,) * nd, pipeline_mode=pl.Buffered(1))


def _ada_kernel(c_ref, w_ref, b_ref, o_ref):
    c = c_ref[...]
    s = c * _sigmoid(c)
    o_ref[...] = _dot(s.astype(BF16), w_ref[...].astype(BF16)) + b_ref[...]


def _ada(cc, w_ada, b_ada):
    d = cc.shape[1]
    n = w_ada.shape[1]
    return pl.pallas_call(
        _ada_kernel,
        grid=(n // d,),
        in_specs=[pl.BlockSpec((8, d), lambda j: (0, 0)),
                  pl.BlockSpec((d, d), lambda j: (0, j)),
                  pl.BlockSpec((1, d), lambda j: (0, j))],
        out_specs=pl.BlockSpec((8, d), lambda j: (0, j)),
        out_shape=jax.ShapeDtypeStruct((8, n), F32),
        compiler_params=_params("arbitrary"),
        name="ada",
    )(cc, w_ada, b_ada.reshape(1, n))


def _rope_rotate(t, cos, sin):
    lane = lax.broadcasted_iota(jnp.int32, t.shape, 1)
    partner = jnp.where((lane & 32) == 0, pltpu.roll(t, 96, 1), pltpu.roll(t, 32, 1))
    return t * cos + partner * sin


def _rope_rotate_t(t, cos, sin):
    q = M_DQK // 4
    partner = jnp.concatenate([t[q:2 * q], t[0:q], t[3 * q:4 * q], t[2 * q:3 * q]], axis=0)
    return t * cos + partner * sin


def _inproj_kernel(*refs, rope):
    if rope:
        (x_ref, mod_ref, w_ref, wt_ref, bg_ref, bgt_ref, qnw_ref, knw_ref, seg_ref, segt_ref,
         rcos_ref, rsin_ref, ccos_ref, csin_ref,
         mq_ref, mv_ref, mo_ref, gf_ref, nq_ref, nk_ref, nv_ref, gm_ref, gn_ref,
         mkt_ref, git_ref, gft_ref) = refs
    else:
        (x_ref, mod_ref, w_ref, wt_ref, bg_ref, bgt_ref, qnw_ref, knw_ref, seg_ref, segt_ref,
         mq_ref, mv_ref, mo_ref, gf_ref, nq_ref, nk_ref, nv_ref, gm_ref, gn_ref,
         mkt_ref, git_ref, gft_ref) = refs
    x = x_ref[0]
    xn = x * lax.rsqrt(jnp.mean(x * x, axis=-1, keepdims=True) + EPS)
    h = xn * (1.0 + mod_ref[0, 1:2, :]) + mod_ref[0, 0:1, :]
    hb = h.astype(BF16)

    def proj(name):
        off, width = _W_OFF[name]
        return _dot(hb, w_ref[:, off:off + width])

    def head_rms(t, w_row, scale):
        ss = _dot((t * t).astype(BF16), seg_ref[...])
        r = lax.rsqrt(ss * (1.0 / NA_DH) + EPS)
        r_hi = r.astype(BF16)
        r_lo = (r - r_hi.astype(F32)).astype(BF16)
        rb = _dot(r_hi, segt_ref[...]) + _dot(r_lo, segt_ref[...])
        return t * rb * w_row * scale

    mq = proj("mq") * (M_DQK ** -0.5)
    if rope:
        tm = x.shape[0]
        spread = lambda r: jnp.broadcast_to(r[:, None, :], (tm // GRID_W, GRID_W, LANES)).reshape(tm, LANES)
        cos = spread(rcos_ref[...]) + ccos_ref[...]
        sin = spread(rsin_ref[...]) + csin_ref[...]
        mq = jnp.concatenate([_rope_rotate(mq[:, i * LANES:(i + 1) * LANES], cos, sin)
                              for i in range(M_HEADS)], axis=1)
    mq_ref[0] = mq.astype(BF16)
    mv_ref[0] = proj("mv").astype(BF16)
    mo_ref[0] = _sigmoid(proj("mo")).astype(BF16)
    gf_ref[0] = proj("gf") + bg_ref[0:1, :]
    nq_ref[0] = head_rms(proj("nq"), qnw_ref[...], NA_DH ** -0.5).astype(BF16)
    nk_ref[0] = head_rms(proj("nk"), knw_ref[...], 1.0).astype(BF16)
    nv_ref[0] = proj("nv").astype(BF16)
    gm_ref[0] = _sigmoid(proj("gm")).astype(BF16)
    gn_ref[0] = _sigmoid(proj("gn")).astype(BF16)

    qk_w = M_HEADS * M_DQK
    mkt = _dot_nt(wt_ref[0:qk_w, :], hb)
    if rope:
        cost, sint = jnp.transpose(cos), jnp.transpose(sin)
        mkt = jnp.concatenate([_rope_rotate_t(mkt[i * M_DQK:(i + 1) * M_DQK], cost, sint)
                               for i in range(M_HEADS)], axis=0)
    mkt_ref[0] = mkt.astype(BF16)
    git_ref[0] = _dot_nt(wt_ref[qk_w:qk_w + LANES, :], hb) + bgt_ref[0:LANES, :]
    gft_ref[0] = _dot_nt(wt_ref[qk_w + LANES:qk_w + 2 * LANES, :], hb) + bgt_ref[LANES:2 * LANES, :]


def _inproj(x, mod, w_all, wt_all, bg, bgt, qnw, knw, seg, segt, rope_tabs, tm):
    b, t, d = x.shape
    rope = rope_tabs is not None
    tok = lambda w: pl.BlockSpec((1, tm, w), lambda bi, i: (bi, i, 0))
    tok_t = lambda w: pl.BlockSpec((1, w, tm), lambda bi, i: (bi, 0, i))
    in_specs = [tok(d),
                pl.BlockSpec((1, 8, d), lambda bi, i: (bi, 0, 0)),
                _resident(w_all.shape), _resident(wt_all.shape), _resident(bg.shape),
                _resident(bgt.shape), _resident(qnw.shape),
                _resident(knw.shape), _resident(seg.shape), _resident(segt.shape)]
    args = [x, mod, w_all, wt_all, bg, bgt, qnw, knw, seg, segt]
    if rope:
        in_specs += [pl.BlockSpec((tm // GRID_W, LANES), lambda bi, i: (i, 0))] * 2
        in_specs += [_resident((tm, LANES))] * 2
        args += list(rope_tabs)
    widths = [("mq", BF16), ("mv", BF16), ("mo", BF16), ("gf", F32),
              ("nq", BF16), ("nk", BF16), ("nv", BF16), ("gm", BF16), ("gn", BF16)]
    out_specs = [tok(_W_OFF[n][1]) for n, _ in widths]
    out_shape = [jax.ShapeDtypeStruct((b, t, _W_OFF[n][1]), dt) for n, dt in widths]
    out_specs += [tok_t(M_HEADS * M_DQK), tok_t(LANES), tok_t(LANES)]
    out_shape += [jax.ShapeDtypeStruct((b, M_HEADS * M_DQK, t), BF16),
                  jax.ShapeDtypeStruct((b, LANES, t), F32),
                  jax.ShapeDtypeStruct((b, LANES, t), F32)]
    return pl.pallas_call(
        functools.partial(_inproj_kernel, rope=rope),
        grid=(b, t // tm),
        in_specs=in_specs, out_specs=out_specs, out_shape=out_shape,
        compiler_params=_params("parallel", "parallel"),
        name="inproj_rope" if rope else "inproj_ctx",
    )(*args)


def _log_sigmoid(x):
    return jnp.minimum(x, 0.0) - jnp.log(1.0 + jnp.exp(-jnp.abs(x)))


def _dot_split(a, b, split_a):
    x = a if split_a else b
    hi = x.astype(BF16)
    lo = (x - hi.astype(F32)).astype(BF16)
    return (_dot(hi, b) + _dot(lo, b)) if split_a else (_dot(a, hi) + _dot(a, lo))


MLSTM_EXT = M_DV + LANES


def _mlstm_kernel(qf_ref, ktf_ref, vf_ref, gff_ref, gitf_ref, gftf_ref,
                  qb_ref, ktb_ref, vb_ref, gfb_ref, gitb_ref, gftb_ref,
                  c0_ref, m0_ref,
                  hf_ref, hb_ref, cn_ref, mn_ref,
                  *scratch):
    c_scrs, m_scr = scratch[:2 * M_HEADS], scratch[2 * M_HEADS]
    step = pl.program_id(1)
    L = qf_ref.shape[1]
    nu = 2 * M_HEADS

    @pl.when(step == 0)
    def _():
        for j, c_scr in enumerate(c_scrs):
            c_scr[...] = c0_ref[0, j]
        m_scr[...] = m0_ref[0]

    row_i = lax.broadcasted_iota(jnp.int32, (L, L), 0)
    col_i = lax.broadcasted_iota(jnp.int32, (L, L), 1)
    lower = col_i <= row_i
    upper = col_i >= row_i
    tri_lo = jnp.where(lower, 1.0, 0.0).astype(BF16)
    tri_up = jnp.where(upper, 1.0, 0.0).astype(BF16)

    is_f = lax.broadcasted_iota(jnp.int32, (nu, L), 0) < M_HEADS
    gi_t = jnp.where(is_f, gitf_ref[0, 0:nu, :], gitb_ref[0, 0:nu, :])
    ls_tf = _log_sigmoid(gftf_ref[0, 0:nu, :])
    ls_tb = _log_sigmoid(gftb_ref[0, 0:nu, :])
    b_t = jnp.where(is_f, _dot_split(ls_tf, tri_up, True), _dot_split(ls_tb, tri_lo, True))
    u_t = gi_t - b_t
    g_c = jnp.sum(jnp.where(is_f, ls_tf, ls_tb), axis=1, keepdims=True)
    m_prev = m_scr[...]
    a_t = g_c + u_t
    m_new = jnp.maximum(g_c + m_prev, jnp.max(a_t, axis=1, keepdims=True))
    decay = jnp.exp(g_c + m_prev - m_new)
    wa_t = jnp.exp(a_t - jnp.concatenate([m_new] * (L // LANES), axis=1))
    m_scr[...] = m_new

    ones = jnp.ones((L, LANES), BF16)
    dirs = ((qf_ref, ktf_ref, vf_ref, gff_ref, hf_ref, lower, tri_lo),
            (qb_ref, ktb_ref, vb_ref, gfb_ref, hb_ref, upper, tri_up))
    for d, (q_ref, kt_ref, v_ref, gf_ref, h_ref, mask, tri) in enumerate(dirs):
        bcum = _dot_split(tri, _log_sigmoid(gf_ref[0]), False)
        for hd in range(M_HEADS):
            j = d * M_HEADS + hd
            c_scr = c_scrs[j]
            q = q_ref[0, :, hd * M_DQK:(hd + 1) * M_DQK]
            k_t = kt_ref[0, hd * M_DQK:(hd + 1) * M_DQK, :]
            v_ext = jnp.concatenate([v_ref[0, :, hd * M_DV:(hd + 1) * M_DV], ones], axis=1)
            u_row = u_t[j:j + 1, :]
            mp_row = m_prev[j:j + 1, :]
            c_prev = c_scr[...]

            m_loc = jnp.max(jnp.where(mask, u_row, NEG), axis=1, keepdims=True)
            m_rep = jnp.maximum(jnp.broadcast_to(m_loc, (L, LANES)), mp_row)
            m_wide = jnp.concatenate([m_rep] * (L // LANES), axis=1)
            dmat = jnp.exp(jnp.where(mask, u_row - m_wide, NEG))
            s = (_dot(q, k_t) * dmat).astype(BF16)
            qw = (q.astype(F32) * jnp.exp(mp_row - m_rep)).astype(BF16)
            r = _dot(s, v_ext) + _dot(qw, c_prev.astype(BF16))
            b_rep = jnp.broadcast_to(bcum[:, j:j + 1], (L, LANES))
            dn = jnp.maximum(jnp.abs(r[:, M_DV:]), jnp.exp(-(b_rep + m_rep)))
            h_ref[0, :, hd * M_DV:(hd + 1) * M_DV] = (
                r[:, :M_DV] / jnp.concatenate([dn] * (M_DV // LANES), axis=1)).astype(h_ref.dtype)

            kw = (k_t.astype(F32) * wa_t[j:j + 1, :]).astype(BF16)
            dec = jnp.concatenate([decay[j:j + 1, :]] * (MLSTM_EXT // LANES), axis=1)
            c_scr[...] = dec * c_prev + _dot(kw, v_ext)

    @pl.when(step == pl.num_programs(1) - 1)
    def _():
        for j, c_scr in enumerate(c_scrs):
            cn_ref[0, j] = c_scr[...]
        mn_ref[0] = m_scr[...]


def _mlstm(q, kt, v, gf, git, gft, c0, m0):
    b, t, _ = q.shape
    L = min(MLSTM_CHUNK, t)
    nc = t // L
    fwd = lambda w: pl.BlockSpec((1, L, w), lambda bi, i: (bi, i, 0))
    bwd = lambda w: pl.BlockSpec((1, L, w), lambda bi, i: (bi, nc - 1 - i, 0))
    fwd_t = lambda w: pl.BlockSpec((1, w, L), lambda bi, i: (bi, 0, i))
    bwd_t = lambda w: pl.BlockSpec((1, w, L), lambda bi, i: (bi, 0, nc - 1 - i))
    st_c = pl.BlockSpec((1, 8, M_DQK, MLSTM_EXT), lambda bi, i: (bi, 0, 0, 0))
    st_v = pl.BlockSpec((1, 8, LANES), lambda bi, i: (bi, 0, 0))
    qk_w, v_w = M_HEADS * M_DQK, M_HEADS * M_DV
    return pl.pallas_call(
        _mlstm_kernel,
        grid=(b, nc),
        in_specs=[fwd(qk_w), fwd_t(qk_w), fwd(v_w), fwd(LANES), fwd_t(LANES), fwd_t(LANES),
                  bwd(qk_w), bwd_t(qk_w), bwd(v_w), bwd(LANES), bwd_t(LANES), bwd_t(LANES),
                  st_c, st_v],
        out_specs=[fwd(v_w), bwd(v_w), st_c, st_v],
        out_shape=[jax.ShapeDtypeStruct((b, t, v_w), BF16),
                   jax.ShapeDtypeStruct((b, t, v_w), BF16),
                   jax.ShapeDtypeStruct(c0.shape, F32),
                   jax.ShapeDtypeStruct(m0.shape, F32)],
        scratch_shapes=([pltpu.VMEM((M_DQK, MLSTM_EXT), F32) for _ in range(2 * M_HEADS)]
                        + [pltpu.VMEM((8, LANES), F32)]),
        compiler_params=_params("parallel", "arbitrary"),
        name="mlstm",
    )(q, kt, v, gf, git, gft, q, kt, v, gf, git, gft, c0, m0)


def _na_kernel(q_ref, k_ref, v_ref, kc_ref, vc_ref, bias_ref, o_ref, *, rows):
    r0 = pl.program_id(2) * NA_ROWS
    ks = jnp.clip(r0 - NA_KH // 2, 0, rows - NA_KEY_ROWS)
    kstart = pl.multiple_of(ks * GRID_W, GRID_W)
    nkeys = NA_KEY_ROWS * GRID_W
    kblk = k_ref[0, pl.ds(kstart, nkeys), :]
    vblk = v_ref[0, pl.ds(kstart, nkeys), :]
    kc = kc_ref[0]
    vc = vc_ref[0]
    q = q_ref[0]
    lane = lax.broadcasted_iota(jnp.int32, q.shape, 1)
    outs = []
    for hh in range(2):
        in_head = (lane < NA_DH) if hh == 0 else (lane >= NA_DH)
        qm = jnp.where(in_head, q, jnp.zeros_like(q))
        sw = _dot_nt(qm, kblk) + bias_ref[hh, 0]
        sc = _dot_nt(qm, kc)
        m = jnp.maximum(jnp.max(sw, axis=1, keepdims=True), jnp.max(sc, axis=1, keepdims=True))
        ew = jnp.exp(sw - m)
        ec = jnp.exp(sc - m)
        l = jnp.sum(ew, axis=1, keepdims=True) + jnp.sum(ec, axis=1, keepdims=True)
        o = _dot(ew.astype(BF16), vblk) + _dot(ec.astype(BF16), vc)
        outs.append(o / l)
    o_ref[0] = jnp.where(lane < NA_DH, outs[0], outs[1]).astype(o_ref.dtype)


def _na(nq, nk, nv, cnk, cnv, bias):
    b, t, w = nq.shape
    rows = t // GRID_W
    tq = NA_ROWS * GRID_W
    nrb = rows // NA_ROWS
    nctx = cnk.shape[1]
    kind = lambda rb: jnp.where(rb == 0, 0, jnp.where(rb == nrb - 1, 2, 1))
    return pl.pallas_call(
        functools.partial(_na_kernel, rows=rows),
        grid=(b, w // LANES, nrb),
        in_specs=[pl.BlockSpec((1, tq, LANES), lambda bi, hp, rb: (bi, rb, hp)),
                  pl.BlockSpec((1, t, LANES), lambda bi, hp, rb: (bi, 0, hp)),
                  pl.BlockSpec((1, t, LANES), lambda bi, hp, rb: (bi, 0, hp)),
                  pl.BlockSpec((1, nctx, LANES), lambda bi, hp, rb: (bi, 0, hp)),
                  pl.BlockSpec((1, nctx, LANES), lambda bi, hp, rb: (bi, 0, hp)),
                  pl.BlockSpec((2, 1, tq, NA_KEY_ROWS * GRID_W),
                               lambda bi, hp, rb: (hp, kind(rb), 0, 0))],
        out_specs=pl.BlockSpec((1, tq, LANES), lambda bi, hp, rb: (bi, rb, hp)),
        out_shape=jax.ShapeDtypeStruct((b, t, w), BF16),
        compiler_params=_params("parallel", "parallel", "arbitrary"),
        name="na",
    )(nq, nk, nv, cnk, cnv, bias)


def _na_bias_table(na_rpb, rows):
    h = na_rpb.shape[0]
    w = GRID_W
    c = np.arange(w)[:, None]
    kj = np.arange(w)[None, :]
    cs = np.clip(c - NA_KW // 2, 0, w - NA_KW)
    col_valid = (kj >= cs) & (kj < cs + NA_KW)
    dc = np.clip(kj - c + (NA_KW - 1), 0, 2 * NA_KW - 2)
    onehot = np.zeros((2 * NA_KW - 1, w, w), np.float32)
    onehot[dc, np.arange(w)[:, None], np.arange(w)[None, :]] = 1.0
    t2 = jnp.einsum("hrd,dck->hrck", na_rpb, jnp.asarray(onehot), precision=lax.Precision.HIGHEST)
    t2 = jnp.where(jnp.asarray(col_valid)[None, None], t2, NEG)
    t2 = jnp.concatenate([t2, jnp.full((h, 1, w, w), NEG, F32)], axis=1)
    invalid = 2 * NA_KH - 1
    dr_idx = np.full((3, NA_ROWS, NA_KEY_ROWS), invalid, np.int32)
    for kind, r0 in enumerate((0, NA_ROWS, rows - NA_ROWS)):
        ks = int(np.clip(r0 - NA_KH // 2, 0, rows - NA_KEY_ROWS))
        for qa in range(NA_ROWS):
            r = r0 + qa
            rs = int(np.clip(r - NA_KH // 2, 0, rows - NA_KH))
            for kl in range(NA_KEY_ROWS):
                ki = ks + kl
                if rs <= ki < rs + NA_KH:
                    dr_idx[kind, qa, kl] = ki - r + NA_KH - 1
    t2t = t2.transpose(0, 2, 1, 3)
    strips = [jnp.concatenate([t2t[:, :, int(dr), :] for dr in dr_idx[kind, qa]], axis=-1)
              for kind in range(3) for qa in range(NA_ROWS)]
    return jnp.stack(strips, axis=1).reshape(h, 3, NA_ROWS * w, NA_KEY_ROWS * w)


def _post_kernel(x_ref, mod_ref, hf_ref, hb_ref, mo_ref, na_ref, gm_ref, gn_ref,
                 mnw_ref, wbm_ref, wbn_ref, wout_ref, wr_ref, wsg_ref, wsu_ref, wsd_ref, rb_ref,
                 h2_ref, base_ref, e_ref, w_ref, r_ref, cnt_ref, run_scr):
    @pl.when(pl.program_id(0) == 0)
    def _():
        run_scr[...] = jnp.zeros_like(run_scr)

    hm = hf_ref[0].astype(F32) + hb_ref[0].astype(F32)
    parts = []
    for hd in range(M_HEADS):
        t = hm[:, hd * M_DV:(hd + 1) * M_DV]
        parts.append(t * lax.rsqrt(jnp.mean(t * t, axis=-1, keepdims=True) + EPS))
    y_m = jnp.concatenate(parts, axis=1) * mnw_ref[...] * mo_ref[0].astype(F32)
    a = _dot(y_m.astype(BF16), wbm_ref[...])
    bn = _dot(na_ref[0], wbn_ref[...])
    z = gm_ref[0].astype(F32) * a + gn_ref[0].astype(F32) * bn
    y = _dot(z.astype(BF16), wout_ref[...])
    x1 = x_ref[0] + mod_ref[0, 2:3, :] * y
    xn = x1 * lax.rsqrt(jnp.mean(x1 * x1, axis=-1, keepdims=True) + EPS)
    h2f = xn * (1.0 + mod_ref[0, 4:5, :]) + mod_ref[0, 3:4, :]
    h2_ref[...] = _pack_bf16_pairs(h2f)
    h2 = h2f.astype(BF16)
    sh = _dot(h2, wsg_ref[...])
    sh = sh * _sigmoid(sh) * _dot(h2, wsu_ref[...])
    base_ref[0] = x1 + mod_ref[0, 5:6, :] * _dot(sh.astype(BF16), wsd_ref[...])
    scores = _sigmoid(_dot_nt(wr_ref[...], h2))
    _route_block(scores, rb_ref, e_ref, w_ref, r_ref, cnt_ref, run_scr)


def _post(x, mod, hf, hb, mo, yna, gm, gn, mnw, wbm, wbn, wout, wr_t, wsg, wsu, wsd, rbias, tm):
    b, t, d = x.shape
    nt = t // tm
    n = b * t
    tok = lambda w: pl.BlockSpec((1, tm, w), lambda s: (s // nt, s % nt, 0))
    rt = lambda: pl.BlockSpec((TOP_K, tm), lambda s: (0, s))
    res = [mnw, wbm, wbn, wout, wr_t, wsg, wsu, wsd, rbias]
    return pl.pallas_call(
        _post_kernel,
        grid=(b * nt,),
        in_specs=[tok(d), pl.BlockSpec((1, 8, d), lambda s: (s // nt, 0, 0)),
                  tok(hf.shape[2]), tok(hb.shape[2]), tok(mo.shape[2]), tok(yna.shape[2]),
                  tok(gm.shape[2]), tok(gn.shape[2])] + [_resident(a.shape) for a in res],
        out_specs=[pl.BlockSpec((tm, d // 2), lambda s: (s, 0)),
                   tok(d), rt(), rt(), rt(),
                   pl.BlockSpec((N_EXPERTS, LANES), lambda s: (0, 0))],
        out_shape=[jax.ShapeDtypeStruct((n, d // 2), jnp.int32),
                   jax.ShapeDtypeStruct((b, t, d), F32),
                   jax.ShapeDtypeStruct((TOP_K, n), jnp.int32),
                   jax.ShapeDtypeStruct((TOP_K, n), F32),
                   jax.ShapeDtypeStruct((TOP_K, n), jnp.int32),
                   jax.ShapeDtypeStruct((N_EXPERTS, LANES), F32)],
        scratch_shapes=[pltpu.VMEM((N_EXPERTS, LANES), F32)],
        compiler_params=_params("arbitrary"),
        name="post",
    )(x, mod, hf, hb, mo, yna, gm, gn, *res)


def _route_block(s, b_ref, e_ref, w_ref, r_ref, cnt_ref, run_scr):
    tm = s.shape[1]
    sel = s + b_ref[...][:, 0:1]
    gsz = N_EXPERTS // N_GROUPS
    ninf = -jnp.inf

    x3 = sel.reshape(N_GROUPS, gsz, tm)
    r3 = lax.broadcasted_iota(jnp.int32, x3.shape, 1)
    m1 = jnp.max(x3, axis=1, keepdims=True)
    i1 = jnp.min(jnp.where(x3 == m1, r3, gsz), axis=1, keepdims=True)
    m2 = jnp.max(jnp.where(r3 == i1, ninf, x3), axis=1)
    gs = m1[:, 0, :] + m2

    gidx = lax.broadcasted_iota(jnp.int32, gs.shape, 0)
    gkeep = jnp.zeros(gs.shape, jnp.bool_)
    cur = gs
    for _ in range(TOPK_GROUPS):
        mm = jnp.max(cur, axis=0, keepdims=True)
        ii = jnp.min(jnp.where(cur == mm, gidx, N_GROUPS), axis=0, keepdims=True)
        hit = gidx == ii
        gkeep = jnp.logical_or(gkeep, hit)
        cur = jnp.where(hit, ninf, cur)
    keep = jnp.broadcast_to(gkeep[:, None, :], x3.shape).reshape(N_EXPERTS, tm)

    row = lax.broadcasted_iota(jnp.int32, s.shape, 0)
    cur = jnp.where(keep, sel, ninf)
    idxs, ws = [], []
    chosen = jnp.zeros(s.shape, jnp.bool_)
    for _ in range(TOP_K):
        mm = jnp.max(cur, axis=0, keepdims=True)
        ii = jnp.min(jnp.where(cur == mm, row, N_EXPERTS), axis=0, keepdims=True)
        hit = row == ii
        idxs.append(ii)
        ws.append(jnp.sum(jnp.where(hit, s, 0.0), axis=0, keepdims=True))
        chosen = jnp.logical_or(chosen, hit)
        cur = jnp.where(hit, ninf, cur)
    wsum = ws[0]
    for wk in ws[1:]:
        wsum = wsum + wk

    chosen_f = jnp.where(chosen, 1.0, 0.0)
    tp = lax.broadcasted_iota(jnp.int32, (tm, tm), 0)
    tc = lax.broadcasted_iota(jnp.int32, (tm, tm), 1)
    before = jnp.where(tp < tc, 1.0, 0.0).astype(BF16)
    rank = _dot(chosen_f.astype(BF16), before) + run_scr[...][:, 0:1]
    run_scr[...] = run_scr[...] + jnp.sum(chosen_f, axis=1, keepdims=True)
    cnt_ref[...] = run_scr[...]

    for kk in range(TOP_K):
        e_ref[kk:kk + 1, :] = idxs[kk]
        w_ref[kk:kk + 1, :] = ws[kk] / wsum * ROUTE_SCALE
        r_ref[kk:kk + 1, :] = jnp.sum(jnp.where(row == idxs[kk], rank, 0.0), axis=0,
                                      keepdims=True).astype(jnp.int32)


def _slot_kernel(e_ref, r_ref, ps_ref, o_ref):
    row = lax.broadcasted_iota(jnp.int32, (N_EXPERTS, e_ref.shape[1]), 0)
    ps = ps_ref[...][:, 0:1]
    for kk in range(TOP_K):
        first = jnp.sum(jnp.where(row == e_ref[kk:kk + 1, :], ps, 0.0), axis=0, keepdims=True)
        o_ref[kk:kk + 1, :] = first.astype(jnp.int32) + r_ref[kk:kk + 1, :]


def _slots(top_e, rank, pstart, tm):
    k, n = top_e.shape
    blk = pl.BlockSpec((k, tm), lambda i: (0, i))
    return pl.pallas_call(
        _slot_kernel,
        grid=(n // tm,),
        in_specs=[blk, blk, _resident(pstart.shape)],
        out_specs=blk,
        out_shape=jax.ShapeDtypeStruct((k, n), jnp.int32),
        compiler_params=_params("parallel"),
        name="slots",
    )(top_e, rank, pstart)


SC_WINDOW = 128


def _sc_mesh():
    return plsc.VectorSubcoreMesh(core_axis_name="core", subcore_axis_name="subcore")


def _sc_workers():
    info = plsc.get_sparse_core_info()
    return info.num_cores, info.num_cores * info.num_subcores


def _dispatch_rows(x, pos, p_rows):
    n, w = x.shape
    kk = pos.shape[0]
    ncores, nw = _sc_workers()
    steps = n // nw // SC_WINDOW
    pos4 = pos.reshape(kk, nw, steps, SC_WINDOW).transpose(1, 2, 0, 3)

    @functools.partial(
        pl.kernel, mesh=_sc_mesh(),
        out_type=jax.ShapeDtypeStruct((p_rows, w), x.dtype),
        scratch_types=[pltpu.VMEM((kk, SC_WINDOW), jnp.int32),
                       pltpu.VMEM((SC_WINDOW, w), x.dtype),
                       pltpu.SemaphoreType.DMA],
    )
    def scatter(x_hbm, i_hbm, o_hbm, idx_v, rows_v, sem):
        wid = lax.axis_index("subcore") * ncores + lax.axis_index("core")

        @pl.loop(0, steps)
        def _(s):
            base = pl.multiple_of((wid * steps + s) * SC_WINDOW, SC_WINDOW)
            pltpu.sync_copy(i_hbm.at[wid, s], idx_v)
            pltpu.sync_copy(x_hbm.at[pl.ds(base, SC_WINDOW)], rows_v)
            copies = [pltpu.make_async_copy(rows_v, o_hbm.at[idx_v.at[j]], sem) for j in range(kk)]
            for cp in copies:
                cp.start()
            for cp in copies:
                cp.wait()

    return scatter(x, pos4)


def _gather_rows(x, idx):
    m = idx.shape[0]
    w = x.shape[1]
    ncores, nw = _sc_workers()
    steps = m // nw // SC_WINDOW
    idx3 = idx.reshape(nw, steps, SC_WINDOW)

    @functools.partial(
        pl.kernel, mesh=_sc_mesh(),
        out_type=jax.ShapeDtypeStruct((m, w), x.dtype),
        scratch_types=[pltpu.VMEM((steps, SC_WINDOW), jnp.int32),
                       pltpu.VMEM((SC_WINDOW, w), x.dtype),
                       pltpu.SemaphoreType.DMA],
    )
    def gather(x_hbm, i_hbm, o_hbm, idx_v, rows_v, sem):
        wid = lax.axis_index("subcore") * ncores + lax.axis_index("core")
        pltpu.sync_copy(i_hbm.at[wid], idx_v)

        @pl.loop(0, steps)
        def _(s):
            pltpu.async_copy(x_hbm.at[idx_v.at[s]], rows_v, sem).wait()
            base = pl.multiple_of((wid * steps + s) * SC_WINDOW, SC_WINDOW)
            pltpu.sync_copy(rows_v, o_hbm.at[pl.ds(base, SC_WINDOW)])

    return gather(x, idx3)


def _experts_kernel(blk0_ref, nblk_ref, cnt_ref, nu_ref,
                    x_hbm, wg_ref, wu_ref, wd_ref, y_hbm,
                    xbuf, ybuf, wg_scr, wu_scr, wd_scr, in_sem, out_sem):
    e = pl.program_id(0)
    n_used = nu_ref[0]
    blk0 = blk0_ref[e]
    ns = EXPERT_SLOTS

    def x_copy(g):
        return pltpu.make_async_copy(x_hbm.at[pl.ds(pl.multiple_of(g * MOE_BLOCK, MOE_BLOCK), MOE_BLOCK)],
                                     xbuf.at[g % ns], in_sem.at[g % ns])

    def y_copy(g):
        return pltpu.make_async_copy(ybuf.at[g % ns],
                                     y_hbm.at[pl.ds(pl.multiple_of(g * MOE_BLOCK, MOE_BLOCK), MOE_BLOCK)],
                                     out_sem.at[g % ns])

    @pl.when(e == 0)
    def _():
        for g0 in range(ns - 1):
            @pl.when(g0 < n_used)
            def _():
                x_copy(g0).start()

    wg_scr[...] = wg_ref[0].astype(BF16)
    wu_scr[...] = wu_ref[0].astype(BF16)
    wd_scr[...] = wd_ref[0].astype(BF16)

    def block(b, carry):
        g = blk0 + b
        x_copy(g).wait()

        @pl.when(g + ns - 1 < n_used)
        def _():
            x_copy(g + ns - 1).start()

        rid = lax.broadcasted_iota(jnp.int32, (MOE_BLOCK, xbuf.shape[2]), 0)
        xp = jnp.where(rid < cnt_ref[e] - b * MOE_BLOCK, xbuf[g % ns], 0)
        x = _unpack_bf16_pairs(xp).astype(BF16)
        gt = _dot(x, wg_scr[...])
        up = _dot(x, wu_scr[...])
        a = (gt * _sigmoid(gt) * up).astype(BF16)
        y = _pack_bf16_pairs(_dot(a, wd_scr[...]))

        @pl.when(g >= ns)
        def _():
            y_copy(g - ns).wait()

        ybuf[g % ns] = y
        y_copy(g).start()
        return carry

    lax.fori_loop(0, nblk_ref[e], block, 0)

    @pl.when(e == pl.num_programs(0) - 1)
    def _():
        for back in range(ns, 0, -1):
            @pl.when(n_used >= back)
            def _():
                y_copy(n_used - back).wait()


def _experts(blk0, nblk, counts, n_used, xs, wg, wu, wd):
    p, dp = xs.shape
    ne, d, ff = wg.shape
    grid_spec = pltpu.PrefetchScalarGridSpec(
        num_scalar_prefetch=4,
        grid=(ne,),
        in_specs=[pl.BlockSpec(memory_space=pl.ANY),
                  pl.BlockSpec((1, d, ff), lambda e, *_: (e, 0, 0)),
                  pl.BlockSpec((1, d, ff), lambda e, *_: (e, 0, 0)),
                  pl.BlockSpec((1, ff, d), lambda e, *_: (e, 0, 0))],
        out_specs=pl.BlockSpec(memory_space=pl.ANY),
        scratch_shapes=[pltpu.VMEM((EXPERT_SLOTS, MOE_BLOCK, dp), jnp.int32),
                        pltpu.VMEM((EXPERT_SLOTS, MOE_BLOCK, dp), jnp.int32),
                        pltpu.VMEM((d, ff), BF16), pltpu.VMEM((d, ff), BF16), pltpu.VMEM((ff, d), BF16),
                        pltpu.SemaphoreType.DMA((EXPERT_SLOTS,)),
                        pltpu.SemaphoreType.DMA((EXPERT_SLOTS,))],
    )
    return pl.pallas_call(
        _experts_kernel,
        grid_spec=grid_spec,
        out_shape=jax.ShapeDtypeStruct((p, dp), jnp.int32),
        compiler_params=_params("arbitrary"),
        name="experts",
    )(blk0, nblk, counts, n_used, xs, wg, wu, wd)


def _combine_kernel(base_ref, mod_ref, w_ref, y_ref, o_ref):
    acc = None
    for kk in range(TOP_K):
        term = w_ref[:, kk:kk + 1] * _unpack_bf16_pairs(y_ref[kk])
        acc = term if acc is None else acc + term
    o_ref[0] = base_ref[0] + mod_ref[0, 5:6, :] * acc


def _combine(acc, bi, mod, w_tk, yg, tm):
    b, t, d = acc.shape
    nt = t // tm
    tok = pl.BlockSpec((1, tm, d), lambda i: (bi, i, 0))
    return pl.pallas_call(
        _combine_kernel,
        grid=(nt,),
        in_specs=[tok, pl.BlockSpec((1, 8, d), lambda i: (bi, 0, 0)),
                  pl.BlockSpec((tm, TOP_K), lambda i: (bi * nt + i, 0)),
                  pl.BlockSpec((TOP_K, tm, d // 2), lambda i: (0, i, 0))],
        out_specs=tok,
        out_shape=jax.ShapeDtypeStruct((b, t, d), F32),
        input_output_aliases={0: 0},
        compiler_params=_params("parallel"),
        name="combine",
    )(acc, mod, w_tk, yg)


def _rope_tables(t, tm):
    half = M_DQK // 2
    nf = half // 2
    inv = jnp.asarray(np.power(ROPE_BASE, -np.arange(nf, dtype=np.float32) / nf).astype(np.float32))
    ar = jnp.arange(t // GRID_W, dtype=F32)[:, None] * inv[None, :]
    ac = jnp.arange(GRID_W, dtype=F32)[:, None] * inv[None, :]
    zr, zc = jnp.zeros_like(ar), jnp.zeros_like(ac)
    rcos = jnp.concatenate([jnp.cos(ar), jnp.cos(ar), zr, zr], axis=1)
    rsin = jnp.concatenate([-jnp.sin(ar), jnp.sin(ar), zr, zr], axis=1)
    ccos = jnp.tile(jnp.concatenate([zc, zc, jnp.cos(ac), jnp.cos(ac)], axis=1), (tm // GRID_W, 1))
    csin = jnp.tile(jnp.concatenate([zc, zc, -jnp.sin(ac), jnp.sin(ac)], axis=1), (tm // GRID_W, 1))
    return rcos, rsin, ccos, csin


_IN_SIZES = (512, 512, 1024, 1024, 16, 512, 512, 512, 1024, 1024)
_IN_OFFS = tuple(int(v) for v in np.concatenate([[0], np.cumsum(_IN_SIZES)]))


def _arrange_kernel(w_ref, wa_ref, wt_ref):
    seg = lambda i: w_ref[:, _IN_OFFS[i]:_IN_OFFS[i + 1]]
    mq, mk, mv, mo, _, nq, nk, nv, gm, gn = [seg(i) for i in range(10)]
    g0 = _IN_OFFS[4]
    c = w_ref[:, g0:g0 + LANES]
    lane = lax.broadcasted_iota(jnp.int32, c.shape, 1)
    left4, left8 = pltpu.roll(c, LANES - 4, 1), pltpu.roll(c, LANES - 8, 1)
    gi = jnp.where(lane < 4, c, jnp.where(lane < 8, left4, 0.0))
    gf = jnp.where(lane < 4, left4, jnp.where(lane < 8, left8, 0.0))
    wa_ref[...] = jnp.concatenate([mq, mv, mo, gf, nq, nk, nv, gm, gn], axis=1).astype(BF16)
    wt_ref[...] = jnp.concatenate([jnp.transpose(mk), jnp.transpose(gi), jnp.transpose(gf)],
                                  axis=0).astype(BF16)


def _arrange_w_in(w_in, b_mgate):
    d = w_in.shape[0]
    tr = 256
    wt_rows = M_HEADS * M_DQK + 2 * LANES
    w_all, wt_all = pl.pallas_call(
        _arrange_kernel,
        grid=(d // tr,),
        in_specs=[pl.BlockSpec((tr, w_in.shape[1]), lambda i: (i, 0))],
        out_specs=[pl.BlockSpec((tr, W_COLS), lambda i: (i, 0)),
                   pl.BlockSpec((wt_rows, tr), lambda i: (0, i))],
        out_shape=[jax.ShapeDtypeStruct((d, W_COLS), BF16),
                   jax.ShapeDtypeStruct((wt_rows, d), BF16)],
        compiler_params=_params("parallel"),
        name="arrange_w_in",
    )(w_in)
    bpad = jnp.zeros((LANES - 2 * M_HEADS,), F32)
    bi = jnp.concatenate([b_mgate[0:4], b_mgate[8:12], bpad])
    bf = jnp.concatenate([b_mgate[4:8], b_mgate[12:16], bpad])
    bg = jnp.concatenate([bf[None, :], jnp.zeros((7, LANES), F32)], axis=0)
    bgt = jnp.concatenate([bi, bf])[:, None]
    return w_all, wt_all, bg, bgt


def _segment_mats():
    na_w = NA_HEADS * NA_DH
    seg = np.zeros((na_w, LANES), np.float32)
    seg[np.arange(na_w), np.arange(na_w) // NA_DH] = 1.0
    return jnp.asarray(seg, BF16), jnp.asarray(seg.T.copy(), BF16)


def kernel(x, c, ctx, c_ctx, w_ada, b_ada, w_in, b_mgate, m_norm_w, na_qn_w, na_kn_w, na_rpb,
           w_br_m, w_br_na, w_out, w_router, router_bias, w_exp_gate, w_exp_up, w_exp_down,
           w_sh_gate, w_sh_up, w_sh_down):
    b, t, d = x.shape
    n = b * t
    rows = t // GRID_W
    l = 0

    cc = jnp.concatenate([c, c_ctx[None, :], jnp.zeros((8 - b - 1, d), F32)], axis=0)
    mod = _ada(cc, w_ada[l], b_ada[l])
    mod = mod.reshape(8, 6, d)
    mod = jnp.concatenate([mod, jnp.zeros((8, 2, d), F32)], axis=1)
    mod_x = mod[:b]
    mod_c = jnp.broadcast_to(mod[b:b + 1], (b, 8, d))

    w_all, wt_all, bg, bgt = _arrange_w_in(w_in[l], b_mgate[l])
    seg, segt = _segment_mats()
    qnw = jnp.tile(na_qn_w[l], NA_HEADS)[None, :]
    knw = jnp.tile(na_kn_w[l], NA_HEADS)[None, :]
    tm = min(512, t)

    cp = _inproj(ctx, mod_c, w_all, wt_all, bg, bgt, qnw, knw, seg, segt, None,
                 min(tm, ctx.shape[1]))
    xp = _inproj(x, mod_x, w_all, wt_all, bg, bgt, qnw, knw, seg, segt, _rope_tables(t, tm), tm)
    cmq, cmv, _, cgf, _, cnk, cnv, _, _, cmkt, cgit, cgft = cp
    mq, mv, mo, gf, nq, nk, nv, gm, gn, mkt, git, gft = xp

    c0 = jnp.zeros((b, 8, M_DQK, MLSTM_EXT), F32)
    m0 = jnp.zeros((b, 8, LANES), F32)
    _, _, c1, m1 = _mlstm(cmq, cmkt, cmv, cgf, cgit, cgft, c0, m0)
    hf, hb, _, _ = _mlstm(mq, mkt, mv, gf, git, gft, c1, m1)

    yna = _na(nq, nk, nv, cnk, cnv, _na_bias_table(na_rpb[l], rows))

    bias_col = jnp.broadcast_to(router_bias[l][:, None], (N_EXPERTS, LANES))
    h2p, base, top_e, top_w, rank, cnt = _post(
        x, mod_x, hf, hb, mo, yna, gm, gn, m_norm_w[l][None, :],
        w_br_m[l].astype(BF16), w_br_na[l].astype(BF16), w_out[l].astype(BF16),
        w_router[l].T.astype(BF16), w_sh_gate[l].astype(BF16), w_sh_up[l].astype(BF16),
        w_sh_down[l].astype(BF16), bias_col, tm)

    counts = cnt[:, 0].astype(jnp.int32)
    padded = (counts + MOE_BLOCK - 1) // MOE_BLOCK * MOE_BLOCK
    pend = jnp.cumsum(padded)
    pstart = pend - padded
    nb = -(-(n * TOP_K) // MOE_BLOCK) + N_EXPERTS
    p_rows = nb * MOE_BLOCK
    n_used = pend[-1:] // MOE_BLOCK

    pstart_col = jnp.broadcast_to(pstart.astype(F32)[:, None], (N_EXPERTS, LANES))
    pos = _slots(top_e, rank, pstart_col, min(512, n))
    xs = _dispatch_rows(h2p, pos, p_rows)
    ys = _experts(pstart // MOE_BLOCK, padded // MOE_BLOCK, counts, n_used, xs,
                  w_exp_gate[l], w_exp_up[l], w_exp_down[l])
    w_tk = top_w.T
    out = base
    for bi in range(b):
        idx = pos[:, bi * t:(bi + 1) * t].reshape(-1)
        yg = _gather_rows(ys, idx).reshape(TOP_K, t, d // 2)
        out = _combine(out, bi, mod_x, w_tk, yg, tm)
    return out
```

```python
import functools

import numpy as np
import jax
import jax.numpy as jnp
from jax import lax
from jax.experimental import pallas as pl
from jax.experimental.pallas import tpu as pltpu
from jax.experimental.pallas import tpu_sc as plsc

F32 = jnp.float32
BF16 = jnp.bfloat16

EPS = 1e-6
GRID_W = 64
M_HEADS, M_DQK, M_DV = 4, 128, 256
ROPE_BASE = 10000.0
NA_HEADS, NA_DH, NA_KH, NA_KW = 8, 64, 8, 16
N_EXPERTS, TOP_K, N_GROUPS, TOPK_GROUPS = 256, 8, 8, 4
ROUTE_SCALE = 2.5

LANES = 128
VMEM_LIMIT = 56 * 1024 * 1024
NEG = -1e30

MLSTM_CHUNK = 256
NA_ROWS = 4
NA_KEY_ROWS = NA_ROWS + NA_KH - 1
MOE_BLOCK = 512
EXPERT_SLOTS = 4

_W_SEGS = (("mq", 512), ("mv", 1024), ("mo", 1024), ("gf", 128),
           ("nq", 512), ("nk", 512), ("nv", 512), ("gm", 1024), ("gn", 1024))
_W_OFF = {}
_o = 0
for _n, _w in _W_SEGS:
    _W_OFF[_n] = (_o, _w)
    _o += _w
W_COLS = _o


def _dot(a, b):
    return jnp.dot(a, b, preferred_element_type=F32)


def _dot_nt(a, b):
    return lax.dot_general(a, b, (((1,), (1,)), ((), ())), preferred_element_type=F32)


def _sigmoid(x):
    return 1.0 / (1.0 + jnp.exp(-x))


def _pack_bf16_pairs(v):
    w = v.shape[1] // 2
    bits = pltpu.bitcast(v.astype(BF16).astype(F32), jnp.int32)
    return lax.shift_right_logical(bits[:, :w], 16) | bits[:, w:]


def _unpack_bf16_pairs(p):
    lo = pltpu.bitcast(lax.shift_left(p, 16), F32)
    hi = pltpu.bitcast(p & jnp.int32(-65536), F32)
    return jnp.concatenate([lo, hi], axis=1)


def _params(*sem):
    return pltpu.CompilerParams(dimension_semantics=sem, vmem_limit_bytes=VMEM_LIMIT)


def _resident(shape):
    nd = len(shape)
    return pl.BlockSpec(shape, lambda *_: (0,) * nd, pipeline_mode=pl.Buffered(1))


def _ada_kernel(c_ref, w_ref, b_ref, o_ref):
    c = c_ref[...]
    s = c * _sigmoid(c)
    o_ref[...] = _dot(s.astype(BF16), w_ref[...].astype(BF16)) + b_ref[...]


def _ada(cc, w_ada, b_ada):
    d = cc.shape[1]
    n = w_ada.shape[1]
    return pl.pallas_call(
        _ada_kernel,
        grid=(n // d,),
        in_specs=[pl.BlockSpec((8, d), lambda j: (0, 0)),
                  pl.BlockSpec((d, d), lambda j: (0, j)),
                  pl.BlockSpec((1, d), lambda j: (0, j))],
        out_specs=pl.BlockSpec((8, d), lambda j: (0, j)),
        out_shape=jax.ShapeDtypeStruct((8, n), F32),
        compiler_params=_params("arbitrary"),
        name="ada",
    )(cc, w_ada, b_ada.reshape(1, n))


def _rope_rotate(t, cos, sin):
    lane = lax.broadcasted_iota(jnp.int32, t.shape, 1)
    partner = jnp.where((lane & 32) == 0, pltpu.roll(t, 96, 1), pltpu.roll(t, 32, 1))
    return t * cos + partner * sin


def _rope_rotate_t(t, cos, sin):
    q = M_DQK // 4
    partner = jnp.concatenate([t[q:2 * q], t[0:q], t[3 * q:4 * q], t[2 * q:3 * q]], axis=0)
    return t * cos + partner * sin


def _inproj_kernel(*refs, rope):
    if rope:
        (x_ref, mod_ref, w_ref, wt_ref, bg_ref, bgt_ref, qnw_ref, knw_ref, seg_ref, segt_ref,
         rcos_ref, rsin_ref, ccos_ref, csin_ref,
         mq_ref, mv_ref, mo_ref, gf_ref, nq_ref, nk_ref, nv_ref, gm_ref, gn_ref,
         mkt_ref, git_ref, gft_ref) = refs
    else:
        (x_ref, mod_ref, w_ref, wt_ref, bg_ref, bgt_ref, qnw_ref, knw_ref, seg_ref, segt_ref,
         mq_ref, mv_ref, mo_ref, gf_ref, nq_ref, nk_ref, nv_ref, gm_ref, gn_ref,
         mkt_ref, git_ref, gft_ref) = refs
    x = x_ref[0]
    xn = x * lax.rsqrt(jnp.mean(x * x, axis=-1, keepdims=True) + EPS)
    h = xn * (1.0 + mod_ref[0, 1:2, :]) + mod_ref[0, 0:1, :]
    hb = h.astype(BF16)

    def proj(name):
        off, width = _W_OFF[name]
        return _dot(hb, w_ref[:, off:off + width])

    def head_rms(t, w_row, scale):
        ss = _dot((t * t).astype(BF16), seg_ref[...])
        r = lax.rsqrt(ss * (1.0 / NA_DH) + EPS)
        r_hi = r.astype(BF16)
        r_lo = (r - r_hi.astype(F32)).astype(BF16)
        rb = _dot(r_hi, segt_ref[...]) + _dot(r_lo, segt_ref[...])
        return t * rb * w_row * scale

    mq = proj("mq") * (M_DQK ** -0.5)
    if rope:
        tm = x.shape[0]
        spread = lambda r: jnp.broadcast_to(r[:, None, :], (tm // GRID_W, GRID_W, LANES)).reshape(tm, LANES)
        cos = spread(rcos_ref[...]) + ccos_ref[...]
        sin = spread(rsin_ref[...]) + csin_ref[...]
        mq = jnp.concatenate([_rope_rotate(mq[:, i * LANES:(i + 1) * LANES], cos, sin)
                              for i in range(M_HEADS)], axis=1)
    mq_ref[0] = mq.astype(BF16)
    mv_ref[0] = proj("mv").astype(BF16)
    mo_ref[0] = _sigmoid(proj("mo")).astype(BF16)
    gf_ref[0] = proj("gf") + bg_ref[0:1, :]
    nq_ref[0] = head_rms(proj("nq"), qnw_ref[...], NA_DH ** -0.5).astype(BF16)
    nk_ref[0] = head_rms(proj("nk"), knw_ref[...], 1.0).astype(BF16)
    nv_ref[0] = proj("nv").astype(BF16)
    gm_ref[0] = _sigmoid(proj("gm")).astype(BF16)
    gn_ref[0] = _sigmoid(proj("gn")).astype(BF16)

    qk_w = M_HEADS * M_DQK
    mkt = _dot_nt(wt_ref[0:qk_w, :], hb)
    if rope:
        cost, sint = jnp.transpose(cos), jnp.transpose(sin)
        mkt = jnp.concatenate([_rope_rotate_t(mkt[i * M_DQK:(i + 1) * M_DQK], cost, sint)
                               for i in range(M_HEADS)], axis=0)
    mkt_ref[0] = mkt.astype(BF16)
    git_ref[0] = _dot_nt(wt_ref[qk_w:qk_w + LANES, :], hb) + bgt_ref[0:LANES, :]
    gft_ref[0] = _dot_nt(wt_ref[qk_w + LANES:qk_w + 2 * LANES, :], hb) + bgt_ref[LANES:2 * LANES, :]


def _inproj(x, mod, w_all, wt_all, bg, bgt, qnw, knw, seg, segt, rope_tabs, tm):
    b, t, d = x.shape
    rope = rope_tabs is not None
    tok = lambda w: pl.BlockSpec((1, tm, w), lambda bi, i: (bi, i, 0))
    tok_t = lambda w: pl.BlockSpec((1, w, tm), lambda bi, i: (bi, 0, i))
    in_specs = [tok(d),
                pl.BlockSpec((1, 8, d), lambda bi, i: (bi, 0, 0)),
                _resident(w_all.shape), _resident(wt_all.shape), _resident(bg.shape),
                _resident(bgt.shape), _resident(qnw.shape),
                _resident(knw.shape), _resident(seg.shape), _resident(segt.shape)]
    args = [x, mod, w_all, wt_all, bg, bgt, qnw, knw, seg, segt]
    if rope:
        in_specs += [pl.BlockSpec((tm // GRID_W, LANES), lambda bi, i: (i, 0))] * 2
        in_specs += [_resident((tm, LANES))] * 2
        args += list(rope_tabs)
    widths = [("mq", BF16), ("mv", BF16), ("mo", BF16), ("gf", F32),
              ("nq", BF16), ("nk", BF16), ("nv", BF16), ("gm", BF16), ("gn", BF16)]
    out_specs = [tok(_W_OFF[n][1]) for n, _ in widths]
    out_shape = [jax.ShapeDtypeStruct((b, t, _W_OFF[n][1]), dt) for n, dt in widths]
    out_specs += [tok_t(M_HEADS * M_DQK), tok_t(LANES), tok_t(LANES)]
    out_shape += [jax.ShapeDtypeStruct((b, M_HEADS * M_DQK, t), BF16),
                  jax.ShapeDtypeStruct((b, LANES, t), F32),
                  jax.ShapeDtypeStruct((b, LANES, t), F32)]
    return pl.pallas_call(
        functools.partial(_inproj_kernel, rope=rope),
        grid=(b, t // tm),
        in_specs=in_specs, out_specs=out_specs, out_shape=out_shape,
        compiler_params=_params("parallel", "parallel"),
        name="inproj_rope" if rope else "inproj_ctx",
    )(*args)


def _log_sigmoid(x):
    return jnp.minimum(x, 0.0) - jnp.log(1.0 + jnp.exp(-jnp.abs(x)))


def _dot_split(a, b, split_a):
    x = a if split_a else b
    hi = x.astype(BF16)
    lo = (x - hi.astype(F32)).astype(BF16)
    return (_dot(hi, b) + _dot(lo, b)) if split_a else (_dot(a, hi) + _dot(a, lo))


MLSTM_EXT = M_DV + LANES


def _mlstm_kernel(qf_ref, ktf_ref, vf_ref, gff_ref, gitf_ref, gftf_ref,
                  qb_ref, ktb_ref, vb_ref, gfb_ref, gitb_ref, gftb_ref,
                  c0_ref, m0_ref,
                  hf_ref, hb_ref, cn_ref, mn_ref,
                  *scratch):
    c_scrs, m_scr = scratch[:2 * M_HEADS], scratch[2 * M_HEADS]
    step = pl.program_id(1)
    L = qf_ref.shape[1]
    nu = 2 * M_HEADS

    @pl.when(step == 0)
    def _():
        for j, c_scr in enumerate(c_scrs):
            c_scr[...] = c0_ref[0, j]
        m_scr[...] = m0_ref[0]

    row_i = lax.broadcasted_iota(jnp.int32, (L, L), 0)
    col_i = lax.broadcasted_iota(jnp.int32, (L, L), 1)
    lower = col_i <= row_i
    upper = col_i >= row_i
    tri_lo = jnp.where(lower, 1.0, 0.0).astype(BF16)
    tri_up = jnp.where(upper, 1.0, 0.0).astype(BF16)

    is_f = lax.broadcasted_iota(jnp.int32, (nu, L), 0) < M_HEADS
    gi_t = jnp.where(is_f, gitf_ref[0, 0:nu, :], gitb_ref[0, 0:nu, :])
    ls_tf = _log_sigmoid(gftf_ref[0, 0:nu, :])
    ls_tb = _log_sigmoid(gftb_ref[0, 0:nu, :])
    b_t = jnp.where(is_f, _dot_split(ls_tf, tri_up, True), _dot_split(ls_tb, tri_lo, True))
    u_t = gi_t - b_t
    g_c = jnp.sum(jnp.where(is_f, ls_tf, ls_tb), axis=1, keepdims=True)
    m_prev = m_scr[...]
    a_t = g_c + u_t
    m_new = jnp.maximum(g_c + m_prev, jnp.max(a_t, axis=1, keepdims=True))
    decay = jnp.exp(g_c + m_prev - m_new)
    wa_t = jnp.exp(a_t - jnp.concatenate([m_new] * (L // LANES), axis=1))
    m_scr[...] = m_new

    ones = jnp.ones((L, LANES), BF16)
    dirs = ((qf_ref, ktf_ref, vf_ref, gff_ref, hf_ref, lower, tri_lo),
            (qb_ref, ktb_ref, vb_ref, gfb_ref, hb_ref, upper, tri_up))
    for d, (q_ref, kt_ref, v_ref, gf_ref, h_ref, mask, tri) in enumerate(dirs):
        bcum = _dot_split(tri, _log_sigmoid(gf_ref[0]), False)
        for hd in range(M_HEADS):
            j = d * M_HEADS + hd
            c_scr = c_scrs[j]
            q = q_ref[0, :, hd * M_DQK:(hd + 1) * M_DQK]
            k_t = kt_ref[0, hd * M_DQK:(hd + 1) * M_DQK, :]
            v_ext = jnp.concatenate([v_ref[0, :, hd * M_DV:(hd + 1) * M_DV], ones], axis=1)
            u_row = u_t[j:j + 1, :]
            mp_row = m_prev[j:j + 1, :]
            c_prev = c_scr[...]

            m_loc = jnp.max(jnp.where(mask, u_row, NEG), axis=1, keepdims=True)
            m_rep = jnp.maximum(jnp.broadcast_to(m_loc, (L, LANES)), mp_row)
            m_wide = jnp.concatenate([m_rep] * (L // LANES), axis=1)
            dmat = jnp.exp(jnp.where(mask, u_row - m_wide, NEG))
            s = (_dot(q, k_t) * dmat).astype(BF16)
            qw = (q.astype(F32) * jnp.exp(mp_row - m_rep)).astype(BF16)
            r = _dot(s, v_ext) + _dot(qw, c_prev.astype(BF16))
            b_rep = jnp.broadcast_to(bcum[:, j:j + 1], (L, LANES))
            dn = jnp.maximum(jnp.abs(r[:, M_DV:]), jnp.exp(-(b_rep + m_rep)))
            h_ref[0, :, hd * M_DV:(hd + 1) * M_DV] = (
                r[:, :M_DV] / jnp.concatenate([dn] * (M_DV // LANES), axis=1)).astype(h_ref.dtype)

            kw = (k_t.astype(F32) * wa_t[j:j + 1, :]).astype(BF16)
            dec = jnp.concatenate([decay[j:j + 1, :]] * (MLSTM_EXT // LANES), axis=1)
            c_scr[...] = dec * c_prev + _dot(kw, v_ext)

    @pl.when(step == pl.num_programs(1) - 1)
    def _():
        for j, c_scr in enumerate(c_scrs):
            cn_ref[0, j] = c_scr[...]
        mn_ref[0] = m_scr[...]


def _mlstm(q, kt, v, gf, git, gft, c0, m0):
    b, t, _ = q.shape
    L = min(MLSTM_CHUNK, t)
    nc = t // L
    fwd = lambda w: pl.BlockSpec((1, L, w), lambda bi, i: (bi, i, 0))
    bwd = lambda w: pl.BlockSpec((1, L, w), lambda bi, i: (bi, nc - 1 - i, 0))
    fwd_t = lambda w: pl.BlockSpec((1, w, L), lambda bi, i: (bi, 0, i))
    bwd_t = lambda w: pl.BlockSpec((1, w, L), lambda bi, i: (bi, 0, nc - 1 - i))
    st_c = pl.BlockSpec((1, 8, M_DQK, MLSTM_EXT), lambda bi, i: (bi, 0, 0, 0))
    st_v = pl.BlockSpec((1, 8, LANES), lambda bi, i: (bi, 0, 0))
    qk_w, v_w = M_HEADS * M_DQK, M_HEADS * M_DV
    return pl.pallas_call(
        _mlstm_kernel,
        grid=(b, nc),
        in_specs=[fwd(qk_w), fwd_t(qk_w), fwd(v_w), fwd(LANES), fwd_t(LANES), fwd_t(LANES),
                  bwd(qk_w), bwd_t(qk_w), bwd(v_w), bwd(LANES), bwd_t(LANES), bwd_t(LANES),
                  st_c, st_v],
        out_specs=[fwd(v_w), bwd(v_w), st_c, st_v],
        out_shape=[jax.ShapeDtypeStruct((b, t, v_w), BF16),
                   jax.ShapeDtypeStruct((b, t, v_w), BF16),
                   jax.ShapeDtypeStruct(c0.shape, F32),
                   jax.ShapeDtypeStruct(m0.shape, F32)],
        scratch_shapes=([pltpu.VMEM((M_DQK, MLSTM_EXT), F32) for _ in range(2 * M_HEADS)]
                        + [pltpu.VMEM((8, LANES), F32)]),
        compiler_params=_params("parallel", "arbitrary"),
        name="mlstm",
    )(q, kt, v, gf, git, gft, q, kt, v, gf, git, gft, c0, m0)


def _na_kernel(q_ref, k_ref, v_ref, kc_ref, vc_ref, bias_ref, o_ref, *, rows):
    r0 = pl.program_id(2) * NA_ROWS
    ks = jnp.clip(r0 - NA_KH // 2, 0, rows - NA_KEY_ROWS)
    kstart = pl.multiple_of(ks * GRID_W, GRID_W)
    nkeys = NA_KEY_ROWS * GRID_W
    kblk = k_ref[0, pl.ds(kstart, nkeys), :]
    vblk = v_ref[0, pl.ds(kstart, nkeys), :]
    kc = kc_ref[0]
    vc = vc_ref[0]
    q = q_ref[0]
    lane = lax.broadcasted_iota(jnp.int32, q.shape, 1)
    outs = []
    for hh in range(2):
        in_head = (lane < NA_DH) if hh == 0 else (lane >= NA_DH)
        qm = jnp.where(in_head, q, jnp.zeros_like(q))
        sw = _dot_nt(qm, kblk) + bias_ref[hh, 0]
        sc = _dot_nt(qm, kc)
        m = jnp.maximum(jnp.max(sw, axis=1, keepdims=True), jnp.max(sc, axis=1, keepdims=True))
        ew = jnp.exp(sw - m)
        ec = jnp.exp(sc - m)
        l = jnp.sum(ew, axis=1, keepdims=True) + jnp.sum(ec, axis=1, keepdims=True)
        o = _dot(ew.astype(BF16), vblk) + _dot(ec.astype(BF16), vc)
        outs.append(o / l)
    o_ref[0] = jnp.where(lane < NA_DH, outs[0], outs[1]).astype(o_ref.dtype)


def _na(nq, nk, nv, cnk, cnv, bias):
    b, t, w = nq.shape
    rows = t // GRID_W
    tq = NA_ROWS * GRID_W
    nrb = rows // NA_ROWS
    nctx = cnk.shape[1]
    kind = lambda rb: jnp.where(rb == 0, 0, jnp.where(rb == nrb - 1, 2, 1))
    return pl.pallas_call(
        functools.partial(_na_kernel, rows=rows),
        grid=(b, w // LANES, nrb),
        in_specs=[pl.BlockSpec((1, tq, LANES), lambda bi, hp, rb: (bi, rb, hp)),
                  pl.BlockSpec((1, t, LANES), lambda bi, hp, rb: (bi, 0, hp)),
                  pl.BlockSpec((1, t, LANES), lambda bi, hp, rb: (bi, 0, hp)),
                  pl.BlockSpec((1, nctx, LANES), lambda bi, hp, rb: (bi, 0, hp)),
                  pl.BlockSpec((1, nctx, LANES), lambda bi, hp, rb: (bi, 0, hp)),
                  pl.BlockSpec((2, 1, tq, NA_KEY_ROWS * GRID_W),
                               lambda bi, hp, rb: (hp, kind(rb), 0, 0))],
        out_specs=pl.BlockSpec((1, tq, LANES), lambda bi, hp, rb: (bi, rb, hp)),
        out_shape=jax.ShapeDtypeStruct((b, t, w), BF16),
        compiler_params=_params("parallel", "parallel", "arbitrary"),
        name="na",
    )(nq, nk, nv, cnk, cnv, bias)


def _na_bias_table(na_rpb, rows):
    h = na_rpb.shape[0]
    w = GRID_W
    c = np.arange(w)[:, None]
    kj = np.arange(w)[None, :]
    cs = np.clip(c - NA_KW // 2, 0, w - NA_KW)
    col_valid = (kj >= cs) & (kj < cs + NA_KW)
    dc = np.clip(kj - c + (NA_KW - 1), 0, 2 * NA_KW - 2)
    onehot = np.zeros((2 * NA_KW - 1, w, w), np.float32)
    onehot[dc, np.arange(w)[:, None], np.arange(w)[None, :]] = 1.0
    t2 = jnp.einsum("hrd,dck->hrck", na_rpb, jnp.asarray(onehot), precision=lax.Precision.HIGHEST)
    t2 = jnp.where(jnp.asarray(col_valid)[None, None], t2, NEG)
    t2 = jnp.concatenate([t2, jnp.full((h, 1, w, w), NEG, F32)], axis=1)
    invalid = 2 * NA_KH - 1
    dr_idx = np.full((3, NA_ROWS, NA_KEY_ROWS), invalid, np.int32)
    for kind, r0 in enumerate((0, NA_ROWS, rows - NA_ROWS)):
        ks = int(np.clip(r0 - NA_KH // 2, 0, rows - NA_KEY_ROWS))
        for qa in range(NA_ROWS):
            r = r0 + qa
            rs = int(np.clip(r - NA_KH // 2, 0, rows - NA_KH))
            for kl in range(NA_KEY_ROWS):
                ki = ks + kl
                if rs <= ki < rs + NA_KH:
                    dr_idx[kind, qa, kl] = ki - r + NA_KH - 1
    t2t = t2.transpose(0, 2, 1, 3)
    strips = [jnp.concatenate([t2t[:, :, int(dr), :] for dr in dr_idx[kind, qa]], axis=-1)
              for kind in range(3) for qa in range(NA_ROWS)]
    return jnp.stack(strips, axis=1).reshape(h, 3, NA_ROWS * w, NA_KEY_ROWS * w)


def _post_kernel(x_ref, mod_ref, hf_ref, hb_ref, mo_ref, na_ref, gm_ref, gn_ref,
                 mnw_ref, wbm_ref, wbn_ref, wout_ref, wr_ref, wsg_ref, wsu_ref, wsd_ref, rb_ref,
                 h2_ref, base_ref, e_ref, w_ref, r_ref, cnt_ref, run_scr):
    @pl.when(pl.program_id(0) == 0)
    def _():
        run_scr[...] = jnp.zeros_like(run_scr)

    hm = hf_ref[0].astype(F32) + hb_ref[0].astype(F32)
    parts = []
    for hd in range(M_HEADS):
        t = hm[:, hd * M_DV:(hd + 1) * M_DV]
        parts.append(t * lax.rsqrt(jnp.mean(t * t, axis=-1, keepdims=True) + EPS))
    y_m = jnp.concatenate(parts, axis=1) * mnw_ref[...] * mo_ref[0].astype(F32)
    a = _dot(y_m.astype(BF16), wbm_ref[...])
    bn = _dot(na_ref[0], wbn_ref[...])
    z = gm_ref[0].astype(F32) * a + gn_ref[0].astype(F32) * bn
    y = _dot(z.astype(BF16), wout_ref[...])
    x1 = x_ref[0] + mod_ref[0, 2:3, :] * y
    xn = x1 * lax.rsqrt(jnp.mean(x1 * x1, axis=-1, keepdims=True) + EPS)
    h2f = xn * (1.0 + mod_ref[0, 4:5, :]) + mod_ref[0, 3:4, :]
    h2_ref[...] = _pack_bf16_pairs(h2f)
    h2 = h2f.astype(BF16)
    sh = _dot(h2, wsg_ref[...])
    sh = sh * _sigmoid(sh) * _dot(h2, wsu_ref[...])
    base_ref[0] = x1 + mod_ref[0, 5:6, :] * _dot(sh.astype(BF16), wsd_ref[...])
    scores = _sigmoid(_dot_nt(wr_ref[...], h2))
    _route_block(scores, rb_ref, e_ref, w_ref, r_ref, cnt_ref, run_scr)


def _post(x, mod, hf, hb, mo, yna, gm, gn, mnw, wbm, wbn, wout, wr_t, wsg, wsu, wsd, rbias, tm):
    b, t, d = x.shape
    nt = t // tm
    n = b * t
    tok = lambda w: pl.BlockSpec((1, tm, w), lambda s: (s // nt, s % nt, 0))
    rt = lambda: pl.BlockSpec((TOP_K, tm), lambda s: (0, s))
    res = [mnw, wbm, wbn, wout, wr_t, wsg, wsu, wsd, rbias]
    return pl.pallas_call(
        _post_kernel,
        grid=(b * nt,),
        in_specs=[tok(d), pl.BlockSpec((1, 8, d), lambda s: (s // nt, 0, 0)),
                  tok(hf.shape[2]), tok(hb.shape[2]), tok(mo.shape[2]), tok(yna.shape[2]),
                  tok(gm.shape[2]), tok(gn.shape[2])] + [_resident(a.shape) for a in res],
        out_specs=[pl.BlockSpec((tm, d // 2), lambda s: (s, 0)),
                   tok(d), rt(), pl.BlockSpec((tm, TOP_K), lambda s: (s, 0)), rt(),
                   pl.BlockSpec((N_EXPERTS, LANES), lambda s: (0, 0))],
        out_shape=[jax.ShapeDtypeStruct((n, d // 2), jnp.int32),
                   jax.ShapeDtypeStruct((b, t, d), F32),
                   jax.ShapeDtypeStruct((TOP_K, n), jnp.int32),
                   jax.ShapeDtypeStruct((n, TOP_K), F32),
                   jax.ShapeDtypeStruct((TOP_K, n), jnp.int32),
                   jax.ShapeDtypeStruct((N_EXPERTS, LANES), F32)],
        scratch_shapes=[pltpu.VMEM((N_EXPERTS, LANES), F32)],
        compiler_params=_params("arbitrary"),
        name="post",
    )(x, mod, hf, hb, mo, yna, gm, gn, *res)


def _route_block(s, b_ref, e_ref, w_ref, r_ref, cnt_ref, run_scr):
    tm = s.shape[1]
    sel = s + b_ref[...][:, 0:1]
    gsz = N_EXPERTS // N_GROUPS
    ninf = -jnp.inf

    x3 = sel.reshape(N_GROUPS, gsz, tm)
    r3 = lax.broadcasted_iota(jnp.int32, x3.shape, 1)
    m1 = jnp.max(x3, axis=1, keepdims=True)
    i1 = jnp.min(jnp.where(x3 == m1, r3, gsz), axis=1, keepdims=True)
    m2 = jnp.max(jnp.where(r3 == i1, ninf, x3), axis=1)
    gs = m1[:, 0, :] + m2

    gidx = lax.broadcasted_iota(jnp.int32, gs.shape, 0)
    gkeep = jnp.zeros(gs.shape, jnp.bool_)
    cur = gs
    for _ in range(TOPK_GROUPS):
        mm = jnp.max(cur, axis=0, keepdims=True)
        ii = jnp.min(jnp.where(cur == mm, gidx, N_GROUPS), axis=0, keepdims=True)
        hit = gidx == ii
        gkeep = jnp.logical_or(gkeep, hit)
        cur = jnp.where(hit, ninf, cur)
    keep = jnp.broadcast_to(gkeep[:, None, :], x3.shape).reshape(N_EXPERTS, tm)

    row = lax.broadcasted_iota(jnp.int32, s.shape, 0)
    cur = jnp.where(keep, sel, ninf)
    idxs, ws = [], []
    chosen = jnp.zeros(s.shape, jnp.bool_)
    for _ in range(TOP_K):
        mm = jnp.max(cur, axis=0, keepdims=True)
        ii = jnp.min(jnp.where(cur == mm, row, N_EXPERTS), axis=0, keepdims=True)
        hit = row == ii
        idxs.append(ii)
        ws.append(jnp.sum(jnp.where(hit, s, 0.0), axis=0, keepdims=True))
        chosen = jnp.logical_or(chosen, hit)
        cur = jnp.where(hit, ninf, cur)
    wsum = ws[0]
    for wk in ws[1:]:
        wsum = wsum + wk

    chosen_f = jnp.where(chosen, 1.0, 0.0)
    tp = lax.broadcasted_iota(jnp.int32, (tm, tm), 0)
    tc = lax.broadcasted_iota(jnp.int32, (tm, tm), 1)
    before = jnp.where(tp < tc, 1.0, 0.0).astype(BF16)
    rank = _dot(chosen_f.astype(BF16), before) + run_scr[...][:, 0:1]
    run_scr[...] = run_scr[...] + jnp.sum(chosen_f, axis=1, keepdims=True)
    cnt_ref[...] = run_scr[...]

    for kk in range(TOP_K):
        e_ref[kk:kk + 1, :] = idxs[kk]
        r_ref[kk:kk + 1, :] = jnp.sum(jnp.where(row == idxs[kk], rank, 0.0), axis=0,
                                      keepdims=True).astype(jnp.int32)
    w_ref[...] = jnp.transpose(jnp.concatenate([wk / wsum * ROUTE_SCALE for wk in ws], axis=0))


def _slot_kernel(e_ref, r_ref, ps_ref, o_ref):
    row = lax.broadcasted_iota(jnp.int32, (N_EXPERTS, e_ref.shape[1]), 0)
    ps = ps_ref[...][:, 0:1]
    for kk in range(TOP_K):
        first = jnp.sum(jnp.where(row == e_ref[kk:kk + 1, :], ps, 0.0), axis=0, keepdims=True)
        o_ref[kk:kk + 1, :] = first.astype(jnp.int32) + r_ref[kk:kk + 1, :]


def _slots(top_e, rank, pstart, tm):
    k, n = top_e.shape
    blk = pl.BlockSpec((k, tm), lambda i: (0, i))
    return pl.pallas_call(
        _slot_kernel,
        grid=(n // tm,),
        in_specs=[blk, blk, _resident(pstart.shape)],
        out_specs=blk,
        out_shape=jax.ShapeDtypeStruct((k, n), jnp.int32),
        compiler_params=_params("parallel"),
        name="slots",
    )(top_e, rank, pstart)


SC_WINDOW = 128


def _sc_mesh():
    return plsc.VectorSubcoreMesh(core_axis_name="core", subcore_axis_name="subcore")


def _sc_workers():
    info = plsc.get_sparse_core_info()
    return info.num_cores, info.num_cores * info.num_subcores


def _dispatch_rows(x, pos, p_rows):
    n, w = x.shape
    kk = pos.shape[0]
    ncores, nw = _sc_workers()
    steps = n // nw // SC_WINDOW
    pos4 = pos.reshape(kk, nw, steps, SC_WINDOW).transpose(1, 2, 0, 3)

    @functools.partial(
        pl.kernel, mesh=_sc_mesh(),
        out_type=jax.ShapeDtypeStruct((p_rows, w), x.dtype),
        scratch_types=[pltpu.VMEM((kk, SC_WINDOW), jnp.int32),
                       pltpu.VMEM((SC_WINDOW, w), x.dtype),
                       pltpu.SemaphoreType.DMA],
    )
    def scatter(x_hbm, i_hbm, o_hbm, idx_v, rows_v, sem):
        wid = lax.axis_index("subcore") * ncores + lax.axis_index("core")

        @pl.loop(0, steps)
        def _(s):
            base = pl.multiple_of((wid * steps + s) * SC_WINDOW, SC_WINDOW)
            pltpu.sync_copy(i_hbm.at[wid, s], idx_v)
            pltpu.sync_copy(x_hbm.at[pl.ds(base, SC_WINDOW)], rows_v)
            copies = [pltpu.make_async_copy(rows_v, o_hbm.at[idx_v.at[j]], sem) for j in range(kk)]
            for cp in copies:
                cp.start()
            for cp in copies:
                cp.wait()

    return scatter(x, pos4)


def _gather_rows(x, idx):
    m = idx.shape[0]
    w = x.shape[1]
    ncores, nw = _sc_workers()
    steps = m // nw // SC_WINDOW
    idx3 = idx.reshape(nw, steps, SC_WINDOW)

    @functools.partial(
        pl.kernel, mesh=_sc_mesh(),
        out_type=jax.ShapeDtypeStruct((m, w), x.dtype),
        scratch_types=[pltpu.VMEM((steps, SC_WINDOW), jnp.int32),
                       pltpu.VMEM((SC_WINDOW, w), x.dtype),
                       pltpu.SemaphoreType.DMA],
    )
    def gather(x_hbm, i_hbm, o_hbm, idx_v, rows_v, sem):
        wid = lax.axis_index("subcore") * ncores + lax.axis_index("core")
        pltpu.sync_copy(i_hbm.at[wid], idx_v)

        @pl.loop(0, steps)
        def _(s):
            pltpu.async_copy(x_hbm.at[idx_v.at[s]], rows_v, sem).wait()
            base = pl.multiple_of((wid * steps + s) * SC_WINDOW, SC_WINDOW)
            pltpu.sync_copy(rows_v, o_hbm.at[pl.ds(base, SC_WINDOW)])

    return gather(x, idx3)


def _experts_kernel(blk0_ref, nblk_ref, cnt_ref, row0_ref, half_ref, nu_ref,
                    x_hbm, wg_ref, wu_ref, wd_ref, y_hbm,
                    xbuf, ybuf, wg_scr, wu_scr, wd_scr, in_sem, out_sem):
    e = pl.program_id(0)
    n_used = nu_ref[0]
    blk0 = blk0_ref[e]
    ns = EXPERT_SLOTS
    sizes = (MOE_BLOCK, MOE_BLOCK // 2)

    def x_copy(g, rows):
        r0 = pl.multiple_of(row0_ref[g], MOE_BLOCK // 2)
        return pltpu.make_async_copy(x_hbm.at[pl.ds(r0, rows)], xbuf.at[g % ns, pl.ds(0, rows)],
                                     in_sem.at[g % ns])

    def y_copy(g, rows):
        r0 = pl.multiple_of(row0_ref[g], MOE_BLOCK // 2)
        return pltpu.make_async_copy(ybuf.at[g % ns, pl.ds(0, rows)], y_hbm.at[pl.ds(r0, rows)],
                                     out_sem.at[g % ns])

    def by_size(g, fn):
        for is_half, rows in enumerate(sizes):
            @pl.when(half_ref[g] == is_half)
            def _():
                fn(rows)

    @pl.when(e == 0)
    def _():
        for g0 in range(ns - 1):
            @pl.when(g0 < n_used)
            def _():
                by_size(g0, lambda rows: x_copy(g0, rows).start())

    wg_scr[...] = wg_ref[0].astype(BF16)
    wu_scr[...] = wu_ref[0].astype(BF16)
    wd_scr[...] = wd_ref[0].astype(BF16)

    def block(b, carry):
        g = blk0 + b
        by_size(g, lambda rows: x_copy(g, rows).wait())

        @pl.when(g + ns - 1 < n_used)
        def _():
            by_size(g + ns - 1, lambda rows: x_copy(g + ns - 1, rows).start())

        @pl.when(g >= ns)
        def _():
            by_size(g - ns, lambda rows: y_copy(g - ns, rows).wait())

        def run(rows):
            rid = lax.broadcasted_iota(jnp.int32, (rows, xbuf.shape[2]), 0)
            xp = jnp.where(rid < cnt_ref[e] - b * MOE_BLOCK, xbuf[g % ns, 0:rows], 0)
            x = _unpack_bf16_pairs(xp).astype(BF16)
            gt = _dot(x, wg_scr[...])
            up = _dot(x, wu_scr[...])
            a = (gt * _sigmoid(gt) * up).astype(BF16)
            ybuf[g % ns, 0:rows] = _pack_bf16_pairs(_dot(a, wd_scr[...]))
            y_copy(g, rows).start()

        by_size(g, run)
        return carry

    lax.fori_loop(0, nblk_ref[e], block, 0)

    @pl.when(e == pl.num_programs(0) - 1)
    def _():
        for back in range(ns, 0, -1):
            @pl.when(n_used >= back)
            def _():
                by_size(n_used - back, lambda rows: y_copy(n_used - back, rows).wait())


def _expert_plan(counts, nb_max):
    half = MOE_BLOCK // 2
    units = (counts + half - 1) // half
    nfull, tail = units // 2, units % 2
    nblk = nfull + tail
    pend = jnp.cumsum(units * half)
    pstart = pend - units * half
    blk_end = jnp.cumsum(nblk)
    blk0 = blk_end - nblk
    g = jnp.arange(nb_max, dtype=jnp.int32)
    ne = counts.shape[0]
    eg = jnp.minimum(jnp.sum((blk_end[None, :] <= g[:, None]).astype(jnp.int32), axis=1), ne - 1)
    onehot = (eg[:, None] == jnp.arange(ne, dtype=jnp.int32)[None, :]).astype(jnp.int32)
    pick = lambda v: jnp.sum(onehot * v[None, :], axis=1)
    local = g - pick(blk0)
    is_half = ((local == pick(nfull)) & (pick(tail) == 1)).astype(jnp.int32)
    row0 = jnp.clip(pick(pstart) + local * MOE_BLOCK, 0, pend[-1] - half)
    return pstart, blk0, nblk, row0, is_half, blk_end[-1:]


def _experts(blk0, nblk, counts, row0, is_half, n_used, xs, wg, wu, wd):
    p, dp = xs.shape
    ne, d, ff = wg.shape
    grid_spec = pltpu.PrefetchScalarGridSpec(
        num_scalar_prefetch=6,
        grid=(ne,),
        in_specs=[pl.BlockSpec(memory_space=pl.ANY),
                  pl.BlockSpec((1, d, ff), lambda e, *_: (e, 0, 0)),
                  pl.BlockSpec((1, d, ff), lambda e, *_: (e, 0, 0)),
                  pl.BlockSpec((1, ff, d), lambda e, *_: (e, 0, 0))],
        out_specs=pl.BlockSpec(memory_space=pl.ANY),
        scratch_shapes=[pltpu.VMEM((EXPERT_SLOTS, MOE_BLOCK, dp), jnp.int32),
                        pltpu.VMEM((EXPERT_SLOTS, MOE_BLOCK, dp), jnp.int32),
                        pltpu.VMEM((d, ff), BF16), pltpu.VMEM((d, ff), BF16), pltpu.VMEM((ff, d), BF16),
                        pltpu.SemaphoreType.DMA((EXPERT_SLOTS,)),
                        pltpu.SemaphoreType.DMA((EXPERT_SLOTS,))],
    )
    return pl.pallas_call(
        _experts_kernel,
        grid_spec=grid_spec,
        out_shape=jax.ShapeDtypeStruct((p, dp), jnp.int32),
        compiler_params=_params("arbitrary"),
        name="experts",
    )(blk0, nblk, counts, row0, is_half, n_used, xs, wg, wu, wd)


def _combine_kernel(base_ref, mod_ref, w_ref, y_ref, o_ref):
    acc = None
    for kk in range(TOP_K):
        term = w_ref[:, kk:kk + 1] * _unpack_bf16_pairs(y_ref[kk])
        acc = term if acc is None else acc + term
    o_ref[0] = base_ref[0] + mod_ref[0, 5:6, :] * acc


def _combine(acc, bi, mod, w_tk, yg, tm):
    b, t, d = acc.shape
    nt = t // tm
    tok = pl.BlockSpec((1, tm, d), lambda i: (bi, i, 0))
    return pl.pallas_call(
        _combine_kernel,
        grid=(nt,),
        in_specs=[tok, pl.BlockSpec((1, 8, d), lambda i: (bi, 0, 0)),
                  pl.BlockSpec((tm, TOP_K), lambda i: (bi * nt + i, 0)),
                  pl.BlockSpec((TOP_K, tm, d // 2), lambda i: (0, i, 0))],
        out_specs=tok,
        out_shape=jax.ShapeDtypeStruct((b, t, d), F32),
        input_output_aliases={0: 0},
        compiler_params=_params("parallel"),
        name="combine",
    )(acc, mod, w_tk, yg)


def _rope_tables(t, tm):
    half = M_DQK // 2
    nf = half // 2
    inv = jnp.asarray(np.power(ROPE_BASE, -np.arange(nf, dtype=np.float32) / nf).astype(np.float32))
    ar = jnp.arange(t // GRID_W, dtype=F32)[:, None] * inv[None, :]
    ac = jnp.arange(GRID_W, dtype=F32)[:, None] * inv[None, :]
    zr, zc = jnp.zeros_like(ar), jnp.zeros_like(ac)
    rcos = jnp.concatenate([jnp.cos(ar), jnp.cos(ar), zr, zr], axis=1)
    rsin = jnp.concatenate([-jnp.sin(ar), jnp.sin(ar), zr, zr], axis=1)
    ccos = jnp.tile(jnp.concatenate([zc, zc, jnp.cos(ac), jnp.cos(ac)], axis=1), (tm // GRID_W, 1))
    csin = jnp.tile(jnp.concatenate([zc, zc, -jnp.sin(ac), jnp.sin(ac)], axis=1), (tm // GRID_W, 1))
    return rcos, rsin, ccos, csin


_IN_SIZES = (512, 512, 1024, 1024, 16, 512, 512, 512, 1024, 1024)
_IN_OFFS = tuple(int(v) for v in np.concatenate([[0], np.cumsum(_IN_SIZES)]))


def _arrange_kernel(w_ref, wa_ref, wt_ref):
    seg = lambda i: w_ref[:, _IN_OFFS[i]:_IN_OFFS[i + 1]]
    mq, mk, mv, mo, _, nq, nk, nv, gm, gn = [seg(i) for i in range(10)]
    g0 = _IN_OFFS[4]
    c = w_ref[:, g0:g0 + LANES]
    lane = lax.broadcasted_iota(jnp.int32, c.shape, 1)
    left4, left8 = pltpu.roll(c, LANES - 4, 1), pltpu.roll(c, LANES - 8, 1)
    gi = jnp.where(lane < 4, c, jnp.where(lane < 8, left4, 0.0))
    gf = jnp.where(lane < 4, left4, jnp.where(lane < 8, left8, 0.0))
    wa_ref[...] = jnp.concatenate([mq, mv, mo, gf, nq, nk, nv, gm, gn], axis=1).astype(BF16)
    wt_ref[...] = jnp.concatenate([jnp.transpose(mk), jnp.transpose(gi), jnp.transpose(gf)],
                                  axis=0).astype(BF16)


def _arrange_w_in(w_in, b_mgate):
    d = w_in.shape[0]
    tr = 256
    wt_rows = M_HEADS * M_DQK + 2 * LANES
    w_all, wt_all = pl.pallas_call(
        _arrange_kernel,
        grid=(d // tr,),
        in_specs=[pl.BlockSpec((tr, w_in.shape[1]), lambda i: (i, 0))],
        out_specs=[pl.BlockSpec((tr, W_COLS), lambda i: (i, 0)),
                   pl.BlockSpec((wt_rows, tr), lambda i: (0, i))],
        out_shape=[jax.ShapeDtypeStruct((d, W_COLS), BF16),
                   jax.ShapeDtypeStruct((wt_rows, d), BF16)],
        compiler_params=_params("parallel"),
        name="arrange_w_in",
    )(w_in)
    bpad = jnp.zeros((LANES - 2 * M_HEADS,), F32)
    bi = jnp.concatenate([b_mgate[0:4], b_mgate[8:12], bpad])
    bf = jnp.concatenate([b_mgate[4:8], b_mgate[12:16], bpad])
    bg = jnp.concatenate([bf[None, :], jnp.zeros((7, LANES), F32)], axis=0)
    bgt = jnp.concatenate([bi, bf])[:, None]
    return w_all, wt_all, bg, bgt


def _segment_mats():
    na_w = NA_HEADS * NA_DH
    seg = np.zeros((na_w, LANES), np.float32)
    seg[np.arange(na_w), np.arange(na_w) // NA_DH] = 1.0
    return jnp.asarray(seg, BF16), jnp.asarray(seg.T.copy(), BF16)


def kernel(x, c, ctx, c_ctx, w_ada, b_ada, w_in, b_mgate, m_norm_w, na_qn_w, na_kn_w, na_rpb,
           w_br_m, w_br_na, w_out, w_router, router_bias, w_exp_gate, w_exp_up, w_exp_down,
           w_sh_gate, w_sh_up, w_sh_down):
    b, t, d = x.shape
    n = b * t
    rows = t // GRID_W
    l = 0

    cc = jnp.concatenate([c, c_ctx[None, :], jnp.zeros((8 - b - 1, d), F32)], axis=0)
    mod = _ada(cc, w_ada[l], b_ada[l])
    mod = mod.reshape(8, 6, d)
    mod = jnp.concatenate([mod, jnp.zeros((8, 2, d), F32)], axis=1)
    mod_x = mod[:b]
    mod_c = jnp.broadcast_to(mod[b:b + 1], (b, 8, d))

    w_all, wt_all, bg, bgt = _arrange_w_in(w_in[l], b_mgate[l])
    seg, segt = _segment_mats()
    qnw = jnp.tile(na_qn_w[l], NA_HEADS)[None, :]
    knw = jnp.tile(na_kn_w[l], NA_HEADS)[None, :]
    tm = min(512, t)

    cp = _inproj(ctx, mod_c, w_all, wt_all, bg, bgt, qnw, knw, seg, segt, None,
                 min(tm, ctx.shape[1]))
    xp = _inproj(x, mod_x, w_all, wt_all, bg, bgt, qnw, knw, seg, segt, _rope_tables(t, tm), tm)
    cmq, cmv, _, cgf, _, cnk, cnv, _, _, cmkt, cgit, cgft = cp
    mq, mv, mo, gf, nq, nk, nv, gm, gn, mkt, git, gft = xp

    c0 = jnp.zeros((b, 8, M_DQK, MLSTM_EXT), F32)
    m0 = jnp.zeros((b, 8, LANES), F32)
    _, _, c1, m1 = _mlstm(cmq, cmkt, cmv, cgf, cgit, cgft, c0, m0)
    hf, hb, _, _ = _mlstm(mq, mkt, mv, gf, git, gft, c1, m1)

    yna = _na(nq, nk, nv, cnk, cnv, _na_bias_table(na_rpb[l], rows))

    bias_col = jnp.broadcast_to(router_bias[l][:, None], (N_EXPERTS, LANES))
    h2p, base, top_e, top_w, rank, cnt = _post(
        x, mod_x, hf, hb, mo, yna, gm, gn, m_norm_w[l][None, :],
        w_br_m[l].astype(BF16), w_br_na[l].astype(BF16), w_out[l].astype(BF16),
        w_router[l].T.astype(BF16), w_sh_gate[l].astype(BF16), w_sh_up[l].astype(BF16),
        w_sh_down[l].astype(BF16), bias_col, tm)

    counts = cnt[:, 0].astype(jnp.int32)
    half = MOE_BLOCK // 2
    p_rows = (-(-(n * TOP_K) // half) + N_EXPERTS) * half
    nb_max = -(-(n * TOP_K) // MOE_BLOCK) + N_EXPERTS
    pstart, blk0, nblk, row0, is_half, n_used = _expert_plan(counts, nb_max)

    pstart_col = jnp.broadcast_to(pstart.astype(F32)[:, None], (N_EXPERTS, LANES))
    pos = _slots(top_e, rank, pstart_col, min(512, n))
    xs = _dispatch_rows(h2p, pos, p_rows)
    ys = _experts(blk0, nblk, counts, row0, is_half, n_used, xs,
                  w_exp_gate[l], w_exp_up[l], w_exp_down[l])
    out = base
    for bi in range(b):
        idx = pos[:, bi * t:(bi + 1) * t].reshape(-1)
        yg = _gather_rows(ys, idx).reshape(TOP_K, t, d // 2)
        out = _combine(out, bi, mod_x, top_w, yg, tm)
    return out
```

```python
import functools

import numpy as np
import jax
import jax.numpy as jnp
from jax import lax
from jax.experimental import pallas as pl
from jax.experimental.pallas import tpu as pltpu
from jax.experimental.pallas import tpu_sc as plsc

F32 = jnp.float32
BF16 = jnp.bfloat16

EPS = 1e-6
GRID_W = 64
M_HEADS, M_DQK, M_DV = 4, 128, 256
ROPE_BASE = 10000.0
NA_HEADS, NA_DH, NA_KH, NA_KW = 8, 64, 8, 16
N_EXPERTS, TOP_K, N_GROUPS, TOPK_GROUPS = 256, 8, 8, 4
ROUTE_SCALE = 2.5

LANES = 128
VMEM_LIMIT = 56 * 1024 * 1024
NEG = -1e30

MLSTM_CHUNK = 256
NA_ROWS = 4
NA_KEY_ROWS = NA_ROWS + NA_KH - 1
MOE_BLOCK = 512
EXPERT_SLOTS = 4

_W_SEGS = (("mq", 512), ("mv", 1024), ("mo", 1024), ("gf", 128),
           ("nq", 512), ("nk", 512), ("nv", 512), ("gm", 1024), ("gn", 1024))
_W_OFF = {}
_o = 0
for _n, _w in _W_SEGS:
    _W_OFF[_n] = (_o, _w)
    _o += _w
W_COLS = _o


def _dot(a, b):
    return jnp.dot(a, b, preferred_element_type=F32)


def _dot_nt(a, b):
    return lax.dot_general(a, b, (((1,), (1,)), ((), ())), preferred_element_type=F32)


def _sigmoid(x):
    return 1.0 / (1.0 + jnp.exp(-x))


def _pack_bf16_pairs(v):
    w = v.shape[1] // 2
    bits = pltpu.bitcast(v.astype(BF16).astype(F32), jnp.int32)
    return lax.shift_right_logical(bits[:, :w], 16) | bits[:, w:]


def _unpack_bf16_pairs(p):
    lo = pltpu.bitcast(lax.shift_left(p, 16), F32)
    hi = pltpu.bitcast(p & jnp.int32(-65536), F32)
    return jnp.concatenate([lo, hi], axis=1)


def _params(*sem):
    return pltpu.CompilerParams(dimension_semantics=sem, vmem_limit_bytes=VMEM_LIMIT)


def _resident(shape):
    nd = len(shape)
    return pl.BlockSpec(shape, lambda *_: (0,) * nd, pipeline_mode=pl.Buffered(1))


def _ada_kernel(c_ref, w_ref, b_ref, o_ref):
    c = c_ref[...]
    s = c * _sigmoid(c)
    o_ref[...] = _dot(s.astype(BF16), w_ref[...].astype(BF16)) + b_ref[...]


def _ada(cc, w_ada, b_ada):
    d = cc.shape[1]
    n = w_ada.shape[1]
    return pl.pallas_call(
        _ada_kernel,
        grid=(n // d,),
        in_specs=[pl.BlockSpec((8, d), lambda j: (0, 0)),
                  pl.BlockSpec((d, d), lambda j: (0, j)),
                  pl.BlockSpec((1, d), lambda j: (0, j))],
        out_specs=pl.BlockSpec((8, d), lambda j: (0, j)),
        out_shape=jax.ShapeDtypeStruct((8, n), F32),
        compiler_params=_params("arbitrary"),
        name="ada",
    )(cc, w_ada, b_ada.reshape(1, n))


def _rope_rotate(t, cos, sin):
    lane = lax.broadcasted_iota(jnp.int32, t.shape, 1)
    partner = jnp.where((lane & 32) == 0, pltpu.roll(t, 96, 1), pltpu.roll(t, 32, 1))
    return t * cos + partner * sin


def _rope_rotate_t(t, cos, sin):
    q = M_DQK // 4
    partner = jnp.concatenate([t[q:2 * q], t[0:q], t[3 * q:4 * q], t[2 * q:3 * q]], axis=0)
    return t * cos + partner * sin


def _inproj_kernel(*refs, rope):
    if rope:
        (x_ref, mod_ref, w_ref, wt_ref, bg_ref, bgt_ref, qnw_ref, knw_ref, seg_ref, segt_ref,
         rcos_ref, rsin_ref, ccos_ref, csin_ref,
         mq_ref, mv_ref, mo_ref, gf_ref, nq_ref, nk_ref, nv_ref, gm_ref, gn_ref,
         mkt_ref, git_ref, gft_ref) = refs
    else:
        (x_ref, mod_ref, w_ref, wt_ref, bg_ref, bgt_ref, qnw_ref, knw_ref, seg_ref, segt_ref,
         mq_ref, mv_ref, mo_ref, gf_ref, nq_ref, nk_ref, nv_ref, gm_ref, gn_ref,
         mkt_ref, git_ref, gft_ref) = refs
    x = x_ref[0]
    xn = x * lax.rsqrt(jnp.mean(x * x, axis=-1, keepdims=True) + EPS)
    h = xn * (1.0 + mod_ref[0, 1:2, :]) + mod_ref[0, 0:1, :]
    hb = h.astype(BF16)

    def proj(name):
        off, width = _W_OFF[name]
        return _dot(hb, w_ref[:, off:off + width])

    def head_rms(t, w_row, scale):
        ss = _dot((t * t).astype(BF16), seg_ref[...])
        r = lax.rsqrt(ss * (1.0 / NA_DH) + EPS)
        r_hi = r.astype(BF16)
        r_lo = (r - r_hi.astype(F32)).astype(BF16)
        rb = _dot(r_hi, segt_ref[...]) + _dot(r_lo, segt_ref[...])
        return t * rb * w_row * scale

    mq = proj("mq") * (M_DQK ** -0.5)
    if rope:
        tm = x.shape[0]
        spread = lambda r: jnp.broadcast_to(r[:, None, :], (tm // GRID_W, GRID_W, LANES)).reshape(tm, LANES)
        cos = spread(rcos_ref[...]) + ccos_ref[...]
        sin = spread(rsin_ref[...]) + csin_ref[...]
        mq = jnp.concatenate([_rope_rotate(mq[:, i * LANES:(i + 1) * LANES], cos, sin)
                              for i in range(M_HEADS)], axis=1)
    mq_ref[0] = mq.astype(BF16)
    mv_ref[0] = proj("mv").astype(BF16)
    mo_ref[0] = _sigmoid(proj("mo")).astype(BF16)
    gf_ref[0] = proj("gf") + bg_ref[0:1, :]
    nq_ref[0] = head_rms(proj("nq"), qnw_ref[...], NA_DH ** -0.5).astype(BF16)
    nk_ref[0] = head_rms(proj("nk"), knw_ref[...], 1.0).astype(BF16)
    nv_ref[0] = proj("nv").astype(BF16)
    gm_ref[0] = _sigmoid(proj("gm")).astype(BF16)
    gn_ref[0] = _sigmoid(proj("gn")).astype(BF16)

    qk_w = M_HEADS * M_DQK
    mkt = _dot_nt(wt_ref[0:qk_w, :], hb)
    if rope:
        cost, sint = jnp.transpose(cos), jnp.transpose(sin)
        mkt = jnp.concatenate([_rope_rotate_t(mkt[i * M_DQK:(i + 1) * M_DQK], cost, sint)
                               for i in range(M_HEADS)], axis=0)
    mkt_ref[0] = mkt.astype(BF16)
    git_ref[0] = _dot_nt(wt_ref[qk_w:qk_w + LANES, :], hb) + bgt_ref[0:LANES, :]
    gft_ref[0] = _dot_nt(wt_ref[qk_w + LANES:qk_w + 2 * LANES, :], hb) + bgt_ref[LANES:2 * LANES, :]


def _inproj(x, mod, w_all, wt_all, bg, bgt, qnw, knw, seg, segt, rope_tabs, tm):
    b, t, d = x.shape
    rope = rope_tabs is not None
    tok = lambda w: pl.BlockSpec((1, tm, w), lambda bi, i: (bi, i, 0))
    tok_t = lambda w: pl.BlockSpec((1, w, tm), lambda bi, i: (bi, 0, i))
    in_specs = [tok(d),
                pl.BlockSpec((1, 8, d), lambda bi, i: (bi, 0, 0)),
                _resident(w_all.shape), _resident(wt_all.shape), _resident(bg.shape),
                _resident(bgt.shape), _resident(qnw.shape),
                _resident(knw.shape), _resident(seg.shape), _resident(segt.shape)]
    args = [x, mod, w_all, wt_all, bg, bgt, qnw, knw, seg, segt]
    if rope:
        in_specs += [pl.BlockSpec((tm // GRID_W, LANES), lambda bi, i: (i, 0))] * 2
        in_specs += [_resident((tm, LANES))] * 2
        args += list(rope_tabs)
    widths = [("mq", BF16), ("mv", BF16), ("mo", BF16), ("gf", F32),
              ("nq", BF16), ("nk", BF16), ("nv", BF16), ("gm", BF16), ("gn", BF16)]
    out_specs = [tok(_W_OFF[n][1]) for n, _ in widths]
    out_shape = [jax.ShapeDtypeStruct((b, t, _W_OFF[n][1]), dt) for n, dt in widths]
    out_specs += [tok_t(M_HEADS * M_DQK), tok_t(LANES), tok_t(LANES)]
    out_shape += [jax.ShapeDtypeStruct((b, M_HEADS * M_DQK, t), BF16),
                  jax.ShapeDtypeStruct((b, LANES, t), F32),
                  jax.ShapeDtypeStruct((b, LANES, t), F32)]
    return pl.pallas_call(
        functools.partial(_inproj_kernel, rope=rope),
        grid=(b, t // tm),
        in_specs=in_specs, out_specs=out_specs, out_shape=out_shape,
        compiler_params=_params("parallel", "parallel"),
        name="inproj_rope" if rope else "inproj_ctx",
    )(*args)


def _log_sigmoid(x):
    return jnp.minimum(x, 0.0) - jnp.log(1.0 + jnp.exp(-jnp.abs(x)))


def _dot_split(a, b, split_a):
    x = a if split_a else b
    hi = x.astype(BF16)
    lo = (x - hi.astype(F32)).astype(BF16)
    return (_dot(hi, b) + _dot(lo, b)) if split_a else (_dot(a, hi) + _dot(a, lo))


MLSTM_EXT = M_DV + LANES


def _mlstm_kernel(qf_ref, ktf_ref, vf_ref, gff_ref, gitf_ref, gftf_ref,
                  qb_ref, ktb_ref, vb_ref, gfb_ref, gitb_ref, gftb_ref,
                  c0_ref, m0_ref,
                  hf_ref, hb_ref, cn_ref, mn_ref,
                  *scratch):
    c_scrs, m_scr = scratch[:2 * M_HEADS], scratch[2 * M_HEADS]
    step = pl.program_id(1)
    L = qf_ref.shape[1]
    nu = 2 * M_HEADS

    @pl.when(step == 0)
    def _():
        for j, c_scr in enumerate(c_scrs):
            c_scr[...] = c0_ref[0, j]
        m_scr[...] = m0_ref[0]

    row_i = lax.broadcasted_iota(jnp.int32, (L, L), 0)
    col_i = lax.broadcasted_iota(jnp.int32, (L, L), 1)
    lower = col_i <= row_i
    upper = col_i >= row_i
    tri_lo = jnp.where(lower, 1.0, 0.0).astype(BF16)
    tri_up = jnp.where(upper, 1.0, 0.0).astype(BF16)

    is_f = lax.broadcasted_iota(jnp.int32, (nu, L), 0) < M_HEADS
    gi_t = jnp.where(is_f, gitf_ref[0, 0:nu, :], gitb_ref[0, 0:nu, :])
    ls_tf = _log_sigmoid(gftf_ref[0, 0:nu, :])
    ls_tb = _log_sigmoid(gftb_ref[0, 0:nu, :])
    b_t = jnp.where(is_f, _dot_split(ls_tf, tri_up, True), _dot_split(ls_tb, tri_lo, True))
    u_t = gi_t - b_t
    g_c = jnp.sum(jnp.where(is_f, ls_tf, ls_tb), axis=1, keepdims=True)
    m_prev = m_scr[...]
    a_t = g_c + u_t
    m_new = jnp.maximum(g_c + m_prev, jnp.max(a_t, axis=1, keepdims=True))
    decay = jnp.exp(g_c + m_prev - m_new)
    wa_t = jnp.exp(a_t - jnp.concatenate([m_new] * (L // LANES), axis=1))
    m_scr[...] = m_new

    ones = jnp.ones((L, LANES), BF16)
    dirs = ((qf_ref, ktf_ref, vf_ref, gff_ref, hf_ref, lower, tri_lo),
            (qb_ref, ktb_ref, vb_ref, gfb_ref, hb_ref, upper, tri_up))
    for d, (q_ref, kt_ref, v_ref, gf_ref, h_ref, mask, tri) in enumerate(dirs):
        bcum = _dot_split(tri, _log_sigmoid(gf_ref[0]), False)
        for hd in range(M_HEADS):
            j = d * M_HEADS + hd
            c_scr = c_scrs[j]
            q = q_ref[0, :, hd * M_DQK:(hd + 1) * M_DQK]
            k_t = kt_ref[0, hd * M_DQK:(hd + 1) * M_DQK, :]
            v_ext = jnp.concatenate([v_ref[0, :, hd * M_DV:(hd + 1) * M_DV], ones], axis=1)
            u_row = u_t[j:j + 1, :]
            mp_row = m_prev[j:j + 1, :]
            c_prev = c_scr[...]

            m_loc = jnp.max(jnp.where(mask, u_row, NEG), axis=1, keepdims=True)
            m_rep = jnp.maximum(jnp.broadcast_to(m_loc, (L, LANES)), mp_row)
            m_wide = jnp.concatenate([m_rep] * (L // LANES), axis=1)
            dmat = jnp.exp(jnp.where(mask, u_row - m_wide, NEG))
            s = (_dot(q, k_t) * dmat).astype(BF16)
            qw = (q.astype(F32) * jnp.exp(mp_row - m_rep)).astype(BF16)
            r = _dot(s, v_ext) + _dot(qw, c_prev.astype(BF16))
            b_rep = jnp.broadcast_to(bcum[:, j:j + 1], (L, LANES))
            dn = jnp.maximum(jnp.abs(r[:, M_DV:]), jnp.exp(-(b_rep + m_rep)))
            h_ref[0, :, hd * M_DV:(hd + 1) * M_DV] = (
                r[:, :M_DV] / jnp.concatenate([dn] * (M_DV // LANES), axis=1)).astype(h_ref.dtype)

            kw = (k_t.astype(F32) * wa_t[j:j + 1, :]).astype(BF16)
            dec = jnp.concatenate([decay[j:j + 1, :]] * (MLSTM_EXT // LANES), axis=1)
            c_scr[...] = dec * c_prev + _dot(kw, v_ext)

    @pl.when(step == pl.num_programs(1) - 1)
    def _():
        for j, c_scr in enumerate(c_scrs):
            cn_ref[0, j] = c_scr[...]
        mn_ref[0] = m_scr[...]


def _mlstm(q, kt, v, gf, git, gft, c0, m0):
    b, t, _ = q.shape
    L = min(MLSTM_CHUNK, t)
    nc = t // L
    fwd = lambda w: pl.BlockSpec((1, L, w), lambda bi, i: (bi, i, 0))
    bwd = lambda w: pl.BlockSpec((1, L, w), lambda bi, i: (bi, nc - 1 - i, 0))
    fwd_t = lambda w: pl.BlockSpec((1, w, L), lambda bi, i: (bi, 0, i))
    bwd_t = lambda w: pl.BlockSpec((1, w, L), lambda bi, i: (bi, 0, nc - 1 - i))
    st_c = pl.BlockSpec((1, 8, M_DQK, MLSTM_EXT), lambda bi, i: (bi, 0, 0, 0))
    st_v = pl.BlockSpec((1, 8, LANES), lambda bi, i: (bi, 0, 0))
    qk_w, v_w = M_HEADS * M_DQK, M_HEADS * M_DV
    return pl.pallas_call(
        _mlstm_kernel,
        grid=(b, nc),
        in_specs=[fwd(qk_w), fwd_t(qk_w), fwd(v_w), fwd(LANES), fwd_t(LANES), fwd_t(LANES),
                  bwd(qk_w), bwd_t(qk_w), bwd(v_w), bwd(LANES), bwd_t(LANES), bwd_t(LANES),
                  st_c, st_v],
        out_specs=[fwd(v_w), bwd(v_w), st_c, st_v],
        out_shape=[jax.ShapeDtypeStruct((b, t, v_w), BF16),
                   jax.ShapeDtypeStruct((b, t, v_w), BF16),
                   jax.ShapeDtypeStruct(c0.shape, F32),
                   jax.ShapeDtypeStruct(m0.shape, F32)],
        scratch_shapes=([pltpu.VMEM((M_DQK, MLSTM_EXT), F32) for _ in range(2 * M_HEADS)]
                        + [pltpu.VMEM((8, LANES), F32)]),
        compiler_params=_params("parallel", "arbitrary"),
        name="mlstm",
    )(q, kt, v, gf, git, gft, q, kt, v, gf, git, gft, c0, m0)


def _na_kernel(q_ref, k_ref, v_ref, kc_ref, vc_ref, bias_ref, o_ref, *, rows):
    r0 = pl.program_id(2) * NA_ROWS
    ks = jnp.clip(r0 - NA_KH // 2, 0, rows - NA_KEY_ROWS)
    kstart = pl.multiple_of(ks * GRID_W, GRID_W)
    nkeys = NA_KEY_ROWS * GRID_W
    kblk = k_ref[0, pl.ds(kstart, nkeys), :]
    vblk = v_ref[0, pl.ds(kstart, nkeys), :]
    kc = kc_ref[0]
    vc = vc_ref[0]
    q = q_ref[0]
    lane = lax.broadcasted_iota(jnp.int32, q.shape, 1)
    outs = []
    for hh in range(2):
        in_head = (lane < NA_DH) if hh == 0 else (lane >= NA_DH)
        qm = jnp.where(in_head, q, jnp.zeros_like(q))
        sw = _dot_nt(qm, kblk) + bias_ref[hh, 0]
        sc = _dot_nt(qm, kc)
        m = jnp.maximum(jnp.max(sw, axis=1, keepdims=True), jnp.max(sc, axis=1, keepdims=True))
        ew = jnp.exp(sw - m)
        ec = jnp.exp(sc - m)
        l = jnp.sum(ew, axis=1, keepdims=True) + jnp.sum(ec, axis=1, keepdims=True)
        o = _dot(ew.astype(BF16), vblk) + _dot(ec.astype(BF16), vc)
        outs.append(o / l)
    o_ref[0] = jnp.where(lane < NA_DH, outs[0], outs[1]).astype(o_ref.dtype)


def _na(nq, nk, nv, cnk, cnv, bias):
    b, t, w = nq.shape
    rows = t // GRID_W
    tq = NA_ROWS * GRID_W
    nrb = rows // NA_ROWS
    nctx = cnk.shape[1]
    kind = lambda rb: jnp.where(rb == 0, 0, jnp.where(rb == nrb - 1, 2, 1))
    return pl.pallas_call(
        functools.partial(_na_kernel, rows=rows),
        grid=(b, w // LANES, nrb),
        in_specs=[pl.BlockSpec((1, tq, LANES), lambda bi, hp, rb: (bi, rb, hp)),
                  pl.BlockSpec((1, t, LANES), lambda bi, hp, rb: (bi, 0, hp)),
                  pl.BlockSpec((1, t, LANES), lambda bi, hp, rb: (bi, 0, hp)),
                  pl.BlockSpec((1, nctx, LANES), lambda bi, hp, rb: (bi, 0, hp)),
                  pl.BlockSpec((1, nctx, LANES), lambda bi, hp, rb: (bi, 0, hp)),
                  pl.BlockSpec((2, 1, tq, NA_KEY_ROWS * GRID_W),
                               lambda bi, hp, rb: (hp, kind(rb), 0, 0))],
        out_specs=pl.BlockSpec((1, tq, LANES), lambda bi, hp, rb: (bi, rb, hp)),
        out_shape=jax.ShapeDtypeStruct((b, t, w), BF16),
        compiler_params=_params("parallel", "parallel", "arbitrary"),
        name="na",
    )(nq, nk, nv, cnk, cnv, bias)


def _na_bias_table(na_rpb, rows):
    h = na_rpb.shape[0]
    w = GRID_W
    c = np.arange(w)[:, None]
    kj = np.arange(w)[None, :]
    cs = np.clip(c - NA_KW // 2, 0, w - NA_KW)
    col_valid = (kj >= cs) & (kj < cs + NA_KW)
    dc = np.clip(kj - c + (NA_KW - 1), 0, 2 * NA_KW - 2)
    onehot = np.zeros((2 * NA_KW - 1, w, w), np.float32)
    onehot[dc, np.arange(w)[:, None], np.arange(w)[None, :]] = 1.0
    t2 = jnp.einsum("hrd,dck->hrck", na_rpb, jnp.asarray(onehot), precision=lax.Precision.HIGHEST)
    t2 = jnp.where(jnp.asarray(col_valid)[None, None], t2, NEG)
    t2 = jnp.concatenate([t2, jnp.full((h, 1, w, w), NEG, F32)], axis=1)
    invalid = 2 * NA_KH - 1
    dr_idx = np.full((3, NA_ROWS, NA_KEY_ROWS), invalid, np.int32)
    for kind, r0 in enumerate((0, NA_ROWS, rows - NA_ROWS)):
        ks = int(np.clip(r0 - NA_KH // 2, 0, rows - NA_KEY_ROWS))
        for qa in range(NA_ROWS):
            r = r0 + qa
            rs = int(np.clip(r - NA_KH // 2, 0, rows - NA_KH))
            for kl in range(NA_KEY_ROWS):
                ki = ks + kl
                if rs <= ki < rs + NA_KH:
                    dr_idx[kind, qa, kl] = ki - r + NA_KH - 1
    t2t = t2.transpose(0, 2, 1, 3)
    strips = [jnp.concatenate([t2t[:, :, int(dr), :] for dr in dr_idx[kind, qa]], axis=-1)
              for kind in range(3) for qa in range(NA_ROWS)]
    return jnp.stack(strips, axis=1).reshape(h, 3, NA_ROWS * w, NA_KEY_ROWS * w)


def _post_kernel(x_ref, mod_ref, hf_ref, hb_ref, mo_ref, na_ref, gm_ref, gn_ref,
                 mnw_ref, wbm_ref, wbn_ref, wout_ref, wr_ref, wsg_ref, wsu_ref, wsd_ref, rb_ref,
                 h2_ref, base_ref, e_ref, w_ref, r_ref, cnt_ref, run_scr):
    @pl.when(pl.program_id(0) == 0)
    def _():
        run_scr[...] = jnp.zeros_like(run_scr)

    hm = hf_ref[0].astype(F32) + hb_ref[0].astype(F32)
    parts = []
    for hd in range(M_HEADS):
        t = hm[:, hd * M_DV:(hd + 1) * M_DV]
        parts.append(t * lax.rsqrt(jnp.mean(t * t, axis=-1, keepdims=True) + EPS))
    y_m = jnp.concatenate(parts, axis=1) * mnw_ref[...] * mo_ref[0].astype(F32)
    a = _dot(y_m.astype(BF16), wbm_ref[...])
    bn = _dot(na_ref[0], wbn_ref[...])
    z = gm_ref[0].astype(F32) * a + gn_ref[0].astype(F32) * bn
    y = _dot(z.astype(BF16), wout_ref[...])
    x1 = x_ref[0] + mod_ref[0, 2:3, :] * y
    xn = x1 * lax.rsqrt(jnp.mean(x1 * x1, axis=-1, keepdims=True) + EPS)
    h2f = xn * (1.0 + mod_ref[0, 4:5, :]) + mod_ref[0, 3:4, :]
    h2_ref[...] = _pack_bf16_pairs(h2f)
    h2 = h2f.astype(BF16)
    sh = _dot(h2, wsg_ref[...])
    sh = sh * _sigmoid(sh) * _dot(h2, wsu_ref[...])
    base_ref[0] = x1 + mod_ref[0, 5:6, :] * _dot(sh.astype(BF16), wsd_ref[...])
    scores = _sigmoid(_dot_nt(wr_ref[...], h2))
    _route_block(scores, rb_ref, e_ref, w_ref, r_ref, cnt_ref, run_scr)


def _post(x, mod, hf, hb, mo, yna, gm, gn, mnw, wbm, wbn, wout, wr_t, wsg, wsu, wsd, rbias, tm):
    b, t, d = x.shape
    nt = t // tm
    n = b * t
    tok = lambda w: pl.BlockSpec((1, tm, w), lambda s: (s // nt, s % nt, 0))
    rt = lambda: pl.BlockSpec((TOP_K, tm), lambda s: (0, s))
    res = [mnw, wbm, wbn, wout, wr_t, wsg, wsu, wsd, rbias]
    return pl.pallas_call(
        _post_kernel,
        grid=(b * nt,),
        in_specs=[tok(d), pl.BlockSpec((1, 8, d), lambda s: (s // nt, 0, 0)),
                  tok(hf.shape[2]), tok(hb.shape[2]), tok(mo.shape[2]), tok(yna.shape[2]),
                  tok(gm.shape[2]), tok(gn.shape[2])] + [_resident(a.shape) for a in res],
        out_specs=[pl.BlockSpec((tm, d // 2), lambda s: (s, 0)),
                   tok(d), rt(), pl.BlockSpec((tm, TOP_K), lambda s: (s, 0)), rt(),
                   pl.BlockSpec((N_EXPERTS, LANES), lambda s: (0, 0))],
        out_shape=[jax.ShapeDtypeStruct((n, d // 2), jnp.int32),
                   jax.ShapeDtypeStruct((b, t, d), F32),
                   jax.ShapeDtypeStruct((TOP_K, n), jnp.int32),
                   jax.ShapeDtypeStruct((n, TOP_K), F32),
                   jax.ShapeDtypeStruct((TOP_K, n), jnp.int32),
                   jax.ShapeDtypeStruct((N_EXPERTS, LANES), F32)],
        scratch_shapes=[pltpu.VMEM((N_EXPERTS, LANES), F32)],
        compiler_params=_params("arbitrary"),
        name="post",
    )(x, mod, hf, hb, mo, yna, gm, gn, *res)


def _route_block(s, b_ref, e_ref, w_ref, r_ref, cnt_ref, run_scr):
    tm = s.shape[1]
    sel = s + b_ref[...][:, 0:1]
    gsz = N_EXPERTS // N_GROUPS
    ninf = -jnp.inf

    x3 = sel.reshape(N_GROUPS, gsz, tm)
    r3 = lax.broadcasted_iota(jnp.int32, x3.shape, 1)
    m1 = jnp.max(x3, axis=1, keepdims=True)
    i1 = jnp.min(jnp.where(x3 == m1, r3, gsz), axis=1, keepdims=True)
    m2 = jnp.max(jnp.where(r3 == i1, ninf, x3), axis=1)
    gs = m1[:, 0, :] + m2

    gidx = lax.broadcasted_iota(jnp.int32, gs.shape, 0)
    gkeep = jnp.zeros(gs.shape, jnp.bool_)
    cur = gs
    for _ in range(TOPK_GROUPS):
        mm = jnp.max(cur, axis=0, keepdims=True)
        ii = jnp.min(jnp.where(cur == mm, gidx, N_GROUPS), axis=0, keepdims=True)
        hit = gidx == ii
        gkeep = jnp.logical_or(gkeep, hit)
        cur = jnp.where(hit, ninf, cur)
    keep = jnp.broadcast_to(gkeep[:, None, :], x3.shape).reshape(N_EXPERTS, tm)

    row = lax.broadcasted_iota(jnp.int32, s.shape, 0).astype(F32)
    cur = jnp.where(keep, sel, ninf)
    idxs, ws = [], []
    chosen_f = jnp.zeros(s.shape, F32)
    for _ in range(TOP_K):
        mm = jnp.max(cur, axis=0, keepdims=True)
        ii = jnp.min(jnp.where(cur == mm, row, float(N_EXPERTS)), axis=0, keepdims=True)
        hit = row == ii
        idxs.append(ii)
        ws.append(jnp.sum(jnp.where(hit, s, 0.0), axis=0, keepdims=True))
        chosen_f = jnp.where(hit, 1.0, chosen_f)
        cur = jnp.where(hit, ninf, cur)
    wsum = ws[0]
    for wk in ws[1:]:
        wsum = wsum + wk

    tp =lax.broadcasted_iota(jnp.int32, (tm, tm), 0)
    tc = lax.broadcasted_iota(jnp.int32, (tm, tm), 1)
    before = jnp.where(tp < tc, 1.0, 0.0).astype(BF16)
    rank = _dot(chosen_f.astype(BF16), before) + run_scr[...][:, 0:1]
    run_scr[...] = run_scr[...] + jnp.sum(chosen_f, axis=1, keepdims=True)
    cnt_ref[...] = run_scr[...]

    for kk in range(TOP_K):
        e_ref[kk:kk + 1, :] = idxs[kk].astype(jnp.int32)
        r_ref[kk:kk + 1, :] = jnp.sum(jnp.where(row == idxs[kk], rank, 0.0), axis=0,
                                      keepdims=True).astype(jnp.int32)
    w_ref[...] = jnp.transpose(jnp.concatenate([wk / wsum * ROUTE_SCALE for wk in ws], axis=0))


def _slot_kernel(e_ref, r_ref, ps_ref, o_ref):
    row = lax.broadcasted_iota(jnp.int32, (N_EXPERTS, e_ref.shape[1]), 0)
    ps = ps_ref[...][:, 0:1]
    for kk in range(TOP_K):
        first = jnp.sum(jnp.where(row == e_ref[kk:kk + 1, :], ps, 0.0), axis=0, keepdims=True)
        o_ref[kk:kk + 1, :] = first.astype(jnp.int32) + r_ref[kk:kk + 1, :]


def _slots(top_e, rank, pstart, tm):
    k, n = top_e.shape
    blk = pl.BlockSpec((k, tm), lambda i: (0, i))
    return pl.pallas_call(
        _slot_kernel,
        grid=(n // tm,),
        in_specs=[blk, blk, _resident(pstart.shape)],
        out_specs=blk,
        out_shape=jax.ShapeDtypeStruct((k, n), jnp.int32),
        compiler_params=_params("parallel"),
        name="slots",
    )(top_e, rank, pstart)


SC_WINDOW = 128


def _sc_mesh():
    return plsc.VectorSubcoreMesh(core_axis_name="core", subcore_axis_name="subcore")


def _sc_workers():
    info = plsc.get_sparse_core_info()
    return info.num_cores, info.num_cores * info.num_subcores


def _dispatch_rows(x, pos, p_rows):
    n, w = x.shape
    kk = pos.shape[0]
    ncores, nw = _sc_workers()
    steps = n // nw // SC_WINDOW
    pos4 = pos.reshape(kk, nw, steps, SC_WINDOW).transpose(1, 2, 0, 3)

    @functools.partial(
        pl.kernel, mesh=_sc_mesh(),
        out_type=jax.ShapeDtypeStruct((p_rows, w), x.dtype),
        scratch_types=[pltpu.VMEM((kk, SC_WINDOW), jnp.int32),
                       pltpu.VMEM((SC_WINDOW, w), x.dtype),
                       pltpu.SemaphoreType.DMA],
    )
    def scatter(x_hbm, i_hbm, o_hbm, idx_v, rows_v, sem):
        wid = lax.axis_index("subcore") * ncores + lax.axis_index("core")

        @pl.loop(0, steps)
        def _(s):
            base = pl.multiple_of((wid * steps + s) * SC_WINDOW, SC_WINDOW)
            pltpu.sync_copy(i_hbm.at[wid, s], idx_v)
            pltpu.sync_copy(x_hbm.at[pl.ds(base, SC_WINDOW)], rows_v)
            copies = [pltpu.make_async_copy(rows_v, o_hbm.at[idx_v.at[j]], sem) for j in range(kk)]
            for cp in copies:
                cp.start()
            for cp in copies:
                cp.wait()

    return scatter(x, pos4)


def _gather_rows(x, idx):
    m = idx.shape[0]
    w = x.shape[1]
    ncores, nw = _sc_workers()
    steps = m // nw // SC_WINDOW
    idx3 = idx.reshape(nw, steps, SC_WINDOW)

    @functools.partial(
        pl.kernel, mesh=_sc_mesh(),
        out_type=jax.ShapeDtypeStruct((m, w), x.dtype),
        scratch_types=[pltpu.VMEM((steps, SC_WINDOW), jnp.int32),
                       pltpu.VMEM((SC_WINDOW, w), x.dtype),
                       pltpu.SemaphoreType.DMA],
    )
    def gather(x_hbm, i_hbm, o_hbm, idx_v, rows_v, sem):
        wid = lax.axis_index("subcore") * ncores + lax.axis_index("core")
        pltpu.sync_copy(i_hbm.at[wid], idx_v)

        @pl.loop(0, steps)
        def _(s):
            pltpu.async_copy(x_hbm.at[idx_v.at[s]], rows_v, sem).wait()
            base = pl.multiple_of((wid * steps + s) * SC_WINDOW, SC_WINDOW)
            pltpu.sync_copy(rows_v, o_hbm.at[pl.ds(base, SC_WINDOW)])

    return gather(x, idx3)


def _experts_kernel(blk0_ref, nblk_ref, cnt_ref, row0_ref, half_ref, nu_ref,
                    x_hbm, wg_ref, wu_ref, wd_ref, y_hbm,
                    xbuf, ybuf, wg_scr, wu_scr, wd_scr, in_sem, out_sem):
    e = pl.program_id(0)
    n_used = nu_ref[0]
    blk0 = blk0_ref[e]
    ns = EXPERT_SLOTS
    sizes = (MOE_BLOCK, MOE_BLOCK // 2)

    def x_copy(g, rows):
        r0 = pl.multiple_of(row0_ref[g], MOE_BLOCK // 2)
        return pltpu.make_async_copy(x_hbm.at[pl.ds(r0, rows)], xbuf.at[g % ns, pl.ds(0, rows)],
                                     in_sem.at[g % ns])

    def y_copy(g, rows):
        r0 = pl.multiple_of(row0_ref[g], MOE_BLOCK // 2)
        return pltpu.make_async_copy(ybuf.at[g % ns, pl.ds(0, rows)], y_hbm.at[pl.ds(r0, rows)],
                                     out_sem.at[g % ns])

    def by_size(g, fn):
        for is_half, rows in enumerate(sizes):
            @pl.when(half_ref[g] == is_half)
            def _():
                fn(rows)

    @pl.when(e == 0)
    def _():
        for g0 in range(ns - 1):
            @pl.when(g0 < n_used)
            def _():
                by_size(g0, lambda rows: x_copy(g0, rows).start())

    wg_scr[...] = wg_ref[0].astype(BF16)
    wu_scr[...] = wu_ref[0].astype(BF16)
    wd_scr[...] = wd_ref[0].astype(BF16)

    def block(b, carry):
        g = blk0 + b
        by_size(g, lambda rows: x_copy(g, rows).wait())

        @pl.when(g + ns - 1 < n_used)
        def _():
            by_size(g + ns - 1, lambda rows: x_copy(g + ns - 1, rows).start())

        @pl.when(g >= ns)
        def _():
            by_size(g - ns, lambda rows: y_copy(g - ns, rows).wait())

        def run(rows):
            rid = lax.broadcasted_iota(jnp.int32, (rows, xbuf.shape[2]), 0)
            xp = jnp.where(rid < cnt_ref[e] - b * MOE_BLOCK, xbuf[g % ns, 0:rows], 0)
            x = _unpack_bf16_pairs(xp).astype(BF16)
            gt = _dot(x, wg_scr[...])
            up = _dot(x, wu_scr[...])
            a = (gt * _sigmoid(gt) * up).astype(BF16)
            ybuf[g % ns, 0:rows] = _pack_bf16_pairs(_dot(a, wd_scr[...]))
            y_copy(g, rows).start()

        by_size(g, run)
        return carry

    lax.fori_loop(0, nblk_ref[e], block, 0)

    @pl.when(e == pl.num_programs(0) - 1)
    def _():
        for back in range(ns, 0, -1):
            @pl.when(n_used >= back)
            def _():
                by_size(n_used - back, lambda rows: y_copy(n_used - back, rows).wait())


def _expert_plan(counts, nb_max):
    half = MOE_BLOCK // 2
    units = (counts + half - 1) // half
    nfull, tail = units // 2, units % 2
    nblk = nfull + tail
    pend = jnp.cumsum(units * half)
    pstart = pend - units * half
    blk_end = jnp.cumsum(nblk)
    blk0 = blk_end - nblk
    g = jnp.arange(nb_max, dtype=jnp.int32)
    ne = counts.shape[0]
    eg = jnp.minimum(jnp.sum((blk_end[None, :] <= g[:, None]).astype(jnp.int32), axis=1), ne - 1)
    onehot = (eg[:, None] == jnp.arange(ne, dtype=jnp.int32)[None, :]).astype(jnp.int32)
    pick = lambda v: jnp.sum(onehot * v[None, :], axis=1)
    local = g - pick(blk0)
    is_half = ((local == pick(nfull)) & (pick(tail) == 1)).astype(jnp.int32)
    row0 = jnp.clip(pick(pstart) + local * MOE_BLOCK, 0, pend[-1] - half)
    return pstart, blk0, nblk, row0, is_half, blk_end[-1:]


def _experts(blk0, nblk, counts, row0, is_half, n_used, xs, wg, wu, wd):
    p, dp = xs.shape
    ne, d, ff = wg.shape
    grid_spec = pltpu.PrefetchScalarGridSpec(
        num_scalar_prefetch=6,
        grid=(ne,),
        in_specs=[pl.BlockSpec(memory_space=pl.ANY),
                  pl.BlockSpec((1, d, ff), lambda e, *_: (e, 0, 0)),
                  pl.BlockSpec((1, d, ff), lambda e, *_: (e, 0, 0)),
                  pl.BlockSpec((1, ff, d), lambda e, *_: (e, 0, 0))],
        out_specs=pl.BlockSpec(memory_space=pl.ANY),
        scratch_shapes=[pltpu.VMEM((EXPERT_SLOTS, MOE_BLOCK, dp), jnp.int32),
                        pltpu.VMEM((EXPERT_SLOTS, MOE_BLOCK, dp), jnp.int32),
                        pltpu.VMEM((d, ff), BF16), pltpu.VMEM((d, ff), BF16), pltpu.VMEM((ff, d), BF16),
                        pltpu.SemaphoreType.DMA((EXPERT_SLOTS,)),
                        pltpu.SemaphoreType.DMA((EXPERT_SLOTS,))],
    )
    return pl.pallas_call(
        _experts_kernel,
        grid_spec=grid_spec,
        out_shape=jax.ShapeDtypeStruct((p, dp), jnp.int32),
        compiler_params=_params("arbitrary"),
        name="experts",
    )(blk0, nblk, counts, row0, is_half, n_used, xs, wg, wu, wd)


def _combine_kernel(base_ref, mod_ref, w_ref, y_ref, o_ref):
    acc = None
    for kk in range(TOP_K):
        term = w_ref[:, kk:kk + 1] * _unpack_bf16_pairs(y_ref[kk])
        acc = term if acc is None else acc + term
    o_ref[0] = base_ref[0] + mod_ref[0, 5:6, :] * acc


def _combine(acc, bi, mod, w_tk, yg, tm):
    b, t, d = acc.shape
    nt = t // tm
    tok = pl.BlockSpec((1, tm, d), lambda i: (bi, i, 0))
    return pl.pallas_call(
        _combine_kernel,
        grid=(nt,),
        in_specs=[tok, pl.BlockSpec((1, 8, d), lambda i: (bi, 0, 0)),
                  pl.BlockSpec((tm, TOP_K), lambda i: (bi * nt + i, 0)),
                  pl.BlockSpec((TOP_K, tm, d // 2), lambda i: (0, i, 0))],
        out_specs=tok,
        out_shape=jax.ShapeDtypeStruct((b, t, d), F32),
        input_output_aliases={0: 0},
        compiler_params=_params("parallel"),
        name="combine",
    )(acc, mod, w_tk, yg)


def _rope_tables(t, tm):
    half = M_DQK // 2
    nf = half // 2
    inv = jnp.asarray(np.power(ROPE_BASE, -np.arange(nf, dtype=np.float32) / nf).astype(np.float32))
    ar = jnp.arange(t // GRID_W, dtype=F32)[:, None] * inv[None, :]
    ac = jnp.arange(GRID_W, dtype=F32)[:, None] * inv[None, :]
    zr, zc = jnp.zeros_like(ar), jnp.zeros_like(ac)
    rcos = jnp.concatenate([jnp.cos(ar), jnp.cos(ar), zr, zr], axis=1)
    rsin = jnp.concatenate([-jnp.sin(ar), jnp.sin(ar), zr, zr], axis=1)
    ccos = jnp.tile(jnp.concatenate([zc, zc, jnp.cos(ac), jnp.cos(ac)], axis=1), (tm // GRID_W, 1))
    csin = jnp.tile(jnp.concatenate([zc, zc, -jnp.sin(ac), jnp.sin(ac)], axis=1), (tm // GRID_W, 1))
    return rcos, rsin, ccos, csin


_IN_SIZES = (512, 512, 1024, 1024, 16, 512, 512, 512, 1024, 1024)
_IN_OFFS = tuple(int(v) for v in np.concatenate([[0], np.cumsum(_IN_SIZES)]))


def _arrange_kernel(w_ref, wa_ref, wt_ref):
    seg = lambda i: w_ref[:, _IN_OFFS[i]:_IN_OFFS[i + 1]]
    mq, mk, mv, mo, _, nq, nk, nv, gm, gn = [seg(i) for i in range(10)]
    g0 = _IN_OFFS[4]
    c = w_ref[:, g0:g0 + LANES]
    lane = lax.broadcasted_iota(jnp.int32, c.shape, 1)
    left4, left8 = pltpu.roll(c, LANES - 4, 1), pltpu.roll(c, LANES - 8, 1)
    gi = jnp.where(lane < 4, c, jnp.where(lane < 8, left4, 0.0))
    gf = jnp.where(lane < 4, left4, jnp.where(lane < 8, left8, 0.0))
    wa_ref[...] = jnp.concatenate([mq, mv, mo, gf, nq, nk, nv, gm, gn], axis=1).astype(BF16)
    wt_ref[...] = jnp.concatenate([jnp.transpose(mk), jnp.transpose(gi), jnp.transpose(gf)],
                                  axis=0).astype(BF16)


def _arrange_w_in(w_in, b_mgate):
    d = w_in.shape[0]
    tr = 256
    wt_rows = M_HEADS * M_DQK + 2 * LANES
    w_all, wt_all = pl.pallas_call(
        _arrange_kernel,
        grid=(d // tr,),
        in_specs=[pl.BlockSpec((tr, w_in.shape[1]), lambda i: (i, 0))],
        out_specs=[pl.BlockSpec((tr, W_COLS), lambda i: (i, 0)),
                   pl.BlockSpec((wt_rows, tr), lambda i: (0, i))],
        out_shape=[jax.ShapeDtypeStruct((d, W_COLS), BF16),
                   jax.ShapeDtypeStruct((wt_rows, d), BF16)],
        compiler_params=_params("parallel"),
        name="arrange_w_in",
    )(w_in)
    bpad = jnp.zeros((LANES - 2 * M_HEADS,), F32)
    bi = jnp.concatenate([b_mgate[0:4], b_mgate[8:12], bpad])
    bf = jnp.concatenate([b_mgate[4:8], b_mgate[12:16], bpad])
    bg = jnp.concatenate([bf[None, :], jnp.zeros((7, LANES), F32)], axis=0)
    bgt = jnp.concatenate([bi, bf])[:, None]
    return w_all, wt_all, bg, bgt


def _segment_mats():
    na_w = NA_HEADS * NA_DH
    seg = np.zeros((na_w, LANES), np.float32)
    seg[np.arange(na_w), np.arange(na_w) // NA_DH] = 1.0
    return jnp.asarray(seg, BF16), jnp.asarray(seg.T.copy(), BF16)


def kernel(x, c, ctx, c_ctx, w_ada, b_ada, w_in, b_mgate, m_norm_w, na_qn_w, na_kn_w, na_rpb,
           w_br_m, w_br_na, w_out, w_router, router_bias, w_exp_gate, w_exp_up, w_exp_down,
           w_sh_gate, w_sh_up, w_sh_down):
    b, t, d = x.shape
    n = b * t
    rows = t // GRID_W
    l = 0

    cc = jnp.concatenate([c, c_ctx[None, :], jnp.zeros((8 - b - 1, d), F32)], axis=0)
    mod = _ada(cc, w_ada[l], b_ada[l])
    mod = mod.reshape(8, 6, d)
    mod = jnp.concatenate([mod, jnp.zeros((8, 2, d), F32)], axis=1)
    mod_x = mod[:b]
    mod_c = jnp.broadcast_to(mod[b:b + 1], (b, 8, d))

    w_all, wt_all, bg, bgt = _arrange_w_in(w_in[l], b_mgate[l])
    seg, segt = _segment_mats()
    qnw = jnp.tile(na_qn_w[l], NA_HEADS)[None, :]
    knw = jnp.tile(na_kn_w[l], NA_HEADS)[None, :]
    tm = min(512, t)

    cp = _inproj(ctx, mod_c, w_all, wt_all, bg, bgt, qnw, knw, seg, segt, None,
                 min(tm, ctx.shape[1]))
    xp = _inproj(x, mod_x, w_all, wt_all, bg, bgt, qnw, knw, seg, segt, _rope_tables(t, tm), tm)
    cmq, cmv, _, cgf, _, cnk, cnv, _, _, cmkt, cgit, cgft = cp
    mq, mv, mo, gf, nq, nk, nv, gm, gn, mkt, git, gft = xp

    c0 = jnp.zeros((b, 8, M_DQK, MLSTM_EXT), F32)
    m0 = jnp.zeros((b, 8, LANES), F32)
    _, _, c1, m1 = _mlstm(cmq, cmkt, cmv, cgf, cgit, cgft, c0, m0)
    hf, hb, _, _ = _mlstm(mq, mkt, mv, gf, git, gft, c1, m1)

    yna = _na(nq, nk, nv, cnk, cnv, _na_bias_table(na_rpb[l], rows))

    bias_col = jnp.broadcast_to(router_bias[l][:, None], (N_EXPERTS, LANES))
    h2p, base, top_e, top_w, rank, cnt = _post(
        x, mod_x, hf, hb, mo, yna, gm, gn, m_norm_w[l][None, :],
        w_br_m[l].astype(BF16), w_br_na[l].astype(BF16), w_out[l].astype(BF16),
        w_router[l].T.astype(BF16), w_sh_gate[l].astype(BF16), w_sh_up[l].astype(BF16),
        w_sh_down[l].astype(BF16), bias_col, tm)

    counts = cnt[:, 0].astype(jnp.int32)
    half = MOE_BLOCK // 2
    p_rows = (-(-(n * TOP_K) // half) + N_EXPERTS) * half
    nb_max = -(-(n * TOP_K) // MOE_BLOCK) + N_EXPERTS
    pstart, blk0, nblk, row0, is_half, n_used = _expert_plan(counts, nb_max)

    pstart_col = jnp.broadcast_to(pstart.astype(F32)[:, None], (N_EXPERTS, LANES))
    pos = _slots(top_e, rank, pstart_col, min(512, n))
    xs = _dispatch_rows(h2p, pos, p_rows)
    ys = _experts(blk0, nblk, counts, row0, is_half, n_used, xs,
                  w_exp_gate[l], w_exp_up[l], w_exp_down[l])
    out = base
    for bi in range(b):
        idx = pos[:, bi * t:(bi + 1) * t].reshape(-1)
        yg = _gather_rows(ys, idx).reshape(TOP_K, t, d // 2)
        out = _combine(out, bi, mod_x, top_w, yg, tm)
    return out
```

```python
import functools

import numpy as np
import jax
import jax.numpy as jnp
from jax import lax
from jax.experimental import pallas as pl
from jax.experimental.pallas import tpu as pltpu
from jax.experimental.pallas import tpu_sc as plsc

F32 = jnp.float32
BF16 = jnp.bfloat16

EPS = 1e-6
GRID_W = 64
M_HEADS, M_DQK, M_DV = 4, 128, 256
ROPE_BASE = 10000.0
NA_HEADS, NA_DH, NA_KH, NA_KW = 8, 64, 8, 16
N_EXPERTS, TOP_K, N_GROUPS, TOPK_GROUPS = 256, 8, 8, 4
ROUTE_SCALE = 2.5

LANES = 128
VMEM_LIMIT = 56 * 1024 * 1024
NEG = -1e30

MLSTM_CHUNK = 256
NA_ROWS = 4
NA_KEY_ROWS = NA_ROWS + NA_KH - 1
MOE_BLOCK = 512
EXPERT_SLOTS = 4

_W_SEGS = (("mq", 512), ("mv", 1024), ("mo", 1024), ("gf", 128),
           ("nq", 512), ("nk", 512), ("nv", 512), ("gm", 1024), ("gn", 1024))
_W_OFF = {}
_o = 0
for _n, _w in _W_SEGS:
    _W_OFF[_n] = (_o, _w)
    _o += _w
W_COLS = _o


def _dot(a, b):
    return jnp.dot(a, b, preferred_element_type=F32)


def _dot_nt(a, b):
    return lax.dot_general(a, b, (((1,), (1,)), ((), ())), preferred_element_type=F32)


def _sigmoid(x):
    return 1.0 / (1.0 + jnp.exp(-x))


def _pack_bf16_pairs(v):
    w = v.shape[1] // 2
    bits = pltpu.bitcast(v.astype(BF16).astype(F32), jnp.int32)
    return lax.shift_right_logical(bits[:, :w], 16) | bits[:, w:]


def _unpack_bf16_pairs(p):
    lo = pltpu.bitcast(lax.shift_left(p, 16), F32)
    hi = pltpu.bitcast(p & jnp.int32(-65536), F32)
    return jnp.concatenate([lo, hi], axis=1)


def _params(*sem):
    return pltpu.CompilerParams(dimension_semantics=sem, vmem_limit_bytes=VMEM_LIMIT)


def _resident(shape):
    nd = len(shape)
    return pl.BlockSpec(shape, lambda *_: (0,) * nd, pipeline_mode=pl.Buffered(1))


def _ada_kernel(c_ref, w_ref, b_ref, o_ref):
    c = c_ref[...]
    s = c * _sigmoid(c)
    o_ref[...] = _dot(s.astype(BF16), w_ref[...].astype(BF16)) + b_ref[...]


def _ada(cc, w_ada, b_ada):
    d = cc.shape[1]
    n = w_ada.shape[1]
    return pl.pallas_call(
        _ada_kernel,
        grid=(n // d,),
        in_specs=[pl.BlockSpec((8, d), lambda j: (0, 0)),
                  pl.BlockSpec((d, d), lambda j: (0, j)),
                  pl.BlockSpec((1, d), lambda j: (0, j))],
        out_specs=pl.BlockSpec((8, d), lambda j: (0, j)),
        out_shape=jax.ShapeDtypeStruct((8, n), F32),
        compiler_params=_params("arbitrary"),
        name="ada",
    )(cc, w_ada, b_ada.reshape(1, n))


def _rope_rotate(t, cos, sin):
    q = M_DQK // 4
    lane = lax.broadcasted_iota(jnp.int32, t.shape, 1)
    partner = jnp.where((lane & q) == 0, pltpu.roll(t, M_DQK - q, 1), pltpu.roll(t, q, 1))
    return t * cos + partner * sin


def _rope_rotate_t(t, cos, sin):
    q = M_DQK // 4
    partner = jnp.concatenate([t[q:2 * q], t[0:q], t[3 * q:4 * q], t[2 * q:3 * q]], axis=0)
    return t * cos + partner * sin


def _inproj_kernel(*refs, rope):
    if rope:
        (x_ref, mod_ref, w_ref, wt_ref, bg_ref, bgt_ref, qnw_ref, knw_ref, seg_ref, segt_ref,
         rcos_ref, rsin_ref, ccos_ref, csin_ref,
         mq_ref, mv_ref, mo_ref, gf_ref, nq_ref, nk_ref, nv_ref, gm_ref, gn_ref,
         mkt_ref, git_ref, gft_ref) = refs
    else:
        (x_ref, mod_ref, w_ref, wt_ref, bg_ref, bgt_ref, qnw_ref, knw_ref, seg_ref, segt_ref,
         mq_ref, mv_ref, mo_ref, gf_ref, nq_ref, nk_ref, nv_ref, gm_ref, gn_ref,
         mkt_ref, git_ref, gft_ref) = refs
    x = x_ref[0]
    xn = x * lax.rsqrt(jnp.mean(x * x, axis=-1, keepdims=True) + EPS)
    h = xn * (1.0 + mod_ref[0, 1:2, :]) + mod_ref[0, 0:1, :]
    hb = h.astype(BF16)

    def proj(name):
        off, width = _W_OFF[name]
        return _dot(hb, w_ref[:, off:off + width])

    def head_rms(t, w_row, scale):
        ss = _dot((t * t).astype(BF16), seg_ref[...])
        r = lax.rsqrt(ss * (1.0 / NA_DH) + EPS)
        r_hi = r.astype(BF16)
        r_lo = (r - r_hi.astype(F32)).astype(BF16)
        rb = _dot(r_hi, segt_ref[...]) + _dot(r_lo, segt_ref[...])
        return t * rb * w_row * scale

    mq = proj("mq") * (M_DQK ** -0.5)
    if rope:
        tm = x.shape[0]
        spread = lambda r: jnp.broadcast_to(r[:, None, :], (tm // GRID_W, GRID_W, LANES)).reshape(tm, LANES)
        cos = spread(rcos_ref[...]) + ccos_ref[...]
        sin = spread(rsin_ref[...]) + csin_ref[...]
        mq = jnp.concatenate([_rope_rotate(mq[:, i * LANES:(i + 1) * LANES], cos, sin)
                              for i in range(M_HEADS)], axis=1)
    mq_ref[0] = mq.astype(BF16)
    mv_ref[0] = proj("mv").astype(BF16)
    mo_ref[0] = _sigmoid(proj("mo")).astype(BF16)
    gf_ref[0] = proj("gf") + bg_ref[0:1, :]
    nq_ref[0] = head_rms(proj("nq"), qnw_ref[...], NA_DH ** -0.5).astype(BF16)
    nk_ref[0] = head_rms(proj("nk"), knw_ref[...], 1.0).astype(BF16)
    nv_ref[0] = proj("nv").astype(BF16)
    gm_ref[0] = _sigmoid(proj("gm")).astype(BF16)
    gn_ref[0] = _sigmoid(proj("gn")).astype(BF16)

    qk_w = M_HEADS * M_DQK
    mkt = _dot_nt(wt_ref[0:qk_w, :], hb)
    if rope:
        cost, sint = jnp.transpose(cos), jnp.transpose(sin)
        mkt = jnp.concatenate([_rope_rotate_t(mkt[i * M_DQK:(i + 1) * M_DQK], cost, sint)
                               for i in range(M_HEADS)], axis=0)
    mkt_ref[0] = mkt.astype(BF16)
    git_ref[0] = _dot_nt(wt_ref[qk_w:qk_w + LANES, :], hb) + bgt_ref[0:LANES, :]
    gft_ref[0] = _dot_nt(wt_ref[qk_w + LANES:qk_w + 2 * LANES, :], hb) + bgt_ref[LANES:2 * LANES, :]


def _inproj(x, mod, w_all, wt_all, bg, bgt, qnw, knw, seg, segt, rope_tabs, tm):
    b, t, d = x.shape
    rope = rope_tabs is not None
    tok = lambda w: pl.BlockSpec((1, tm, w), lambda bi, i: (bi, i, 0))
    tok_t = lambda w: pl.BlockSpec((1, w, tm), lambda bi, i: (bi, 0, i))
    in_specs = [tok(d),
                pl.BlockSpec((1, 8, d), lambda bi, i: (bi, 0, 0)),
                _resident(w_all.shape), _resident(wt_all.shape), _resident(bg.shape),
                _resident(bgt.shape), _resident(qnw.shape),
                _resident(knw.shape), _resident(seg.shape), _resident(segt.shape)]
    args = [x, mod, w_all, wt_all, bg, bgt, qnw, knw, seg, segt]
    if rope:
        in_specs += [pl.BlockSpec((tm // GRID_W, LANES), lambda bi, i: (i, 0))] * 2
        in_specs += [_resident((tm, LANES))] * 2
        args += list(rope_tabs)
    widths = [("mq", BF16), ("mv", BF16), ("mo", BF16), ("gf", F32),
              ("nq", BF16), ("nk", BF16), ("nv", BF16), ("gm", BF16), ("gn", BF16)]
    out_specs = [tok(_W_OFF[n][1]) for n, _ in widths]
    out_shape = [jax.ShapeDtypeStruct((b, t, _W_OFF[n][1]), dt) for n, dt in widths]
    out_specs += [tok_t(M_HEADS * M_DQK), tok_t(LANES), tok_t(LANES)]
    out_shape += [jax.ShapeDtypeStruct((b, M_HEADS * M_DQK, t), BF16),
                  jax.ShapeDtypeStruct((b, LANES, t), F32),
                  jax.ShapeDtypeStruct((b, LANES, t), F32)]
    return pl.pallas_call(
        functools.partial(_inproj_kernel, rope=rope),
        grid=(b, t // tm),
        in_specs=in_specs, out_specs=out_specs, out_shape=out_shape,
        compiler_params=_params("parallel", "parallel"),
        name="inproj_rope" if rope else "inproj_ctx",
    )(*args)


def _log_sigmoid(x):
    return jnp.minimum(x, 0.0) - jnp.log(1.0 + jnp.exp(-jnp.abs(x)))


def _dot_split(a, b, split_a):
    x = a if split_a else b
    hi = x.astype(BF16)
    lo = (x - hi.astype(F32)).astype(BF16)
    return (_dot(hi, b) + _dot(lo, b)) if split_a else (_dot(a, hi) + _dot(a, lo))


MLSTM_EXT = M_DV + LANES


def _mlstm_kernel(qf_ref, ktf_ref, vf_ref, gff_ref, gitf_ref, gftf_ref,
                  qb_ref, ktb_ref, vb_ref, gfb_ref, gitb_ref, gftb_ref,
                  c0_ref, m0_ref,
                  hf_ref, hb_ref, cn_ref, mn_ref,
                  *scratch):
    c_scrs, m_scr = scratch[:2 * M_HEADS], scratch[2 * M_HEADS]
    step = pl.program_id(1)
    L = qf_ref.shape[1]
    nu = 2 * M_HEADS

    @pl.when(step == 0)
    def _():
        for j, c_scr in enumerate(c_scrs):
            c_scr[...] = c0_ref[0, j]
        m_scr[...] = m0_ref[0]

    row_i = lax.broadcasted_iota(jnp.int32, (L, L), 0)
    col_i = lax.broadcasted_iota(jnp.int32, (L, L), 1)
    lower = col_i <= row_i
    upper = col_i >= row_i
    tri_lo = jnp.where(lower, 1.0, 0.0).astype(BF16)
    tri_up = jnp.where(upper, 1.0, 0.0).astype(BF16)

    is_f = lax.broadcasted_iota(jnp.int32, (nu, L), 0) < M_HEADS
    gi_t = jnp.where(is_f, gitf_ref[0, 0:nu, :], gitb_ref[0, 0:nu, :])
    ls_tf = _log_sigmoid(gftf_ref[0, 0:nu, :])
    ls_tb = _log_sigmoid(gftb_ref[0, 0:nu, :])
    b_t = jnp.where(is_f, _dot_split(ls_tf, tri_up, True), _dot_split(ls_tb, tri_lo, True))
    u_t = gi_t - b_t
    g_c = jnp.sum(jnp.where(is_f, ls_tf, ls_tb), axis=1, keepdims=True)
    m_prev = m_scr[...]
    a_t = g_c + u_t
    m_new = jnp.maximum(g_c + m_prev, jnp.max(a_t, axis=1, keepdims=True))
    decay = jnp.exp(g_c + m_prev - m_new)
    wa_t = jnp.exp(a_t - jnp.concatenate([m_new] * (L // LANES), axis=1))
    m_scr[...] = m_new

    ones = jnp.ones((L, LANES), BF16)
    dirs = ((qf_ref, ktf_ref, vf_ref, gff_ref, hf_ref, lower, tri_lo),
            (qb_ref, ktb_ref, vb_ref, gfb_ref, hb_ref, upper, tri_up))
    for d, (q_ref, kt_ref, v_ref, gf_ref, h_ref, mask, tri) in enumerate(dirs):
        bcum = _dot_split(tri, _log_sigmoid(gf_ref[0]), False)
        for hd in range(M_HEADS):
            j = d * M_HEADS + hd
            c_scr = c_scrs[j]
            q = q_ref[0, :, hd * M_DQK:(hd + 1) * M_DQK]
            k_t = kt_ref[0, hd * M_DQK:(hd + 1) * M_DQK, :]
            v_ext = jnp.concatenate([v_ref[0, :, hd * M_DV:(hd + 1) * M_DV], ones], axis=1)
            u_row = u_t[j:j + 1, :]
            mp_row = m_prev[j:j + 1, :]
            c_prev = c_scr[...]

            m_loc = jnp.max(jnp.where(mask, u_row, NEG), axis=1, keepdims=True)
            m_rep = jnp.maximum(jnp.broadcast_to(m_loc, (L, LANES)), mp_row)
            m_wide = jnp.concatenate([m_rep] * (L // LANES), axis=1)
            dmat = jnp.exp(jnp.where(mask, u_row - m_wide, NEG))
            s = (_dot(q, k_t) * dmat).astype(BF16)
            qw = (q.astype(F32) * jnp.exp(mp_row - m_rep)).astype(BF16)
            r = _dot(s, v_ext) + _dot(qw, c_prev.astype(BF16))
            b_rep = jnp.broadcast_to(bcum[:, j:j + 1], (L, LANES))
            dn = jnp.maximum(jnp.abs(r[:, M_DV:]), jnp.exp(-(b_rep + m_rep)))
            h_ref[0, :, hd * M_DV:(hd + 1) * M_DV] = (
                r[:, :M_DV] / jnp.concatenate([dn] * (M_DV // LANES), axis=1)).astype(h_ref.dtype)

            kw = (k_t.astype(F32) * wa_t[j:j + 1, :]).astype(BF16)
            dec = jnp.concatenate([decay[j:j + 1, :]] * (MLSTM_EXT // LANES), axis=1)
            c_scr[...] = dec * c_prev + _dot(kw, v_ext)

    @pl.when(step == pl.num_programs(1) - 1)
    def _():
        for j, c_scr in enumerate(c_scrs):
            cn_ref[0, j] = c_scr[...]
        mn_ref[0] = m_scr[...]


def _mlstm(q, kt, v, gf, git, gft, c0, m0):
    b, t, _ = q.shape
    L = min(MLSTM_CHUNK, t)
    nc = t // L
    fwd = lambda w: pl.BlockSpec((1, L, w), lambda bi, i: (bi, i, 0))
    bwd = lambda w: pl.BlockSpec((1, L, w), lambda bi, i: (bi, nc - 1 - i, 0))
    fwd_t = lambda w: pl.BlockSpec((1, w, L), lambda bi, i: (bi, 0, i))
    bwd_t = lambda w: pl.BlockSpec((1, w, L), lambda bi, i: (bi, 0, nc - 1 - i))
    st_c = pl.BlockSpec((1, 8, M_DQK, MLSTM_EXT), lambda bi, i: (bi, 0, 0, 0))
    st_v = pl.BlockSpec((1, 8, LANES), lambda bi, i: (bi, 0, 0))
    qk_w, v_w = M_HEADS * M_DQK, M_HEADS * M_DV
    return pl.pallas_call(
        _mlstm_kernel,
        grid=(b, nc),
        in_specs=[fwd(qk_w), fwd_t(qk_w), fwd(v_w), fwd(LANES), fwd_t(LANES), fwd_t(LANES),
                  bwd(qk_w), bwd_t(qk_w), bwd(v_w), bwd(LANES), bwd_t(LANES), bwd_t(LANES),
                  st_c, st_v],
        out_specs=[fwd(v_w), bwd(v_w), st_c, st_v],
        out_shape=[jax.ShapeDtypeStruct((b, t, v_w), BF16),
                   jax.ShapeDtypeStruct((b, t, v_w), BF16),
                   jax.ShapeDtypeStruct(c0.shape, F32),
                   jax.ShapeDtypeStruct(m0.shape, F32)],
        scratch_shapes=([pltpu.VMEM((M_DQK, MLSTM_EXT), F32) for _ in range(2 * M_HEADS)]
                        + [pltpu.VMEM((8, LANES), F32)]),
        compiler_params=_params("parallel", "arbitrary"),
        name="mlstm",
    )(q, kt, v, gf, git, gft, q, kt, v, gf, git, gft, c0, m0)


def _na_kernel(q_ref, k_ref, v_ref, kc_ref, vc_ref, bias_ref, o_ref, *, rows):
    r0 = pl.program_id(2) * NA_ROWS
    ks = jnp.clip(r0 - NA_KH // 2, 0, rows - NA_KEY_ROWS)
    kstart = pl.multiple_of(ks * GRID_W, GRID_W)
    nkeys = NA_KEY_ROWS * GRID_W
    kblk = k_ref[0, pl.ds(kstart, nkeys), :]
    vblk = v_ref[0, pl.ds(kstart, nkeys), :]
    kc = kc_ref[0]
    vc = vc_ref[0]
    q = q_ref[0]
    lane = lax.broadcasted_iota(jnp.int32, q.shape, 1)
    outs = []
    for hh in range(2):
        in_head = (lane < NA_DH) if hh == 0 else (lane >= NA_DH)
        qm = jnp.where(in_head, q, jnp.zeros_like(q))
        sw = _dot_nt(qm, kblk) + bias_ref[hh, 0]
        sc = _dot_nt(qm, kc)
        m = jnp.maximum(jnp.max(sw, axis=1, keepdims=True), jnp.max(sc, axis=1, keepdims=True))
        ew = jnp.exp(sw - m)
        ec = jnp.exp(sc - m)
        l = jnp.sum(ew, axis=1, keepdims=True) + jnp.sum(ec, axis=1, keepdims=True)
        o = _dot(ew.astype(BF16), vblk) + _dot(ec.astype(BF16), vc)
        outs.append(o / l)
    o_ref[0] = jnp.where(lane < NA_DH, outs[0], outs[1]).astype(o_ref.dtype)


def _na(nq, nk, nv, cnk, cnv, bias):
    b, t, w = nq.shape
    rows = t // GRID_W
    tq = NA_ROWS * GRID_W
    nrb = rows // NA_ROWS
    nctx = cnk.shape[1]
    kind = lambda rb: jnp.where(rb == 0, 0, jnp.where(rb == nrb - 1, 2, 1))
    return pl.pallas_call(
        functools.partial(_na_kernel, rows=rows),
        grid=(b, w // LANES, nrb),
        in_specs=[pl.BlockSpec((1, tq, LANES), lambda bi, hp, rb: (bi, rb, hp)),
                  pl.BlockSpec((1, t, LANES), lambda bi, hp, rb: (bi, 0, hp)),
                  pl.BlockSpec((1, t, LANES), lambda bi, hp, rb: (bi, 0, hp)),
                  pl.BlockSpec((1, nctx, LANES), lambda bi, hp, rb: (bi, 0, hp)),
                  pl.BlockSpec((1, nctx, LANES), lambda bi, hp, rb: (bi, 0, hp)),
                  pl.BlockSpec((2, 1, tq, NA_KEY_ROWS * GRID_W),
                               lambda bi, hp, rb: (hp, kind(rb), 0, 0))],
        out_specs=pl.BlockSpec((1, tq, LANES), lambda bi, hp, rb: (bi, rb, hp)),
        out_shape=jax.ShapeDtypeStruct((b, t, w), BF16),
        compiler_params=_params("parallel", "parallel", "arbitrary"),
        name="na",
    )(nq, nk, nv, cnk, cnv, bias)


def _na_bias_table(na_rpb, rows):
    h = na_rpb.shape[0]
    w = GRID_W
    c = np.arange(w)[:, None]
    kj = np.arange(w)[None, :]
    cs = np.clip(c - NA_KW // 2, 0, w - NA_KW)
    col_valid = (kj >= cs) & (kj < cs + NA_KW)
    dc = np.clip(kj - c + (NA_KW - 1), 0, 2 * NA_KW - 2)
    onehot = np.zeros((2 * NA_KW - 1, w, w), np.float32)
    onehot[dc, np.arange(w)[:, None], np.arange(w)[None, :]] = 1.0
    t2 = jnp.einsum("hrd,dck->hrck", na_rpb, jnp.asarray(onehot), precision=lax.Precision.HIGHEST)
    t2 = jnp.where(jnp.asarray(col_valid)[None, None], t2, NEG)
    t2 = jnp.concatenate([t2, jnp.full((h, 1, w, w), NEG, F32)], axis=1)
    invalid = 2 * NA_KH - 1
    dr_idx = np.full((3, NA_ROWS, NA_KEY_ROWS), invalid, np.int32)
    for kind, r0 in enumerate((0, NA_ROWS, rows - NA_ROWS)):
        ks = int(np.clip(r0 - NA_KH // 2, 0, rows - NA_KEY_ROWS))
        for qa in range(NA_ROWS):
            r = r0 + qa
            rs = int(np.clip(r - NA_KH // 2, 0, rows - NA_KH))
            for kl in range(NA_KEY_ROWS):
                ki = ks + kl
                if rs <= ki < rs + NA_KH:
                    dr_idx[kind, qa, kl] = ki - r + NA_KH - 1
    t2t = t2.transpose(0, 2, 1, 3)
    strips = [jnp.concatenate([t2t[:, :, int(dr), :] for dr in dr_idx[kind, qa]], axis=-1)
              for kind in range(3) for qa in range(NA_ROWS)]
    return jnp.stack(strips, axis=1).reshape(h, 3, NA_ROWS * w, NA_KEY_ROWS * w)


def _post_kernel(x_ref, mod_ref, hf_ref, hb_ref, mo_ref, na_ref, gm_ref, gn_ref,
                 mnw_ref, wbm_ref, wbn_ref, wout_ref, wr_ref, wsg_ref, wsu_ref, wsd_ref, rb_ref,
                 h2_ref, base_ref, e_ref, w_ref, r_ref, cnt_ref, run_scr):
    @pl.when(pl.program_id(0) == 0)
    def _():
        run_scr[...] = jnp.zeros_like(run_scr)

    hm = hf_ref[0].astype(F32) + hb_ref[0].astype(F32)
    parts = []
    for hd in range(M_HEADS):
        t = hm[:, hd * M_DV:(hd + 1) * M_DV]
        parts.append(t * lax.rsqrt(jnp.mean(t * t, axis=-1, keepdims=True) + EPS))
    y_m = jnp.concatenate(parts, axis=1) * mnw_ref[...] * mo_ref[0].astype(F32)
    a = _dot(y_m.astype(BF16), wbm_ref[...])
    bn = _dot(na_ref[0], wbn_ref[...])
    z = gm_ref[0].astype(F32) * a + gn_ref[0].astype(F32) * bn
    y = _dot(z.astype(BF16), wout_ref[...])
    x1 = x_ref[0] + mod_ref[0, 2:3, :] * y
    xn = x1 * lax.rsqrt(jnp.mean(x1 * x1, axis=-1, keepdims=True) + EPS)
    h2f = xn * (1.0 + mod_ref[0, 4:5, :]) + mod_ref[0, 3:4, :]
    h2_ref[...] = _pack_bf16_pairs(h2f)
    h2 = h2f.astype(BF16)
    sh = _dot(h2, wsg_ref[...])
    sh = sh * _sigmoid(sh) * _dot(h2, wsu_ref[...])
    base_ref[0] = x1 + mod_ref[0, 5:6, :] * _dot(sh.astype(BF16), wsd_ref[...])
    scores = _sigmoid(_dot_nt(wr_ref[...], h2))
    _route_block(scores, rb_ref, e_ref, w_ref, r_ref, cnt_ref, run_scr)


def _post(x, mod, hf, hb, mo, yna, gm, gn, mnw, wbm, wbn, wout, wr_t, wsg, wsu, wsd, rbias, tm):
    b, t, d = x.shape
    nt = t // tm
    n = b * t
    tok = lambda w: pl.BlockSpec((1, tm, w), lambda s: (s // nt, s % nt, 0))
    rt = lambda: pl.BlockSpec((TOP_K, tm), lambda s: (0, s))
    res = [mnw, wbm, wbn, wout, wr_t, wsg, wsu, wsd, rbias]
    return pl.pallas_call(
        _post_kernel,
        grid=(b * nt,),
        in_specs=[tok(d), pl.BlockSpec((1, 8, d), lambda s: (s // nt, 0, 0)),
                  tok(hf.shape[2]), tok(hb.shape[2]), tok(mo.shape[2]), tok(yna.shape[2]),
                  tok(gm.shape[2]), tok(gn.shape[2])] + [_resident(a.shape) for a in res],
        out_specs=[pl.BlockSpec((tm, d // 2), lambda s: (s, 0)),
                   tok(d), rt(), pl.BlockSpec((tm, TOP_K), lambda s: (s, 0)), rt(),
                   pl.BlockSpec((N_EXPERTS, LANES), lambda s: (0, 0))],
        out_shape=[jax.ShapeDtypeStruct((n, d // 2), jnp.int32),
                   jax.ShapeDtypeStruct((b, t, d), F32),
                   jax.ShapeDtypeStruct((TOP_K, n), jnp.int32),
                   jax.ShapeDtypeStruct((n, TOP_K), F32),
                   jax.ShapeDtypeStruct((TOP_K, n), jnp.int32),
                   jax.ShapeDtypeStruct((N_EXPERTS, LANES), F32)],
        scratch_shapes=[pltpu.VMEM((N_EXPERTS, LANES), F32)],
        compiler_params=_params("arbitrary"),
        name="post",
    )(x, mod, hf, hb, mo, yna, gm, gn, *res)


def _route_block(s, b_ref, e_ref, w_ref, r_ref, cnt_ref, run_scr):
    tm = s.shape[1]
    sel = s + b_ref[...][:, 0:1]
    gsz = N_EXPERTS // N_GROUPS
    ninf = -jnp.inf

    x3 = sel.reshape(N_GROUPS, gsz, tm)
    r3 = lax.broadcasted_iota(jnp.int32, x3.shape, 1)
    m1 = jnp.max(x3, axis=1, keepdims=True)
    i1 = jnp.min(jnp.where(x3 == m1, r3, gsz), axis=1, keepdims=True)
    m2 = jnp.max(jnp.where(r3 == i1, ninf, x3), axis=1)
    gs = m1[:, 0, :] + m2

    gidx = lax.broadcasted_iota(jnp.int32, gs.shape, 0)
    gkeep = jnp.zeros(gs.shape, jnp.bool_)
    cur = gs
    for _ in range(TOPK_GROUPS):
        mm = jnp.max(cur, axis=0, keepdims=True)
        ii = jnp.min(jnp.where(cur == mm, gidx, N_GROUPS), axis=0, keepdims=True)
        hit = gidx == ii
        gkeep = jnp.logical_or(gkeep, hit)
        cur = jnp.where(hit, ninf, cur)
    keep = jnp.broadcast_to(gkeep[:, None, :], x3.shape).reshape(N_EXPERTS, tm)

    row = lax.broadcasted_iota(jnp.int32, s.shape, 0).astype(F32)
    cur = jnp.where(keep, sel, ninf)
    idxs, ws = [], []
    chosen_f = jnp.zeros(s.shape, F32)
    for _ in range(TOP_K):
        mm = jnp.max(cur, axis=0, keepdims=True)
        ii = jnp.min(jnp.where(cur == mm, row, float(N_EXPERTS)), axis=0, keepdims=True)
        hit = row == ii
        idxs.append(ii)
        ws.append(jnp.sum(jnp.where(hit, s, 0.0), axis=0, keepdims=True))
        chosen_f = jnp.where(hit, 1.0, chosen_f)
        cur = jnp.where(hit, ninf, cur)
    wsum = ws[0]
    for wk in ws[1:]:
        wsum = wsum + wk

    tp =lax.broadcasted_iota(jnp.int32, (tm, tm), 0)
    tc = lax.broadcasted_iota(jnp.int32, (tm, tm), 1)
    before = jnp.where(tp < tc, 1.0, 0.0).astype(BF16)
    rank = _dot(chosen_f.astype(BF16), before) + run_scr[...][:, 0:1]
    run_scr[...] = run_scr[...] + jnp.sum(chosen_f, axis=1, keepdims=True)
    cnt_ref[...] = run_scr[...]

    for kk in range(TOP_K):
        e_ref[kk:kk + 1, :] = idxs[kk].astype(jnp.int32)
        r_ref[kk:kk + 1, :] = jnp.sum(jnp.where(row == idxs[kk], rank, 0.0), axis=0,
                                      keepdims=True).astype(jnp.int32)
    w_ref[...] = jnp.transpose(jnp.concatenate([wk / wsum * ROUTE_SCALE for wk in ws], axis=0))


SC_WINDOW = 128


def _sc_mesh():
    return plsc.VectorSubcoreMesh(core_axis_name="core", subcore_axis_name="subcore")


def _sc_workers():
    info = plsc.get_sparse_core_info()
    return info.num_cores, info.num_cores * info.num_subcores


def _dispatch_rows(x, top_e, rank, pstart, p_rows):
    n, w = x.shape
    kk = top_e.shape[0]
    ncores, nw = _sc_workers()
    lanes = plsc.get_sparse_core_info().num_lanes
    steps = n // nw // SC_WINDOW
    per_worker = lambda a: a.reshape(kk, nw, steps, SC_WINDOW).transpose(1, 2, 0, 3)

    @functools.partial(
        pl.kernel, mesh=_sc_mesh(),
        out_type=[jax.ShapeDtypeStruct((p_rows, w), x.dtype),
                  jax.ShapeDtypeStruct((nw, steps, kk, SC_WINDOW), jnp.int32)],
        scratch_types=[pltpu.VMEM((kk, SC_WINDOW), jnp.int32),
                       pltpu.VMEM((kk, SC_WINDOW), jnp.int32),
                       pltpu.VMEM((kk, SC_WINDOW), jnp.int32),
                       pltpu.VMEM(pstart.shape, jnp.int32),
                       pltpu.VMEM((SC_WINDOW, w), x.dtype),
                       pltpu.SemaphoreType.DMA],
        compiler_params=pltpu.CompilerParams(needs_layout_passes=False),
    )
    def scatter(x_hbm, e_hbm, r_hbm, ps_hbm, o_hbm, pos_hbm, e_v, r_v, pos_v, ps_v, rows_v, sem):
        wid = lax.axis_index("subcore") * ncores + lax.axis_index("core")
        pltpu.sync_copy(ps_hbm, ps_v)

        @pl.loop(0, steps)
        def _(s):
            base = pl.multiple_of((wid * steps + s) * SC_WINDOW, SC_WINDOW)
            pltpu.sync_copy(e_hbm.at[wid, s], e_v)
            pltpu.sync_copy(r_hbm.at[wid, s], r_v)
            pltpu.sync_copy(x_hbm.at[pl.ds(base, SC_WINDOW)], rows_v)
            for j in range(kk):
                for c in range(SC_WINDOW // lanes):
                    cols = pl.ds(c * lanes, lanes)
                    pos_v[j, cols] = plsc.load_gather(ps_v, [e_v[j, cols]]) + r_v[j, cols]
            pltpu.sync_copy(pos_v, pos_hbm.at[wid, s])
            copies = [pltpu.make_async_copy(rows_v, o_hbm.at[pos_v.at[j]], sem) for j in range(kk)]
            for cp in copies:
                cp.start()
            for cp in copies:
                cp.wait()

    out, pos4 = scatter(x, per_worker(top_e), per_worker(rank), pstart)
    return out, pos4.transpose(2, 0, 1, 3).reshape(kk, n)


def _gather_rows(x, idx):
    m = idx.shape[0]
    w = x.shape[1]
    ncores, nw = _sc_workers()
    steps = m // nw // SC_WINDOW
    idx3 = idx.reshape(nw, steps, SC_WINDOW)

    @functools.partial(
        pl.kernel, mesh=_sc_mesh(),
        out_type=jax.ShapeDtypeStruct((m, w), x.dtype),
        scratch_types=[pltpu.VMEM((steps, SC_WINDOW), jnp.int32),
                       pltpu.VMEM((SC_WINDOW, w), x.dtype),
                       pltpu.SemaphoreType.DMA],
    )
    def gather(x_hbm, i_hbm, o_hbm, idx_v, rows_v, sem):
        wid = lax.axis_index("subcore") * ncores + lax.axis_index("core")
        pltpu.sync_copy(i_hbm.at[wid], idx_v)

        @pl.loop(0, steps)
        def _(s):
            pltpu.async_copy(x_hbm.at[idx_v.at[s]], rows_v, sem).wait()
            base = pl.multiple_of((wid * steps + s) * SC_WINDOW, SC_WINDOW)
            pltpu.sync_copy(rows_v, o_hbm.at[pl.ds(base, SC_WINDOW)])

    return gather(x, idx3)


def _experts_kernel(blk0_ref, nblk_ref, cnt_ref, row0_ref, half_ref, nu_ref,
                    x_hbm, wg_ref, wu_ref, wd_ref, y_hbm,
                    xbuf, ybuf, wg_scr, wu_scr, wd_scr, in_sem, out_sem):
    e = pl.program_id(0)
    n_used = nu_ref[0]
    blk0 = blk0_ref[e]
    ns = EXPERT_SLOTS
    sizes = (MOE_BLOCK, MOE_BLOCK // 2)

    def x_copy(g, rows):
        r0 = pl.multiple_of(row0_ref[g], MOE_BLOCK // 2)
        return pltpu.make_async_copy(x_hbm.at[pl.ds(r0, rows)], xbuf.at[g % ns, pl.ds(0, rows)],
                                     in_sem.at[g % ns])

    def y_copy(g, rows):
        r0 = pl.multiple_of(row0_ref[g], MOE_BLOCK // 2)
        return pltpu.make_async_copy(ybuf.at[g % ns, pl.ds(0, rows)], y_hbm.at[pl.ds(r0, rows)],
                                     out_sem.at[g % ns])

    def by_size(g, fn):
        for is_half, rows in enumerate(sizes):
            @pl.when(half_ref[g] == is_half)
            def _():
                fn(rows)

    @pl.when(e == 0)
    def _():
        for g0 in range(ns - 1):
            @pl.when(g0 < n_used)
            def _():
                by_size(g0, lambda rows: x_copy(g0, rows).start())

    wg_scr[...] = wg_ref[0].astype(BF16)
    wu_scr[...] = wu_ref[0].astype(BF16)
    wd_scr[...] = wd_ref[0].astype(BF16)

    def block(b, carry):
        g = blk0 + b
        by_size(g, lambda rows: x_copy(g, rows).wait())

        @pl.when(g + ns - 1 < n_used)
        def _():
            by_size(g + ns - 1, lambda rows: x_copy(g + ns - 1, rows).start())

        @pl.when(g >= ns)
        def _():
            by_size(g - ns, lambda rows: y_copy(g - ns, rows).wait())

        def run(rows):
            rid = lax.broadcasted_iota(jnp.int32, (rows, xbuf.shape[2]), 0)
            xp = jnp.where(rid < cnt_ref[e] - b * MOE_BLOCK, xbuf[g % ns, 0:rows], 0)
            x = _unpack_bf16_pairs(xp).astype(BF16)
            gt = _dot(x, wg_scr[...])
            up = _dot(x, wu_scr[...])
            a = (gt * _sigmoid(gt) * up).astype(BF16)
            ybuf[g % ns, 0:rows] = _pack_bf16_pairs(_dot(a, wd_scr[...]))
            y_copy(g, rows).start()

        by_size(g, run)
        return carry

    lax.fori_loop(0, nblk_ref[e], block, 0)

    @pl.when(e == pl.num_programs(0) - 1)
    def _():
        for back in range(ns, 0, -1):
            @pl.when(n_used >= back)
            def _():
                by_size(n_used - back, lambda rows: y_copy(n_used - back, rows).wait())


def _expert_plan(counts, nb_max):
    half = MOE_BLOCK // 2
    units = (counts + half - 1) // half
    nfull, tail = units // 2, units % 2
    nblk = nfull + tail
    pend = jnp.cumsum(units * half)
    pstart = pend - units * half
    blk_end = jnp.cumsum(nblk)
    blk0 = blk_end - nblk
    g = jnp.arange(nb_max, dtype=jnp.int32)
    ne = counts.shape[0]
    eg = jnp.minimum(jnp.sum((blk_end[None, :] <= g[:, None]).astype(jnp.int32), axis=1), ne - 1)
    onehot = (eg[:, None] == jnp.arange(ne, dtype=jnp.int32)[None, :]).astype(jnp.int32)
    pick = lambda v: jnp.sum(onehot * v[None, :], axis=1)
    local = g - pick(blk0)
    is_half = ((local == pick(nfull)) & (pick(tail) == 1)).astype(jnp.int32)
    row0 = jnp.clip(pick(pstart) + local * MOE_BLOCK, 0, pend[-1] - half)
    return pstart, blk0, nblk, row0, is_half, blk_end[-1:]


def _experts(blk0, nblk, counts, row0, is_half, n_used, xs, wg, wu, wd):
    p, dp = xs.shape
    ne, d, ff = wg.shape
    grid_spec = pltpu.PrefetchScalarGridSpec(
        num_scalar_prefetch=6,
        grid=(ne,),
        in_specs=[pl.BlockSpec(memory_space=pl.ANY),
                  pl.BlockSpec((1, d, ff), lambda e, *_: (e, 0, 0)),
                  pl.BlockSpec((1, d, ff), lambda e, *_: (e, 0, 0)),
                  pl.BlockSpec((1, ff, d), lambda e, *_: (e, 0, 0))],
        out_specs=pl.BlockSpec(memory_space=pl.ANY),
        scratch_shapes=[pltpu.VMEM((EXPERT_SLOTS, MOE_BLOCK, dp), jnp.int32),
                        pltpu.VMEM((EXPERT_SLOTS, MOE_BLOCK, dp), jnp.int32),
                        pltpu.VMEM((d, ff), BF16), pltpu.VMEM((d, ff), BF16), pltpu.VMEM((ff, d), BF16),
                        pltpu.SemaphoreType.DMA((EXPERT_SLOTS,)),
                        pltpu.SemaphoreType.DMA((EXPERT_SLOTS,))],
    )
    return pl.pallas_call(
        _experts_kernel,
        grid_spec=grid_spec,
        out_shape=jax.ShapeDtypeStruct((p, dp), jnp.int32),
        compiler_params=_params("arbitrary"),
        name="experts",
    )(blk0, nblk, counts, row0, is_half, n_used, xs, wg, wu, wd)


def _combine_kernel(base_ref, mod_ref, w_ref, y_ref, o_ref):
    acc = None
    for kk in range(TOP_K):
        term = w_ref[:, kk:kk + 1] * _unpack_bf16_pairs(y_ref[kk])
        acc = term if acc is None else acc + term
    o_ref[0] = base_ref[0] + mod_ref[0, 5:6, :] * acc


def _combine(acc, bi, mod, w_tk, yg, tm):
    b, t, d = acc.shape
    nt = t // tm
    tok = pl.BlockSpec((1, tm, d), lambda i: (bi, i, 0))
    return pl.pallas_call(
        _combine_kernel,
        grid=(nt,),
        in_specs=[tok, pl.BlockSpec((1, 8, d), lambda i: (bi, 0, 0)),
                  pl.BlockSpec((tm, TOP_K), lambda i: (bi * nt + i, 0)),
                  pl.BlockSpec((TOP_K, tm, d // 2), lambda i: (0, i, 0))],
        out_specs=tok,
        out_shape=jax.ShapeDtypeStruct((b, t, d), F32),
        input_output_aliases={0: 0},
        compiler_params=_params("parallel"),
        name="combine",
    )(acc, mod, w_tk, yg)


def _rope_tables(t, tm):
    half = M_DQK // 2
    nf = half // 2
    inv = jnp.asarray(np.power(ROPE_BASE, -np.arange(nf, dtype=np.float32) / nf).astype(np.float32))
    ar = jnp.arange(t // GRID_W, dtype=F32)[:, None] * inv[None, :]
    ac = jnp.arange(GRID_W, dtype=F32)[:, None] * inv[None, :]
    zr, zc = jnp.zeros_like(ar), jnp.zeros_like(ac)
    rcos = jnp.concatenate([jnp.cos(ar), jnp.cos(ar), zr, zr], axis=1)
    rsin = jnp.concatenate([-jnp.sin(ar), jnp.sin(ar), zr, zr], axis=1)
    ccos = jnp.tile(jnp.concatenate([zc, zc, jnp.cos(ac), jnp.cos(ac)], axis=1), (tm // GRID_W, 1))
    csin = jnp.tile(jnp.concatenate([zc, zc, -jnp.sin(ac), jnp.sin(ac)], axis=1), (tm // GRID_W, 1))
    return rcos, rsin, ccos, csin


_IN_SIZES = (512, 512, 1024, 1024, 16, 512, 512, 512, 1024, 1024)
_IN_OFFS = tuple(int(v) for v in np.concatenate([[0], np.cumsum(_IN_SIZES)]))


def _arrange_kernel(w_ref, wa_ref, wt_ref):
    seg = lambda i: w_ref[:, _IN_OFFS[i]:_IN_OFFS[i + 1]]
    mq, mk, mv, mo, _, nq, nk, nv, gm, gn = [seg(i) for i in range(10)]
    g0 = _IN_OFFS[4]
    c = w_ref[:, g0:g0 + LANES]
    lane = lax.broadcasted_iota(jnp.int32, c.shape, 1)
    left4, left8 = pltpu.roll(c, LANES - 4, 1), pltpu.roll(c, LANES - 8, 1)
    gi = jnp.where(lane < 4, c, jnp.where(lane < 8, left4, 0.0))
    gf = jnp.where(lane < 4, left4, jnp.where(lane < 8, left8, 0.0))
    wa_ref[...] = jnp.concatenate([mq, mv, mo, gf, nq, nk, nv, gm, gn], axis=1).astype(BF16)
    wt_ref[...] = jnp.concatenate([jnp.transpose(mk), jnp.transpose(gi), jnp.transpose(gf)],
                                  axis=0).astype(BF16)


def _arrange_w_in(w_in, b_mgate):
    d = w_in.shape[0]
    tr = 256
    wt_rows = M_HEADS * M_DQK + 2 * LANES
    w_all, wt_all = pl.pallas_call(
        _arrange_kernel,
        grid=(d // tr,),
        in_specs=[pl.BlockSpec((tr, w_in.shape[1]), lambda i: (i, 0))],
        out_specs=[pl.BlockSpec((tr, W_COLS), lambda i: (i, 0)),
                   pl.BlockSpec((wt_rows, tr), lambda i: (0, i))],
        out_shape=[jax.ShapeDtypeStruct((d, W_COLS), BF16),
                   jax.ShapeDtypeStruct((wt_rows, d), BF16)],
        compiler_params=_params("parallel"),
        name="arrange_w_in",
    )(w_in)
    bpad = jnp.zeros((LANES - 2 * M_HEADS,), F32)
    bi = jnp.concatenate([b_mgate[0:4], b_mgate[8:12], bpad])
    bf = jnp.concatenate([b_mgate[4:8], b_mgate[12:16], bpad])
    bg = jnp.concatenate([bf[None, :], jnp.zeros((7, LANES), F32)], axis=0)
    bgt = jnp.concatenate([bi, bf])[:, None]
    return w_all, wt_all, bg, bgt


def _segment_mats():
    na_w = NA_HEADS * NA_DH
    seg = np.zeros((na_w, LANES), np.float32)
    seg[np.arange(na_w), np.arange(na_w) // NA_DH] = 1.0
    return jnp.asarray(seg, BF16), jnp.asarray(seg.T.copy(), BF16)


def kernel(x, c, ctx, c_ctx, w_ada, b_ada, w_in, b_mgate, m_norm_w, na_qn_w, na_kn_w, na_rpb,
           w_br_m, w_br_na, w_out, w_router, router_bias, w_exp_gate, w_exp_up, w_exp_down,
           w_sh_gate, w_sh_up, w_sh_down):
    b, t, d = x.shape
    n = b * t
    rows = t // GRID_W
    l = 0

    cc = jnp.concatenate([c, c_ctx[None, :], jnp.zeros((8 - b - 1, d), F32)], axis=0)
    mod = _ada(cc, w_ada[l], b_ada[l])
    mod = mod.reshape(8, 6, d)
    mod = jnp.concatenate([mod, jnp.zeros((8, 2, d), F32)], axis=1)
    mod_x = mod[:b]
    mod_c = jnp.broadcast_to(mod[b:b + 1], (b, 8, d))

    w_all, wt_all, bg, bgt = _arrange_w_in(w_in[l], b_mgate[l])
    seg, segt = _segment_mats()
    qnw = jnp.tile(na_qn_w[l], NA_HEADS)[None, :]
    knw = jnp.tile(na_kn_w[l], NA_HEADS)[None, :]
    tm = min(512, t)

    cp = _inproj(ctx, mod_c, w_all, wt_all, bg, bgt, qnw, knw, seg, segt, None,
                 min(tm, ctx.shape[1]))
    xp = _inproj(x, mod_x, w_all, wt_all, bg, bgt, qnw, knw, seg, segt, _rope_tables(t, tm), tm)
    cmq, cmv, _, cgf, _, cnk, cnv, _, _, cmkt, cgit, cgft = cp
    mq, mv, mo, gf, nq, nk, nv, gm, gn, mkt, git, gft = xp

    c0 = jnp.zeros((b, 8, M_DQK, MLSTM_EXT), F32)
    m0 = jnp.zeros((b, 8, LANES), F32)
    _, _, c1, m1 = _mlstm(cmq, cmkt, cmv, cgf, cgit, cgft, c0, m0)
    hf, hb, _, _ = _mlstm(mq, mkt, mv, gf, git, gft, c1, m1)

    yna = _na(nq, nk, nv, cnk, cnv, _na_bias_table(na_rpb[l], rows))

    bias_col = jnp.broadcast_to(router_bias[l][:, None], (N_EXPERTS, LANES))
    h2p, base, top_e, top_w, rank, cnt = _post(
        x, mod_x, hf, hb, mo, yna, gm, gn, m_norm_w[l][None, :],
        w_br_m[l].astype(BF16), w_br_na[l].astype(BF16), w_out[l].astype(BF16),
        w_router[l].T.astype(BF16), w_sh_gate[l].astype(BF16), w_sh_up[l].astype(BF16),
        w_sh_down[l].astype(BF16), bias_col, tm)

    counts = cnt[:, 0].astype(jnp.int32)
    half = MOE_BLOCK // 2
    p_rows = (-(-(n * TOP_K) // half) + N_EXPERTS) * half
    nb_max = -(-(n * TOP_K) // MOE_BLOCK) + N_EXPERTS
    pstart, blk0, nblk, row0, is_half, n_used = _expert_plan(counts, nb_max)

    xs, pos = _dispatch_rows(h2p, top_e, rank, pstart, p_rows)
    ys = _experts(blk0, nblk, counts, row0, is_half, n_used, xs,
                  w_exp_gate[l], w_exp_up[l], w_exp_down[l])
    out = base
    for bi in range(b):
        idx = pos[:, bi * t:(bi + 1) * t].reshape(-1)
        yg = _gather_rows(ys, idx).reshape(TOP_K, t, d // 2)
        out = _combine(out, bi, mod_x, top_w, yg, tm)
    return out
```

```python
import functools

import numpy as np
import jax
import jax.numpy as jnp
from jax import lax
from jax.experimental import pallas as pl
from jax.experimental.pallas import tpu as pltpu
from jax.experimental.pallas import tpu_sc as plsc

F32 = jnp.float32
BF16 = jnp.bfloat16

EPS = 1e-6
GRID_W = 64
M_HEADS, M_DQK, M_DV = 4, 128, 256
ROPE_BASE = 10000.0
NA_HEADS, NA_DH, NA_KH, NA_KW = 8, 64, 8, 16
N_EXPERTS, TOP_K, N_GROUPS, TOPK_GROUPS = 256, 8, 8, 4
ROUTE_SCALE = 2.5

LANES = 128
VMEM_LIMIT = 56 * 1024 * 1024
NEG = -1e30
LOG2E = 1.4426950408889634

MLSTM_CHUNK = 256
NA_ROWS = 4
NA_KEY_ROWS = NA_ROWS + NA_KH - 1
NA_SUB = 4
MOE_BLOCK = 512
EXPERT_SLOTS = 4

_W_SEGS = (("mq", 512), ("mv", 1024), ("mo", 1024), ("gf", 128),
           ("nq", 512), ("nk", 512), ("nv", 512), ("gm", 1024), ("gn", 1024))
_W_OFF = {}
_o = 0
for _n, _w in _W_SEGS:
    _W_OFF[_n] = (_o, _w)
    _o += _w
W_COLS = _o


def _dot(a, b):
    return jnp.dot(a, b, preferred_element_type=F32)


def _dot_nt(a, b):
    return lax.dot_general(a, b, (((1,), (1,)), ((), ())), preferred_element_type=F32)


def _sigmoid(x):
    return 1.0 / (1.0 + jnp.exp(-x))


def _pack_bf16_pairs(v):
    w = v.shape[1] // 2
    bits = pltpu.bitcast(v.astype(BF16).astype(F32), jnp.int32)
    return lax.shift_right_logical(bits[:, :w], 16) | bits[:, w:]


def _unpack_bf16_pairs(p):
    lo = pltpu.bitcast(lax.shift_left(p, 16), F32)
    hi = pltpu.bitcast(p & jnp.int32(-65536), F32)
    return jnp.concatenate([lo, hi], axis=1)


def _params(*sem):
    return pltpu.CompilerParams(dimension_semantics=sem, vmem_limit_bytes=VMEM_LIMIT)


def _resident(shape):
    nd = len(shape)
    return pl.BlockSpec(shape, lambda *_: (0,) * nd, pipeline_mode=pl.Buffered(1))


def _ada_kernel(c_ref, w_ref, b_ref, o_ref):
    c = c_ref[...]
    s = c * _sigmoid(c)
    o_ref[...] = _dot(s.astype(BF16), w_ref[...].astype(BF16)) + b_ref[...]


def _ada(cc, w_ada, b_ada):
    d = cc.shape[1]
    n = w_ada.shape[1]
    return pl.pallas_call(
        _ada_kernel,
        grid=(n // d,),
        in_specs=[pl.BlockSpec((8, d), lambda j: (0, 0)),
                  pl.BlockSpec((d, d), lambda j: (0, j)),
                  pl.BlockSpec((1, d), lambda j: (0, j))],
        out_specs=pl.BlockSpec((8, d), lambda j: (0, j)),
        out_shape=jax.ShapeDtypeStruct((8, n), F32),
        compiler_params=_params("arbitrary"),
        name="ada",
    )(cc, w_ada, b_ada.reshape(1, n))


def _rope_rotate(t, cos, sin):
    q = M_DQK // 4
    lane = lax.broadcasted_iota(jnp.int32, t.shape, 1)
    partner = jnp.where((lane & q) == 0, pltpu.roll(t, M_DQK - q, 1), pltpu.roll(t, q, 1))
    return t * cos + partner * sin


def _rope_rotate_t(t, cos, sin):
    q = M_DQK // 4
    partner = jnp.concatenate([t[q:2 * q], t[0:q], t[3 * q:4 * q], t[2 * q:3 * q]], axis=0)
    return t * cos + partner * sin


def _inproj_kernel(*refs, rope):
    if rope:
        (x_ref, mod_ref, w_ref, wt_ref, bg_ref, bgt_ref, qnw_ref, knw_ref, seg_ref, segt_ref,
         rcos_ref, rsin_ref, ccos_ref, csin_ref,
         mq_ref, mv_ref, mo_ref, gf_ref, nq_ref, nk_ref, nv_ref, gm_ref, gn_ref,
         mkt_ref, git_ref, gft_ref) = refs
    else:
        (x_ref, mod_ref, w_ref, wt_ref, bg_ref, bgt_ref, qnw_ref, knw_ref, seg_ref, segt_ref,
         mq_ref, mv_ref, mo_ref, gf_ref, nq_ref, nk_ref, nv_ref, gm_ref, gn_ref,
         mkt_ref, git_ref, gft_ref) = refs
    x = x_ref[0]
    xn = x * lax.rsqrt(jnp.mean(x * x, axis=-1, keepdims=True) + EPS)
    h = xn * (1.0 + mod_ref[0, 1:2, :]) + mod_ref[0, 0:1, :]
    hb = h.astype(BF16)

    def proj(name):
        off, width = _W_OFF[name]
        return _dot(hb, w_ref[:, off:off + width])

    def head_rms(t, w_row, scale):
        ss = _dot((t * t).astype(BF16), seg_ref[...])
        r = lax.rsqrt(ss * (1.0 / NA_DH) + EPS)
        r_hi = r.astype(BF16)
        r_lo = (r - r_hi.astype(F32)).astype(BF16)
        rb = _dot(r_hi, segt_ref[...]) + _dot(r_lo, segt_ref[...])
        return t * rb * w_row * scale

    mq = proj("mq") * (M_DQK ** -0.5)
    if rope:
        tm = x.shape[0]
        spread = lambda r: jnp.broadcast_to(r[:, None, :], (tm // GRID_W, GRID_W, LANES)).reshape(tm, LANES)
        cos = spread(rcos_ref[...]) + ccos_ref[...]
        sin = spread(rsin_ref[...]) + csin_ref[...]
        mq = jnp.concatenate([_rope_rotate(mq[:, i * LANES:(i + 1) * LANES], cos, sin)
                              for i in range(M_HEADS)], axis=1)
    mq_ref[0] = mq.astype(BF16)
    mv_ref[0] = proj("mv").astype(BF16)
    mo_ref[0] = _sigmoid(proj("mo")).astype(BF16)
    gf_ref[0] = proj("gf") + bg_ref[0:1, :]
    nq_ref[0] = head_rms(proj("nq"), qnw_ref[...], NA_DH ** -0.5 * LOG2E).astype(BF16)
    nk_ref[0] = head_rms(proj("nk"), knw_ref[...], 1.0).astype(BF16)
    nv_ref[0] = proj("nv").astype(BF16)
    gm_ref[0] = _sigmoid(proj("gm")).astype(BF16)
    gn_ref[0] = _sigmoid(proj("gn")).astype(BF16)

    qk_w = M_HEADS * M_DQK
    mkt = _dot_nt(wt_ref[0:qk_w, :], hb)
    if rope:
        cost, sint = jnp.transpose(cos), jnp.transpose(sin)
        mkt = jnp.concatenate([_rope_rotate_t(mkt[i * M_DQK:(i + 1) * M_DQK], cost, sint)
                               for i in range(M_HEADS)], axis=0)
    mkt_ref[0] = mkt.astype(BF16)
    git_ref[0] = _dot_nt(wt_ref[qk_w:qk_w + LANES, :], hb) + bgt_ref[0:LANES, :]
    gft_ref[0] = _dot_nt(wt_ref[qk_w + LANES:qk_w + 2 * LANES, :], hb) + bgt_ref[LANES:2 * LANES, :]


def _inproj(x, mod, w_all, wt_all, bg, bgt, qnw, knw, seg, segt, rope_tabs, tm):
    b, t, d = x.shape
    rope = rope_tabs is not None
    tok = lambda w: pl.BlockSpec((1, tm, w), lambda bi, i: (bi, i, 0))
    tok_t = lambda w: pl.BlockSpec((1, w, tm), lambda bi, i: (bi, 0, i))
    in_specs = [tok(d),
                pl.BlockSpec((1, 8, d), lambda bi, i: (bi, 0, 0)),
                _resident(w_all.shape), _resident(wt_all.shape), _resident(bg.shape),
                _resident(bgt.shape), _resident(qnw.shape),
                _resident(knw.shape), _resident(seg.shape), _resident(segt.shape)]
    args = [x, mod, w_all, wt_all, bg, bgt, qnw, knw, seg, segt]
    if rope:
        in_specs += [pl.BlockSpec((tm // GRID_W, LANES), lambda bi, i: (i, 0))] * 2
        in_specs += [_resident((tm, LANES))] * 2
        args += list(rope_tabs)
    widths = [("mq", BF16), ("mv", BF16), ("mo", BF16), ("gf", F32),
              ("nq", BF16), ("nk", BF16), ("nv", BF16), ("gm", BF16), ("gn", BF16)]
    out_specs = [tok(_W_OFF[n][1]) for n, _ in widths]
    out_shape = [jax.ShapeDtypeStruct((b, t, _W_OFF[n][1]), dt) for n, dt in widths]
    out_specs += [tok_t(M_HEADS * M_DQK), tok_t(LANES), tok_t(LANES)]
    out_shape += [jax.ShapeDtypeStruct((b, M_HEADS * M_DQK, t), BF16),
                  jax.ShapeDtypeStruct((b, LANES, t), F32),
                  jax.ShapeDtypeStruct((b, LANES, t), F32)]
    return pl.pallas_call(
        functools.partial(_inproj_kernel, rope=rope),
        grid=(b, t // tm),
        in_specs=in_specs, out_specs=out_specs, out_shape=out_shape,
        compiler_params=_params("parallel", "parallel"),
        name="inproj_rope" if rope else "inproj_ctx",
    )(*args)


def _log_sigmoid(x):
    return jnp.minimum(x, 0.0) - jnp.log(1.0 + jnp.exp(-jnp.abs(x)))


def _dot_split(a, b, split_a):
    x = a if split_a else b
    hi = x.astype(BF16)
    lo = (x - hi.astype(F32)).astype(BF16)
    return (_dot(hi, b) + _dot(lo, b)) if split_a else (_dot(a, hi) + _dot(a, lo))


MLSTM_EXT = M_DV + LANES


def _mlstm_kernel(qf_ref, ktf_ref, vf_ref, gff_ref, gitf_ref, gftf_ref,
                  qb_ref, ktb_ref, vb_ref, gfb_ref, gitb_ref, gftb_ref,
                  c0_ref, m0_ref,
                  hf_ref, hb_ref, cn_ref, mn_ref,
                  *scratch):
    c_scrs, m_scr = scratch[:2 * M_HEADS], scratch[2 * M_HEADS]
    step = pl.program_id(1)
    L = qf_ref.shape[1]
    nu = 2 * M_HEADS

    @pl.when(step == 0)
    def _():
        for j, c_scr in enumerate(c_scrs):
            c_scr[...] = c0_ref[0, j]
        m_scr[...] = m0_ref[0]

    row_i = lax.broadcasted_iota(jnp.int32, (L, L), 0)
    col_i = lax.broadcasted_iota(jnp.int32, (L, L), 1)
    lower = col_i <= row_i
    upper = col_i >= row_i
    tri_lo = jnp.where(lower, 1.0, 0.0).astype(BF16)
    tri_up = jnp.where(upper, 1.0, 0.0).astype(BF16)

    is_f = lax.broadcasted_iota(jnp.int32, (nu, L), 0) < M_HEADS
    gi_t = jnp.where(is_f, gitf_ref[0, 0:nu, :], gitb_ref[0, 0:nu, :]) * LOG2E
    ls_tf = _log_sigmoid(gftf_ref[0, 0:nu, :]) * LOG2E
    ls_tb = _log_sigmoid(gftb_ref[0, 0:nu, :]) * LOG2E
    b_t = jnp.where(is_f, _dot_split(ls_tf, tri_up, True), _dot_split(ls_tb, tri_lo, True))
    u_t = gi_t - b_t
    g_c = jnp.sum(jnp.where(is_f, ls_tf, ls_tb), axis=1, keepdims=True)
    m_prev = m_scr[...]
    a_t = g_c + u_t
    m_new = jnp.maximum(g_c + m_prev, jnp.max(a_t, axis=1, keepdims=True))
    decay = jnp.exp2(g_c + m_prev - m_new)
    wa_t = jnp.exp2(a_t - jnp.concatenate([m_new] * (L // LANES), axis=1))
    m_scr[...] = m_new

    ones = jnp.ones((L, LANES), BF16)
    dirs = ((qf_ref, ktf_ref, vf_ref, gff_ref, hf_ref, lower, tri_lo),
            (qb_ref, ktb_ref, vb_ref, gfb_ref, hb_ref, upper, tri_up))
    for d, (q_ref, kt_ref, v_ref, gf_ref, h_ref, mask, tri) in enumerate(dirs):
        bcum = _dot_split(tri, _log_sigmoid(gf_ref[0]) * LOG2E, False)
        for hd in range(M_HEADS):
            j = d * M_HEADS + hd
            c_scr = c_scrs[j]
            q = q_ref[0, :, hd * M_DQK:(hd + 1) * M_DQK]
            k_t = kt_ref[0, hd * M_DQK:(hd + 1) * M_DQK, :]
            v_ext = jnp.concatenate([v_ref[0, :, hd * M_DV:(hd + 1) * M_DV], ones], axis=1)
            u_row = u_t[j:j + 1, :]
            mp_row = m_prev[j:j + 1, :]
            c_prev = c_scr[...]

            m_loc = jnp.max(jnp.where(mask, u_row, NEG), axis=1, keepdims=True)
            m_rep = jnp.maximum(jnp.broadcast_to(m_loc, (L, LANES)), mp_row)
            m_wide = jnp.concatenate([m_rep] * (L // LANES), axis=1)
            dmat = jnp.exp2(jnp.where(mask, u_row - m_wide, NEG))
            s = (_dot(q, k_t) * dmat).astype(BF16)
            qw = (q.astype(F32) * jnp.exp2(mp_row - m_rep)).astype(BF16)
            r = _dot(s, v_ext) + _dot(qw, c_prev.astype(BF16))
            b_rep = jnp.broadcast_to(bcum[:, j:j + 1], (L, LANES))
            dn = jnp.maximum(jnp.abs(r[:, M_DV:]), jnp.exp2(-(b_rep + m_rep)))
            h_ref[0, :, hd * M_DV:(hd + 1) * M_DV] = (
                r[:, :M_DV] / jnp.concatenate([dn] * (M_DV // LANES), axis=1)).astype(h_ref.dtype)

            kw = (k_t.astype(F32) * wa_t[j:j + 1, :]).astype(BF16)
            dec = jnp.concatenate([decay[j:j + 1, :]] * (MLSTM_EXT // LANES), axis=1)
            c_scr[...] = dec * c_prev + _dot(kw, v_ext)

    @pl.when(step == pl.num_programs(1) - 1)
    def _():
        for j, c_scr in enumerate(c_scrs):
            cn_ref[0, j] = c_scr[...]
        mn_ref[0] = m_scr[...]


def _mlstm(q, kt, v, gf, git, gft, c0, m0):
    b, t, _ = q.shape
    L = min(MLSTM_CHUNK, t)
    nc = t // L
    fwd = lambda w: pl.BlockSpec((1, L, w), lambda bi, i: (bi, i, 0))
    bwd = lambda w: pl.BlockSpec((1, L, w), lambda bi, i: (bi, nc - 1 - i, 0))
    fwd_t = lambda w: pl.BlockSpec((1, w, L), lambda bi, i: (bi, 0, i))
    bwd_t = lambda w: pl.BlockSpec((1, w, L), lambda bi, i: (bi, 0, nc - 1 - i))
    st_c = pl.BlockSpec((1, 8, M_DQK, MLSTM_EXT), lambda bi, i: (bi, 0, 0, 0))
    st_v = pl.BlockSpec((1, 8, LANES), lambda bi, i: (bi, 0, 0))
    qk_w, v_w = M_HEADS * M_DQK, M_HEADS * M_DV
    return pl.pallas_call(
        _mlstm_kernel,
        grid=(b, nc),
        in_specs=[fwd(qk_w), fwd_t(qk_w), fwd(v_w), fwd(LANES), fwd_t(LANES), fwd_t(LANES),
                  bwd(qk_w), bwd_t(qk_w), bwd(v_w), bwd(LANES), bwd_t(LANES), bwd_t(LANES),
                  st_c, st_v],
        out_specs=[fwd(v_w), bwd(v_w), st_c, st_v],
        out_shape=[jax.ShapeDtypeStruct((b, t, v_w), BF16),
                   jax.ShapeDtypeStruct((b, t, v_w), BF16),
                   jax.ShapeDtypeStruct(c0.shape, F32),
                   jax.ShapeDtypeStruct(m0.shape, F32)],
        scratch_shapes=([pltpu.VMEM((M_DQK, MLSTM_EXT), F32) for _ in range(2 * M_HEADS)]
                        + [pltpu.VMEM((8, LANES), F32)]),
        compiler_params=_params("parallel", "arbitrary"),
        name="mlstm",
    )(q, kt, v, gf, git, gft, q, kt, v, gf, git, gft, c0, m0)


def _na_kernel(q_ref, k_ref, v_ref, kc_ref, vc_ref, *rest, rows):
    bias_refs, o_ref = rest[:NA_SUB], rest[NA_SUB]
    tq = NA_ROWS * GRID_W
    nkeys = NA_KEY_ROWS * GRID_W
    kc = kc_ref[0]
    vc = vc_ref[0]
    for sb in range(NA_SUB):
        r0 = (pl.program_id(2) * NA_SUB + sb) * NA_ROWS
        ks = jnp.clip(r0 - NA_KH // 2, 0, rows - NA_KEY_ROWS)
        kstart = pl.multiple_of(ks * GRID_W, GRID_W)
        kblk = k_ref[0, pl.ds(kstart, nkeys), :]
        vblk = v_ref[0, pl.ds(kstart, nkeys), :]
        q = q_ref[0, sb * tq:(sb + 1) * tq, :]
        lane = lax.broadcasted_iota(jnp.int32, q.shape, 1)
        outs = []
        for hh in range(2):
            in_head = (lane < NA_DH) if hh == 0 else (lane >= NA_DH)
            qm = jnp.where(in_head, q, jnp.zeros_like(q))
            sw = _dot_nt(qm, kblk) + bias_refs[sb][hh, 0]
            sc = _dot_nt(qm, kc)
            m = jnp.maximum(jnp.max(sw, axis=1, keepdims=True), jnp.max(sc, axis=1, keepdims=True))
            ew = jnp.exp2(sw - m)
            ec = jnp.exp2(sc - m)
            l = jnp.sum(ew, axis=1, keepdims=True) + jnp.sum(ec, axis=1, keepdims=True)
            o = _dot(ew.astype(BF16), vblk) + _dot(ec.astype(BF16), vc)
            outs.append(o / l)
        o_ref[0, sb * tq:(sb + 1) * tq, :] = jnp.where(lane < NA_DH, outs[0], outs[1]).astype(o_ref.dtype)


def _na(nq, nk, nv, cnk, cnv, bias):
    b, t, w = nq.shape
    rows = t // GRID_W
    tq = NA_ROWS * GRID_W
    nrb = rows // NA_ROWS
    nctx = cnk.shape[1]
    kind = lambda rb: jnp.where(rb == 0, 0, jnp.where(rb == nrb - 1, 2, 1))
    bias_spec = lambda sb: pl.BlockSpec((2, 1, tq, NA_KEY_ROWS * GRID_W),
                                        lambda bi, hp, st: (hp, kind(st * NA_SUB + sb), 0, 0))
    return pl.pallas_call(
        functools.partial(_na_kernel, rows=rows),
        grid=(b, w // LANES, nrb // NA_SUB),
        in_specs=[pl.BlockSpec((1, NA_SUB * tq, LANES), lambda bi, hp, st: (bi, st, hp)),
                  pl.BlockSpec((1, t, LANES), lambda bi, hp, st: (bi, 0, hp)),
                  pl.BlockSpec((1, t, LANES), lambda bi, hp, st: (bi, 0, hp)),
                  pl.BlockSpec((1, nctx, LANES), lambda bi, hp, st: (bi, 0, hp)),
                  pl.BlockSpec((1, nctx, LANES), lambda bi, hp, st: (bi, 0, hp))]
                 + [bias_spec(sb) for sb in range(NA_SUB)],
        out_specs=pl.BlockSpec((1, NA_SUB * tq, LANES), lambda bi, hp, st: (bi, st, hp)),
        out_shape=jax.ShapeDtypeStruct((b, t, w), BF16),
        compiler_params=_params("parallel", "parallel", "arbitrary"),
        name="na",
    )(nq, nk, nv, cnk, cnv, *([bias] * NA_SUB))


def _na_bias_table(na_rpb, rows):
    h = na_rpb.shape[0]
    w = GRID_W
    c = np.arange(w)[:, None]
    kj = np.arange(w)[None, :]
    cs = np.clip(c - NA_KW // 2, 0, w - NA_KW)
    col_valid = (kj >= cs) & (kj < cs + NA_KW)
    dc = np.clip(kj - c + (NA_KW - 1), 0, 2 * NA_KW - 2)
    onehot = np.zeros((2 * NA_KW - 1, w, w), np.float32)
    onehot[dc, np.arange(w)[:, None], np.arange(w)[None, :]] = 1.0
    t2 = jnp.einsum("hrd,dck->hrck", na_rpb, jnp.asarray(onehot), precision=lax.Precision.HIGHEST)
    t2 = jnp.where(jnp.asarray(col_valid)[None, None], t2 * LOG2E, NEG)
    t2 = jnp.concatenate([t2, jnp.full((h, 1, w, w), NEG, F32)], axis=1)
    invalid = 2 * NA_KH - 1
    dr_idx = np.full((3, NA_ROWS, NA_KEY_ROWS), invalid, np.int32)
    for kind, r0 in enumerate((0, NA_ROWS, rows - NA_ROWS)):
        ks = int(np.clip(r0 - NA_KH // 2, 0, rows - NA_KEY_ROWS))
        for qa in range(NA_ROWS):
            r = r0 + qa
            rs = int(np.clip(r - NA_KH // 2, 0, rows - NA_KH))
            for kl in range(NA_KEY_ROWS):
                ki = ks + kl
                if rs <= ki < rs + NA_KH:
                    dr_idx[kind, qa, kl] = ki - r + NA_KH - 1
    t2t = t2.transpose(0, 2, 1, 3)
    strips = [jnp.concatenate([t2t[:, :, int(dr), :] for dr in dr_idx[kind, qa]], axis=-1)
              for kind in range(3) for qa in range(NA_ROWS)]
    return jnp.stack(strips, axis=1).reshape(h, 3, NA_ROWS * w, NA_KEY_ROWS * w)


def _post_kernel(x_ref, mod_ref, hf_ref, hb_ref, mo_ref, na_ref, gm_ref, gn_ref,
                 mnw_ref, wbm_ref, wbn_ref, wout_ref, wr_ref, wsg_ref, wsu_ref, wsd_ref, rb_ref,
                 h2_ref, base_ref, e_ref, w_ref, r_ref, cnt_ref, run_scr):
    @pl.when(pl.program_id(0) == 0)
    def _():
        run_scr[...] = jnp.zeros_like(run_scr)

    hm = hf_ref[0].astype(F32) + hb_ref[0].astype(F32)
    parts = []
    for hd in range(M_HEADS):
        t = hm[:, hd * M_DV:(hd + 1) * M_DV]
        parts.append(t * lax.rsqrt(jnp.mean(t * t, axis=-1, keepdims=True) + EPS))
    y_m = jnp.concatenate(parts, axis=1) * mnw_ref[...] * mo_ref[0].astype(F32)
    a = _dot(y_m.astype(BF16), wbm_ref[...])
    bn = _dot(na_ref[0], wbn_ref[...])
    z = gm_ref[0].astype(F32) * a + gn_ref[0].astype(F32) * bn
    y = _dot(z.astype(BF16), wout_ref[...])
    x1 = x_ref[0] + mod_ref[0, 2:3, :] * y
    xn = x1 * lax.rsqrt(jnp.mean(x1 * x1, axis=-1, keepdims=True) + EPS)
    h2f = xn * (1.0 + mod_ref[0, 4:5, :]) + mod_ref[0, 3:4, :]
    h2_ref[...] = _pack_bf16_pairs(h2f)
    h2 = h2f.astype(BF16)
    sh = _dot(h2, wsg_ref[...])
    sh = sh * _sigmoid(sh) * _dot(h2, wsu_ref[...])
    base_ref[0] = x1 + mod_ref[0, 5:6, :] * _dot(sh.astype(BF16), wsd_ref[...])
    scores = _sigmoid(_dot_nt(wr_ref[...], h2))
    _route_block(scores, rb_ref, e_ref, w_ref, r_ref, cnt_ref, run_scr)


def _post(x, mod, hf, hb, mo, yna, gm, gn, mnw, wbm, wbn, wout, wr_t, wsg, wsu, wsd, rbias, tm):
    b, t, d = x.shape
    nt = t // tm
    n = b * t
    tok = lambda w: pl.BlockSpec((1, tm, w), lambda s: (s // nt, s % nt, 0))
    rt = lambda: pl.BlockSpec((TOP_K, tm), lambda s: (0, s))
    res = [mnw, wbm, wbn, wout, wr_t, wsg, wsu, wsd, rbias]
    return pl.pallas_call(
        _post_kernel,
        grid=(b * nt,),
        in_specs=[tok(d), pl.BlockSpec((1, 8, d), lambda s: (s // nt, 0, 0)),
                  tok(hf.shape[2]), tok(hb.shape[2]), tok(mo.shape[2]), tok(yna.shape[2]),
                  tok(gm.shape[2]), tok(gn.shape[2])] + [_resident(a.shape) for a in res],
        out_specs=[pl.BlockSpec((tm, d // 2), lambda s: (s, 0)),
                   tok(d), rt(), pl.BlockSpec((tm, TOP_K), lambda s: (s, 0)), rt(),
                   pl.BlockSpec((N_EXPERTS, LANES), lambda s: (0, 0))],
        out_shape=[jax.ShapeDtypeStruct((n, d // 2), jnp.int32),
                   jax.ShapeDtypeStruct((b, t, d), F32),
                   jax.ShapeDtypeStruct((TOP_K, n), jnp.int32),
                   jax.ShapeDtypeStruct((n, TOP_K), F32),
                   jax.ShapeDtypeStruct((TOP_K, n), jnp.int32),
                   jax.ShapeDtypeStruct((N_EXPERTS, LANES), F32)],
        scratch_shapes=[pltpu.VMEM((N_EXPERTS, LANES), F32)],
        compiler_params=_params("arbitrary"),
        name="post",
    )(x, mod, hf, hb, mo, yna, gm, gn, *res)


def _route_block(s, b_ref, e_ref, w_ref, r_ref, cnt_ref, run_scr):
    tm = s.shape[1]
    sel = s + b_ref[...][:, 0:1]
    gsz = N_EXPERTS // N_GROUPS
    ninf = -jnp.inf

    x3 = sel.reshape(N_GROUPS, gsz, tm)
    r3 = lax.broadcasted_iota(jnp.int32, x3.shape, 1)
    m1 = jnp.max(x3, axis=1, keepdims=True)
    i1 = jnp.min(jnp.where(x3 == m1, r3, gsz), axis=1, keepdims=True)
    m2 = jnp.max(jnp.where(r3 == i1, ninf, x3), axis=1)
    gs = m1[:, 0, :] + m2

    gidx = lax.broadcasted_iota(jnp.int32, gs.shape, 0)
    gkeep = jnp.zeros(gs.shape, jnp.bool_)
    cur = gs
    for _ in range(TOPK_GROUPS):
        mm = jnp.max(cur, axis=0, keepdims=True)
        ii = jnp.min(jnp.where(cur == mm, gidx, N_GROUPS), axis=0, keepdims=True)
        hit = gidx == ii
        gkeep = jnp.logical_or(gkeep, hit)
        cur = jnp.where(hit, ninf, cur)
    keep = jnp.broadcast_to(gkeep[:, None, :], x3.shape).reshape(N_EXPERTS, tm)

    row = lax.broadcasted_iota(jnp.int32, s.shape, 0).astype(F32)
    cur = jnp.where(keep, sel, ninf)
    idxs, ws = [], []
    chosen_f = jnp.zeros(s.shape, F32)
    for _ in range(TOP_K):
        mm = jnp.max(cur, axis=0, keepdims=True)
        ii = jnp.min(jnp.where(cur == mm, row, float(N_EXPERTS)), axis=0, keepdims=True)
        hit = row == ii
        idxs.append(ii)
        ws.append(jnp.sum(jnp.where(hit, s, 0.0), axis=0, keepdims=True))
        chosen_f = jnp.where(hit, 1.0, chosen_f)
        cur = jnp.where(hit, ninf, cur)
    wsum = ws[0]
    for wk in ws[1:]:
        wsum = wsum + wk

    tp =lax.broadcasted_iota(jnp.int32, (tm, tm), 0)
    tc = lax.broadcasted_iota(jnp.int32, (tm, tm), 1)
    before = jnp.where(tp < tc, 1.0, 0.0).astype(BF16)
    rank = _dot(chosen_f.astype(BF16), before) + run_scr[...][:, 0:1]
    run_scr[...] = run_scr[...] + jnp.sum(chosen_f, axis=1, keepdims=True)
    cnt_ref[...] = run_scr[...]

    for kk in range(TOP_K):
        e_ref[kk:kk + 1, :] = idxs[kk].astype(jnp.int32)
        r_ref[kk:kk + 1, :] = jnp.sum(jnp.where(row == idxs[kk], rank, 0.0), axis=0,
                                      keepdims=True).astype(jnp.int32)
    w_ref[...] = jnp.transpose(jnp.concatenate([wk / wsum * ROUTE_SCALE for wk in ws], axis=0))


SC_WINDOW = 128


def _sc_mesh():
    return plsc.VectorSubcoreMesh(core_axis_name="core", subcore_axis_name="subcore")


def _sc_workers():
    info = plsc.get_sparse_core_info()
    return info.num_cores, info.num_cores * info.num_subcores


def _dispatch_rows(x, top_e, rank, pstart, p_rows):
    n, w = x.shape
    kk = top_e.shape[0]
    ncores, nw = _sc_workers()
    lanes = plsc.get_sparse_core_info().num_lanes
    steps = n // nw // SC_WINDOW
    per_worker = lambda a: a.reshape(kk, nw, steps, SC_WINDOW).transpose(1, 2, 0, 3)

    @functools.partial(
        pl.kernel, mesh=_sc_mesh(),
        out_type=[jax.ShapeDtypeStruct((p_rows, w), x.dtype),
                  jax.ShapeDtypeStruct((nw, steps, kk, SC_WINDOW), jnp.int32)],
        scratch_types=[pltpu.VMEM((kk, SC_WINDOW), jnp.int32),
                       pltpu.VMEM((kk, SC_WINDOW), jnp.int32),
                       pltpu.VMEM((kk, SC_WINDOW), jnp.int32),
                       pltpu.VMEM(pstart.shape, jnp.int32),
                       pltpu.VMEM((SC_WINDOW, w), x.dtype),
                       pltpu.SemaphoreType.DMA],
        compiler_params=pltpu.CompilerParams(needs_layout_passes=False),
    )
    def scatter(x_hbm, e_hbm, r_hbm, ps_hbm, o_hbm, pos_hbm, e_v, r_v, pos_v, ps_v, rows_v, sem):
        wid = lax.axis_index("subcore") * ncores + lax.axis_index("core")
        pltpu.sync_copy(ps_hbm, ps_v)

        @pl.loop(0, steps)
        def _(s):
            base = pl.multiple_of((wid * steps + s) * SC_WINDOW, SC_WINDOW)
            pltpu.sync_copy(e_hbm.at[wid, s], e_v)
            pltpu.sync_copy(r_hbm.at[wid, s], r_v)
            pltpu.sync_copy(x_hbm.at[pl.ds(base, SC_WINDOW)], rows_v)
            for j in range(kk):
                for c in range(SC_WINDOW // lanes):
                    cols = pl.ds(c * lanes, lanes)
                    pos_v[j, cols] = plsc.load_gather(ps_v, [e_v[j, cols]]) + r_v[j, cols]
            pltpu.sync_copy(pos_v, pos_hbm.at[wid, s])
            copies = [pltpu.make_async_copy(rows_v, o_hbm.at[pos_v.at[j]], sem) for j in range(kk)]
            for cp in copies:
                cp.start()
            for cp in copies:
                cp.wait()

    out, pos4 = scatter(x, per_worker(top_e), per_worker(rank), pstart)
    return out, pos4.transpose(2, 0, 1, 3).reshape(kk, n)


def _gather_rows(x, idx):
    m = idx.shape[0]
    w = x.shape[1]
    ncores, nw = _sc_workers()
    steps = m // nw // SC_WINDOW
    idx3 = idx.reshape(nw, steps, SC_WINDOW)

    @functools.partial(
        pl.kernel, mesh=_sc_mesh(),
        out_type=jax.ShapeDtypeStruct((m, w), x.dtype),
        scratch_types=[pltpu.VMEM((steps, SC_WINDOW), jnp.int32),
                       pltpu.VMEM((SC_WINDOW, w), x.dtype),
                       pltpu.SemaphoreType.DMA],
    )
    def gather(x_hbm, i_hbm, o_hbm, idx_v, rows_v, sem):
        wid = lax.axis_index("subcore") * ncores + lax.axis_index("core")
        pltpu.sync_copy(i_hbm.at[wid], idx_v)

        @pl.loop(0, steps)
        def _(s):
            pltpu.async_copy(x_hbm.at[idx_v.at[s]], rows_v, sem).wait()
            base = pl.multiple_of((wid * steps + s) * SC_WINDOW, SC_WINDOW)
            pltpu.sync_copy(rows_v, o_hbm.at[pl.ds(base, SC_WINDOW)])

    return gather(x, idx3)


def _experts_kernel(blk0_ref, nblk_ref, cnt_ref, row0_ref, half_ref, nu_ref,
                    x_hbm, wg_ref, wu_ref, wd_ref, y_hbm,
                    xbuf, ybuf, wg_scr, wu_scr, wd_scr, in_sem, out_sem):
    e = pl.program_id(0)
    n_used = nu_ref[0]
    blk0 = blk0_ref[e]
    ns = EXPERT_SLOTS
    sizes = (MOE_BLOCK, MOE_BLOCK // 2)

    def x_copy(g, rows):
        r0 = pl.multiple_of(row0_ref[g], MOE_BLOCK // 2)
        return pltpu.make_async_copy(x_hbm.at[pl.ds(r0, rows)], xbuf.at[g % ns, pl.ds(0, rows)],
                                     in_sem.at[g % ns])

    def y_copy(g, rows):
        r0 = pl.multiple_of(row0_ref[g], MOE_BLOCK // 2)
        return pltpu.make_async_copy(ybuf.at[g % ns, pl.ds(0, rows)], y_hbm.at[pl.ds(r0, rows)],
                                     out_sem.at[g % ns])

    def by_size(g, fn):
        for is_half, rows in enumerate(sizes):
            @pl.when(half_ref[g] == is_half)
            def _():
                fn(rows)

    @pl.when(e == 0)
    def _():
        for g0 in range(ns - 1):
            @pl.when(g0 < n_used)
            def _():
                by_size(g0, lambda rows: x_copy(g0, rows).start())

    wg_scr[...] = wg_ref[0].astype(BF16)
    wu_scr[...] = wu_ref[0].astype(BF16)
    wd_scr[...] = wd_ref[0].astype(BF16)

    def block(b, carry):
        g = blk0 + b
        by_size(g, lambda rows: x_copy(g, rows).wait())

        @pl.when(g + ns - 1 < n_used)
        def _():
            by_size(g + ns - 1, lambda rows: x_copy(g + ns - 1, rows).start())

        @pl.when(g >= ns)
        def _():
            by_size(g - ns, lambda rows: y_copy(g - ns, rows).wait())

        def run(rows):
            rid = lax.broadcasted_iota(jnp.int32, (rows, xbuf.shape[2]), 0)
            xp = jnp.where(rid < cnt_ref[e] - b * MOE_BLOCK, xbuf[g % ns, 0:rows], 0)
            x = _unpack_bf16_pairs(xp).astype(BF16)
            gt = _dot(x, wg_scr[...])
            up = _dot(x, wu_scr[...])
            a = (gt * _sigmoid(gt) * up).astype(BF16)
            ybuf[g % ns, 0:rows] = _pack_bf16_pairs(_dot(a, wd_scr[...]))
            y_copy(g, rows).start()

        by_size(g, run)
        return carry

    lax.fori_loop(0, nblk_ref[e], block, 0)

    @pl.when(e == pl.num_programs(0) - 1)
    def _():
        for back in range(ns, 0, -1):
            @pl.when(n_used >= back)
            def _():
                by_size(n_used - back, lambda rows: y_copy(n_used - back, rows).wait())


def _expert_plan(counts, nb_max):
    half = MOE_BLOCK // 2
    units = (counts + half - 1) // half
    nfull, tail = units // 2, units % 2
    nblk = nfull + tail
    pend = jnp.cumsum(units * half)
    pstart = pend - units * half
    blk_end = jnp.cumsum(nblk)
    blk0 = blk_end - nblk
    g = jnp.arange(nb_max, dtype=jnp.int32)
    ne = counts.shape[0]
    eg = jnp.minimum(jnp.sum((blk_end[None, :] <= g[:, None]).astype(jnp.int32), axis=1), ne - 1)
    onehot = (eg[:, None] == jnp.arange(ne, dtype=jnp.int32)[None, :]).astype(jnp.int32)
    pick = lambda v: jnp.sum(onehot * v[None, :], axis=1)
    local = g - pick(blk0)
    is_half = ((local == pick(nfull)) & (pick(tail) == 1)).astype(jnp.int32)
    row0 = jnp.clip(pick(pstart) + local * MOE_BLOCK, 0, pend[-1] - half)
    return pstart, blk0, nblk, row0, is_half, blk_end[-1:]


def _experts(blk0, nblk, counts, row0, is_half, n_used, xs, wg, wu, wd):
    p, dp = xs.shape
    ne, d, ff = wg.shape
    grid_spec = pltpu.PrefetchScalarGridSpec(
        num_scalar_prefetch=6,
        grid=(ne,),
        in_specs=[pl.BlockSpec(memory_space=pl.ANY),
                  pl.BlockSpec((1, d, ff), lambda e, *_: (e, 0, 0)),
                  pl.BlockSpec((1, d, ff), lambda e, *_: (e, 0, 0)),
                  pl.BlockSpec((1, ff, d), lambda e, *_: (e, 0, 0))],
        out_specs=pl.BlockSpec(memory_space=pl.ANY),
        scratch_shapes=[pltpu.VMEM((EXPERT_SLOTS, MOE_BLOCK, dp), jnp.int32),
                        pltpu.VMEM((EXPERT_SLOTS, MOE_BLOCK, dp), jnp.int32),
                        pltpu.VMEM((d, ff), BF16), pltpu.VMEM((d, ff), BF16), pltpu.VMEM((ff, d), BF16),
                        pltpu.SemaphoreType.DMA((EXPERT_SLOTS,)),
                        pltpu.SemaphoreType.DMA((EXPERT_SLOTS,))],
    )
    return pl.pallas_call(
        _experts_kernel,
        grid_spec=grid_spec,
        out_shape=jax.ShapeDtypeStruct((p, dp), jnp.int32),
        compiler_params=_params("arbitrary"),
        name="experts",
    )(blk0, nblk, counts, row0, is_half, n_used, xs, wg, wu, wd)


def _combine_kernel(base_ref, mod_ref, w_ref, y_ref, o_ref):
    acc = None
    for kk in range(TOP_K):
        term = w_ref[:, kk:kk + 1] * _unpack_bf16_pairs(y_ref[kk])
        acc = term if acc is None else acc + term
    o_ref[0] = base_ref[0] + mod_ref[0, 5:6, :] * acc


def _combine(acc, bi, mod, w_tk, yg, tm):
    b, t, d = acc.shape
    nt = t // tm
    tok = pl.BlockSpec((1, tm, d), lambda i: (bi, i, 0))
    return pl.pallas_call(
        _combine_kernel,
        grid=(nt,),
        in_specs=[tok, pl.BlockSpec((1, 8, d), lambda i: (bi, 0, 0)),
                  pl.BlockSpec((tm, TOP_K), lambda i: (bi * nt + i, 0)),
                  pl.BlockSpec((TOP_K, tm, d // 2), lambda i: (0, i, 0))],
        out_specs=tok,
        out_shape=jax.ShapeDtypeStruct((b, t, d), F32),
        input_output_aliases={0: 0},
        compiler_params=_params("parallel"),
        name="combine",
    )(acc, mod, w_tk, yg)


def _rope_tables(t, tm):
    half = M_DQK // 2
    nf = half // 2
    inv = jnp.asarray(np.power(ROPE_BASE, -np.arange(nf, dtype=np.float32) / nf).astype(np.float32))
    ar = jnp.arange(t // GRID_W, dtype=F32)[:, None] * inv[None, :]
    ac = jnp.arange(GRID_W, dtype=F32)[:, None] * inv[None, :]
    zr, zc = jnp.zeros_like(ar), jnp.zeros_like(ac)
    rcos = jnp.concatenate([jnp.cos(ar), jnp.cos(ar), zr, zr], axis=1)
    rsin = jnp.concatenate([-jnp.sin(ar), jnp.sin(ar), zr, zr], axis=1)
    ccos = jnp.tile(jnp.concatenate([zc, zc, jnp.cos(ac), jnp.cos(ac)], axis=1), (tm // GRID_W, 1))
    csin = jnp.tile(jnp.concatenate([zc, zc, -jnp.sin(ac), jnp.sin(ac)], axis=1), (tm // GRID_W, 1))
    return rcos, rsin, ccos, csin


_IN_SIZES = (512, 512, 1024, 1024, 16, 512, 512, 512, 1024, 1024)
_IN_OFFS = tuple(int(v) for v in np.concatenate([[0], np.cumsum(_IN_SIZES)]))


def _arrange_kernel(w_ref, wa_ref, wt_ref):
    seg = lambda i: w_ref[:, _IN_OFFS[i]:_IN_OFFS[i + 1]]
    mq, mk, mv, mo, _, nq, nk, nv, gm, gn = [seg(i) for i in range(10)]
    g0 = _IN_OFFS[4]
    c = w_ref[:, g0:g0 + LANES]
    lane = lax.broadcasted_iota(jnp.int32, c.shape, 1)
    left4, left8 = pltpu.roll(c, LANES - 4, 1), pltpu.roll(c, LANES - 8, 1)
    gi = jnp.where(lane < 4, c, jnp.where(lane < 8, left4, 0.0))
    gf = jnp.where(lane < 4, left4, jnp.where(lane < 8, left8, 0.0))
    wa_ref[...] = jnp.concatenate([mq, mv, mo, gf, nq, nk, nv, gm, gn], axis=1).astype(BF16)
    wt_ref[...] = jnp.concatenate([jnp.transpose(mk), jnp.transpose(gi), jnp.transpose(gf)],
                                  axis=0).astype(BF16)


def _arrange_w_in(w_in, b_mgate):
    d = w_in.shape[0]
    tr = 256
    wt_rows = M_HEADS * M_DQK + 2 * LANES
    w_all, wt_all = pl.pallas_call(
        _arrange_kernel,
        grid=(d // tr,),
        in_specs=[pl.BlockSpec((tr, w_in.shape[1]), lambda i: (i, 0))],
        out_specs=[pl.BlockSpec((tr, W_COLS), lambda i: (i, 0)),
                   pl.BlockSpec((wt_rows, tr), lambda i: (0, i))],
        out_shape=[jax.ShapeDtypeStruct((d, W_COLS), BF16),
                   jax.ShapeDtypeStruct((wt_rows, d), BF16)],
        compiler_params=_params("parallel"),
        name="arrange_w_in",
    )(w_in)
    bpad = jnp.zeros((LANES - 2 * M_HEADS,), F32)
    bi = jnp.concatenate([b_mgate[0:4], b_mgate[8:12], bpad])
    bf = jnp.concatenate([b_mgate[4:8], b_mgate[12:16], bpad])
    bg = jnp.concatenate([bf[None, :], jnp.zeros((7, LANES), F32)], axis=0)
    bgt = jnp.concatenate([bi, bf])[:, None]
    return w_all, wt_all, bg, bgt


def _segment_mats():
    na_w = NA_HEADS * NA_DH
    seg = np.zeros((na_w, LANES), np.float32)
    seg[np.arange(na_w), np.arange(na_w) // NA_DH] = 1.0
    return jnp.asarray(seg, BF16), jnp.asarray(seg.T.copy(), BF16)


def kernel(x, c, ctx, c_ctx, w_ada, b_ada, w_in, b_mgate, m_norm_w, na_qn_w, na_kn_w, na_rpb,
           w_br_m, w_br_na, w_out, w_router, router_bias, w_exp_gate, w_exp_up, w_exp_down,
           w_sh_gate, w_sh_up, w_sh_down):
    b, t, d = x.shape
    n = b * t
    rows = t // GRID_W
    l = 0

    cc = jnp.concatenate([c, c_ctx[None, :], jnp.zeros((8 - b - 1, d), F32)], axis=0)
    mod = _ada(cc, w_ada[l], b_ada[l])
    mod = mod.reshape(8, 6, d)
    mod = jnp.concatenate([mod, jnp.zeros((8, 2, d), F32)], axis=1)
    mod_x = mod[:b]
    mod_c = jnp.broadcast_to(mod[b:b + 1], (b, 8, d))

    w_all, wt_all, bg, bgt = _arrange_w_in(w_in[l], b_mgate[l])
    seg, segt = _segment_mats()
    qnw = jnp.tile(na_qn_w[l], NA_HEADS)[None, :]
    knw = jnp.tile(na_kn_w[l], NA_HEADS)[None, :]
    tm = min(512, t)

    cp = _inproj(ctx, mod_c, w_all, wt_all, bg, bgt, qnw, knw, seg, segt, None,
                 min(tm, ctx.shape[1]))
    xp = _inproj(x, mod_x, w_all, wt_all, bg, bgt, qnw, knw, seg, segt, _rope_tables(t, tm), tm)
    cmq, cmv, _, cgf, _, cnk, cnv, _, _, cmkt, cgit, cgft = cp
    mq, mv, mo, gf, nq, nk, nv, gm, gn, mkt, git, gft = xp

    c0 = jnp.zeros((b, 8, M_DQK, MLSTM_EXT), F32)
    m0 = jnp.zeros((b, 8, LANES), F32)
    _, _, c1, m1 = _mlstm(cmq, cmkt, cmv, cgf, cgit, cgft, c0, m0)
    hf, hb, _, _ = _mlstm(mq, mkt, mv, gf, git, gft, c1, m1)

    yna = _na(nq, nk, nv, cnk, cnv, _na_bias_table(na_rpb[l], rows))

    bias_col = jnp.broadcast_to(router_bias[l][:, None], (N_EXPERTS, LANES))
    h2p, base, top_e, top_w, rank, cnt = _post(
        x, mod_x, hf, hb, mo, yna, gm, gn, m_norm_w[l][None, :],
        w_br_m[l].astype(BF16), w_br_na[l].astype(BF16), w_out[l].astype(BF16),
        w_router[l].T.astype(BF16), w_sh_gate[l].astype(BF16), w_sh_up[l].astype(BF16),
        w_sh_down[l].astype(BF16), bias_col, tm)

    counts = cnt[:, 0].astype(jnp.int32)
    half = MOE_BLOCK // 2
    p_rows = (-(-(n * TOP_K) // half) + N_EXPERTS) * half
    nb_max = -(-(n * TOP_K) // MOE_BLOCK) + N_EXPERTS
    pstart, blk0, nblk, row0, is_half, n_used = _expert_plan(counts, nb_max)

    xs, pos = _dispatch_rows(h2p, top_e, rank, pstart, p_rows)
    ys = _experts(blk0, nblk, counts, row0, is_half, n_used, xs,
                  w_exp_gate[l], w_exp_up[l], w_exp_down[l])
    out = base
    for bi in range(b):
        idx = pos[:, bi * t:(bi + 1) * t].reshape(-1)
        yg = _gather_rows(ys, idx).reshape(TOP_K, t, d // 2)
        out = _combine(out, bi, mod_x, top_w, yg, tm)
    return out
```

```python
import functools

import numpy as np
import jax
import jax.numpy as jnp
from jax import lax
from jax.experimental import pallas as pl
from jax.experimental.pallas import tpu as pltpu
from jax.experimental.pallas import tpu_sc as plsc

F32 = jnp.float32
BF16 = jnp.bfloat16

EPS = 1e-6
GRID_W = 64
M_HEADS, M_DQK, M_DV = 4, 128, 256
ROPE_BASE = 10000.0
NA_HEADS, NA_DH, NA_KH, NA_KW = 8, 64, 8, 16
N_EXPERTS, TOP_K, N_GROUPS, TOPK_GROUPS = 256, 8, 8, 4
ROUTE_SCALE = 2.5

LANES = 128
VMEM_LIMIT = 56 * 1024 * 1024
NEG = -1e30
LOG2E = 1.4426950408889634

MLSTM_CHUNK = 256
NA_ROWS = 4
NA_KEY_ROWS = NA_ROWS + NA_KH - 1
NA_SUB = 8
MOE_BLOCK = 512
EXPERT_SLOTS = 4

_W_SEGS = (("mq", 512), ("mv", 1024), ("mo", 1024), ("gf", 128),
           ("nq", 512), ("nk", 512), ("nv", 512), ("gm", 1024), ("gn", 1024))
_W_OFF = {}
_o = 0
for _n, _w in _W_SEGS:
    _W_OFF[_n] = (_o, _w)
    _o += _w
W_COLS = _o


def _dot(a, b):
    return jnp.dot(a, b, preferred_element_type=F32)


def _dot_nt(a, b):
    return lax.dot_general(a, b, (((1,), (1,)), ((), ())), preferred_element_type=F32)


def _sigmoid(x):
    return 1.0 / (1.0 + jnp.exp(-x))


def _pack_bf16_pairs(v):
    w = v.shape[1] // 2
    bits = pltpu.bitcast(v.astype(BF16).astype(F32), jnp.int32)
    return lax.shift_right_logical(bits[:, :w], 16) | bits[:, w:]


def _unpack_bf16_pairs(p):
    lo = pltpu.bitcast(lax.shift_left(p, 16), F32)
    hi = pltpu.bitcast(p & jnp.int32(-65536), F32)
    return jnp.concatenate([lo, hi], axis=1)


def _params(*sem):
    return pltpu.CompilerParams(dimension_semantics=sem, vmem_limit_bytes=VMEM_LIMIT)


def _resident(shape):
    nd = len(shape)
    return pl.BlockSpec(shape, lambda *_: (0,) * nd, pipeline_mode=pl.Buffered(1))


def _ada_kernel(c_ref, w_ref, b_ref, o_ref):
    c = c_ref[...]
    s = c * _sigmoid(c)
    o_ref[...] = _dot(s.astype(BF16), w_ref[...].astype(BF16)) + b_ref[...]


def _ada(cc, w_ada, b_ada):
    d = cc.shape[1]
    n = w_ada.shape[1]
    return pl.pallas_call(
        _ada_kernel,
        grid=(n // d,),
        in_specs=[pl.BlockSpec((8, d), lambda j: (0, 0)),
                  pl.BlockSpec((d, d), lambda j: (0, j)),
                  pl.BlockSpec((1, d), lambda j: (0, j))],
        out_specs=pl.BlockSpec((8, d), lambda j: (0, j)),
        out_shape=jax.ShapeDtypeStruct((8, n), F32),
        compiler_params=_params("arbitrary"),
        name="ada",
    )(cc, w_ada, b_ada.reshape(1, n))


def _rope_rotate(t, cos, sin):
    q = M_DQK // 4
    lane = lax.broadcasted_iota(jnp.int32, t.shape, 1)
    partner = jnp.where((lane & q) == 0, pltpu.roll(t, M_DQK - q, 1), pltpu.roll(t, q, 1))
    return t * cos + partner * sin


def _rope_rotate_t(t, cos, sin):
    q = M_DQK // 4
    partner = jnp.concatenate([t[q:2 * q], t[0:q], t[3 * q:4 * q], t[2 * q:3 * q]], axis=0)
    return t * cos + partner * sin


def _inproj_kernel(*refs, rope):
    if rope:
        (x_ref, mod_ref, w_ref, wt_ref, bg_ref, bgt_ref, qnw_ref, knw_ref, seg_ref, segt_ref,
         rcos_ref, rsin_ref, ccos_ref, csin_ref,
         mq_ref, mv_ref, mo_ref, gf_ref, nq_ref, nk_ref, nv_ref, gm_ref, gn_ref,
         mkt_ref, git_ref, gft_ref) = refs
    else:
        (x_ref, mod_ref, w_ref, wt_ref, bg_ref, bgt_ref, qnw_ref, knw_ref, seg_ref, segt_ref,
         mq_ref, mv_ref, mo_ref, gf_ref, nq_ref, nk_ref, nv_ref, gm_ref, gn_ref,
         mkt_ref, git_ref, gft_ref) = refs
    x = x_ref[0]
    xn = x * lax.rsqrt(jnp.mean(x * x, axis=-1, keepdims=True) + EPS)
    h = xn * (1.0 + mod_ref[0, 1:2, :]) + mod_ref[0, 0:1, :]
    hb = h.astype(BF16)

    def proj(name):
        off, width = _W_OFF[name]
        return _dot(hb, w_ref[:, off:off + width])

    def head_rms(t, w_row, scale):
        ss = _dot((t * t).astype(BF16), seg_ref[...])
        r = lax.rsqrt(ss * (1.0 / NA_DH) + EPS)
        r_hi = r.astype(BF16)
        r_lo = (r - r_hi.astype(F32)).astype(BF16)
        rb = _dot(r_hi, segt_ref[...]) + _dot(r_lo, segt_ref[...])
        return t * rb * w_row * scale

    mq = proj("mq") * (M_DQK ** -0.5)
    if rope:
        tm = x.shape[0]
        spread = lambda r: jnp.broadcast_to(r[:, None, :], (tm // GRID_W, GRID_W, LANES)).reshape(tm, LANES)
        cos = spread(rcos_ref[...]) + ccos_ref[...]
        sin = spread(rsin_ref[...]) + csin_ref[...]
        mq = jnp.concatenate([_rope_rotate(mq[:, i * LANES:(i + 1) * LANES], cos, sin)
                              for i in range(M_HEADS)], axis=1)
    mq_ref[0] = mq.astype(BF16)
    mv_ref[0] = proj("mv").astype(BF16)
    mo_ref[0] = _sigmoid(proj("mo")).astype(BF16)
    gf_ref[0] = proj("gf") + bg_ref[0:1, :]
    nq_ref[0] = head_rms(proj("nq"), qnw_ref[...], NA_DH ** -0.5 * LOG2E).astype(BF16)
    nk_ref[0] = head_rms(proj("nk"), knw_ref[...], 1.0).astype(BF16)
    nv_ref[0] = proj("nv").astype(BF16)
    gm_ref[0] = _sigmoid(proj("gm")).astype(BF16)
    gn_ref[0] = _sigmoid(proj("gn")).astype(BF16)

    qk_w = M_HEADS * M_DQK
    mkt = _dot_nt(wt_ref[0:qk_w, :], hb)
    if rope:
        cost, sint = jnp.transpose(cos), jnp.transpose(sin)
        mkt = jnp.concatenate([_rope_rotate_t(mkt[i * M_DQK:(i + 1) * M_DQK], cost, sint)
                               for i in range(M_HEADS)], axis=0)
    mkt_ref[0] = mkt.astype(BF16)
    git_ref[0] = _dot_nt(wt_ref[qk_w:qk_w + LANES, :], hb) + bgt_ref[0:LANES, :]
    gft_ref[0] = _dot_nt(wt_ref[qk_w + LANES:qk_w + 2 * LANES, :], hb) + bgt_ref[LANES:2 * LANES, :]


def _inproj(x, mod, w_all, wt_all, bg, bgt, qnw, knw, seg, segt, rope_tabs, tm):
    b, t, d = x.shape
    rope = rope_tabs is not None
    tok = lambda w: pl.BlockSpec((1, tm, w), lambda bi, i: (bi, i, 0))
    tok_t = lambda w: pl.BlockSpec((1, w, tm), lambda bi, i: (bi, 0, i))
    in_specs = [tok(d),
                pl.BlockSpec((1, 8, d), lambda bi, i: (bi, 0, 0)),
                _resident(w_all.shape), _resident(wt_all.shape), _resident(bg.shape),
                _resident(bgt.shape), _resident(qnw.shape),
                _resident(knw.shape), _resident(seg.shape), _resident(segt.shape)]
    args = [x, mod, w_all, wt_all, bg, bgt, qnw, knw, seg, segt]
    if rope:
        in_specs += [pl.BlockSpec((tm // GRID_W, LANES), lambda bi, i: (i, 0))] * 2
        in_specs += [_resident((tm, LANES))] * 2
        args += list(rope_tabs)
    widths = [("mq", BF16), ("mv", BF16), ("mo", BF16), ("gf", F32),
              ("nq", BF16), ("nk", BF16), ("nv", BF16), ("gm", BF16), ("gn", BF16)]
    out_specs = [tok(_W_OFF[n][1]) for n, _ in widths]
    out_shape = [jax.ShapeDtypeStruct((b, t, _W_OFF[n][1]), dt) for n, dt in widths]
    out_specs += [tok_t(M_HEADS * M_DQK), tok_t(LANES), tok_t(LANES)]
    out_shape += [jax.ShapeDtypeStruct((b, M_HEADS * M_DQK, t), BF16),
                  jax.ShapeDtypeStruct((b, LANES, t), F32),
                  jax.ShapeDtypeStruct((b, LANES, t), F32)]
    return pl.pallas_call(
        functools.partial(_inproj_kernel, rope=rope),
        grid=(b, t // tm),
        in_specs=in_specs, out_specs=out_specs, out_shape=out_shape,
        compiler_params=_params("parallel", "parallel"),
        name="inproj_rope" if rope else "inproj_ctx",
    )(*args)


def _log_sigmoid(x):
    return jnp.minimum(x, 0.0) - jnp.log(1.0 + jnp.exp(-jnp.abs(x)))


def _dot_split(a, b, split_a):
    x = a if split_a else b
    hi = x.astype(BF16)
    lo = (x - hi.astype(F32)).astype(BF16)
    return (_dot(hi, b) + _dot(lo, b)) if split_a else (_dot(a, hi) + _dot(a, lo))


MLSTM_EXT = M_DV + LANES


def _mlstm_kernel(qf_ref, ktf_ref, vf_ref, gff_ref, gitf_ref, gftf_ref,
                  qb_ref, ktb_ref, vb_ref, gfb_ref, gitb_ref, gftb_ref,
                  c0_ref, m0_ref,
                  hf_ref, hb_ref, cn_ref, mn_ref,
                  *scratch):
    c_scrs, m_scr = scratch[:2 * M_HEADS], scratch[2 * M_HEADS]
    step = pl.program_id(1)
    L = qf_ref.shape[1]
    nu = 2 * M_HEADS

    @pl.when(step == 0)
    def _():
        for j, c_scr in enumerate(c_scrs):
            c_scr[...] = c0_ref[0, j]
        m_scr[...] = m0_ref[0]

    row_i = lax.broadcasted_iota(jnp.int32, (L, L), 0)
    col_i = lax.broadcasted_iota(jnp.int32, (L, L), 1)
    lower = col_i <= row_i
    upper = col_i >= row_i
    tri_lo = jnp.where(lower, 1.0, 0.0).astype(BF16)
    tri_up = jnp.where(upper, 1.0, 0.0).astype(BF16)

    is_f = lax.broadcasted_iota(jnp.int32, (nu, L), 0) < M_HEADS
    gi_t = jnp.where(is_f, gitf_ref[0, 0:nu, :], gitb_ref[0, 0:nu, :]) * LOG2E
    ls_tf = _log_sigmoid(gftf_ref[0, 0:nu, :]) * LOG2E
    ls_tb = _log_sigmoid(gftb_ref[0, 0:nu, :]) * LOG2E
    b_t = jnp.where(is_f, _dot_split(ls_tf, tri_up, True), _dot_split(ls_tb, tri_lo, True))
    u_t = gi_t - b_t
    g_c = jnp.sum(jnp.where(is_f, ls_tf, ls_tb), axis=1, keepdims=True)
    m_prev = m_scr[...]
    a_t = g_c + u_t
    m_new = jnp.maximum(g_c + m_prev, jnp.max(a_t, axis=1, keepdims=True))
    decay = jnp.exp2(g_c + m_prev - m_new)
    wa_t = jnp.exp2(a_t - jnp.concatenate([m_new] * (L // LANES), axis=1))
    m_scr[...] = m_new

    ones = jnp.ones((L, LANES), BF16)
    dirs = ((qf_ref, ktf_ref, vf_ref, gff_ref, hf_ref, lower, tri_lo),
            (qb_ref, ktb_ref, vb_ref, gfb_ref, hb_ref, upper, tri_up))
    for d, (q_ref, kt_ref, v_ref, gf_ref, h_ref, mask, tri) in enumerate(dirs):
        bcum = _dot_split(tri, _log_sigmoid(gf_ref[0]) * LOG2E, False)
        for hd in range(M_HEADS):
            j = d * M_HEADS + hd
            c_scr = c_scrs[j]
            q = q_ref[0, :, hd * M_DQK:(hd + 1) * M_DQK]
            k_t = kt_ref[0, hd * M_DQK:(hd + 1) * M_DQK, :]
            v_ext = jnp.concatenate([v_ref[0, :, hd * M_DV:(hd + 1) * M_DV], ones], axis=1)
            u_row = u_t[j:j + 1, :]
            mp_row = m_prev[j:j + 1, :]
            c_prev = c_scr[...]

            m_loc = jnp.max(jnp.where(mask, u_row, NEG), axis=1, keepdims=True)
            m_rep = jnp.maximum(jnp.broadcast_to(m_loc, (L, LANES)), mp_row)
            m_wide = jnp.concatenate([m_rep] * (L // LANES), axis=1)
            dmat = jnp.exp2(jnp.where(mask, u_row - m_wide, NEG))
            s = (_dot(q, k_t) * dmat).astype(BF16)
            qw = (q.astype(F32) * jnp.exp2(mp_row - m_rep)).astype(BF16)
            r = _dot(s, v_ext) + _dot(qw, c_prev.astype(BF16))
            b_rep = jnp.broadcast_to(bcum[:, j:j + 1], (L, LANES))
            dn = jnp.maximum(jnp.abs(r[:, M_DV:]), jnp.exp2(-(b_rep + m_rep)))
            h_ref[0, :, hd * M_DV:(hd + 1) * M_DV] = (
                r[:, :M_DV] / jnp.concatenate([dn] * (M_DV // LANES), axis=1)).astype(h_ref.dtype)

            kw = (k_t.astype(F32) * wa_t[j:j + 1, :]).astype(BF16)
            dec = jnp.concatenate([decay[j:j + 1, :]] * (MLSTM_EXT // LANES), axis=1)
            c_scr[...] = dec * c_prev + _dot(kw, v_ext)

    @pl.when(step == pl.num_programs(1) - 1)
    def _():
        for j, c_scr in enumerate(c_scrs):
            cn_ref[0, j] = c_scr[...]
        mn_ref[0] = m_scr[...]


def _mlstm(q, kt, v, gf, git, gft, c0, m0):
    b, t, _ = q.shape
    L = min(MLSTM_CHUNK, t)
    nc = t // L
    fwd = lambda w: pl.BlockSpec((1, L, w), lambda bi, i: (bi, i, 0))
    bwd = lambda w: pl.BlockSpec((1, L, w), lambda bi, i: (bi, nc - 1 - i, 0))
    fwd_t = lambda w: pl.BlockSpec((1, w, L), lambda bi, i: (bi, 0, i))
    bwd_t = lambda w: pl.BlockSpec((1, w, L), lambda bi, i: (bi, 0, nc - 1 - i))
    st_c = pl.BlockSpec((1, 8, M_DQK, MLSTM_EXT), lambda bi, i: (bi, 0, 0, 0))
    st_v = pl.BlockSpec((1, 8, LANES), lambda bi, i: (bi, 0, 0))
    qk_w, v_w = M_HEADS * M_DQK, M_HEADS * M_DV
    return pl.pallas_call(
        _mlstm_kernel,
        grid=(b, nc),
        in_specs=[fwd(qk_w), fwd_t(qk_w), fwd(v_w), fwd(LANES), fwd_t(LANES), fwd_t(LANES),
                  bwd(qk_w), bwd_t(qk_w), bwd(v_w), bwd(LANES), bwd_t(LANES), bwd_t(LANES),
                  st_c, st_v],
        out_specs=[fwd(v_w), bwd(v_w), st_c, st_v],
        out_shape=[jax.ShapeDtypeStruct((b, t, v_w), BF16),
                   jax.ShapeDtypeStruct((b, t, v_w), BF16),
                   jax.ShapeDtypeStruct(c0.shape, F32),
                   jax.ShapeDtypeStruct(m0.shape, F32)],
        scratch_shapes=([pltpu.VMEM((M_DQK, MLSTM_EXT), F32) for _ in range(2 * M_HEADS)]
                        + [pltpu.VMEM((8, LANES), F32)]),
        compiler_params=_params("parallel", "arbitrary"),
        name="mlstm",
    )(q, kt, v, gf, git, gft, q, kt, v, gf, git, gft, c0, m0)


def _na_kernel(q_ref, k_ref, v_ref, kc_ref, vc_ref, bias_ref, o_ref, *, rows, nsub):
    tq = NA_ROWS * GRID_W
    nkeys = NA_KEY_ROWS * GRID_W
    last_rb = rows // NA_ROWS - 1
    kc = kc_ref[0]
    vc = vc_ref[0]
    for sb in range(nsub):
        rb = pl.program_id(2) * nsub + sb
        kind = jnp.where(rb == 0, 0, jnp.where(rb == last_rb, 2, 1))
        ks = jnp.clip(rb * NA_ROWS - NA_KH // 2, 0, rows - NA_KEY_ROWS)
        kstart = pl.multiple_of(ks * GRID_W, GRID_W)
        kblk = k_ref[0, pl.ds(kstart, nkeys), :]
        vblk = v_ref[0, pl.ds(kstart, nkeys), :]
        q = q_ref[0, sb * tq:(sb + 1) * tq, :]
        lane = lax.broadcasted_iota(jnp.int32, q.shape, 1)
        outs = []
        for hh in range(2):
            in_head = (lane < NA_DH) if hh == 0 else (lane >= NA_DH)
            qm = jnp.where(in_head, q, jnp.zeros_like(q))
            sw = _dot_nt(qm, kblk) + bias_ref[hh, kind]
            sc = _dot_nt(qm, kc)
            m = jnp.maximum(jnp.max(sw, axis=1, keepdims=True), jnp.max(sc, axis=1, keepdims=True))
            ew = jnp.exp2(sw - m)
            ec = jnp.exp2(sc - m)
            l = jnp.sum(ew, axis=1, keepdims=True) + jnp.sum(ec, axis=1, keepdims=True)
            o = _dot(ew.astype(BF16), vblk) + _dot(ec.astype(BF16), vc)
            outs.append(o / l)
        o_ref[0, sb * tq:(sb + 1) * tq, :] = jnp.where(lane < NA_DH, outs[0], outs[1]).astype(o_ref.dtype)


def _na(nq, nk, nv, cnk, cnv, bias):
    b, t, w = nq.shape
    rows = t // GRID_W
    tq = NA_ROWS * GRID_W
    nrb = rows // NA_ROWS
    nsub = min(NA_SUB, nrb)
    nctx = cnk.shape[1]
    return pl.pallas_call(
        functools.partial(_na_kernel, rows=rows, nsub=nsub),
        grid=(b, w // LANES, nrb // nsub),
        in_specs=[pl.BlockSpec((1, nsub * tq, LANES), lambda bi, hp, st: (bi, st, hp)),
                  pl.BlockSpec((1, t, LANES), lambda bi, hp, st: (bi, 0, hp)),
                  pl.BlockSpec((1, t, LANES), lambda bi, hp, st: (bi, 0, hp)),
                  pl.BlockSpec((1, nctx, LANES), lambda bi, hp, st: (bi, 0, hp)),
                  pl.BlockSpec((1, nctx, LANES), lambda bi, hp, st: (bi, 0, hp)),
                  pl.BlockSpec((2,) + bias.shape[1:], lambda bi, hp, st: (hp, 0, 0, 0))],
        out_specs=pl.BlockSpec((1, nsub * tq, LANES), lambda bi, hp, st: (bi, st, hp)),
        out_shape=jax.ShapeDtypeStruct((b, t, w), BF16),
        compiler_params=_params("parallel", "parallel", "arbitrary"),
        name="na",
    )(nq, nk, nv, cnk, cnv, bias)


def _na_bias_table(na_rpb, rows):
    h = na_rpb.shape[0]
    w = GRID_W
    c = np.arange(w)[:, None]
    kj = np.arange(w)[None, :]
    cs = np.clip(c - NA_KW // 2, 0, w - NA_KW)
    col_valid = (kj >= cs) & (kj < cs + NA_KW)
    dc = np.clip(kj - c + (NA_KW - 1), 0, 2 * NA_KW - 2)
    onehot = np.zeros((2 * NA_KW - 1, w, w), np.float32)
    onehot[dc, np.arange(w)[:, None], np.arange(w)[None, :]] = 1.0
    t2 = jnp.einsum("hrd,dck->hrck", na_rpb, jnp.asarray(onehot), precision=lax.Precision.HIGHEST)
    t2 = jnp.where(jnp.asarray(col_valid)[None, None], t2 * LOG2E, NEG)
    t2 = jnp.concatenate([t2, jnp.full((h, 1, w, w), NEG, F32)], axis=1)
    invalid = 2 * NA_KH - 1
    dr_idx = np.full((3, NA_ROWS, NA_KEY_ROWS), invalid, np.int32)
    for kind, r0 in enumerate((0, NA_ROWS, rows - NA_ROWS)):
        ks = int(np.clip(r0 - NA_KH // 2, 0, rows - NA_KEY_ROWS))
        for qa in range(NA_ROWS):
            r = r0 + qa
            rs = int(np.clip(r - NA_KH // 2, 0, rows - NA_KH))
            for kl in range(NA_KEY_ROWS):
                ki = ks + kl
                if rs <= ki < rs + NA_KH:
                    dr_idx[kind, qa, kl] = ki - r + NA_KH - 1
    t2t = t2.transpose(0, 2, 1, 3)
    strips = [jnp.concatenate([t2t[:, :, int(dr), :] for dr in dr_idx[kind, qa]], axis=-1)
              for kind in range(3) for qa in range(NA_ROWS)]
    return jnp.stack(strips, axis=1).reshape(h, 3, NA_ROWS * w, NA_KEY_ROWS * w)


def _post_kernel(x_ref, mod_ref, hf_ref, hb_ref, mo_ref, na_ref, gm_ref, gn_ref,
                 mnw_ref, wbm_ref, wbn_ref, wout_ref, wr_ref, wsg_ref, wsu_ref, wsd_ref, rb_ref,
                 h2_ref, base_ref, e_ref, w_ref, r_ref, cnt_ref, run_scr):
    @pl.when(pl.program_id(0) == 0)
    def _():
        run_scr[...] = jnp.zeros_like(run_scr)

    hm = hf_ref[0].astype(F32) + hb_ref[0].astype(F32)
    parts = []
    for hd in range(M_HEADS):
        t = hm[:, hd * M_DV:(hd + 1) * M_DV]
        parts.append(t * lax.rsqrt(jnp.mean(t * t, axis=-1, keepdims=True) + EPS))
    y_m = jnp.concatenate(parts, axis=1) * mnw_ref[...] * mo_ref[0].astype(F32)
    a = _dot(y_m.astype(BF16), wbm_ref[...])
    bn = _dot(na_ref[0], wbn_ref[...])
    z = gm_ref[0].astype(F32) * a + gn_ref[0].astype(F32) * bn
    y = _dot(z.astype(BF16), wout_ref[...])
    x1 = x_ref[0] + mod_ref[0, 2:3, :] * y
    xn = x1 * lax.rsqrt(jnp.mean(x1 * x1, axis=-1, keepdims=True) + EPS)
    h2f = xn * (1.0 + mod_ref[0, 4:5, :]) + mod_ref[0, 3:4, :]
    h2_ref[...] = _pack_bf16_pairs(h2f)
    h2 = h2f.astype(BF16)
    sh = _dot(h2, wsg_ref[...])
    sh = sh * _sigmoid(sh) * _dot(h2, wsu_ref[...])
    base_ref[0] = x1 + mod_ref[0, 5:6, :] * _dot(sh.astype(BF16), wsd_ref[...])
    scores = _sigmoid(_dot_nt(wr_ref[...], h2))
    _route_block(scores, rb_ref, e_ref, w_ref, r_ref, cnt_ref, run_scr)


def _post(x, mod, hf, hb, mo, yna, gm, gn, mnw, wbm, wbn, wout, wr_t, wsg, wsu, wsd, rbias, tm):
    b, t, d = x.shape
    nt = t // tm
    n = b * t
    tok = lambda w: pl.BlockSpec((1, tm, w), lambda s: (s // nt, s % nt, 0))
    rt = lambda: pl.BlockSpec((TOP_K, tm), lambda s: (0, s))
    res = [mnw, wbm, wbn, wout, wr_t, wsg, wsu, wsd, rbias]
    return pl.pallas_call(
        _post_kernel,
        grid=(b * nt,),
        in_specs=[tok(d), pl.BlockSpec((1, 8, d), lambda s: (s // nt, 0, 0)),
                  tok(hf.shape[2]), tok(hb.shape[2]), tok(mo.shape[2]), tok(yna.shape[2]),
                  tok(gm.shape[2]), tok(gn.shape[2])] + [_resident(a.shape) for a in res],
        out_specs=[pl.BlockSpec((tm, d // 2), lambda s: (s, 0)),
                   tok(d), rt(), pl.BlockSpec((tm, TOP_K), lambda s: (s, 0)), rt(),
                   pl.BlockSpec((N_EXPERTS, LANES), lambda s: (0, 0))],
        out_shape=[jax.ShapeDtypeStruct((n, d // 2), jnp.int32),
                   jax.ShapeDtypeStruct((b, t, d), F32),
                   jax.ShapeDtypeStruct((TOP_K, n), jnp.int32),
                   jax.ShapeDtypeStruct((n, TOP_K), F32),
                   jax.ShapeDtypeStruct((TOP_K, n), jnp.int32),
                   jax.ShapeDtypeStruct((N_EXPERTS, LANES), F32)],
        scratch_shapes=[pltpu.VMEM((N_EXPERTS, LANES), F32)],
        compiler_params=_params("arbitrary"),
        name="post",
    )(x, mod, hf, hb, mo, yna, gm, gn, *res)


def _route_block(s, b_ref, e_ref, w_ref, r_ref, cnt_ref, run_scr):
    tm = s.shape[1]
    sel = s + b_ref[...][:, 0:1]
    gsz = N_EXPERTS // N_GROUPS
    ninf = -jnp.inf

    x3 = sel.reshape(N_GROUPS, gsz, tm)
    r3 = lax.broadcasted_iota(jnp.int32, x3.shape, 1)
    m1 = jnp.max(x3, axis=1, keepdims=True)
    i1 = jnp.min(jnp.where(x3 == m1, r3, gsz), axis=1, keepdims=True)
    m2 = jnp.max(jnp.where(r3 == i1, ninf, x3), axis=1)
    gs = m1[:, 0, :] + m2

    gidx = lax.broadcasted_iota(jnp.int32, gs.shape, 0)
    gkeep = jnp.zeros(gs.shape, jnp.bool_)
    cur = gs
    for _ in range(TOPK_GROUPS):
        mm = jnp.max(cur, axis=0, keepdims=True)
        ii = jnp.min(jnp.where(cur == mm, gidx, N_GROUPS), axis=0, keepdims=True)
        hit = gidx == ii
        gkeep = jnp.logical_or(gkeep, hit)
        cur = jnp.where(hit, ninf, cur)
    keep = jnp.broadcast_to(gkeep[:, None, :], x3.shape).reshape(N_EXPERTS, tm)

    row = lax.broadcasted_iota(jnp.int32, s.shape, 0).astype(F32)
    cur = jnp.where(keep, sel, ninf)
    idxs, ws = [], []
    chosen_f = jnp.zeros(s.shape, F32)
    for _ in range(TOP_K):
        mm = jnp.max(cur, axis=0, keepdims=True)
        ii = jnp.min(jnp.where(cur == mm, row, float(N_EXPERTS)), axis=0, keepdims=True)
        hit = row == ii
        idxs.append(ii)
        ws.append(jnp.sum(jnp.where(hit, s, 0.0), axis=0, keepdims=True))
        chosen_f = jnp.where(hit, 1.0, chosen_f)
        cur = jnp.where(hit, ninf, cur)
    wsum = ws[0]
    for wk in ws[1:]:
        wsum = wsum + wk

    tp =lax.broadcasted_iota(jnp.int32, (tm, tm), 0)
    tc = lax.broadcasted_iota(jnp.int32, (tm, tm), 1)
    before = jnp.where(tp < tc, 1.0, 0.0).astype(BF16)
    rank = _dot(chosen_f.astype(BF16), before) + run_scr[...][:, 0:1]
    run_scr[...] = run_scr[...] + jnp.sum(chosen_f, axis=1, keepdims=True)
    cnt_ref[...] = run_scr[...]

    for kk in range(TOP_K):
        e_ref[kk:kk + 1, :] = idxs[kk].astype(jnp.int32)
        r_ref[kk:kk + 1, :] = jnp.sum(jnp.where(row == idxs[kk], rank, 0.0), axis=0,
                                      keepdims=True).astype(jnp.int32)
    w_ref[...] = jnp.transpose(jnp.concatenate([wk / wsum * ROUTE_SCALE for wk in ws], axis=0))


SC_WINDOW = 128


def _sc_mesh():
    return plsc.VectorSubcoreMesh(core_axis_name="core", subcore_axis_name="subcore")


def _sc_workers():
    info = plsc.get_sparse_core_info()
    return info.num_cores, info.num_cores * info.num_subcores


def _dispatch_rows(x, top_e, rank, pstart, p_rows):
    n, w = x.shape
    kk = top_e.shape[0]
    ncores, nw = _sc_workers()
    lanes = plsc.get_sparse_core_info().num_lanes
    steps = n // nw // SC_WINDOW
    per_worker = lambda a: a.reshape(kk, nw, steps, SC_WINDOW).transpose(1, 2, 0, 3)

    @functools.partial(
        pl.kernel, mesh=_sc_mesh(),
        out_type=[jax.ShapeDtypeStruct((p_rows, w), x.dtype),
                  jax.ShapeDtypeStruct((nw, steps, kk, SC_WINDOW), jnp.int32)],
        scratch_types=[pltpu.VMEM((kk, SC_WINDOW), jnp.int32),
                       pltpu.VMEM((kk, SC_WINDOW), jnp.int32),
                       pltpu.VMEM((kk, SC_WINDOW), jnp.int32),
                       pltpu.VMEM(pstart.shape, jnp.int32),
                       pltpu.VMEM((SC_WINDOW, w), x.dtype),
                       pltpu.SemaphoreType.DMA],
        compiler_params=pltpu.CompilerParams(needs_layout_passes=False),
    )
    def scatter(x_hbm, e_hbm, r_hbm, ps_hbm, o_hbm, pos_hbm, e_v, r_v, pos_v, ps_v, rows_v, sem):
        wid = lax.axis_index("subcore") * ncores + lax.axis_index("core")
        pltpu.sync_copy(ps_hbm, ps_v)

        @pl.loop(0, steps)
        def _(s):
            base = pl.multiple_of((wid * steps + s) * SC_WINDOW, SC_WINDOW)
            pltpu.sync_copy(e_hbm.at[wid, s], e_v)
            pltpu.sync_copy(r_hbm.at[wid, s], r_v)
            pltpu.sync_copy(x_hbm.at[pl.ds(base, SC_WINDOW)], rows_v)
            for j in range(kk):
                for c in range(SC_WINDOW // lanes):
                    cols = pl.ds(c * lanes, lanes)
                    pos_v[j, cols] = plsc.load_gather(ps_v, [e_v[j, cols]]) + r_v[j, cols]
            pltpu.sync_copy(pos_v, pos_hbm.at[wid, s])
            copies = [pltpu.make_async_copy(rows_v, o_hbm.at[pos_v.at[j]], sem) for j in range(kk)]
            for cp in copies:
                cp.start()
            for cp in copies:
                cp.wait()

    out, pos4 = scatter(x, per_worker(top_e), per_worker(rank), pstart)
    return out, pos4.transpose(2, 0, 1, 3).reshape(kk, n)


def _gather_rows(x, idx):
    m = idx.shape[0]
    w = x.shape[1]
    ncores, nw = _sc_workers()
    steps = m // nw // SC_WINDOW
    idx3 = idx.reshape(nw, steps, SC_WINDOW)

    @functools.partial(
        pl.kernel, mesh=_sc_mesh(),
        out_type=jax.ShapeDtypeStruct((m, w), x.dtype),
        scratch_types=[pltpu.VMEM((steps, SC_WINDOW), jnp.int32),
                       pltpu.VMEM((SC_WINDOW, w), x.dtype),
                       pltpu.SemaphoreType.DMA],
    )
    def gather(x_hbm, i_hbm, o_hbm, idx_v, rows_v, sem):
        wid = lax.axis_index("subcore") * ncores + lax.axis_index("core")
        pltpu.sync_copy(i_hbm.at[wid], idx_v)

        @pl.loop(0, steps)
        def _(s):
            pltpu.async_copy(x_hbm.at[idx_v.at[s]], rows_v, sem).wait()
            base = pl.multiple_of((wid * steps + s) * SC_WINDOW, SC_WINDOW)
            pltpu.sync_copy(rows_v, o_hbm.at[pl.ds(base, SC_WINDOW)])

    return gather(x, idx3)


def _experts_kernel(blk0_ref, nblk_ref, cnt_ref, row0_ref, half_ref, nu_ref,
                    x_hbm, wg_ref, wu_ref, wd_ref, y_hbm,
                    xbuf, ybuf, wg_scr, wu_scr, wd_scr, in_sem, out_sem):
    e = pl.program_id(0)
    n_used = nu_ref[0]
    blk0 = blk0_ref[e]
    ns = EXPERT_SLOTS
    sizes = (MOE_BLOCK, MOE_BLOCK // 2)

    def x_copy(g, rows):
        r0 = pl.multiple_of(row0_ref[g], MOE_BLOCK // 2)
        return pltpu.make_async_copy(x_hbm.at[pl.ds(r0, rows)], xbuf.at[g % ns, pl.ds(0, rows)],
                                     in_sem.at[g % ns])

    def y_copy(g, rows):
        r0 = pl.multiple_of(row0_ref[g], MOE_BLOCK // 2)
        return pltpu.make_async_copy(ybuf.at[g % ns, pl.ds(0, rows)], y_hbm.at[pl.ds(r0, rows)],
                                     out_sem.at[g % ns])

    def by_size(g, fn):
        for is_half, rows in enumerate(sizes):
            @pl.when(half_ref[g] == is_half)
            def _():
                fn(rows)

    @pl.when(e == 0)
    def _():
        for g0 in range(ns - 1):
            @pl.when(g0 < n_used)
            def _():
                by_size(g0, lambda rows: x_copy(g0, rows).start())

    wg_scr[...] = wg_ref[0].astype(BF16)
    wu_scr[...] = wu_ref[0].astype(BF16)
    wd_scr[...] = wd_ref[0].astype(BF16)

    def block(b, carry):
        g = blk0 + b
        by_size(g, lambda rows: x_copy(g, rows).wait())

        @pl.when(g + ns - 1 < n_used)
        def _():
            by_size(g + ns - 1, lambda rows: x_copy(g + ns - 1, rows).start())

        @pl.when(g >= ns)
        def _():
            by_size(g - ns, lambda rows: y_copy(g - ns, rows).wait())

        def run(rows):
            rid = lax.broadcasted_iota(jnp.int32, (rows, xbuf.shape[2]), 0)
            xp = jnp.where(rid < cnt_ref[e] - b * MOE_BLOCK, xbuf[g % ns, 0:rows], 0)
            x = _unpack_bf16_pairs(xp).astype(BF16)
            gt = _dot(x, wg_scr[...])
            up = _dot(x, wu_scr[...])
            a = (gt * _sigmoid(gt) * up).astype(BF16)
            ybuf[g % ns, 0:rows] = _pack_bf16_pairs(_dot(a, wd_scr[...]))
            y_copy(g, rows).start()

        by_size(g, run)
        return carry

    lax.fori_loop(0, nblk_ref[e], block, 0)

    @pl.when(e == pl.num_programs(0) - 1)
    def _():
        for back in range(ns, 0, -1):
            @pl.when(n_used >= back)
            def _():
                by_size(n_used - back, lambda rows: y_copy(n_used - back, rows).wait())


def _expert_plan(counts, nb_max):
    half = MOE_BLOCK // 2
    units = (counts + half - 1) // half
    nfull, tail = units // 2, units % 2
    nblk = nfull + tail
    pend = jnp.cumsum(units * half)
    pstart = pend - units * half
    blk_end = jnp.cumsum(nblk)
    blk0 = blk_end - nblk
    g = jnp.arange(nb_max, dtype=jnp.int32)
    ne = counts.shape[0]
    eg = jnp.minimum(jnp.sum((blk_end[None, :] <= g[:, None]).astype(jnp.int32), axis=1), ne - 1)
    onehot = (eg[:, None] == jnp.arange(ne, dtype=jnp.int32)[None, :]).astype(jnp.int32)
    pick = lambda v: jnp.sum(onehot * v[None, :], axis=1)
    local = g - pick(blk0)
    is_half = ((local == pick(nfull)) & (pick(tail) == 1)).astype(jnp.int32)
    row0 = jnp.clip(pick(pstart) + local * MOE_BLOCK, 0, pend[-1] - half)
    return pstart, blk0, nblk, row0, is_half, blk_end[-1:]


def _experts(blk0, nblk, counts, row0, is_half, n_used, xs, wg, wu, wd):
    p, dp = xs.shape
    ne, d, ff = wg.shape
    grid_spec = pltpu.PrefetchScalarGridSpec(
        num_scalar_prefetch=6,
        grid=(ne,),
        in_specs=[pl.BlockSpec(memory_space=pl.ANY),
                  pl.BlockSpec((1, d, ff), lambda e, *_: (e, 0, 0)),
                  pl.BlockSpec((1, d, ff), lambda e, *_: (e, 0, 0)),
                  pl.BlockSpec((1, ff, d), lambda e, *_: (e, 0, 0))],
        out_specs=pl.BlockSpec(memory_space=pl.ANY),
        scratch_shapes=[pltpu.VMEM((EXPERT_SLOTS, MOE_BLOCK, dp), jnp.int32),
                        pltpu.VMEM((EXPERT_SLOTS, MOE_BLOCK, dp), jnp.int32),
                        pltpu.VMEM((d, ff), BF16), pltpu.VMEM((d, ff), BF16), pltpu.VMEM((ff, d), BF16),
                        pltpu.SemaphoreType.DMA((EXPERT_SLOTS,)),
                        pltpu.SemaphoreType.DMA((EXPERT_SLOTS,))],
    )
    return pl.pallas_call(
        _experts_kernel,
        grid_spec=grid_spec,
        out_shape=jax.ShapeDtypeStruct((p, dp), jnp.int32),
        compiler_params=_params("arbitrary"),
        name="experts",
    )(blk0, nblk, counts, row0, is_half, n_used, xs, wg, wu, wd)


def _combine_kernel(base_ref, mod_ref, w_ref, y_ref, o_ref):
    acc = None
    for kk in range(TOP_K):
        term = w_ref[:, kk:kk + 1] * _unpack_bf16_pairs(y_ref[kk])
        acc = term if acc is None else acc + term
    o_ref[0] = base_ref[0] + mod_ref[0, 5:6, :] * acc


def _combine(acc, bi, mod, w_tk, yg, tm):
    b, t, d = acc.shape
    nt = t // tm
    tok = pl.BlockSpec((1, tm, d), lambda i: (bi, i, 0))
    return pl.pallas_call(
        _combine_kernel,
        grid=(nt,),
        in_specs=[tok, pl.BlockSpec((1, 8, d), lambda i: (bi, 0, 0)),
                  pl.BlockSpec((tm, TOP_K), lambda i: (bi * nt + i, 0)),
                  pl.BlockSpec((TOP_K, tm, d // 2), lambda i: (0, i, 0))],
        out_specs=tok,
        out_shape=jax.ShapeDtypeStruct((b, t, d), F32),
        input_output_aliases={0: 0},
        compiler_params=_params("parallel"),
        name="combine",
    )(acc, mod, w_tk, yg)


def _rope_tables(t, tm):
    half = M_DQK // 2
    nf = half // 2
    inv = jnp.asarray(np.power(ROPE_BASE, -np.arange(nf, dtype=np.float32) / nf).astype(np.float32))
    ar = jnp.arange(t // GRID_W, dtype=F32)[:, None] * inv[None, :]
    ac = jnp.arange(GRID_W, dtype=F32)[:, None] * inv[None, :]
    zr, zc = jnp.zeros_like(ar), jnp.zeros_like(ac)
    rcos = jnp.concatenate([jnp.cos(ar), jnp.cos(ar), zr, zr], axis=1)
    rsin = jnp.concatenate([-jnp.sin(ar), jnp.sin(ar), zr, zr], axis=1)
    ccos = jnp.tile(jnp.concatenate([zc, zc, jnp.cos(ac), jnp.cos(ac)], axis=1), (tm // GRID_W, 1))
    csin = jnp.tile(jnp.concatenate([zc, zc, -jnp.sin(ac), jnp.sin(ac)], axis=1), (tm // GRID_W, 1))
    return rcos, rsin, ccos, csin


_IN_SIZES = (512, 512, 1024, 1024, 16, 512, 512, 512, 1024, 1024)
_IN_OFFS = tuple(int(v) for v in np.concatenate([[0], np.cumsum(_IN_SIZES)]))


def _arrange_kernel(w_ref, wa_ref, wt_ref):
    seg = lambda i: w_ref[:, _IN_OFFS[i]:_IN_OFFS[i + 1]]
    mq, mk, mv, mo, _, nq, nk, nv, gm, gn = [seg(i) for i in range(10)]
    g0 = _IN_OFFS[4]
    c = w_ref[:, g0:g0 + LANES]
    lane = lax.broadcasted_iota(jnp.int32, c.shape, 1)
    left4, left8 = pltpu.roll(c, LANES - 4, 1), pltpu.roll(c, LANES - 8, 1)
    gi = jnp.where(lane < 4, c, jnp.where(lane < 8, left4, 0.0))
    gf = jnp.where(lane < 4, left4, jnp.where(lane < 8, left8, 0.0))
    wa_ref[...] = jnp.concatenate([mq, mv, mo, gf, nq, nk, nv, gm, gn], axis=1).astype(BF16)
    wt_ref[...] = jnp.concatenate([jnp.transpose(mk), jnp.transpose(gi), jnp.transpose(gf)],
                                  axis=0).astype(BF16)


def _arrange_w_in(w_in, b_mgate):
    d = w_in.shape[0]
    tr = 256
    wt_rows = M_HEADS * M_DQK + 2 * LANES
    w_all, wt_all = pl.pallas_call(
        _arrange_kernel,
        grid=(d // tr,),
        in_specs=[pl.BlockSpec((tr, w_in.shape[1]), lambda i: (i, 0))],
        out_specs=[pl.BlockSpec((tr, W_COLS), lambda i: (i, 0)),
                   pl.BlockSpec((wt_rows, tr), lambda i: (0, i))],
        out_shape=[jax.ShapeDtypeStruct((d, W_COLS), BF16),
                   jax.ShapeDtypeStruct((wt_rows, d), BF16)],
        compiler_params=_params("parallel"),
        name="arrange_w_in",
    )(w_in)
    bpad = jnp.zeros((LANES - 2 * M_HEADS,), F32)
    bi = jnp.concatenate([b_mgate[0:4], b_mgate[8:12], bpad])
    bf = jnp.concatenate([b_mgate[4:8], b_mgate[12:16], bpad])
    bg = jnp.concatenate([bf[None, :], jnp.zeros((7, LANES), F32)], axis=0)
    bgt = jnp.concatenate([bi, bf])[:, None]
    return w_all, wt_all, bg, bgt


def _segment_mats():
    na_w = NA_HEADS * NA_DH
    seg = np.zeros((na_w, LANES), np.float32)
    seg[np.arange(na_w), np.arange(na_w) // NA_DH] = 1.0
    return jnp.asarray(seg, BF16), jnp.asarray(seg.T.copy(), BF16)


def kernel(x, c, ctx, c_ctx, w_ada, b_ada, w_in, b_mgate, m_norm_w, na_qn_w, na_kn_w, na_rpb,
           w_br_m, w_br_na, w_out, w_router, router_bias, w_exp_gate, w_exp_up, w_exp_down,
           w_sh_gate, w_sh_up, w_sh_down):
    b, t, d = x.shape
    n = b * t
    rows = t // GRID_W
    l = 0

    cc = jnp.concatenate([c, c_ctx[None, :], jnp.zeros((8 - b - 1, d), F32)], axis=0)
    mod = _ada(cc, w_ada[l], b_ada[l])
    mod = mod.reshape(8, 6, d)
    mod = jnp.concatenate([mod, jnp.zeros((8, 2, d), F32)], axis=1)
    mod_x = mod[:b]
    mod_c = jnp.broadcast_to(mod[b:b + 1], (b, 8, d))

    w_all, wt_all, bg, bgt = _arrange_w_in(w_in[l], b_mgate[l])
    seg, segt = _segment_mats()
    qnw = jnp.tile(na_qn_w[l], NA_HEADS)[None, :]
    knw = jnp.tile(na_kn_w[l], NA_HEADS)[None, :]
    tm = min(512, t)

    cp = _inproj(ctx, mod_c, w_all, wt_all, bg, bgt, qnw, knw, seg, segt, None,
                 min(tm, ctx.shape[1]))
    xp = _inproj(x, mod_x, w_all, wt_all, bg, bgt, qnw, knw, seg, segt, _rope_tables(t, tm), tm)
    cmq, cmv, _, cgf, _, cnk, cnv, _, _, cmkt, cgit, cgft = cp
    mq, mv, mo, gf, nq, nk, nv, gm, gn, mkt, git, gft = xp

    c0 = jnp.zeros((b, 8, M_DQK, MLSTM_EXT), F32)
    m0 = jnp.zeros((b, 8, LANES), F32)
    _, _, c1, m1 = _mlstm(cmq, cmkt, cmv, cgf, cgit, cgft, c0, m0)
    hf, hb, _, _ = _mlstm(mq, mkt, mv, gf, git, gft, c1, m1)

    yna = _na(nq, nk, nv, cnk, cnv, _na_bias_table(na_rpb[l], rows))

    bias_col = jnp.broadcast_to(router_bias[l][:, None], (N_EXPERTS, LANES))
    h2p, base, top_e, top_w, rank, cnt = _post(
        x, mod_x, hf, hb, mo, yna, gm, gn, m_norm_w[l][None, :],
        w_br_m[l].astype(BF16), w_br_na[l].astype(BF16), w_out[l].astype(BF16),
        w_router[l].T.astype(BF16), w_sh_gate[l].astype(BF16), w_sh_up[l].astype(BF16),
        w_sh_down[l].astype(BF16), bias_col, tm)

    counts = cnt[:, 0].astype(jnp.int32)
    half = MOE_BLOCK // 2
    p_rows = (-(-(n * TOP_K) // half) + N_EXPERTS) * half
    nb_max = -(-(n * TOP_K) // MOE_BLOCK) + N_EXPERTS
    pstart, blk0, nblk, row0, is_half, n_used = _expert_plan(counts, nb_max)

    xs, pos = _dispatch_rows(h2p, top_e, rank, pstart, p_rows)
    ys = _experts(blk0, nblk, counts, row0, is_half, n_used, xs,
                  w_exp_gate[l], w_exp_up[l], w_exp_down[l])
    out = base
    for bi in range(b):
        idx = pos[:, bi * t:(bi + 1) * t].reshape(-1)
        yg = _gather_rows(ys, idx).reshape(TOP_K, t, d // 2)
        out = _combine(out, bi, mod_x, top_w, yg, tm)
    return out
```

```python
import functools

import numpy as np
import jax
import jax.numpy as jnp
from jax import lax
from jax.experimental import pallas as pl
from jax.experimental.pallas import tpu as pltpu
from jax.experimental.pallas import tpu_sc as plsc

F32 = jnp.float32
BF16 = jnp.bfloat16

EPS = 1e-6
GRID_W = 64
M_HEADS, M_DQK, M_DV = 4, 128, 256
ROPE_BASE = 10000.0
NA_HEADS, NA_DH, NA_KH, NA_KW = 8, 64, 8, 16
N_EXPERTS, TOP_K, N_GROUPS, TOPK_GROUPS = 256, 8, 8, 4
ROUTE_SCALE = 2.5

LANES = 128
VMEM_LIMIT = 56 * 1024 * 1024
NEG = -1e30
LOG2E = 1.4426950408889634

MLSTM_CHUNK = 256
NA_ROWS = 4
NA_KEY_ROWS = NA_ROWS + NA_KH - 1
NA_SUB = 8
MOE_BLOCK = 512
EXPERT_SLOTS = 4

_W_SEGS = (("mq", 512), ("mv", 1024), ("mo", 1024), ("gf", 128),
           ("nq", 512), ("nk", 512), ("nv", 512), ("gm", 1024), ("gn", 1024))
_W_OFF = {}
_o = 0
for _n, _w in _W_SEGS:
    _W_OFF[_n] = (_o, _w)
    _o += _w
W_COLS = _o


def _dot(a, b):
    return jnp.dot(a, b, preferred_element_type=F32)


def _dot_nt(a, b):
    return lax.dot_general(a, b, (((1,), (1,)), ((), ())), preferred_element_type=F32)


def _sigmoid(x):
    return 1.0 / (1.0 + jnp.exp(-x))


def _pack_bf16_pairs(v):
    w = v.shape[1] // 2
    bits = pltpu.bitcast(v.astype(BF16).astype(F32), jnp.int32)
    return lax.shift_right_logical(bits[:, :w], 16) | bits[:, w:]


def _unpack_bf16_pairs(p):
    lo = pltpu.bitcast(lax.shift_left(p, 16), F32)
    hi = pltpu.bitcast(p & jnp.int32(-65536), F32)
    return jnp.concatenate([lo, hi], axis=1)


def _params(*sem):
    return pltpu.CompilerParams(dimension_semantics=sem, vmem_limit_bytes=VMEM_LIMIT)


def _resident(shape):
    nd = len(shape)
    return pl.BlockSpec(shape, lambda *_: (0,) * nd, pipeline_mode=pl.Buffered(1))


def _ada_kernel(c_ref, w_ref, b_ref, o_ref):
    c = c_ref[...]
    s = c * _sigmoid(c)
    o_ref[...] = _dot(s.astype(BF16), w_ref[...].astype(BF16)) + b_ref[...]


def _ada(cc, w_ada, b_ada):
    d = cc.shape[1]
    n = w_ada.shape[1]
    return pl.pallas_call(
        _ada_kernel,
        grid=(n // d,),
        in_specs=[pl.BlockSpec((8, d), lambda j: (0, 0)),
                  pl.BlockSpec((d, d), lambda j: (0, j)),
                  pl.BlockSpec((1, d), lambda j: (0, j))],
        out_specs=pl.BlockSpec((8, d), lambda j: (0, j)),
        out_shape=jax.ShapeDtypeStruct((8, n), F32),
        compiler_params=_params("arbitrary"),
        name="ada",
    )(cc, w_ada, b_ada.reshape(1, n))


def _rope_rotate(t, cos, sin):
    q = M_DQK // 4
    lane = lax.broadcasted_iota(jnp.int32, t.shape, 1)
    partner = jnp.where((lane & q) == 0, pltpu.roll(t, M_DQK - q, 1), pltpu.roll(t, q, 1))
    return t * cos + partner * sin


def _rope_rotate_t(t, cos, sin):
    q = M_DQK // 4
    partner = jnp.concatenate([t[q:2 * q], t[0:q], t[3 * q:4 * q], t[2 * q:3 * q]], axis=0)
    return t * cos + partner * sin


def _inproj_kernel(*refs, rope):
    if rope:
        (x_ref, mod_ref, w_ref, wt_ref, bg_ref, bgt_ref, qnw_ref, knw_ref, seg_ref, segt_ref,
         rcos_ref, rsin_ref, ccos_ref, csin_ref,
         mq_ref, mv_ref, mo_ref, gf_ref, nq_ref, nk_ref, nv_ref, gm_ref, gn_ref,
         mkt_ref, git_ref, gft_ref) = refs
    else:
        (x_ref, mod_ref, w_ref, wt_ref, bg_ref, bgt_ref, qnw_ref, knw_ref, seg_ref, segt_ref,
         mq_ref, mv_ref, mo_ref, gf_ref, nq_ref, nk_ref, nv_ref, gm_ref, gn_ref,
         mkt_ref, git_ref, gft_ref) = refs
    x = x_ref[0]
    xn = x * lax.rsqrt(jnp.mean(x * x, axis=-1, keepdims=True) + EPS)
    h = xn * (1.0 + mod_ref[0, 1:2, :]) + mod_ref[0, 0:1, :]
    hb = h.astype(BF16)

    def proj(name):
        off, width = _W_OFF[name]
        return _dot(hb, w_ref[:, off:off + width])

    def head_rms(t, w_row, scale):
        ss = _dot((t * t).astype(BF16), seg_ref[...])
        r = lax.rsqrt(ss * (1.0 / NA_DH) + EPS)
        r_hi = r.astype(BF16)
        r_lo = (r - r_hi.astype(F32)).astype(BF16)
        rb = _dot(r_hi, segt_ref[...]) + _dot(r_lo, segt_ref[...])
        return t * rb * w_row * scale

    if rope:
        tm = x.shape[0]
        spread = lambda r: jnp.broadcast_to(r[:, None, :], (tm // GRID_W, GRID_W, LANES)).reshape(tm, LANES)
        cos = spread(rcos_ref[...]) + ccos_ref[...]
        sin = spread(rsin_ref[...]) + csin_ref[...]

    def fin_mq(t):
        t = t * (M_DQK ** -0.5)
        if rope:
            t = jnp.concatenate([_rope_rotate(t[:, i * LANES:(i + 1) * LANES], cos, sin)
                                 for i in range(M_HEADS)], axis=1)
        mq_ref[0] = t.astype(BF16)

    def fin_mkt(t):
        if rope:
            cost, sint = jnp.transpose(cos), jnp.transpose(sin)
            t = jnp.concatenate([_rope_rotate_t(t[i * M_DQK:(i + 1) * M_DQK], cost, sint)
                                 for i in range(M_HEADS)], axis=0)
        mkt_ref[0] = t.astype(BF16)

    def store(ref, fn=lambda t: t):
        def fin(t):
            ref[0] = fn(t).astype(ref.dtype)
        return fin

    qk_w = M_HEADS * M_DQK
    proj_t = lambda lo, hi: (lambda: _dot_nt(wt_ref[lo:hi, :], hb))
    stages = [
        (lambda: proj("mq"), fin_mq),
        (lambda: proj("mv"), store(mv_ref)),
        (lambda: proj("mo"), store(mo_ref, _sigmoid)),
        (lambda: proj("gf"), store(gf_ref, lambda t: t + bg_ref[0:1, :])),
        (lambda: proj("nq"), store(nq_ref, lambda t: head_rms(t, qnw_ref[...], NA_DH ** -0.5 * LOG2E))),
        (lambda: proj("nk"), store(nk_ref, lambda t: head_rms(t, knw_ref[...], 1.0))),
        (lambda: proj("nv"), store(nv_ref)),
        (lambda: proj("gm"), store(gm_ref, _sigmoid)),
        (lambda: proj("gn"), store(gn_ref, _sigmoid)),
        (proj_t(0, qk_w), fin_mkt),
        (proj_t(qk_w, qk_w + LANES), store(git_ref, lambda t: t + bgt_ref[0:LANES, :])),
        (proj_t(qk_w + LANES, qk_w + 2 * LANES), store(gft_ref, lambda t: t + bgt_ref[LANES:2 * LANES, :])),
    ]
    acc = stages[0][0]()
    for i, (_, finish) in enumerate(stages):
        nxt = stages[i + 1][0]() if i + 1 < len(stages) else None
        finish(acc)
        acc = nxt


def _inproj(x, mod, w_all, wt_all, bg, bgt, qnw, knw, seg, segt, rope_tabs, tm):
    b, t, d = x.shape
    rope = rope_tabs is not None
    tok = lambda w: pl.BlockSpec((1, tm, w), lambda bi, i: (bi, i, 0))
    tok_t = lambda w: pl.BlockSpec((1, w, tm), lambda bi, i: (bi, 0, i))
    in_specs = [tok(d),
                pl.BlockSpec((1, 8, d), lambda bi, i: (bi, 0, 0)),
                _resident(w_all.shape), _resident(wt_all.shape), _resident(bg.shape),
                _resident(bgt.shape), _resident(qnw.shape),
                _resident(knw.shape), _resident(seg.shape), _resident(segt.shape)]
    args = [x, mod, w_all, wt_all, bg, bgt, qnw, knw, seg, segt]
    if rope:
        in_specs += [pl.BlockSpec((tm // GRID_W, LANES), lambda bi, i: (i, 0))] * 2
        in_specs += [_resident((tm, LANES))] * 2
        args += list(rope_tabs)
    widths = [("mq", BF16), ("mv", BF16), ("mo", BF16), ("gf", F32),
              ("nq", BF16), ("nk", BF16), ("nv", BF16), ("gm", BF16), ("gn", BF16)]
    out_specs = [tok(_W_OFF[n][1]) for n, _ in widths]
    out_shape = [jax.ShapeDtypeStruct((b, t, _W_OFF[n][1]), dt) for n, dt in widths]
    out_specs += [tok_t(M_HEADS * M_DQK), tok_t(LANES), tok_t(LANES)]
    out_shape += [jax.ShapeDtypeStruct((b, M_HEADS * M_DQK, t), BF16),
                  jax.ShapeDtypeStruct((b, LANES, t), F32),
                  jax.ShapeDtypeStruct((b, LANES, t), F32)]
    return pl.pallas_call(
        functools.partial(_inproj_kernel, rope=rope),
        grid=(b, t // tm),
        in_specs=in_specs, out_specs=out_specs, out_shape=out_shape,
        compiler_params=_params("parallel", "parallel"),
        name="inproj_rope" if rope else "inproj_ctx",
    )(*args)


def _log_sigmoid(x):
    return jnp.minimum(x, 0.0) - jnp.log(1.0 + jnp.exp(-jnp.abs(x)))


def _dot_split(a, b, split_a):
    x = a if split_a else b
    hi = x.astype(BF16)
    lo = (x - hi.astype(F32)).astype(BF16)
    return (_dot(hi, b) + _dot(lo, b)) if split_a else (_dot(a, hi) + _dot(a, lo))


MLSTM_EXT = M_DV + LANES


def _mlstm_kernel(qf_ref, ktf_ref, vf_ref, gff_ref, gitf_ref, gftf_ref,
                  qb_ref, ktb_ref, vb_ref, gfb_ref, gitb_ref, gftb_ref,
                  c0_ref, m0_ref,
                  hf_ref, hb_ref, cn_ref, mn_ref,
                  *scratch):
    c_scrs, m_scr = scratch[:2 * M_HEADS], scratch[2 * M_HEADS]
    step = pl.program_id(1)
    L = qf_ref.shape[1]
    nu = 2 * M_HEADS

    @pl.when(step == 0)
    def _():
        for j, c_scr in enumerate(c_scrs):
            c_scr[...] = c0_ref[0, j]
        m_scr[...] = m0_ref[0]

    row_i = lax.broadcasted_iota(jnp.int32, (L, L), 0)
    col_i = lax.broadcasted_iota(jnp.int32, (L, L), 1)
    lower = col_i <= row_i
    upper = col_i >= row_i
    tri_lo = jnp.where(lower, 1.0, 0.0).astype(BF16)
    tri_up = jnp.where(upper, 1.0, 0.0).astype(BF16)

    is_f = lax.broadcasted_iota(jnp.int32, (nu, L), 0) < M_HEADS
    gi_t = jnp.where(is_f, gitf_ref[0, 0:nu, :], gitb_ref[0, 0:nu, :]) * LOG2E
    ls_tf = _log_sigmoid(gftf_ref[0, 0:nu, :]) * LOG2E
    ls_tb = _log_sigmoid(gftb_ref[0, 0:nu, :]) * LOG2E
    b_t = jnp.where(is_f, _dot_split(ls_tf, tri_up, True), _dot_split(ls_tb, tri_lo, True))
    u_t = gi_t - b_t
    g_c = jnp.sum(jnp.where(is_f, ls_tf, ls_tb), axis=1, keepdims=True)
    m_prev = m_scr[...]
    a_t = g_c + u_t
    m_new = jnp.maximum(g_c + m_prev, jnp.max(a_t, axis=1, keepdims=True))
    decay = jnp.exp2(g_c + m_prev - m_new)
    wa_t = jnp.exp2(a_t - jnp.concatenate([m_new] * (L // LANES), axis=1))
    m_scr[...] = m_new

    ones = jnp.ones((L, LANES), BF16)
    dirs = ((qf_ref, ktf_ref, vf_ref, gff_ref, hf_ref, lower, tri_lo),
            (qb_ref, ktb_ref, vb_ref, gfb_ref, hb_ref, upper, tri_up))
    bcums = [_dot_split(tri, _log_sigmoid(gf_ref[0]) * LOG2E, False) for *_, gf_ref, _, _, tri in dirs]

    def head(j):
        d, hd = divmod(j, M_HEADS)
        q_ref, kt_ref, v_ref, _, _, mask, _ = dirs[d]
        q = q_ref[0, :, hd * M_DQK:(hd + 1) * M_DQK]
        k_t = kt_ref[0, hd * M_DQK:(hd + 1) * M_DQK, :]
        v_ext = jnp.concatenate([v_ref[0, :, hd * M_DV:(hd + 1) * M_DV], ones], axis=1)
        c_prev = c_scrs[j][...]
        u_row = u_t[j:j + 1, :]
        m_loc = jnp.max(jnp.where(mask, u_row, NEG), axis=1, keepdims=True)
        kw = (k_t.astype(F32) * wa_t[j:j + 1, :]).astype(BF16)
        dec = jnp.concatenate([decay[j:j + 1, :]] * (MLSTM_EXT // LANES), axis=1)
        c_scrs[j][...] = dec * c_prev + _dot(kw, v_ext)
        return q, v_ext, c_prev.astype(BF16), u_row, m_loc, _dot(q, k_t)

    def tail(j, q, v_ext, c_prev, u_row, m_loc, s_raw):
        d, hd = divmod(j, M_HEADS)
        h_ref, mask = dirs[d][4], dirs[d][5]
        mp_row = m_prev[j:j + 1, :]
        m_rep = jnp.maximum(jnp.broadcast_to(m_loc, (L, LANES)), mp_row)
        m_wide = jnp.concatenate([m_rep] * (L // LANES), axis=1)
        s = (s_raw * jnp.exp2(jnp.where(mask, u_row - m_wide, NEG))).astype(BF16)
        qw = (q.astype(F32) * jnp.exp2(mp_row - m_rep)).astype(BF16)
        r = _dot(s, v_ext) + _dot(qw, c_prev)
        b_rep = jnp.broadcast_to(bcums[d][:, j:j + 1], (L, LANES))
        dn = jnp.maximum(jnp.abs(r[:, M_DV:]), jnp.exp2(-(b_rep + m_rep)))
        h_ref[0, :, hd * M_DV:(hd + 1) * M_DV] = (
            r[:, :M_DV] / jnp.concatenate([dn] * (M_DV // LANES), axis=1)).astype(h_ref.dtype)

    nxt = head(0)
    for j in range(nu):
        cur = nxt
        if j + 1 < nu:
            nxt = head(j + 1)
        tail(j, *cur)

    @pl.when(step == pl.num_programs(1) - 1)
    def _():
        for j, c_scr in enumerate(c_scrs):
            cn_ref[0, j] = c_scr[...]
        mn_ref[0] = m_scr[...]


def _mlstm(q, kt, v, gf, git, gft, c0, m0):
    b, t, _ = q.shape
    L = min(MLSTM_CHUNK, t)
    nc = t // L
    fwd = lambda w: pl.BlockSpec((1, L, w), lambda bi, i: (bi, i, 0))
    bwd = lambda w: pl.BlockSpec((1, L, w), lambda bi, i: (bi, nc - 1 - i, 0))
    fwd_t = lambda w: pl.BlockSpec((1, w, L), lambda bi, i: (bi, 0, i))
    bwd_t = lambda w: pl.BlockSpec((1, w, L), lambda bi, i: (bi, 0, nc - 1 - i))
    st_c = pl.BlockSpec((1, 8, M_DQK, MLSTM_EXT), lambda bi, i: (bi, 0, 0, 0))
    st_v = pl.BlockSpec((1, 8, LANES), lambda bi, i: (bi, 0, 0))
    qk_w, v_w = M_HEADS * M_DQK, M_HEADS * M_DV
    return pl.pallas_call(
        _mlstm_kernel,
        grid=(b, nc),
        in_specs=[fwd(qk_w), fwd_t(qk_w), fwd(v_w), fwd(LANES), fwd_t(LANES), fwd_t(LANES),
                  bwd(qk_w), bwd_t(qk_w), bwd(v_w), bwd(LANES), bwd_t(LANES), bwd_t(LANES),
                  st_c, st_v],
        out_specs=[fwd(v_w), bwd(v_w), st_c, st_v],
        out_shape=[jax.ShapeDtypeStruct((b, t, v_w), BF16),
                   jax.ShapeDtypeStruct((b, t, v_w), BF16),
                   jax.ShapeDtypeStruct(c0.shape, F32),
                   jax.ShapeDtypeStruct(m0.shape, F32)],
        scratch_shapes=([pltpu.VMEM((M_DQK, MLSTM_EXT), F32) for _ in range(2 * M_HEADS)]
                        + [pltpu.VMEM((8, LANES), F32)]),
        compiler_params=_params("parallel", "arbitrary"),
        name="mlstm",
    )(q, kt, v, gf, git, gft, q, kt, v, gf, git, gft, c0, m0)


def _na_kernel(q_ref, k_ref, v_ref, kc_ref, vc_ref, bias_ref, o_ref, *, rows, nsub):
    tq = NA_ROWS * GRID_W
    nkeys = NA_KEY_ROWS * GRID_W
    last_rb = rows // NA_ROWS - 1
    kc = kc_ref[0]
    vc = vc_ref[0]
    lane = lax.broadcasted_iota(jnp.int32, (tq, LANES), 1)

    def logits(sb, hh):
        rb = pl.program_id(2) * nsub + sb
        kind = jnp.where(rb == 0, 0, jnp.where(rb == last_rb, 2, 1))
        ks = jnp.clip(rb * NA_ROWS - NA_KH // 2, 0, rows - NA_KEY_ROWS)
        kstart = pl.multiple_of(ks * GRID_W, GRID_W)
        kblk = k_ref[0, pl.ds(kstart, nkeys), :]
        q = q_ref[0, sb * tq:(sb + 1) * tq, :]
        in_head = (lane < NA_DH) if hh == 0 else (lane >= NA_DH)
        qm = jnp.where(in_head, q, jnp.zeros_like(q))
        return _dot_nt(qm, kblk) + bias_ref[hh, kind], _dot_nt(qm, kc), kstart

    chains = [(sb, hh) for sb in range(nsub) for hh in range(2)]
    nxt = logits(*chains[0])
    outs = []
    for i, (sb, hh) in enumerate(chains):
        sw, sc, kstart = nxt
        if i + 1 < len(chains):
            nxt = logits(*chains[i + 1])
        m = jnp.maximum(jnp.max(sw, axis=1, keepdims=True), jnp.max(sc, axis=1, keepdims=True))
        ew = jnp.exp2(sw - m)
        ec = jnp.exp2(sc - m)
        l = jnp.sum(ew, axis=1, keepdims=True) + jnp.sum(ec, axis=1, keepdims=True)
        vblk = v_ref[0, pl.ds(kstart, nkeys), :]
        o = _dot(ew.astype(BF16), vblk) + _dot(ec.astype(BF16), vc)
        outs.append(o / l)
        if hh == 1:
            o_ref[0, sb * tq:(sb + 1) * tq, :] = jnp.where(lane < NA_DH, outs[0], outs[1]).astype(o_ref.dtype)
            outs = []


def _na(nq, nk, nv, cnk, cnv, bias):
    b, t, w = nq.shape
    rows = t // GRID_W
    tq = NA_ROWS * GRID_W
    nrb = rows // NA_ROWS
    nsub = min(NA_SUB, nrb)
    nctx = cnk.shape[1]
    return pl.pallas_call(
        functools.partial(_na_kernel, rows=rows, nsub=nsub),
        grid=(b, w // LANES, nrb // nsub),
        in_specs=[pl.BlockSpec((1, nsub * tq, LANES), lambda bi, hp, st: (bi, st, hp)),
                  pl.BlockSpec((1, t, LANES), lambda bi, hp, st: (bi, 0, hp)),
                  pl.BlockSpec((1, t, LANES), lambda bi, hp, st: (bi, 0, hp)),
                  pl.BlockSpec((1, nctx, LANES), lambda bi, hp, st: (bi, 0, hp)),
                  pl.BlockSpec((1, nctx, LANES), lambda bi, hp, st: (bi, 0, hp)),
                  pl.BlockSpec((2,) + bias.shape[1:], lambda bi, hp, st: (hp, 0, 0, 0))],
        out_specs=pl.BlockSpec((1, nsub * tq, LANES), lambda bi, hp, st: (bi, st, hp)),
        out_shape=jax.ShapeDtypeStruct((b, t, w), BF16),
        compiler_params=_params("parallel", "parallel", "arbitrary"),
        name="na",
    )(nq, nk, nv, cnk, cnv, bias)


def _na_bias_table(na_rpb, rows):
    h = na_rpb.shape[0]
    w = GRID_W
    c = np.arange(w)[:, None]
    kj = np.arange(w)[None, :]
    cs = np.clip(c - NA_KW // 2, 0, w - NA_KW)
    col_valid = (kj >= cs) & (kj < cs + NA_KW)
    dc = np.clip(kj - c + (NA_KW - 1), 0, 2 * NA_KW - 2)
    onehot = np.zeros((2 * NA_KW - 1, w, w), np.float32)
    onehot[dc, np.arange(w)[:, None], np.arange(w)[None, :]] = 1.0
    t2 = jnp.einsum("hrd,dck->hrck", na_rpb, jnp.asarray(onehot), precision=lax.Precision.HIGHEST)
    t2 = jnp.where(jnp.asarray(col_valid)[None, None], t2 * LOG2E, NEG)
    t2 = jnp.concatenate([t2, jnp.full((h, 1, w, w), NEG, F32)], axis=1)
    invalid = 2 * NA_KH - 1
    dr_idx = np.full((3, NA_ROWS, NA_KEY_ROWS), invalid, np.int32)
    for kind, r0 in enumerate((0, NA_ROWS, rows - NA_ROWS)):
        ks = int(np.clip(r0 - NA_KH // 2, 0, rows - NA_KEY_ROWS))
        for qa in range(NA_ROWS):
            r = r0 + qa
            rs = int(np.clip(r - NA_KH // 2, 0, rows - NA_KH))
            for kl in range(NA_KEY_ROWS):
                ki = ks + kl
                if rs <= ki < rs + NA_KH:
                    dr_idx[kind, qa, kl] = ki - r + NA_KH - 1
    t2t = t2.transpose(0, 2, 1, 3)
    strips = [jnp.concatenate([t2t[:, :, int(dr), :] for dr in dr_idx[kind, qa]], axis=-1)
              for kind in range(3) for qa in range(NA_ROWS)]
    return jnp.stack(strips, axis=1).reshape(h, 3, NA_ROWS * w, NA_KEY_ROWS * w)


def _interleave(*streams):
    live = list(streams)
    while live:
        for g in list(live):
            try:
                next(g)
            except StopIteration:
                live.remove(g)


def _post_kernel(x_ref, mod_ref, hf_ref, hb_ref, mo_ref, na_ref, gm_ref, gn_ref,
                 mnw_ref, wbm_ref, wbn_ref, wout_ref, wr_ref, wsg_ref, wsu_ref, wsd_ref, rb_ref,
                 h2_ref, base_ref, e_ref, w_ref, r_ref, cnt_ref, run_scr, sc_scr):
    step = pl.program_id(0)

    @pl.when(step == 0)
    def _():
        run_scr[...] = jnp.zeros_like(run_scr)
        sc_scr[...] = jnp.zeros_like(sc_scr)

    def mixer_stages():
        hm = hf_ref[0].astype(F32) + hb_ref[0].astype(F32)
        parts = []
        for hd in range(M_HEADS):
            t = hm[:, hd * M_DV:(hd + 1) * M_DV]
            parts.append(t * lax.rsqrt(jnp.mean(t * t, axis=-1, keepdims=True) + EPS))
        y_m = jnp.concatenate(parts, axis=1) * mnw_ref[...] * mo_ref[0].astype(F32)
        a = _dot(y_m.astype(BF16), wbm_ref[...])
        bn = _dot(na_ref[0], wbn_ref[...])
        yield
        z = gm_ref[0].astype(F32) * a + gn_ref[0].astype(F32) * bn
        y = _dot(z.astype(BF16), wout_ref[...])
        yield
        x1 = x_ref[0] + mod_ref[0, 2:3, :] * y
        xn = x1 * lax.rsqrt(jnp.mean(x1 * x1, axis=-1, keepdims=True) + EPS)
        h2f = xn * (1.0 + mod_ref[0, 4:5, :]) + mod_ref[0, 3:4, :]
        h2_ref[...] = _pack_bf16_pairs(h2f)
        h2 = h2f.astype(BF16)
        sg = _dot(h2, wsg_ref[...])
        su = _dot(h2, wsu_ref[...])
        yield
        sh = sg * _sigmoid(sg) * su
        base_ref[0] = x1 + mod_ref[0, 5:6, :] * _dot(sh.astype(BF16), wsd_ref[...])
        yield
        sc_scr[...] = _sigmoid(_dot_nt(wr_ref[...], h2))

    live = jnp.where(step > 0, 1.0, 0.0)
    _interleave(_route_stages(sc_scr[...], live, rb_ref, e_ref, w_ref, r_ref, cnt_ref, run_scr),
                mixer_stages())


def _post(x, mod, hf, hb, mo, yna, gm, gn, mnw, wbm, wbn, wout, wr_t, wsg, wsu, wsd, rbias, tm):
    b, t, d = x.shape
    nt = t // tm
    n = b * t
    nblk = b * nt
    cur = lambda s: jnp.minimum(s, nblk - 1)
    prev = lambda s: jnp.maximum(s - 1, 0)
    tok = lambda w: pl.BlockSpec((1, tm, w), lambda s: (cur(s) // nt, cur(s) % nt, 0))
    rt = lambda: pl.BlockSpec((TOP_K, tm), lambda s: (0, prev(s)))
    res = [mnw, wbm, wbn, wout, wr_t, wsg, wsu, wsd, rbias]
    return pl.pallas_call(
        _post_kernel,
        grid=(nblk + 1,),
        in_specs=[tok(d), pl.BlockSpec((1, 8, d), lambda s: (cur(s) // nt, 0, 0)),
                  tok(hf.shape[2]), tok(hb.shape[2]), tok(mo.shape[2]), tok(yna.shape[2]),
                  tok(gm.shape[2]), tok(gn.shape[2])] + [_resident(a.shape) for a in res],
        out_specs=[pl.BlockSpec((tm, d // 2), lambda s: (cur(s), 0)),
                   tok(d), rt(), pl.BlockSpec((tm, TOP_K), lambda s: (prev(s), 0)), rt(),
                   pl.BlockSpec((N_EXPERTS, LANES), lambda s: (0, 0))],
        out_shape=[jax.ShapeDtypeStruct((n, d // 2), jnp.int32),
                   jax.ShapeDtypeStruct((b, t, d), F32),
                   jax.ShapeDtypeStruct((TOP_K, n), jnp.int32),
                   jax.ShapeDtypeStruct((n, TOP_K), F32),
                   jax.ShapeDtypeStruct((TOP_K, n), jnp.int32),
                   jax.ShapeDtypeStruct((N_EXPERTS, LANES), F32)],
        scratch_shapes=[pltpu.VMEM((N_EXPERTS, LANES), F32), pltpu.VMEM((N_EXPERTS, tm), F32)],
        compiler_params=_params("arbitrary"),
        name="post",
    )(x, mod, hf, hb, mo, yna, gm, gn, *res)


def _route_stages(s, live, b_ref, e_ref, w_ref, r_ref, cnt_ref, run_scr):
    tm = s.shape[1]
    sel = s + b_ref[...][:, 0:1]
    gsz = N_EXPERTS // N_GROUPS
    ninf = -jnp.inf

    x3 = sel.reshape(N_GROUPS, gsz, tm)
    r3 = lax.broadcasted_iota(jnp.int32, x3.shape, 1)
    m1 = jnp.max(x3, axis=1, keepdims=True)
    i1 = jnp.min(jnp.where(x3 == m1, r3, gsz), axis=1, keepdims=True)
    m2 = jnp.max(jnp.where(r3 == i1, ninf, x3), axis=1)
    gs = m1[:, 0, :] + m2

    gidx = lax.broadcasted_iota(jnp.int32, gs.shape, 0)
    gkeep = jnp.zeros(gs.shape, jnp.bool_)
    cur = gs
    for _ in range(TOPK_GROUPS):
        mm = jnp.max(cur, axis=0, keepdims=True)
        ii = jnp.min(jnp.where(cur == mm, gidx, N_GROUPS), axis=0, keepdims=True)
        hit = gidx == ii
        gkeep = jnp.logical_or(gkeep, hit)
        cur = jnp.where(hit, ninf, cur)
    keep = jnp.broadcast_to(gkeep[:, None, :], x3.shape).reshape(N_EXPERTS, tm)
    yield

    row = lax.broadcasted_iota(jnp.int32, s.shape, 0).astype(F32)
    cur = jnp.where(keep, sel, ninf)
    idxs, ws = [], []
    chosen_f = jnp.zeros(s.shape, F32)
    for kk in range(TOP_K):
        mm = jnp.max(cur, axis=0, keepdims=True)
        ii = jnp.min(jnp.where(cur == mm, row, float(N_EXPERTS)), axis=0, keepdims=True)
        hit = row == ii
        idxs.append(ii)
        ws.append(jnp.sum(jnp.where(hit, s, 0.0), axis=0, keepdims=True))
        chosen_f = jnp.where(hit, 1.0, chosen_f)
        cur = jnp.where(hit, ninf, cur)
        if kk % 3 == 2:
            yield
    wsum = ws[0]
    for wk in ws[1:]:
        wsum = wsum + wk

    tp =lax.broadcasted_iota(jnp.int32, (tm, tm), 0)
    tc = lax.broadcasted_iota(jnp.int32, (tm, tm), 1)
    before = jnp.where(tp < tc, 1.0, 0.0).astype(BF16)
    rank = _dot(chosen_f.astype(BF16), before) + run_scr[...][:, 0:1]
    run_scr[...] = run_scr[...] + live * jnp.sum(chosen_f, axis=1, keepdims=True)
    cnt_ref[...] = run_scr[...]

    for kk in range(TOP_K):
        e_ref[kk:kk + 1, :] = idxs[kk].astype(jnp.int32)
        r_ref[kk:kk + 1, :] = jnp.sum(jnp.where(row == idxs[kk], rank, 0.0), axis=0,
                                      keepdims=True).astype(jnp.int32)
    w_ref[...] = jnp.transpose(jnp.concatenate([wk / wsum * ROUTE_SCALE for wk in ws], axis=0))


SC_WINDOW = 128


def _sc_mesh():
    return plsc.VectorSubcoreMesh(core_axis_name="core", subcore_axis_name="subcore")


def _sc_workers():
    info = plsc.get_sparse_core_info()
    return info.num_cores, info.num_cores * info.num_subcores


def _dispatch_rows(x, top_e, rank, pstart, p_rows):
    n, w = x.shape
    kk = top_e.shape[0]
    ncores, nw = _sc_workers()
    lanes = plsc.get_sparse_core_info().num_lanes
    steps = n // nw // SC_WINDOW
    per_worker = lambda a: a.reshape(kk, nw, steps, SC_WINDOW).transpose(1, 2, 0, 3)

    @functools.partial(
        pl.kernel, mesh=_sc_mesh(),
        out_type=[jax.ShapeDtypeStruct((p_rows, w), x.dtype),
                  jax.ShapeDtypeStruct((nw, steps, kk, SC_WINDOW), jnp.int32)],
        scratch_types=[pltpu.VMEM((kk, SC_WINDOW), jnp.int32),
                       pltpu.VMEM((kk, SC_WINDOW), jnp.int32),
                       pltpu.VMEM((kk, SC_WINDOW), jnp.int32),
                       pltpu.VMEM(pstart.shape, jnp.int32),
                       pltpu.VMEM((SC_WINDOW, w), x.dtype),
                       pltpu.SemaphoreType.DMA],
        compiler_params=pltpu.CompilerParams(needs_layout_passes=False),
    )
    def scatter(x_hbm, e_hbm, r_hbm, ps_hbm, o_hbm, pos_hbm, e_v, r_v, pos_v, ps_v, rows_v, sem):
        wid = lax.axis_index("subcore") * ncores + lax.axis_index("core")
        pltpu.sync_copy(ps_hbm, ps_v)

        @pl.loop(0, steps)
        def _(s):
            base = pl.multiple_of((wid * steps + s) * SC_WINDOW, SC_WINDOW)
            pltpu.sync_copy(e_hbm.at[wid, s], e_v)
            pltpu.sync_copy(r_hbm.at[wid, s], r_v)
            pltpu.sync_copy(x_hbm.at[pl.ds(base, SC_WINDOW)], rows_v)
            for j in range(kk):
                for c in range(SC_WINDOW // lanes):
                    cols = pl.ds(c * lanes, lanes)
                    pos_v[j, cols] = plsc.load_gather(ps_v, [e_v[j, cols]]) + r_v[j, cols]
            pltpu.sync_copy(pos_v, pos_hbm.at[wid, s])
            copies = [pltpu.make_async_copy(rows_v, o_hbm.at[pos_v.at[j]], sem) for j in range(kk)]
            for cp in copies:
                cp.start()
            for cp in copies:
                cp.wait()

    out, pos4 = scatter(x, per_worker(top_e), per_worker(rank), pstart)
    return out, pos4.transpose(2, 0, 1, 3).reshape(kk, n)


def _gather_rows(x, idx):
    m = idx.shape[0]
    w = x.shape[1]
    ncores, nw = _sc_workers()
    steps = m // nw // SC_WINDOW
    idx3 = idx.reshape(nw, steps, SC_WINDOW)

    @functools.partial(
        pl.kernel, mesh=_sc_mesh(),
        out_type=jax.ShapeDtypeStruct((m, w), x.dtype),
        scratch_types=[pltpu.VMEM((steps, SC_WINDOW), jnp.int32),
                       pltpu.VMEM((SC_WINDOW, w), x.dtype),
                       pltpu.SemaphoreType.DMA],
    )
    def gather(x_hbm, i_hbm, o_hbm, idx_v, rows_v, sem):
        wid = lax.axis_index("subcore") * ncores + lax.axis_index("core")
        pltpu.sync_copy(i_hbm.at[wid], idx_v)

        @pl.loop(0, steps)
        def _(s):
            pltpu.async_copy(x_hbm.at[idx_v.at[s]], rows_v, sem).wait()
            base = pl.multiple_of((wid * steps + s) * SC_WINDOW, SC_WINDOW)
            pltpu.sync_copy(rows_v, o_hbm.at[pl.ds(base, SC_WINDOW)])

    return gather(x, idx3)


def _experts_kernel(blk0_ref, nblk_ref, cnt_ref, row0_ref, half_ref, nu_ref,
                    x_hbm, wg_ref, wu_ref, wd_ref, y_hbm,
                    xbuf, ybuf, wg_scr, wu_scr, wd_scr, in_sem, out_sem):
    e = pl.program_id(0)
    n_used = nu_ref[0]
    blk0 = blk0_ref[e]
    ns = EXPERT_SLOTS
    sizes = (MOE_BLOCK, MOE_BLOCK // 2)

    def x_copy(g, rows):
        r0 = pl.multiple_of(row0_ref[g], MOE_BLOCK // 2)
        return pltpu.make_async_copy(x_hbm.at[pl.ds(r0, rows)], xbuf.at[g % ns, pl.ds(0, rows)],
                                     in_sem.at[g % ns])

    def y_copy(g, rows):
        r0 = pl.multiple_of(row0_ref[g], MOE_BLOCK // 2)
        return pltpu.make_async_copy(ybuf.at[g % ns, pl.ds(0, rows)], y_hbm.at[pl.ds(r0, rows)],
                                     out_sem.at[g % ns])

    def by_size(g, fn):
        for is_half, rows in enumerate(sizes):
            @pl.when(half_ref[g] == is_half)
            def _():
                fn(rows)

    @pl.when(e == 0)
    def _():
        for g0 in range(ns - 1):
            @pl.when(g0 < n_used)
            def _():
                by_size(g0, lambda rows: x_copy(g0, rows).start())

    wg_scr[...] = wg_ref[0].astype(BF16)
    wu_scr[...] = wu_ref[0].astype(BF16)
    wd_scr[...] = wd_ref[0].astype(BF16)

    def block(b, carry):
        g = blk0 + b
        by_size(g, lambda rows: x_copy(g, rows).wait())

        @pl.when(g + ns - 1 < n_used)
        def _():
            by_size(g + ns - 1, lambda rows: x_copy(g + ns - 1, rows).start())

        @pl.when(g >= ns)
        def _():
            by_size(g - ns, lambda rows: y_copy(g - ns, rows).wait())

        def run(rows):
            rid = lax.broadcasted_iota(jnp.int32, (rows, xbuf.shape[2]), 0)
            xp = jnp.where(rid < cnt_ref[e] - b * MOE_BLOCK, xbuf[g % ns, 0:rows], 0)
            x = _unpack_bf16_pairs(xp).astype(BF16)
            gt = _dot(x, wg_scr[...])
            up = _dot(x, wu_scr[...])
            a = (gt * _sigmoid(gt) * up).astype(BF16)
            ybuf[g % ns, 0:rows] = _pack_bf16_pairs(_dot(a, wd_scr[...]))
            y_copy(g, rows).start()

        by_size(g, run)
        return carry

    lax.fori_loop(0, nblk_ref[e], block, 0)

    @pl.when(e == pl.num_programs(0) - 1)
    def _():
        for back in range(ns, 0, -1):
            @pl.when(n_used >= back)
            def _():
                by_size(n_used - back, lambda rows: y_copy(n_used - back, rows).wait())


def _expert_plan(counts, nb_max):
    half = MOE_BLOCK // 2
    units = (counts + half - 1) // half
    nfull, tail = units // 2, units % 2
    nblk = nfull + tail
    pend = jnp.cumsum(units * half)
    pstart = pend - units * half
    blk_end = jnp.cumsum(nblk)
    blk0 = blk_end - nblk
    g = jnp.arange(nb_max, dtype=jnp.int32)
    ne = counts.shape[0]
    eg = jnp.minimum(jnp.sum((blk_end[None, :] <= g[:, None]).astype(jnp.int32), axis=1), ne - 1)
    onehot = (eg[:, None] == jnp.arange(ne, dtype=jnp.int32)[None, :]).astype(jnp.int32)
    pick = lambda v: jnp.sum(onehot * v[None, :], axis=1)
    local = g - pick(blk0)
    is_half = ((local == pick(nfull)) & (pick(tail) == 1)).astype(jnp.int32)
    row0 = jnp.clip(pick(pstart) + local * MOE_BLOCK, 0, pend[-1] - half)
    return pstart, blk0, nblk, row0, is_half, blk_end[-1:]


def _experts(blk0, nblk, counts, row0, is_half, n_used, xs, wg, wu, wd):
    p, dp = xs.shape
    ne, d, ff = wg.shape
    grid_spec = pltpu.PrefetchScalarGridSpec(
        num_scalar_prefetch=6,
        grid=(ne,),
        in_specs=[pl.BlockSpec(memory_space=pl.ANY),
                  pl.BlockSpec((1, d, ff), lambda e, *_: (e, 0, 0)),
                  pl.BlockSpec((1, d, ff), lambda e, *_: (e, 0, 0)),
                  pl.BlockSpec((1, ff, d), lambda e, *_: (e, 0, 0))],
        out_specs=pl.BlockSpec(memory_space=pl.ANY),
        scratch_shapes=[pltpu.VMEM((EXPERT_SLOTS, MOE_BLOCK, dp), jnp.int32),
                        pltpu.VMEM((EXPERT_SLOTS, MOE_BLOCK, dp), jnp.int32),
                        pltpu.VMEM((d, ff), BF16), pltpu.VMEM((d, ff), BF16), pltpu.VMEM((ff, d), BF16),
                        pltpu.SemaphoreType.DMA((EXPERT_SLOTS,)),
                        pltpu.SemaphoreType.DMA((EXPERT_SLOTS,))],
    )
    return pl.pallas_call(
        _experts_kernel,
        grid_spec=grid_spec,
        out_shape=jax.ShapeDtypeStruct((p, dp), jnp.int32),
        compiler_params=_params("arbitrary"),
        name="experts",
    )(blk0, nblk, counts, row0, is_half, n_used, xs, wg, wu, wd)


def _combine_kernel(base_ref, mod_ref, w_ref, y_ref, o_ref):
    acc = None
    for kk in range(TOP_K):
        term = w_ref[:, kk:kk + 1] * _unpack_bf16_pairs(y_ref[kk])
        acc = term if acc is None else acc + term
    o_ref[0] = base_ref[0] + mod_ref[0, 5:6, :] * acc


def _combine(acc, bi, mod, w_tk, yg, tm):
    b, t, d = acc.shape
    nt = t // tm
    tok = pl.BlockSpec((1, tm, d), lambda i: (bi, i, 0))
    return pl.pallas_call(
        _combine_kernel,
        grid=(nt,),
        in_specs=[tok, pl.BlockSpec((1, 8, d), lambda i: (bi, 0, 0)),
                  pl.BlockSpec((tm, TOP_K), lambda i: (bi * nt + i, 0)),
                  pl.BlockSpec((TOP_K, tm, d // 2), lambda i: (0, i, 0))],
        out_specs=tok,
        out_shape=jax.ShapeDtypeStruct((b, t, d), F32),
        input_output_aliases={0: 0},
        compiler_params=_params("parallel"),
        name="combine",
    )(acc, mod, w_tk, yg)


def _rope_tables(t, tm):
    half = M_DQK // 2
    nf = half // 2
    inv = jnp.asarray(np.power(ROPE_BASE, -np.arange(nf, dtype=np.float32) / nf).astype(np.float32))
    ar = jnp.arange(t // GRID_W, dtype=F32)[:, None] * inv[None, :]
    ac = jnp.arange(GRID_W, dtype=F32)[:, None] * inv[None, :]
    zr, zc = jnp.zeros_like(ar), jnp.zeros_like(ac)
    rcos = jnp.concatenate([jnp.cos(ar), jnp.cos(ar), zr, zr], axis=1)
    rsin = jnp.concatenate([-jnp.sin(ar), jnp.sin(ar), zr, zr], axis=1)
    ccos = jnp.tile(jnp.concatenate([zc, zc, jnp.cos(ac), jnp.cos(ac)], axis=1), (tm // GRID_W, 1))
    csin = jnp.tile(jnp.concatenate([zc, zc, -jnp.sin(ac), jnp.sin(ac)], axis=1), (tm // GRID_W, 1))
    return rcos, rsin, ccos, csin


_IN_SIZES = (512, 512, 1024, 1024, 16, 512, 512, 512, 1024, 1024)
_IN_OFFS = tuple(int(v) for v in np.concatenate([[0], np.cumsum(_IN_SIZES)]))


def _arrange_kernel(w_ref, wa_ref, wt_ref):
    seg = lambda i: w_ref[:, _IN_OFFS[i]:_IN_OFFS[i + 1]]
    mq, mk, mv, mo, _, nq, nk, nv, gm, gn = [seg(i) for i in range(10)]
    g0 = _IN_OFFS[4]
    c = w_ref[:, g0:g0 + LANES]
    lane = lax.broadcasted_iota(jnp.int32, c.shape, 1)
    left4, left8 = pltpu.roll(c, LANES - 4, 1), pltpu.roll(c, LANES - 8, 1)
    gi = jnp.where(lane < 4, c, jnp.where(lane < 8, left4, 0.0))
    gf = jnp.where(lane < 4, left4, jnp.where(lane < 8, left8, 0.0))
    wa_ref[...] = jnp.concatenate([mq, mv, mo, gf, nq, nk, nv, gm, gn], axis=1).astype(BF16)
    wt_ref[...] = jnp.concatenate([jnp.transpose(mk), jnp.transpose(gi), jnp.transpose(gf)],
                                  axis=0).astype(BF16)


def _arrange_w_in(w_in, b_mgate):
    d = w_in.shape[0]
    tr = 256
    wt_rows = M_HEADS * M_DQK + 2 * LANES
    w_all, wt_all = pl.pallas_call(
        _arrange_kernel,
        grid=(d // tr,),
        in_specs=[pl.BlockSpec((tr, w_in.shape[1]), lambda i: (i, 0))],
        out_specs=[pl.BlockSpec((tr, W_COLS), lambda i: (i, 0)),
                   pl.BlockSpec((wt_rows, tr), lambda i: (0, i))],
        out_shape=[jax.ShapeDtypeStruct((d, W_COLS), BF16),
                   jax.ShapeDtypeStruct((wt_rows, d), BF16)],
        compiler_params=_params("parallel"),
        name="arrange_w_in",
    )(w_in)
    bpad = jnp.zeros((LANES - 2 * M_HEADS,), F32)
    bi = jnp.concatenate([b_mgate[0:4], b_mgate[8:12], bpad])
    bf = jnp.concatenate([b_mgate[4:8], b_mgate[12:16], bpad])
    bg = jnp.concatenate([bf[None, :], jnp.zeros((7, LANES), F32)], axis=0)
    bgt = jnp.concatenate([bi, bf])[:, None]
    return w_all, wt_all, bg, bgt


def _segment_mats():
    na_w = NA_HEADS * NA_DH
    seg = np.zeros((na_w, LANES), np.float32)
    seg[np.arange(na_w), np.arange(na_w) // NA_DH] = 1.0
    return jnp.asarray(seg, BF16), jnp.asarray(seg.T.copy(), BF16)


def kernel(x, c, ctx, c_ctx, w_ada, b_ada, w_in, b_mgate, m_norm_w, na_qn_w, na_kn_w, na_rpb,
           w_br_m, w_br_na, w_out, w_router, router_bias, w_exp_gate, w_exp_up, w_exp_down,
           w_sh_gate, w_sh_up, w_sh_down):
    b, t, d = x.shape
    n = b * t
    rows = t // GRID_W
    l = 0

    cc = jnp.concatenate([c, c_ctx[None, :], jnp.zeros((8 - b - 1, d), F32)], axis=0)
    mod = _ada(cc, w_ada[l], b_ada[l])
    mod = mod.reshape(8, 6, d)
    mod = jnp.concatenate([mod, jnp.zeros((8, 2, d), F32)], axis=1)
    mod_x = mod[:b]
    mod_c = jnp.broadcast_to(mod[b:b + 1], (b, 8, d))

    w_all, wt_all, bg, bgt = _arrange_w_in(w_in[l], b_mgate[l])
    seg, segt = _segment_mats()
    qnw = jnp.tile(na_qn_w[l], NA_HEADS)[None, :]
    knw = jnp.tile(na_kn_w[l], NA_HEADS)[None, :]
    tm = min(512, t)

    cp = _inproj(ctx, mod_c, w_all, wt_all, bg, bgt, qnw, knw, seg, segt, None,
                 min(tm, ctx.shape[1]))
    xp = _inproj(x, mod_x, w_all, wt_all, bg, bgt, qnw, knw, seg, segt, _rope_tables(t, tm), tm)
    cmq, cmv, _, cgf, _, cnk, cnv, _, _, cmkt, cgit, cgft = cp
    mq, mv, mo, gf, nq, nk, nv, gm, gn, mkt, git, gft = xp

    c0 = jnp.zeros((b, 8, M_DQK, MLSTM_EXT), F32)
    m0 = jnp.zeros((b, 8, LANES), F32)
    _, _, c1, m1 = _mlstm(cmq, cmkt, cmv, cgf, cgit, cgft, c0, m0)
    hf, hb, _, _ = _mlstm(mq, mkt, mv, gf, git, gft, c1, m1)

    yna = _na(nq, nk, nv, cnk, cnv, _na_bias_table(na_rpb[l], rows))

    bias_col = jnp.broadcast_to(router_bias[l][:, None], (N_EXPERTS, LANES))
    h2p, base, top_e, top_w, rank, cnt = _post(
        x, mod_x, hf, hb, mo, yna, gm, gn, m_norm_w[l][None, :],
        w_br_m[l].astype(BF16), w_br_na[l].astype(BF16), w_out[l].astype(BF16),
        w_router[l].T.astype(BF16), w_sh_gate[l].astype(BF16), w_sh_up[l].astype(BF16),
        w_sh_down[l].astype(BF16), bias_col, tm)

    counts = cnt[:, 0].astype(jnp.int32)
    half = MOE_BLOCK // 2
    p_rows = (-(-(n * TOP_K) // half) + N_EXPERTS) * half
    nb_max = -(-(n * TOP_K) // MOE_BLOCK) + N_EXPERTS
    pstart, blk0, nblk, row0, is_half, n_used = _expert_plan(counts, nb_max)

    xs, pos = _dispatch_rows(h2p, top_e, rank, pstart, p_rows)
    ys = _experts(blk0, nblk, counts, row0, is_half, n_used, xs,
                  w_exp_gate[l], w_exp_up[l], w_exp_down[l])
    out = base
    for bi in range(b):
        idx = pos[:, bi * t:(bi + 1) * t].reshape(-1)
        yg = _gather_rows(ys, idx).reshape(TOP_K, t, d // 2)
        out = _combine(out, bi, mod_x, top_w, yg, tm)
    return out
```

```python
import functools

import numpy as np
import jax
import jax.numpy as jnp
from jax import lax
from jax.experimental import pallas as pl
from jax.experimental.pallas import tpu as pltpu
from jax.experimental.pallas import tpu_sc as plsc

F32 = jnp.float32
BF16 = jnp.bfloat16

EPS = 1e-6
GRID_W = 64
M_HEADS, M_DQK, M_DV = 4, 128, 256
ROPE_BASE = 10000.0
NA_HEADS, NA_DH, NA_KH, NA_KW = 8, 64, 8, 16
N_EXPERTS, TOP_K, N_GROUPS, TOPK_GROUPS = 256, 8, 8, 4
ROUTE_SCALE = 2.5

LANES = 128
VMEM_LIMIT = 56 * 1024 * 1024
NEG = -1e30
LOG2E = 1.4426950408889634

MLSTM_CHUNK = 256
NA_ROWS = 4
NA_KEY_ROWS = NA_ROWS + NA_KH - 1
NA_SUB = 8
MOE_BLOCK = 512
EXPERT_SLOTS = 4

_W_SEGS = (("mq", 512), ("mv", 1024), ("mo", 1024), ("gf", 128),
           ("nq", 512), ("nk", 512), ("nv", 512), ("gm", 1024), ("gn", 1024))
_W_OFF = {}
_o = 0
for _n, _w in _W_SEGS:
    _W_OFF[_n] = (_o, _w)
    _o += _w
W_COLS = _o


def _dot(a, b):
    return jnp.dot(a, b, preferred_element_type=F32)


def _dot_nt(a, b):
    return lax.dot_general(a, b, (((1,), (1,)), ((), ())), preferred_element_type=F32)


def _sigmoid(x):
    return 1.0 / (1.0 + jnp.exp(-x))


def _pack_bf16_pairs(v):
    w = v.shape[1] // 2
    bits = pltpu.bitcast(v.astype(BF16).astype(F32), jnp.int32)
    return lax.shift_right_logical(bits[:, :w], 16) | bits[:, w:]


def _unpack_bf16_pairs(p):
    lo = pltpu.bitcast(lax.shift_left(p, 16), F32)
    hi = pltpu.bitcast(p & jnp.int32(-65536), F32)
    return jnp.concatenate([lo, hi], axis=1)


def _params(*sem):
    return pltpu.CompilerParams(dimension_semantics=sem, vmem_limit_bytes=VMEM_LIMIT)


def _resident(shape):
    nd = len(shape)
    return pl.BlockSpec(shape, lambda *_: (0,) * nd, pipeline_mode=pl.Buffered(1))


def _ada_kernel(c_ref, w_ref, b_ref, o_ref):
    c = c_ref[...]
    s = c * _sigmoid(c)
    o_ref[...] = _dot(s.astype(BF16), w_ref[...].astype(BF16)) + b_ref[...]


def _ada(cc, w_ada, b_ada):
    d = cc.shape[1]
    n = w_ada.shape[1]
    return pl.pallas_call(
        _ada_kernel,
        grid=(n // d,),
        in_specs=[pl.BlockSpec((8, d), lambda j: (0, 0)),
                  pl.BlockSpec((d, d), lambda j: (0, j)),
                  pl.BlockSpec((1, d), lambda j: (0, j))],
        out_specs=pl.BlockSpec((8, d), lambda j: (0, j)),
        out_shape=jax.ShapeDtypeStruct((8, n), F32),
        compiler_params=_params("arbitrary"),
        name="ada",
    )(cc, w_ada, b_ada.reshape(1, n))


def _rope_rotate(t, cos, sin):
    q = M_DQK // 4
    lane = lax.broadcasted_iota(jnp.int32, t.shape, 1)
    partner = jnp.where((lane & q) == 0, pltpu.roll(t, M_DQK - q, 1), pltpu.roll(t, q, 1))
    return t * cos + partner * sin


def _rope_rotate_t(t, cos, sin):
    q = M_DQK // 4
    partner = jnp.concatenate([t[q:2 * q], t[0:q], t[3 * q:4 * q], t[2 * q:3 * q]], axis=0)
    return t * cos + partner * sin


def _inproj_kernel(*refs, rope):
    if rope:
        (x_ref, mod_ref, w_ref, wt_ref, bg_ref, bgt_ref, qnw_ref, knw_ref, seg_ref, segt_ref,
         rcos_ref, rsin_ref, ccos_ref, csin_ref,
         mq_ref, mv_ref, mo_ref, gf_ref, nq_ref, nk_ref, nv_ref, gm_ref, gn_ref,
         mkt_ref, git_ref, gft_ref) = refs
    else:
        (x_ref, mod_ref, w_ref, wt_ref, bg_ref, bgt_ref, qnw_ref, knw_ref, seg_ref, segt_ref,
         mq_ref, mv_ref, mo_ref, gf_ref, nq_ref, nk_ref, nv_ref, gm_ref, gn_ref,
         mkt_ref, git_ref, gft_ref) = refs
    x = x_ref[0]
    xn = x * lax.rsqrt(jnp.mean(x * x, axis=-1, keepdims=True) + EPS)
    h = xn * (1.0 + mod_ref[0, 1:2, :]) + mod_ref[0, 0:1, :]
    hb = h.astype(BF16)

    def proj(name):
        off, width = _W_OFF[name]
        return _dot(hb, w_ref[:, off:off + width])

    def head_rms(t, w_row, scale):
        ss = _dot((t * t).astype(BF16), seg_ref[...])
        r = lax.rsqrt(ss * (1.0 / NA_DH) + EPS)
        r_hi = r.astype(BF16)
        r_lo = (r - r_hi.astype(F32)).astype(BF16)
        rb = _dot(r_hi, segt_ref[...]) + _dot(r_lo, segt_ref[...])
        return t * rb * w_row * scale

    if rope:
        tm = x.shape[0]
        spread = lambda r: jnp.broadcast_to(r[:, None, :], (tm // GRID_W, GRID_W, LANES)).reshape(tm, LANES)
        cos = spread(rcos_ref[...]) + ccos_ref[...]
        sin = spread(rsin_ref[...]) + csin_ref[...]

    def fin_mq(t):
        t = t * (M_DQK ** -0.5)
        if rope:
            t = jnp.concatenate([_rope_rotate(t[:, i * LANES:(i + 1) * LANES], cos, sin)
                                 for i in range(M_HEADS)], axis=1)
        mq_ref[0] = t.astype(BF16)

    def fin_mkt(t):
        if rope:
            cost, sint = jnp.transpose(cos), jnp.transpose(sin)
            t = jnp.concatenate([_rope_rotate_t(t[i * M_DQK:(i + 1) * M_DQK], cost, sint)
                                 for i in range(M_HEADS)], axis=0)
        mkt_ref[0] = t.astype(BF16)

    def store(ref, fn=lambda t: t):
        def fin(t):
            ref[0] = fn(t).astype(ref.dtype)
        return fin

    qk_w = M_HEADS * M_DQK
    proj_t = lambda lo, hi: (lambda: _dot_nt(wt_ref[lo:hi, :], hb))
    stages = [
        (lambda: proj("mq"), fin_mq),
        (lambda: proj("mv"), store(mv_ref)),
        (lambda: proj("mo"), store(mo_ref, _sigmoid)),
        (lambda: proj("gf"), store(gf_ref, lambda t: t + bg_ref[0:1, :])),
        (lambda: proj("nq"), store(nq_ref, lambda t: head_rms(t, qnw_ref[...], NA_DH ** -0.5 * LOG2E))),
        (lambda: proj("nk"), store(nk_ref, lambda t: head_rms(t, knw_ref[...], 1.0))),
        (lambda: proj("nv"), store(nv_ref)),
        (lambda: proj("gm"), store(gm_ref, _sigmoid)),
        (lambda: proj("gn"), store(gn_ref, _sigmoid)),
        (proj_t(0, qk_w), fin_mkt),
        (proj_t(qk_w, qk_w + LANES), store(git_ref, lambda t: t + bgt_ref[0:LANES, :])),
        (proj_t(qk_w + LANES, qk_w + 2 * LANES), store(gft_ref, lambda t: t + bgt_ref[LANES:2 * LANES, :])),
    ]
    acc = stages[0][0]()
    for i, (_, finish) in enumerate(stages):
        nxt = stages[i + 1][0]() if i + 1 < len(stages) else None
        finish(acc)
        acc = nxt


def _inproj(x, mod, w_all, wt_all, bg, bgt, qnw, knw, seg, segt, rope_tabs, tm):
    b, t, d = x.shape
    rope = rope_tabs is not None
    tok = lambda w: pl.BlockSpec((1, tm, w), lambda bi, i: (bi, i, 0))
    tok_t = lambda w: pl.BlockSpec((1, w, tm), lambda bi, i: (bi, 0, i))
    in_specs = [tok(d),
                pl.BlockSpec((1, 8, d), lambda bi, i: (bi, 0, 0)),
                _resident(w_all.shape), _resident(wt_all.shape), _resident(bg.shape),
                _resident(bgt.shape), _resident(qnw.shape),
                _resident(knw.shape), _resident(seg.shape), _resident(segt.shape)]
    args = [x, mod, w_all, wt_all, bg, bgt, qnw, knw, seg, segt]
    if rope:
        in_specs += [pl.BlockSpec((tm // GRID_W, LANES), lambda bi, i: (i, 0))] * 2
        in_specs += [_resident((tm, LANES))] * 2
        args += list(rope_tabs)
    widths = [("mq", BF16), ("mv", BF16), ("mo", BF16), ("gf", F32),
              ("nq", BF16), ("nk", BF16), ("nv", BF16), ("gm", BF16), ("gn", BF16)]
    out_specs = [tok(_W_OFF[n][1]) for n, _ in widths]
    out_shape = [jax.ShapeDtypeStruct((b, t, _W_OFF[n][1]), dt) for n, dt in widths]
    out_specs += [tok_t(M_HEADS * M_DQK), tok_t(LANES), tok_t(LANES)]
    out_shape += [jax.ShapeDtypeStruct((b, M_HEADS * M_DQK, t), BF16),
                  jax.ShapeDtypeStruct((b, LANES, t), F32),
                  jax.ShapeDtypeStruct((b, LANES, t), F32)]
    return pl.pallas_call(
        functools.partial(_inproj_kernel, rope=rope),
        grid=(b, t // tm),
        in_specs=in_specs, out_specs=out_specs, out_shape=out_shape,
        compiler_params=_params("parallel", "parallel"),
        name="inproj_rope" if rope else "inproj_ctx",
    )(*args)


def _log_sigmoid(x):
    return jnp.minimum(x, 0.0) - jnp.log(1.0 + jnp.exp(-jnp.abs(x)))


def _dot_split(a, b, split_a):
    x = a if split_a else b
    hi = x.astype(BF16)
    lo = (x - hi.astype(F32)).astype(BF16)
    return (_dot(hi, b) + _dot(lo, b)) if split_a else (_dot(a, hi) + _dot(a, lo))


MLSTM_EXT = M_DV + LANES


def _mlstm_kernel(qf_ref, ktf_ref, vf_ref, gff_ref, gitf_ref, gftf_ref,
                  qb_ref, ktb_ref, vb_ref, gfb_ref, gitb_ref, gftb_ref,
                  c0_ref, m0_ref,
                  hf_ref, hb_ref, cn_ref, mn_ref,
                  *scratch):
    c_scrs, m_scr = scratch[:2 * M_HEADS], scratch[2 * M_HEADS]
    step = pl.program_id(1)
    L = qf_ref.shape[1]
    nu = 2 * M_HEADS

    @pl.when(step == 0)
    def _():
        for j, c_scr in enumerate(c_scrs):
            c_scr[...] = c0_ref[0, j]
        m_scr[...] = m0_ref[0]

    row_i = lax.broadcasted_iota(jnp.int32, (L, L), 0)
    col_i = lax.broadcasted_iota(jnp.int32, (L, L), 1)
    lower = col_i <= row_i
    upper = col_i >= row_i
    tri_lo = jnp.where(lower, 1.0, 0.0).astype(BF16)
    tri_up = jnp.where(upper, 1.0, 0.0).astype(BF16)

    is_f = lax.broadcasted_iota(jnp.int32, (nu, L), 0) < M_HEADS
    gi_t = jnp.where(is_f, gitf_ref[0, 0:nu, :], gitb_ref[0, 0:nu, :]) * LOG2E
    ls_tf = _log_sigmoid(gftf_ref[0, 0:nu, :]) * LOG2E
    ls_tb = _log_sigmoid(gftb_ref[0, 0:nu, :]) * LOG2E
    b_t = jnp.where(is_f, _dot_split(ls_tf, tri_up, True), _dot_split(ls_tb, tri_lo, True))
    u_t = gi_t - b_t
    g_c = jnp.sum(jnp.where(is_f, ls_tf, ls_tb), axis=1, keepdims=True)
    m_prev = m_scr[...]
    a_t = g_c + u_t
    m_new = jnp.maximum(g_c + m_prev, jnp.max(a_t, axis=1, keepdims=True))
    decay = jnp.exp2(g_c + m_prev - m_new)
    wa_t = jnp.exp2(a_t - jnp.concatenate([m_new] * (L // LANES), axis=1))
    m_scr[...] = m_new

    ones = jnp.ones((L, LANES), BF16)
    dirs = ((qf_ref, ktf_ref, vf_ref, gff_ref, hf_ref, lower, tri_lo),
            (qb_ref, ktb_ref, vb_ref, gfb_ref, hb_ref, upper, tri_up))
    bcums = [_dot_split(tri, _log_sigmoid(gf_ref[0]) * LOG2E, False) for *_, gf_ref, _, _, tri in dirs]

    def head(j):
        d, hd = divmod(j, M_HEADS)
        q_ref, kt_ref, v_ref, _, _, mask, _ = dirs[d]
        q = q_ref[0, :, hd * M_DQK:(hd + 1) * M_DQK]
        k_t = kt_ref[0, hd * M_DQK:(hd + 1) * M_DQK, :]
        v_ext = jnp.concatenate([v_ref[0, :, hd * M_DV:(hd + 1) * M_DV], ones], axis=1)
        c_prev = c_scrs[j][...]
        u_row = u_t[j:j + 1, :]
        m_loc = jnp.max(jnp.where(mask, u_row, NEG), axis=1, keepdims=True)
        kw = (k_t.astype(F32) * wa_t[j:j + 1, :]).astype(BF16)
        dec = jnp.concatenate([decay[j:j + 1, :]] * (MLSTM_EXT // LANES), axis=1)
        c_scrs[j][...] = dec * c_prev + _dot(kw, v_ext)
        return q, v_ext, c_prev.astype(BF16), u_row, m_loc, _dot(q, k_t)

    def tail(j, q, v_ext, c_prev, u_row, m_loc, s_raw):
        d, hd = divmod(j, M_HEADS)
        h_ref, mask = dirs[d][4], dirs[d][5]
        mp_row = m_prev[j:j + 1, :]
        m_rep = jnp.maximum(jnp.broadcast_to(m_loc, (L, LANES)), mp_row)
        m_wide = jnp.concatenate([m_rep] * (L // LANES), axis=1)
        s = (s_raw * jnp.exp2(jnp.where(mask, u_row - m_wide, NEG))).astype(BF16)
        qw = (q.astype(F32) * jnp.exp2(mp_row - m_rep)).astype(BF16)
        r = _dot(s, v_ext) + _dot(qw, c_prev)
        b_rep = jnp.broadcast_to(bcums[d][:, j:j + 1], (L, LANES))
        dn = jnp.maximum(jnp.abs(r[:, M_DV:]), jnp.exp2(-(b_rep + m_rep)))
        h_ref[0, :, hd * M_DV:(hd + 1) * M_DV] = (
            r[:, :M_DV] / jnp.concatenate([dn] * (M_DV // LANES), axis=1)).astype(h_ref.dtype)

    nxt = head(0)
    for j in range(nu):
        cur = nxt
        if j + 1 < nu:
            nxt = head(j + 1)
        tail(j, *cur)

    @pl.when(step == pl.num_programs(1) - 1)
    def _():
        for j, c_scr in enumerate(c_scrs):
            cn_ref[0, j] = c_scr[...]
        mn_ref[0] = m_scr[...]


def _mlstm(q, kt, v, gf, git, gft, c0, m0):
    b, t, _ = q.shape
    L = min(MLSTM_CHUNK, t)
    nc = t // L
    fwd = lambda w: pl.BlockSpec((1, L, w), lambda bi, i: (bi, i, 0))
    bwd = lambda w: pl.BlockSpec((1, L, w), lambda bi, i: (bi, nc - 1 - i, 0))
    fwd_t = lambda w: pl.BlockSpec((1, w, L), lambda bi, i: (bi, 0, i))
    bwd_t = lambda w: pl.BlockSpec((1, w, L), lambda bi, i: (bi, 0, nc - 1 - i))
    st_c = pl.BlockSpec((1, 8, M_DQK, MLSTM_EXT), lambda bi, i: (bi, 0, 0, 0))
    st_v = pl.BlockSpec((1, 8, LANES), lambda bi, i: (bi, 0, 0))
    qk_w, v_w = M_HEADS * M_DQK, M_HEADS * M_DV
    return pl.pallas_call(
        _mlstm_kernel,
        grid=(b, nc),
        in_specs=[fwd(qk_w), fwd_t(qk_w), fwd(v_w), fwd(LANES), fwd_t(LANES), fwd_t(LANES),
                  bwd(qk_w), bwd_t(qk_w), bwd(v_w), bwd(LANES), bwd_t(LANES), bwd_t(LANES),
                  st_c, st_v],
        out_specs=[fwd(v_w), bwd(v_w), st_c, st_v],
        out_shape=[jax.ShapeDtypeStruct((b, t, v_w), BF16),
                   jax.ShapeDtypeStruct((b, t, v_w), BF16),
                   jax.ShapeDtypeStruct(c0.shape, F32),
                   jax.ShapeDtypeStruct(m0.shape, F32)],
        scratch_shapes=([pltpu.VMEM((M_DQK, MLSTM_EXT), F32) for _ in range(2 * M_HEADS)]
                        + [pltpu.VMEM((8, LANES), F32)]),
        compiler_params=_params("parallel", "arbitrary"),
        name="mlstm",
    )(q, kt, v, gf, git, gft, q, kt, v, gf, git, gft, c0, m0)


def _na_kernel(q_ref, k_ref, v_ref, kc_ref, vc_ref, bias_ref, o_ref, *, rows, nsub):
    tq = NA_ROWS * GRID_W
    nkeys = NA_KEY_ROWS * GRID_W
    last_rb = rows // NA_ROWS - 1
    kc = kc_ref[0]
    vc = vc_ref[0]
    lane = lax.broadcasted_iota(jnp.int32, (tq, LANES), 1)

    def logits(sb, hh):
        rb = pl.program_id(2) * nsub + sb
        kind = jnp.where(rb == 0, 0, jnp.where(rb == last_rb, 2, 1))
        ks = jnp.clip(rb * NA_ROWS - NA_KH // 2, 0, rows - NA_KEY_ROWS)
        kstart = pl.multiple_of(ks * GRID_W, GRID_W)
        kblk = k_ref[0, pl.ds(kstart, nkeys), :]
        q = q_ref[0, sb * tq:(sb + 1) * tq, :]
        in_head = (lane < NA_DH) if hh == 0 else (lane >= NA_DH)
        qm = jnp.where(in_head, q, jnp.zeros_like(q))
        return _dot_nt(qm, kblk) + bias_ref[hh, kind], _dot_nt(qm, kc), kstart

    chains = [(sb, hh) for sb in range(nsub) for hh in range(2)]
    nxt = logits(*chains[0])
    outs = []
    for i, (sb, hh) in enumerate(chains):
        sw, sc, kstart = nxt
        if i + 1 < len(chains):
            nxt = logits(*chains[i + 1])
        m = jnp.maximum(jnp.max(sw, axis=1, keepdims=True), jnp.max(sc, axis=1, keepdims=True))
        ew = jnp.exp2(sw - m)
        ec = jnp.exp2(sc - m)
        l = jnp.sum(ew, axis=1, keepdims=True) + jnp.sum(ec, axis=1, keepdims=True)
        vblk = v_ref[0, pl.ds(kstart, nkeys), :]
        o = _dot(ew.astype(BF16), vblk) + _dot(ec.astype(BF16), vc)
        outs.append(o / l)
        if hh == 1:
            o_ref[0, sb * tq:(sb + 1) * tq, :] = jnp.where(lane < NA_DH, outs[0], outs[1]).astype(o_ref.dtype)
            outs = []


def _na(nq, nk, nv, cnk, cnv, bias):
    b, t, w = nq.shape
    rows = t // GRID_W
    tq = NA_ROWS * GRID_W
    nrb = rows // NA_ROWS
    nsub = min(NA_SUB, nrb)
    nctx = cnk.shape[1]
    return pl.pallas_call(
        functools.partial(_na_kernel, rows=rows, nsub=nsub),
        grid=(b, w // LANES, nrb // nsub),
        in_specs=[pl.BlockSpec((1, nsub * tq, LANES), lambda bi, hp, st: (bi, st, hp)),
                  pl.BlockSpec((1, t, LANES), lambda bi, hp, st: (bi, 0, hp)),
                  pl.BlockSpec((1, t, LANES), lambda bi, hp, st: (bi, 0, hp)),
                  pl.BlockSpec((1, nctx, LANES), lambda bi, hp, st: (bi, 0, hp)),
                  pl.BlockSpec((1, nctx, LANES), lambda bi, hp, st: (bi, 0, hp)),
                  pl.BlockSpec((2,) + bias.shape[1:], lambda bi, hp, st: (hp, 0, 0, 0))],
        out_specs=pl.BlockSpec((1, nsub * tq, LANES), lambda bi, hp, st: (bi, st, hp)),
        out_shape=jax.ShapeDtypeStruct((b, t, w), BF16),
        compiler_params=_params("parallel", "parallel", "arbitrary"),
        name="na",
    )(nq, nk, nv, cnk, cnv, bias)


def _na_bias_table(na_rpb, rows):
    h = na_rpb.shape[0]
    w = GRID_W
    c = np.arange(w)[:, None]
    kj = np.arange(w)[None, :]
    cs = np.clip(c - NA_KW // 2, 0, w - NA_KW)
    col_valid = (kj >= cs) & (kj < cs + NA_KW)
    dc = np.clip(kj - c + (NA_KW - 1), 0, 2 * NA_KW - 2)
    onehot = np.zeros((2 * NA_KW - 1, w, w), np.float32)
    onehot[dc, np.arange(w)[:, None], np.arange(w)[None, :]] = 1.0
    t2 = jnp.einsum("hrd,dck->hrck", na_rpb, jnp.asarray(onehot), precision=lax.Precision.HIGHEST)
    t2 = jnp.where(jnp.asarray(col_valid)[None, None], t2 * LOG2E, NEG)
    t2 = jnp.concatenate([t2, jnp.full((h, 1, w, w), NEG, F32)], axis=1)
    invalid = 2 * NA_KH - 1
    dr_idx = np.full((3, NA_ROWS, NA_KEY_ROWS), invalid, np.int32)
    for kind, r0 in enumerate((0, NA_ROWS, rows - NA_ROWS)):
        ks = int(np.clip(r0 - NA_KH // 2, 0, rows - NA_KEY_ROWS))
        for qa in range(NA_ROWS):
            r = r0 + qa
            rs = int(np.clip(r - NA_KH // 2, 0, rows - NA_KH))
            for kl in range(NA_KEY_ROWS):
                ki = ks + kl
                if rs <= ki < rs + NA_KH:
                    dr_idx[kind, qa, kl] = ki - r + NA_KH - 1
    t2t = t2.transpose(0, 2, 1, 3)
    strips = [jnp.concatenate([t2t[:, :, int(dr), :] for dr in dr_idx[kind, qa]], axis=-1)
              for kind in range(3) for qa in range(NA_ROWS)]
    return jnp.stack(strips, axis=1).reshape(h, 3, NA_ROWS * w, NA_KEY_ROWS * w)


def _interleave(*streams):
    live = list(streams)
    while live:
        for g in list(live):
            try:
                next(g)
            except StopIteration:
                live.remove(g)


def _post_kernel(x_ref, mod_ref, hf_ref, hb_ref, mo_ref, na_ref, gm_ref, gn_ref,
                 mnw_ref, wbm_ref, wbn_ref, wout_ref, wr_ref, wsg_ref, wsu_ref, wsd_ref, rb_ref,
                 h2_ref, base_ref, e_ref, w_ref, r_ref, cnt_ref, run_scr):
    @pl.when(pl.program_id(0) == 0)
    def _():
        run_scr[...] = jnp.zeros_like(run_scr)

    hm = hf_ref[0].astype(F32) + hb_ref[0].astype(F32)
    parts = []
    for hd in range(M_HEADS):
        t = hm[:, hd * M_DV:(hd + 1) * M_DV]
        parts.append(t * lax.rsqrt(jnp.mean(t * t, axis=-1, keepdims=True) + EPS))
    y_m = jnp.concatenate(parts, axis=1) * mnw_ref[...] * mo_ref[0].astype(F32)
    a = _dot(y_m.astype(BF16), wbm_ref[...])
    bn = _dot(na_ref[0], wbn_ref[...])
    z = gm_ref[0].astype(F32) * a + gn_ref[0].astype(F32) * bn
    y = _dot(z.astype(BF16), wout_ref[...])
    x1 = x_ref[0] + mod_ref[0, 2:3, :] * y
    xn = x1 * lax.rsqrt(jnp.mean(x1 * x1, axis=-1, keepdims=True) + EPS)
    h2f = xn * (1.0 + mod_ref[0, 4:5, :]) + mod_ref[0, 3:4, :]
    h2_ref[...] = _pack_bf16_pairs(h2f)
    h2 = h2f.astype(BF16)
    scores = _sigmoid(_dot_nt(wr_ref[...], h2))

    def shared_expert():
        sg = _dot(h2, wsg_ref[...])
        su = _dot(h2, wsu_ref[...])
        yield
        sh = sg * _sigmoid(sg) * su
        base_ref[0] = x1 + mod_ref[0, 5:6, :] * _dot(sh.astype(BF16), wsd_ref[...])

    _interleave(_route_stages(scores, rb_ref, e_ref, w_ref, r_ref, cnt_ref, run_scr), shared_expert())


def _post(x, mod, hf, hb, mo, yna, gm, gn, mnw, wbm, wbn, wout, wr_t, wsg, wsu, wsd, rbias, tm):
    b, t, d = x.shape
    nt = t // tm
    n = b * t
    tok = lambda w: pl.BlockSpec((1, tm, w), lambda s: (s // nt, s % nt, 0))
    rt = lambda: pl.BlockSpec((TOP_K, tm), lambda s: (0, s))
    res = [mnw, wbm, wbn, wout, wr_t, wsg, wsu, wsd, rbias]
    return pl.pallas_call(
        _post_kernel,
        grid=(b * nt,),
        in_specs=[tok(d), pl.BlockSpec((1, 8, d), lambda s: (s // nt, 0, 0)),
                  tok(hf.shape[2]), tok(hb.shape[2]), tok(mo.shape[2]), tok(yna.shape[2]),
                  tok(gm.shape[2]), tok(gn.shape[2])] + [_resident(a.shape) for a in res],
        out_specs=[pl.BlockSpec((tm, d // 2), lambda s: (s, 0)),
                   tok(d), rt(), pl.BlockSpec((tm, TOP_K), lambda s: (s, 0)), rt(),
                   pl.BlockSpec((N_EXPERTS, LANES), lambda s: (0, 0))],
        out_shape=[jax.ShapeDtypeStruct((n, d // 2), jnp.int32),
                   jax.ShapeDtypeStruct((b, t, d), F32),
                   jax.ShapeDtypeStruct((TOP_K, n), jnp.int32),
                   jax.ShapeDtypeStruct((n, TOP_K), F32),
                   jax.ShapeDtypeStruct((TOP_K, n), jnp.int32),
                   jax.ShapeDtypeStruct((N_EXPERTS, LANES), F32)],
        scratch_shapes=[pltpu.VMEM((N_EXPERTS, LANES), F32)],
        compiler_params=_params("arbitrary"),
        name="post",
    )(x, mod, hf, hb, mo, yna, gm, gn, *res)


def _route_stages(s, b_ref, e_ref, w_ref, r_ref, cnt_ref, run_scr):
    tm = s.shape[1]
    sel = s + b_ref[...][:, 0:1]
    gsz = N_EXPERTS // N_GROUPS
    ninf = -jnp.inf

    x3 = sel.reshape(N_GROUPS, gsz, tm)
    r3 = lax.broadcasted_iota(jnp.int32, x3.shape, 1)
    m1 = jnp.max(x3, axis=1, keepdims=True)
    i1 = jnp.min(jnp.where(x3 == m1, r3, gsz), axis=1, keepdims=True)
    m2 = jnp.max(jnp.where(r3 == i1, ninf, x3), axis=1)
    gs = m1[:, 0, :] + m2

    gidx = lax.broadcasted_iota(jnp.int32, gs.shape, 0)
    gkeep = jnp.zeros(gs.shape, jnp.bool_)
    cur = gs
    for _ in range(TOPK_GROUPS):
        mm = jnp.max(cur, axis=0, keepdims=True)
        ii = jnp.min(jnp.where(cur == mm, gidx, N_GROUPS), axis=0, keepdims=True)
        hit = gidx == ii
        gkeep = jnp.logical_or(gkeep, hit)
        cur = jnp.where(hit, ninf, cur)
    keep = jnp.broadcast_to(gkeep[:, None, :], x3.shape).reshape(N_EXPERTS, tm)
    yield

    row = lax.broadcasted_iota(jnp.int32, s.shape, 0).astype(F32)
    cur = jnp.where(keep, sel, ninf)
    idxs, ws = [], []
    chosen_f = jnp.zeros(s.shape, F32)
    for kk in range(TOP_K):
        mm = jnp.max(cur, axis=0, keepdims=True)
        ii = jnp.min(jnp.where(cur == mm, row, float(N_EXPERTS)), axis=0, keepdims=True)
        hit = row == ii
        idxs.append(ii)
        ws.append(jnp.sum(jnp.where(hit, s, 0.0), axis=0, keepdims=True))
        chosen_f = jnp.where(hit, 1.0, chosen_f)
        cur = jnp.where(hit, ninf, cur)
        if kk == TOP_K // 2 - 1:
            yield
    wsum = ws[0]
    for wk in ws[1:]:
        wsum = wsum + wk

    tp =lax.broadcasted_iota(jnp.int32, (tm, tm), 0)
    tc = lax.broadcasted_iota(jnp.int32, (tm, tm), 1)
    before = jnp.where(tp < tc, 1.0, 0.0).astype(BF16)
    rank = _dot(chosen_f.astype(BF16), before) + run_scr[...][:, 0:1]
    run_scr[...] = run_scr[...] + jnp.sum(chosen_f, axis=1, keepdims=True)
    cnt_ref[...] = run_scr[...]

    for kk in range(TOP_K):
        e_ref[kk:kk + 1, :] = idxs[kk].astype(jnp.int32)
        r_ref[kk:kk + 1, :] = jnp.sum(jnp.where(row == idxs[kk], rank, 0.0), axis=0,
                                      keepdims=True).astype(jnp.int32)
    w_ref[...] = jnp.transpose(jnp.concatenate([wk / wsum * ROUTE_SCALE for wk in ws], axis=0))


SC_WINDOW = 128


def _sc_mesh():
    return plsc.VectorSubcoreMesh(core_axis_name="core", subcore_axis_name="subcore")


def _sc_workers():
    info = plsc.get_sparse_core_info()
    return info.num_cores, info.num_cores * info.num_subcores


def _dispatch_rows(x, top_e, rank, pstart, p_rows):
    n, w = x.shape
    kk = top_e.shape[0]
    ncores, nw = _sc_workers()
    lanes = plsc.get_sparse_core_info().num_lanes
    steps = n // nw // SC_WINDOW
    per_worker = lambda a: a.reshape(kk, nw, steps, SC_WINDOW).transpose(1, 2, 0, 3)

    @functools.partial(
        pl.kernel, mesh=_sc_mesh(),
        out_type=[jax.ShapeDtypeStruct((p_rows, w), x.dtype),
                  jax.ShapeDtypeStruct((nw, steps, kk, SC_WINDOW), jnp.int32)],
        scratch_types=[pltpu.VMEM((kk, SC_WINDOW), jnp.int32),
                       pltpu.VMEM((kk, SC_WINDOW), jnp.int32),
                       pltpu.VMEM((kk, SC_WINDOW), jnp.int32),
                       pltpu.VMEM(pstart.shape, jnp.int32),
                       pltpu.VMEM((SC_WINDOW, w), x.dtype),
                       pltpu.SemaphoreType.DMA],
        compiler_params=pltpu.CompilerParams(needs_layout_passes=False),
    )
    def scatter(x_hbm, e_hbm, r_hbm, ps_hbm, o_hbm, pos_hbm, e_v, r_v, pos_v, ps_v, rows_v, sem):
        wid = lax.axis_index("subcore") * ncores + lax.axis_index("core")
        pltpu.sync_copy(ps_hbm, ps_v)

        @pl.loop(0, steps)
        def _(s):
            base = pl.multiple_of((wid * steps + s) * SC_WINDOW, SC_WINDOW)
            pltpu.sync_copy(e_hbm.at[wid, s], e_v)
            pltpu.sync_copy(r_hbm.at[wid, s], r_v)
            pltpu.sync_copy(x_hbm.at[pl.ds(base, SC_WINDOW)], rows_v)
            for j in range(kk):
                for c in range(SC_WINDOW // lanes):
                    cols = pl.ds(c * lanes, lanes)
                    pos_v[j, cols] = plsc.load_gather(ps_v, [e_v[j, cols]]) + r_v[j, cols]
            pltpu.sync_copy(pos_v, pos_hbm.at[wid, s])
            copies = [pltpu.make_async_copy(rows_v, o_hbm.at[pos_v.at[j]], sem) for j in range(kk)]
            for cp in copies:
                cp.start()
            for cp in copies:
                cp.wait()

    out, pos4 = scatter(x, per_worker(top_e), per_worker(rank), pstart)
    return out, pos4.transpose(2, 0, 1, 3).reshape(kk, n)


def _gather_rows(x, idx):
    m = idx.shape[0]
    w = x.shape[1]
    ncores, nw = _sc_workers()
    steps = m // nw // SC_WINDOW
    idx3 = idx.reshape(nw, steps, SC_WINDOW)

    @functools.partial(
        pl.kernel, mesh=_sc_mesh(),
        out_type=jax.ShapeDtypeStruct((m, w), x.dtype),
        scratch_types=[pltpu.VMEM((steps, SC_WINDOW), jnp.int32),
                       pltpu.VMEM((SC_WINDOW, w), x.dtype),
                       pltpu.SemaphoreType.DMA],
    )
    def gather(x_hbm, i_hbm, o_hbm, idx_v, rows_v, sem):
        wid = lax.axis_index("subcore") * ncores + lax.axis_index("core")
        pltpu.sync_copy(i_hbm.at[wid], idx_v)

        @pl.loop(0, steps)
        def _(s):
            pltpu.async_copy(x_hbm.at[idx_v.at[s]], rows_v, sem).wait()
            base = pl.multiple_of((wid * steps + s) * SC_WINDOW, SC_WINDOW)
            pltpu.sync_copy(rows_v, o_hbm.at[pl.ds(base, SC_WINDOW)])

    return gather(x, idx3)


def _experts_kernel(blk0_ref, nblk_ref, cnt_ref, row0_ref, half_ref, nu_ref,
                    x_hbm, wg_ref, wu_ref, wd_ref, y_hbm,
                    xbuf, ybuf, wg_scr, wu_scr, wd_scr, in_sem, out_sem):
    e = pl.program_id(0)
    n_used = nu_ref[0]
    blk0 = blk0_ref[e]
    ns = EXPERT_SLOTS
    sizes = (MOE_BLOCK, MOE_BLOCK // 2)

    def x_copy(g, rows):
        r0 = pl.multiple_of(row0_ref[g], MOE_BLOCK // 2)
        return pltpu.make_async_copy(x_hbm.at[pl.ds(r0, rows)], xbuf.at[g % ns, pl.ds(0, rows)],
                                     in_sem.at[g % ns])

    def y_copy(g, rows):
        r0 = pl.multiple_of(row0_ref[g], MOE_BLOCK // 2)
        return pltpu.make_async_copy(ybuf.at[g % ns, pl.ds(0, rows)], y_hbm.at[pl.ds(r0, rows)],
                                     out_sem.at[g % ns])

    def by_size(g, fn):
        for is_half, rows in enumerate(sizes):
            @pl.when(half_ref[g] == is_half)
            def _():
                fn(rows)

    @pl.when(e == 0)
    def _():
        for g0 in range(ns - 1):
            @pl.when(g0 < n_used)
            def _():
                by_size(g0, lambda rows: x_copy(g0, rows).start())

    wg_scr[...] = wg_ref[0].astype(BF16)
    wu_scr[...] = wu_ref[0].astype(BF16)
    wd_scr[...] = wd_ref[0].astype(BF16)

    def block(b, carry):
        g = blk0 + b
        by_size(g, lambda rows: x_copy(g, rows).wait())

        @pl.when(g + ns - 1 < n_used)
        def _():
            by_size(g + ns - 1, lambda rows: x_copy(g + ns - 1, rows).start())

        @pl.when(g >= ns)
        def _():
            by_size(g - ns, lambda rows: y_copy(g - ns, rows).wait())

        def run(rows):
            rid = lax.broadcasted_iota(jnp.int32, (rows, xbuf.shape[2]), 0)
            xp = jnp.where(rid < cnt_ref[e] - b * MOE_BLOCK, xbuf[g % ns, 0:rows], 0)
            x = _unpack_bf16_pairs(xp).astype(BF16)
            gt = _dot(x, wg_scr[...])
            up = _dot(x, wu_scr[...])
            a = (gt * _sigmoid(gt) * up).astype(BF16)
            ybuf[g % ns, 0:rows] = _pack_bf16_pairs(_dot(a, wd_scr[...]))
            y_copy(g, rows).start()

        by_size(g, run)
        return carry

    lax.fori_loop(0, nblk_ref[e], block, 0)

    @pl.when(e == pl.num_programs(0) - 1)
    def _():
        for back in range(ns, 0, -1):
            @pl.when(n_used >= back)
            def _():
                by_size(n_used - back, lambda rows: y_copy(n_used - back, rows).wait())


def _expert_plan(counts, nb_max):
    half = MOE_BLOCK // 2
    units = (counts + half - 1) // half
    nfull, tail = units // 2, units % 2
    nblk = nfull + tail
    pend = jnp.cumsum(units * half)
    pstart = pend - units * half
    blk_end = jnp.cumsum(nblk)
    blk0 = blk_end - nblk
    g = jnp.arange(nb_max, dtype=jnp.int32)
    ne = counts.shape[0]
    eg = jnp.minimum(jnp.sum((blk_end[None, :] <= g[:, None]).astype(jnp.int32), axis=1), ne - 1)
    onehot = (eg[:, None] == jnp.arange(ne, dtype=jnp.int32)[None, :]).astype(jnp.int32)
    pick = lambda v: jnp.sum(onehot * v[None, :], axis=1)
    local = g - pick(blk0)
    is_half = ((local == pick(nfull)) & (pick(tail) == 1)).astype(jnp.int32)
    row0 = jnp.clip(pick(pstart) + local * MOE_BLOCK, 0, pend[-1] - half)
    return pstart, blk0, nblk, row0, is_half, blk_end[-1:]


def _experts(blk0, nblk, counts, row0, is_half, n_used, xs, wg, wu, wd):
    p, dp = xs.shape
    ne, d, ff = wg.shape
    grid_spec = pltpu.PrefetchScalarGridSpec(
        num_scalar_prefetch=6,
        grid=(ne,),
        in_specs=[pl.BlockSpec(memory_space=pl.ANY),
                  pl.BlockSpec((1, d, ff), lambda e, *_: (e, 0, 0)),
                  pl.BlockSpec((1, d, ff), lambda e, *_: (e, 0, 0)),
                  pl.BlockSpec((1, ff, d), lambda e, *_: (e, 0, 0))],
        out_specs=pl.BlockSpec(memory_space=pl.ANY),
        scratch_shapes=[pltpu.VMEM((EXPERT_SLOTS, MOE_BLOCK, dp), jnp.int32),
                        pltpu.VMEM((EXPERT_SLOTS, MOE_BLOCK, dp), jnp.int32),
                        pltpu.VMEM((d, ff), BF16), pltpu.VMEM((d, ff), BF16), pltpu.VMEM((ff, d), BF16),
                        pltpu.SemaphoreType.DMA((EXPERT_SLOTS,)),
                        pltpu.SemaphoreType.DMA((EXPERT_SLOTS,))],
    )
    return pl.pallas_call(
        _experts_kernel,
        grid_spec=grid_spec,
        out_shape=jax.ShapeDtypeStruct((p, dp), jnp.int32),
        compiler_params=_params("arbitrary"),
        name="experts",
    )(blk0, nblk, counts, row0, is_half, n_used, xs, wg, wu, wd)


def _combine_kernel(base_ref, mod_ref, w_ref, y_ref, o_ref):
    acc = None
    for kk in range(TOP_K):
        term = w_ref[:, kk:kk + 1] * _unpack_bf16_pairs(y_ref[kk])
        acc = term if acc is None else acc + term
    o_ref[0] = base_ref[0] + mod_ref[0, 5:6, :] * acc


def _combine(acc, bi, mod, w_tk, yg, tm):
    b, t, d = acc.shape
    nt = t // tm
    tok = pl.BlockSpec((1, tm, d), lambda i: (bi, i, 0))
    return pl.pallas_call(
        _combine_kernel,
        grid=(nt,),
        in_specs=[tok, pl.BlockSpec((1, 8, d), lambda i: (bi, 0, 0)),
                  pl.BlockSpec((tm, TOP_K), lambda i: (bi * nt + i, 0)),
                  pl.BlockSpec((TOP_K, tm, d // 2), lambda i: (0, i, 0))],
        out_specs=tok,
        out_shape=jax.ShapeDtypeStruct((b, t, d), F32),
        input_output_aliases={0: 0},
        compiler_params=_params("parallel"),
        name="combine",
    )(acc, mod, w_tk, yg)


def _rope_tables(t, tm):
    half = M_DQK // 2
    nf = half // 2
    inv = jnp.asarray(np.power(ROPE_BASE, -np.arange(nf, dtype=np.float32) / nf).astype(np.float32))
    ar = jnp.arange(t // GRID_W, dtype=F32)[:, None] * inv[None, :]
    ac = jnp.arange(GRID_W, dtype=F32)[:, None] * inv[None, :]
    zr, zc = jnp.zeros_like(ar), jnp.zeros_like(ac)
    rcos = jnp.concatenate([jnp.cos(ar), jnp.cos(ar), zr, zr], axis=1)
    rsin = jnp.concatenate([-jnp.sin(ar), jnp.sin(ar), zr, zr], axis=1)
    ccos = jnp.tile(jnp.concatenate([zc, zc, jnp.cos(ac), jnp.cos(ac)], axis=1), (tm // GRID_W, 1))
    csin = jnp.tile(jnp.concatenate([zc, zc, -jnp.sin(ac), jnp.sin(ac)], axis=1), (tm // GRID_W, 1))
    return rcos, rsin, ccos, csin


_IN_SIZES = (512, 512, 1024, 1024, 16, 512, 512, 512, 1024, 1024)
_IN_OFFS = tuple(int(v) for v in np.concatenate([[0], np.cumsum(_IN_SIZES)]))


def _arrange_kernel(w_ref, wa_ref, wt_ref):
    seg = lambda i: w_ref[:, _IN_OFFS[i]:_IN_OFFS[i + 1]]
    mq, mk, mv, mo, _, nq, nk, nv, gm, gn = [seg(i) for i in range(10)]
    g0 = _IN_OFFS[4]
    c = w_ref[:, g0:g0 + LANES]
    lane = lax.broadcasted_iota(jnp.int32, c.shape, 1)
    left4, left8 = pltpu.roll(c, LANES - 4, 1), pltpu.roll(c, LANES - 8, 1)
    gi = jnp.where(lane < 4, c, jnp.where(lane < 8, left4, 0.0))
    gf = jnp.where(lane < 4, left4, jnp.where(lane < 8, left8, 0.0))
    wa_ref[...] = jnp.concatenate([mq, mv, mo, gf, nq, nk, nv, gm, gn], axis=1).astype(BF16)
    wt_ref[...] = jnp.concatenate([jnp.transpose(mk), jnp.transpose(gi), jnp.transpose(gf)],
                                  axis=0).astype(BF16)


def _arrange_w_in(w_in, b_mgate):
    d = w_in.shape[0]
    tr = 256
    wt_rows = M_HEADS * M_DQK + 2 * LANES
    w_all, wt_all = pl.pallas_call(
        _arrange_kernel,
        grid=(d // tr,),
        in_specs=[pl.BlockSpec((tr, w_in.shape[1]), lambda i: (i, 0))],
        out_specs=[pl.BlockSpec((tr, W_COLS), lambda i: (i, 0)),
                   pl.BlockSpec((wt_rows, tr), lambda i: (0, i))],
        out_shape=[jax.ShapeDtypeStruct((d, W_COLS), BF16),
                   jax.ShapeDtypeStruct((wt_rows, d), BF16)],
        compiler_params=_params("parallel"),
        name="arrange_w_in",
    )(w_in)
    bpad = jnp.zeros((LANES - 2 * M_HEADS,), F32)
    bi = jnp.concatenate([b_mgate[0:4], b_mgate[8:12], bpad])
    bf = jnp.concatenate([b_mgate[4:8], b_mgate[12:16], bpad])
    bg = jnp.concatenate([bf[None, :], jnp.zeros((7, LANES), F32)], axis=0)
    bgt = jnp.concatenate([bi, bf])[:, None]
    return w_all, wt_all, bg, bgt


def _segment_mats():
    na_w = NA_HEADS * NA_DH
    seg = np.zeros((na_w, LANES), np.float32)
    seg[np.arange(na_w), np.arange(na_w) // NA_DH] = 1.0
    return jnp.asarray(seg, BF16), jnp.asarray(seg.T.copy(), BF16)


def kernel(x, c, ctx, c_ctx, w_ada, b_ada, w_in, b_mgate, m_norm_w, na_qn_w, na_kn_w, na_rpb,
           w_br_m, w_br_na, w_out, w_router, router_bias, w_exp_gate, w_exp_up, w_exp_down,
           w_sh_gate, w_sh_up, w_sh_down):
    b, t, d = x.shape
    n = b * t
    rows = t // GRID_W
    l = 0

    cc = jnp.concatenate([c, c_ctx[None, :], jnp.zeros((8 - b - 1, d), F32)], axis=0)
    mod = _ada(cc, w_ada[l], b_ada[l])
    mod = mod.reshape(8, 6, d)
    mod = jnp.concatenate([mod, jnp.zeros((8, 2, d), F32)], axis=1)
    mod_x = mod[:b]
    mod_c = jnp.broadcast_to(mod[b:b + 1], (b, 8, d))

    w_all, wt_all, bg, bgt = _arrange_w_in(w_in[l], b_mgate[l])
    seg, segt = _segment_mats()
    qnw = jnp.tile(na_qn_w[l], NA_HEADS)[None, :]
    knw = jnp.tile(na_kn_w[l], NA_HEADS)[None, :]
    tm = min(512, t)

    cp = _inproj(ctx, mod_c, w_all, wt_all, bg, bgt, qnw, knw, seg, segt, None,
                 min(tm, ctx.shape[1]))
    xp = _inproj(x, mod_x, w_all, wt_all, bg, bgt, qnw, knw, seg, segt, _rope_tables(t, tm), tm)
    cmq, cmv, _, cgf, _, cnk, cnv, _, _, cmkt, cgit, cgft = cp
    mq, mv, mo, gf, nq, nk, nv, gm, gn, mkt, git, gft = xp

    c0 = jnp.zeros((b, 8, M_DQK, MLSTM_EXT), F32)
    m0 = jnp.zeros((b, 8, LANES), F32)
    _, _, c1, m1 = _mlstm(cmq, cmkt, cmv, cgf, cgit, cgft, c0, m0)
    hf, hb, _, _ = _mlstm(mq, mkt, mv, gf, git, gft, c1, m1)

    yna = _na(nq, nk, nv, cnk, cnv, _na_bias_table(na_rpb[l], rows))

    bias_col = jnp.broadcast_to(router_bias[l][:, None], (N_EXPERTS, LANES))
    h2p, base, top_e, top_w, rank, cnt = _post(
        x, mod_x, hf, hb, mo, yna, gm, gn, m_norm_w[l][None, :],
        w_br_m[l].astype(BF16), w_br_na[l].astype(BF16), w_out[l].astype(BF16),
        w_router[l].T.astype(BF16), w_sh_gate[l].astype(BF16), w_sh_up[l].astype(BF16),
        w_sh_down[l].astype(BF16), bias_col, tm)

    counts = cnt[:, 0].astype(jnp.int32)
    half = MOE_BLOCK // 2
    p_rows = (-(-(n * TOP_K) // half) + N_EXPERTS) * half
    nb_max = -(-(n * TOP_K) // MOE_BLOCK) + N_EXPERTS
    pstart, blk0, nblk, row0, is_half, n_used = _expert_plan(counts, nb_max)

    xs, pos = _dispatch_rows(h2p, top_e, rank, pstart, p_rows)
    ys = _experts(blk0, nblk, counts, row0, is_half, n_used, xs,
                  w_exp_gate[l], w_exp_up[l], w_exp_down[l])
    out = base
    for bi in range(b):
        idx = pos[:, bi * t:(bi + 1) * t].reshape(-1)
        yg = _gather_rows(ys, idx).reshape(TOP_K, t, d // 2)
        out = _combine(out, bi, mod_x, top_w, yg, tm)
    return out
```

```python
import functools

import numpy as np
import jax
import jax.numpy as jnp
from jax import lax
from jax.experimental import pallas as pl
from jax.experimental.pallas import tpu as pltpu
from jax.experimental.pallas import tpu_sc as plsc

F32 = jnp.float32
BF16 = jnp.bfloat16

EPS = 1e-6
GRID_W = 64
M_HEADS, M_DQK, M_DV = 4, 128, 256
ROPE_BASE = 10000.0
NA_HEADS, NA_DH, NA_KH, NA_KW = 8, 64, 8, 16
N_EXPERTS, TOP_K, N_GROUPS, TOPK_GROUPS = 256, 8, 8, 4
ROUTE_SCALE = 2.5

LANES = 128
VMEM_LIMIT = 56 * 1024 * 1024
NEG = -1e30
LOG2E = 1.4426950408889634

MLSTM_CHUNK = 256
NA_ROWS = 4
NA_KEY_ROWS = NA_ROWS + NA_KH - 1
NA_SUB = 8
MOE_BLOCK = 512
EXPERT_SLOTS = 4
WEIGHT_SLOTS = 3

_W_SEGS = (("mq", 512), ("mv", 1024), ("mo", 1024), ("gf", 128),
           ("nq", 512), ("nk", 512), ("nv", 512), ("gm", 1024), ("gn", 1024))
_W_OFF = {}
_o = 0
for _n, _w in _W_SEGS:
    _W_OFF[_n] = (_o, _w)
    _o += _w
W_COLS = _o


def _dot(a, b):
    return jnp.dot(a, b, preferred_element_type=F32)


def _dot_nt(a, b):
    return lax.dot_general(a, b, (((1,), (1,)), ((), ())), preferred_element_type=F32)


def _sigmoid(x):
    return 1.0 / (1.0 + jnp.exp(-x))


def _pack_bf16_pairs(v):
    w = v.shape[1] // 2
    bits = pltpu.bitcast(v.astype(BF16).astype(F32), jnp.int32)
    return lax.shift_right_logical(bits[:, :w], 16) | bits[:, w:]


def _unpack_bf16_pairs(p):
    lo = pltpu.bitcast(lax.shift_left(p, 16), F32)
    hi = pltpu.bitcast(p & jnp.int32(-65536), F32)
    return jnp.concatenate([lo, hi], axis=1)


def _params(*sem):
    return pltpu.CompilerParams(dimension_semantics=sem, vmem_limit_bytes=VMEM_LIMIT)


def _resident(shape):
    nd = len(shape)
    return pl.BlockSpec(shape, lambda *_: (0,) * nd, pipeline_mode=pl.Buffered(1))


def _ada_kernel(c_ref, w_ref, b_ref, o_ref):
    c = c_ref[...]
    s = c * _sigmoid(c)
    o_ref[...] = _dot(s.astype(BF16), w_ref[...].astype(BF16)) + b_ref[...]


def _ada(cc, w_ada, b_ada):
    d = cc.shape[1]
    n = w_ada.shape[1]
    return pl.pallas_call(
        _ada_kernel,
        grid=(n // d,),
        in_specs=[pl.BlockSpec((8, d), lambda j: (0, 0)),
                  pl.BlockSpec((d, d), lambda j: (0, j)),
                  pl.BlockSpec((1, d), lambda j: (0, j))],
        out_specs=pl.BlockSpec((8, d), lambda j: (0, j)),
        out_shape=jax.ShapeDtypeStruct((8, n), F32),
        compiler_params=_params("arbitrary"),
        name="ada",
    )(cc, w_ada, b_ada.reshape(1, n))


def _rope_rotate(t, cos, sin):
    q = M_DQK // 4
    lane = lax.broadcasted_iota(jnp.int32, t.shape, 1)
    partner = jnp.where((lane & q) == 0, pltpu.roll(t, M_DQK - q, 1), pltpu.roll(t, q, 1))
    return t * cos + partner * sin


def _rope_rotate_t(t, cos, sin):
    q = M_DQK // 4
    partner = jnp.concatenate([t[q:2 * q], t[0:q], t[3 * q:4 * q], t[2 * q:3 * q]], axis=0)
    return t * cos + partner * sin


def _inproj_kernel(*refs, rope):
    if rope:
        (x_ref, mod_ref, w_ref, wt_ref, bg_ref, bgt_ref, qnw_ref, knw_ref, seg_ref, segt_ref,
         rcos_ref, rsin_ref, ccos_ref, csin_ref,
         mq_ref, mv_ref, mo_ref, gf_ref, nq_ref, nk_ref, nv_ref, gm_ref, gn_ref,
         mkt_ref, git_ref, gft_ref) = refs
    else:
        (x_ref, mod_ref, w_ref, wt_ref, bg_ref, bgt_ref, qnw_ref, knw_ref, seg_ref, segt_ref,
         mq_ref, mv_ref, mo_ref, gf_ref, nq_ref, nk_ref, nv_ref, gm_ref, gn_ref,
         mkt_ref, git_ref, gft_ref) = refs
    x = x_ref[0]
    xn = x * lax.rsqrt(jnp.mean(x * x, axis=-1, keepdims=True) + EPS)
    h = xn * (1.0 + mod_ref[0, 1:2, :]) + mod_ref[0, 0:1, :]
    hb = h.astype(BF16)

    def proj(name):
        off, width = _W_OFF[name]
        return _dot(hb, w_ref[:, off:off + width])

    def head_rms(t, w_row, scale):
        ss = _dot((t * t).astype(BF16), seg_ref[...])
        r = lax.rsqrt(ss * (1.0 / NA_DH) + EPS)
        r_hi = r.astype(BF16)
        r_lo = (r - r_hi.astype(F32)).astype(BF16)
        rb = _dot(r_hi, segt_ref[...]) + _dot(r_lo, segt_ref[...])
        return t * rb * w_row * scale

    if rope:
        tm = x.shape[0]
        spread = lambda r: jnp.broadcast_to(r[:, None, :], (tm // GRID_W, GRID_W, LANES)).reshape(tm, LANES)
        cos = spread(rcos_ref[...]) + ccos_ref[...]
        sin = spread(rsin_ref[...]) + csin_ref[...]

    def fin_mq(t):
        t = t * (M_DQK ** -0.5)
        if rope:
            t = jnp.concatenate([_rope_rotate(t[:, i * LANES:(i + 1) * LANES], cos, sin)
                                 for i in range(M_HEADS)], axis=1)
        mq_ref[0] = t.astype(BF16)

    def fin_mkt(t):
        if rope:
            cost, sint = jnp.transpose(cos), jnp.transpose(sin)
            t = jnp.concatenate([_rope_rotate_t(t[i * M_DQK:(i + 1) * M_DQK], cost, sint)
                                 for i in range(M_HEADS)], axis=0)
        mkt_ref[0] = t.astype(BF16)

    def store(ref, fn=lambda t: t):
        def fin(t):
            ref[0] = fn(t).astype(ref.dtype)
        return fin

    qk_w = M_HEADS * M_DQK
    proj_t = lambda lo, hi: (lambda: _dot_nt(wt_ref[lo:hi, :], hb))
    stages = [
        (lambda: proj("mq"), fin_mq),
        (lambda: proj("mv"), store(mv_ref)),
        (lambda: proj("mo"), store(mo_ref, _sigmoid)),
        (lambda: proj("gf"), store(gf_ref, lambda t: t + bg_ref[0:1, :])),
        (lambda: proj("nq"), store(nq_ref, lambda t: head_rms(t, qnw_ref[...], NA_DH ** -0.5 * LOG2E))),
        (lambda: proj("nk"), store(nk_ref, lambda t: head_rms(t, knw_ref[...], 1.0))),
        (lambda: proj("nv"), store(nv_ref)),
        (lambda: proj("gm"), store(gm_ref, _sigmoid)),
        (lambda: proj("gn"), store(gn_ref, _sigmoid)),
        (proj_t(0, qk_w), fin_mkt),
        (proj_t(qk_w, qk_w + LANES), store(git_ref, lambda t: t + bgt_ref[0:LANES, :])),
        (proj_t(qk_w + LANES, qk_w + 2 * LANES), store(gft_ref, lambda t: t + bgt_ref[LANES:2 * LANES, :])),
    ]
    acc = stages[0][0]()
    for i, (_, finish) in enumerate(stages):
        nxt = stages[i + 1][0]() if i + 1 < len(stages) else None
        finish(acc)
        acc = nxt


def _inproj(x, mod, w_all, wt_all, bg, bgt, qnw, knw, seg, segt, rope_tabs, tm):
    b, t, d = x.shape
    rope = rope_tabs is not None
    tok = lambda w: pl.BlockSpec((1, tm, w), lambda bi, i: (bi, i, 0))
    tok_t = lambda w: pl.BlockSpec((1, w, tm), lambda bi, i: (bi, 0, i))
    in_specs = [tok(d),
                pl.BlockSpec((1, 8, d), lambda bi, i: (bi, 0, 0)),
                _resident(w_all.shape), _resident(wt_all.shape), _resident(bg.shape),
                _resident(bgt.shape), _resident(qnw.shape),
                _resident(knw.shape), _resident(seg.shape), _resident(segt.shape)]
    args = [x, mod, w_all, wt_all, bg, bgt, qnw, knw, seg, segt]
    if rope:
        in_specs += [pl.BlockSpec((tm // GRID_W, LANES), lambda bi, i: (i, 0))] * 2
        in_specs += [_resident((tm, LANES))] * 2
        args += list(rope_tabs)
    widths = [("mq", BF16), ("mv", BF16), ("mo", BF16), ("gf", F32),
              ("nq", BF16), ("nk", BF16), ("nv", BF16), ("gm", BF16), ("gn", BF16)]
    out_specs = [tok(_W_OFF[n][1]) for n, _ in widths]
    out_shape = [jax.ShapeDtypeStruct((b, t, _W_OFF[n][1]), dt) for n, dt in widths]
    out_specs += [tok_t(M_HEADS * M_DQK), tok_t(LANES), tok_t(LANES)]
    out_shape += [jax.ShapeDtypeStruct((b, M_HEADS * M_DQK, t), BF16),
                  jax.ShapeDtypeStruct((b, LANES, t), F32),
                  jax.ShapeDtypeStruct((b, LANES, t), F32)]
    return pl.pallas_call(
        functools.partial(_inproj_kernel, rope=rope),
        grid=(b, t // tm),
        in_specs=in_specs, out_specs=out_specs, out_shape=out_shape,
        compiler_params=_params("parallel", "parallel"),
        name="inproj_rope" if rope else "inproj_ctx",
    )(*args)


def _log_sigmoid(x):
    return jnp.minimum(x, 0.0) - jnp.log(1.0 + jnp.exp(-jnp.abs(x)))


def _dot_split(a, b, split_a):
    x = a if split_a else b
    hi = x.astype(BF16)
    lo = (x - hi.astype(F32)).astype(BF16)
    return (_dot(hi, b) + _dot(lo, b)) if split_a else (_dot(a, hi) + _dot(a, lo))


MLSTM_EXT = M_DV + LANES


def _mlstm_kernel(qf_ref, ktf_ref, vf_ref, gff_ref, gitf_ref, gftf_ref,
                  qb_ref, ktb_ref, vb_ref, gfb_ref, gitb_ref, gftb_ref,
                  c0_ref, m0_ref,
                  hf_ref, hb_ref, cn_ref, mn_ref,
                  *scratch):
    c_scrs, m_scr = scratch[:2 * M_HEADS], scratch[2 * M_HEADS]
    step = pl.program_id(1)
    L = qf_ref.shape[1]
    nu = 2 * M_HEADS

    @pl.when(step == 0)
    def _():
        for j, c_scr in enumerate(c_scrs):
            c_scr[...] = c0_ref[0, j]
        m_scr[...] = m0_ref[0]

    row_i = lax.broadcasted_iota(jnp.int32, (L, L), 0)
    col_i = lax.broadcasted_iota(jnp.int32, (L, L), 1)
    lower = col_i <= row_i
    upper = col_i >= row_i
    tri_lo = jnp.where(lower, 1.0, 0.0).astype(BF16)
    tri_up = jnp.where(upper, 1.0, 0.0).astype(BF16)

    is_f = lax.broadcasted_iota(jnp.int32, (nu, L), 0) < M_HEADS
    gi_t = jnp.where(is_f, gitf_ref[0, 0:nu, :], gitb_ref[0, 0:nu, :]) * LOG2E
    ls_tf = _log_sigmoid(gftf_ref[0, 0:nu, :]) * LOG2E
    ls_tb = _log_sigmoid(gftb_ref[0, 0:nu, :]) * LOG2E
    b_t = jnp.where(is_f, _dot_split(ls_tf, tri_up, True), _dot_split(ls_tb, tri_lo, True))
    u_t = gi_t - b_t
    g_c = jnp.sum(jnp.where(is_f, ls_tf, ls_tb), axis=1, keepdims=True)
    m_prev = m_scr[...]
    a_t = g_c + u_t
    m_new = jnp.maximum(g_c + m_prev, jnp.max(a_t, axis=1, keepdims=True))
    decay = jnp.exp2(g_c + m_prev - m_new)
    wa_t = jnp.exp2(a_t - jnp.concatenate([m_new] * (L // LANES), axis=1))
    m_scr[...] = m_new

    ones = jnp.ones((L, LANES), BF16)
    dirs = ((qf_ref, ktf_ref, vf_ref, gff_ref, hf_ref, lower, tri_lo),
            (qb_ref, ktb_ref, vb_ref, gfb_ref, hb_ref, upper, tri_up))
    bcums = [_dot_split(tri, _log_sigmoid(gf_ref[0]) * LOG2E, False) for *_, gf_ref, _, _, tri in dirs]

    def head(j):
        d, hd = divmod(j, M_HEADS)
        q_ref, kt_ref, v_ref, _, _, mask, _ = dirs[d]
        q = q_ref[0, :, hd * M_DQK:(hd + 1) * M_DQK]
        k_t = kt_ref[0, hd * M_DQK:(hd + 1) * M_DQK, :]
        v_ext = jnp.concatenate([v_ref[0, :, hd * M_DV:(hd + 1) * M_DV], ones], axis=1)
        c_prev = c_scrs[j][...]
        u_row = u_t[j:j + 1, :]
        m_loc = jnp.max(jnp.where(mask, u_row, NEG), axis=1, keepdims=True)
        kw = (k_t.astype(F32) * wa_t[j:j + 1, :]).astype(BF16)
        dec = jnp.concatenate([decay[j:j + 1, :]] * (MLSTM_EXT // LANES), axis=1)
        c_scrs[j][...] = dec * c_prev + _dot(kw, v_ext)
        return q, v_ext, c_prev.astype(BF16), u_row, m_loc, _dot(q, k_t)

    def tail(j, q, v_ext, c_prev, u_row, m_loc, s_raw):
        d, hd = divmod(j, M_HEADS)
        h_ref, mask = dirs[d][4], dirs[d][5]
        mp_row = m_prev[j:j + 1, :]
        m_rep = jnp.maximum(jnp.broadcast_to(m_loc, (L, LANES)), mp_row)
        m_wide = jnp.concatenate([m_rep] * (L // LANES), axis=1)
        s = (s_raw * jnp.exp2(jnp.where(mask, u_row - m_wide, NEG))).astype(BF16)
        qw = (q.astype(F32) * jnp.exp2(mp_row - m_rep)).astype(BF16)
        r = _dot(s, v_ext) + _dot(qw, c_prev)
        b_rep = jnp.broadcast_to(bcums[d][:, j:j + 1], (L, LANES))
        dn = jnp.maximum(jnp.abs(r[:, M_DV:]), jnp.exp2(-(b_rep + m_rep)))
        h_ref[0, :, hd * M_DV:(hd + 1) * M_DV] = (
            r[:, :M_DV] / jnp.concatenate([dn] * (M_DV // LANES), axis=1)).astype(h_ref.dtype)

    nxt = head(0)
    for j in range(nu):
        cur = nxt
        if j + 1 < nu:
            nxt = head(j + 1)
        tail(j, *cur)

    @pl.when(step == pl.num_programs(1) - 1)
    def _():
        for j, c_scr in enumerate(c_scrs):
            cn_ref[0, j] = c_scr[...]
        mn_ref[0] = m_scr[...]


def _mlstm(q, kt, v, gf, git, gft, c0, m0):
    b, t, _ = q.shape
    L = min(MLSTM_CHUNK, t)
    nc = t // L
    fwd = lambda w: pl.BlockSpec((1, L, w), lambda bi, i: (bi, i, 0))
    bwd = lambda w: pl.BlockSpec((1, L, w), lambda bi, i: (bi, nc - 1 - i, 0))
    fwd_t = lambda w: pl.BlockSpec((1, w, L), lambda bi, i: (bi, 0, i))
    bwd_t = lambda w: pl.BlockSpec((1, w, L), lambda bi, i: (bi, 0, nc - 1 - i))
    st_c = pl.BlockSpec((1, 8, M_DQK, MLSTM_EXT), lambda bi, i: (bi, 0, 0, 0))
    st_v = pl.BlockSpec((1, 8, LANES), lambda bi, i: (bi, 0, 0))
    qk_w, v_w = M_HEADS * M_DQK, M_HEADS * M_DV
    return pl.pallas_call(
        _mlstm_kernel,
        grid=(b, nc),
        in_specs=[fwd(qk_w), fwd_t(qk_w), fwd(v_w), fwd(LANES), fwd_t(LANES), fwd_t(LANES),
                  bwd(qk_w), bwd_t(qk_w), bwd(v_w), bwd(LANES), bwd_t(LANES), bwd_t(LANES),
                  st_c, st_v],
        out_specs=[fwd(v_w), bwd(v_w), st_c, st_v],
        out_shape=[jax.ShapeDtypeStruct((b, t, v_w), BF16),
                   jax.ShapeDtypeStruct((b, t, v_w), BF16),
                   jax.ShapeDtypeStruct(c0.shape, F32),
                   jax.ShapeDtypeStruct(m0.shape, F32)],
        scratch_shapes=([pltpu.VMEM((M_DQK, MLSTM_EXT), F32) for _ in range(2 * M_HEADS)]
                        + [pltpu.VMEM((8, LANES), F32)]),
        compiler_params=_params("parallel", "arbitrary"),
        name="mlstm",
    )(q, kt, v, gf, git, gft, q, kt, v, gf, git, gft, c0, m0)


def _na_kernel(q_ref, k_ref, v_ref, kc_ref, vc_ref, bias_ref, o_ref, *, rows, nsub):
    tq = NA_ROWS * GRID_W
    nkeys = NA_KEY_ROWS * GRID_W
    last_rb = rows // NA_ROWS - 1
    kc = kc_ref[0]
    vc = vc_ref[0]
    lane = lax.broadcasted_iota(jnp.int32, (tq, LANES), 1)

    def logits(sb, hh):
        rb = pl.program_id(2) * nsub + sb
        kind = jnp.where(rb == 0, 0, jnp.where(rb == last_rb, 2, 1))
        ks = jnp.clip(rb * NA_ROWS - NA_KH // 2, 0, rows - NA_KEY_ROWS)
        kstart = pl.multiple_of(ks * GRID_W, GRID_W)
        kblk = k_ref[0, pl.ds(kstart, nkeys), :]
        q = q_ref[0, sb * tq:(sb + 1) * tq, :]
        in_head = (lane < NA_DH) if hh == 0 else (lane >= NA_DH)
        qm = jnp.where(in_head, q, jnp.zeros_like(q))
        return _dot_nt(qm, kblk) + bias_ref[hh, kind], _dot_nt(qm, kc), kstart

    chains = [(sb, hh) for sb in range(nsub) for hh in range(2)]
    nxt = logits(*chains[0])
    outs = []
    for i, (sb, hh) in enumerate(chains):
        sw, sc, kstart = nxt
        if i + 1 < len(chains):
            nxt = logits(*chains[i + 1])
        m = jnp.maximum(jnp.max(sw, axis=1, keepdims=True), jnp.max(sc, axis=1, keepdims=True))
        ew = jnp.exp2(sw - m)
        ec = jnp.exp2(sc - m)
        l = jnp.sum(ew, axis=1, keepdims=True) + jnp.sum(ec, axis=1, keepdims=True)
        vblk = v_ref[0, pl.ds(kstart, nkeys), :]
        o = _dot(ew.astype(BF16), vblk) + _dot(ec.astype(BF16), vc)
        outs.append(o / l)
        if hh == 1:
            o_ref[0, sb * tq:(sb + 1) * tq, :] = jnp.where(lane < NA_DH, outs[0], outs[1]).astype(o_ref.dtype)
            outs = []


def _na(nq, nk, nv, cnk, cnv, bias):
    b, t, w = nq.shape
    rows = t // GRID_W
    tq = NA_ROWS * GRID_W
    nrb = rows // NA_ROWS
    nsub = min(NA_SUB, nrb)
    nctx = cnk.shape[1]
    return pl.pallas_call(
        functools.partial(_na_kernel, rows=rows, nsub=nsub),
        grid=(b, w // LANES, nrb // nsub),
        in_specs=[pl.BlockSpec((1, nsub * tq, LANES), lambda bi, hp, st: (bi, st, hp)),
                  pl.BlockSpec((1, t, LANES), lambda bi, hp, st: (bi, 0, hp)),
                  pl.BlockSpec((1, t, LANES), lambda bi, hp, st: (bi, 0, hp)),
                  pl.BlockSpec((1, nctx, LANES), lambda bi, hp, st: (bi, 0, hp)),
                  pl.BlockSpec((1, nctx, LANES), lambda bi, hp, st: (bi, 0, hp)),
                  pl.BlockSpec((2,) + bias.shape[1:], lambda bi, hp, st: (hp, 0, 0, 0))],
        out_specs=pl.BlockSpec((1, nsub * tq, LANES), lambda bi, hp, st: (bi, st, hp)),
        out_shape=jax.ShapeDtypeStruct((b, t, w), BF16),
        compiler_params=_params("parallel", "parallel", "arbitrary"),
        name="na",
    )(nq, nk, nv, cnk, cnv, bias)


def _na_bias_table(na_rpb, rows):
    h = na_rpb.shape[0]
    w = GRID_W
    c = np.arange(w)[:, None]
    kj = np.arange(w)[None, :]
    cs = np.clip(c - NA_KW // 2, 0, w - NA_KW)
    col_valid = (kj >= cs) & (kj < cs + NA_KW)
    dc = np.clip(kj - c + (NA_KW - 1), 0, 2 * NA_KW - 2)
    onehot = np.zeros((2 * NA_KW - 1, w, w), np.float32)
    onehot[dc, np.arange(w)[:, None], np.arange(w)[None, :]] = 1.0
    t2 = jnp.einsum("hrd,dck->hrck", na_rpb, jnp.asarray(onehot), precision=lax.Precision.HIGHEST)
    t2 = jnp.where(jnp.asarray(col_valid)[None, None], t2 * LOG2E, NEG)
    t2 = jnp.concatenate([t2, jnp.full((h, 1, w, w), NEG, F32)], axis=1)
    invalid = 2 * NA_KH - 1
    dr_idx = np.full((3, NA_ROWS, NA_KEY_ROWS), invalid, np.int32)
    for kind, r0 in enumerate((0, NA_ROWS, rows - NA_ROWS)):
        ks = int(np.clip(r0 - NA_KH // 2, 0, rows - NA_KEY_ROWS))
        for qa in range(NA_ROWS):
            r = r0 + qa
            rs = int(np.clip(r - NA_KH // 2, 0, rows - NA_KH))
            for kl in range(NA_KEY_ROWS):
                ki = ks + kl
                if rs <= ki < rs + NA_KH:
                    dr_idx[kind, qa, kl] = ki - r + NA_KH - 1
    t2t = t2.transpose(0, 2, 1, 3)
    strips = [jnp.concatenate([t2t[:, :, int(dr), :] for dr in dr_idx[kind, qa]], axis=-1)
              for kind in range(3) for qa in range(NA_ROWS)]
    return jnp.stack(strips, axis=1).reshape(h, 3, NA_ROWS * w, NA_KEY_ROWS * w)


def _interleave(*streams):
    live = list(streams)
    while live:
        for g in list(live):
            try:
                next(g)
            except StopIteration:
                live.remove(g)


def _post_kernel(x_ref, mod_ref, hf_ref, hb_ref, mo_ref, na_ref, gm_ref, gn_ref,
                 mnw_ref, wbm_ref, wbn_ref, wout_ref, wr_ref, wsg_ref, wsu_ref, wsd_ref, rb_ref,
                 h2_ref, base_ref, e_ref, w_ref, r_ref, cnt_ref, run_scr):
    @pl.when(pl.program_id(0) == 0)
    def _():
        run_scr[...] = jnp.zeros_like(run_scr)

    hm = hf_ref[0].astype(F32) + hb_ref[0].astype(F32)
    parts = []
    for hd in range(M_HEADS):
        t = hm[:, hd * M_DV:(hd + 1) * M_DV]
        parts.append(t * lax.rsqrt(jnp.mean(t * t, axis=-1, keepdims=True) + EPS))
    y_m = jnp.concatenate(parts, axis=1) * mnw_ref[...] * mo_ref[0].astype(F32)
    a = _dot(y_m.astype(BF16), wbm_ref[...])
    bn = _dot(na_ref[0], wbn_ref[...])
    z = gm_ref[0].astype(F32) * a + gn_ref[0].astype(F32) * bn
    y = _dot(z.astype(BF16), wout_ref[...])
    x1 = x_ref[0] + mod_ref[0, 2:3, :] * y
    xn = x1 * lax.rsqrt(jnp.mean(x1 * x1, axis=-1, keepdims=True) + EPS)
    h2f = xn * (1.0 + mod_ref[0, 4:5, :]) + mod_ref[0, 3:4, :]
    h2_ref[...] = _pack_bf16_pairs(h2f)
    h2 = h2f.astype(BF16)
    scores = _sigmoid(_dot_nt(wr_ref[...], h2))

    def shared_expert():
        sg = _dot(h2, wsg_ref[...])
        su = _dot(h2, wsu_ref[...])
        yield
        sh = sg * _sigmoid(sg) * su
        base_ref[0] = x1 + mod_ref[0, 5:6, :] * _dot(sh.astype(BF16), wsd_ref[...])

    _interleave(_route_stages(scores, rb_ref, e_ref, w_ref, r_ref, cnt_ref, run_scr), shared_expert())


def _post(x, mod, hf, hb, mo, yna, gm, gn, mnw, wbm, wbn, wout, wr_t, wsg, wsu, wsd, rbias, tm):
    b, t, d = x.shape
    nt = t // tm
    n = b * t
    tok = lambda w: pl.BlockSpec((1, tm, w), lambda s: (s // nt, s % nt, 0))
    rt = lambda: pl.BlockSpec((TOP_K, tm), lambda s: (0, s))
    res = [mnw, wbm, wbn, wout, wr_t, wsg, wsu, wsd, rbias]
    return pl.pallas_call(
        _post_kernel,
        grid=(b * nt,),
        in_specs=[tok(d), pl.BlockSpec((1, 8, d), lambda s: (s // nt, 0, 0)),
                  tok(hf.shape[2]), tok(hb.shape[2]), tok(mo.shape[2]), tok(yna.shape[2]),
                  tok(gm.shape[2]), tok(gn.shape[2])] + [_resident(a.shape) for a in res],
        out_specs=[pl.BlockSpec((tm, d // 2), lambda s: (s, 0)),
                   tok(d), rt(), pl.BlockSpec((tm, TOP_K), lambda s: (s, 0)), rt(),
                   pl.BlockSpec((N_EXPERTS, LANES), lambda s: (0, 0))],
        out_shape=[jax.ShapeDtypeStruct((n, d // 2), jnp.int32),
                   jax.ShapeDtypeStruct((b, t, d), F32),
                   jax.ShapeDtypeStruct((TOP_K, n), jnp.int32),
                   jax.ShapeDtypeStruct((n, TOP_K), F32),
                   jax.ShapeDtypeStruct((TOP_K, n), jnp.int32),
                   jax.ShapeDtypeStruct((N_EXPERTS, LANES), F32)],
        scratch_shapes=[pltpu.VMEM((N_EXPERTS, LANES), F32)],
        compiler_params=_params("arbitrary"),
        name="post",
    )(x, mod, hf, hb, mo, yna, gm, gn, *res)


def _route_stages(s, b_ref, e_ref, w_ref, r_ref, cnt_ref, run_scr):
    tm = s.shape[1]
    sel = s + b_ref[...][:, 0:1]
    gsz = N_EXPERTS // N_GROUPS
    ninf = -jnp.inf

    x3 = sel.reshape(N_GROUPS, gsz, tm)
    r3 = lax.broadcasted_iota(jnp.int32, x3.shape, 1)
    m1 = jnp.max(x3, axis=1, keepdims=True)
    i1 = jnp.min(jnp.where(x3 == m1, r3, gsz), axis=1, keepdims=True)
    m2 = jnp.max(jnp.where(r3 == i1, ninf, x3), axis=1)
    gs = m1[:, 0, :] + m2

    gidx = lax.broadcasted_iota(jnp.int32, gs.shape, 0)
    gkeep = jnp.zeros(gs.shape, jnp.bool_)
    cur = gs
    for _ in range(TOPK_GROUPS):
        mm = jnp.max(cur, axis=0, keepdims=True)
        ii = jnp.min(jnp.where(cur == mm, gidx, N_GROUPS), axis=0, keepdims=True)
        hit = gidx == ii
        gkeep = jnp.logical_or(gkeep, hit)
        cur = jnp.where(hit, ninf, cur)
    keep = jnp.broadcast_to(gkeep[:, None, :], x3.shape).reshape(N_EXPERTS, tm)
    yield

    row = lax.broadcasted_iota(jnp.int32, s.shape, 0).astype(F32)
    cur = jnp.where(keep, sel, ninf)
    idxs, ws = [], []
    chosen_f = jnp.zeros(s.shape, F32)
    for kk in range(TOP_K):
        mm = jnp.max(cur, axis=0, keepdims=True)
        ii = jnp.min(jnp.where(cur == mm, row, float(N_EXPERTS)), axis=0, keepdims=True)
        hit = row == ii
        idxs.append(ii)
        ws.append(jnp.sum(jnp.where(hit, s, 0.0), axis=0, keepdims=True))
        chosen_f = jnp.where(hit, 1.0, chosen_f)
        cur = jnp.where(hit, ninf, cur)
        if kk == TOP_K // 2 - 1:
            yield
    wsum = ws[0]
    for wk in ws[1:]:
        wsum = wsum + wk

    tp =lax.broadcasted_iota(jnp.int32, (tm, tm), 0)
    tc = lax.broadcasted_iota(jnp.int32, (tm, tm), 1)
    before = jnp.where(tp < tc, 1.0, 0.0).astype(BF16)
    rank = _dot(chosen_f.astype(BF16), before) + run_scr[...][:, 0:1]
    run_scr[...] = run_scr[...] + jnp.sum(chosen_f, axis=1, keepdims=True)
    cnt_ref[...] = run_scr[...]

    for kk in range(TOP_K):
        e_ref[kk:kk + 1, :] = idxs[kk].astype(jnp.int32)
        r_ref[kk:kk + 1, :] = jnp.sum(jnp.where(row == idxs[kk], rank, 0.0), axis=0,
                                      keepdims=True).astype(jnp.int32)
    w_ref[...] = jnp.transpose(jnp.concatenate([wk / wsum * ROUTE_SCALE for wk in ws], axis=0))


SC_WINDOW = 128


def _sc_mesh():
    return plsc.VectorSubcoreMesh(core_axis_name="core", subcore_axis_name="subcore")


def _sc_workers():
    info = plsc.get_sparse_core_info()
    return info.num_cores, info.num_cores * info.num_subcores


def _dispatch_rows(x, top_e, rank, pstart, p_rows):
    n, w = x.shape
    kk = top_e.shape[0]
    ncores, nw = _sc_workers()
    lanes = plsc.get_sparse_core_info().num_lanes
    steps = n // nw // SC_WINDOW
    per_worker = lambda a: a.reshape(kk, nw, steps, SC_WINDOW).transpose(1, 2, 0, 3)

    @functools.partial(
        pl.kernel, mesh=_sc_mesh(),
        out_type=[jax.ShapeDtypeStruct((p_rows, w), x.dtype),
                  jax.ShapeDtypeStruct((nw, steps, kk, SC_WINDOW), jnp.int32)],
        scratch_types=[pltpu.VMEM((kk, SC_WINDOW), jnp.int32),
                       pltpu.VMEM((kk, SC_WINDOW), jnp.int32),
                       pltpu.VMEM((kk, SC_WINDOW), jnp.int32),
                       pltpu.VMEM(pstart.shape, jnp.int32),
                       pltpu.VMEM((SC_WINDOW, w), x.dtype),
                       pltpu.SemaphoreType.DMA],
        compiler_params=pltpu.CompilerParams(needs_layout_passes=False),
    )
    def scatter(x_hbm, e_hbm, r_hbm, ps_hbm, o_hbm, pos_hbm, e_v, r_v, pos_v, ps_v, rows_v, sem):
        wid = lax.axis_index("subcore") * ncores + lax.axis_index("core")
        pltpu.sync_copy(ps_hbm, ps_v)

        @pl.loop(0, steps)
        def _(s):
            base = pl.multiple_of((wid * steps + s) * SC_WINDOW, SC_WINDOW)
            pltpu.sync_copy(e_hbm.at[wid, s], e_v)
            pltpu.sync_copy(r_hbm.at[wid, s], r_v)
            pltpu.sync_copy(x_hbm.at[pl.ds(base, SC_WINDOW)], rows_v)
            for j in range(kk):
                for c in range(SC_WINDOW // lanes):
                    cols = pl.ds(c * lanes, lanes)
                    pos_v[j, cols] = plsc.load_gather(ps_v, [e_v[j, cols]]) + r_v[j, cols]
            pltpu.sync_copy(pos_v, pos_hbm.at[wid, s])
            copies = [pltpu.make_async_copy(rows_v, o_hbm.at[pos_v.at[j]], sem) for j in range(kk)]
            for cp in copies:
                cp.start()
            for cp in copies:
                cp.wait()

    out, pos4 = scatter(x, per_worker(top_e), per_worker(rank), pstart)
    return out, pos4.transpose(2, 0, 1, 3).reshape(kk, n)


def _gather_rows(x, idx):
    m = idx.shape[0]
    w = x.shape[1]
    ncores, nw = _sc_workers()
    steps = m // nw // SC_WINDOW
    idx3 = idx.reshape(nw, steps, SC_WINDOW)

    @functools.partial(
        pl.kernel, mesh=_sc_mesh(),
        out_type=jax.ShapeDtypeStruct((m, w), x.dtype),
        scratch_types=[pltpu.VMEM((steps, SC_WINDOW), jnp.int32),
                       pltpu.VMEM((SC_WINDOW, w), x.dtype),
                       pltpu.SemaphoreType.DMA],
    )
    def gather(x_hbm, i_hbm, o_hbm, idx_v, rows_v, sem):
        wid = lax.axis_index("subcore") * ncores + lax.axis_index("core")
        pltpu.sync_copy(i_hbm.at[wid], idx_v)

        @pl.loop(0, steps)
        def _(s):
            pltpu.async_copy(x_hbm.at[idx_v.at[s]], rows_v, sem).wait()
            base = pl.multiple_of((wid * steps + s) * SC_WINDOW, SC_WINDOW)
            pltpu.sync_copy(rows_v, o_hbm.at[pl.ds(base, SC_WINDOW)])

    return gather(x, idx3)


def _experts_kernel(blk0_ref, nblk_ref, cnt_ref, row0_ref, half_ref, nu_ref,
                    x_hbm, wg_hbm, wu_hbm, wd_hbm, y_hbm,
                    xbuf, ybuf, wg_raw, wu_raw, wd_raw, wg_scr, wu_scr, wd_scr, in_sem, out_sem, w_sem):
    e = pl.program_id(0)
    ne = pl.num_programs(0)
    n_used = nu_ref[0]
    blk0 = blk0_ref[e]
    ns = EXPERT_SLOTS
    sizes = (MOE_BLOCK, MOE_BLOCK // 2)

    def w_copies(ex):
        slot = ex % WEIGHT_SLOTS
        return [pltpu.make_async_copy(hbm.at[ex], raw.at[slot], w_sem.at[i, slot])
                for i, (hbm, raw) in enumerate(((wg_hbm, wg_raw), (wu_hbm, wu_raw), (wd_hbm, wd_raw)))]

    @pl.when(e == 0)
    def _():
        for e0 in range(WEIGHT_SLOTS - 1):
            @pl.when(e0 < ne)
            def _():
                for cp in w_copies(e0):
                    cp.start()

    @pl.when(e + WEIGHT_SLOTS - 1 < ne)
    def _():
        for cp in w_copies(e + WEIGHT_SLOTS - 1):
            cp.start()

    def x_copy(g, rows):
        r0 = pl.multiple_of(row0_ref[g], MOE_BLOCK // 2)
        return pltpu.make_async_copy(x_hbm.at[pl.ds(r0, rows)], xbuf.at[g % ns, pl.ds(0, rows)],
                                     in_sem.at[g % ns])

    def y_copy(g, rows):
        r0 = pl.multiple_of(row0_ref[g], MOE_BLOCK // 2)
        return pltpu.make_async_copy(ybuf.at[g % ns, pl.ds(0, rows)], y_hbm.at[pl.ds(r0, rows)],
                                     out_sem.at[g % ns])

    def by_size(g, fn):
        for is_half, rows in enumerate(sizes):
            @pl.when(half_ref[g] == is_half)
            def _():
                fn(rows)

    @pl.when(e == 0)
    def _():
        for g0 in range(ns - 1):
            @pl.when(g0 < n_used)
            def _():
                by_size(g0, lambda rows: x_copy(g0, rows).start())

    for cp in w_copies(e):
        cp.wait()
    wslot = e % WEIGHT_SLOTS
    wg_scr[...] = wg_raw[wslot].astype(BF16)
    wu_scr[...] = wu_raw[wslot].astype(BF16)
    wd_scr[...] = wd_raw[wslot].astype(BF16)

    def block(b, carry):
        g = blk0 + b
        by_size(g, lambda rows: x_copy(g, rows).wait())

        @pl.when(g + ns - 1 < n_used)
        def _():
            by_size(g + ns - 1, lambda rows: x_copy(g + ns - 1, rows).start())

        @pl.when(g >= ns)
        def _():
            by_size(g - ns, lambda rows: y_copy(g - ns, rows).wait())

        def run(rows):
            rid = lax.broadcasted_iota(jnp.int32, (rows, xbuf.shape[2]), 0)
            xp = jnp.where(rid < cnt_ref[e] - b * MOE_BLOCK, xbuf[g % ns, 0:rows], 0)
            x = _unpack_bf16_pairs(xp).astype(BF16)
            gt = _dot(x, wg_scr[...])
            up = _dot(x, wu_scr[...])
            a = (gt * _sigmoid(gt) * up).astype(BF16)
            ybuf[g % ns, 0:rows] = _pack_bf16_pairs(_dot(a, wd_scr[...]))
            y_copy(g, rows).start()

        by_size(g, run)
        return carry

    lax.fori_loop(0, nblk_ref[e], block, 0)

    @pl.when(e == pl.num_programs(0) - 1)
    def _():
        for back in range(ns, 0, -1):
            @pl.when(n_used >= back)
            def _():
                by_size(n_used - back, lambda rows: y_copy(n_used - back, rows).wait())


def _expert_plan(counts, nb_max):
    half = MOE_BLOCK // 2
    units = (counts + half - 1) // half
    nfull, tail = units // 2, units % 2
    nblk = nfull + tail
    pend = jnp.cumsum(units * half)
    pstart = pend - units * half
    blk_end = jnp.cumsum(nblk)
    blk0 = blk_end - nblk
    g = jnp.arange(nb_max, dtype=jnp.int32)
    ne = counts.shape[0]
    eg = jnp.minimum(jnp.sum((blk_end[None, :] <= g[:, None]).astype(jnp.int32), axis=1), ne - 1)
    onehot = (eg[:, None] == jnp.arange(ne, dtype=jnp.int32)[None, :]).astype(jnp.int32)
    pick = lambda v: jnp.sum(onehot * v[None, :], axis=1)
    local = g - pick(blk0)
    is_half = ((local == pick(nfull)) & (pick(tail) == 1)).astype(jnp.int32)
    row0 = jnp.clip(pick(pstart) + local * MOE_BLOCK, 0, pend[-1] - half)
    return pstart, blk0, nblk, row0, is_half, blk_end[-1:]


def _experts(blk0, nblk, counts, row0, is_half, n_used, xs, wg, wu, wd):
    p, dp = xs.shape
    ne, d, ff = wg.shape
    grid_spec = pltpu.PrefetchScalarGridSpec(
        num_scalar_prefetch=6,
        grid=(ne,),
        in_specs=[pl.BlockSpec(memory_space=pl.ANY)] * 4,
        out_specs=pl.BlockSpec(memory_space=pl.ANY),
        scratch_shapes=[pltpu.VMEM((EXPERT_SLOTS, MOE_BLOCK, dp), jnp.int32),
                        pltpu.VMEM((EXPERT_SLOTS, MOE_BLOCK, dp), jnp.int32),
                        pltpu.VMEM((WEIGHT_SLOTS, d, ff), F32), pltpu.VMEM((WEIGHT_SLOTS, d, ff), F32),
                        pltpu.VMEM((WEIGHT_SLOTS, ff, d), F32),
                        pltpu.VMEM((d, ff), BF16), pltpu.VMEM((d, ff), BF16), pltpu.VMEM((ff, d), BF16),
                        pltpu.SemaphoreType.DMA((EXPERT_SLOTS,)),
                        pltpu.SemaphoreType.DMA((EXPERT_SLOTS,)),
                        pltpu.SemaphoreType.DMA((3, WEIGHT_SLOTS))],
    )
    return pl.pallas_call(
        _experts_kernel,
        grid_spec=grid_spec,
        out_shape=jax.ShapeDtypeStruct((p, dp), jnp.int32),
        compiler_params=_params("arbitrary"),
        name="experts",
    )(blk0, nblk, counts, row0, is_half, n_used, xs, wg, wu, wd)


def _combine_kernel(base_ref, mod_ref, w_ref, y_ref, o_ref):
    acc = None
    for kk in range(TOP_K):
        term = w_ref[:, kk:kk + 1] * _unpack_bf16_pairs(y_ref[kk])
        acc = term if acc is None else acc + term
    o_ref[0] = base_ref[0] + mod_ref[0, 5:6, :] * acc


def _combine(acc, bi, mod, w_tk, yg, tm):
    b, t, d = acc.shape
    nt = t // tm
    tok = pl.BlockSpec((1, tm, d), lambda i: (bi, i, 0))
    return pl.pallas_call(
        _combine_kernel,
        grid=(nt,),
        in_specs=[tok, pl.BlockSpec((1, 8, d), lambda i: (bi, 0, 0)),
                  pl.BlockSpec((tm, TOP_K), lambda i: (bi * nt + i, 0)),
                  pl.BlockSpec((TOP_K, tm, d // 2), lambda i: (0, i, 0))],
        out_specs=tok,
        out_shape=jax.ShapeDtypeStruct((b, t, d), F32),
        input_output_aliases={0: 0},
        compiler_params=_params("parallel"),
        name="combine",
    )(acc, mod, w_tk, yg)


def _rope_tables(t, tm):
    half = M_DQK // 2
    nf = half // 2
    inv = jnp.asarray(np.power(ROPE_BASE, -np.arange(nf, dtype=np.float32) / nf).astype(np.float32))
    ar = jnp.arange(t // GRID_W, dtype=F32)[:, None] * inv[None, :]
    ac = jnp.arange(GRID_W, dtype=F32)[:, None] * inv[None, :]
    zr, zc = jnp.zeros_like(ar), jnp.zeros_like(ac)
    rcos = jnp.concatenate([jnp.cos(ar), jnp.cos(ar), zr, zr], axis=1)
    rsin = jnp.concatenate([-jnp.sin(ar), jnp.sin(ar), zr, zr], axis=1)
    ccos = jnp.tile(jnp.concatenate([zc, zc, jnp.cos(ac), jnp.cos(ac)], axis=1), (tm // GRID_W, 1))
    csin = jnp.tile(jnp.concatenate([zc, zc, -jnp.sin(ac), jnp.sin(ac)], axis=1), (tm // GRID_W, 1))
    return rcos, rsin, ccos, csin


_IN_SIZES = (512, 512, 1024, 1024, 16, 512, 512, 512, 1024, 1024)
_IN_OFFS = tuple(int(v) for v in np.concatenate([[0], np.cumsum(_IN_SIZES)]))


def _arrange_kernel(w_ref, wa_ref, wt_ref):
    seg = lambda i: w_ref[:, _IN_OFFS[i]:_IN_OFFS[i + 1]]
    mq, mk, mv, mo, _, nq, nk, nv, gm, gn = [seg(i) for i in range(10)]
    g0 = _IN_OFFS[4]
    c = w_ref[:, g0:g0 + LANES]
    lane = lax.broadcasted_iota(jnp.int32, c.shape, 1)
    left4, left8 = pltpu.roll(c, LANES - 4, 1), pltpu.roll(c, LANES - 8, 1)
    gi = jnp.where(lane < 4, c, jnp.where(lane < 8, left4, 0.0))
    gf = jnp.where(lane < 4, left4, jnp.where(lane < 8, left8, 0.0))
    wa_ref[...] = jnp.concatenate([mq, mv, mo, gf, nq, nk, nv, gm, gn], axis=1).astype(BF16)
    wt_ref[...] = jnp.concatenate([jnp.transpose(mk), jnp.transpose(gi), jnp.transpose(gf)],
                                  axis=0).astype(BF16)


def _arrange_w_in(w_in, b_mgate):
    d = w_in.shape[0]
    tr = 256
    wt_rows = M_HEADS * M_DQK + 2 * LANES
    w_all, wt_all = pl.pallas_call(
        _arrange_kernel,
        grid=(d // tr,),
        in_specs=[pl.BlockSpec((tr, w_in.shape[1]), lambda i: (i, 0))],
        out_specs=[pl.BlockSpec((tr, W_COLS), lambda i: (i, 0)),
                   pl.BlockSpec((wt_rows, tr), lambda i: (0, i))],
        out_shape=[jax.ShapeDtypeStruct((d, W_COLS), BF16),
                   jax.ShapeDtypeStruct((wt_rows, d), BF16)],
        compiler_params=_params("parallel"),
        name="arrange_w_in",
    )(w_in)
    bpad = jnp.zeros((LANES - 2 * M_HEADS,), F32)
    bi = jnp.concatenate([b_mgate[0:4], b_mgate[8:12], bpad])
    bf = jnp.concatenate([b_mgate[4:8], b_mgate[12:16], bpad])
    bg = jnp.concatenate([bf[None, :], jnp.zeros((7, LANES), F32)], axis=0)
    bgt = jnp.concatenate([bi, bf])[:, None]
    return w_all, wt_all, bg, bgt


def _segment_mats():
    na_w = NA_HEADS * NA_DH
    seg = np.zeros((na_w, LANES), np.float32)
    seg[np.arange(na_w), np.arange(na_w) // NA_DH] = 1.0
    return jnp.asarray(seg, BF16), jnp.asarray(seg.T.copy(), BF16)


def kernel(x, c, ctx, c_ctx, w_ada, b_ada, w_in, b_mgate, m_norm_w, na_qn_w, na_kn_w, na_rpb,
           w_br_m, w_br_na, w_out, w_router, router_bias, w_exp_gate, w_exp_up, w_exp_down,
           w_sh_gate, w_sh_up, w_sh_down):
    b, t, d = x.shape
    n = b * t
    rows = t // GRID_W
    l = 0

    cc = jnp.concatenate([c, c_ctx[None, :], jnp.zeros((8 - b - 1, d), F32)], axis=0)
    mod = _ada(cc, w_ada[l], b_ada[l])
    mod = mod.reshape(8, 6, d)
    mod = jnp.concatenate([mod, jnp.zeros((8, 2, d), F32)], axis=1)
    mod_x = mod[:b]
    mod_c = jnp.broadcast_to(mod[b:b + 1], (b, 8, d))

    w_all, wt_all, bg, bgt = _arrange_w_in(w_in[l], b_mgate[l])
    seg, segt = _segment_mats()
    qnw = jnp.tile(na_qn_w[l], NA_HEADS)[None, :]
    knw = jnp.tile(na_kn_w[l], NA_HEADS)[None, :]
    tm = min(512, t)

    cp = _inproj(ctx, mod_c, w_all, wt_all, bg, bgt, qnw, knw, seg, segt, None,
                 min(tm, ctx.shape[1]))
    xp = _inproj(x, mod_x, w_all, wt_all, bg, bgt, qnw, knw, seg, segt, _rope_tables(t, tm), tm)
    cmq, cmv, _, cgf, _, cnk, cnv, _, _, cmkt, cgit, cgft = cp
    mq, mv, mo, gf, nq, nk, nv, gm, gn, mkt, git, gft = xp

    c0 = jnp.zeros((b, 8, M_DQK, MLSTM_EXT), F32)
    m0 = jnp.zeros((b, 8, LANES), F32)
    _, _, c1, m1 = _mlstm(cmq, cmkt, cmv, cgf, cgit, cgft, c0, m0)
    hf, hb, _, _ = _mlstm(mq, mkt, mv, gf, git, gft, c1, m1)

    yna = _na(nq, nk, nv, cnk, cnv, _na_bias_table(na_rpb[l], rows))

    bias_col = jnp.broadcast_to(router_bias[l][:, None], (N_EXPERTS, LANES))
    h2p, base, top_e, top_w, rank, cnt = _post(
        x, mod_x, hf, hb, mo, yna, gm, gn, m_norm_w[l][None, :],
        w_br_m[l].astype(BF16), w_br_na[l].astype(BF16), w_out[l].astype(BF16),
        w_router[l].T.astype(BF16), w_sh_gate[l].astype(BF16), w_sh_up[l].astype(BF16),
        w_sh_down[l].astype(BF16), bias_col, tm)

    counts = cnt[:, 0].astype(jnp.int32)
    half = MOE_BLOCK // 2
    p_rows = (-(-(n * TOP_K) // half) + N_EXPERTS) * half
    nb_max = -(-(n * TOP_K) // MOE_BLOCK) + N_EXPERTS
    pstart, blk0, nblk, row0, is_half, n_used = _expert_plan(counts, nb_max)

    xs, pos = _dispatch_rows(h2p, top_e, rank, pstart, p_rows)
    ys = _experts(blk0, nblk, counts, row0, is_half, n_used, xs,
                  w_exp_gate[l], w_exp_up[l], w_exp_down[l])
    out = base
    for bi in range(b):
        idx = pos[:, bi * t:(bi + 1) * t].reshape(-1)
        yg = _gather_rows(ys, idx).reshape(TOP_K, t, d // 2)
        out = _combine(out, bi, mod_x, top_w, yg, tm)
    return out
```

```python
import functools

import numpy as np
import jax
import jax.numpy as jnp
from jax import lax
from jax.experimental import pallas as pl
from jax.experimental.pallas import tpu as pltpu
from jax.experimental.pallas import tpu_sc as plsc

F32 = jnp.float32
BF16 = jnp.bfloat16

EPS = 1e-6
GRID_W = 64
M_HEADS, M_DQK, M_DV = 4, 128, 256
ROPE_BASE = 10000.0
NA_HEADS, NA_DH, NA_KH, NA_KW = 8, 64, 8, 16
N_EXPERTS, TOP_K, N_GROUPS, TOPK_GROUPS = 256, 8, 8, 4
ROUTE_SCALE = 2.5

LANES = 128
VMEM_LIMIT = 56 * 1024 * 1024
NEG = -1e30
LOG2E = 1.4426950408889634

MLSTM_CHUNK = 256
NA_ROWS = 4
NA_KEY_ROWS = NA_ROWS + NA_KH - 1
NA_SUB = 8
MOE_BLOCK = 512
EXPERT_SLOTS = 6
WEIGHT_SLOTS = 4

_W_SEGS = (("mq", 512), ("mv", 1024), ("mo", 1024), ("gf", 128),
           ("nq", 512), ("nk", 512), ("nv", 512), ("gm", 1024), ("gn", 1024))
_W_OFF = {}
_o = 0
for _n, _w in _W_SEGS:
    _W_OFF[_n] = (_o, _w)
    _o += _w
W_COLS = _o


def _dot(a, b):
    return jnp.dot(a, b, preferred_element_type=F32)


def _dot_nt(a, b):
    return lax.dot_general(a, b, (((1,), (1,)), ((), ())), preferred_element_type=F32)


def _sigmoid(x):
    return 1.0 / (1.0 + jnp.exp(-x))


def _pack_bf16_pairs(v):
    w = v.shape[1] // 2
    bits = pltpu.bitcast(v.astype(BF16).astype(F32), jnp.int32)
    return lax.shift_right_logical(bits[:, :w], 16) | bits[:, w:]


def _unpack_bf16_pairs(p):
    lo = pltpu.bitcast(lax.shift_left(p, 16), F32)
    hi = pltpu.bitcast(p & jnp.int32(-65536), F32)
    return jnp.concatenate([lo, hi], axis=1)


def _params(*sem):
    return pltpu.CompilerParams(dimension_semantics=sem, vmem_limit_bytes=VMEM_LIMIT)


def _resident(shape):
    nd = len(shape)
    return pl.BlockSpec(shape, lambda *_: (0,) * nd, pipeline_mode=pl.Buffered(1))


def _ada_kernel(c_ref, w_ref, b_ref, o_ref):
    c = c_ref[...]
    s = c * _sigmoid(c)
    o_ref[...] = _dot(s.astype(BF16), w_ref[...].astype(BF16)) + b_ref[...]


def _ada(cc, w_ada, b_ada):
    d = cc.shape[1]
    n = w_ada.shape[1]
    return pl.pallas_call(
        _ada_kernel,
        grid=(n // d,),
        in_specs=[pl.BlockSpec((8, d), lambda j: (0, 0)),
                  pl.BlockSpec((d, d), lambda j: (0, j)),
                  pl.BlockSpec((1, d), lambda j: (0, j))],
        out_specs=pl.BlockSpec((8, d), lambda j: (0, j)),
        out_shape=jax.ShapeDtypeStruct((8, n), F32),
        compiler_params=_params("arbitrary"),
        name="ada",
    )(cc, w_ada, b_ada.reshape(1, n))


def _rope_rotate(t, cos, sin):
    q = M_DQK // 4
    lane = lax.broadcasted_iota(jnp.int32, t.shape, 1)
    partner = jnp.where((lane & q) == 0, pltpu.roll(t, M_DQK - q, 1), pltpu.roll(t, q, 1))
    return t * cos + partner * sin


def _rope_rotate_t(t, cos, sin):
    q = M_DQK // 4
    partner = jnp.concatenate([t[q:2 * q], t[0:q], t[3 * q:4 * q], t[2 * q:3 * q]], axis=0)
    return t * cos + partner * sin


def _inproj_kernel(*refs, rope):
    if rope:
        (x_ref, mod_ref, w_ref, wt_ref, bg_ref, bgt_ref, qnw_ref, knw_ref, seg_ref, segt_ref,
         rcos_ref, rsin_ref, ccos_ref, csin_ref,
         mq_ref, mv_ref, mo_ref, gf_ref, nq_ref, nk_ref, nv_ref, gm_ref, gn_ref,
         mkt_ref, git_ref, gft_ref) = refs
    else:
        (x_ref, mod_ref, w_ref, wt_ref, bg_ref, bgt_ref, qnw_ref, knw_ref, seg_ref, segt_ref,
         mq_ref, mv_ref, mo_ref, gf_ref, nq_ref, nk_ref, nv_ref, gm_ref, gn_ref,
         mkt_ref, git_ref, gft_ref) = refs
    x = x_ref[0]
    xn = x * lax.rsqrt(jnp.mean(x * x, axis=-1, keepdims=True) + EPS)
    h = xn * (1.0 + mod_ref[0, 1:2, :]) + mod_ref[0, 0:1, :]
    hb = h.astype(BF16)

    def proj(name):
        off, width = _W_OFF[name]
        return _dot(hb, w_ref[:, off:off + width])

    def head_rms(t, w_row, scale):
        ss = _dot((t * t).astype(BF16), seg_ref[...])
        r = lax.rsqrt(ss * (1.0 / NA_DH) + EPS)
        r_hi = r.astype(BF16)
        r_lo = (r - r_hi.astype(F32)).astype(BF16)
        rb = _dot(r_hi, segt_ref[...]) + _dot(r_lo, segt_ref[...])
        return t * rb * w_row * scale

    if rope:
        tm = x.shape[0]
        spread = lambda r: jnp.broadcast_to(r[:, None, :], (tm // GRID_W, GRID_W, LANES)).reshape(tm, LANES)
        cos = spread(rcos_ref[...]) + ccos_ref[...]
        sin = spread(rsin_ref[...]) + csin_ref[...]

    def fin_mq(t):
        t = t * (M_DQK ** -0.5)
        if rope:
            t = jnp.concatenate([_rope_rotate(t[:, i * LANES:(i + 1) * LANES], cos, sin)
                                 for i in range(M_HEADS)], axis=1)
        mq_ref[0] = t.astype(BF16)

    def fin_mkt(t):
        if rope:
            cost, sint = jnp.transpose(cos), jnp.transpose(sin)
            t = jnp.concatenate([_rope_rotate_t(t[i * M_DQK:(i + 1) * M_DQK], cost, sint)
                                 for i in range(M_HEADS)], axis=0)
        mkt_ref[0] = t.astype(BF16)

    def store(ref, fn=lambda t: t):
        def fin(t):
            ref[0] = fn(t).astype(ref.dtype)
        return fin

    qk_w = M_HEADS * M_DQK
    proj_t = lambda lo, hi: (lambda: _dot_nt(wt_ref[lo:hi, :], hb))
    stages = [
        (lambda: proj("mq"), fin_mq),
        (lambda: proj("mv"), store(mv_ref)),
        (lambda: proj("mo"), store(mo_ref, _sigmoid)),
        (lambda: proj("gf"), store(gf_ref, lambda t: t + bg_ref[0:1, :])),
        (lambda: proj("nq"), store(nq_ref, lambda t: head_rms(t, qnw_ref[...], NA_DH ** -0.5 * LOG2E))),
        (lambda: proj("nk"), store(nk_ref, lambda t: head_rms(t, knw_ref[...], 1.0))),
        (lambda: proj("nv"), store(nv_ref)),
        (lambda: proj("gm"), store(gm_ref, _sigmoid)),
        (lambda: proj("gn"), store(gn_ref, _sigmoid)),
        (proj_t(0, qk_w), fin_mkt),
        (proj_t(qk_w, qk_w + LANES), store(git_ref, lambda t: t + bgt_ref[0:LANES, :])),
        (proj_t(qk_w + LANES, qk_w + 2 * LANES), store(gft_ref, lambda t: t + bgt_ref[LANES:2 * LANES, :])),
    ]
    acc = stages[0][0]()
    for i, (_, finish) in enumerate(stages):
        nxt = stages[i + 1][0]() if i + 1 < len(stages) else None
        finish(acc)
        acc = nxt


def _inproj(x, mod, w_all, wt_all, bg, bgt, qnw, knw, seg, segt, rope_tabs, tm):
    b, t, d = x.shape
    rope = rope_tabs is not None
    tok = lambda w: pl.BlockSpec((1, tm, w), lambda bi, i: (bi, i, 0))
    tok_t = lambda w: pl.BlockSpec((1, w, tm), lambda bi, i: (bi, 0, i))
    in_specs = [tok(d),
                pl.BlockSpec((1, 8, d), lambda bi, i: (bi, 0, 0)),
                _resident(w_all.shape), _resident(wt_all.shape), _resident(bg.shape),
                _resident(bgt.shape), _resident(qnw.shape),
                _resident(knw.shape), _resident(seg.shape), _resident(segt.shape)]
    args = [x, mod, w_all, wt_all, bg, bgt, qnw, knw, seg, segt]
    if rope:
        in_specs += [pl.BlockSpec((tm // GRID_W, LANES), lambda bi, i: (i, 0))] * 2
        in_specs += [_resident((tm, LANES))] * 2
        args += list(rope_tabs)
    widths = [("mq", BF16), ("mv", BF16), ("mo", BF16), ("gf", F32),
              ("nq", BF16), ("nk", BF16), ("nv", BF16), ("gm", BF16), ("gn", BF16)]
    out_specs = [tok(_W_OFF[n][1]) for n, _ in widths]
    out_shape = [jax.ShapeDtypeStruct((b, t, _W_OFF[n][1]), dt) for n, dt in widths]
    out_specs += [tok_t(M_HEADS * M_DQK), tok_t(LANES), tok_t(LANES)]
    out_shape += [jax.ShapeDtypeStruct((b, M_HEADS * M_DQK, t), BF16),
                  jax.ShapeDtypeStruct((b, LANES, t), F32),
                  jax.ShapeDtypeStruct((b, LANES, t), F32)]
    return pl.pallas_call(
        functools.partial(_inproj_kernel, rope=rope),
        grid=(b, t // tm),
        in_specs=in_specs, out_specs=out_specs, out_shape=out_shape,
        compiler_params=_params("parallel", "parallel"),
        name="inproj_rope" if rope else "inproj_ctx",
    )(*args)


def _log_sigmoid(x):
    return jnp.minimum(x, 0.0) - jnp.log(1.0 + jnp.exp(-jnp.abs(x)))


def _dot_split(a, b, split_a):
    x = a if split_a else b
    hi = x.astype(BF16)
    lo = (x - hi.astype(F32)).astype(BF16)
    return (_dot(hi, b) + _dot(lo, b)) if split_a else (_dot(a, hi) + _dot(a, lo))


MLSTM_EXT = M_DV + LANES


def _mlstm_kernel(qf_ref, ktf_ref, vf_ref, gff_ref, gitf_ref, gftf_ref,
                  qb_ref, ktb_ref, vb_ref, gfb_ref, gitb_ref, gftb_ref,
                  c0_ref, m0_ref,
                  hf_ref, hb_ref, cn_ref, mn_ref,
                  *scratch):
    c_scrs, m_scr = scratch[:2 * M_HEADS], scratch[2 * M_HEADS]
    step = pl.program_id(1)
    L = qf_ref.shape[1]
    nu = 2 * M_HEADS

    @pl.when(step == 0)
    def _():
        for j, c_scr in enumerate(c_scrs):
            c_scr[...] = c0_ref[0, j]
        m_scr[...] = m0_ref[0]

    row_i = lax.broadcasted_iota(jnp.int32, (L, L), 0)
    col_i = lax.broadcasted_iota(jnp.int32, (L, L), 1)
    lower = col_i <= row_i
    upper = col_i >= row_i
    tri_lo = jnp.where(lower, 1.0, 0.0).astype(BF16)
    tri_up = jnp.where(upper, 1.0, 0.0).astype(BF16)

    is_f = lax.broadcasted_iota(jnp.int32, (nu, L), 0) < M_HEADS
    gi_t = jnp.where(is_f, gitf_ref[0, 0:nu, :], gitb_ref[0, 0:nu, :]) * LOG2E
    ls_tf = _log_sigmoid(gftf_ref[0, 0:nu, :]) * LOG2E
    ls_tb = _log_sigmoid(gftb_ref[0, 0:nu, :]) * LOG2E
    b_t = jnp.where(is_f, _dot_split(ls_tf, tri_up, True), _dot_split(ls_tb, tri_lo, True))
    u_t = gi_t - b_t
    g_c = jnp.sum(jnp.where(is_f, ls_tf, ls_tb), axis=1, keepdims=True)
    m_prev = m_scr[...]
    a_t = g_c + u_t
    m_new = jnp.maximum(g_c + m_prev, jnp.max(a_t, axis=1, keepdims=True))
    decay = jnp.exp2(g_c + m_prev - m_new)
    wa_t = jnp.exp2(a_t - jnp.concatenate([m_new] * (L // LANES), axis=1))
    m_scr[...] = m_new

    ones = jnp.ones((L, LANES), BF16)
    dirs = ((qf_ref, ktf_ref, vf_ref, gff_ref, hf_ref, lower, tri_lo),
            (qb_ref, ktb_ref, vb_ref, gfb_ref, hb_ref, upper, tri_up))
    bcums = [_dot_split(tri, _log_sigmoid(gf_ref[0]) * LOG2E, False) for *_, gf_ref, _, _, tri in dirs]

    def head(j):
        d, hd = divmod(j, M_HEADS)
        q_ref, kt_ref, v_ref, _, _, mask, _ = dirs[d]
        q = q_ref[0, :, hd * M_DQK:(hd + 1) * M_DQK]
        k_t = kt_ref[0, hd * M_DQK:(hd + 1) * M_DQK, :]
        v_ext = jnp.concatenate([v_ref[0, :, hd * M_DV:(hd + 1) * M_DV], ones], axis=1)
        c_prev = c_scrs[j][...]
        u_row = u_t[j:j + 1, :]
        m_loc = jnp.max(jnp.where(mask, u_row, NEG), axis=1, keepdims=True)
        kw = (k_t.astype(F32) * wa_t[j:j + 1, :]).astype(BF16)
        dec = jnp.concatenate([decay[j:j + 1, :]] * (MLSTM_EXT // LANES), axis=1)
        c_scrs[j][...] = dec * c_prev + _dot(kw, v_ext)
        return q, v_ext, c_prev.astype(BF16), u_row, m_loc, _dot(q, k_t)

    def tail(j, q, v_ext, c_prev, u_row, m_loc, s_raw):
        d, hd = divmod(j, M_HEADS)
        h_ref, mask = dirs[d][4], dirs[d][5]
        mp_row = m_prev[j:j + 1, :]
        m_rep = jnp.maximum(jnp.broadcast_to(m_loc, (L, LANES)), mp_row)
        m_wide = jnp.concatenate([m_rep] * (L // LANES), axis=1)
        s = (s_raw * jnp.exp2(jnp.where(mask, u_row - m_wide, NEG))).astype(BF16)
        qw = (q.astype(F32) * jnp.exp2(mp_row - m_rep)).astype(BF16)
        r = _dot(s, v_ext) + _dot(qw, c_prev)
        b_rep = jnp.broadcast_to(bcums[d][:, j:j + 1], (L, LANES))
        dn = jnp.maximum(jnp.abs(r[:, M_DV:]), jnp.exp2(-(b_rep + m_rep)))
        h_ref[0, :, hd * M_DV:(hd + 1) * M_DV] = (
            r[:, :M_DV] / jnp.concatenate([dn] * (M_DV // LANES), axis=1)).astype(h_ref.dtype)

    nxt = head(0)
    for j in range(nu):
        cur = nxt
        if j + 1 < nu:
            nxt = head(j + 1)
        tail(j, *cur)

    @pl.when(step == pl.num_programs(1) - 1)
    def _():
        for j, c_scr in enumerate(c_scrs):
            cn_ref[0, j] = c_scr[...]
        mn_ref[0] = m_scr[...]


def _mlstm(q, kt, v, gf, git, gft, c0, m0):
    b, t, _ = q.shape
    L = min(MLSTM_CHUNK, t)
    nc = t // L
    fwd = lambda w: pl.BlockSpec((1, L, w), lambda bi, i: (bi, i, 0))
    bwd = lambda w: pl.BlockSpec((1, L, w), lambda bi, i: (bi, nc - 1 - i, 0))
    fwd_t = lambda w: pl.BlockSpec((1, w, L), lambda bi, i: (bi, 0, i))
    bwd_t = lambda w: pl.BlockSpec((1, w, L), lambda bi, i: (bi, 0, nc - 1 - i))
    st_c = pl.BlockSpec((1, 8, M_DQK, MLSTM_EXT), lambda bi, i: (bi, 0, 0, 0))
    st_v = pl.BlockSpec((1, 8, LANES), lambda bi, i: (bi, 0, 0))
    qk_w, v_w = M_HEADS * M_DQK, M_HEADS * M_DV
    return pl.pallas_call(
        _mlstm_kernel,
        grid=(b, nc),
        in_specs=[fwd(qk_w), fwd_t(qk_w), fwd(v_w), fwd(LANES), fwd_t(LANES), fwd_t(LANES),
                  bwd(qk_w), bwd_t(qk_w), bwd(v_w), bwd(LANES), bwd_t(LANES), bwd_t(LANES),
                  st_c, st_v],
        out_specs=[fwd(v_w), bwd(v_w), st_c, st_v],
        out_shape=[jax.ShapeDtypeStruct((b, t, v_w), BF16),
                   jax.ShapeDtypeStruct((b, t, v_w), BF16),
                   jax.ShapeDtypeStruct(c0.shape, F32),
                   jax.ShapeDtypeStruct(m0.shape, F32)],
        scratch_shapes=([pltpu.VMEM((M_DQK, MLSTM_EXT), F32) for _ in range(2 * M_HEADS)]
                        + [pltpu.VMEM((8, LANES), F32)]),
        compiler_params=_params("parallel", "arbitrary"),
        name="mlstm",
    )(q, kt, v, gf, git, gft, q, kt, v, gf, git, gft, c0, m0)


def _na_kernel(q_ref, k_ref, v_ref, kc_ref, vc_ref, bias_ref, o_ref, *, rows, nsub):
    tq = NA_ROWS * GRID_W
    nkeys = NA_KEY_ROWS * GRID_W
    last_rb = rows // NA_ROWS - 1
    kc = kc_ref[0]
    vc = vc_ref[0]
    lane = lax.broadcasted_iota(jnp.int32, (tq, LANES), 1)

    def logits(sb, hh):
        rb = pl.program_id(2) * nsub + sb
        kind = jnp.where(rb == 0, 0, jnp.where(rb == last_rb, 2, 1))
        ks = jnp.clip(rb * NA_ROWS - NA_KH // 2, 0, rows - NA_KEY_ROWS)
        kstart = pl.multiple_of(ks * GRID_W, GRID_W)
        kblk = k_ref[0, pl.ds(kstart, nkeys), :]
        q = q_ref[0, sb * tq:(sb + 1) * tq, :]
        in_head = (lane < NA_DH) if hh == 0 else (lane >= NA_DH)
        qm = jnp.where(in_head, q, jnp.zeros_like(q))
        return _dot_nt(qm, kblk) + bias_ref[hh, kind], _dot_nt(qm, kc), kstart

    chains = [(sb, hh) for sb in range(nsub) for hh in range(2)]
    nxt = logits(*chains[0])
    outs = []
    for i, (sb, hh) in enumerate(chains):
        sw, sc, kstart = nxt
        if i + 1 < len(chains):
            nxt = logits(*chains[i + 1])
        m = jnp.maximum(jnp.max(sw, axis=1, keepdims=True), jnp.max(sc, axis=1, keepdims=True))
        ew = jnp.exp2(sw - m)
        ec = jnp.exp2(sc - m)
        l = jnp.sum(ew, axis=1, keepdims=True) + jnp.sum(ec, axis=1, keepdims=True)
        vblk = v_ref[0, pl.ds(kstart, nkeys), :]
        o = _dot(ew.astype(BF16), vblk) + _dot(ec.astype(BF16), vc)
        outs.append(o / l)
        if hh == 1:
            o_ref[0, sb * tq:(sb + 1) * tq, :] = jnp.where(lane < NA_DH, outs[0], outs[1]).astype(o_ref.dtype)
            outs = []


def _na(nq, nk, nv, cnk, cnv, bias):
    b, t, w = nq.shape
    rows = t // GRID_W
    tq = NA_ROWS * GRID_W
    nrb = rows // NA_ROWS
    nsub = min(NA_SUB, nrb)
    nctx = cnk.shape[1]
    return pl.pallas_call(
        functools.partial(_na_kernel, rows=rows, nsub=nsub),
        grid=(b, w // LANES, nrb // nsub),
        in_specs=[pl.BlockSpec((1, nsub * tq, LANES), lambda bi, hp, st: (bi, st, hp)),
                  pl.BlockSpec((1, t, LANES), lambda bi, hp, st: (bi, 0, hp)),
                  pl.BlockSpec((1, t, LANES), lambda bi, hp, st: (bi, 0, hp)),
                  pl.BlockSpec((1, nctx, LANES), lambda bi, hp, st: (bi, 0, hp)),
                  pl.BlockSpec((1, nctx, LANES), lambda bi, hp, st: (bi, 0, hp)),
                  pl.BlockSpec((2,) + bias.shape[1:], lambda bi, hp, st: (hp, 0, 0, 0))],
        out_specs=pl.BlockSpec((1, nsub * tq, LANES), lambda bi, hp, st: (bi, st, hp)),
        out_shape=jax.ShapeDtypeStruct((b, t, w), BF16),
        compiler_params=_params("parallel", "parallel", "arbitrary"),
        name="na",
    )(nq, nk, nv, cnk, cnv, bias)


def _na_bias_table(na_rpb, rows):
    h = na_rpb.shape[0]
    w = GRID_W
    c = np.arange(w)[:, None]
    kj = np.arange(w)[None, :]
    cs = np.clip(c - NA_KW // 2, 0, w - NA_KW)
    col_valid = (kj >= cs) & (kj < cs + NA_KW)
    dc = np.clip(kj - c + (NA_KW - 1), 0, 2 * NA_KW - 2)
    onehot = np.zeros((2 * NA_KW - 1, w, w), np.float32)
    onehot[dc, np.arange(w)[:, None], np.arange(w)[None, :]] = 1.0
    t2 = jnp.einsum("hrd,dck->hrck", na_rpb, jnp.asarray(onehot), precision=lax.Precision.HIGHEST)
    t2 = jnp.where(jnp.asarray(col_valid)[None, None], t2 * LOG2E, NEG)
    t2 = jnp.concatenate([t2, jnp.full((h, 1, w, w), NEG, F32)], axis=1)
    invalid = 2 * NA_KH - 1
    dr_idx = np.full((3, NA_ROWS, NA_KEY_ROWS), invalid, np.int32)
    for kind, r0 in enumerate((0, NA_ROWS, rows - NA_ROWS)):
        ks = int(np.clip(r0 - NA_KH // 2, 0, rows - NA_KEY_ROWS))
        for qa in range(NA_ROWS):
            r = r0 + qa
            rs = int(np.clip(r - NA_KH // 2, 0, rows - NA_KH))
            for kl in range(NA_KEY_ROWS):
                ki = ks + kl
                if rs <= ki < rs + NA_KH:
                    dr_idx[kind, qa, kl] = ki - r + NA_KH - 1
    t2t = t2.transpose(0, 2, 1, 3)
    strips = [jnp.concatenate([t2t[:, :, int(dr), :] for dr in dr_idx[kind, qa]], axis=-1)
              for kind in range(3) for qa in range(NA_ROWS)]
    return jnp.stack(strips, axis=1).reshape(h, 3, NA_ROWS * w, NA_KEY_ROWS * w)


def _interleave(*streams):
    live = list(streams)
    while live:
        for g in list(live):
            try:
                next(g)
            except StopIteration:
                live.remove(g)


def _post_kernel(x_ref, mod_ref, hf_ref, hb_ref, mo_ref, na_ref, gm_ref, gn_ref,
                 mnw_ref, wbm_ref, wbn_ref, wout_ref, wr_ref, wsg_ref, wsu_ref, wsd_ref, rb_ref,
                 h2_ref, base_ref, e_ref, w_ref, r_ref, cnt_ref, run_scr):
    @pl.when(pl.program_id(0) == 0)
    def _():
        run_scr[...] = jnp.zeros_like(run_scr)

    hm = hf_ref[0].astype(F32) + hb_ref[0].astype(F32)
    parts = []
    for hd in range(M_HEADS):
        t = hm[:, hd * M_DV:(hd + 1) * M_DV]
        parts.append(t * lax.rsqrt(jnp.mean(t * t, axis=-1, keepdims=True) + EPS))
    y_m = jnp.concatenate(parts, axis=1) * mnw_ref[...] * mo_ref[0].astype(F32)
    a = _dot(y_m.astype(BF16), wbm_ref[...])
    bn = _dot(na_ref[0], wbn_ref[...])
    z = gm_ref[0].astype(F32) * a + gn_ref[0].astype(F32) * bn
    y = _dot(z.astype(BF16), wout_ref[...])
    x1 = x_ref[0] + mod_ref[0, 2:3, :] * y
    xn = x1 * lax.rsqrt(jnp.mean(x1 * x1, axis=-1, keepdims=True) + EPS)
    h2f = xn * (1.0 + mod_ref[0, 4:5, :]) + mod_ref[0, 3:4, :]
    h2_ref[...] = _pack_bf16_pairs(h2f)
    h2 = h2f.astype(BF16)
    scores = _sigmoid(_dot_nt(wr_ref[...], h2))

    def shared_expert():
        sg = _dot(h2, wsg_ref[...])
        su = _dot(h2, wsu_ref[...])
        yield
        sh = sg * _sigmoid(sg) * su
        base_ref[0] = x1 + mod_ref[0, 5:6, :] * _dot(sh.astype(BF16), wsd_ref[...])

    _interleave(_route_stages(scores, rb_ref, e_ref, w_ref, r_ref, cnt_ref, run_scr), shared_expert())


def _post(x, mod, hf, hb, mo, yna, gm, gn, mnw, wbm, wbn, wout, wr_t, wsg, wsu, wsd, rbias, tm):
    b, t, d = x.shape
    nt = t // tm
    n = b * t
    tok = lambda w: pl.BlockSpec((1, tm, w), lambda s: (s // nt, s % nt, 0))
    rt = lambda: pl.BlockSpec((TOP_K, tm), lambda s: (0, s))
    res = [mnw, wbm, wbn, wout, wr_t, wsg, wsu, wsd, rbias]
    return pl.pallas_call(
        _post_kernel,
        grid=(b * nt,),
        in_specs=[tok(d), pl.BlockSpec((1, 8, d), lambda s: (s // nt, 0, 0)),
                  tok(hf.shape[2]), tok(hb.shape[2]), tok(mo.shape[2]), tok(yna.shape[2]),
                  tok(gm.shape[2]), tok(gn.shape[2])] + [_resident(a.shape) for a in res],
        out_specs=[pl.BlockSpec((tm, d // 2), lambda s: (s, 0)),
                   tok(d), rt(), pl.BlockSpec((tm, TOP_K), lambda s: (s, 0)), rt(),
                   pl.BlockSpec((N_EXPERTS, LANES), lambda s: (0, 0))],
        out_shape=[jax.ShapeDtypeStruct((n, d // 2), jnp.int32),
                   jax.ShapeDtypeStruct((b, t, d), F32),
                   jax.ShapeDtypeStruct((TOP_K, n), jnp.int32),
                   jax.ShapeDtypeStruct((n, TOP_K), F32),
                   jax.ShapeDtypeStruct((TOP_K, n), jnp.int32),
                   jax.ShapeDtypeStruct((N_EXPERTS, LANES), F32)],
        scratch_shapes=[pltpu.VMEM((N_EXPERTS, LANES), F32)],
        compiler_params=_params("arbitrary"),
        name="post",
    )(x, mod, hf, hb, mo, yna, gm, gn, *res)


def _route_stages(s, b_ref, e_ref, w_ref, r_ref, cnt_ref, run_scr):
    tm = s.shape[1]
    sel = s + b_ref[...][:, 0:1]
    gsz = N_EXPERTS // N_GROUPS
    ninf = -jnp.inf

    x3 = sel.reshape(N_GROUPS, gsz, tm)
    r3 = lax.broadcasted_iota(jnp.int32, x3.shape, 1)
    m1 = jnp.max(x3, axis=1, keepdims=True)
    i1 = jnp.min(jnp.where(x3 == m1, r3, gsz), axis=1, keepdims=True)
    m2 = jnp.max(jnp.where(r3 == i1, ninf, x3), axis=1)
    gs = m1[:, 0, :] + m2

    gidx = lax.broadcasted_iota(jnp.int32, gs.shape, 0)
    gkeep = jnp.zeros(gs.shape, jnp.bool_)
    cur = gs
    for _ in range(TOPK_GROUPS):
        mm = jnp.max(cur, axis=0, keepdims=True)
        ii = jnp.min(jnp.where(cur == mm, gidx, N_GROUPS), axis=0, keepdims=True)
        hit = gidx == ii
        gkeep = jnp.logical_or(gkeep, hit)
        cur = jnp.where(hit, ninf, cur)
    keep = jnp.broadcast_to(gkeep[:, None, :], x3.shape).reshape(N_EXPERTS, tm)
    yield

    row = lax.broadcasted_iota(jnp.int32, s.shape, 0).astype(F32)
    cur = jnp.where(keep, sel, ninf)
    idxs, ws = [], []
    chosen_f = jnp.zeros(s.shape, F32)
    for kk in range(TOP_K):
        mm = jnp.max(cur, axis=0, keepdims=True)
        ii = jnp.min(jnp.where(cur == mm, row, float(N_EXPERTS)), axis=0, keepdims=True)
        hit = row == ii
        idxs.append(ii)
        ws.append(jnp.sum(jnp.where(hit, s, 0.0), axis=0, keepdims=True))
        chosen_f = jnp.where(hit, 1.0, chosen_f)
        cur = jnp.where(hit, ninf, cur)
        if kk == TOP_K // 2 - 1:
            yield
    wsum = ws[0]
    for wk in ws[1:]:
        wsum = wsum + wk

    tp =lax.broadcasted_iota(jnp.int32, (tm, tm), 0)
    tc = lax.broadcasted_iota(jnp.int32, (tm, tm), 1)
    before = jnp.where(tp < tc, 1.0, 0.0).astype(BF16)
    rank = _dot(chosen_f.astype(BF16), before) + run_scr[...][:, 0:1]
    run_scr[...] = run_scr[...] + jnp.sum(chosen_f, axis=1, keepdims=True)
    cnt_ref[...] = run_scr[...]

    for kk in range(TOP_K):
        e_ref[kk:kk + 1, :] = idxs[kk].astype(jnp.int32)
        r_ref[kk:kk + 1, :] = jnp.sum(jnp.where(row == idxs[kk], rank, 0.0), axis=0,
                                      keepdims=True).astype(jnp.int32)
    w_ref[...] = jnp.transpose(jnp.concatenate([wk / wsum * ROUTE_SCALE for wk in ws], axis=0))


SC_WINDOW = 128


def _sc_mesh():
    return plsc.VectorSubcoreMesh(core_axis_name="core", subcore_axis_name="subcore")


def _sc_workers():
    info = plsc.get_sparse_core_info()
    return info.num_cores, info.num_cores * info.num_subcores


def _dispatch_rows(x, top_e, rank, pstart, p_rows):
    n, w = x.shape
    kk = top_e.shape[0]
    ncores, nw = _sc_workers()
    lanes = plsc.get_sparse_core_info().num_lanes
    steps = n // nw // SC_WINDOW
    per_worker = lambda a: a.reshape(kk, nw, steps, SC_WINDOW).transpose(1, 2, 0, 3)

    @functools.partial(
        pl.kernel, mesh=_sc_mesh(),
        out_type=[jax.ShapeDtypeStruct((p_rows, w), x.dtype),
                  jax.ShapeDtypeStruct((nw, steps, kk, SC_WINDOW), jnp.int32)],
        scratch_types=[pltpu.VMEM((kk, SC_WINDOW), jnp.int32),
                       pltpu.VMEM((kk, SC_WINDOW), jnp.int32),
                       pltpu.VMEM((kk, SC_WINDOW), jnp.int32),
                       pltpu.VMEM(pstart.shape, jnp.int32),
                       pltpu.VMEM((SC_WINDOW, w), x.dtype),
                       pltpu.SemaphoreType.DMA],
        compiler_params=pltpu.CompilerParams(needs_layout_passes=False),
    )
    def scatter(x_hbm, e_hbm, r_hbm, ps_hbm, o_hbm, pos_hbm, e_v, r_v, pos_v, ps_v, rows_v, sem):
        wid = lax.axis_index("subcore") * ncores + lax.axis_index("core")
        pltpu.sync_copy(ps_hbm, ps_v)

        @pl.loop(0, steps)
        def _(s):
            base = pl.multiple_of((wid * steps + s) * SC_WINDOW, SC_WINDOW)
            pltpu.sync_copy(e_hbm.at[wid, s], e_v)
            pltpu.sync_copy(r_hbm.at[wid, s], r_v)
            pltpu.sync_copy(x_hbm.at[pl.ds(base, SC_WINDOW)], rows_v)
            for j in range(kk):
                for c in range(SC_WINDOW // lanes):
                    cols = pl.ds(c * lanes, lanes)
                    pos_v[j, cols] = plsc.load_gather(ps_v, [e_v[j, cols]]) + r_v[j, cols]
            pltpu.sync_copy(pos_v, pos_hbm.at[wid, s])
            copies = [pltpu.make_async_copy(rows_v, o_hbm.at[pos_v.at[j]], sem) for j in range(kk)]
            for cp in copies:
                cp.start()
            for cp in copies:
                cp.wait()

    out, pos4 = scatter(x, per_worker(top_e), per_worker(rank), pstart)
    return out, pos4.transpose(2, 0, 1, 3).reshape(kk, n)


def _gather_rows(x, idx):
    m = idx.shape[0]
    w = x.shape[1]
    ncores, nw = _sc_workers()
    steps = m // nw // SC_WINDOW
    idx3 = idx.reshape(nw, steps, SC_WINDOW)

    @functools.partial(
        pl.kernel, mesh=_sc_mesh(),
        out_type=jax.ShapeDtypeStruct((m, w), x.dtype),
        scratch_types=[pltpu.VMEM((steps, SC_WINDOW), jnp.int32),
                       pltpu.VMEM((SC_WINDOW, w), x.dtype),
                       pltpu.SemaphoreType.DMA],
    )
    def gather(x_hbm, i_hbm, o_hbm, idx_v, rows_v, sem):
        wid = lax.axis_index("subcore") * ncores + lax.axis_index("core")
        pltpu.sync_copy(i_hbm.at[wid], idx_v)

        @pl.loop(0, steps)
        def _(s):
            pltpu.async_copy(x_hbm.at[idx_v.at[s]], rows_v, sem).wait()
            base = pl.multiple_of((wid * steps + s) * SC_WINDOW, SC_WINDOW)
            pltpu.sync_copy(rows_v, o_hbm.at[pl.ds(base, SC_WINDOW)])

    return gather(x, idx3)


def _experts_kernel(blk0_ref, nblk_ref, cnt_ref, row0_ref, half_ref, nu_ref,
                    x_hbm, wg_hbm, wu_hbm, wd_hbm, y_hbm,
                    xbuf, ybuf, wg_raw, wu_raw, wd_raw, wg_scr, wu_scr, wd_scr, in_sem, out_sem, w_sem):
    e = pl.program_id(0)
    ne = pl.num_programs(0)
    n_used = nu_ref[0]
    blk0 = blk0_ref[e]
    ns = EXPERT_SLOTS
    sizes = (MOE_BLOCK, MOE_BLOCK // 2)

    def w_copies(ex):
        slot = ex % WEIGHT_SLOTS
        return [pltpu.make_async_copy(hbm.at[ex], raw.at[slot], w_sem.at[i, slot])
                for i, (hbm, raw) in enumerate(((wg_hbm, wg_raw), (wu_hbm, wu_raw), (wd_hbm, wd_raw)))]

    @pl.when(e == 0)
    def _():
        for e0 in range(WEIGHT_SLOTS - 1):
            @pl.when(e0 < ne)
            def _():
                for cp in w_copies(e0):
                    cp.start()

    @pl.when(e + WEIGHT_SLOTS - 1 < ne)
    def _():
        for cp in w_copies(e + WEIGHT_SLOTS - 1):
            cp.start()

    def x_copy(g, rows):
        r0 = pl.multiple_of(row0_ref[g], MOE_BLOCK // 2)
        return pltpu.make_async_copy(x_hbm.at[pl.ds(r0, rows)], xbuf.at[g % ns, pl.ds(0, rows)],
                                     in_sem.at[g % ns])

    def y_copy(g, rows):
        r0 = pl.multiple_of(row0_ref[g], MOE_BLOCK // 2)
        return pltpu.make_async_copy(ybuf.at[g % ns, pl.ds(0, rows)], y_hbm.at[pl.ds(r0, rows)],
                                     out_sem.at[g % ns])

    def by_size(g, fn):
        for is_half, rows in enumerate(sizes):
            @pl.when(half_ref[g] == is_half)
            def _():
                fn(rows)

    @pl.when(e == 0)
    def _():
        for g0 in range(ns - 1):
            @pl.when(g0 < n_used)
            def _():
                by_size(g0, lambda rows: x_copy(g0, rows).start())

    for cp in w_copies(e):
        cp.wait()
    wslot = e % WEIGHT_SLOTS
    wg_scr[...] = wg_raw[wslot].astype(BF16)
    wu_scr[...] = wu_raw[wslot].astype(BF16)
    wd_scr[...] = wd_raw[wslot].astype(BF16)

    def block(b, carry):
        g = blk0 + b
        by_size(g, lambda rows: x_copy(g, rows).wait())

        @pl.when(g + ns - 1 < n_used)
        def _():
            by_size(g + ns - 1, lambda rows: x_copy(g + ns - 1, rows).start())

        @pl.when(g >= ns)
        def _():
            by_size(g - ns, lambda rows: y_copy(g - ns, rows).wait())

        def run(rows):
            rid = lax.broadcasted_iota(jnp.int32, (rows, xbuf.shape[2]), 0)
            xp = jnp.where(rid < cnt_ref[e] - b * MOE_BLOCK, xbuf[g % ns, 0:rows], 0)
            x = _unpack_bf16_pairs(xp).astype(BF16)
            gt = _dot(x, wg_scr[...])
            up = _dot(x, wu_scr[...])
            a = (gt * _sigmoid(gt) * up).astype(BF16)
            ybuf[g % ns, 0:rows] = _pack_bf16_pairs(_dot(a, wd_scr[...]))
            y_copy(g, rows).start()

        by_size(g, run)
        return carry

    lax.fori_loop(0, nblk_ref[e], block, 0)

    @pl.when(e == pl.num_programs(0) - 1)
    def _():
        for back in range(ns, 0, -1):
            @pl.when(n_used >= back)
            def _():
                by_size(n_used - back, lambda rows: y_copy(n_used - back, rows).wait())


def _expert_plan(counts, nb_max):
    half = MOE_BLOCK // 2
    units = (counts + half - 1) // half
    nfull, tail = units // 2, units % 2
    nblk = nfull + tail
    pend = jnp.cumsum(units * half)
    pstart = pend - units * half
    blk_end = jnp.cumsum(nblk)
    blk0 = blk_end - nblk
    g = jnp.arange(nb_max, dtype=jnp.int32)
    ne = counts.shape[0]
    eg = jnp.minimum(jnp.sum((blk_end[None, :] <= g[:, None]).astype(jnp.int32), axis=1), ne - 1)
    onehot = (eg[:, None] == jnp.arange(ne, dtype=jnp.int32)[None, :]).astype(jnp.int32)
    pick = lambda v: jnp.sum(onehot * v[None, :], axis=1)
    local = g - pick(blk0)
    is_half = ((local == pick(nfull)) & (pick(tail) == 1)).astype(jnp.int32)
    row0 = jnp.clip(pick(pstart) + local * MOE_BLOCK, 0, pend[-1] - half)
    return pstart, blk0, nblk, row0, is_half, blk_end[-1:]


def _experts(blk0, nblk, counts, row0, is_half, n_used, xs, wg, wu, wd):
    p, dp = xs.shape
    ne, d, ff = wg.shape
    grid_spec = pltpu.PrefetchScalarGridSpec(
        num_scalar_prefetch=6,
        grid=(ne,),
        in_specs=[pl.BlockSpec(memory_space=pl.ANY)] * 4,
        out_specs=pl.BlockSpec(memory_space=pl.ANY),
        scratch_shapes=[pltpu.VMEM((EXPERT_SLOTS, MOE_BLOCK, dp), jnp.int32),
                        pltpu.VMEM((EXPERT_SLOTS, MOE_BLOCK, dp), jnp.int32),
                        pltpu.VMEM((WEIGHT_SLOTS, d, ff), F32), pltpu.VMEM((WEIGHT_SLOTS, d, ff), F32),
                        pltpu.VMEM((WEIGHT_SLOTS, ff, d), F32),
                        pltpu.VMEM((d, ff), BF16), pltpu.VMEM((d, ff), BF16), pltpu.VMEM((ff, d), BF16),
                        pltpu.SemaphoreType.DMA((EXPERT_SLOTS,)),
                        pltpu.SemaphoreType.DMA((EXPERT_SLOTS,)),
                        pltpu.SemaphoreType.DMA((3, WEIGHT_SLOTS))],
    )
    return pl.pallas_call(
        _experts_kernel,
        grid_spec=grid_spec,
        out_shape=jax.ShapeDtypeStruct((p, dp), jnp.int32),
        compiler_params=_params("arbitrary"),
        name="experts",
    )(blk0, nblk, counts, row0, is_half, n_used, xs, wg, wu, wd)


def _combine_kernel(base_ref, mod_ref, w_ref, y_ref, o_ref):
    acc = None
    for kk in range(TOP_K):
        term = w_ref[:, kk:kk + 1] * _unpack_bf16_pairs(y_ref[kk])
        acc = term if acc is None else acc + term
    o_ref[0] = base_ref[0] + mod_ref[0, 5:6, :] * acc


def _combine(acc, bi, mod, w_tk, yg, tm):
    b, t, d = acc.shape
    nt = t // tm
    tok = pl.BlockSpec((1, tm, d), lambda i: (bi, i, 0))
    return pl.pallas_call(
        _combine_kernel,
        grid=(nt,),
        in_specs=[tok, pl.BlockSpec((1, 8, d), lambda i: (bi, 0, 0)),
                  pl.BlockSpec((tm, TOP_K), lambda i: (bi * nt + i, 0)),
                  pl.BlockSpec((TOP_K, tm, d // 2), lambda i: (0, i, 0))],
        out_specs=tok,
        out_shape=jax.ShapeDtypeStruct((b, t, d), F32),
        input_output_aliases={0: 0},
        compiler_params=_params("parallel"),
        name="combine",
    )(acc, mod, w_tk, yg)


def _rope_tables(t, tm):
    half = M_DQK // 2
    nf = half // 2
    inv = jnp.asarray(np.power(ROPE_BASE, -np.arange(nf, dtype=np.float32) / nf).astype(np.float32))
    ar = jnp.arange(t // GRID_W, dtype=F32)[:, None] * inv[None, :]
    ac = jnp.arange(GRID_W, dtype=F32)[:, None] * inv[None, :]
    zr, zc = jnp.zeros_like(ar), jnp.zeros_like(ac)
    rcos = jnp.concatenate([jnp.cos(ar), jnp.cos(ar), zr, zr], axis=1)
    rsin = jnp.concatenate([-jnp.sin(ar), jnp.sin(ar), zr, zr], axis=1)
    ccos = jnp.tile(jnp.concatenate([zc, zc, jnp.cos(ac), jnp.cos(ac)], axis=1), (tm // GRID_W, 1))
    csin = jnp.tile(jnp.concatenate([zc, zc, -jnp.sin(ac), jnp.sin(ac)], axis=1), (tm // GRID_W, 1))
    return rcos, rsin, ccos, csin


_IN_SIZES = (512, 512, 1024, 1024, 16, 512, 512, 512, 1024, 1024)
_IN_OFFS = tuple(int(v) for v in np.concatenate([[0], np.cumsum(_IN_SIZES)]))


def _arrange_kernel(w_ref, wa_ref, wt_ref):
    seg = lambda i: w_ref[:, _IN_OFFS[i]:_IN_OFFS[i + 1]]
    mq, mk, mv, mo, _, nq, nk, nv, gm, gn = [seg(i) for i in range(10)]
    g0 = _IN_OFFS[4]
    c = w_ref[:, g0:g0 + LANES]
    lane = lax.broadcasted_iota(jnp.int32, c.shape, 1)
    left4, left8 = pltpu.roll(c, LANES - 4, 1), pltpu.roll(c, LANES - 8, 1)
    gi = jnp.where(lane < 4, c, jnp.where(lane < 8, left4, 0.0))
    gf = jnp.where(lane < 4, left4, jnp.where(lane < 8, left8, 0.0))
    wa_ref[...] = jnp.concatenate([mq, mv, mo, gf, nq, nk, nv, gm, gn], axis=1).astype(BF16)
    wt_ref[...] = jnp.concatenate([jnp.transpose(mk), jnp.transpose(gi), jnp.transpose(gf)],
                                  axis=0).astype(BF16)


def _arrange_w_in(w_in, b_mgate):
    d = w_in.shape[0]
    tr = 256
    wt_rows = M_HEADS * M_DQK + 2 * LANES
    w_all, wt_all = pl.pallas_call(
        _arrange_kernel,
        grid=(d // tr,),
        in_specs=[pl.BlockSpec((tr, w_in.shape[1]), lambda i: (i, 0))],
        out_specs=[pl.BlockSpec((tr, W_COLS), lambda i: (i, 0)),
                   pl.BlockSpec((wt_rows, tr), lambda i: (0, i))],
        out_shape=[jax.ShapeDtypeStruct((d, W_COLS), BF16),
                   jax.ShapeDtypeStruct((wt_rows, d), BF16)],
        compiler_params=_params("parallel"),
        name="arrange_w_in",
    )(w_in)
    bpad = jnp.zeros((LANES - 2 * M_HEADS,), F32)
    bi = jnp.concatenate([b_mgate[0:4], b_mgate[8:12], bpad])
    bf = jnp.concatenate([b_mgate[4:8], b_mgate[12:16], bpad])
    bg = jnp.concatenate([bf[None, :], jnp.zeros((7, LANES), F32)], axis=0)
    bgt = jnp.concatenate([bi, bf])[:, None]
    return w_all, wt_all, bg, bgt


def _segment_mats():
    na_w = NA_HEADS * NA_DH
    seg = np.zeros((na_w, LANES), np.float32)
    seg[np.arange(na_w), np.arange(na_w) // NA_DH] = 1.0
    return jnp.asarray(seg, BF16), jnp.asarray(seg.T.copy(), BF16)


def kernel(x, c, ctx, c_ctx, w_ada, b_ada, w_in, b_mgate, m_norm_w, na_qn_w, na_kn_w, na_rpb,
           w_br_m, w_br_na, w_out, w_router, router_bias, w_exp_gate, w_exp_up, w_exp_down,
           w_sh_gate, w_sh_up, w_sh_down):
    b, t, d = x.shape
    n = b * t
    rows = t // GRID_W
    l = 0

    cc = jnp.concatenate([c, c_ctx[None, :], jnp.zeros((8 - b - 1, d), F32)], axis=0)
    mod = _ada(cc, w_ada[l], b_ada[l])
    mod = mod.reshape(8, 6, d)
    mod = jnp.concatenate([mod, jnp.zeros((8, 2, d), F32)], axis=1)
    mod_x = mod[:b]
    mod_c = jnp.broadcast_to(mod[b:b + 1], (b, 8, d))

    w_all, wt_all, bg, bgt = _arrange_w_in(w_in[l], b_mgate[l])
    seg, segt = _segment_mats()
    qnw = jnp.tile(na_qn_w[l], NA_HEADS)[None, :]
    knw = jnp.tile(na_kn_w[l], NA_HEADS)[None, :]
    tm = min(512, t)

    cp = _inproj(ctx, mod_c, w_all, wt_all, bg, bgt, qnw, knw, seg, segt, None,
                 min(tm, ctx.shape[1]))
    xp = _inproj(x, mod_x, w_all, wt_all, bg, bgt, qnw, knw, seg, segt, _rope_tables(t, tm), tm)
    cmq, cmv, _, cgf, _, cnk, cnv, _, _, cmkt, cgit, cgft = cp
    mq, mv, mo, gf, nq, nk, nv, gm, gn, mkt, git, gft = xp

    c0 = jnp.zeros((b, 8, M_DQK, MLSTM_EXT), F32)
    m0 = jnp.zeros((b, 8, LANES), F32)
    _, _, c1, m1 = _mlstm(cmq, cmkt, cmv, cgf, cgit, cgft, c0, m0)
    hf, hb, _, _ = _mlstm(mq, mkt, mv, gf, git, gft, c1, m1)

    yna = _na(nq, nk, nv, cnk, cnv, _na_bias_table(na_rpb[l], rows))

    bias_col = jnp.broadcast_to(router_bias[l][:, None], (N_EXPERTS, LANES))
    h2p, base, top_e, top_w, rank, cnt = _post(
        x, mod_x, hf, hb, mo, yna, gm, gn, m_norm_w[l][None, :],
        w_br_m[l].astype(BF16), w_br_na[l].astype(BF16), w_out[l].astype(BF16),
        w_router[l].T.astype(BF16), w_sh_gate[l].astype(BF16), w_sh_up[l].astype(BF16),
        w_sh_down[l].astype(BF16), bias_col, tm)

    counts = cnt[:, 0].astype(jnp.int32)
    half = MOE_BLOCK // 2
    p_rows = (-(-(n * TOP_K) // half) + N_EXPERTS) * half
    nb_max = -(-(n * TOP_K) // MOE_BLOCK) + N_EXPERTS
    pstart, blk0, nblk, row0, is_half, n_used = _expert_plan(counts, nb_max)

    xs, pos = _dispatch_rows(h2p, top_e, rank, pstart, p_rows)
    ys = _experts(blk0, nblk, counts, row0, is_half, n_used, xs,
                  w_exp_gate[l], w_exp_up[l], w_exp_down[l])
    out = base
    for bi in range(b):
        idx = pos[:, bi * t:(bi + 1) * t].reshape(-1)
        yg = _gather_rows(ys, idx).reshape(TOP_K, t, d // 2)
        out = _combine(out, bi, mod_x, top_w, yg, tm)
    return out
```

```python
import functools

import numpy as np
import jax
import jax.numpy as jnp
from jax import lax
from jax.experimental import pallas as pl
from jax.experimental.pallas import tpu as pltpu
from jax.experimental.pallas import tpu_sc as plsc

F32 = jnp.float32
BF16 = jnp.bfloat16

EPS = 1e-6
GRID_W = 64
M_HEADS, M_DQK, M_DV = 4, 128, 256
ROPE_BASE = 10000.0
NA_HEADS, NA_DH, NA_KH, NA_KW = 8, 64, 8, 16
N_EXPERTS, TOP_K, N_GROUPS, TOPK_GROUPS = 256, 8, 8, 4
ROUTE_SCALE = 2.5

LANES = 128
VMEM_LIMIT = 56 * 1024 * 1024
NEG = -1e30
LOG2E = 1.4426950408889634

MLSTM_CHUNK = 256
NA_ROWS = 4
NA_KEY_ROWS = NA_ROWS + NA_KH - 1
NA_SUB = 8
MOE_BLOCK = 512
EXPERT_SLOTS = 4
WEIGHT_SLOTS = 3

_W_SEGS = (("mq", 512), ("mv", 1024), ("mo", 1024), ("gf", 128),
           ("nq", 512), ("nk", 512), ("nv", 512), ("gm", 1024), ("gn", 1024))
_W_OFF = {}
_o = 0
for _n, _w in _W_SEGS:
    _W_OFF[_n] = (_o, _w)
    _o += _w
W_COLS = _o


def _dot(a, b):
    return jnp.dot(a, b, preferred_element_type=F32)


def _dot_nt(a, b):
    return lax.dot_general(a, b, (((1,), (1,)), ((), ())), preferred_element_type=F32)


def _sigmoid(x):
    return 1.0 / (1.0 + jnp.exp(-x))


def _pack_bf16_pairs(v):
    w = v.shape[1] // 2
    bits = pltpu.bitcast(v.astype(BF16).astype(F32), jnp.int32)
    return lax.shift_right_logical(bits[:, :w], 16) | bits[:, w:]


def _unpack_bf16_pairs(p):
    lo = pltpu.bitcast(lax.shift_left(p, 16), F32)
    hi = pltpu.bitcast(p & jnp.int32(-65536), F32)
    return jnp.concatenate([lo, hi], axis=1)


def _params(*sem):
    return pltpu.CompilerParams(dimension_semantics=sem, vmem_limit_bytes=VMEM_LIMIT)


def _resident(shape):
    nd = len(shape)
    return pl.BlockSpec(shape, lambda *_: (0,) * nd, pipeline_mode=pl.Buffered(1))


def _ada_kernel(c_ref, w_ref, b_ref, o_ref):
    c = c_ref[...]
    s = c * _sigmoid(c)
    o_ref[...] = _dot(s.astype(BF16), w_ref[...].astype(BF16)) + b_ref[...]


def _ada(cc, w_ada, b_ada):
    d = cc.shape[1]
    n = w_ada.shape[1]
    return pl.pallas_call(
        _ada_kernel,
        grid=(n // d,),
        in_specs=[pl.BlockSpec((8, d), lambda j: (0, 0)),
                  pl.BlockSpec((d, d), lambda j: (0, j)),
                  pl.BlockSpec((1, d), lambda j: (0, j))],
        out_specs=pl.BlockSpec((8, d), lambda j: (0, j)),
        out_shape=jax.ShapeDtypeStruct((8, n), F32),
        compiler_params=_params("arbitrary"),
        name="ada",
    )(cc, w_ada, b_ada.reshape(1, n))


def _rope_rotate(t, cos, sin):
    q = M_DQK // 4
    lane = lax.broadcasted_iota(jnp.int32, t.shape, 1)
    partner = jnp.where((lane & q) == 0, pltpu.roll(t, M_DQK - q, 1), pltpu.roll(t, q, 1))
    return t * cos + partner * sin


def _rope_rotate_t(t, cos, sin):
    q = M_DQK // 4
    partner = jnp.concatenate([t[q:2 * q], t[0:q], t[3 * q:4 * q], t[2 * q:3 * q]], axis=0)
    return t * cos + partner * sin


def _inproj_kernel(*refs, rope):
    if rope:
        (x_ref, mod_ref, w_ref, wt_ref, bg_ref, bgt_ref, qnw_ref, knw_ref, seg_ref, segt_ref,
         rcos_ref, rsin_ref, ccos_ref, csin_ref,
         mq_ref, mv_ref, mo_ref, gf_ref, nq_ref, nk_ref, nv_ref, gm_ref, gn_ref,
         mkt_ref, git_ref, gft_ref) = refs
    else:
        (x_ref, mod_ref, w_ref, wt_ref, bg_ref, bgt_ref, qnw_ref, knw_ref, seg_ref, segt_ref,
         mq_ref, mv_ref, mo_ref, gf_ref, nq_ref, nk_ref, nv_ref, gm_ref, gn_ref,
         mkt_ref, git_ref, gft_ref) = refs
    x = x_ref[0]
    xn = x * lax.rsqrt(jnp.mean(x * x, axis=-1, keepdims=True) + EPS)
    h = xn * (1.0 + mod_ref[0, 1:2, :]) + mod_ref[0, 0:1, :]
    hb = h.astype(BF16)

    def proj(name):
        off, width = _W_OFF[name]
        return _dot(hb, w_ref[:, off:off + width])

    def head_rms(t, w_row, scale):
        ss = _dot((t * t).astype(BF16), seg_ref[...])
        r = lax.rsqrt(ss * (1.0 / NA_DH) + EPS)
        r_hi = r.astype(BF16)
        r_lo = (r - r_hi.astype(F32)).astype(BF16)
        rb = _dot(jnp.concatenate([r_hi, r_lo], axis=1), segt_ref[...])
        return t * rb * w_row * scale

    if rope:
        tm = x.shape[0]
        spread = lambda r: jnp.broadcast_to(r[:, None, :], (tm // GRID_W, GRID_W, LANES)).reshape(tm, LANES)
        cos = spread(rcos_ref[...]) + ccos_ref[...]
        sin = spread(rsin_ref[...]) + csin_ref[...]

    def fin_mq(t):
        t = t * (M_DQK ** -0.5)
        if rope:
            t = jnp.concatenate([_rope_rotate(t[:, i * LANES:(i + 1) * LANES], cos, sin)
                                 for i in range(M_HEADS)], axis=1)
        mq_ref[0] = t.astype(BF16)

    def fin_mkt(t):
        if rope:
            cost, sint = jnp.transpose(cos), jnp.transpose(sin)
            t = jnp.concatenate([_rope_rotate_t(t[i * M_DQK:(i + 1) * M_DQK], cost, sint)
                                 for i in range(M_HEADS)], axis=0)
        mkt_ref[0] = t.astype(BF16)

    def store(ref, fn=lambda t: t):
        def fin(t):
            ref[0] = fn(t).astype(ref.dtype)
        return fin

    qk_w = M_HEADS * M_DQK

    def fin_feature_major(t):
        fin_mkt(t[0:qk_w])
        git_ref[0] = t[qk_w:qk_w + LANES] + bgt_ref[0:LANES, :]
        gft_ref[0] = t[qk_w + LANES:qk_w + 2 * LANES] + bgt_ref[LANES:2 * LANES, :]

    stages = [
        (lambda: proj("mq"), fin_mq),
        (lambda: proj("mv"), store(mv_ref)),
        (lambda: proj("mo"), store(mo_ref, _sigmoid)),
        (lambda: proj("gf"), store(gf_ref, lambda t: t + bg_ref[0:1, :])),
        (lambda: proj("nq"), store(nq_ref, lambda t: head_rms(t, qnw_ref[...], NA_DH ** -0.5 * LOG2E))),
        (lambda: proj("nk"), store(nk_ref, lambda t: head_rms(t, knw_ref[...], 1.0))),
        (lambda: proj("nv"), store(nv_ref)),
        (lambda: proj("gm"), store(gm_ref, _sigmoid)),
        (lambda: proj("gn"), store(gn_ref, _sigmoid)),
        (lambda: _dot_nt(wt_ref[...], hb), fin_feature_major),
    ]
    acc = stages[0][0]()
    for i, (_, finish) in enumerate(stages):
        nxt = stages[i + 1][0]() if i + 1 < len(stages) else None
        finish(acc)
        acc = nxt


def _inproj(x, mod, w_all, wt_all, bg, bgt, qnw, knw, seg, segt, rope_tabs, tm):
    b, t, d = x.shape
    rope = rope_tabs is not None
    tok = lambda w: pl.BlockSpec((1, tm, w), lambda bi, i: (bi, i, 0))
    tok_t = lambda w: pl.BlockSpec((1, w, tm), lambda bi, i: (bi, 0, i))
    in_specs = [tok(d),
                pl.BlockSpec((1, 8, d), lambda bi, i: (bi, 0, 0)),
                _resident(w_all.shape), _resident(wt_all.shape), _resident(bg.shape),
                _resident(bgt.shape), _resident(qnw.shape),
                _resident(knw.shape), _resident(seg.shape), _resident(segt.shape)]
    args = [x, mod, w_all, wt_all, bg, bgt, qnw, knw, seg, segt]
    if rope:
        in_specs += [pl.BlockSpec((tm // GRID_W, LANES), lambda bi, i: (i, 0))] * 2
        in_specs += [_resident((tm, LANES))] * 2
        args += list(rope_tabs)
    widths = [("mq", BF16), ("mv", BF16), ("mo", BF16), ("gf", F32),
              ("nq", BF16), ("nk", BF16), ("nv", BF16), ("gm", BF16), ("gn", BF16)]
    out_specs = [tok(_W_OFF[n][1]) for n, _ in widths]
    out_shape = [jax.ShapeDtypeStruct((b, t, _W_OFF[n][1]), dt) for n, dt in widths]
    out_specs += [tok_t(M_HEADS * M_DQK), tok_t(LANES), tok_t(LANES)]
    out_shape += [jax.ShapeDtypeStruct((b, M_HEADS * M_DQK, t), BF16),
                  jax.ShapeDtypeStruct((b, LANES, t), F32),
                  jax.ShapeDtypeStruct((b, LANES, t), F32)]
    return pl.pallas_call(
        functools.partial(_inproj_kernel, rope=rope),
        grid=(b, t // tm),
        in_specs=in_specs, out_specs=out_specs, out_shape=out_shape,
        compiler_params=_params("parallel", "parallel"),
        name="inproj_rope" if rope else "inproj_ctx",
    )(*args)


def _log_sigmoid(x):
    return jnp.minimum(x, 0.0) - jnp.log(1.0 + jnp.exp(-jnp.abs(x)))


def _dot_split(a, b, split_a):
    x = a if split_a else b
    hi = x.astype(BF16)
    lo = (x - hi.astype(F32)).astype(BF16)
    return (_dot(hi, b) + _dot(lo, b)) if split_a else (_dot(a, hi) + _dot(a, lo))


MLSTM_EXT = M_DV + LANES


def _mlstm_kernel(qf_ref, ktf_ref, vf_ref, gff_ref, gitf_ref, gftf_ref,
                  qb_ref, ktb_ref, vb_ref, gfb_ref, gitb_ref, gftb_ref,
                  c0_ref, m0_ref,
                  hf_ref, hb_ref, cn_ref, mn_ref,
                  *scratch):
    c_scrs, m_scr = scratch[:2 * M_HEADS], scratch[2 * M_HEADS]
    step = pl.program_id(1)
    L = qf_ref.shape[1]
    nu = 2 * M_HEADS

    @pl.when(step == 0)
    def _():
        for j, c_scr in enumerate(c_scrs):
            c_scr[...] = c0_ref[0, j]
        m_scr[...] = m0_ref[0]

    row_i = lax.broadcasted_iota(jnp.int32, (L, L), 0)
    col_i = lax.broadcasted_iota(jnp.int32, (L, L), 1)
    lower = col_i <= row_i
    upper = col_i >= row_i
    tri_lo = jnp.where(lower, 1.0, 0.0).astype(BF16)
    tri_up = jnp.where(upper, 1.0, 0.0).astype(BF16)

    is_f = lax.broadcasted_iota(jnp.int32, (nu, L), 0) < M_HEADS
    gi_t = jnp.where(is_f, gitf_ref[0, 0:nu, :], gitb_ref[0, 0:nu, :]) * LOG2E
    ls_tf = _log_sigmoid(gftf_ref[0, 0:nu, :]) * LOG2E
    ls_tb = _log_sigmoid(gftb_ref[0, 0:nu, :]) * LOG2E
    b_t = jnp.where(is_f, _dot_split(ls_tf, tri_up, True), _dot_split(ls_tb, tri_lo, True))
    u_t = gi_t - b_t
    g_c = jnp.sum(jnp.where(is_f, ls_tf, ls_tb), axis=1, keepdims=True)
    m_prev = m_scr[...]
    a_t = g_c + u_t
    m_new = jnp.maximum(g_c + m_prev, jnp.max(a_t, axis=1, keepdims=True))
    decay = jnp.exp2(g_c + m_prev - m_new)
    wa_t = jnp.exp2(a_t - jnp.concatenate([m_new] * (L // LANES), axis=1))
    m_scr[...] = m_new

    ones = jnp.ones((L, LANES), BF16)
    dirs = ((qf_ref, ktf_ref, vf_ref, gff_ref, hf_ref, lower, tri_lo),
            (qb_ref, ktb_ref, vb_ref, gfb_ref, hb_ref, upper, tri_up))
    bcums = [_dot_split(tri, _log_sigmoid(gf_ref[0]) * LOG2E, False) for *_, gf_ref, _, _, tri in dirs]

    def head(j):
        d, hd = divmod(j, M_HEADS)
        q_ref, kt_ref, v_ref, _, _, mask, _ = dirs[d]
        q = q_ref[0, :, hd * M_DQK:(hd + 1) * M_DQK]
        k_t = kt_ref[0, hd * M_DQK:(hd + 1) * M_DQK, :]
        v_ext = jnp.concatenate([v_ref[0, :, hd * M_DV:(hd + 1) * M_DV], ones], axis=1)
        c_prev = c_scrs[j][...]
        u_row = u_t[j:j + 1, :]
        m_loc = jnp.max(jnp.where(mask, u_row, NEG), axis=1, keepdims=True)
        kw = (k_t.astype(F32) * wa_t[j:j + 1, :]).astype(BF16)
        dec = jnp.concatenate([decay[j:j + 1, :]] * (MLSTM_EXT // LANES), axis=1)
        c_scrs[j][...] = dec * c_prev + _dot(kw, v_ext)
        return q, v_ext, c_prev.astype(BF16), u_row, m_loc, _dot(q, k_t)

    def tail(j, q, v_ext, c_prev, u_row, m_loc, s_raw):
        d, hd = divmod(j, M_HEADS)
        h_ref, mask = dirs[d][4], dirs[d][5]
        mp_row = m_prev[j:j + 1, :]
        m_rep = jnp.maximum(jnp.broadcast_to(m_loc, (L, LANES)), mp_row)
        m_wide = jnp.concatenate([m_rep] * (L // LANES), axis=1)
        s = (s_raw * jnp.exp2(jnp.where(mask, u_row - m_wide, NEG))).astype(BF16)
        qw = (q.astype(F32) * jnp.exp2(mp_row - m_rep)).astype(BF16)
        r = _dot(s, v_ext) + _dot(qw, c_prev)
        b_rep = jnp.broadcast_to(bcums[d][:, j:j + 1], (L, LANES))
        dn = jnp.maximum(jnp.abs(r[:, M_DV:]), jnp.exp2(-(b_rep + m_rep)))
        h_ref[0, :, hd * M_DV:(hd + 1) * M_DV] = (
            r[:, :M_DV] / jnp.concatenate([dn] * (M_DV // LANES), axis=1)).astype(h_ref.dtype)

    nxt = head(0)
    for j in range(nu):
        cur = nxt
        if j + 1 < nu:
            nxt = head(j + 1)
        tail(j, *cur)

    @pl.when(step == pl.num_programs(1) - 1)
    def _():
        for j, c_scr in enumerate(c_scrs):
            cn_ref[0, j] = c_scr[...]
        mn_ref[0] = m_scr[...]


def _mlstm(q, kt, v, gf, git, gft, c0, m0):
    b, t, _ = q.shape
    L = min(MLSTM_CHUNK, t)
    nc = t // L
    fwd = lambda w: pl.BlockSpec((1, L, w), lambda bi, i: (bi, i, 0))
    bwd = lambda w: pl.BlockSpec((1, L, w), lambda bi, i: (bi, nc - 1 - i, 0))
    fwd_t = lambda w: pl.BlockSpec((1, w, L), lambda bi, i: (bi, 0, i))
    bwd_t = lambda w: pl.BlockSpec((1, w, L), lambda bi, i: (bi, 0, nc - 1 - i))
    st_c = pl.BlockSpec((1, 8, M_DQK, MLSTM_EXT), lambda bi, i: (bi, 0, 0, 0))
    st_v = pl.BlockSpec((1, 8, LANES), lambda bi, i: (bi, 0, 0))
    qk_w, v_w = M_HEADS * M_DQK, M_HEADS * M_DV
    return pl.pallas_call(
        _mlstm_kernel,
        grid=(b, nc),
        in_specs=[fwd(qk_w), fwd_t(qk_w), fwd(v_w), fwd(LANES), fwd_t(LANES), fwd_t(LANES),
                  bwd(qk_w), bwd_t(qk_w), bwd(v_w), bwd(LANES), bwd_t(LANES), bwd_t(LANES),
                  st_c, st_v],
        out_specs=[fwd(v_w), bwd(v_w), st_c, st_v],
        out_shape=[jax.ShapeDtypeStruct((b, t, v_w), BF16),
                   jax.ShapeDtypeStruct((b, t, v_w), BF16),
                   jax.ShapeDtypeStruct(c0.shape, F32),
                   jax.ShapeDtypeStruct(m0.shape, F32)],
        scratch_shapes=([pltpu.VMEM((M_DQK, MLSTM_EXT), F32) for _ in range(2 * M_HEADS)]
                        + [pltpu.VMEM((8, LANES), F32)]),
        compiler_params=_params("parallel", "arbitrary"),
        name="mlstm",
    )(q, kt, v, gf, git, gft, q, kt, v, gf, git, gft, c0, m0)


def _na_kernel(q_ref, k_ref, v_ref, kc_ref, vc_ref, bias_ref, o_ref, *, rows, nsub):
    tq = NA_ROWS * GRID_W
    nkeys = NA_KEY_ROWS * GRID_W
    last_rb = rows // NA_ROWS - 1
    kc = kc_ref[0]
    vc = vc_ref[0]
    lane = lax.broadcasted_iota(jnp.int32, (tq, LANES), 1)

    def logits(sb, hh):
        rb = pl.program_id(2) * nsub + sb
        kind = jnp.where(rb == 0, 0, jnp.where(rb == last_rb, 2, 1))
        ks = jnp.clip(rb * NA_ROWS - NA_KH // 2, 0, rows - NA_KEY_ROWS)
        kstart = pl.multiple_of(ks * GRID_W, GRID_W)
        kblk = k_ref[0, pl.ds(kstart, nkeys), :]
        q = q_ref[0, sb * tq:(sb + 1) * tq, :]
        in_head = (lane < NA_DH) if hh == 0 else (lane >= NA_DH)
        qm = jnp.where(in_head, q, jnp.zeros_like(q))
        return _dot_nt(qm, kblk) + bias_ref[hh, kind], _dot_nt(qm, kc), kstart

    chains = [(sb, hh) for sb in range(nsub) for hh in range(2)]
    nxt = logits(*chains[0])
    outs = []
    for i, (sb, hh) in enumerate(chains):
        sw, sc, kstart = nxt
        if i + 1 < len(chains):
            nxt = logits(*chains[i + 1])
        m = jnp.maximum(jnp.max(sw, axis=1, keepdims=True), jnp.max(sc, axis=1, keepdims=True))
        ew = jnp.exp2(sw - m)
        ec = jnp.exp2(sc - m)
        l = jnp.sum(ew, axis=1, keepdims=True) + jnp.sum(ec, axis=1, keepdims=True)
        vblk = v_ref[0, pl.ds(kstart, nkeys), :]
        o = _dot(ew.astype(BF16), vblk) + _dot(ec.astype(BF16), vc)
        outs.append(o / l)
        if hh == 1:
            o_ref[0, sb * tq:(sb + 1) * tq, :] = jnp.where(lane < NA_DH, outs[0], outs[1]).astype(o_ref.dtype)
            outs = []


def _na(nq, nk, nv, cnk, cnv, bias):
    b, t, w = nq.shape
    rows = t // GRID_W
    tq = NA_ROWS * GRID_W
    nrb = rows // NA_ROWS
    nsub = min(NA_SUB, nrb)
    nctx = cnk.shape[1]
    return pl.pallas_call(
        functools.partial(_na_kernel, rows=rows, nsub=nsub),
        grid=(b, w // LANES, nrb // nsub),
        in_specs=[pl.BlockSpec((1, nsub * tq, LANES), lambda bi, hp, st: (bi, st, hp)),
                  pl.BlockSpec((1, t, LANES), lambda bi, hp, st: (bi, 0, hp)),
                  pl.BlockSpec((1, t, LANES), lambda bi, hp, st: (bi, 0, hp)),
                  pl.BlockSpec((1, nctx, LANES), lambda bi, hp, st: (bi, 0, hp)),
                  pl.BlockSpec((1, nctx, LANES), lambda bi, hp, st: (bi, 0, hp)),
                  pl.BlockSpec((2,) + bias.shape[1:], lambda bi, hp, st: (hp, 0, 0, 0))],
        out_specs=pl.BlockSpec((1, nsub * tq, LANES), lambda bi, hp, st: (bi, st, hp)),
        out_shape=jax.ShapeDtypeStruct((b, t, w), BF16),
        compiler_params=_params("parallel", "parallel", "arbitrary"),
        name="na",
    )(nq, nk, nv, cnk, cnv, bias)


def _na_bias_table(na_rpb, rows):
    h = na_rpb.shape[0]
    w = GRID_W
    c = np.arange(w)[:, None]
    kj = np.arange(w)[None, :]
    cs = np.clip(c - NA_KW // 2, 0, w - NA_KW)
    col_valid = (kj >= cs) & (kj < cs + NA_KW)
    dc = np.clip(kj - c + (NA_KW - 1), 0, 2 * NA_KW - 2)
    onehot = np.zeros((2 * NA_KW - 1, w, w), np.float32)
    onehot[dc, np.arange(w)[:, None], np.arange(w)[None, :]] = 1.0
    t2 = jnp.einsum("hrd,dck->hrck", na_rpb, jnp.asarray(onehot), precision=lax.Precision.HIGHEST)
    t2 = jnp.where(jnp.asarray(col_valid)[None, None], t2 * LOG2E, NEG)
    t2 = jnp.concatenate([t2, jnp.full((h, 1, w, w), NEG, F32)], axis=1)
    invalid = 2 * NA_KH - 1
    dr_idx = np.full((3, NA_ROWS, NA_KEY_ROWS), invalid, np.int32)
    for kind, r0 in enumerate((0, NA_ROWS, rows - NA_ROWS)):
        ks = int(np.clip(r0 - NA_KH // 2, 0, rows - NA_KEY_ROWS))
        for qa in range(NA_ROWS):
            r = r0 + qa
            rs = int(np.clip(r - NA_KH // 2, 0, rows - NA_KH))
            for kl in range(NA_KEY_ROWS):
                ki = ks + kl
                if rs <= ki < rs + NA_KH:
                    dr_idx[kind, qa, kl] = ki - r + NA_KH - 1
    t2t = t2.transpose(0, 2, 1, 3)
    strips = [jnp.concatenate([t2t[:, :, int(dr), :] for dr in dr_idx[kind, qa]], axis=-1)
              for kind in range(3) for qa in range(NA_ROWS)]
    return jnp.stack(strips, axis=1).reshape(h, 3, NA_ROWS * w, NA_KEY_ROWS * w)


def _interleave(*streams):
    live = list(streams)
    while live:
        for g in list(live):
            try:
                next(g)
            except StopIteration:
                live.remove(g)


def _post_kernel(x_ref, mod_ref, hf_ref, hb_ref, mo_ref, na_ref, gm_ref, gn_ref,
                 mnw_ref, wbm_ref, wbn_ref, wout_ref, wr_ref, wsg_ref, wsu_ref, wsd_ref, rb_ref,
                 h2_ref, base_ref, e_ref, w_ref, r_ref, cnt_ref, run_scr):
    @pl.when(pl.program_id(0) == 0)
    def _():
        run_scr[...] = jnp.zeros_like(run_scr)

    hm = hf_ref[0].astype(F32) + hb_ref[0].astype(F32)
    parts = []
    for hd in range(M_HEADS):
        t = hm[:, hd * M_DV:(hd + 1) * M_DV]
        parts.append(t * lax.rsqrt(jnp.mean(t * t, axis=-1, keepdims=True) + EPS))
    y_m = jnp.concatenate(parts, axis=1) * mnw_ref[...] * mo_ref[0].astype(F32)
    a = _dot(y_m.astype(BF16), wbm_ref[...])
    bn = _dot(na_ref[0], wbn_ref[...])
    z = gm_ref[0].astype(F32) * a + gn_ref[0].astype(F32) * bn
    y = _dot(z.astype(BF16), wout_ref[...])
    x1 = x_ref[0] + mod_ref[0, 2:3, :] * y
    xn = x1 * lax.rsqrt(jnp.mean(x1 * x1, axis=-1, keepdims=True) + EPS)
    h2f = xn * (1.0 + mod_ref[0, 4:5, :]) + mod_ref[0, 3:4, :]
    h2_ref[...] = _pack_bf16_pairs(h2f)
    h2 = h2f.astype(BF16)
    scores = _sigmoid(_dot_nt(wr_ref[...], h2))

    def shared_expert():
        sg = _dot(h2, wsg_ref[...])
        su = _dot(h2, wsu_ref[...])
        yield
        sh = sg * _sigmoid(sg) * su
        base_ref[0] = x1 + mod_ref[0, 5:6, :] * _dot(sh.astype(BF16), wsd_ref[...])

    _interleave(_route_stages(scores, rb_ref, e_ref, w_ref, r_ref, cnt_ref, run_scr), shared_expert())


def _post(x, mod, hf, hb, mo, yna, gm, gn, mnw, wbm, wbn, wout, wr_t, wsg, wsu, wsd, rbias, tm):
    b, t, d = x.shape
    nt = t // tm
    n = b * t
    tok = lambda w: pl.BlockSpec((1, tm, w), lambda s: (s // nt, s % nt, 0))
    rt = lambda: pl.BlockSpec((TOP_K, tm), lambda s: (0, s))
    res = [mnw, wbm, wbn, wout, wr_t, wsg, wsu, wsd, rbias]
    return pl.pallas_call(
        _post_kernel,
        grid=(b * nt,),
        in_specs=[tok(d), pl.BlockSpec((1, 8, d), lambda s: (s // nt, 0, 0)),
                  tok(hf.shape[2]), tok(hb.shape[2]), tok(mo.shape[2]), tok(yna.shape[2]),
                  tok(gm.shape[2]), tok(gn.shape[2])] + [_resident(a.shape) for a in res],
        out_specs=[pl.BlockSpec((tm, d // 2), lambda s: (s, 0)),
                   tok(d), rt(), pl.BlockSpec((tm, TOP_K), lambda s: (s, 0)), rt(),
                   pl.BlockSpec((N_EXPERTS, LANES), lambda s: (0, 0))],
        out_shape=[jax.ShapeDtypeStruct((n, d // 2), jnp.int32),
                   jax.ShapeDtypeStruct((b, t, d), F32),
                   jax.ShapeDtypeStruct((TOP_K, n), jnp.int32),
                   jax.ShapeDtypeStruct((n, TOP_K), F32),
                   jax.ShapeDtypeStruct((TOP_K, n), jnp.int32),
                   jax.ShapeDtypeStruct((N_EXPERTS, LANES), F32)],
        scratch_shapes=[pltpu.VMEM((N_EXPERTS, LANES), F32)],
        compiler_params=_params("arbitrary"),
        name="post",
    )(x, mod, hf, hb, mo, yna, gm, gn, *res)


def _route_stages(s, b_ref, e_ref, w_ref, r_ref, cnt_ref, run_scr):
    tm = s.shape[1]
    sel = s + b_ref[...][:, 0:1]
    gsz = N_EXPERTS // N_GROUPS
    ninf = -jnp.inf

    x3 = sel.reshape(N_GROUPS, gsz, tm)
    r3 = lax.broadcasted_iota(jnp.int32, x3.shape, 1)
    m1 = jnp.max(x3, axis=1, keepdims=True)
    i1 = jnp.min(jnp.where(x3 == m1, r3, gsz), axis=1, keepdims=True)
    m2 = jnp.max(jnp.where(r3 == i1, ninf, x3), axis=1)
    gs = m1[:, 0, :] + m2

    gidx = lax.broadcasted_iota(jnp.int32, gs.shape, 0)
    gkeep = jnp.zeros(gs.shape, jnp.bool_)
    cur = gs
    for _ in range(TOPK_GROUPS):
        mm = jnp.max(cur, axis=0, keepdims=True)
        ii = jnp.min(jnp.where(cur == mm, gidx, N_GROUPS), axis=0, keepdims=True)
        hit = gidx == ii
        gkeep = jnp.logical_or(gkeep, hit)
        cur = jnp.where(hit, ninf, cur)
    keep = jnp.broadcast_to(gkeep[:, None, :], x3.shape).reshape(N_EXPERTS, tm)
    yield

    row = lax.broadcasted_iota(jnp.int32, s.shape, 0).astype(F32)
    cur = jnp.where(keep, sel, ninf)
    idxs, ws = [], []
    chosen_f = jnp.zeros(s.shape, F32)
    for kk in range(TOP_K):
        mm = jnp.max(cur, axis=0, keepdims=True)
        ii = jnp.min(jnp.where(cur == mm, row, float(N_EXPERTS)), axis=0, keepdims=True)
        hit = row == ii
        idxs.append(ii)
        ws.append(jnp.sum(jnp.where(hit, s, 0.0), axis=0, keepdims=True))
        chosen_f = jnp.where(hit, 1.0, chosen_f)
        cur = jnp.where(hit, ninf, cur)
        if kk == TOP_K // 2 - 1:
            yield
    wsum = ws[0]
    for wk in ws[1:]:
        wsum = wsum + wk

    tp =lax.broadcasted_iota(jnp.int32, (tm, tm), 0)
    tc = lax.broadcasted_iota(jnp.int32, (tm, tm), 1)
    before = jnp.where(tp < tc, 1.0, 0.0).astype(BF16)
    rank = _dot(chosen_f.astype(BF16), before) + run_scr[...][:, 0:1]
    run_scr[...] = run_scr[...] + jnp.sum(chosen_f, axis=1, keepdims=True)
    cnt_ref[...] = run_scr[...]

    for kk in range(TOP_K):
        e_ref[kk:kk + 1, :] = idxs[kk].astype(jnp.int32)
        r_ref[kk:kk + 1, :] = jnp.sum(jnp.where(row == idxs[kk], rank, 0.0), axis=0,
                                      keepdims=True).astype(jnp.int32)
    w_ref[...] = jnp.transpose(jnp.concatenate([wk / wsum * ROUTE_SCALE for wk in ws], axis=0))


SC_WINDOW = 128


def _sc_mesh():
    return plsc.VectorSubcoreMesh(core_axis_name="core", subcore_axis_name="subcore")


def _sc_workers():
    info = plsc.get_sparse_core_info()
    return info.num_cores, info.num_cores * info.num_subcores


def _dispatch_rows(x, top_e, rank, pstart, p_rows):
    n, w = x.shape
    kk = top_e.shape[0]
    ncores, nw = _sc_workers()
    lanes = plsc.get_sparse_core_info().num_lanes
    steps = n // nw // SC_WINDOW
    per_worker = lambda a: a.reshape(kk, nw, steps, SC_WINDOW).transpose(1, 2, 0, 3)

    @functools.partial(
        pl.kernel, mesh=_sc_mesh(),
        out_type=[jax.ShapeDtypeStruct((p_rows, w), x.dtype),
                  jax.ShapeDtypeStruct((nw, steps, kk, SC_WINDOW), jnp.int32)],
        scratch_types=[pltpu.VMEM((kk, SC_WINDOW), jnp.int32),
                       pltpu.VMEM((kk, SC_WINDOW), jnp.int32),
                       pltpu.VMEM((kk, SC_WINDOW), jnp.int32),
                       pltpu.VMEM(pstart.shape, jnp.int32),
                       pltpu.VMEM((SC_WINDOW, w), x.dtype),
                       pltpu.SemaphoreType.DMA],
        compiler_params=pltpu.CompilerParams(needs_layout_passes=False),
    )
    def scatter(x_hbm, e_hbm, r_hbm, ps_hbm, o_hbm, pos_hbm, e_v, r_v, pos_v, ps_v, rows_v, sem):
        wid = lax.axis_index("subcore") * ncores + lax.axis_index("core")
        pltpu.sync_copy(ps_hbm, ps_v)

        @pl.loop(0, steps)
        def _(s):
            base = pl.multiple_of((wid * steps + s) * SC_WINDOW, SC_WINDOW)
            pltpu.sync_copy(e_hbm.at[wid, s], e_v)
            pltpu.sync_copy(r_hbm.at[wid, s], r_v)
            pltpu.sync_copy(x_hbm.at[pl.ds(base, SC_WINDOW)], rows_v)
            for j in range(kk):
                for c in range(SC_WINDOW // lanes):
                    cols = pl.ds(c * lanes, lanes)
                    pos_v[j, cols] = plsc.load_gather(ps_v, [e_v[j, cols]]) + r_v[j, cols]
            pltpu.sync_copy(pos_v, pos_hbm.at[wid, s])
            copies = [pltpu.make_async_copy(rows_v, o_hbm.at[pos_v.at[j]], sem) for j in range(kk)]
            for cp in copies:
                cp.start()
            for cp in copies:
                cp.wait()

    out, pos4 = scatter(x, per_worker(top_e), per_worker(rank), pstart)
    return out, pos4.transpose(2, 0, 1, 3).reshape(kk, n)


def _gather_rows(x, idx):
    m = idx.shape[0]
    w = x.shape[1]
    ncores, nw = _sc_workers()
    steps = m // nw // SC_WINDOW
    idx3 = idx.reshape(nw, steps, SC_WINDOW)

    @functools.partial(
        pl.kernel, mesh=_sc_mesh(),
        out_type=jax.ShapeDtypeStruct((m, w), x.dtype),
        scratch_types=[pltpu.VMEM((steps, SC_WINDOW), jnp.int32),
                       pltpu.VMEM((SC_WINDOW, w), x.dtype),
                       pltpu.SemaphoreType.DMA],
    )
    def gather(x_hbm, i_hbm, o_hbm, idx_v, rows_v, sem):
        wid = lax.axis_index("subcore") * ncores + lax.axis_index("core")
        pltpu.sync_copy(i_hbm.at[wid], idx_v)

        @pl.loop(0, steps)
        def _(s):
            pltpu.async_copy(x_hbm.at[idx_v.at[s]], rows_v, sem).wait()
            base = pl.multiple_of((wid * steps + s) * SC_WINDOW, SC_WINDOW)
            pltpu.sync_copy(rows_v, o_hbm.at[pl.ds(base, SC_WINDOW)])

    return gather(x, idx3)


def _experts_kernel(blk0_ref, nblk_ref, cnt_ref, row0_ref, half_ref, nu_ref,
                    x_hbm, wg_hbm, wu_hbm, wd_hbm, y_hbm,
                    xbuf, ybuf, wg_raw, wu_raw, wd_raw, wg_scr, wu_scr, wd_scr, in_sem, out_sem, w_sem):
    e = pl.program_id(0)
    ne = pl.num_programs(0)
    n_used = nu_ref[0]
    blk0 = blk0_ref[e]
    ns = EXPERT_SLOTS
    sizes = (MOE_BLOCK, MOE_BLOCK // 2)

    def w_copies(ex):
        slot = ex % WEIGHT_SLOTS
        return [pltpu.make_async_copy(hbm.at[ex], raw.at[slot], w_sem.at[i, slot])
                for i, (hbm, raw) in enumerate(((wg_hbm, wg_raw), (wu_hbm, wu_raw), (wd_hbm, wd_raw)))]

    @pl.when(e == 0)
    def _():
        for e0 in range(WEIGHT_SLOTS - 1):
            @pl.when(e0 < ne)
            def _():
                for cp in w_copies(e0):
                    cp.start()

    @pl.when(e + WEIGHT_SLOTS - 1 < ne)
    def _():
        for cp in w_copies(e + WEIGHT_SLOTS - 1):
            cp.start()

    def x_copy(g, rows):
        r0 = pl.multiple_of(row0_ref[g], MOE_BLOCK // 2)
        return pltpu.make_async_copy(x_hbm.at[pl.ds(r0, rows)], xbuf.at[g % ns, pl.ds(0, rows)],
                                     in_sem.at[g % ns])

    def y_copy(g, rows):
        r0 = pl.multiple_of(row0_ref[g], MOE_BLOCK // 2)
        return pltpu.make_async_copy(ybuf.at[g % ns, pl.ds(0, rows)], y_hbm.at[pl.ds(r0, rows)],
                                     out_sem.at[g % ns])

    def by_size(g, fn):
        for is_half, rows in enumerate(sizes):
            @pl.when(half_ref[g] == is_half)
            def _():
                fn(rows)

    @pl.when(e == 0)
    def _():
        for g0 in range(ns - 1):
            @pl.when(g0 < n_used)
            def _():
                by_size(g0, lambda rows: x_copy(g0, rows).start())

    for cp in w_copies(e):
        cp.wait()
    wslot = e % WEIGHT_SLOTS
    wg_scr[...] = wg_raw[wslot].astype(BF16)
    wu_scr[...] = wu_raw[wslot].astype(BF16)
    wd_scr[...] = wd_raw[wslot].astype(BF16)

    def block(b, carry):
        g = blk0 + b
        by_size(g, lambda rows: x_copy(g, rows).wait())

        @pl.when(g + ns - 1 < n_used)
        def _():
            by_size(g + ns - 1, lambda rows: x_copy(g + ns - 1, rows).start())

        @pl.when(g >= ns)
        def _():
            by_size(g - ns, lambda rows: y_copy(g - ns, rows).wait())

        def run(rows):
            rid = lax.broadcasted_iota(jnp.int32, (rows, xbuf.shape[2]), 0)
            xp = jnp.where(rid < cnt_ref[e] - b * MOE_BLOCK, xbuf[g % ns, 0:rows], 0)
            x = _unpack_bf16_pairs(xp).astype(BF16)
            gt = _dot(x, wg_scr[...])
            up = _dot(x, wu_scr[...])
            a = (gt * _sigmoid(gt) * up).astype(BF16)
            ybuf[g % ns, 0:rows] = _pack_bf16_pairs(_dot(a, wd_scr[...]))
            y_copy(g, rows).start()

        by_size(g, run)
        return carry

    lax.fori_loop(0, nblk_ref[e], block, 0)

    @pl.when(e == pl.num_programs(0) - 1)
    def _():
        for back in range(ns, 0, -1):
            @pl.when(n_used >= back)
            def _():
                by_size(n_used - back, lambda rows: y_copy(n_used - back, rows).wait())


def _expert_plan(counts, nb_max):
    half = MOE_BLOCK // 2
    units = (counts + half - 1) // half
    nfull, tail = units // 2, units % 2
    nblk = nfull + tail
    pend = jnp.cumsum(units * half)
    pstart = pend - units * half
    blk_end = jnp.cumsum(nblk)
    blk0 = blk_end - nblk
    g = jnp.arange(nb_max, dtype=jnp.int32)
    ne = counts.shape[0]
    eg = jnp.minimum(jnp.sum((blk_end[None, :] <= g[:, None]).astype(jnp.int32), axis=1), ne - 1)
    onehot = (eg[:, None] == jnp.arange(ne, dtype=jnp.int32)[None, :]).astype(jnp.int32)
    pick = lambda v: jnp.sum(onehot * v[None, :], axis=1)
    local = g - pick(blk0)
    is_half = ((local == pick(nfull)) & (pick(tail) == 1)).astype(jnp.int32)
    row0 = jnp.clip(pick(pstart) + local * MOE_BLOCK, 0, pend[-1] - half)
    return pstart, blk0, nblk, row0, is_half, blk_end[-1:]


def _experts(blk0, nblk, counts, row0, is_half, n_used, xs, wg, wu, wd):
    p, dp = xs.shape
    ne, d, ff = wg.shape
    grid_spec = pltpu.PrefetchScalarGridSpec(
        num_scalar_prefetch=6,
        grid=(ne,),
        in_specs=[pl.BlockSpec(memory_space=pl.ANY)] * 4,
        out_specs=pl.BlockSpec(memory_space=pl.ANY),
        scratch_shapes=[pltpu.VMEM((EXPERT_SLOTS, MOE_BLOCK, dp), jnp.int32),
                        pltpu.VMEM((EXPERT_SLOTS, MOE_BLOCK, dp), jnp.int32),
                        pltpu.VMEM((WEIGHT_SLOTS, d, ff), F32), pltpu.VMEM((WEIGHT_SLOTS, d, ff), F32),
                        pltpu.VMEM((WEIGHT_SLOTS, ff, d), F32),
                        pltpu.VMEM((d, ff), BF16), pltpu.VMEM((d, ff), BF16), pltpu.VMEM((ff, d), BF16),
                        pltpu.SemaphoreType.DMA((EXPERT_SLOTS,)),
                        pltpu.SemaphoreType.DMA((EXPERT_SLOTS,)),
                        pltpu.SemaphoreType.DMA((3, WEIGHT_SLOTS))],
    )
    return pl.pallas_call(
        _experts_kernel,
        grid_spec=grid_spec,
        out_shape=jax.ShapeDtypeStruct((p, dp), jnp.int32),
        compiler_params=_params("arbitrary"),
        name="experts",
    )(blk0, nblk, counts, row0, is_half, n_used, xs, wg, wu, wd)


def _combine_kernel(base_ref, mod_ref, w_ref, y_ref, o_ref):
    acc = None
    for kk in range(TOP_K):
        term = w_ref[:, kk:kk + 1] * _unpack_bf16_pairs(y_ref[kk])
        acc = term if acc is None else acc + term
    o_ref[0] = base_ref[0] + mod_ref[0, 5:6, :] * acc


def _combine(acc, bi, mod, w_tk, yg, tm):
    b, t, d = acc.shape
    nt = t // tm
    tok = pl.BlockSpec((1, tm, d), lambda i: (bi, i, 0))
    return pl.pallas_call(
        _combine_kernel,
        grid=(nt,),
        in_specs=[tok, pl.BlockSpec((1, 8, d), lambda i: (bi, 0, 0)),
                  pl.BlockSpec((tm, TOP_K), lambda i: (bi * nt + i, 0)),
                  pl.BlockSpec((TOP_K, tm, d // 2), lambda i: (0, i, 0))],
        out_specs=tok,
        out_shape=jax.ShapeDtypeStruct((b, t, d), F32),
        input_output_aliases={0: 0},
        compiler_params=_params("parallel"),
        name="combine",
    )(acc, mod, w_tk, yg)


def _rope_tables(t, tm):
    half = M_DQK // 2
    nf = half // 2
    inv = jnp.asarray(np.power(ROPE_BASE, -np.arange(nf, dtype=np.float32) / nf).astype(np.float32))
    ar = jnp.arange(t // GRID_W, dtype=F32)[:, None] * inv[None, :]
    ac = jnp.arange(GRID_W, dtype=F32)[:, None] * inv[None, :]
    zr, zc = jnp.zeros_like(ar), jnp.zeros_like(ac)
    rcos = jnp.concatenate([jnp.cos(ar), jnp.cos(ar), zr, zr], axis=1)
    rsin = jnp.concatenate([-jnp.sin(ar), jnp.sin(ar), zr, zr], axis=1)
    ccos = jnp.tile(jnp.concatenate([zc, zc, jnp.cos(ac), jnp.cos(ac)], axis=1), (tm // GRID_W, 1))
    csin = jnp.tile(jnp.concatenate([zc, zc, -jnp.sin(ac), jnp.sin(ac)], axis=1), (tm // GRID_W, 1))
    return rcos, rsin, ccos, csin


_IN_SIZES = (512, 512, 1024, 1024, 16, 512, 512, 512, 1024, 1024)
_IN_OFFS = tuple(int(v) for v in np.concatenate([[0], np.cumsum(_IN_SIZES)]))


def _arrange_kernel(w_ref, wa_ref, wt_ref):
    seg = lambda i: w_ref[:, _IN_OFFS[i]:_IN_OFFS[i + 1]]
    mq, mk, mv, mo, _, nq, nk, nv, gm, gn = [seg(i) for i in range(10)]
    g0 = _IN_OFFS[4]
    c = w_ref[:, g0:g0 + LANES]
    lane = lax.broadcasted_iota(jnp.int32, c.shape, 1)
    left4, left8 = pltpu.roll(c, LANES - 4, 1), pltpu.roll(c, LANES - 8, 1)
    gi = jnp.where(lane < 4, c, jnp.where(lane < 8, left4, 0.0))
    gf = jnp.where(lane < 4, left4, jnp.where(lane < 8, left8, 0.0))
    wa_ref[...] = jnp.concatenate([mq, mv, mo, gf, nq, nk, nv, gm, gn], axis=1).astype(BF16)
    wt_ref[...] = jnp.concatenate([jnp.transpose(mk), jnp.transpose(gi), jnp.transpose(gf)],
                                  axis=0).astype(BF16)


def _arrange_w_in(w_in, b_mgate):
    d = w_in.shape[0]
    tr = 256
    wt_rows = M_HEADS * M_DQK + 2 * LANES
    w_all, wt_all = pl.pallas_call(
        _arrange_kernel,
        grid=(d // tr,),
        in_specs=[pl.BlockSpec((tr, w_in.shape[1]), lambda i: (i, 0))],
        out_specs=[pl.BlockSpec((tr, W_COLS), lambda i: (i, 0)),
                   pl.BlockSpec((wt_rows, tr), lambda i: (0, i))],
        out_shape=[jax.ShapeDtypeStruct((d, W_COLS), BF16),
                   jax.ShapeDtypeStruct((wt_rows, d), BF16)],
        compiler_params=_params("parallel"),
        name="arrange_w_in",
    )(w_in)
    bpad = jnp.zeros((LANES - 2 * M_HEADS,), F32)
    bi = jnp.concatenate([b_mgate[0:4], b_mgate[8:12], bpad])
    bf = jnp.concatenate([b_mgate[4:8], b_mgate[12:16], bpad])
    bg = jnp.concatenate([bf[None, :], jnp.zeros((7, LANES), F32)], axis=0)
    bgt = jnp.concatenate([bi, bf])[:, None]
    return w_all, wt_all, bg, bgt


def _segment_mats():
    na_w = NA_HEADS * NA_DH
    seg = np.zeros((na_w, LANES), np.float32)
    seg[np.arange(na_w), np.arange(na_w) // NA_DH] = 1.0
    return jnp.asarray(seg, BF16), jnp.asarray(np.concatenate([seg.T, seg.T], axis=0), BF16)


def kernel(x, c, ctx, c_ctx, w_ada, b_ada, w_in, b_mgate, m_norm_w, na_qn_w, na_kn_w, na_rpb,
           w_br_m, w_br_na, w_out, w_router, router_bias, w_exp_gate, w_exp_up, w_exp_down,
           w_sh_gate, w_sh_up, w_sh_down):
    b, t, d = x.shape
    n = b * t
    rows = t // GRID_W
    l = 0

    cc = jnp.concatenate([c, c_ctx[None, :], jnp.zeros((8 - b - 1, d), F32)], axis=0)
    mod = _ada(cc, w_ada[l], b_ada[l])
    mod = mod.reshape(8, 6, d)
    mod = jnp.concatenate([mod, jnp.zeros((8, 2, d), F32)], axis=1)
    mod_x = mod[:b]
    mod_c = jnp.broadcast_to(mod[b:b + 1], (b, 8, d))

    w_all, wt_all, bg, bgt = _arrange_w_in(w_in[l], b_mgate[l])
    seg, segt = _segment_mats()
    qnw = jnp.tile(na_qn_w[l], NA_HEADS)[None, :]
    knw = jnp.tile(na_kn_w[l], NA_HEADS)[None, :]
    tm = min(512, t)

    cp = _inproj(ctx, mod_c, w_all, wt_all, bg, bgt, qnw, knw, seg, segt, None,
                 min(tm, ctx.shape[1]))
    xp = _inproj(x, mod_x, w_all, wt_all, bg, bgt, qnw, knw, seg, segt, _rope_tables(t, tm), tm)
    cmq, cmv, _, cgf, _, cnk, cnv, _, _, cmkt, cgit, cgft = cp
    mq, mv, mo, gf, nq, nk, nv, gm, gn, mkt, git, gft = xp

    c0 = jnp.zeros((b, 8, M_DQK, MLSTM_EXT), F32)
    m0 = jnp.zeros((b, 8, LANES), F32)
    _, _, c1, m1 = _mlstm(cmq, cmkt, cmv, cgf, cgit, cgft, c0, m0)
    hf, hb, _, _ = _mlstm(mq, mkt, mv, gf, git, gft, c1, m1)

    yna = _na(nq, nk, nv, cnk, cnv, _na_bias_table(na_rpb[l], rows))

    bias_col = jnp.broadcast_to(router_bias[l][:, None], (N_EXPERTS, LANES))
    h2p, base, top_e, top_w, rank, cnt = _post(
        x, mod_x, hf, hb, mo, yna, gm, gn, m_norm_w[l][None, :],
        w_br_m[l].astype(BF16), w_br_na[l].astype(BF16), w_out[l].astype(BF16),
        w_router[l].T.astype(BF16), w_sh_gate[l].astype(BF16), w_sh_up[l].astype(BF16),
        w_sh_down[l].astype(BF16), bias_col, tm)

    counts = cnt[:, 0].astype(jnp.int32)
    half = MOE_BLOCK // 2
    p_rows = (-(-(n * TOP_K) // half) + N_EXPERTS) * half
    nb_max = -(-(n * TOP_K) // MOE_BLOCK) + N_EXPERTS
    pstart, blk0, nblk, row0, is_half, n_used = _expert_plan(counts, nb_max)

    xs, pos = _dispatch_rows(h2p, top_e, rank, pstart, p_rows)
    ys = _experts(blk0, nblk, counts, row0, is_half, n_used, xs,
                  w_exp_gate[l], w_exp_up[l], w_exp_down[l])
    out = base
    for bi in range(b):
        idx = pos[:, bi * t:(bi + 1) * t].reshape(-1)
        yg = _gather_rows(ys, idx).reshape(TOP_K, t, d // 2)
        out = _combine(out, bi, mod_x, top_w, yg, tm)
    return out
```

```python
import functools

import numpy as np
import jax
import jax.numpy as jnp
from jax import lax
from jax.experimental import pallas as pl
from jax.experimental.pallas import tpu as pltpu
from jax.experimental.pallas import tpu_sc as plsc

F32 = jnp.float32
BF16 = jnp.bfloat16

EPS = 1e-6
GRID_W = 64
M_HEADS, M_DQK, M_DV = 4, 128, 256
ROPE_BASE = 10000.0
NA_HEADS, NA_DH, NA_KH, NA_KW = 8, 64, 8, 16
N_EXPERTS, TOP_K, N_GROUPS, TOPK_GROUPS = 256, 8, 8, 4
ROUTE_SCALE = 2.5

LANES = 128
VMEM_LIMIT = 56 * 1024 * 1024
NEG = -1e30
LOG2E = 1.4426950408889634

MLSTM_CHUNK = 256
NA_ROWS = 4
NA_KEY_ROWS = NA_ROWS + NA_KH - 1
NA_SUB = 8
MOE_BLOCK = 512
EXPERT_SLOTS = 4
WEIGHT_SLOTS = 3

_W_SEGS = (("mq", 512), ("mv", 1024), ("mo", 1024),
           ("nq", 512), ("nk", 512), ("nv", 512), ("gm", 1024), ("gn", 1024))
_W_OFF = {}
_o = 0
for _n, _w in _W_SEGS:
    _W_OFF[_n] = (_o, _w)
    _o += _w
W_COLS = _o


def _dot(a, b):
    return jnp.dot(a, b, preferred_element_type=F32)


def _dot_nt(a, b):
    return lax.dot_general(a, b, (((1,), (1,)), ((), ())), preferred_element_type=F32)


def _sigmoid(x):
    return 1.0 / (1.0 + jnp.exp(-x))


def _pack_bf16_pairs(v):
    w = v.shape[1] // 2
    bits = pltpu.bitcast(v.astype(BF16).astype(F32), jnp.int32)
    return lax.shift_right_logical(bits[:, :w], 16) | bits[:, w:]


def _unpack_bf16_pairs(p):
    lo = pltpu.bitcast(lax.shift_left(p, 16), F32)
    hi = pltpu.bitcast(p & jnp.int32(-65536), F32)
    return jnp.concatenate([lo, hi], axis=1)


def _params(*sem):
    return pltpu.CompilerParams(dimension_semantics=sem, vmem_limit_bytes=VMEM_LIMIT)


def _resident(shape):
    nd = len(shape)
    return pl.BlockSpec(shape, lambda *_: (0,) * nd, pipeline_mode=pl.Buffered(1))


def _ada_kernel(c_ref, w_ref, b_ref, o_ref):
    c = c_ref[...]
    s = c * _sigmoid(c)
    o_ref[...] = _dot(s.astype(BF16), w_ref[...].astype(BF16)) + b_ref[...]


def _ada(cc, w_ada, b_ada):
    d = cc.shape[1]
    n = w_ada.shape[1]
    return pl.pallas_call(
        _ada_kernel,
        grid=(n // d,),
        in_specs=[pl.BlockSpec((8, d), lambda j: (0, 0)),
                  pl.BlockSpec((d, d), lambda j: (0, j)),
                  pl.BlockSpec((1, d), lambda j: (0, j))],
        out_specs=pl.BlockSpec((8, d), lambda j: (0, j)),
        out_shape=jax.ShapeDtypeStruct((8, n), F32),
        compiler_params=_params("arbitrary"),
        name="ada",
    )(cc, w_ada, b_ada.reshape(1, n))


def _rope_rotate(t, cos, sin):
    q = M_DQK // 4
    lane = lax.broadcasted_iota(jnp.int32, t.shape, 1)
    partner = jnp.where((lane & q) == 0, pltpu.roll(t, M_DQK - q, 1), pltpu.roll(t, q, 1))
    return t * cos + partner * sin


def _rope_rotate_t(t, cos, sin):
    q = M_DQK // 4
    partner = jnp.concatenate([t[q:2 * q], t[0:q], t[3 * q:4 * q], t[2 * q:3 * q]], axis=0)
    return t * cos + partner * sin


def _inproj_kernel(*refs, rope):
    if rope:
        (x_ref, mod_ref, w_ref, wt_ref, bgt_ref, qnw_ref, knw_ref, seg_ref, segt_ref,
         rcos_ref, rsin_ref, ccos_ref, csin_ref,
         mq_ref, mv_ref, mo_ref, nq_ref, nk_ref, nv_ref, gm_ref, gn_ref,
         mkt_ref, git_ref, gft_ref) = refs
    else:
        (x_ref, mod_ref, w_ref, wt_ref, bgt_ref, qnw_ref, knw_ref, seg_ref, segt_ref,
         mq_ref, mv_ref, mo_ref, nq_ref, nk_ref, nv_ref, gm_ref, gn_ref,
         mkt_ref, git_ref, gft_ref) = refs
    x = x_ref[0]
    xn = x * lax.rsqrt(jnp.mean(x * x, axis=-1, keepdims=True) + EPS)
    h = xn * (1.0 + mod_ref[0, 1:2, :]) + mod_ref[0, 0:1, :]
    hb = h.astype(BF16)

    def proj(name):
        off, width = _W_OFF[name]
        return _dot(hb, w_ref[:, off:off + width])

    def head_rms(t, w_row, scale):
        ss = _dot((t * t).astype(BF16), seg_ref[...])
        r = lax.rsqrt(ss * (1.0 / NA_DH) + EPS)
        r_hi = r.astype(BF16)
        r_lo = (r - r_hi.astype(F32)).astype(BF16)
        rb = _dot(jnp.concatenate([r_hi, r_lo], axis=1), segt_ref[...])
        return t * rb * w_row * scale

    if rope:
        tm = x.shape[0]
        spread = lambda r: jnp.broadcast_to(r[:, None, :], (tm // GRID_W, GRID_W, LANES)).reshape(tm, LANES)
        cos = spread(rcos_ref[...]) + ccos_ref[...]
        sin = spread(rsin_ref[...]) + csin_ref[...]

    def fin_mq(t):
        t = t * (M_DQK ** -0.5)
        if rope:
            t = jnp.concatenate([_rope_rotate(t[:, i * LANES:(i + 1) * LANES], cos, sin)
                                 for i in range(M_HEADS)], axis=1)
        mq_ref[0] = t.astype(BF16)

    def fin_mkt(t):
        if rope:
            cost, sint = jnp.transpose(cos), jnp.transpose(sin)
            t = jnp.concatenate([_rope_rotate_t(t[i * M_DQK:(i + 1) * M_DQK], cost, sint)
                                 for i in range(M_HEADS)], axis=0)
        mkt_ref[0] = t.astype(BF16)

    def store(ref, fn=lambda t: t):
        def fin(t):
            ref[0] = fn(t).astype(ref.dtype)
        return fin

    qk_w = M_HEADS * M_DQK

    def fin_feature_major(t):
        fin_mkt(t[0:qk_w])
        git_ref[0] = t[qk_w:qk_w + LANES] + bgt_ref[0:LANES, :]
        gft_ref[0] = t[qk_w + LANES:qk_w + 2 * LANES] + bgt_ref[LANES:2 * LANES, :]

    stages = [
        (lambda: proj("mq"), fin_mq),
        (lambda: proj("mv"), store(mv_ref)),
        (lambda: proj("mo"), store(mo_ref, _sigmoid)),
        (lambda: proj("nq"), store(nq_ref, lambda t: head_rms(t, qnw_ref[...], NA_DH ** -0.5 * LOG2E))),
        (lambda: proj("nk"), store(nk_ref, lambda t: head_rms(t, knw_ref[...], 1.0))),
        (lambda: proj("nv"), store(nv_ref)),
        (lambda: proj("gm"), store(gm_ref, _sigmoid)),
        (lambda: proj("gn"), store(gn_ref, _sigmoid)),
        (lambda: _dot_nt(wt_ref[...], hb), fin_feature_major),
    ]
    acc = stages[0][0]()
    for i, (_, finish) in enumerate(stages):
        nxt = stages[i + 1][0]() if i + 1 < len(stages) else None
        finish(acc)
        acc = nxt


def _inproj(x, mod, w_all, wt_all, bgt, qnw, knw, seg, segt, rope_tabs, tm):
    b, t, d = x.shape
    rope = rope_tabs is not None
    tok = lambda w: pl.BlockSpec((1, tm, w), lambda bi, i: (bi, i, 0))
    tok_t = lambda w: pl.BlockSpec((1, w, tm), lambda bi, i: (bi, 0, i))
    in_specs = [tok(d),
                pl.BlockSpec((1, 8, d), lambda bi, i: (bi, 0, 0)),
                _resident(w_all.shape), _resident(wt_all.shape),
                _resident(bgt.shape), _resident(qnw.shape),
                _resident(knw.shape), _resident(seg.shape), _resident(segt.shape)]
    args = [x, mod, w_all, wt_all, bgt, qnw, knw, seg, segt]
    if rope:
        in_specs += [pl.BlockSpec((tm // GRID_W, LANES), lambda bi, i: (i, 0))] * 2
        in_specs += [_resident((tm, LANES))] * 2
        args += list(rope_tabs)
    widths = [("mq", BF16), ("mv", BF16), ("mo", BF16),
              ("nq", BF16), ("nk", BF16), ("nv", BF16), ("gm", BF16), ("gn", BF16)]
    out_specs = [tok(_W_OFF[n][1]) for n, _ in widths]
    out_shape = [jax.ShapeDtypeStruct((b, t, _W_OFF[n][1]), dt) for n, dt in widths]
    out_specs += [tok_t(M_HEADS * M_DQK), tok_t(LANES), tok_t(LANES)]
    out_shape += [jax.ShapeDtypeStruct((b, M_HEADS * M_DQK, t), BF16),
                  jax.ShapeDtypeStruct((b, LANES, t), F32),
                  jax.ShapeDtypeStruct((b, LANES, t), F32)]
    return pl.pallas_call(
        functools.partial(_inproj_kernel, rope=rope),
        grid=(b, t // tm),
        in_specs=in_specs, out_specs=out_specs, out_shape=out_shape,
        compiler_params=_params("parallel", "parallel"),
        name="inproj_rope" if rope else "inproj_ctx",
    )(*args)


def _log_sigmoid(x):
    return jnp.minimum(x, 0.0) - jnp.log(1.0 + jnp.exp(-jnp.abs(x)))


def _dot_split(a, b):
    hi = a.astype(BF16)
    lo = (a - hi.astype(F32)).astype(BF16)
    return _dot(hi, b) + _dot(lo, b)


MLSTM_EXT = M_DV + LANES


def _mlstm_kernel(qf_ref, ktf_ref, vf_ref, gitf_ref, gftf_ref,
                  qb_ref, ktb_ref, vb_ref, gitb_ref, gftb_ref,
                  c0_ref, m0_ref,
                  hf_ref, hb_ref, cn_ref, mn_ref,
                  *scratch):
    c_scrs, m_scr = scratch[:2 * M_HEADS], scratch[2 * M_HEADS]
    step = pl.program_id(1)
    L = qf_ref.shape[1]
    nu = 2 * M_HEADS

    @pl.when(step == 0)
    def _():
        for j, c_scr in enumerate(c_scrs):
            c_scr[...] = c0_ref[0, j]
        m_scr[...] = m0_ref[0]

    row_i = lax.broadcasted_iota(jnp.int32, (L, L), 0)
    col_i = lax.broadcasted_iota(jnp.int32, (L, L), 1)
    lower = col_i <= row_i
    upper = col_i >= row_i
    tri_lo = jnp.where(lower, 1.0, 0.0).astype(BF16)
    tri_up = jnp.where(upper, 1.0, 0.0).astype(BF16)

    is_f = lax.broadcasted_iota(jnp.int32, (nu, L), 0) < M_HEADS
    gi_t = jnp.where(is_f, gitf_ref[0, 0:nu, :], gitb_ref[0, 0:nu, :]) * LOG2E
    ls_tf = _log_sigmoid(gftf_ref[0, 0:nu, :]) * LOG2E
    ls_tb = _log_sigmoid(gftb_ref[0, 0:nu, :]) * LOG2E
    b_t = jnp.where(is_f, _dot_split(ls_tf, tri_up), _dot_split(ls_tb, tri_lo))
    u_t = gi_t - b_t
    g_c = jnp.sum(jnp.where(is_f, ls_tf, ls_tb), axis=1, keepdims=True)
    m_prev = m_scr[...]
    a_t = g_c + u_t
    m_new = jnp.maximum(g_c + m_prev, jnp.max(a_t, axis=1, keepdims=True))
    decay = jnp.exp2(g_c + m_prev - m_new)
    wa_t = jnp.exp2(a_t - jnp.concatenate([m_new] * (L // LANES), axis=1))
    m_scr[...] = m_new

    ones = jnp.ones((L, LANES), BF16)
    dirs = ((qf_ref, ktf_ref, vf_ref, hf_ref, lower), (qb_ref, ktb_ref, vb_ref, hb_ref, upper))
    b_cols = jnp.transpose(b_t)

    def head(j):
        d, hd = divmod(j, M_HEADS)
        q_ref, kt_ref, v_ref, _, mask = dirs[d]
        q = q_ref[0, :, hd * M_DQK:(hd + 1) * M_DQK]
        k_t = kt_ref[0, hd * M_DQK:(hd + 1) * M_DQK, :]
        v_ext = jnp.concatenate([v_ref[0, :, hd * M_DV:(hd + 1) * M_DV], ones], axis=1)
        c_prev = c_scrs[j][...]
        u_row = u_t[j:j + 1, :]
        m_loc = jnp.max(jnp.where(mask, u_row, NEG), axis=1, keepdims=True)
        kw = (k_t.astype(F32) * wa_t[j:j + 1, :]).astype(BF16)
        dec = jnp.concatenate([decay[j:j + 1, :]] * (MLSTM_EXT // LANES), axis=1)
        c_scrs[j][...] = dec * c_prev + _dot(kw, v_ext)
        return q, v_ext, c_prev.astype(BF16), u_row, m_loc, _dot(q, k_t)

    def tail(j, q, v_ext, c_prev, u_row, m_loc, s_raw):
        d, hd = divmod(j, M_HEADS)
        h_ref, mask = dirs[d][3], dirs[d][4]
        mp_row = m_prev[j:j + 1, :]
        m_rep = jnp.maximum(jnp.broadcast_to(m_loc, (L, LANES)), mp_row)
        m_wide = jnp.concatenate([m_rep] * (L // LANES), axis=1)
        s = (s_raw * jnp.exp2(jnp.where(mask, u_row - m_wide, NEG))).astype(BF16)
        qw = (q.astype(F32) * jnp.exp2(mp_row - m_rep)).astype(BF16)
        r = _dot(s, v_ext) + _dot(qw, c_prev)
        b_rep = jnp.broadcast_to(b_cols[:, j:j + 1], (L, LANES))
        dn = jnp.maximum(jnp.abs(r[:, M_DV:]), jnp.exp2(-(b_rep + m_rep)))
        h_ref[0, :, hd * M_DV:(hd + 1) * M_DV] = (
            r[:, :M_DV] / jnp.concatenate([dn] * (M_DV // LANES), axis=1)).astype(h_ref.dtype)

    nxt = head(0)
    for j in range(nu):
        cur = nxt
        if j + 1 < nu:
            nxt = head(j + 1)
        tail(j, *cur)

    @pl.when(step == pl.num_programs(1) - 1)
    def _():
        for j, c_scr in enumerate(c_scrs):
            cn_ref[0, j] = c_scr[...]
        mn_ref[0] = m_scr[...]


def _mlstm(q, kt, v, git, gft, c0, m0):
    b, t, _ = q.shape
    L = min(MLSTM_CHUNK, t)
    nc = t // L
    fwd = lambda w: pl.BlockSpec((1, L, w), lambda bi, i: (bi, i, 0))
    bwd = lambda w: pl.BlockSpec((1, L, w), lambda bi, i: (bi, nc - 1 - i, 0))
    fwd_t = lambda w: pl.BlockSpec((1, w, L), lambda bi, i: (bi, 0, i))
    bwd_t = lambda w: pl.BlockSpec((1, w, L), lambda bi, i: (bi, 0, nc - 1 - i))
    st_c = pl.BlockSpec((1, 8, M_DQK, MLSTM_EXT), lambda bi, i: (bi, 0, 0, 0))
    st_v = pl.BlockSpec((1, 8, LANES), lambda bi, i: (bi, 0, 0))
    qk_w, v_w = M_HEADS * M_DQK, M_HEADS * M_DV
    return pl.pallas_call(
        _mlstm_kernel,
        grid=(b, nc),
        in_specs=[fwd(qk_w), fwd_t(qk_w), fwd(v_w), fwd_t(LANES), fwd_t(LANES),
                  bwd(qk_w), bwd_t(qk_w), bwd(v_w), bwd_t(LANES), bwd_t(LANES),
                  st_c, st_v],
        out_specs=[fwd(v_w), bwd(v_w), st_c, st_v],
        out_shape=[jax.ShapeDtypeStruct((b, t, v_w), BF16),
                   jax.ShapeDtypeStruct((b, t, v_w), BF16),
                   jax.ShapeDtypeStruct(c0.shape, F32),
                   jax.ShapeDtypeStruct(m0.shape, F32)],
        scratch_shapes=([pltpu.VMEM((M_DQK, MLSTM_EXT), F32) for _ in range(2 * M_HEADS)]
                        + [pltpu.VMEM((8, LANES), F32)]),
        compiler_params=_params("parallel", "arbitrary"),
        name="mlstm",
    )(q, kt, v, git, gft, q, kt, v, git, gft, c0, m0)


def _na_kernel(q_ref, k_ref, v_ref, kc_ref, vc_ref, bias_ref, o_ref, *, rows, nsub):
    tq = NA_ROWS * GRID_W
    nkeys = NA_KEY_ROWS * GRID_W
    last_rb = rows // NA_ROWS - 1
    kc = kc_ref[0]
    vc = vc_ref[0]
    lane = lax.broadcasted_iota(jnp.int32, (tq, LANES), 1)

    def logits(sb, hh):
        rb = pl.program_id(2) * nsub + sb
        kind = jnp.where(rb == 0, 0, jnp.where(rb == last_rb, 2, 1))
        ks = jnp.clip(rb * NA_ROWS - NA_KH // 2, 0, rows - NA_KEY_ROWS)
        kstart = pl.multiple_of(ks * GRID_W, GRID_W)
        kblk = k_ref[0, pl.ds(kstart, nkeys), :]
        q = q_ref[0, sb * tq:(sb + 1) * tq, :]
        in_head = (lane < NA_DH) if hh == 0 else (lane >= NA_DH)
        qm = jnp.where(in_head, q, jnp.zeros_like(q))
        return _dot_nt(qm, kblk) + bias_ref[hh, kind], _dot_nt(qm, kc), kstart

    chains = [(sb, hh) for sb in range(nsub) for hh in range(2)]
    nxt = logits(*chains[0])
    outs = []
    for i, (sb, hh) in enumerate(chains):
        sw, sc, kstart = nxt
        if i + 1 < len(chains):
            nxt = logits(*chains[i + 1])
        m = jnp.maximum(jnp.max(sw, axis=1, keepdims=True), jnp.max(sc, axis=1, keepdims=True))
        ew = jnp.exp2(sw - m)
        ec = jnp.exp2(sc - m)
        l = jnp.sum(ew, axis=1, keepdims=True) + jnp.sum(ec, axis=1, keepdims=True)
        vblk = v_ref[0, pl.ds(kstart, nkeys), :]
        o = _dot(ew.astype(BF16), vblk) + _dot(ec.astype(BF16), vc)
        outs.append(o / l)
        if hh == 1:
            o_ref[0, sb * tq:(sb + 1) * tq, :] = jnp.where(lane < NA_DH, outs[0], outs[1]).astype(o_ref.dtype)
            outs = []


def _na(nq, nk, nv, cnk, cnv, bias):
    b, t, w = nq.shape
    rows = t // GRID_W
    tq = NA_ROWS * GRID_W
    nrb = rows // NA_ROWS
    nsub = min(NA_SUB, nrb)
    nctx = cnk.shape[1]
    return pl.pallas_call(
        functools.partial(_na_kernel, rows=rows, nsub=nsub),
        grid=(b, w // LANES, nrb // nsub),
        in_specs=[pl.BlockSpec((1, nsub * tq, LANES), lambda bi, hp, st: (bi, st, hp)),
                  pl.BlockSpec((1, t, LANES), lambda bi, hp, st: (bi, 0, hp)),
                  pl.BlockSpec((1, t, LANES), lambda bi, hp, st: (bi, 0, hp)),
                  pl.BlockSpec((1, nctx, LANES), lambda bi, hp, st: (bi, 0, hp)),
                  pl.BlockSpec((1, nctx, LANES), lambda bi, hp, st: (bi, 0, hp)),
                  pl.BlockSpec((2,) + bias.shape[1:], lambda bi, hp, st: (hp, 0, 0, 0))],
        out_specs=pl.BlockSpec((1, nsub * tq, LANES), lambda bi, hp, st: (bi, st, hp)),
        out_shape=jax.ShapeDtypeStruct((b, t, w), BF16),
        compiler_params=_params("parallel", "parallel", "arbitrary"),
        name="na",
    )(nq, nk, nv, cnk, cnv, bias)


def _na_bias_table(na_rpb, rows):
    h = na_rpb.shape[0]
    w = GRID_W
    c = np.arange(w)[:, None]
    kj = np.arange(w)[None, :]
    cs = np.clip(c - NA_KW // 2, 0, w - NA_KW)
    col_valid = (kj >= cs) & (kj < cs + NA_KW)
    dc = np.clip(kj - c + (NA_KW - 1), 0, 2 * NA_KW - 2)
    onehot = np.zeros((2 * NA_KW - 1, w, w), np.float32)
    onehot[dc, np.arange(w)[:, None], np.arange(w)[None, :]] = 1.0
    t2 = jnp.einsum("hrd,dck->hrck", na_rpb, jnp.asarray(onehot), precision=lax.Precision.HIGHEST)
    t2 = jnp.where(jnp.asarray(col_valid)[None, None], t2 * LOG2E, NEG)
    t2 = jnp.concatenate([t2, jnp.full((h, 1, w, w), NEG, F32)], axis=1)
    invalid = 2 * NA_KH - 1
    dr_idx = np.full((3, NA_ROWS, NA_KEY_ROWS), invalid, np.int32)
    for kind, r0 in enumerate((0, NA_ROWS, rows - NA_ROWS)):
        ks = int(np.clip(r0 - NA_KH // 2, 0, rows - NA_KEY_ROWS))
        for qa in range(NA_ROWS):
            r = r0 + qa
            rs = int(np.clip(r - NA_KH // 2, 0, rows - NA_KH))
            for kl in range(NA_KEY_ROWS):
                ki = ks + kl
                if rs <= ki < rs + NA_KH:
                    dr_idx[kind, qa, kl] = ki - r + NA_KH - 1
    t2t = t2.transpose(0, 2, 1, 3)
    strips = [jnp.concatenate([t2t[:, :, int(dr), :] for dr in dr_idx[kind, qa]], axis=-1)
              for kind in range(3) for qa in range(NA_ROWS)]
    return jnp.stack(strips, axis=1).reshape(h, 3, NA_ROWS * w, NA_KEY_ROWS * w)


def _interleave(*streams):
    live = list(streams)
    while live:
        for g in list(live):
            try:
                next(g)
            except StopIteration:
                live.remove(g)


def _post_kernel(x_ref, mod_ref, hf_ref, hb_ref, mo_ref, na_ref, gm_ref, gn_ref,
                 mnw_ref, wbm_ref, wbn_ref, wout_ref, wr_ref, wsg_ref, wsu_ref, wsd_ref, rb_ref,
                 h2_ref, base_ref, e_ref, w_ref, r_ref, cnt_ref, run_scr):
    @pl.when(pl.program_id(0) == 0)
    def _():
        run_scr[...] = jnp.zeros_like(run_scr)

    hm = hf_ref[0].astype(F32) + hb_ref[0].astype(F32)
    parts = []
    for hd in range(M_HEADS):
        t = hm[:, hd * M_DV:(hd + 1) * M_DV]
        parts.append(t * lax.rsqrt(jnp.mean(t * t, axis=-1, keepdims=True) + EPS))
    y_m = jnp.concatenate(parts, axis=1) * mnw_ref[...] * mo_ref[0].astype(F32)
    a = _dot(y_m.astype(BF16), wbm_ref[...])
    bn = _dot(na_ref[0], wbn_ref[...])
    z = gm_ref[0].astype(F32) * a + gn_ref[0].astype(F32) * bn
    y = _dot(z.astype(BF16), wout_ref[...])
    x1 = x_ref[0] + mod_ref[0, 2:3, :] * y
    xn = x1 * lax.rsqrt(jnp.mean(x1 * x1, axis=-1, keepdims=True) + EPS)
    h2f = xn * (1.0 + mod_ref[0, 4:5, :]) + mod_ref[0, 3:4, :]
    h2_ref[...] = _pack_bf16_pairs(h2f)
    h2 = h2f.astype(BF16)
    scores = _sigmoid(_dot_nt(wr_ref[...], h2))

    def shared_expert():
        sg = _dot(h2, wsg_ref[...])
        su = _dot(h2, wsu_ref[...])
        yield
        sh = sg * _sigmoid(sg) * su
        base_ref[0] = x1 + mod_ref[0, 5:6, :] * _dot(sh.astype(BF16), wsd_ref[...])

    _interleave(_route_stages(scores, rb_ref, e_ref, w_ref, r_ref, cnt_ref, run_scr), shared_expert())


def _post(x, mod, hf, hb, mo, yna, gm, gn, mnw, wbm, wbn, wout, wr_t, wsg, wsu, wsd, rbias, tm):
    b, t, d = x.shape
    nt = t // tm
    n = b * t
    tok = lambda w: pl.BlockSpec((1, tm, w), lambda s: (s // nt, s % nt, 0))
    rt = lambda: pl.BlockSpec((TOP_K, tm), lambda s: (0, s))
    res = [mnw, wbm, wbn, wout, wr_t, wsg, wsu, wsd, rbias]
    return pl.pallas_call(
        _post_kernel,
        grid=(b * nt,),
        in_specs=[tok(d), pl.BlockSpec((1, 8, d), lambda s: (s // nt, 0, 0)),
                  tok(hf.shape[2]), tok(hb.shape[2]), tok(mo.shape[2]), tok(yna.shape[2]),
                  tok(gm.shape[2]), tok(gn.shape[2])] + [_resident(a.shape) for a in res],
        out_specs=[pl.BlockSpec((tm, d // 2), lambda s: (s, 0)),
                   tok(d), rt(), pl.BlockSpec((tm, TOP_K), lambda s: (s, 0)), rt(),
                   pl.BlockSpec((N_EXPERTS, LANES), lambda s: (0, 0))],
        out_shape=[jax.ShapeDtypeStruct((n, d // 2), jnp.int32),
                   jax.ShapeDtypeStruct((b, t, d), F32),
                   jax.ShapeDtypeStruct((TOP_K, n), jnp.int32),
                   jax.ShapeDtypeStruct((n, TOP_K), F32),
                   jax.ShapeDtypeStruct((TOP_K, n), jnp.int32),
                   jax.ShapeDtypeStruct((N_EXPERTS, LANES), F32)],
        scratch_shapes=[pltpu.VMEM((N_EXPERTS, LANES), F32)],
        compiler_params=_params("arbitrary"),
        name="post",
    )(x, mod, hf, hb, mo, yna, gm, gn, *res)


def _route_stages(s, b_ref, e_ref, w_ref, r_ref, cnt_ref, run_scr):
    tm = s.shape[1]
    sel = s + b_ref[...][:, 0:1]
    gsz = N_EXPERTS // N_GROUPS
    ninf = -jnp.inf

    x3 = sel.reshape(N_GROUPS, gsz, tm)
    r3 = lax.broadcasted_iota(jnp.int32, x3.shape, 1)
    m1 = jnp.max(x3, axis=1, keepdims=True)
    i1 = jnp.min(jnp.where(x3 == m1, r3, gsz), axis=1, keepdims=True)
    m2 = jnp.max(jnp.where(r3 == i1, ninf, x3), axis=1)
    gs = m1[:, 0, :] + m2

    gidx = lax.broadcasted_iota(jnp.int32, gs.shape, 0)
    gkeep = jnp.zeros(gs.shape, jnp.bool_)
    cur = gs
    for _ in range(TOPK_GROUPS):
        mm = jnp.max(cur, axis=0, keepdims=True)
        ii = jnp.min(jnp.where(cur == mm, gidx, N_GROUPS), axis=0, keepdims=True)
        hit = gidx == ii
        gkeep = jnp.logical_or(gkeep, hit)
        cur = jnp.where(hit, ninf, cur)
    keep = jnp.broadcast_to(gkeep[:, None, :], x3.shape).reshape(N_EXPERTS, tm)
    yield

    row = lax.broadcasted_iota(jnp.int32, s.shape, 0).astype(F32)
    cur = jnp.where(keep, sel, ninf)
    idxs, ws = [], []
    chosen_f = jnp.zeros(s.shape, F32)
    for kk in range(TOP_K):
        mm = jnp.max(cur, axis=0, keepdims=True)
        ii = jnp.min(jnp.where(cur == mm, row, float(N_EXPERTS)), axis=0, keepdims=True)
        hit = row == ii
        idxs.append(ii)
        ws.append(jnp.sum(jnp.where(hit, s, 0.0), axis=0, keepdims=True))
        chosen_f = jnp.where(hit, 1.0, chosen_f)
        cur = jnp.where(hit, ninf, cur)
        if kk == TOP_K // 2 - 1:
            yield
    wsum = ws[0]
    for wk in ws[1:]:
        wsum = wsum + wk

    tp =lax.broadcasted_iota(jnp.int32, (tm, tm), 0)
    tc = lax.broadcasted_iota(jnp.int32, (tm, tm), 1)
    before = jnp.where(tp < tc, 1.0, 0.0).astype(BF16)
    rank = _dot(chosen_f.astype(BF16), before) + run_scr[...][:, 0:1]
    run_scr[...] = run_scr[...] + jnp.sum(chosen_f, axis=1, keepdims=True)
    cnt_ref[...] = run_scr[...]

    for kk in range(TOP_K):
        e_ref[kk:kk + 1, :] = idxs[kk].astype(jnp.int32)
        r_ref[kk:kk + 1, :] = jnp.sum(jnp.where(row == idxs[kk], rank, 0.0), axis=0,
                                      keepdims=True).astype(jnp.int32)
    w_ref[...] = jnp.transpose(jnp.concatenate([wk / wsum * ROUTE_SCALE for wk in ws], axis=0))


SC_WINDOW = 128


def _sc_mesh():
    return plsc.VectorSubcoreMesh(core_axis_name="core", subcore_axis_name="subcore")


def _sc_workers():
    info = plsc.get_sparse_core_info()
    return info.num_cores, info.num_cores * info.num_subcores


def _dispatch_rows(x, top_e, rank, pstart, p_rows):
    n, w = x.shape
    kk = top_e.shape[0]
    ncores, nw = _sc_workers()
    lanes = plsc.get_sparse_core_info().num_lanes
    steps = n // nw // SC_WINDOW
    per_worker = lambda a: a.reshape(kk, nw, steps, SC_WINDOW).transpose(1, 2, 0, 3)

    @functools.partial(
        pl.kernel, mesh=_sc_mesh(),
        out_type=[jax.ShapeDtypeStruct((p_rows, w), x.dtype),
                  jax.ShapeDtypeStruct((nw, steps, kk, SC_WINDOW), jnp.int32)],
        scratch_types=[pltpu.VMEM((kk, SC_WINDOW), jnp.int32),
                       pltpu.VMEM((kk, SC_WINDOW), jnp.int32),
                       pltpu.VMEM((kk, SC_WINDOW), jnp.int32),
                       pltpu.VMEM(pstart.shape, jnp.int32),
                       pltpu.VMEM((SC_WINDOW, w), x.dtype),
                       pltpu.SemaphoreType.DMA],
        compiler_params=pltpu.CompilerParams(needs_layout_passes=False),
    )
    def scatter(x_hbm, e_hbm, r_hbm, ps_hbm, o_hbm, pos_hbm, e_v, r_v, pos_v, ps_v, rows_v, sem):
        wid = lax.axis_index("subcore") * ncores + lax.axis_index("core")
        pltpu.sync_copy(ps_hbm, ps_v)

        @pl.loop(0, steps)
        def _(s):
            base = pl.multiple_of((wid * steps + s) * SC_WINDOW, SC_WINDOW)
            pltpu.sync_copy(e_hbm.at[wid, s], e_v)
            pltpu.sync_copy(r_hbm.at[wid, s], r_v)
            pltpu.sync_copy(x_hbm.at[pl.ds(base, SC_WINDOW)], rows_v)
            for j in range(kk):
                for c in range(SC_WINDOW // lanes):
                    cols = pl.ds(c * lanes, lanes)
                    pos_v[j, cols] = plsc.load_gather(ps_v, [e_v[j, cols]]) + r_v[j, cols]
            pltpu.sync_copy(pos_v, pos_hbm.at[wid, s])
            copies = [pltpu.make_async_copy(rows_v, o_hbm.at[pos_v.at[j]], sem) for j in range(kk)]
            for cp in copies:
                cp.start()
            for cp in copies:
                cp.wait()

    out, pos4 = scatter(x, per_worker(top_e), per_worker(rank), pstart)
    return out, pos4.transpose(2, 0, 1, 3).reshape(kk, n)


def _gather_rows(x, idx):
    m = idx.shape[0]
    w = x.shape[1]
    ncores, nw = _sc_workers()
    steps = m // nw // SC_WINDOW
    idx3 = idx.reshape(nw, steps, SC_WINDOW)

    @functools.partial(
        pl.kernel, mesh=_sc_mesh(),
        out_type=jax.ShapeDtypeStruct((m, w), x.dtype),
        scratch_types=[pltpu.VMEM((steps, SC_WINDOW), jnp.int32),
                       pltpu.VMEM((SC_WINDOW, w), x.dtype),
                       pltpu.SemaphoreType.DMA],
    )
    def gather(x_hbm, i_hbm, o_hbm, idx_v, rows_v, sem):
        wid = lax.axis_index("subcore") * ncores + lax.axis_index("core")
        pltpu.sync_copy(i_hbm.at[wid], idx_v)

        @pl.loop(0, steps)
        def _(s):
            pltpu.async_copy(x_hbm.at[idx_v.at[s]], rows_v, sem).wait()
            base = pl.multiple_of((wid * steps + s) * SC_WINDOW, SC_WINDOW)
            pltpu.sync_copy(rows_v, o_hbm.at[pl.ds(base, SC_WINDOW)])

    return gather(x, idx3)


def _experts_kernel(blk0_ref, nblk_ref, cnt_ref, row0_ref, half_ref, nu_ref,
                    x_hbm, wg_hbm, wu_hbm, wd_hbm, y_hbm,
                    xbuf, ybuf, wg_raw, wu_raw, wd_raw, wg_scr, wu_scr, wd_scr, in_sem, out_sem, w_sem):
    e = pl.program_id(0)
    ne = pl.num_programs(0)
    n_used = nu_ref[0]
    blk0 = blk0_ref[e]
    ns = EXPERT_SLOTS
    sizes = (MOE_BLOCK, MOE_BLOCK // 2)

    def w_copies(ex):
        slot = ex % WEIGHT_SLOTS
        return [pltpu.make_async_copy(hbm.at[ex], raw.at[slot], w_sem.at[i, slot])
                for i, (hbm, raw) in enumerate(((wg_hbm, wg_raw), (wu_hbm, wu_raw), (wd_hbm, wd_raw)))]

    @pl.when(e == 0)
    def _():
        for e0 in range(WEIGHT_SLOTS - 1):
            @pl.when(e0 < ne)
            def _():
                for cp in w_copies(e0):
                    cp.start()

    @pl.when(e + WEIGHT_SLOTS - 1 < ne)
    def _():
        for cp in w_copies(e + WEIGHT_SLOTS - 1):
            cp.start()

    def x_copy(g, rows):
        r0 = pl.multiple_of(row0_ref[g], MOE_BLOCK // 2)
        return pltpu.make_async_copy(x_hbm.at[pl.ds(r0, rows)], xbuf.at[g % ns, pl.ds(0, rows)],
                                     in_sem.at[g % ns])

    def y_copy(g, rows):
        r0 = pl.multiple_of(row0_ref[g], MOE_BLOCK // 2)
        return pltpu.make_async_copy(ybuf.at[g % ns, pl.ds(0, rows)], y_hbm.at[pl.ds(r0, rows)],
                                     out_sem.at[g % ns])

    def by_size(g, fn):
        for is_half, rows in enumerate(sizes):
            @pl.when(half_ref[g] == is_half)
            def _():
                fn(rows)

    @pl.when(e == 0)
    def _():
        for g0 in range(ns - 1):
            @pl.when(g0 < n_used)
            def _():
                by_size(g0, lambda rows: x_copy(g0, rows).start())

    for cp in w_copies(e):
        cp.wait()
    wslot = e % WEIGHT_SLOTS
    wg_scr[...] = wg_raw[wslot].astype(BF16)
    wu_scr[...] = wu_raw[wslot].astype(BF16)
    wd_scr[...] = wd_raw[wslot].astype(BF16)

    def block(b, carry):
        g = blk0 + b
        by_size(g, lambda rows: x_copy(g, rows).wait())

        @pl.when(g + ns - 1 < n_used)
        def _():
            by_size(g + ns - 1, lambda rows: x_copy(g + ns - 1, rows).start())

        @pl.when(g >= ns)
        def _():
            by_size(g - ns, lambda rows: y_copy(g - ns, rows).wait())

        def run(rows):
            rid = lax.broadcasted_iota(jnp.int32, (rows, xbuf.shape[2]), 0)
            xp = jnp.where(rid < cnt_ref[e] - b * MOE_BLOCK, xbuf[g % ns, 0:rows], 0)
            x = _unpack_bf16_pairs(xp).astype(BF16)
            gt = _dot(x, wg_scr[...])
            up = _dot(x, wu_scr[...])
            a = (gt * _sigmoid(gt) * up).astype(BF16)
            ybuf[g % ns, 0:rows] = _pack_bf16_pairs(_dot(a, wd_scr[...]))
            y_copy(g, rows).start()

        by_size(g, run)
        return carry

    lax.fori_loop(0, nblk_ref[e], block, 0)

    @pl.when(e == pl.num_programs(0) - 1)
    def _():
        for back in range(ns, 0, -1):
            @pl.when(n_used >= back)
            def _():
                by_size(n_used - back, lambda rows: y_copy(n_used - back, rows).wait())


def _expert_plan(counts, nb_max):
    half = MOE_BLOCK // 2
    units = (counts + half - 1) // half
    nfull, tail = units // 2, units % 2
    nblk = nfull + tail
    pend = jnp.cumsum(units * half)
    pstart = pend - units * half
    blk_end = jnp.cumsum(nblk)
    blk0 = blk_end - nblk
    g = jnp.arange(nb_max, dtype=jnp.int32)
    ne = counts.shape[0]
    eg = jnp.minimum(jnp.sum((blk_end[None, :] <= g[:, None]).astype(jnp.int32), axis=1), ne - 1)
    onehot = (eg[:, None] == jnp.arange(ne, dtype=jnp.int32)[None, :]).astype(jnp.int32)
    pick = lambda v: jnp.sum(onehot * v[None, :], axis=1)
    local = g - pick(blk0)
    is_half = ((local == pick(nfull)) & (pick(tail) == 1)).astype(jnp.int32)
    row0 = jnp.clip(pick(pstart) + local * MOE_BLOCK, 0, pend[-1] - half)
    return pstart, blk0, nblk, row0, is_half, blk_end[-1:]


def _experts(blk0, nblk, counts, row0, is_half, n_used, xs, wg, wu, wd):
    p, dp = xs.shape
    ne, d, ff = wg.shape
    grid_spec = pltpu.PrefetchScalarGridSpec(
        num_scalar_prefetch=6,
        grid=(ne,),
        in_specs=[pl.BlockSpec(memory_space=pl.ANY)] * 4,
        out_specs=pl.BlockSpec(memory_space=pl.ANY),
        scratch_shapes=[pltpu.VMEM((EXPERT_SLOTS, MOE_BLOCK, dp), jnp.int32),
                        pltpu.VMEM((EXPERT_SLOTS, MOE_BLOCK, dp), jnp.int32),
                        pltpu.VMEM((WEIGHT_SLOTS, d, ff), F32), pltpu.VMEM((WEIGHT_SLOTS, d, ff), F32),
                        pltpu.VMEM((WEIGHT_SLOTS, ff, d), F32),
                        pltpu.VMEM((d, ff), BF16), pltpu.VMEM((d, ff), BF16), pltpu.VMEM((ff, d), BF16),
                        pltpu.SemaphoreType.DMA((EXPERT_SLOTS,)),
                        pltpu.SemaphoreType.DMA((EXPERT_SLOTS,)),
                        pltpu.SemaphoreType.DMA((3, WEIGHT_SLOTS))],
    )
    return pl.pallas_call(
        _experts_kernel,
        grid_spec=grid_spec,
        out_shape=jax.ShapeDtypeStruct((p, dp), jnp.int32),
        compiler_params=_params("arbitrary"),
        name="experts",
    )(blk0, nblk, counts, row0, is_half, n_used, xs, wg, wu, wd)


def _combine_kernel(base_ref, mod_ref, w_ref, y_ref, o_ref):
    acc = None
    for kk in range(TOP_K):
        term = w_ref[:, kk:kk + 1] * _unpack_bf16_pairs(y_ref[kk])
        acc = term if acc is None else acc + term
    o_ref[0] = base_ref[0] + mod_ref[0, 5:6, :] * acc


def _combine(acc, bi, mod, w_tk, yg, tm):
    b, t, d = acc.shape
    nt = t // tm
    tok = pl.BlockSpec((1, tm, d), lambda i: (bi, i, 0))
    return pl.pallas_call(
        _combine_kernel,
        grid=(nt,),
        in_specs=[tok, pl.BlockSpec((1, 8, d), lambda i: (bi, 0, 0)),
                  pl.BlockSpec((tm, TOP_K), lambda i: (bi * nt + i, 0)),
                  pl.BlockSpec((TOP_K, tm, d // 2), lambda i: (0, i, 0))],
        out_specs=tok,
        out_shape=jax.ShapeDtypeStruct((b, t, d), F32),
        input_output_aliases={0: 0},
        compiler_params=_params("parallel"),
        name="combine",
    )(acc, mod, w_tk, yg)


def _rope_tables(t, tm):
    half = M_DQK // 2
    nf = half // 2
    inv = jnp.asarray(np.power(ROPE_BASE, -np.arange(nf, dtype=np.float32) / nf).astype(np.float32))
    ar = jnp.arange(t // GRID_W, dtype=F32)[:, None] * inv[None, :]
    ac = jnp.arange(GRID_W, dtype=F32)[:, None] * inv[None, :]
    zr, zc = jnp.zeros_like(ar), jnp.zeros_like(ac)
    rcos = jnp.concatenate([jnp.cos(ar), jnp.cos(ar), zr, zr], axis=1)
    rsin = jnp.concatenate([-jnp.sin(ar), jnp.sin(ar), zr, zr], axis=1)
    ccos = jnp.tile(jnp.concatenate([zc, zc, jnp.cos(ac), jnp.cos(ac)], axis=1), (tm // GRID_W, 1))
    csin = jnp.tile(jnp.concatenate([zc, zc, -jnp.sin(ac), jnp.sin(ac)], axis=1), (tm // GRID_W, 1))
    return rcos, rsin, ccos, csin


_IN_SIZES = (512, 512, 1024, 1024, 16, 512, 512, 512, 1024, 1024)
_IN_OFFS = tuple(int(v) for v in np.concatenate([[0], np.cumsum(_IN_SIZES)]))


def _arrange_kernel(w_ref, wa_ref, wt_ref):
    seg = lambda i: w_ref[:, _IN_OFFS[i]:_IN_OFFS[i + 1]]
    mq, mk, mv, mo, _, nq, nk, nv, gm, gn = [seg(i) for i in range(10)]
    g0 = _IN_OFFS[4]
    c = w_ref[:, g0:g0 + LANES]
    lane = lax.broadcasted_iota(jnp.int32, c.shape, 1)
    left4, left8 = pltpu.roll(c, LANES - 4, 1), pltpu.roll(c, LANES - 8, 1)
    gi = jnp.where(lane < 4, c, jnp.where(lane < 8, left4, 0.0))
    gf = jnp.where(lane < 4, left4, jnp.where(lane < 8, left8, 0.0))
    wa_ref[...] = jnp.concatenate([mq, mv, mo, nq, nk, nv, gm, gn], axis=1).astype(BF16)
    wt_ref[...] = jnp.concatenate([jnp.transpose(mk), jnp.transpose(gi), jnp.transpose(gf)],
                                  axis=0).astype(BF16)


def _arrange_w_in(w_in, b_mgate):
    d = w_in.shape[0]
    tr = 256
    wt_rows = M_HEADS * M_DQK + 2 * LANES
    w_all, wt_all = pl.pallas_call(
        _arrange_kernel,
        grid=(d // tr,),
        in_specs=[pl.BlockSpec((tr, w_in.shape[1]), lambda i: (i, 0))],
        out_specs=[pl.BlockSpec((tr, W_COLS), lambda i: (i, 0)),
                   pl.BlockSpec((wt_rows, tr), lambda i: (0, i))],
        out_shape=[jax.ShapeDtypeStruct((d, W_COLS), BF16),
                   jax.ShapeDtypeStruct((wt_rows, d), BF16)],
        compiler_params=_params("parallel"),
        name="arrange_w_in",
    )(w_in)
    bpad = jnp.zeros((LANES - 2 * M_HEADS,), F32)
    bi = jnp.concatenate([b_mgate[0:4], b_mgate[8:12], bpad])
    bf = jnp.concatenate([b_mgate[4:8], b_mgate[12:16], bpad])
    bgt = jnp.concatenate([bi, bf])[:, None]
    return w_all, wt_all, bgt


def _segment_mats():
    na_w = NA_HEADS * NA_DH
    seg = np.zeros((na_w, LANES), np.float32)
    seg[np.arange(na_w), np.arange(na_w) // NA_DH] = 1.0
    return jnp.asarray(seg, BF16), jnp.asarray(np.concatenate([seg.T, seg.T], axis=0), BF16)


def kernel(x, c, ctx, c_ctx, w_ada, b_ada, w_in, b_mgate, m_norm_w, na_qn_w, na_kn_w, na_rpb,
           w_br_m, w_br_na, w_out, w_router, router_bias, w_exp_gate, w_exp_up, w_exp_down,
           w_sh_gate, w_sh_up, w_sh_down):
    b, t, d = x.shape
    n = b * t
    rows = t // GRID_W
    l = 0

    cc = jnp.concatenate([c, c_ctx[None, :], jnp.zeros((8 - b - 1, d), F32)], axis=0)
    mod = _ada(cc, w_ada[l], b_ada[l])
    mod = mod.reshape(8, 6, d)
    mod = jnp.concatenate([mod, jnp.zeros((8, 2, d), F32)], axis=1)
    mod_x = mod[:b]
    mod_c = jnp.broadcast_to(mod[b:b + 1], (b, 8, d))

    w_all, wt_all, bgt = _arrange_w_in(w_in[l], b_mgate[l])
    seg, segt = _segment_mats()
    qnw = jnp.tile(na_qn_w[l], NA_HEADS)[None, :]
    knw = jnp.tile(na_kn_w[l], NA_HEADS)[None, :]
    tm = min(512, t)

    cp = _inproj(ctx, mod_c, w_all, wt_all, bgt, qnw, knw, seg, segt, None, min(tm, ctx.shape[1]))
    xp = _inproj(x, mod_x, w_all, wt_all, bgt, qnw, knw, seg, segt, _rope_tables(t, tm), tm)
    cmq, cmv, _, _, cnk, cnv, _, _, cmkt, cgit, cgft = cp
    mq, mv, mo, nq, nk, nv, gm, gn, mkt, git, gft = xp

    c0 = jnp.zeros((b, 8, M_DQK, MLSTM_EXT), F32)
    m0 = jnp.zeros((b, 8, LANES), F32)
    _, _, c1, m1 = _mlstm(cmq, cmkt, cmv, cgit, cgft, c0, m0)
    hf, hb, _, _ = _mlstm(mq, mkt, mv, git, gft, c1, m1)

    yna = _na(nq, nk, nv, cnk, cnv, _na_bias_table(na_rpb[l], rows))

    bias_col = jnp.broadcast_to(router_bias[l][:, None], (N_EXPERTS, LANES))
    h2p, base, top_e, top_w, rank, cnt = _post(
        x, mod_x, hf, hb, mo, yna, gm, gn, m_norm_w[l][None, :],
        w_br_m[l].astype(BF16), w_br_na[l].astype(BF16), w_out[l].astype(BF16),
        w_router[l].T.astype(BF16), w_sh_gate[l].astype(BF16), w_sh_up[l].astype(BF16),
        w_sh_down[l].astype(BF16), bias_col, tm)

    counts = cnt[:, 0].astype(jnp.int32)
    half = MOE_BLOCK // 2
    p_rows = (-(-(n * TOP_K) // half) + N_EXPERTS) * half
    nb_max = -(-(n * TOP_K) // MOE_BLOCK) + N_EXPERTS
    pstart, blk0, nblk, row0, is_half, n_used = _expert_plan(counts, nb_max)

    xs, pos = _dispatch_rows(h2p, top_e, rank, pstart, p_rows)
    ys = _experts(blk0, nblk, counts, row0, is_half, n_used, xs,
                  w_exp_gate[l], w_exp_up[l], w_exp_down[l])
    out = base
    for bi in range(b):
        idx = pos[:, bi * t:(bi + 1) * t].reshape(-1)
        yg = _gather_rows(ys, idx).reshape(TOP_K, t, d // 2)
        out = _combine(out, bi, mod_x, top_w, yg, tm)
    return out
```

```python
import functools

import numpy as np
import jax
import jax.numpy as jnp
from jax import lax
from jax.experimental import pallas as pl
from jax.experimental.pallas import tpu as pltpu
from jax.experimental.pallas import tpu_sc as plsc

F32 = jnp.float32
BF16 = jnp.bfloat16

EPS = 1e-6
GRID_W = 64
M_HEADS, M_DQK, M_DV = 4, 128, 256
ROPE_BASE = 10000.0
NA_HEADS, NA_DH, NA_KH, NA_KW = 8, 64, 8, 16
N_EXPERTS, TOP_K, N_GROUPS, TOPK_GROUPS = 256, 8, 8, 4
ROUTE_SCALE = 2.5

LANES = 128
VMEM_LIMIT = 56 * 1024 * 1024
NEG = -1e30
LOG2E = 1.4426950408889634

MLSTM_CHUNK = 256
NA_ROWS = 4
NA_KEY_ROWS = NA_ROWS + NA_KH - 1
NA_SUB = 8
MOE_BLOCK = 512
EXPERT_SLOTS = 4
WEIGHT_SLOTS = 3

_W_SEGS = (("mq", 512), ("mv", 1024), ("mo", 1024),
           ("nq", 512), ("nk", 512), ("nv", 512), ("gm", 1024), ("gn", 1024))
_W_OFF = {}
_o = 0
for _n, _w in _W_SEGS:
    _W_OFF[_n] = (_o, _w)
    _o += _w
W_COLS = _o


def _dot(a, b):
    return jnp.dot(a, b, preferred_element_type=F32)


def _dot_nt(a, b):
    return lax.dot_general(a, b, (((1,), (1,)), ((), ())), preferred_element_type=F32)


def _sigmoid(x):
    return 1.0 / (1.0 + jnp.exp(-x))


def _pack_bf16_pairs(v):
    w = v.shape[1] // 2
    bits = pltpu.bitcast(v.astype(BF16).astype(F32), jnp.int32)
    return lax.shift_right_logical(bits[:, :w], 16) | bits[:, w:]


def _unpack_bf16_pairs(p):
    lo = pltpu.bitcast(lax.shift_left(p, 16), F32)
    hi = pltpu.bitcast(p & jnp.int32(-65536), F32)
    return jnp.concatenate([lo, hi], axis=1)


def _params(*sem):
    return pltpu.CompilerParams(dimension_semantics=sem, vmem_limit_bytes=VMEM_LIMIT)


def _resident(shape):
    nd = len(shape)
    return pl.BlockSpec(shape, lambda *_: (0,) * nd, pipeline_mode=pl.Buffered(1))


def _ada_kernel(c_ref, w_ref, b_ref, o_ref):
    c = c_ref[...]
    s = c * _sigmoid(c)
    o_ref[...] = _dot(s.astype(BF16), w_ref[...].astype(BF16)) + b_ref[...]


def _ada(cc, w_ada, b_ada):
    d = cc.shape[1]
    n = w_ada.shape[1]
    return pl.pallas_call(
        _ada_kernel,
        grid=(n // d,),
        in_specs=[pl.BlockSpec((8, d), lambda j: (0, 0)),
                  pl.BlockSpec((d, d), lambda j: (0, j)),
                  pl.BlockSpec((1, d), lambda j: (0, j))],
        out_specs=pl.BlockSpec((8, d), lambda j: (0, j)),
        out_shape=jax.ShapeDtypeStruct((8, n), F32),
        compiler_params=_params("arbitrary"),
        name="ada",
    )(cc, w_ada, b_ada.reshape(1, n))


def _rope_rotate(t, cos, sin):
    q = M_DQK // 4
    lane = lax.broadcasted_iota(jnp.int32, t.shape, 1)
    partner = jnp.where((lane & q) == 0, pltpu.roll(t, M_DQK - q, 1), pltpu.roll(t, q, 1))
    return t * cos + partner * sin


def _rope_rotate_t(t, cos, sin):
    q = M_DQK // 4
    partner = jnp.concatenate([t[q:2 * q], t[0:q], t[3 * q:4 * q], t[2 * q:3 * q]], axis=0)
    return t * cos + partner * sin


def _inproj_kernel(*refs, rope):
    if rope:
        (x_ref, mod_ref, w_ref, wt_ref, bgt_ref, qnw_ref, knw_ref, seg_ref, segt_ref,
         rcos_ref, rsin_ref, ccos_ref, csin_ref,
         mq_ref, mv_ref, mo_ref, nq_ref, nk_ref, nv_ref, gm_ref, gn_ref,
         mkt_ref, git_ref, gft_ref) = refs
    else:
        (x_ref, mod_ref, w_ref, wt_ref, bgt_ref, qnw_ref, knw_ref, seg_ref, segt_ref,
         mq_ref, mv_ref, mo_ref, nq_ref, nk_ref, nv_ref, gm_ref, gn_ref,
         mkt_ref, git_ref, gft_ref) = refs
    x = x_ref[0]
    xn = x * lax.rsqrt(jnp.mean(x * x, axis=-1, keepdims=True) + EPS)
    h = xn * (1.0 + mod_ref[0, 1:2, :]) + mod_ref[0, 0:1, :]
    hb = h.astype(BF16)

    def proj(name):
        off, width = _W_OFF[name]
        return _dot(hb, w_ref[:, off:off + width])

    def head_rms(t, w_row, scale):
        ss = _dot((t * t).astype(BF16), seg_ref[...])
        r = lax.rsqrt(ss * (1.0 / NA_DH) + EPS)
        r_hi = r.astype(BF16)
        r_lo = (r - r_hi.astype(F32)).astype(BF16)
        rb = _dot(jnp.concatenate([r_hi, r_lo], axis=1), segt_ref[...])
        return t * rb * w_row * scale

    if rope:
        tm = x.shape[0]
        spread = lambda r: jnp.broadcast_to(r[:, None, :], (tm // GRID_W, GRID_W, LANES)).reshape(tm, LANES)
        cos = spread(rcos_ref[...]) + ccos_ref[...]
        sin = spread(rsin_ref[...]) + csin_ref[...]

    def fin_mq(t):
        t = t * (M_DQK ** -0.5)
        if rope:
            t = jnp.concatenate([_rope_rotate(t[:, i * LANES:(i + 1) * LANES], cos, sin)
                                 for i in range(M_HEADS)], axis=1)
        mq_ref[0] = t.astype(BF16)

    def fin_mkt(t):
        if rope:
            cost, sint = jnp.transpose(cos), jnp.transpose(sin)
            t = jnp.concatenate([_rope_rotate_t(t[i * M_DQK:(i + 1) * M_DQK], cost, sint)
                                 for i in range(M_HEADS)], axis=0)
        mkt_ref[0] = t.astype(BF16)

    def store(ref, fn=lambda t: t):
        def fin(t):
            ref[0] = fn(t).astype(ref.dtype)
        return fin

    qk_w = M_HEADS * M_DQK

    def fin_feature_major(t):
        fin_mkt(t[0:qk_w])
        git_ref[0] = t[qk_w:qk_w + LANES] + bgt_ref[0:LANES, :]
        gft_ref[0] = t[qk_w + LANES:qk_w + 2 * LANES] + bgt_ref[LANES:2 * LANES, :]

    stages = [
        (lambda: proj("mq"), fin_mq),
        (lambda: proj("mv"), store(mv_ref)),
        (lambda: proj("mo"), store(mo_ref, _sigmoid)),
        (lambda: proj("nq"), store(nq_ref, lambda t: head_rms(t, qnw_ref[...], NA_DH ** -0.5 * LOG2E))),
        (lambda: proj("nk"), store(nk_ref, lambda t: head_rms(t, knw_ref[...], 1.0))),
        (lambda: proj("nv"), store(nv_ref)),
        (lambda: proj("gm"), store(gm_ref, _sigmoid)),
        (lambda: proj("gn"), store(gn_ref, _sigmoid)),
        (lambda: _dot_nt(wt_ref[...], hb), fin_feature_major),
    ]
    acc = stages[0][0]()
    for i, (_, finish) in enumerate(stages):
        nxt = stages[i + 1][0]() if i + 1 < len(stages) else None
        finish(acc)
        acc = nxt


def _inproj(x, mod, w_all, wt_all, bgt, qnw, knw, seg, segt, rope_tabs, tm):
    b, t, d = x.shape
    rope = rope_tabs is not None
    tok = lambda w: pl.BlockSpec((1, tm, w), lambda bi, i: (bi, i, 0))
    tok_t = lambda w: pl.BlockSpec((1, w, tm), lambda bi, i: (bi, 0, i))
    in_specs = [tok(d),
                pl.BlockSpec((1, 8, d), lambda bi, i: (bi, 0, 0)),
                _resident(w_all.shape), _resident(wt_all.shape),
                _resident(bgt.shape), _resident(qnw.shape),
                _resident(knw.shape), _resident(seg.shape), _resident(segt.shape)]
    args = [x, mod, w_all, wt_all, bgt, qnw, knw, seg, segt]
    if rope:
        in_specs += [pl.BlockSpec((tm // GRID_W, LANES), lambda bi, i: (i, 0))] * 2
        in_specs += [_resident((tm, LANES))] * 2
        args += list(rope_tabs)
    widths = [("mq", BF16), ("mv", BF16), ("mo", BF16),
              ("nq", BF16), ("nk", BF16), ("nv", BF16), ("gm", BF16), ("gn", BF16)]
    out_specs = [tok(_W_OFF[n][1]) for n, _ in widths]
    out_shape = [jax.ShapeDtypeStruct((b, t, _W_OFF[n][1]), dt) for n, dt in widths]
    out_specs += [tok_t(M_HEADS * M_DQK), tok_t(LANES), tok_t(LANES)]
    out_shape += [jax.ShapeDtypeStruct((b, M_HEADS * M_DQK, t), BF16),
                  jax.ShapeDtypeStruct((b, LANES, t), F32),
                  jax.ShapeDtypeStruct((b, LANES, t), F32)]
    return pl.pallas_call(
        functools.partial(_inproj_kernel, rope=rope),
        grid=(b, t // tm),
        in_specs=in_specs, out_specs=out_specs, out_shape=out_shape,
        compiler_params=_params("parallel", "parallel"),
        name="inproj_rope" if rope else "inproj_ctx",
    )(*args)


def _log_sigmoid(x):
    return jnp.minimum(x, 0.0) - jnp.log(1.0 + jnp.exp(-jnp.abs(x)))


def _dot_split(a, b):
    hi = a.astype(BF16)
    lo = (a - hi.astype(F32)).astype(BF16)
    return _dot(hi, b) + _dot(lo, b)


MLSTM_EXT = M_DV + LANES


def _mlstm_kernel(qf_ref, ktf_ref, vf_ref, gitf_ref, gftf_ref,
                  qb_ref, ktb_ref, vb_ref, gitb_ref, gftb_ref,
                  c0_ref, m0_ref,
                  hf_ref, hb_ref, cn_ref, mn_ref,
                  *scratch):
    c_scrs, m_scr = scratch[:2 * M_HEADS], scratch[2 * M_HEADS]
    step = pl.program_id(1)
    L = qf_ref.shape[1]
    nu = 2 * M_HEADS

    @pl.when(step == 0)
    def _():
        for j, c_scr in enumerate(c_scrs):
            c_scr[...] = c0_ref[0, j]
        m_scr[...] = m0_ref[0]

    row_i = lax.broadcasted_iota(jnp.int32, (L, L), 0)
    col_i = lax.broadcasted_iota(jnp.int32, (L, L), 1)
    lower = col_i <= row_i
    upper = col_i >= row_i
    tri_lo = jnp.where(lower, 1.0, 0.0).astype(BF16)
    tri_up = jnp.where(upper, 1.0, 0.0).astype(BF16)

    is_f = lax.broadcasted_iota(jnp.int32, (nu, L), 0) < M_HEADS
    gi_t = jnp.where(is_f, gitf_ref[0, 0:nu, :], gitb_ref[0, 0:nu, :]) * LOG2E
    ls_tf = _log_sigmoid(gftf_ref[0, 0:nu, :]) * LOG2E
    ls_tb = _log_sigmoid(gftb_ref[0, 0:nu, :]) * LOG2E
    b_t = jnp.where(is_f, _dot_split(ls_tf, tri_up), _dot_split(ls_tb, tri_lo))
    u_t = gi_t - b_t
    g_c = jnp.sum(jnp.where(is_f, ls_tf, ls_tb), axis=1, keepdims=True)
    m_prev = m_scr[...]
    a_t = g_c + u_t
    m_new = jnp.maximum(g_c + m_prev, jnp.max(a_t, axis=1, keepdims=True))
    decay = jnp.exp2(g_c + m_prev - m_new)
    wa_t = jnp.exp2(a_t - jnp.concatenate([m_new] * (L // LANES), axis=1))
    m_scr[...] = m_new

    ones = jnp.ones((L, LANES), BF16)
    dirs = ((qf_ref, ktf_ref, vf_ref, hf_ref, lower), (qb_ref, ktb_ref, vb_ref, hb_ref, upper))
    b_cols = jnp.transpose(b_t)

    def head(j):
        d, hd = divmod(j, M_HEADS)
        q_ref, kt_ref, v_ref, _, mask = dirs[d]
        q = q_ref[0, :, hd * M_DQK:(hd + 1) * M_DQK]
        k_t = kt_ref[0, hd * M_DQK:(hd + 1) * M_DQK, :]
        v_ext = jnp.concatenate([v_ref[0, :, hd * M_DV:(hd + 1) * M_DV], ones], axis=1)
        c_prev = c_scrs[j][...]
        u_row = u_t[j:j + 1, :]
        s_raw = _dot(q, k_t)
        m_loc = jnp.max(jnp.where(mask, u_row, NEG), axis=1, keepdims=True)
        kw = (k_t.astype(F32) * wa_t[j:j + 1, :]).astype(BF16)
        dec = jnp.concatenate([decay[j:j + 1, :]] * (MLSTM_EXT // LANES), axis=1)
        c_scrs[j][...] = dec * c_prev + _dot(kw, v_ext)
        return q, v_ext, c_prev.astype(BF16), u_row, m_loc, s_raw

    def tail(j, q, v_ext, c_prev, u_row, m_loc, s_raw):
        d, hd = divmod(j, M_HEADS)
        h_ref, mask = dirs[d][3], dirs[d][4]
        mp_row = m_prev[j:j + 1, :]
        m_rep = jnp.maximum(jnp.broadcast_to(m_loc, (L, LANES)), mp_row)
        m_wide = jnp.concatenate([m_rep] * (L // LANES), axis=1)
        s = (s_raw * jnp.exp2(jnp.where(mask, u_row - m_wide, NEG))).astype(BF16)
        qw = (q.astype(F32) * jnp.exp2(mp_row - m_rep)).astype(BF16)
        r = _dot(s, v_ext) + _dot(qw, c_prev)
        b_rep = jnp.broadcast_to(b_cols[:, j:j + 1], (L, LANES))
        dn = jnp.maximum(jnp.abs(r[:, M_DV:]), jnp.exp2(-(b_rep + m_rep)))
        h_ref[0, :, hd * M_DV:(hd + 1) * M_DV] = (
            r[:, :M_DV] / jnp.concatenate([dn] * (M_DV // LANES), axis=1)).astype(h_ref.dtype)

    nxt = head(0)
    for j in range(nu):
        cur = nxt
        if j + 1 < nu:
            nxt = head(j + 1)
        tail(j, *cur)

    @pl.when(step == pl.num_programs(1) - 1)
    def _():
        for j, c_scr in enumerate(c_scrs):
            cn_ref[0, j] = c_scr[...]
        mn_ref[0] = m_scr[...]


def _mlstm(q, kt, v, git, gft, c0, m0):
    b, t, _ = q.shape
    L = min(MLSTM_CHUNK, t)
    nc = t // L
    fwd = lambda w: pl.BlockSpec((1, L, w), lambda bi, i: (bi, i, 0))
    bwd = lambda w: pl.BlockSpec((1, L, w), lambda bi, i: (bi, nc - 1 - i, 0))
    fwd_t = lambda w: pl.BlockSpec((1, w, L), lambda bi, i: (bi, 0, i))
    bwd_t = lambda w: pl.BlockSpec((1, w, L), lambda bi, i: (bi, 0, nc - 1 - i))
    st_c = pl.BlockSpec((1, 8, M_DQK, MLSTM_EXT), lambda bi, i: (bi, 0, 0, 0))
    st_v = pl.BlockSpec((1, 8, LANES), lambda bi, i: (bi, 0, 0))
    qk_w, v_w = M_HEADS * M_DQK, M_HEADS * M_DV
    return pl.pallas_call(
        _mlstm_kernel,
        grid=(b, nc),
        in_specs=[fwd(qk_w), fwd_t(qk_w), fwd(v_w), fwd_t(LANES), fwd_t(LANES),
                  bwd(qk_w), bwd_t(qk_w), bwd(v_w), bwd_t(LANES), bwd_t(LANES),
                  st_c, st_v],
        out_specs=[fwd(v_w), bwd(v_w), st_c, st_v],
        out_shape=[jax.ShapeDtypeStruct((b, t, v_w), BF16),
                   jax.ShapeDtypeStruct((b, t, v_w), BF16),
                   jax.ShapeDtypeStruct(c0.shape, F32),
                   jax.ShapeDtypeStruct(m0.shape, F32)],
        scratch_shapes=([pltpu.VMEM((M_DQK, MLSTM_EXT), F32) for _ in range(2 * M_HEADS)]
                        + [pltpu.VMEM((8, LANES), F32)]),
        compiler_params=_params("parallel", "arbitrary"),
        name="mlstm",
    )(q, kt, v, git, gft, q, kt, v, git, gft, c0, m0)


def _na_kernel(q_ref, k_ref, v_ref, kc_ref, vc_ref, bias_ref, o_ref, *, rows, nsub):
    tq = NA_ROWS * GRID_W
    nkeys = NA_KEY_ROWS * GRID_W
    last_rb = rows // NA_ROWS - 1
    kc = kc_ref[0]
    vc = vc_ref[0]
    lane = lax.broadcasted_iota(jnp.int32, (tq, LANES), 1)

    def logits(sb, hh):
        rb = pl.program_id(2) * nsub + sb
        kind = jnp.where(rb == 0, 0, jnp.where(rb == last_rb, 2, 1))
        ks = jnp.clip(rb * NA_ROWS - NA_KH // 2, 0, rows - NA_KEY_ROWS)
        kstart = pl.multiple_of(ks * GRID_W, GRID_W)
        kblk = k_ref[0, pl.ds(kstart, nkeys), :]
        q = q_ref[0, sb * tq:(sb + 1) * tq, :]
        in_head = (lane < NA_DH) if hh == 0 else (lane >= NA_DH)
        qm = jnp.where(in_head, q, jnp.zeros_like(q))
        return _dot_nt(qm, kblk) + bias_ref[hh, kind], _dot_nt(qm, kc), kstart

    chains = [(sb, hh) for sb in range(nsub) for hh in range(2)]
    nxt = logits(*chains[0])
    outs = []
    for i, (sb, hh) in enumerate(chains):
        sw, sc, kstart = nxt
        if i + 1 < len(chains):
            nxt = logits(*chains[i + 1])
        m = jnp.maximum(jnp.max(sw, axis=1, keepdims=True), jnp.max(sc, axis=1, keepdims=True))
        ew = jnp.exp2(sw - m)
        ec = jnp.exp2(sc - m)
        l = jnp.sum(ew, axis=1, keepdims=True) + jnp.sum(ec, axis=1, keepdims=True)
        vblk = v_ref[0, pl.ds(kstart, nkeys), :]
        o = _dot(ew.astype(BF16), vblk) + _dot(ec.astype(BF16), vc)
        outs.append(o / l)
        if hh == 1:
            o_ref[0, sb * tq:(sb + 1) * tq, :] = jnp.where(lane < NA_DH, outs[0], outs[1]).astype(o_ref.dtype)
            outs = []


def _na(nq, nk, nv, cnk, cnv, bias):
    b, t, w = nq.shape
    rows = t // GRID_W
    tq = NA_ROWS * GRID_W
    nrb = rows // NA_ROWS
    nsub = min(NA_SUB, nrb)
    nctx = cnk.shape[1]
    return pl.pallas_call(
        functools.partial(_na_kernel, rows=rows, nsub=nsub),
        grid=(b, w // LANES, nrb // nsub),
        in_specs=[pl.BlockSpec((1, nsub * tq, LANES), lambda bi, hp, st: (bi, st, hp)),
                  pl.BlockSpec((1, t, LANES), lambda bi, hp, st: (bi, 0, hp)),
                  pl.BlockSpec((1, t, LANES), lambda bi, hp, st: (bi, 0, hp)),
                  pl.BlockSpec((1, nctx, LANES), lambda bi, hp, st: (bi, 0, hp)),
                  pl.BlockSpec((1, nctx, LANES), lambda bi, hp, st: (bi, 0, hp)),
                  pl.BlockSpec((2,) + bias.shape[1:], lambda bi, hp, st: (hp, 0, 0, 0))],
        out_specs=pl.BlockSpec((1, nsub * tq, LANES), lambda bi, hp, st: (bi, st, hp)),
        out_shape=jax.ShapeDtypeStruct((b, t, w), BF16),
        compiler_params=_params("parallel", "parallel", "arbitrary"),
        name="na",
    )(nq, nk, nv, cnk, cnv, bias)


def _na_bias_table(na_rpb, rows):
    h = na_rpb.shape[0]
    w = GRID_W
    c = np.arange(w)[:, None]
    kj = np.arange(w)[None, :]
    cs = np.clip(c - NA_KW // 2, 0, w - NA_KW)
    col_valid = (kj >= cs) & (kj < cs + NA_KW)
    dc = np.clip(kj - c + (NA_KW - 1), 0, 2 * NA_KW - 2)
    onehot = np.zeros((2 * NA_KW - 1, w, w), np.float32)
    onehot[dc, np.arange(w)[:, None], np.arange(w)[None, :]] = 1.0
    t2 = jnp.einsum("hrd,dck->hrck", na_rpb, jnp.asarray(onehot), precision=lax.Precision.HIGHEST)
    t2 = jnp.where(jnp.asarray(col_valid)[None, None], t2 * LOG2E, NEG)
    t2 = jnp.concatenate([t2, jnp.full((h, 1, w, w), NEG, F32)], axis=1)
    invalid = 2 * NA_KH - 1
    dr_idx = np.full((3, NA_ROWS, NA_KEY_ROWS), invalid, np.int32)
    for kind, r0 in enumerate((0, NA_ROWS, rows - NA_ROWS)):
        ks = int(np.clip(r0 - NA_KH // 2, 0, rows - NA_KEY_ROWS))
        for qa in range(NA_ROWS):
            r = r0 + qa
            rs = int(np.clip(r - NA_KH // 2, 0, rows - NA_KH))
            for kl in range(NA_KEY_ROWS):
                ki = ks + kl
                if rs <= ki < rs + NA_KH:
                    dr_idx[kind, qa, kl] = ki - r + NA_KH - 1
    t2t = t2.transpose(0, 2, 1, 3)
    strips = [jnp.concatenate([t2t[:, :, int(dr), :] for dr in dr_idx[kind, qa]], axis=-1)
              for kind in range(3) for qa in range(NA_ROWS)]
    return jnp.stack(strips, axis=1).reshape(h, 3, NA_ROWS * w, NA_KEY_ROWS * w)


def _interleave(*streams):
    live = list(streams)
    while live:
        for g in list(live):
            try:
                next(g)
            except StopIteration:
                live.remove(g)


def _post_kernel(x_ref, mod_ref, hf_ref, hb_ref, mo_ref, na_ref, gm_ref, gn_ref,
                 mnw_ref, wbm_ref, wbn_ref, wout_ref, wr_ref, wsg_ref, wsu_ref, wsd_ref, rb_ref,
                 h2_ref, base_ref, e_ref, w_ref, r_ref, cnt_ref, run_scr):
    @pl.when(pl.program_id(0) == 0)
    def _():
        run_scr[...] = jnp.zeros_like(run_scr)

    hm = hf_ref[0].astype(F32) + hb_ref[0].astype(F32)
    parts = []
    for hd in range(M_HEADS):
        t = hm[:, hd * M_DV:(hd + 1) * M_DV]
        parts.append(t * lax.rsqrt(jnp.mean(t * t, axis=-1, keepdims=True) + EPS))
    y_m = jnp.concatenate(parts, axis=1) * mnw_ref[...] * mo_ref[0].astype(F32)
    a = _dot(y_m.astype(BF16), wbm_ref[...])
    bn = _dot(na_ref[0], wbn_ref[...])
    z = gm_ref[0].astype(F32) * a + gn_ref[0].astype(F32) * bn
    y = _dot(z.astype(BF16), wout_ref[...])
    x1 = x_ref[0] + mod_ref[0, 2:3, :] * y
    xn = x1 * lax.rsqrt(jnp.mean(x1 * x1, axis=-1, keepdims=True) + EPS)
    h2f = xn * (1.0 + mod_ref[0, 4:5, :]) + mod_ref[0, 3:4, :]
    h2_ref[...] = _pack_bf16_pairs(h2f)
    h2 = h2f.astype(BF16)
    scores = _sigmoid(_dot_nt(wr_ref[...], h2))

    def shared_expert():
        sg = _dot(h2, wsg_ref[...])
        su = _dot(h2, wsu_ref[...])
        yield
        sh = sg * _sigmoid(sg) * su
        base_ref[0] = x1 + mod_ref[0, 5:6, :] * _dot(sh.astype(BF16), wsd_ref[...])

    _interleave(_route_stages(scores, rb_ref, e_ref, w_ref, r_ref, cnt_ref, run_scr), shared_expert())


def _post(x, mod, hf, hb, mo, yna, gm, gn, mnw, wbm, wbn, wout, wr_t, wsg, wsu, wsd, rbias, tm):
    b, t, d = x.shape
    nt = t // tm
    n = b * t
    tok = lambda w: pl.BlockSpec((1, tm, w), lambda s: (s // nt, s % nt, 0))
    rt = lambda: pl.BlockSpec((TOP_K, tm), lambda s: (0, s))
    res = [mnw, wbm, wbn, wout, wr_t, wsg, wsu, wsd, rbias]
    return pl.pallas_call(
        _post_kernel,
        grid=(b * nt,),
        in_specs=[tok(d), pl.BlockSpec((1, 8, d), lambda s: (s // nt, 0, 0)),
                  tok(hf.shape[2]), tok(hb.shape[2]), tok(mo.shape[2]), tok(yna.shape[2]),
                  tok(gm.shape[2]), tok(gn.shape[2])] + [_resident(a.shape) for a in res],
        out_specs=[pl.BlockSpec((tm, d // 2), lambda s: (s, 0)),
                   tok(d), rt(), pl.BlockSpec((tm, TOP_K), lambda s: (s, 0)), rt(),
                   pl.BlockSpec((N_EXPERTS, LANES), lambda s: (0, 0))],
        out_shape=[jax.ShapeDtypeStruct((n, d // 2), jnp.int32),
                   jax.ShapeDtypeStruct((b, t, d), F32),
                   jax.ShapeDtypeStruct((TOP_K, n), jnp.int32),
                   jax.ShapeDtypeStruct((n, TOP_K), F32),
                   jax.ShapeDtypeStruct((TOP_K, n), jnp.int32),
                   jax.ShapeDtypeStruct((N_EXPERTS, LANES), F32)],
        scratch_shapes=[pltpu.VMEM((N_EXPERTS, LANES), F32)],
        compiler_params=_params("arbitrary"),
        name="post",
    )(x, mod, hf, hb, mo, yna, gm, gn, *res)


def _route_stages(s, b_ref, e_ref, w_ref, r_ref, cnt_ref, run_scr):
    tm = s.shape[1]
    sel = s + b_ref[...][:, 0:1]
    gsz = N_EXPERTS // N_GROUPS
    ninf = -jnp.inf

    x3 = sel.reshape(N_GROUPS, gsz, tm)
    r3 = lax.broadcasted_iota(jnp.int32, x3.shape, 1)
    m1 = jnp.max(x3, axis=1, keepdims=True)
    i1 = jnp.min(jnp.where(x3 == m1, r3, gsz), axis=1, keepdims=True)
    m2 = jnp.max(jnp.where(r3 == i1, ninf, x3), axis=1)
    gs = m1[:, 0, :] + m2

    gidx = lax.broadcasted_iota(jnp.int32, gs.shape, 0)
    gkeep = jnp.zeros(gs.shape, jnp.bool_)
    cur = gs
    for _ in range(TOPK_GROUPS):
        mm = jnp.max(cur, axis=0, keepdims=True)
        ii = jnp.min(jnp.where(cur == mm, gidx, N_GROUPS), axis=0, keepdims=True)
        hit = gidx == ii
        gkeep = jnp.logical_or(gkeep, hit)
        cur = jnp.where(hit, ninf, cur)
    keep = jnp.broadcast_to(gkeep[:, None, :], x3.shape).reshape(N_EXPERTS, tm)
    yield

    row = lax.broadcasted_iota(jnp.int32, s.shape, 0).astype(F32)
    cur = jnp.where(keep, sel, ninf)
    idxs, ws = [], []
    chosen_f = jnp.zeros(s.shape, F32)
    for kk in range(TOP_K):
        mm = jnp.max(cur, axis=0, keepdims=True)
        ii = jnp.min(jnp.where(cur == mm, row, float(N_EXPERTS)), axis=0, keepdims=True)
        hit = row == ii
        idxs.append(ii)
        ws.append(jnp.sum(jnp.where(hit, s, 0.0), axis=0, keepdims=True))
        chosen_f = jnp.where(hit, 1.0, chosen_f)
        cur = jnp.where(hit, ninf, cur)
        if kk == TOP_K // 2 - 1:
            yield
    wsum = ws[0]
    for wk in ws[1:]:
        wsum = wsum + wk

    tp =lax.broadcasted_iota(jnp.int32, (tm, tm), 0)
    tc = lax.broadcasted_iota(jnp.int32, (tm, tm), 1)
    before = jnp.where(tp < tc, 1.0, 0.0).astype(BF16)
    rank = _dot(chosen_f.astype(BF16), before) + run_scr[...][:, 0:1]
    run_scr[...] = run_scr[...] + jnp.sum(chosen_f, axis=1, keepdims=True)
    cnt_ref[...] = run_scr[...]

    for kk in range(TOP_K):
        e_ref[kk:kk + 1, :] = idxs[kk].astype(jnp.int32)
        r_ref[kk:kk + 1, :] = jnp.sum(jnp.where(row == idxs[kk], rank, 0.0), axis=0,
                                      keepdims=True).astype(jnp.int32)
    w_ref[...] = jnp.transpose(jnp.concatenate([wk / wsum * ROUTE_SCALE for wk in ws], axis=0))


SC_WINDOW = 128


def _sc_mesh():
    return plsc.VectorSubcoreMesh(core_axis_name="core", subcore_axis_name="subcore")


def _sc_workers():
    info = plsc.get_sparse_core_info()
    return info.num_cores, info.num_cores * info.num_subcores


def _dispatch_rows(x, top_e, rank, pstart, p_rows):
    n, w = x.shape
    kk = top_e.shape[0]
    ncores, nw = _sc_workers()
    lanes = plsc.get_sparse_core_info().num_lanes
    steps = n // nw // SC_WINDOW
    per_worker = lambda a: a.reshape(kk, nw, steps, SC_WINDOW).transpose(1, 2, 0, 3)

    @functools.partial(
        pl.kernel, mesh=_sc_mesh(),
        out_type=[jax.ShapeDtypeStruct((p_rows, w), x.dtype),
                  jax.ShapeDtypeStruct((nw, steps, kk, SC_WINDOW), jnp.int32)],
        scratch_types=[pltpu.VMEM((kk, SC_WINDOW), jnp.int32),
                       pltpu.VMEM((kk, SC_WINDOW), jnp.int32),
                       pltpu.VMEM((kk, SC_WINDOW), jnp.int32),
                       pltpu.VMEM(pstart.shape, jnp.int32),
                       pltpu.VMEM((SC_WINDOW, w), x.dtype),
                       pltpu.SemaphoreType.DMA],
        compiler_params=pltpu.CompilerParams(needs_layout_passes=False),
    )
    def scatter(x_hbm, e_hbm, r_hbm, ps_hbm, o_hbm, pos_hbm, e_v, r_v, pos_v, ps_v, rows_v, sem):
        wid = lax.axis_index("subcore") * ncores + lax.axis_index("core")
        pltpu.sync_copy(ps_hbm, ps_v)

        @pl.loop(0, steps)
        def _(s):
            base = pl.multiple_of((wid * steps + s) * SC_WINDOW, SC_WINDOW)
            pltpu.sync_copy(e_hbm.at[wid, s], e_v)
            pltpu.sync_copy(r_hbm.at[wid, s], r_v)
            pltpu.sync_copy(x_hbm.at[pl.ds(base, SC_WINDOW)], rows_v)
            for j in range(kk):
                for c in range(SC_WINDOW // lanes):
                    cols = pl.ds(c * lanes, lanes)
                    pos_v[j, cols] = plsc.load_gather(ps_v, [e_v[j, cols]]) + r_v[j, cols]
            pltpu.sync_copy(pos_v, pos_hbm.at[wid, s])
            copies = [pltpu.make_async_copy(rows_v, o_hbm.at[pos_v.at[j]], sem) for j in range(kk)]
            for cp in copies:
                cp.start()
            for cp in copies:
                cp.wait()

    out, pos4 = scatter(x, per_worker(top_e), per_worker(rank), pstart)
    return out, pos4.transpose(2, 0, 1, 3).reshape(kk, n)


def _gather_rows(x, idx):
    m = idx.shape[0]
    w = x.shape[1]
    ncores, nw = _sc_workers()
    steps = m // nw // SC_WINDOW
    idx3 = idx.reshape(nw, steps, SC_WINDOW)

    @functools.partial(
        pl.kernel, mesh=_sc_mesh(),
        out_type=jax.ShapeDtypeStruct((m, w), x.dtype),
        scratch_types=[pltpu.VMEM((steps, SC_WINDOW), jnp.int32),
                       pltpu.VMEM((SC_WINDOW, w), x.dtype),
                       pltpu.SemaphoreType.DMA],
    )
    def gather(x_hbm, i_hbm, o_hbm, idx_v, rows_v, sem):
        wid = lax.axis_index("subcore") * ncores + lax.axis_index("core")
        pltpu.sync_copy(i_hbm.at[wid], idx_v)

        @pl.loop(0, steps)
        def _(s):
            pltpu.async_copy(x_hbm.at[idx_v.at[s]], rows_v, sem).wait()
            base = pl.multiple_of((wid * steps + s) * SC_WINDOW, SC_WINDOW)
            pltpu.sync_copy(rows_v, o_hbm.at[pl.ds(base, SC_WINDOW)])

    return gather(x, idx3)


def _experts_kernel(blk0_ref, nblk_ref, cnt_ref, row0_ref, half_ref, nu_ref,
                    x_hbm, wg_hbm, wu_hbm, wd_hbm, y_hbm,
                    xbuf, ybuf, wg_raw, wu_raw, wd_raw, wg_scr, wu_scr, wd_scr, in_sem, out_sem, w_sem):
    e = pl.program_id(0)
    ne = pl.num_programs(0)
    n_used = nu_ref[0]
    blk0 = blk0_ref[e]
    ns = EXPERT_SLOTS
    sizes = (MOE_BLOCK, MOE_BLOCK // 2)

    def w_copies(ex):
        slot = ex % WEIGHT_SLOTS
        return [pltpu.make_async_copy(hbm.at[ex], raw.at[slot], w_sem.at[i, slot])
                for i, (hbm, raw) in enumerate(((wg_hbm, wg_raw), (wu_hbm, wu_raw), (wd_hbm, wd_raw)))]

    @pl.when(e == 0)
    def _():
        for e0 in range(WEIGHT_SLOTS - 1):
            @pl.when(e0 < ne)
            def _():
                for cp in w_copies(e0):
                    cp.start(priority=1)

    @pl.when(e + WEIGHT_SLOTS - 1 < ne)
    def _():
        for cp in w_copies(e + WEIGHT_SLOTS - 1):
            cp.start(priority=1)

    def x_copy(g, rows):
        r0 = pl.multiple_of(row0_ref[g], MOE_BLOCK // 2)
        return pltpu.make_async_copy(x_hbm.at[pl.ds(r0, rows)], xbuf.at[g % ns, pl.ds(0, rows)],
                                     in_sem.at[g % ns])

    def y_copy(g, rows):
        r0 = pl.multiple_of(row0_ref[g], MOE_BLOCK // 2)
        return pltpu.make_async_copy(ybuf.at[g % ns, pl.ds(0, rows)], y_hbm.at[pl.ds(r0, rows)],
                                     out_sem.at[g % ns])

    def by_size(g, fn):
        for is_half, rows in enumerate(sizes):
            @pl.when(half_ref[g] == is_half)
            def _():
                fn(rows)

    @pl.when(e == 0)
    def _():
        for g0 in range(ns - 1):
            @pl.when(g0 < n_used)
            def _():
                by_size(g0, lambda rows: x_copy(g0, rows).start())

    for cp in w_copies(e):
        cp.wait()
    wslot = e % WEIGHT_SLOTS
    wg_scr[...] = wg_raw[wslot].astype(BF16)
    wu_scr[...] = wu_raw[wslot].astype(BF16)
    wd_scr[...] = wd_raw[wslot].astype(BF16)

    def block(b, carry):
        g = blk0 + b
        by_size(g, lambda rows: x_copy(g, rows).wait())

        @pl.when(g + ns - 1 < n_used)
        def _():
            by_size(g + ns - 1, lambda rows: x_copy(g + ns - 1, rows).start())

        @pl.when(g >= ns)
        def _():
            by_size(g - ns, lambda rows: y_copy(g - ns, rows).wait())

        def run(rows):
            rid = lax.broadcasted_iota(jnp.int32, (rows, xbuf.shape[2]), 0)
            xp = jnp.where(rid < cnt_ref[e] - b * MOE_BLOCK, xbuf[g % ns, 0:rows], 0)
            x = _unpack_bf16_pairs(xp).astype(BF16)
            gt = _dot(x, wg_scr[...])
            up = _dot(x, wu_scr[...])
            a = (gt * _sigmoid(gt) * up).astype(BF16)
            ybuf[g % ns, 0:rows] = _pack_bf16_pairs(_dot(a, wd_scr[...]))
            y_copy(g, rows).start()

        by_size(g, run)
        return carry

    lax.fori_loop(0, nblk_ref[e], block, 0)

    @pl.when(e == pl.num_programs(0) - 1)
    def _():
        for back in range(ns, 0, -1):
            @pl.when(n_used >= back)
            def _():
                by_size(n_used - back, lambda rows: y_copy(n_used - back, rows).wait())


def _expert_plan(counts, nb_max):
    half = MOE_BLOCK // 2
    units = (counts + half - 1) // half
    nfull, tail = units // 2, units % 2
    nblk = nfull + tail
    pend = jnp.cumsum(units * half)
    pstart = pend - units * half
    blk_end = jnp.cumsum(nblk)
    blk0 = blk_end - nblk
    g = jnp.arange(nb_max, dtype=jnp.int32)
    ne = counts.shape[0]
    eg = jnp.minimum(jnp.sum((blk_end[None, :] <= g[:, None]).astype(jnp.int32), axis=1), ne - 1)
    onehot = (eg[:, None] == jnp.arange(ne, dtype=jnp.int32)[None, :]).astype(jnp.int32)
    pick = lambda v: jnp.sum(onehot * v[None, :], axis=1)
    local = g - pick(blk0)
    is_half = ((local == pick(nfull)) & (pick(tail) == 1)).astype(jnp.int32)
    row0 = jnp.clip(pick(pstart) + local * MOE_BLOCK, 0, pend[-1] - half)
    return pstart, blk0, nblk, row0, is_half, blk_end[-1:]


def _experts(blk0, nblk, counts, row0, is_half, n_used, xs, wg, wu, wd):
    p, dp = xs.shape
    ne, d, ff = wg.shape
    grid_spec = pltpu.PrefetchScalarGridSpec(
        num_scalar_prefetch=6,
        grid=(ne,),
        in_specs=[pl.BlockSpec(memory_space=pl.ANY)] * 4,
        out_specs=pl.BlockSpec(memory_space=pl.ANY),
        scratch_shapes=[pltpu.VMEM((EXPERT_SLOTS, MOE_BLOCK, dp), jnp.int32),
                        pltpu.VMEM((EXPERT_SLOTS, MOE_BLOCK, dp), jnp.int32),
                        pltpu.VMEM((WEIGHT_SLOTS, d, ff), F32), pltpu.VMEM((WEIGHT_SLOTS, d, ff), F32),
                        pltpu.VMEM((WEIGHT_SLOTS, ff, d), F32),
                        pltpu.VMEM((d, ff), BF16), pltpu.VMEM((d, ff), BF16), pltpu.VMEM((ff, d), BF16),
                        pltpu.SemaphoreType.DMA((EXPERT_SLOTS,)),
                        pltpu.SemaphoreType.DMA((EXPERT_SLOTS,)),
                        pltpu.SemaphoreType.DMA((3, WEIGHT_SLOTS))],
    )
    return pl.pallas_call(
        _experts_kernel,
        grid_spec=grid_spec,
        out_shape=jax.ShapeDtypeStruct((p, dp), jnp.int32),
        compiler_params=_params("arbitrary"),
        name="experts",
    )(blk0, nblk, counts, row0, is_half, n_used, xs, wg, wu, wd)


def _combine_kernel(base_ref, mod_ref, w_ref, y_ref, o_ref):
    acc = None
    for kk in range(TOP_K):
        term = w_ref[:, kk:kk + 1] * _unpack_bf16_pairs(y_ref[kk])
        acc = term if acc is None else acc + term
    o_ref[0] = base_ref[0] + mod_ref[0, 5:6, :] * acc


def _combine(acc, bi, mod, w_tk, yg, tm):
    b, t, d = acc.shape
    nt = t // tm
    tok = pl.BlockSpec((1, tm, d), lambda i: (bi, i, 0))
    return pl.pallas_call(
        _combine_kernel,
        grid=(nt,),
        in_specs=[tok, pl.BlockSpec((1, 8, d), lambda i: (bi, 0, 0)),
                  pl.BlockSpec((tm, TOP_K), lambda i: (bi * nt + i, 0)),
                  pl.BlockSpec((TOP_K, tm, d // 2), lambda i: (0, i, 0))],
        out_specs=tok,
        out_shape=jax.ShapeDtypeStruct((b, t, d), F32),
        input_output_aliases={0: 0},
        compiler_params=_params("parallel"),
        name="combine",
    )(acc, mod, w_tk, yg)


def _rope_tables(t, tm):
    half = M_DQK // 2
    nf = half // 2
    inv = jnp.asarray(np.power(ROPE_BASE, -np.arange(nf, dtype=np.float32) / nf).astype(np.float32))
    ar = jnp.arange(t // GRID_W, dtype=F32)[:, None] * inv[None, :]
    ac = jnp.arange(GRID_W, dtype=F32)[:, None] * inv[None, :]
    zr, zc = jnp.zeros_like(ar), jnp.zeros_like(ac)
    rcos = jnp.concatenate([jnp.cos(ar), jnp.cos(ar), zr, zr], axis=1)
    rsin = jnp.concatenate([-jnp.sin(ar), jnp.sin(ar), zr, zr], axis=1)
    ccos = jnp.tile(jnp.concatenate([zc, zc, jnp.cos(ac), jnp.cos(ac)], axis=1), (tm // GRID_W, 1))
    csin = jnp.tile(jnp.concatenate([zc, zc, -jnp.sin(ac), jnp.sin(ac)], axis=1), (tm // GRID_W, 1))
    return rcos, rsin, ccos, csin


_IN_SIZES = (512, 512, 1024, 1024, 16, 512, 512, 512, 1024, 1024)
_IN_OFFS = tuple(int(v) for v in np.concatenate([[0], np.cumsum(_IN_SIZES)]))


def _arrange_kernel(w_ref, wa_ref, wt_ref):
    seg = lambda i: w_ref[:, _IN_OFFS[i]:_IN_OFFS[i + 1]]
    mq, mk, mv, mo, _, nq, nk, nv, gm, gn = [seg(i) for i in range(10)]
    g0 = _IN_OFFS[4]
    c = w_ref[:, g0:g0 + LANES]
    lane = lax.broadcasted_iota(jnp.int32, c.shape, 1)
    left4, left8 = pltpu.roll(c, LANES - 4, 1), pltpu.roll(c, LANES - 8, 1)
    gi = jnp.where(lane < 4, c, jnp.where(lane < 8, left4, 0.0))
    gf = jnp.where(lane < 4, left4, jnp.where(lane < 8, left8, 0.0))
    wa_ref[...] = jnp.concatenate([mq, mv, mo, nq, nk, nv, gm, gn], axis=1).astype(BF16)
    wt_ref[...] = jnp.concatenate([jnp.transpose(mk), jnp.transpose(gi), jnp.transpose(gf)],
                                  axis=0).astype(BF16)


def _arrange_w_in(w_in, b_mgate):
    d = w_in.shape[0]
    tr = 256
    wt_rows = M_HEADS * M_DQK + 2 * LANES
    w_all, wt_all = pl.pallas_call(
        _arrange_kernel,
        grid=(d // tr,),
        in_specs=[pl.BlockSpec((tr, w_in.shape[1]), lambda i: (i, 0))],
        out_specs=[pl.BlockSpec((tr, W_COLS), lambda i: (i, 0)),
                   pl.BlockSpec((wt_rows, tr), lambda i: (0, i))],
        out_shape=[jax.ShapeDtypeStruct((d, W_COLS), BF16),
                   jax.ShapeDtypeStruct((wt_rows, d), BF16)],
        compiler_params=_params("parallel"),
        name="arrange_w_in",
    )(w_in)
    bpad = jnp.zeros((LANES - 2 * M_HEADS,), F32)
    bi = jnp.concatenate([b_mgate[0:4], b_mgate[8:12], bpad])
    bf = jnp.concatenate([b_mgate[4:8], b_mgate[12:16], bpad])
    bgt = jnp.concatenate([bi, bf])[:, None]
    return w_all, wt_all, bgt


def _segment_mats():
    na_w = NA_HEADS * NA_DH
    seg = np.zeros((na_w, LANES), np.float32)
    seg[np.arange(na_w), np.arange(na_w) // NA_DH] = 1.0
    return jnp.asarray(seg, BF16), jnp.asarray(np.concatenate([seg.T, seg.T], axis=0), BF16)


def kernel(x, c, ctx, c_ctx, w_ada, b_ada, w_in, b_mgate, m_norm_w, na_qn_w, na_kn_w, na_rpb,
           w_br_m, w_br_na, w_out, w_router, router_bias, w_exp_gate, w_exp_up, w_exp_down,
           w_sh_gate, w_sh_up, w_sh_down):
    b, t, d = x.shape
    n = b * t
    rows = t // GRID_W
    l = 0

    cc = jnp.concatenate([c, c_ctx[None, :], jnp.zeros((8 - b - 1, d), F32)], axis=0)
    mod = _ada(cc, w_ada[l], b_ada[l])
    mod = mod.reshape(8, 6, d)
    mod = jnp.concatenate([mod, jnp.zeros((8, 2, d), F32)], axis=1)
    mod_x = mod[:b]
    mod_c = jnp.broadcast_to(mod[b:b + 1], (b, 8, d))

    w_all, wt_all, bgt = _arrange_w_in(w_in[l], b_mgate[l])
    seg, segt = _segment_mats()
    qnw = jnp.tile(na_qn_w[l], NA_HEADS)[None, :]
    knw = jnp.tile(na_kn_w[l], NA_HEADS)[None, :]
    tm = min(512, t)

    cp = _inproj(ctx, mod_c, w_all, wt_all, bgt, qnw, knw, seg, segt, None, min(tm, ctx.shape[1]))
    xp = _inproj(x, mod_x, w_all, wt_all, bgt, qnw, knw, seg, segt, _rope_tables(t, tm), tm)
    cmq, cmv, _, _, cnk, cnv, _, _, cmkt, cgit, cgft = cp
    mq, mv, mo, nq, nk, nv, gm, gn, mkt, git, gft = xp

    c0 = jnp.zeros((b, 8, M_DQK, MLSTM_EXT), F32)
    m0 = jnp.zeros((b, 8, LANES), F32)
    _, _, c1, m1 = _mlstm(cmq, cmkt, cmv, cgit, cgft, c0, m0)
    hf, hb, _, _ = _mlstm(mq, mkt, mv, git, gft, c1, m1)

    yna = _na(nq, nk, nv, cnk, cnv, _na_bias_table(na_rpb[l], rows))

    bias_col = jnp.broadcast_to(router_bias[l][:, None], (N_EXPERTS, LANES))
    h2p, base, top_e, top_w, rank, cnt = _post(
        x, mod_x, hf, hb, mo, yna, gm, gn, m_norm_w[l][None, :],
        w_br_m[l].astype(BF16), w_br_na[l].astype(BF16), w_out[l].astype(BF16),
        w_router[l].T.astype(BF16), w_sh_gate[l].astype(BF16), w_sh_up[l].astype(BF16),
        w_sh_down[l].astype(BF16), bias_col, tm)

    counts = cnt[:, 0].astype(jnp.int32)
    half = MOE_BLOCK // 2
    p_rows = (-(-(n * TOP_K) // half) + N_EXPERTS) * half
    nb_max = -(-(n * TOP_K) // MOE_BLOCK) + N_EXPERTS
    pstart, blk0, nblk, row0, is_half, n_used = _expert_plan(counts, nb_max)

    xs, pos = _dispatch_rows(h2p, top_e, rank, pstart, p_rows)
    ys = _experts(blk0, nblk, counts, row0, is_half, n_used, xs,
                  w_exp_gate[l], w_exp_up[l], w_exp_down[l])
    out = base
    for bi in range(b):
        idx = pos[:, bi * t:(bi + 1) * t].reshape(-1)
        yg = _gather_rows(ys, idx).reshape(TOP_K, t, d // 2)
        out = _combine(out, bi, mod_x, top_w, yg, tm)
    return out
```
